```python
import math
import jax, jax.numpy as jnp
from jax import lax
import numpy as np

D_MODEL = 2048
BATCH = 8
SEQ = 4096
DEPTH = 1

D_MIX = D_MODEL
POOL_WIDTH = D_MIX // 2
SSM_WIDTH = D_MIX - POOL_WIDTH
POOL_WINDOWS = (2, 4, 8, 16)
N_POOL_GROUPS = len(POOL_WINDOWS)
POOL_GROUP = POOL_WIDTH // N_POOL_GROUPS
SSM_GROUP = 16
N_SSM_GROUPS = SSM_WIDTH // SSM_GROUP
SSM_STATE = 64
PLE_DIM = 256
EPS = 1e-6
DT_MIN = 1e-3
DT_MAX = 1e-1
A_RE_MAX = -1e-4

kernel_name = "hybrid_pool_s5_parallel_heads"


def rmsnorm(x, gain):
    x32 = x.astype(jnp.float32)
    y = x32 * lax.rsqrt(jnp.mean(x32 * x32, axis=-1, keepdims=True) + EPS)
    return (y * gain.astype(jnp.float32)).astype(x.dtype)


def pool_mixer(u, w_pool, pool_scale):
    B, L, _ = u.shape
    ug = u.astype(jnp.float32).reshape(B, L, N_POOL_GROUPS, POOL_GROUP)
    t = jnp.arange(L)
    outs = []
    for g, w in enumerate(POOL_WINDOWS):
        v = ug[:, :, g]
        cs = jnp.cumsum(v, axis=1)
        lagged = jnp.pad(cs, ((0, 0), (w, 0), (0, 0)))[:, :L]
        count = jnp.minimum(t + 1, w).astype(jnp.float32)[None, :, None]
        outs.append((cs - lagged) / count - v)
    pooled = jnp.stack(outs, axis=2)
    mixed = jnp.einsum('blgc,gcd->blgd', pooled, w_pool.astype(jnp.float32))
    out = mixed.reshape(B, L, POOL_WIDTH) * pool_scale.astype(jnp.float32)
    return out.astype(u.dtype)


def _scan_combine(e1, e2):
    ar1, ai1, br1, bi1 = e1
    ar2, ai2, br2, bi2 = e2
    ar = ar2 * ar1 - ai2 * ai1
    ai = ar2 * ai1 + ai2 * ar1
    br = ar2 * br1 - ai2 * bi1 + br2
    bi = ar2 * bi1 + ai2 * br1 + bi2
    return (ar, ai, br, bi)


def ssm_mixer(u, a_re, a_im, log_dt, b_re, b_im, c_re, c_im, d_skip, w_glu):
    B, L, _ = u.shape
    f32 = jnp.float32
    u32 = u.astype(f32).reshape(B, L, N_SSM_GROUPS, SSM_GROUP)
    lam_re = jnp.minimum(a_re.astype(f32), A_RE_MAX)
    lam_im = a_im.astype(f32)
    dt = jnp.exp(log_dt.astype(f32))[:, None]
    mag = jnp.exp(lam_re * dt)
    ang = lam_im * dt
    ab_re = mag * jnp.cos(ang)
    ab_im = mag * jnp.sin(ang)
    den = lam_re * lam_re + lam_im * lam_im
    n_re = ab_re - 1.0
    n_im = ab_im
    q_re = (n_re * lam_re + n_im * lam_im) / den
    q_im = (n_im * lam_re - n_re * lam_im) / den
    br = b_re.astype(f32)
    bi = b_im.astype(f32)
    bb_re = q_re[..., None] * br - q_im[..., None] * bi
    bb_im = q_re[..., None] * bi + q_im[..., None] * br
    bu_re = jnp.einsum('blgc,gnc->blgn', u32, bb_re)
    bu_im = jnp.einsum('blgc,gnc->blgn', u32, bb_im)
    shp = (1, L, N_SSM_GROUPS, SSM_STATE)
    a_re_t = jnp.broadcast_to(ab_re[None, None], shp)
    a_im_t = jnp.broadcast_to(ab_im[None, None], shp)
    _, _, s_re, s_im = lax.associative_scan(
        _scan_combine, (a_re_t, a_im_t, bu_re, bu_im), axis=1)
    y = (jnp.einsum('blgn,gcn->blgc', s_re, c_re.astype(f32))
         - jnp.einsum('blgn,gcn->blgc', s_im, c_im.astype(f32))
         + d_skip.astype(f32).reshape(N_SSM_GROUPS, SSM_GROUP) * u32)
    y = y.reshape(B, L, SSM_WIDTH)
    g = jax.nn.gelu(y)
    hg = g @ w_glu.astype(f32)
    out = hg[..., :SSM_WIDTH] * jax.nn.sigmoid(hg[..., SSM_WIDTH:])
    return out.astype(u.dtype)


def _fwd_setup_inputs(seed: int = 0) -> dict:
    key = jax.random.key(seed)
    ks = jax.random.split(key, 20)
    f32 = jnp.float32
    n = lambda k, shape, s: jax.random.normal(k, shape, f32) * s
    x = jax.random.normal(ks[0], (BATCH, SEQ, D_MODEL), f32)
    p = jax.random.normal(ks[1], (DEPTH, BATCH, SEQ, PLE_DIM), f32)
    norm_gain = 1.0 + n(ks[2], (DEPTH, D_MODEL), 0.02)
    w_in = n(ks[3], (DEPTH, D_MODEL, 2 * D_MIX), D_MODEL ** -0.5)
    w_pool = n(ks[4], (DEPTH, N_POOL_GROUPS, POOL_GROUP, POOL_GROUP), POOL_GROUP ** -0.5)
    pool_scale = 1.0 + n(ks[5], (DEPTH, POOL_WIDTH), 0.02)
    a_re = -0.5 + n(ks[6], (DEPTH, N_SSM_GROUPS, SSM_STATE), 0.01)
    a_im = (math.pi * jnp.arange(SSM_STATE, dtype=f32))[None, None, :] + n(
        ks[7], (DEPTH, N_SSM_GROUPS, SSM_STATE), 0.01)
    log_dt = jax.random.uniform(ks[8], (DEPTH, N_SSM_GROUPS), f32,
                                math.log(DT_MIN), math.log(DT_MAX))
    b_scale = (2.0 * SSM_GROUP) ** -0.5
    b_re = n(ks[9], (DEPTH, N_SSM_GROUPS, SSM_STATE, SSM_GROUP), b_scale)
    b_im = n(ks[10], (DEPTH, N_SSM_GROUPS, SSM_STATE, SSM_GROUP), b_scale)
    c_scale = SSM_STATE ** -0.5
    c_re = n(ks[11], (DEPTH, N_SSM_GROUPS, SSM_GROUP, SSM_STATE), c_scale)
    c_im = n(ks[12], (DEPTH, N_SSM_GROUPS, SSM_GROUP, SSM_STATE), c_scale)
    d_skip = n(ks[13], (DEPTH, SSM_WIDTH), 1.0)
    w_glu = n(ks[14], (DEPTH, SSM_WIDTH, 2 * SSM_WIDTH), SSM_WIDTH ** -0.5)
    w_out = n(ks[15], (DEPTH, D_MIX, D_MODEL), D_MIX ** -0.5)
    w_ple = n(ks[16], (DEPTH, PLE_DIM, D_MODEL), PLE_DIM ** -0.5)
    w_ple_gate = n(ks[17], (DEPTH, D_MODEL, D_MODEL), D_MODEL ** -0.5)
    final_gain = 1.0 + n(ks[18], (D_MODEL,), 0.02)
    return {"x": x, "p": p, "norm_gain": norm_gain, "w_in": w_in, "w_pool": w_pool,
            "pool_scale": pool_scale, "a_re": a_re, "a_im": a_im, "log_dt": log_dt,
            "b_re": b_re, "b_im": b_im, "c_re": c_re, "c_im": c_im, "d_skip": d_skip,
            "w_glu": w_glu, "w_out": w_out, "w_ple": w_ple, "w_ple_gate": w_ple_gate,
            "final_gain": final_gain}


def _fwd_reference(x, p, norm_gain, w_in, w_pool, pool_scale, a_re, a_im, log_dt, b_re, b_im,
              c_re, c_im, d_skip, w_glu, w_out, w_ple, w_ple_gate, final_gain):
    h = x
    for i in range(DEPTH):
        hn = rmsnorm(h, norm_gain[i])
        proj = hn @ w_in[i]
        pool_in = proj[..., :POOL_WIDTH]
        pool_gate = proj[..., POOL_WIDTH:2 * POOL_WIDTH]
        ssm_in = proj[..., 2 * POOL_WIDTH:2 * POOL_WIDTH + SSM_WIDTH]
        ssm_gate = proj[..., 2 * POOL_WIDTH + SSM_WIDTH:]
        ya = pool_mixer(pool_in, w_pool[i], pool_scale[i]) * jax.nn.silu(pool_gate)
        yb = ssm_mixer(ssm_in, a_re[i], a_im[i], log_dt[i], b_re[i], b_im[i],
                       c_re[i], c_im[i], d_skip[i], w_glu[i]) * jax.nn.silu(ssm_gate)
        h = h + jnp.concatenate([ya, yb], axis=-1) @ w_out[i]
        h = h + (p[i] @ w_ple[i]) * jax.nn.sigmoid(h @ w_ple_gate[i])
    return rmsnorm(h, final_gain)


import jax as _jax
import jax.numpy as _jnp

TWIN_FORMAT = 'train_step'
FWD_PARAMS = ['x', 'p', 'norm_gain', 'w_in', 'w_pool', 'pool_scale', 'a_re', 'a_im', 'log_dt', 'b_re', 'b_im', 'c_re', 'c_im', 'd_skip', 'w_glu', 'w_out', 'w_ple', 'w_ple_gate', 'final_gain']
TWIN_WEIGHTS = ['norm_gain', 'w_in', 'w_pool', 'pool_scale', 'a_re', 'a_im', 'log_dt', 'b_re', 'b_im', 'c_re', 'c_im', 'd_skip', 'w_glu', 'w_out', 'w_ple', 'w_ple_gate', 'final_gain']
TWIN_DIFF_INPUT = 'x'
TWIN_INPUTS = ['x', 'p', 'norm_gain', 'w_in', 'w_pool', 'pool_scale', 'a_re', 'a_im', 'log_dt', 'b_re', 'b_im', 'c_re', 'c_im', 'd_skip', 'w_glu', 'w_out', 'w_ple', 'w_ple_gate', 'final_gain', 'loss_target', 'm_norm_gain', 'm_w_in', 'm_w_pool', 'm_pool_scale', 'm_a_re', 'm_a_im', 'm_log_dt', 'm_b_re', 'm_b_im', 'm_c_re', 'm_c_im', 'm_d_skip', 'm_w_glu', 'm_w_out', 'm_w_ple', 'm_w_ple_gate', 'm_final_gain', 'v_norm_gain', 'v_w_in', 'v_w_pool', 'v_pool_scale', 'v_a_re', 'v_a_im', 'v_log_dt', 'v_b_re', 'v_b_im', 'v_c_re', 'v_c_im', 'v_d_skip', 'v_w_glu', 'v_w_out', 'v_w_ple', 'v_w_ple_gate', 'v_final_gain']
TWIN_OUTPUTS = ['loss', 'grad_x', 'grad_norm_gain', 'grad_w_in', 'grad_w_pool', 'grad_pool_scale', 'grad_a_re', 'grad_a_im', 'grad_log_dt', 'grad_b_re', 'grad_b_im', 'grad_c_re', 'grad_c_im', 'grad_d_skip', 'grad_w_glu', 'grad_w_out', 'grad_w_ple', 'grad_w_ple_gate', 'grad_final_gain', 'delta_norm_gain', 'delta_w_in', 'delta_w_pool', 'delta_pool_scale', 'delta_a_re', 'delta_a_im', 'delta_log_dt', 'delta_b_re', 'delta_b_im', 'delta_c_re', 'delta_c_im', 'delta_d_skip', 'delta_w_glu', 'delta_w_out', 'delta_w_ple', 'delta_w_ple_gate', 'delta_final_gain', 'new_m_norm_gain', 'new_m_w_in', 'new_m_w_pool', 'new_m_pool_scale', 'new_m_a_re', 'new_m_a_im', 'new_m_log_dt', 'new_m_b_re', 'new_m_b_im', 'new_m_c_re', 'new_m_c_im', 'new_m_d_skip', 'new_m_w_glu', 'new_m_w_out', 'new_m_w_ple', 'new_m_w_ple_gate', 'new_m_final_gain', 'new_v_norm_gain', 'new_v_w_in', 'new_v_w_pool', 'new_v_pool_scale', 'new_v_a_re', 'new_v_a_im', 'new_v_log_dt', 'new_v_b_re', 'new_v_b_im', 'new_v_c_re', 'new_v_c_im', 'new_v_d_skip', 'new_v_w_glu', 'new_v_w_out', 'new_v_w_ple', 'new_v_w_ple_gate', 'new_v_final_gain']
TWIN_LEAF_KINDS = {'loss': 'loss', 'grad_x': 'grad_x', 'grad_norm_gain': 'grad_w', 'grad_w_in': 'grad_w', 'grad_w_pool': 'grad_w', 'grad_pool_scale': 'grad_w', 'grad_a_re': 'grad_w', 'grad_a_im': 'grad_w', 'grad_log_dt': 'grad_w', 'grad_b_re': 'grad_w', 'grad_b_im': 'grad_w', 'grad_c_re': 'grad_w', 'grad_c_im': 'grad_w', 'grad_d_skip': 'grad_w', 'grad_w_glu': 'grad_w', 'grad_w_out': 'grad_w', 'grad_w_ple': 'grad_w', 'grad_w_ple_gate': 'grad_w', 'grad_final_gain': 'grad_w', 'delta_norm_gain': 'delta_w', 'delta_w_in': 'delta_w', 'delta_w_pool': 'delta_w', 'delta_pool_scale': 'delta_w', 'delta_a_re': 'delta_w', 'delta_a_im': 'delta_w', 'delta_log_dt': 'delta_w', 'delta_b_re': 'delta_w', 'delta_b_im': 'delta_w', 'delta_c_re': 'delta_w', 'delta_c_im': 'delta_w', 'delta_d_skip': 'delta_w', 'delta_w_glu': 'delta_w', 'delta_w_out': 'delta_w', 'delta_w_ple': 'delta_w', 'delta_w_ple_gate': 'delta_w', 'delta_final_gain': 'delta_w', 'new_m_norm_gain': 'new_m', 'new_m_w_in': 'new_m', 'new_m_w_pool': 'new_m', 'new_m_pool_scale': 'new_m', 'new_m_a_re': 'new_m', 'new_m_a_im': 'new_m', 'new_m_log_dt': 'new_m', 'new_m_b_re': 'new_m', 'new_m_b_im': 'new_m', 'new_m_c_re': 'new_m', 'new_m_c_im': 'new_m', 'new_m_d_skip': 'new_m', 'new_m_w_glu': 'new_m', 'new_m_w_out': 'new_m', 'new_m_w_ple': 'new_m', 'new_m_w_ple_gate': 'new_m', 'new_m_final_gain': 'new_m', 'new_v_norm_gain': 'new_v', 'new_v_w_in': 'new_v', 'new_v_w_pool': 'new_v', 'new_v_pool_scale': 'new_v', 'new_v_a_re': 'new_v', 'new_v_a_im': 'new_v', 'new_v_log_dt': 'new_v', 'new_v_b_re': 'new_v', 'new_v_b_im': 'new_v', 'new_v_c_re': 'new_v', 'new_v_c_im': 'new_v', 'new_v_d_skip': 'new_v', 'new_v_w_glu': 'new_v', 'new_v_w_out': 'new_v', 'new_v_w_ple': 'new_v', 'new_v_w_ple_gate': 'new_v', 'new_v_final_gain': 'new_v'}


def _forward(args):
    return _fwd_reference(*[args[k] for k in FWD_PARAMS])


def _output_shape():
    def fwd():
        inp = _fwd_setup_inputs(0)
        return _fwd_reference(*[inp[k] for k in FWD_PARAMS])
    out = _jax.eval_shape(fwd)
    return out.shape, out.dtype

N_MICROBATCH = 1
ADAM_LR = 0.001
ADAM_B1 = 0.9
ADAM_B2 = 0.999
ADAM_EPS = 1e-08
ADAM_WD = 0.01
ADAM_STEP = 10
PER_EXAMPLE_BATCH_AXIS = {'x': 0, 'p': 1, 'loss_target': 0}
SHARED_INPUTS = []
_WEIGHT_DTYPES = {'norm_gain': _jnp.float32, 'w_in': _jnp.float32, 'w_pool': _jnp.float32, 'pool_scale': _jnp.float32, 'a_re': _jnp.float32, 'a_im': _jnp.float32, 'log_dt': _jnp.float32, 'b_re': _jnp.float32, 'b_im': _jnp.float32, 'c_re': _jnp.float32, 'c_im': _jnp.float32, 'd_skip': _jnp.float32, 'w_glu': _jnp.float32, 'w_out': _jnp.float32, 'w_ple': _jnp.float32, 'w_ple_gate': _jnp.float32, 'final_gain': _jnp.float32}
MOMENT_SCALE = {'norm_gain': 4.261546e-02, 'w_in': 3.059915e-02, 'w_pool': 3.933612e-02, 'pool_scale': 3.943506e-02, 'a_re': 1.155384e-03, 'a_im': 1.107312e-03, 'log_dt': 7.160069e-01, 'b_re': 7.488369e-04, 'b_im': 7.554087e-04, 'c_re': 1.069225e-03, 'c_im': 1.076094e-03, 'd_skip': 1.720805e-02, 'w_glu': 1.188120e-02, 'w_out': 3.001920e-02, 'w_ple': 4.042702e-02, 'w_ple_gate': 1.643543e-02, 'final_gain': 1.600459e+01}


def _to_microbatches(a, axis):
    t = _jnp.moveaxis(a, axis, 0)
    t = t.reshape((N_MICROBATCH, t.shape[0] // N_MICROBATCH) + t.shape[1:])
    return _jnp.moveaxis(t, 1, axis + 1)


def setup_inputs(seed: int = 0) -> dict:
    inp = _fwd_setup_inputs(seed)
    key = _jax.random.fold_in(_jax.random.key(seed), 7919)
    shape, _ = _output_shape()
    out = dict(inp)
    out["loss_target"] = _jax.random.normal(_jax.random.fold_in(key, 0), shape, _jnp.float32)
    for i, name in enumerate(TWIN_WEIGHTS):
        w = inp[name].astype(_jnp.float32)
        if MOMENT_SCALE is None:
            s = _jnp.sqrt(_jnp.mean(_jnp.square(w)) + 1e-30)
        else:
            s = MOMENT_SCALE[name]
        km, kv = _jax.random.split(_jax.random.fold_in(key, i + 1))
        out[name] = w
        out["m_" + name] = s * _jax.random.normal(km, w.shape, _jnp.float32)
        out["v_" + name] = (s * s) * _jax.random.uniform(kv, w.shape, _jnp.float32, 0.5, 1.5)
    if N_MICROBATCH > 1:
        for name, axis in PER_EXAMPLE_BATCH_AXIS.items():
            out[name] = _to_microbatches(out[name], axis)
    return {'x': out['x'], 'p': out['p'], 'norm_gain': out['norm_gain'], 'w_in': out['w_in'], 'w_pool': out['w_pool'], 'pool_scale': out['pool_scale'], 'a_re': out['a_re'], 'a_im': out['a_im'], 'log_dt': out['log_dt'], 'b_re': out['b_re'], 'b_im': out['b_im'], 'c_re': out['c_re'], 'c_im': out['c_im'], 'd_skip': out['d_skip'], 'w_glu': out['w_glu'], 'w_out': out['w_out'], 'w_ple': out['w_ple'], 'w_ple_gate': out['w_ple_gate'], 'final_gain': out['final_gain'], 'loss_target': out['loss_target'], 'm_norm_gain': out['m_norm_gain'], 'm_w_in': out['m_w_in'], 'm_w_pool': out['m_w_pool'], 'm_pool_scale': out['m_pool_scale'], 'm_a_re': out['m_a_re'], 'm_a_im': out['m_a_im'], 'm_log_dt': out['m_log_dt'], 'm_b_re': out['m_b_re'], 'm_b_im': out['m_b_im'], 'm_c_re': out['m_c_re'], 'm_c_im': out['m_c_im'], 'm_d_skip': out['m_d_skip'], 'm_w_glu': out['m_w_glu'], 'm_w_out': out['m_w_out'], 'm_w_ple': out['m_w_ple'], 'm_w_ple_gate': out['m_w_ple_gate'], 'm_final_gain': out['m_final_gain'], 'v_norm_gain': out['v_norm_gain'], 'v_w_in': out['v_w_in'], 'v_w_pool': out['v_w_pool'], 'v_pool_scale': out['v_pool_scale'], 'v_a_re': out['v_a_re'], 'v_a_im': out['v_a_im'], 'v_log_dt': out['v_log_dt'], 'v_b_re': out['v_b_re'], 'v_b_im': out['v_b_im'], 'v_c_re': out['v_c_re'], 'v_c_im': out['v_c_im'], 'v_d_skip': out['v_d_skip'], 'v_w_glu': out['v_w_glu'], 'v_w_out': out['v_w_out'], 'v_w_ple': out['v_w_ple'], 'v_w_ple_gate': out['v_w_ple_gate'], 'v_final_gain': out['v_final_gain']}


def _loss(weights, diff, rest, loss_target):
    with _jax.named_scope("forward"):
        args = {**rest, TWIN_DIFF_INPUT: diff, **{k: w.astype(_WEIGHT_DTYPES[k]) for k, w in weights.items()}}
        y = _forward(args)
    with _jax.named_scope("loss_head"):
        err = _jnp.square(y.astype(_jnp.float32) - loss_target)
        return 0.5 * _jnp.sum(_jnp.mean(err, axis=-1)) if err.ndim else 0.5 * err


def _adamw(w, g, m, v):
    m = ADAM_B1 * m + (1.0 - ADAM_B1) * g
    v = ADAM_B2 * v + (1.0 - ADAM_B2) * _jnp.square(g)
    m_hat = m / (1.0 - ADAM_B1 ** ADAM_STEP)
    v_hat = v / (1.0 - ADAM_B2 ** ADAM_STEP)
    delta = -ADAM_LR * (m_hat / (_jnp.sqrt(v_hat) + ADAM_EPS) + ADAM_WD * w)
    return delta, m, v


def reference(x, p, norm_gain, w_in, w_pool, pool_scale, a_re, a_im, log_dt, b_re, b_im, c_re, c_im, d_skip, w_glu, w_out, w_ple, w_ple_gate, final_gain, loss_target, m_norm_gain, m_w_in, m_w_pool, m_pool_scale, m_a_re, m_a_im, m_log_dt, m_b_re, m_b_im, m_c_re, m_c_im, m_d_skip, m_w_glu, m_w_out, m_w_ple, m_w_ple_gate, m_final_gain, v_norm_gain, v_w_in, v_w_pool, v_pool_scale, v_a_re, v_a_im, v_log_dt, v_b_re, v_b_im, v_c_re, v_c_im, v_d_skip, v_w_glu, v_w_out, v_w_ple, v_w_ple_gate, v_final_gain):
    given = dict(x=x, p=p, norm_gain=norm_gain, w_in=w_in, w_pool=w_pool, pool_scale=pool_scale, a_re=a_re, a_im=a_im, log_dt=log_dt, b_re=b_re, b_im=b_im, c_re=c_re, c_im=c_im, d_skip=d_skip, w_glu=w_glu, w_out=w_out, w_ple=w_ple, w_ple_gate=w_ple_gate, final_gain=final_gain, loss_target=loss_target, m_norm_gain=m_norm_gain, m_w_in=m_w_in, m_w_pool=m_w_pool, m_pool_scale=m_pool_scale, m_a_re=m_a_re, m_a_im=m_a_im, m_log_dt=m_log_dt, m_b_re=m_b_re, m_b_im=m_b_im, m_c_re=m_c_re, m_c_im=m_c_im, m_d_skip=m_d_skip, m_w_glu=m_w_glu, m_w_out=m_w_out, m_w_ple=m_w_ple, m_w_ple_gate=m_w_ple_gate, m_final_gain=m_final_gain, v_norm_gain=v_norm_gain, v_w_in=v_w_in, v_w_pool=v_w_pool, v_pool_scale=v_pool_scale, v_a_re=v_a_re, v_a_im=v_a_im, v_log_dt=v_log_dt, v_b_re=v_b_re, v_b_im=v_b_im, v_c_re=v_c_re, v_c_im=v_c_im, v_d_skip=v_d_skip, v_w_glu=v_w_glu, v_w_out=v_w_out, v_w_ple=v_w_ple, v_w_ple_gate=v_w_ple_gate, v_final_gain=v_final_gain)
    weights = {n: given[n] for n in TWIN_WEIGHTS}
    shared = {n: given[n] for n in SHARED_INPUTS}
    per_example = {n: given[n] for n in ['x', 'p']}
    grad_fn = _jax.value_and_grad(_loss, argnums=(0, 1))

    def one_microbatch(ex, loss_target):
        ex = dict(ex)
        diff = ex.pop(TWIN_DIFF_INPUT)
        return grad_fn(weights, diff, {**shared, **ex}, loss_target)

    if N_MICROBATCH == 1:
        loss, (grad_w, grad_x) = one_microbatch(per_example, given["loss_target"])
    else:
        def body(carry, xs):
            loss_sum, grad_sum = carry
            l_k, (gw_k, gx_k) = one_microbatch(xs[0], xs[1])
            with _jax.named_scope("update"):
                return (loss_sum + l_k, _jax.tree.map(_jnp.add, grad_sum, gw_k)), gx_k

        init = (_jnp.zeros((), _jnp.float32), _jax.tree.map(_jnp.zeros_like, weights))
        (loss, grad_w), grad_x = _jax.lax.scan(body, init, (per_example, given["loss_target"]))
    with _jax.named_scope("update"):
        delta_w, new_m, new_v = {}, {}, {}
        for n in TWIN_WEIGHTS:
            delta_w[n], new_m[n], new_v[n] = _adamw(weights[n], grad_w[n], given["m_" + n], given["v_" + n])
    return (loss, grad_x, *[grad_w[n] for n in TWIN_WEIGHTS], *[delta_w[n] for n in TWIN_WEIGHTS],
            *[new_m[n] for n in TWIN_WEIGHTS], *[new_v[n] for n in TWIN_WEIGHTS])
```

```python
import functools
import math

import jax
import jax.numpy as jnp
from jax import lax
from jax.experimental import pallas as pl
from jax.experimental.pallas import tpu as pltpu

F32 = jnp.float32
BF16 = jnp.bfloat16
MESH = pl.DeviceIdType.MESH
MESH_AXES = ("x", "y", "c")
N_DEV = 8

D_MODEL = 2048
POOL_WIDTH = 1024
SSM_WIDTH = 1024
N_POOL_GROUPS = 4
POOL_GROUP = 256
SSM_GROUP = 16
N_SSM_GROUPS = 64
SSM_STATE = 64
SSM_FLAT = N_SSM_GROUPS * SSM_STATE
SSM_CHUNKS = 4
CHUNK_IN = SSM_WIDTH // SSM_CHUNKS
CHUNK_STATE = SSM_FLAT // SSM_CHUNKS
PLE_DIM = 256
EPS = 1e-6
A_RE_MAX = -1e-4
ADAM_LR = 0.001
ADAM_B1 = 0.9
ADAM_B2 = 0.999
ADAM_EPS = 1e-08
ADAM_WD = 0.01
ADAM_STEP = 10
GELU_C = math.sqrt(2.0 / math.pi)
GELU_A = 0.044715

SUBLANES = 8
LANES = 128
VMEM_LIMIT_BYTES = 48 * 1024 * 1024

DOT_NN = (((1,), (0,)), ((), ()))
DOT_NT = (((1,), (1,)), ((), ()))
DOT_TN = (((0,), (0,)), ((), ()))


def _tile(n, pref):
    return pref if n % pref == 0 else n


def _params(sem):
    return pltpu.CompilerParams(dimension_semantics=sem, vmem_limit_bytes=VMEM_LIMIT_BYTES)


def _sigmoid(v):
    return 1.0 / (1.0 + jnp.exp(-v))


def _silu_and_grad(v):
    s = _sigmoid(v)
    return v * s, s * (1.0 + v * (1.0 - s))


def _mm(name, pairs, dims, grid, outs, k_steps, extras=(), epilogue=None):
    n_pairs, n_ex, n_out = len(pairs), len(extras), len(outs)
    acc_shape = tuple(d for d in outs[0][2] if d is not None)
    if epilogue is None:
        def epilogue(acc, ex, out_refs):
            out_refs[0][...] = acc.astype(out_refs[0].dtype)

    def body(*refs):
        ab = refs[:2 * n_pairs]
        ex = refs[2 * n_pairs:2 * n_pairs + n_ex]
        out_refs = refs[2 * n_pairs + n_ex:2 * n_pairs + n_ex + n_out]
        acc = refs[-1]
        k = pl.program_id(2)

        @pl.when(k == 0)
        def _():
            acc[...] = jnp.zeros_like(acc)

        part = None
        for q in range(n_pairs):
            d = lax.dot_general(ab[2 * q][...].astype(BF16), ab[2 * q + 1][...].astype(BF16), dims,
                                preferred_element_type=F32)
            part = d if part is None else part + d
        acc[...] += part

        @pl.when(k == k_steps - 1)
        def _():
            epilogue(acc[...], ex, out_refs)

    in_specs, operands = [], []
    for a, a_blk, a_map, b, b_blk, b_map in pairs:
        in_specs += [pl.BlockSpec(a_blk, a_map), pl.BlockSpec(b_blk, b_map)]
        operands += [a, b]
    for e, e_blk, e_map in extras:
        in_specs.append(pl.BlockSpec(e_blk, e_map))
        operands.append(e)
    return pl.pallas_call(
        body, name=name, grid=grid, in_specs=in_specs,
        out_specs=[pl.BlockSpec(o[2], o[3]) for o in outs],
        out_shape=[jax.ShapeDtypeStruct(o[0], o[1]) for o in outs],
        scratch_shapes=[pltpu.VMEM(acc_shape, F32)],
        compiler_params=_params(("arbitrary", "arbitrary", "arbitrary")),
    )(*operands)


def _mm_nn(name, a, b, out_dtypes, tm=1024, tn=1024, tk=1024, a_col0=0, extras=(), epilogue=None):
    m, n = a.shape[0], b.shape[1]
    k = b.shape[0]
    tm, tn, tk = _tile(m, tm), _tile(n, tn), _tile(k, tk)
    outs = [((m, n), dt, (tm, tn), lambda i, j, s: (i, j)) for dt in out_dtypes]
    ex = [(e, (tm, tn), lambda i, j, s: (i, j)) for e in extras]
    return _mm(name, [(a, (tm, tk), lambda i, j, s: (i, a_col0 + s), b, (tk, tn), lambda i, j, s: (s, j))],
               DOT_NN, (m // tm, n // tn, k // tk), outs, k // tk, ex, epilogue)


def _mm_nt(name, a, b, out_dtypes, tm=1024, tn=1024, tk=1024, extras=(), epilogue=None):
    m, kk = a.shape
    n = b.shape[0]
    tm, tn, tk = _tile(m, tm), _tile(n, tn), _tile(kk, tk)
    outs = [((m, n), dt, (tm, tn), lambda i, j, s: (i, j)) for dt in out_dtypes]
    ex = [(e, (tm, tn), lambda i, j, s: (i, j)) for e in extras]
    return _mm(name, [(a, (tm, tk), lambda i, j, s: (i, s), b, (tn, tk), lambda i, j, s: (j, s))],
               DOT_NT, (m // tm, n // tn, kk // tk), outs, kk // tk, ex, epilogue)


def _mm_tn(name, a, b, out_dtype, tm=512, tn=2048, tk=1024):
    m, kk = a.shape
    n = b.shape[1]
    tm, tn, tk = _tile(kk, tm), _tile(n, tn), _tile(m, tk)
    outs = [((kk, n), out_dtype, (tm, tn), lambda i, j, s: (i, j))]
    return _mm(name, [(a, (tk, tm), lambda i, j, s: (s, i), b, (tk, tn), lambda i, j, s: (s, j))],
               DOT_TN, (kk // tm, n // tn, m // tk), outs, m // tk)[0]


def _norm1_fwd(x, gain):
    t = x.shape[0]
    tm = _tile(t, 512)

    def body(x_ref, g_ref, hn_ref):
        xv = x_ref[...]
        r = lax.rsqrt(jnp.mean(xv * xv, axis=-1, keepdims=True) + EPS)
        hn_ref[...] = (xv * r * g_ref[...]).astype(BF16)

    return pl.pallas_call(
        body, name="norm1_fwd", grid=(t // tm,),
        in_specs=[pl.BlockSpec((tm, D_MODEL), lambda i: (i, 0)), pl.BlockSpec((1, D_MODEL), lambda i: (0, 0))],
        out_specs=pl.BlockSpec((tm, D_MODEL), lambda i: (i, 0)),
        out_shape=jax.ShapeDtypeStruct((t, D_MODEL), BF16),
        compiler_params=_params(("arbitrary",)),
    )(x, gain)


def _norm1_bwd(x, dhn, dh1, gain):
    t = x.shape[0]
    tm = _tile(t, 512)

    def body(x_ref, dhn_ref, dh1_ref, g_ref, dx_ref, gg_ref):
        @pl.when(pl.program_id(0) == 0)
        def _():
            gg_ref[...] = jnp.zeros_like(gg_ref)

        xv = x_ref[...]
        r = lax.rsqrt(jnp.mean(xv * xv, axis=-1, keepdims=True) + EPS)
        xh = xv * r
        dhn_v = dhn_ref[...]
        gg_ref[...] += jnp.sum(dhn_v * xh, axis=0, keepdims=True)
        dxh = dhn_v * g_ref[...]
        dx_ref[...] = dh1_ref[...] + r * (dxh - xh * jnp.mean(dxh * xh, axis=-1, keepdims=True))

    row = pl.BlockSpec((tm, D_MODEL), lambda i: (i, 0))
    vec = pl.BlockSpec((1, D_MODEL), lambda i: (0, 0))
    return pl.pallas_call(
        body, name="norm1_bwd", grid=(t // tm,),
        in_specs=[row, row, row, vec], out_specs=[row, vec],
        out_shape=[jax.ShapeDtypeStruct((t, D_MODEL), F32), jax.ShapeDtypeStruct((1, D_MODEL), F32)],
        compiler_params=_params(("arbitrary",)),
    )(x, dhn, dh1, gain)


def _pool_counts(t, width, group):
    row = lax.broadcasted_iota(jnp.int32, (t, width), 0)
    window = jnp.left_shift(jnp.int32(2), group)
    return row, jnp.minimum(row + 1, window).astype(F32)


def _select_window(group, s2, s4, s8, s16):
    return jnp.where(group == 0, s2, jnp.where(group == 1, s4, jnp.where(group == 2, s8, s16)))


def _pool_fwd(proj):
    t = proj.shape[0]
    tc = LANES

    def body(u_ref, o_ref):
        group = pl.program_id(0) // (POOL_GROUP // tc)
        v = u_ref[...]
        row, count = _pool_counts(t, tc, group)

        def down(a, j):
            return jnp.where(row >= j, pltpu.roll(a, j, 0), 0.0)

        s2 = v + down(v, 1)
        s4 = s2 + down(s2, 2)
        s8 = s4 + down(s4, 4)
        s16 = s8 + down(s8, 8)
        o_ref[...] = (_select_window(group, s2, s4, s8, s16) / count - v).astype(BF16)

    return pl.pallas_call(
        body, name="pool_fwd", grid=(POOL_WIDTH // tc,),
        in_specs=[pl.BlockSpec((t, tc), lambda j: (0, j))],
        out_specs=pl.BlockSpec((t, tc), lambda j: (0, j)),
        out_shape=jax.ShapeDtypeStruct((t, POOL_WIDTH), BF16),
        compiler_params=_params(("arbitrary",)),
    )(proj)


def _pool_bwd(dpooled):
    t = dpooled.shape[0]
    tc = LANES

    def body(d_ref, o_ref):
        group = pl.program_id(0) // (POOL_GROUP // tc)
        dp = d_ref[...]
        row, count = _pool_counts(t, tc, group)
        r = dp / count

        def up(a, j):
            return jnp.where(row < t - j, pltpu.roll(a, t - j, 0), 0.0)

        s2 = r + up(r, 1)
        s4 = s2 + up(s2, 2)
        s8 = s4 + up(s4, 4)
        s16 = s8 + up(s8, 8)
        o_ref[...] = (_select_window(group, s2, s4, s8, s16) - dp).astype(BF16)

    return pl.pallas_call(
        body, name="pool_bwd", grid=(POOL_WIDTH // tc,),
        in_specs=[pl.BlockSpec((t, tc), lambda j: (0, j))],
        out_specs=pl.BlockSpec((t, tc), lambda j: (0, j)),
        out_shape=jax.ShapeDtypeStruct((t, POOL_WIDTH), BF16),
        compiler_params=_params(("arbitrary",)),
    )(dpooled)


def _gate_fwd(mixed, proj, hg, pool_scale):
    t = mixed.shape[0]
    tm = _tile(t, 512)

    def body(mx_ref, ga_ref, gb_ref, hg_ref, ps_ref, cat_ref):
        silu_a, _ = _silu_and_grad(ga_ref[...])
        cat_ref[:, :POOL_WIDTH] = (mx_ref[...] * ps_ref[...] * silu_a).astype(BF16)
        silu_b, _ = _silu_and_grad(gb_ref[...])
        sb = hg_ref[:, :SSM_WIDTH] * _sigmoid(hg_ref[:, SSM_WIDTH:])
        cat_ref[:, POOL_WIDTH:] = (sb * silu_b).astype(BF16)

    return pl.pallas_call(
        body, name="gate_fwd", grid=(t // tm,),
        in_specs=[pl.BlockSpec((tm, POOL_WIDTH), lambda i: (i, 0)),
                  pl.BlockSpec((tm, POOL_WIDTH), lambda i: (i, 1)),
                  pl.BlockSpec((tm, SSM_WIDTH), lambda i: (i, 3)),
                  pl.BlockSpec((tm, 2 * SSM_WIDTH), lambda i: (i, 0)),
                  pl.BlockSpec((1, POOL_WIDTH), lambda i: (0, 0))],
        out_specs=pl.BlockSpec((tm, D_MODEL), lambda i: (i, 0)),
        out_shape=jax.ShapeDtypeStruct((t, D_MODEL), BF16),
        compiler_params=_params(("arbitrary",)),
    )(mixed, proj, proj, hg, pool_scale)


def _gate_bwd(dcat, mixed, proj, hg, pool_scale):
    t = mixed.shape[0]
    tm = _tile(t, 512)

    def body(dc_ref, mx_ref, ga_ref, gb_ref, hg_ref, ps_ref, dmx_ref, dga_ref, dgb_ref, dhg_ref, gps_ref):
        @pl.when(pl.program_id(0) == 0)
        def _():
            gps_ref[...] = jnp.zeros_like(gps_ref)

        ps = ps_ref[...]
        mx = mx_ref[...]
        dya = dc_ref[:, :POOL_WIDTH]
        silu_a, dsilu_a = _silu_and_grad(ga_ref[...])
        dpa = dya * silu_a
        gps_ref[...] += jnp.sum(dpa * mx, axis=0, keepdims=True)
        dmx_ref[...] = (dpa * ps).astype(BF16)
        dga_ref[...] = (dya * mx * ps * dsilu_a).astype(BF16)

        dyb = dc_ref[:, POOL_WIDTH:]
        silu_b, dsilu_b = _silu_and_grad(gb_ref[...])
        h_a = hg_ref[:, :SSM_WIDTH]
        sg = _sigmoid(hg_ref[:, SSM_WIDTH:])
        dsb = dyb * silu_b
        dgb_ref[...] = (dyb * h_a * sg * dsilu_b).astype(BF16)
        dhg_ref[:, :SSM_WIDTH] = (dsb * sg).astype(BF16)
        dhg_ref[:, SSM_WIDTH:] = (dsb * h_a * sg * (1.0 - sg)).astype(BF16)

    half = pl.BlockSpec((tm, POOL_WIDTH), lambda i: (i, 0))
    full = pl.BlockSpec((tm, D_MODEL), lambda i: (i, 0))
    vec = pl.BlockSpec((1, POOL_WIDTH), lambda i: (0, 0))
    return pl.pallas_call(
        body, name="gate_bwd", grid=(t // tm,),
        in_specs=[full, half,
                  pl.BlockSpec((tm, POOL_WIDTH), lambda i: (i, 1)),
                  pl.BlockSpec((tm, SSM_WIDTH), lambda i: (i, 3)),
                  full, vec],
        out_specs=[half, half, half, full, vec],
        out_shape=[jax.ShapeDtypeStruct((t, POOL_WIDTH), BF16), jax.ShapeDtypeStruct((t, POOL_WIDTH), BF16),
                   jax.ShapeDtypeStruct((t, SSM_WIDTH), BF16), jax.ShapeDtypeStruct((t, 2 * SSM_WIDTH), BF16),
                   jax.ShapeDtypeStruct((1, POOL_WIDTH), F32)],
        compiler_params=_params(("arbitrary",)),
    )(dcat, mixed, proj, proj, hg, pool_scale)


def _final(h1, e, q, target, gain):
    t = h1.shape[0]
    tm = _tile(t, 256)

    def body(h1_ref, e_ref, q_ref, tg_ref, g_ref, de_ref, dq_ref, dh2_ref, gg_ref, loss_ref):
        @pl.when(pl.program_id(0) == 0)
        def _():
            gg_ref[...] = jnp.zeros_like(gg_ref)
            loss_ref[...] = jnp.zeros_like(loss_ref)

        ev = e_ref[...]
        sg = _sigmoid(q_ref[...])
        h2 = h1_ref[...] + ev * sg
        r = lax.rsqrt(jnp.mean(h2 * h2, axis=-1, keepdims=True) + EPS)
        n = h2 * r
        gain_v = g_ref[...]
        diff = n * gain_v - tg_ref[...]
        row_loss = jnp.sum(diff * diff, axis=-1, keepdims=True)
        loss_ref[...] += (0.5 / D_MODEL) * jnp.sum(row_loss, axis=0, keepdims=True)
        dout = diff * (1.0 / D_MODEL)
        gg_ref[...] += jnp.sum(dout * n, axis=0, keepdims=True)
        dn = dout * gain_v
        dh2 = r * (dn - n * jnp.mean(dn * n, axis=-1, keepdims=True))
        dh2_ref[...] = dh2
        de_ref[...] = (dh2 * sg).astype(BF16)
        dq_ref[...] = (dh2 * ev * sg * (1.0 - sg)).astype(BF16)

    row = pl.BlockSpec((tm, D_MODEL), lambda i: (i, 0))
    vec = pl.BlockSpec((1, D_MODEL), lambda i: (0, 0))
    return pl.pallas_call(
        body, name="final_norm_loss", grid=(t // tm,),
        in_specs=[row, row, row, row, vec],
        out_specs=[row, row, row, vec, pl.BlockSpec((1, 1), lambda i: (0, 0))],
        out_shape=[jax.ShapeDtypeStruct((t, D_MODEL), BF16), jax.ShapeDtypeStruct((t, D_MODEL), BF16),
                   jax.ShapeDtypeStruct((t, D_MODEL), F32), jax.ShapeDtypeStruct((1, D_MODEL), F32),
                   jax.ShapeDtypeStruct((1, 1), F32)],
        compiler_params=_params(("arbitrary",)),
    )(h1, e, q, target, gain)


def _zoh(a_re, a_im, log_dt, b_re_t, b_im_t):
    lam_re = jnp.minimum(a_re, A_RE_MAX)
    lam_im = a_im
    dt = jnp.exp(log_dt)
    mag = jnp.exp(lam_re * dt)
    ang = lam_im * dt
    ab_re = mag * jnp.cos(ang)
    ab_im = mag * jnp.sin(ang)
    den = lam_re * lam_re + lam_im * lam_im
    n_re = ab_re - 1.0
    n_im = ab_im
    q_re = (n_re * lam_re + n_im * lam_im) / den
    q_im = (n_im * lam_re - n_re * lam_im) / den
    bb_re = q_re[:, None, :] * b_re_t - q_im[:, None, :] * b_im_t
    bb_im = q_re[:, None, :] * b_im_t + q_im[:, None, :] * b_re_t
    return ab_re, ab_im, bb_re, bb_im


def _ssm_params(a_re, a_im, log_dt, b_re_t, b_im_t):
    g, n = a_re.shape

    def body(are_ref, aim_ref, dt_ref, bre_ref, bim_ref, pre_ref, pim_ref, bbre_ref, bbim_ref):
        ab_re, ab_im, bb_re, bb_im = _zoh(are_ref[...], aim_ref[...], dt_ref[...], bre_ref[...], bim_ref[...])
        bbre_ref[...] = bb_re
        bbim_ref[...] = bb_im
        p_re, p_im = ab_re, ab_im
        for r in range(SUBLANES):
            pre_ref[r] = p_re
            pim_ref[r] = p_im
            p_re, p_im = p_re * ab_re - p_im * ab_im, p_re * ab_im + p_im * ab_re

    return pl.pallas_call(
        body, name="ssm_params",
        out_shape=[jax.ShapeDtypeStruct((SUBLANES, g, n), F32), jax.ShapeDtypeStruct((SUBLANES, g, n), F32),
                   jax.ShapeDtypeStruct(b_re_t.shape, F32), jax.ShapeDtypeStruct(b_re_t.shape, F32)],
        compiler_params=_params(None),
    )(a_re, a_im, log_dt, b_re_t, b_im_t)


def _ssm_params_bwd(a_re, a_im, log_dt, b_re_t, b_im_t, g_ab_re, g_ab_im, g_bb_re, g_bb_im):
    def body(are_ref, aim_ref, dt_ref, bre_ref, bim_ref, gar_ref, gai_ref, gbr_ref, gbi_ref,
             o_are, o_aim, o_dt, o_bre, o_bim):
        _, vjp = jax.vjp(_zoh, are_ref[...], aim_ref[...], dt_ref[...], bre_ref[...], bim_ref[...])
        d_are, d_aim, d_dt, d_bre, d_bim = vjp((gar_ref[...], gai_ref[...], gbr_ref[...], gbi_ref[...]))
        o_are[...] = d_are
        o_aim[...] = d_aim
        o_dt[...] = d_dt
        o_bre[...] = d_bre
        o_bim[...] = d_bim

    ins = (a_re, a_im, log_dt, b_re_t, b_im_t)
    return pl.pallas_call(
        body, name="ssm_params_bwd",
        out_shape=[jax.ShapeDtypeStruct(v.shape, F32) for v in ins],
        compiler_params=_params(None),
    )(*ins, g_ab_re, g_ab_im, g_bb_re, g_bb_im)


def _eye16():
    return jnp.eye(SSM_CHUNKS * 4, dtype=F32)


def _blockdiag_in(w):
    w4 = w.reshape(SSM_CHUNKS, 16, SSM_GROUP, SSM_STATE)
    return (w4[:, :, :, None, :] * _eye16()[None, :, None, :, None]).reshape(SSM_CHUNKS, CHUNK_IN, CHUNK_STATE)


def _blockdiag_out(w):
    w4 = jnp.transpose(w.reshape(SSM_CHUNKS, 16, SSM_GROUP, SSM_STATE), (0, 1, 3, 2))
    return (w4[:, :, :, None, :] * _eye16()[None, :, None, :, None]).reshape(SSM_CHUNKS, CHUNK_STATE, CHUNK_IN)


def _diag_in(g):
    g5 = g.reshape(SSM_CHUNKS, 16, SSM_GROUP, 16, SSM_STATE)
    return jnp.einsum("kgchn,gh->kgcn", g5, _eye16()).reshape(N_SSM_GROUPS, SSM_GROUP, SSM_STATE)


def _diag_out(g):
    g5 = g.reshape(SSM_CHUNKS, 16, SSM_STATE, 16, SSM_GROUP)
    d = jnp.einsum("kgnhc,gh->kgnc", g5, _eye16())
    return jnp.transpose(d, (0, 1, 3, 2)).reshape(N_SSM_GROUPS, SSM_GROUP, SSM_STATE)


def _ssm_in_proj(name, proj, wbd, tm=1024):
    t = proj.shape[0]
    tm = _tile(t, tm)
    col0 = 2 * POOL_WIDTH // CHUNK_IN
    return _mm(name, [(proj, (tm, CHUNK_IN), lambda i, j, s: (i, col0 + j),
                       wbd, (None, CHUNK_IN, CHUNK_STATE), lambda i, j, s: (j, 0, 0))],
               DOT_NN, (t // tm, SSM_CHUNKS, 1),
               [((t, SSM_FLAT), F32, (tm, CHUNK_STATE), lambda i, j, s: (i, j))], 1)[0]


def _ssm_out_proj(s_re, s_im, cbd_re, cbd_imneg, proj, d_skip, tm=1024):
    t = s_re.shape[0]
    tm = _tile(t, tm)
    col0 = 2 * POOL_WIDTH // CHUNK_IN

    def epilogue(acc, ex, out_refs):
        y = acc + ex[1][...] * ex[0][...]
        out_refs[0][...] = y
        out_refs[1][...] = (0.5 * y * (1.0 + jnp.tanh(GELU_C * (y + GELU_A * y * y * y)))).astype(BF16)

    a_map = lambda i, j, s: (i, j)
    b_map = lambda i, j, s: (j, 0, 0)
    o_map = lambda i, j, s: (i, j)
    return _mm("ssm_out_proj",
               [(s_re, (tm, CHUNK_STATE), a_map, cbd_re, (None, CHUNK_STATE, CHUNK_IN), b_map),
                (s_im, (tm, CHUNK_STATE), a_map, cbd_imneg, (None, CHUNK_STATE, CHUNK_IN), b_map)],
               DOT_NN, (t // tm, SSM_CHUNKS, 1),
               [((t, SSM_WIDTH), F32, (tm, CHUNK_IN), o_map), ((t, SSM_WIDTH), BF16, (tm, CHUNK_IN), o_map)], 1,
               [(proj, (tm, CHUNK_IN), lambda i, j, s: (i, col0 + j)), (d_skip, (1, CHUNK_IN), lambda i, j, s: (0, j))],
               epilogue)


def _ssm_dstate(name, dy, cbd, tm=1024):
    t = dy.shape[0]
    tm = _tile(t, tm)
    return _mm(name, [(dy, (tm, CHUNK_IN), lambda i, j, s: (i, j),
                       cbd, (None, CHUNK_STATE, CHUNK_IN), lambda i, j, s: (j, 0, 0))],
               DOT_NT, (t // tm, SSM_CHUNKS, 1),
               [((t, SSM_FLAT), F32, (tm, CHUNK_STATE), lambda i, j, s: (i, j))], 1)[0]


def _ssm_grad_c(name, s, dy, tk=1024):
    t = s.shape[0]
    tk = _tile(t, tk)
    return _mm(name, [(s, (tk, CHUNK_STATE), lambda i, j, k: (k, i), dy, (tk, CHUNK_IN), lambda i, j, k: (k, i))],
               DOT_TN, (SSM_CHUNKS, 1, t // tk),
               [((SSM_CHUNKS, CHUNK_STATE, CHUNK_IN), F32, (None, CHUNK_STATE, CHUNK_IN), lambda i, j, k: (i, 0, 0))],
               t // tk)[0]


def _ssm_grad_b(name, proj, z, tk=1024):
    t = z.shape[0]
    tk = _tile(t, tk)
    col0 = 2 * POOL_WIDTH // CHUNK_IN
    return _mm(name, [(proj, (tk, CHUNK_IN), lambda i, j, k: (k, col0 + i), z, (tk, CHUNK_STATE), lambda i, j, k: (k, i))],
               DOT_TN, (SSM_CHUNKS, 1, t // tk),
               [((SSM_CHUNKS, CHUNK_IN, CHUNK_STATE), F32, (None, CHUNK_IN, CHUNK_STATE), lambda i, j, k: (i, 0, 0))],
               t // tk)[0]


def _ssm_din(z_re, z_im, wbd_re, wbd_im, dy, d_skip, tm=1024):
    t = z_re.shape[0]
    tm = _tile(t, tm)

    def epilogue(acc, ex, out_refs):
        out_refs[0][...] = (acc + ex[1][...] * ex[0][...]).astype(BF16)

    a_map = lambda i, j, s: (i, j)
    b_map = lambda i, j, s: (j, 0, 0)
    return _mm("ssm_din",
               [(z_re, (tm, CHUNK_STATE), a_map, wbd_re, (None, CHUNK_IN, CHUNK_STATE), b_map),
                (z_im, (tm, CHUNK_STATE), a_map, wbd_im, (None, CHUNK_IN, CHUNK_STATE), b_map)],
               DOT_NT, (t // tm, SSM_CHUNKS, 1),
               [((t, SSM_WIDTH), BF16, (tm, CHUNK_IN), lambda i, j, s: (i, j))], 1,
               [(dy, (tm, CHUNK_IN), lambda i, j, s: (i, j)), (d_skip, (1, CHUNK_IN), lambda i, j, s: (0, j))],
               epilogue)[0]


SCAN_COLS = 256
SCAN_ROWS = 1024


def _masked_powers(q_re, q_im, row, forward):
    out = []
    for k in (1, 2, 4):
        src = k - 1 if forward else SUBLANES - k
        mask = (row >= k) if forward else (row < SUBLANES - k)
        p_re = jnp.broadcast_to(q_re[src:src + 1, :], q_re.shape)
        p_im = jnp.broadcast_to(q_im[src:src + 1, :], q_im.shape)
        out.append((jnp.where(mask, p_re, 0.0), jnp.where(mask, p_im, 0.0), k if forward else SUBLANES - k))
    return out


def _tile_scan(x_re, x_im, powers):
    for p_re, p_im, shift in powers:
        r_re = pltpu.roll(x_re, shift, 0)
        r_im = pltpu.roll(x_im, shift, 0)
        x_re, x_im = x_re + p_re * r_re - p_im * r_im, x_im + p_re * r_im + p_im * r_re
    return x_re, x_im


def _scan_fwd(bu_re, bu_im, q_re, q_im):
    t = bu_re.shape[0]
    tc, tt = SCAN_COLS, _tile(t, SCAN_ROWS)
    n_tiles = tt // SUBLANES

    def body(bre_ref, bim_ref, qre_ref, qim_ref, sre_ref, sim_ref, cre_ref, cim_ref):
        @pl.when(pl.program_id(1) == 0)
        def _():
            cre_ref[...] = jnp.zeros_like(cre_ref)
            cim_ref[...] = jnp.zeros_like(cim_ref)

        q_re_v, q_im_v = qre_ref[...], qim_ref[...]
        row = lax.broadcasted_iota(jnp.int32, (SUBLANES, tc), 0)
        powers = _masked_powers(q_re_v, q_im_v, row, True)

        def tile(i, carry):
            c_re, c_im = carry
            rows = pl.ds(pl.multiple_of(i * SUBLANES, SUBLANES), SUBLANES)
            x_re, x_im = _tile_scan(bre_ref[rows, :], bim_ref[rows, :], powers)
            s_re = x_re + q_re_v * c_re - q_im_v * c_im
            s_im = x_im + q_re_v * c_im + q_im_v * c_re
            sre_ref[rows, :] = s_re
            sim_ref[rows, :] = s_im
            last = SUBLANES - 1
            return (jnp.broadcast_to(s_re[last:, :], s_re.shape), jnp.broadcast_to(s_im[last:, :], s_im.shape))

        c_re, c_im = lax.fori_loop(0, n_tiles, tile, (cre_ref[...], cim_ref[...]))
        cre_ref[...] = c_re
        cim_ref[...] = c_im

    blk = pl.BlockSpec((tt, tc), lambda j, i: (i, j))
    qblk = pl.BlockSpec((SUBLANES, tc), lambda j, i: (0, j))
    return pl.pallas_call(
        body, name="scan_fwd", grid=(SSM_FLAT // tc, t // tt),
        in_specs=[blk, blk, qblk, qblk], out_specs=[blk, blk],
        out_shape=[jax.ShapeDtypeStruct((t, SSM_FLAT), F32)] * 2,
        scratch_shapes=[pltpu.VMEM((SUBLANES, tc), F32)] * 2,
        compiler_params=_params(("arbitrary", "arbitrary")),
    )(bu_re, bu_im, q_re, q_im)


def _scan_bwd(ds_re, ds_im, s_re, s_im, qb_re, qb_im):
    t = ds_re.shape[0]
    tc, tt = SCAN_COLS, _tile(t, SCAN_ROWS)
    n_tiles = tt // SUBLANES
    n_chunks = t // tt

    def body(dre_ref, dim_ref, sre_ref, sim_ref, qre_ref, qim_ref, zre_ref, zim_ref, gre_ref, gim_ref,
             cre_ref, cim_ref):
        @pl.when(pl.program_id(1) == 0)
        def _():
            cre_ref[...] = jnp.zeros_like(cre_ref)
            cim_ref[...] = jnp.zeros_like(cim_ref)
            gre_ref[...] = jnp.zeros_like(gre_ref)
            gim_ref[...] = jnp.zeros_like(gim_ref)

        q_re_v, q_im_v = qre_ref[...], qim_ref[...]
        row = lax.broadcasted_iota(jnp.int32, (SUBLANES, tc), 0)
        powers = _masked_powers(q_re_v, q_im_v, row, False)
        is_last = row == SUBLANES - 1

        def tile(n, carry):
            c_re, c_im, g_re, g_im = carry
            i = n_tiles - 1 - n
            rows = pl.ds(pl.multiple_of(i * SUBLANES, SUBLANES), SUBLANES)
            x_re, x_im = _tile_scan(dre_ref[rows, :], dim_ref[rows, :], powers)
            z_re = x_re + q_re_v * c_re - q_im_v * c_im
            z_im = x_im + q_re_v * c_im + q_im_v * c_re
            zre_ref[rows, :] = z_re
            zim_ref[rows, :] = z_im
            zn_re = jnp.where(is_last, c_re, pltpu.roll(z_re, SUBLANES - 1, 0))
            zn_im = jnp.where(is_last, c_im, pltpu.roll(z_im, SUBLANES - 1, 0))
            s_re_v, s_im_v = sre_ref[rows, :], sim_ref[rows, :]
            g_re = g_re + zn_re * s_re_v + zn_im * s_im_v
            g_im = g_im + zn_im * s_re_v - zn_re * s_im_v
            return (jnp.broadcast_to(z_re[:1, :], z_re.shape), jnp.broadcast_to(z_im[:1, :], z_im.shape), g_re, g_im)

        c_re, c_im, g_re, g_im = lax.fori_loop(
            0, n_tiles, tile, (cre_ref[...], cim_ref[...], gre_ref[...], gim_ref[...]))
        cre_ref[...] = c_re
        cim_ref[...] = c_im
        gre_ref[...] = g_re
        gim_ref[...] = g_im

    blk = pl.BlockSpec((tt, tc), lambda j, i: (n_chunks - 1 - i, j))
    qblk = pl.BlockSpec((SUBLANES, tc), lambda j, i: (0, j))
    return pl.pallas_call(
        body, name="scan_bwd", grid=(SSM_FLAT // tc, n_chunks),
        in_specs=[blk, blk, blk, blk, qblk, qblk], out_specs=[blk, blk, qblk, qblk],
        out_shape=[jax.ShapeDtypeStruct((t, SSM_FLAT), F32)] * 2 + [jax.ShapeDtypeStruct((SUBLANES, SSM_FLAT), F32)] * 2,
        scratch_shapes=[pltpu.VMEM((SUBLANES, tc), F32)] * 2,
        compiler_params=_params(("arbitrary", "arbitrary")),
    )(ds_re, ds_im, s_re, s_im, qb_re, qb_im)


def _block(ref, axis, size, index):
    idx = [slice(None)] * len(ref.shape)
    idx[axis] = pl.ds(pl.multiple_of(index * size, size), size)
    return ref.at[tuple(idx)]


def _all_gather(name, shards, axes):
    n = len(shards)
    sizes = [s.shape[a] for s, a in zip(shards, axes)]

    def body(*refs):
        ins, outs = refs[:n], refs[n:2 * n]
        send_sems, recv_sems, local_sems = refs[2 * n:]
        x, y, c = (lax.axis_index(a) for a in MESH_AXES)
        me, sibling = (x, y, c), (x, y, 1 - c)
        chips = [(1 - x, y), (x, 1 - y), (1 - x, 1 - y)]

        def rows(i, dev):
            return _block(outs[i], axes[i], sizes[i], 4 * dev[0] + 2 * dev[1] + dev[2])

        def copy(i, k, block, to, src=None):
            return pltpu.make_async_remote_copy(
                src_ref=rows(i, block) if src is None else src, dst_ref=rows(i, block),
                send_sem=send_sems.at[7 * i + k], recv_sem=recv_sems.at[7 * i + k],
                device_id=to, device_id_type=MESH)

        mine = [pltpu.make_async_copy(ins[i], rows(i, me), local_sems.at[i]) for i in range(n)]
        for cp in mine:
            cp.start()
        first = []
        for i in range(n):
            first.append(copy(i, 0, me, sibling, src=ins[i]))
            first += [copy(i, 1 + j, me, (*chip, c), src=ins[i]) for j, chip in enumerate(chips)]
        for cp in first:
            cp.start()
        passed = []
        for i in range(n):
            for j, chip in enumerate(chips):
                copy(i, 1 + j, (*chip, c), me).wait_recv()
                fwd = copy(i, 4 + j, (*chip, c), sibling)
                fwd.start()
                passed.append(fwd)
        for i in range(n):
            copy(i, 0, sibling, me).wait_recv()
            for j, chip in enumerate(chips):
                copy(i, 4 + j, (*chip, 1 - c), me).wait_recv()
        for cp in first + passed:
            cp.wait_send()
        for cp in mine:
            cp.wait()

    out_shape = []
    for s, a in zip(shards, axes):
        shape = list(s.shape)
        shape[a] *= N_DEV
        out_shape.append(jax.ShapeDtypeStruct(tuple(shape), s.dtype))
    any_spec = pl.BlockSpec(memory_space=pl.ANY)
    return pl.pallas_call(
        body, name=name, out_shape=out_shape,
        in_specs=[any_spec] * n, out_specs=[any_spec] * n,
        scratch_shapes=[pltpu.SemaphoreType.DMA((7 * n,)), pltpu.SemaphoreType.DMA((7 * n,)),
                        pltpu.SemaphoreType.DMA((n,))],
    )(*shards)


def _grad_exchange(name, fulls, axes):
    n = len(fulls)
    sizes = [f.shape[a] // N_DEV for f, a in zip(fulls, axes)]

    def body(*refs):
        ins, outs = refs[:n], refs[n:2 * n]
        send_sems, recv_sems = refs[2 * n:]
        x, y, c = (lax.axis_index(a) for a in MESH_AXES)
        copies = []
        for i in range(n):
            for m in range(1, N_DEV):
                px = 1 - x if m & 4 else x
                py = 1 - y if m & 2 else y
                pc = 1 - c if m & 1 else c
                copies.append(pltpu.make_async_remote_copy(
                    src_ref=_block(ins[i], axes[i], sizes[i], 4 * px + 2 * py + pc), dst_ref=outs[i].at[m - 1],
                    send_sem=send_sems.at[7 * i + m - 1], recv_sem=recv_sems.at[7 * i + m - 1],
                    device_id=(px, py, pc), device_id_type=MESH))
        for cp in copies:
            cp.start()
        for cp in copies:
            cp.wait_recv()
        for cp in copies:
            cp.wait_send()

    out_shape = []
    for f, a, size in zip(fulls, axes, sizes):
        shape = list(f.shape)
        shape[a] = size
        out_shape.append(jax.ShapeDtypeStruct((N_DEV - 1, *shape), f.dtype))
    any_spec = pl.BlockSpec(memory_space=pl.ANY)
    return pl.pallas_call(
        body, name=name, out_shape=out_shape,
        in_specs=[any_spec] * n, out_specs=[any_spec] * n,
        scratch_shapes=[pltpu.SemaphoreType.DMA((7 * n,)), pltpu.SemaphoreType.DMA((7 * n,))],
    )(*fulls)


def _adamw(name, w, m, v, parts):
    r, c = w.shape
    tr = _tile(r, 256)
    c1 = 1.0 - ADAM_B1 ** ADAM_STEP
    c2 = 1.0 - ADAM_B2 ** ADAM_STEP
    n_parts = len(parts)

    def body(*refs):
        w_ref, m_ref, v_ref = refs[:3]
        part_refs = refs[3:3 + n_parts]
        g_ref, d_ref, nm_ref, nv_ref = refs[3 + n_parts:]
        g = None
        for p_ref in part_refs:
            terms = [p_ref[...]] if len(p_ref.shape) == 2 else [p_ref[s] for s in range(p_ref.shape[0])]
            for term in terms:
                term = term.astype(F32)
                g = term if g is None else g + term
        new_m = ADAM_B1 * m_ref[...] + (1.0 - ADAM_B1) * g
        new_v = ADAM_B2 * v_ref[...] + (1.0 - ADAM_B2) * (g * g)
        g_ref[...] = g
        nm_ref[...] = new_m
        nv_ref[...] = new_v
        d_ref[...] = -ADAM_LR * ((new_m / c1) / (jnp.sqrt(new_v / c2) + ADAM_EPS) + ADAM_WD * w_ref[...])

    row = pl.BlockSpec((tr, c), lambda i: (i, 0))
    in_specs = [row, row, row]
    for p in parts:
        in_specs.append(row if p.ndim == 2 else pl.BlockSpec((p.shape[0], tr, c), lambda i: (0, i, 0)))
    return pl.pallas_call(
        body, name=name, grid=(r // tr,), in_specs=in_specs, out_specs=[row] * 4,
        out_shape=[jax.ShapeDtypeStruct((r, c), F32)] * 4,
        compiler_params=_params(("arbitrary",)),
    )(w, m, v, *parts)


SMALL = ("norm_gain", "pool_scale", "a_re", "a_im", "log_dt", "b_re", "b_im", "c_re", "c_im", "d_skip", "final_gain")
LARGE = ("w_in", "w_pool", "w_glu", "w_out", "w_ple", "w_ple_gate")
LARGE_AXIS = {"w_in": 1, "w_pool": 1, "w_glu": 1, "w_out": 0, "w_ple": 1, "w_ple_gate": 0}
WEIGHTS = ("norm_gain", "w_in", "w_pool", "pool_scale", "a_re", "a_im", "log_dt", "b_re", "b_im", "c_re", "c_im",
           "d_skip", "w_glu", "w_out", "w_ple", "w_ple_gate", "final_gain")
PACK_ROWS_ALIGN = SUBLANES * LANES


def _pack(arrays):
    flat = jnp.concatenate([a.reshape(-1) for a in arrays])
    pad = (-flat.shape[0]) % PACK_ROWS_ALIGN
    return jnp.pad(flat, (0, pad)).reshape(-1, LANES)


def _unpack(packed, shapes):
    flat = packed.reshape(-1)
    out, pos = [], 0
    for s in shapes:
        size = math.prod(s)
        out.append(flat[pos:pos + size].reshape(s))
        pos += size
    return out


def kernel(x, p, norm_gain, w_in, w_pool, pool_scale, a_re, a_im, log_dt, b_re, b_im, c_re, c_im, d_skip, w_glu, w_out, w_ple, w_ple_gate, final_gain, loss_target, m_norm_gain, m_w_in, m_w_pool, m_pool_scale, m_a_re, m_a_im, m_log_dt, m_b_re, m_b_im, m_c_re, m_c_im, m_d_skip, m_w_glu, m_w_out, m_w_ple, m_w_ple_gate, m_final_gain, v_norm_gain, v_w_in, v_w_pool, v_pool_scale, v_a_re, v_a_im, v_log_dt, v_b_re, v_b_im, v_c_re, v_c_im, v_d_skip, v_w_glu, v_w_out, v_w_ple, v_w_ple_gate, v_final_gain):
    weights = dict(norm_gain=norm_gain, w_in=w_in, w_pool=w_pool, pool_scale=pool_scale, a_re=a_re, a_im=a_im,
                   log_dt=log_dt, b_re=b_re, b_im=b_im, c_re=c_re, c_im=c_im, d_skip=d_skip, w_glu=w_glu,
                   w_out=w_out, w_ple=w_ple, w_ple_gate=w_ple_gate, final_gain=final_gain)
    mom_m = dict(norm_gain=m_norm_gain, w_in=m_w_in, w_pool=m_w_pool, pool_scale=m_pool_scale, a_re=m_a_re,
                 a_im=m_a_im, log_dt=m_log_dt, b_re=m_b_re, b_im=m_b_im, c_re=m_c_re, c_im=m_c_im,
                 d_skip=m_d_skip, w_glu=m_w_glu, w_out=m_w_out, w_ple=m_w_ple, w_ple_gate=m_w_ple_gate,
                 final_gain=m_final_gain)
    mom_v = dict(norm_gain=v_norm_gain, w_in=v_w_in, w_pool=v_w_pool, pool_scale=v_pool_scale, a_re=v_a_re,
                 a_im=v_a_im, log_dt=v_log_dt, b_re=v_b_re, b_im=v_b_im, c_re=v_c_re, c_im=v_c_im,
                 d_skip=v_d_skip, w_glu=v_w_glu, w_out=v_w_out, w_ple=v_w_ple, w_ple_gate=v_w_ple_gate,
                 final_gain=v_final_gain)

    t = x.shape[1]
    xs = x.reshape(t, D_MODEL)
    ps = p.reshape(t, PLE_DIM)
    target = loss_target.reshape(t, D_MODEL)
    gain1 = norm_gain.reshape(1, D_MODEL)
    gain_f = final_gain.reshape(1, D_MODEL)
    scale_p = pool_scale.reshape(1, POOL_WIDTH)
    skip = d_skip.reshape(1, SSM_WIDTH)

    shard2d = {k: weights[k][0] for k in LARGE}
    gathered = _all_gather("weights_all_gather", [shard2d[k].astype(BF16) for k in LARGE],
                           [LARGE_AXIS[k] for k in LARGE])
    full = dict(zip(LARGE, gathered))

    ar, ai = a_re[0], a_im[0]
    ldt = log_dt.reshape(N_SSM_GROUPS, 1)
    br_t = jnp.transpose(b_re[0], (0, 2, 1))
    bi_t = jnp.transpose(b_im[0], (0, 2, 1))
    pw_re, pw_im, bb_re, bb_im = _ssm_params(ar, ai, ldt, br_t, bi_t)
    q_re = pw_re.reshape(SUBLANES, SSM_FLAT)
    q_im = pw_im.reshape(SUBLANES, SSM_FLAT)
    qb_re = q_re[::-1]
    qb_im = -q_im[::-1]
    wbd_re = _blockdiag_in(bb_re).astype(BF16)
    wbd_im = _blockdiag_in(bb_im).astype(BF16)
    cbd_re = _blockdiag_out(c_re[0]).astype(BF16)
    cbd_imneg = _blockdiag_out(-c_im[0]).astype(BF16)

    hn = _norm1_fwd(xs, gain1)
    proj = _mm_nn("in_proj", hn, full["w_in"], [F32], tk=2048)[0]
    pooled = _pool_fwd(proj)
    tm = _tile(t, 1024)
    mixed = _mm("pool_mix", [(pooled, (tm, POOL_GROUP), lambda i, j, s: (i, j),
                              full["w_pool"], (None, POOL_GROUP, POOL_GROUP), lambda i, j, s: (j, 0, 0))],
                DOT_NN, (t // tm, N_POOL_GROUPS, 1),
                [((t, POOL_WIDTH), F32, (tm, POOL_GROUP), lambda i, j, s: (i, j))], 1)[0]
    bu_re = _ssm_in_proj("ssm_in_proj_re", proj, wbd_re)
    bu_im = _ssm_in_proj("ssm_in_proj_im", proj, wbd_im)
    s_re, s_im = _scan_fwd(bu_re, bu_im, q_re, q_im)
    y, gel = _ssm_out_proj(s_re, s_im, cbd_re, cbd_imneg, proj, skip)
    hg = _mm_nn("glu_proj", gel, full["w_glu"], [F32])[0]
    cat = _gate_fwd(mixed, proj, hg, scale_p)

    def residual_epilogue(acc, ex, out_refs):
        h = acc + ex[0][...]
        out_refs[0][...] = h
        out_refs[1][...] = h.astype(BF16)

    h1, h1b = _mm_nn("out_proj", cat, full["w_out"], [F32, BF16], extras=[xs], epilogue=residual_epilogue)
    e = _mm_nn("ple_proj", ps, full["w_ple"], [F32])[0]
    q = _mm_nn("ple_gate_proj", h1b, full["w_ple_gate"], [F32], tk=2048)[0]
    de, dq, dh2, g_final_gain, loss_part = _final(h1, e, q, target, gain_f)
    loss = lax.psum(loss_part[0, 0], MESH_AXES)

    grads = {}
    grads["w_ple_gate"] = _mm_tn("ple_gate_wgrad", h1b, dq, BF16)
    grads["w_ple"] = _mm_tn("ple_wgrad", ps, de, BF16)
    dh1, dh1b = _mm_nt("ple_gate_dgrad", dq, full["w_ple_gate"], [F32, BF16], extras=[dh2],
                       epilogue=residual_epilogue)
    grads["w_out"] = _mm_tn("out_wgrad", cat, dh1b, BF16)
    dcat = _mm_nt("out_dgrad", dh1b, full["w_out"], [F32], tk=2048)[0]
    dmixed, dga, dgb, dhg, g_pool_scale = _gate_bwd(dcat, mixed, proj, hg, scale_p)

    tk = _tile(t, 1024)
    grads["w_pool"] = _mm("pool_wgrad", [(pooled, (tk, POOL_GROUP), lambda i, j, s: (s, i),
                                          dmixed, (tk, POOL_GROUP), lambda i, j, s: (s, i))],
                          DOT_TN, (N_POOL_GROUPS, 1, t // tk),
                          [((N_POOL_GROUPS, POOL_GROUP, POOL_GROUP), BF16, (None, POOL_GROUP, POOL_GROUP),
                            lambda i, j, s: (i, 0, 0))], t // tk)[0]
    dpooled = _mm("pool_dgrad", [(dmixed, (tm, POOL_GROUP), lambda i, j, s: (i, j),
                                  full["w_pool"], (None, POOL_GROUP, POOL_GROUP), lambda i, j, s: (j, 0, 0))],
                  DOT_NT, (t // tm, N_POOL_GROUPS, 1),
                  [((t, POOL_WIDTH), F32, (tm, POOL_GROUP), lambda i, j, s: (i, j))], 1)[0]
    dua = _pool_bwd(dpooled)

    grads["w_glu"] = _mm_tn("glu_wgrad", gel, dhg, BF16)

    def gelu_bwd_epilogue(acc, ex, out_refs):
        yv = ex[0][...]
        th = jnp.tanh(GELU_C * (yv + GELU_A * yv * yv * yv))
        dgelu = 0.5 * (1.0 + th) + 0.5 * yv * (1.0 - th * th) * GELU_C * (1.0 + 3.0 * GELU_A * yv * yv)
        out_refs[0][...] = acc * dgelu

    dy = _mm_nt("glu_dgrad", dhg, full["w_glu"], [F32], tk=2048, extras=[y], epilogue=gelu_bwd_epilogue)[0]
    ds_re = _ssm_dstate("ssm_dstate_re", dy, cbd_re)
    ds_im = _ssm_dstate("ssm_dstate_im", dy, cbd_imneg)
    g_cbd_re = _ssm_grad_c("ssm_c_re_wgrad", s_re, dy)
    g_cbd_imneg = _ssm_grad_c("ssm_c_im_wgrad", s_im, dy)
    z_re, z_im, gab_re8, gab_im8 = _scan_bwd(ds_re, ds_im, s_re, s_im, qb_re, qb_im)
    g_wbd_re = _ssm_grad_b("ssm_b_re_wgrad", proj, z_re)
    g_wbd_im = _ssm_grad_b("ssm_b_im_wgrad", proj, z_im)
    dub = _ssm_din(z_re, z_im, wbd_re, wbd_im, dy, skip)

    g_ab_re = jnp.sum(gab_re8, axis=0).reshape(N_SSM_GROUPS, SSM_STATE)
    g_ab_im = jnp.sum(gab_im8, axis=0).reshape(N_SSM_GROUPS, SSM_STATE)
    d_ar, d_ai, d_ldt, d_br_t, d_bi_t = _ssm_params_bwd(
        ar, ai, ldt, br_t, bi_t, g_ab_re, g_ab_im, _diag_in(g_wbd_re), _diag_in(g_wbd_im))

    dproj = jnp.concatenate([dua, dga, dub, dgb], axis=1)
    grads["w_in"] = _mm_tn("in_wgrad", hn, dproj, BF16)
    dhn = _mm_nt("in_dgrad", dproj, full["w_in"], [F32], tk=2048)[0]
    grad_x, g_norm_gain = _norm1_bwd(xs, dhn, dh1, gain1)
    g_d_skip = _skip_grad(dy, proj)

    small_grads = dict(
        norm_gain=g_norm_gain, pool_scale=g_pool_scale, a_re=d_ar, a_im=d_ai, log_dt=d_ldt,
        b_re=jnp.transpose(d_br_t, (0, 2, 1)), b_im=jnp.transpose(d_bi_t, (0, 2, 1)),
        c_re=_diag_out(g_cbd_re), c_im=-_diag_out(g_cbd_imneg), d_skip=g_d_skip, final_gain=g_final_gain)
    packed_g = _pack([small_grads[k] for k in SMALL])
    rows = packed_g.shape[0]
    all_g = _all_gather("small_grads_all_gather", [packed_g], [0])[0].reshape(N_DEV, rows, LANES)
    s_g, s_d, s_m, s_v = _adamw("adamw_small", _pack([weights[k] for k in SMALL]), _pack([mom_m[k] for k in SMALL]),
                                _pack([mom_v[k] for k in SMALL]), [all_g])
    shapes = [weights[k].shape for k in SMALL]
    out_g = dict(zip(SMALL, _unpack(s_g, shapes)))
    out_d = dict(zip(SMALL, _unpack(s_d, shapes)))
    out_m = dict(zip(SMALL, _unpack(s_m, shapes)))
    out_v = dict(zip(SMALL, _unpack(s_v, shapes)))

    landed = _grad_exchange("grads_exchange", [grads[k] for k in LARGE], [LARGE_AXIS[k] for k in LARGE])
    me = 4 * lax.axis_index("x") + 2 * lax.axis_index("y") + lax.axis_index("c")
    for k, land in zip(LARGE, landed):
        shard_shape = shard2d[k].shape
        size = shard_shape[LARGE_AXIS[k]]
        own = lax.dynamic_slice_in_dim(grads[k], me * size, size, axis=LARGE_AXIS[k])
        view = (-1, shard_shape[-1])
        res = _adamw("adamw_" + k, shard2d[k].reshape(view), mom_m[k][0].reshape(view), mom_v[k][0].reshape(view),
                     [own.reshape(view), land.reshape((N_DEV - 1,) + (math.prod(shard_shape[:-1]), shard_shape[-1]))])
        out_g[k], out_d[k], out_m[k], out_v[k] = (r.reshape(weights[k].shape) for r in res)

    return (loss, grad_x.reshape(x.shape), *[out_g[k] for k in WEIGHTS], *[out_d[k] for k in WEIGHTS],
            *[out_m[k] for k in WEIGHTS], *[out_v[k] for k in WEIGHTS])


def _skip_grad(dy, proj):
    t = dy.shape[0]
    tm = _tile(t, 512)

    def body(dy_ref, u_ref, o_ref):
        @pl.when(pl.program_id(0) == 0)
        def _():
            o_ref[...] = jnp.zeros_like(o_ref)

        o_ref[...] += jnp.sum(dy_ref[...] * u_ref[...], axis=0, keepdims=True)

    return pl.pallas_call(
        body, name="skip_grad", grid=(t // tm,),
        in_specs=[pl.BlockSpec((tm, SSM_WIDTH), lambda i: (i, 0)), pl.BlockSpec((tm, SSM_WIDTH), lambda i: (i, 2))],
        out_specs=pl.BlockSpec((1, SSM_WIDTH), lambda i: (0, 0)),
        out_shape=jax.ShapeDtypeStruct((1, SSM_WIDTH), F32),
        compiler_params=_params(("arbitrary",)),
    )(dy, proj)
```

```python
import functools
import math

import jax
import jax.numpy as jnp
from jax import lax
from jax.experimental import pallas as pl
from jax.experimental.pallas import tpu as pltpu

F32 = jnp.float32
BF16 = jnp.bfloat16
MESH = pl.DeviceIdType.MESH
MESH_AXES = ("x", "y", "c")
N_DEV = 8

D_MODEL = 2048
POOL_WIDTH = 1024
SSM_WIDTH = 1024
N_POOL_GROUPS = 4
POOL_GROUP = 256
SSM_GROUP = 16
N_SSM_GROUPS = 64
SSM_STATE = 64
SSM_FLAT = N_SSM_GROUPS * SSM_STATE
SSM_CHUNKS = 4
CHUNK_IN = SSM_WIDTH // SSM_CHUNKS
CHUNK_STATE = SSM_FLAT // SSM_CHUNKS
PLE_DIM = 256
EPS = 1e-6
A_RE_MAX = -1e-4
ADAM_LR = 0.001
ADAM_B1 = 0.9
ADAM_B2 = 0.999
ADAM_EPS = 1e-08
ADAM_WD = 0.01
ADAM_STEP = 10
GELU_C = math.sqrt(2.0 / math.pi)
GELU_A = 0.044715

SUBLANES = 8
LANES = 128
VMEM_LIMIT_BYTES = 48 * 1024 * 1024

DOT_NN = (((1,), (0,)), ((), ()))
DOT_NT = (((1,), (1,)), ((), ()))
DOT_TN = (((0,), (0,)), ((), ()))


def _tile(n, pref):
    return pref if n % pref == 0 else n


def _params(sem):
    return pltpu.CompilerParams(dimension_semantics=sem, vmem_limit_bytes=VMEM_LIMIT_BYTES)


def _sigmoid(v):
    return 1.0 / (1.0 + jnp.exp(-v))


def _silu_and_grad(v):
    s = _sigmoid(v)
    return v * s, s * (1.0 + v * (1.0 - s))


def _mm(name, pairs, dims, grid, outs, k_steps, extras=(), epilogue=None):
    n_pairs, n_ex, n_out = len(pairs), len(extras), len(outs)
    acc_shape = tuple(d for d in outs[0][2] if d is not None)
    if epilogue is None:
        def epilogue(acc, ex, out_refs):
            out_refs[0][...] = acc.astype(out_refs[0].dtype)

    def body(*refs):
        ab = refs[:2 * n_pairs]
        ex = refs[2 * n_pairs:2 * n_pairs + n_ex]
        out_refs = refs[2 * n_pairs + n_ex:2 * n_pairs + n_ex + n_out]
        acc = refs[-1]
        k = pl.program_id(2)

        @pl.when(k == 0)
        def _():
            acc[...] = jnp.zeros_like(acc)

        part = None
        for q in range(n_pairs):
            d = lax.dot_general(ab[2 * q][...].astype(BF16), ab[2 * q + 1][...].astype(BF16), dims,
                                preferred_element_type=F32)
            part = d if part is None else part + d
        acc[...] += part

        @pl.when(k == k_steps - 1)
        def _():
            epilogue(acc[...], ex, out_refs)

    in_specs, operands = [], []
    for a, a_blk, a_map, b, b_blk, b_map in pairs:
        in_specs += [pl.BlockSpec(a_blk, a_map), pl.BlockSpec(b_blk, b_map)]
        operands += [a, b]
    for e, e_blk, e_map in extras:
        in_specs.append(pl.BlockSpec(e_blk, e_map))
        operands.append(e)
    return pl.pallas_call(
        body, name=name, grid=grid, in_specs=in_specs,
        out_specs=[pl.BlockSpec(o[2], o[3]) for o in outs],
        out_shape=[jax.ShapeDtypeStruct(o[0], o[1]) for o in outs],
        scratch_shapes=[pltpu.VMEM(acc_shape, F32)],
        compiler_params=_params(("arbitrary", "arbitrary", "arbitrary")),
    )(*operands)


def _mm_nn(name, a, b, out_dtypes, tm=1024, tn=1024, tk=1024, a_col0=0, extras=(), epilogue=None):
    m, n = a.shape[0], b.shape[1]
    k = b.shape[0]
    tm, tn, tk = _tile(m, tm), _tile(n, tn), _tile(k, tk)
    outs = [((m, n), dt, (tm, tn), lambda i, j, s: (i, j)) for dt in out_dtypes]
    ex = [(e, (tm, tn), lambda i, j, s: (i, j)) for e in extras]
    return _mm(name, [(a, (tm, tk), lambda i, j, s: (i, a_col0 + s), b, (tk, tn), lambda i, j, s: (s, j))],
               DOT_NN, (m // tm, n // tn, k // tk), outs, k // tk, ex, epilogue)


def _mm_nt(name, a, b, out_dtypes, tm=1024, tn=1024, tk=1024, extras=(), epilogue=None):
    m, kk = a.shape
    n = b.shape[0]
    tm, tn, tk = _tile(m, tm), _tile(n, tn), _tile(kk, tk)
    outs = [((m, n), dt, (tm, tn), lambda i, j, s: (i, j)) for dt in out_dtypes]
    ex = [(e, (tm, tn), lambda i, j, s: (i, j)) for e in extras]
    return _mm(name, [(a, (tm, tk), lambda i, j, s: (i, s), b, (tn, tk), lambda i, j, s: (j, s))],
               DOT_NT, (m // tm, n // tn, kk // tk), outs, kk // tk, ex, epilogue)


def _mm_tn(name, a, b, out_dtype, tm=512, tn=2048, tk=1024):
    m, kk = a.shape
    n = b.shape[1]
    tm, tn, tk = _tile(kk, tm), _tile(n, tn), _tile(m, tk)
    outs = [((kk, n), out_dtype, (tm, tn), lambda i, j, s: (i, j))]
    return _mm(name, [(a, (tk, tm), lambda i, j, s: (s, i), b, (tk, tn), lambda i, j, s: (s, j))],
               DOT_TN, (kk // tm, n // tn, m // tk), outs, m // tk)[0]


def _norm1_fwd(x, gain):
    t = x.shape[0]
    tm = _tile(t, 512)

    def body(x_ref, g_ref, hn_ref):
        xv = x_ref[...]
        r = lax.rsqrt(jnp.mean(xv * xv, axis=-1, keepdims=True) + EPS)
        hn_ref[...] = (xv * r * g_ref[...]).astype(BF16)

    return pl.pallas_call(
        body, name="norm1_fwd", grid=(t // tm,),
        in_specs=[pl.BlockSpec((tm, D_MODEL), lambda i: (i, 0)), pl.BlockSpec((1, D_MODEL), lambda i: (0, 0))],
        out_specs=pl.BlockSpec((tm, D_MODEL), lambda i: (i, 0)),
        out_shape=jax.ShapeDtypeStruct((t, D_MODEL), BF16),
        compiler_params=_params(("arbitrary",)),
    )(x, gain)


def _norm1_bwd(x, dhn, dh1, gain):
    t = x.shape[0]
    tm = _tile(t, 512)

    def body(x_ref, dhn_ref, dh1_ref, g_ref, dx_ref, gg_ref):
        @pl.when(pl.program_id(0) == 0)
        def _():
            gg_ref[...] = jnp.zeros_like(gg_ref)

        xv = x_ref[...]
        r = lax.rsqrt(jnp.mean(xv * xv, axis=-1, keepdims=True) + EPS)
        xh = xv * r
        dhn_v = dhn_ref[...]
        gg_ref[...] += jnp.sum(dhn_v * xh, axis=0, keepdims=True)
        dxh = dhn_v * g_ref[...]
        dx_ref[...] = dh1_ref[...] + r * (dxh - xh * jnp.mean(dxh * xh, axis=-1, keepdims=True))

    row = pl.BlockSpec((tm, D_MODEL), lambda i: (i, 0))
    vec = pl.BlockSpec((1, D_MODEL), lambda i: (0, 0))
    return pl.pallas_call(
        body, name="norm1_bwd", grid=(t // tm,),
        in_specs=[row, row, row, vec], out_specs=[row, vec],
        out_shape=[jax.ShapeDtypeStruct((t, D_MODEL), F32), jax.ShapeDtypeStruct((1, D_MODEL), F32)],
        compiler_params=_params(("arbitrary",)),
    )(x, dhn, dh1, gain)


def _pool_counts(t, width, group):
    row = lax.broadcasted_iota(jnp.int32, (t, width), 0)
    window = jnp.left_shift(jnp.int32(2), group)
    return row, jnp.minimum(row + 1, window).astype(F32)


def _select_window(group, s2, s4, s8, s16):
    return jnp.where(group == 0, s2, jnp.where(group == 1, s4, jnp.where(group == 2, s8, s16)))


def _pool_fwd(proj):
    t = proj.shape[0]
    tc = LANES

    def body(u_ref, o_ref):
        group = pl.program_id(0) // (POOL_GROUP // tc)
        v = u_ref[...]
        row, count = _pool_counts(t, tc, group)

        def down(a, j):
            return jnp.where(row >= j, pltpu.roll(a, j, 0), 0.0)

        s2 = v + down(v, 1)
        s4 = s2 + down(s2, 2)
        s8 = s4 + down(s4, 4)
        s16 = s8 + down(s8, 8)
        o_ref[...] = (_select_window(group, s2, s4, s8, s16) / count - v).astype(BF16)

    return pl.pallas_call(
        body, name="pool_fwd", grid=(POOL_WIDTH // tc,),
        in_specs=[pl.BlockSpec((t, tc), lambda j: (0, j))],
        out_specs=pl.BlockSpec((t, tc), lambda j: (0, j)),
        out_shape=jax.ShapeDtypeStruct((t, POOL_WIDTH), BF16),
        compiler_params=_params(("arbitrary",)),
    )(proj)


def _pool_bwd(dpooled):
    t = dpooled.shape[0]
    tc = LANES

    def body(d_ref, o_ref):
        group = pl.program_id(0) // (POOL_GROUP // tc)
        dp = d_ref[...]
        row, count = _pool_counts(t, tc, group)
        r = dp / count

        def up(a, j):
            return jnp.where(row < t - j, pltpu.roll(a, t - j, 0), 0.0)

        s2 = r + up(r, 1)
        s4 = s2 + up(s2, 2)
        s8 = s4 + up(s4, 4)
        s16 = s8 + up(s8, 8)
        o_ref[...] = (_select_window(group, s2, s4, s8, s16) - dp).astype(BF16)

    return pl.pallas_call(
        body, name="pool_bwd", grid=(POOL_WIDTH // tc,),
        in_specs=[pl.BlockSpec((t, tc), lambda j: (0, j))],
        out_specs=pl.BlockSpec((t, tc), lambda j: (0, j)),
        out_shape=jax.ShapeDtypeStruct((t, POOL_WIDTH), BF16),
        compiler_params=_params(("arbitrary",)),
    )(dpooled)


def _gate_fwd(mixed, proj, hg, pool_scale):
    t = mixed.shape[0]
    tm = _tile(t, 512)

    def body(mx_ref, ga_ref, gb_ref, hg_ref, ps_ref, cat_ref):
        silu_a, _ = _silu_and_grad(ga_ref[...])
        cat_ref[:, :POOL_WIDTH] = (mx_ref[...] * ps_ref[...] * silu_a).astype(BF16)
        silu_b, _ = _silu_and_grad(gb_ref[...])
        sb = hg_ref[:, :SSM_WIDTH] * _sigmoid(hg_ref[:, SSM_WIDTH:])
        cat_ref[:, POOL_WIDTH:] = (sb * silu_b).astype(BF16)

    return pl.pallas_call(
        body, name="gate_fwd", grid=(t // tm,),
        in_specs=[pl.BlockSpec((tm, POOL_WIDTH), lambda i: (i, 0)),
                  pl.BlockSpec((tm, POOL_WIDTH), lambda i: (i, 1)),
                  pl.BlockSpec((tm, SSM_WIDTH), lambda i: (i, 3)),
                  pl.BlockSpec((tm, 2 * SSM_WIDTH), lambda i: (i, 0)),
                  pl.BlockSpec((1, POOL_WIDTH), lambda i: (0, 0))],
        out_specs=pl.BlockSpec((tm, D_MODEL), lambda i: (i, 0)),
        out_shape=jax.ShapeDtypeStruct((t, D_MODEL), BF16),
        compiler_params=_params(("arbitrary",)),
    )(mixed, proj, proj, hg, pool_scale)


def _gate_bwd(dcat, mixed, proj, hg, pool_scale):
    t = mixed.shape[0]
    tm = _tile(t, 512)

    def body(dc_ref, mx_ref, ga_ref, gb_ref, hg_ref, ps_ref, dmx_ref, dga_ref, dgb_ref, dhg_ref, gps_ref):
        @pl.when(pl.program_id(0) == 0)
        def _():
            gps_ref[...] = jnp.zeros_like(gps_ref)

        ps = ps_ref[...]
        mx = mx_ref[...]
        dya = dc_ref[:, :POOL_WIDTH]
        silu_a, dsilu_a = _silu_and_grad(ga_ref[...])
        dpa = dya * silu_a
        gps_ref[...] += jnp.sum(dpa * mx, axis=0, keepdims=True)
        dmx_ref[...] = (dpa * ps).astype(BF16)
        dga_ref[...] = (dya * mx * ps * dsilu_a).astype(BF16)

        dyb = dc_ref[:, POOL_WIDTH:]
        silu_b, dsilu_b = _silu_and_grad(gb_ref[...])
        h_a = hg_ref[:, :SSM_WIDTH]
        sg = _sigmoid(hg_ref[:, SSM_WIDTH:])
        dsb = dyb * silu_b
        dgb_ref[...] = (dyb * h_a * sg * dsilu_b).astype(BF16)
        dhg_ref[:, :SSM_WIDTH] = (dsb * sg).astype(BF16)
        dhg_ref[:, SSM_WIDTH:] = (dsb * h_a * sg * (1.0 - sg)).astype(BF16)

    half = pl.BlockSpec((tm, POOL_WIDTH), lambda i: (i, 0))
    full = pl.BlockSpec((tm, D_MODEL), lambda i: (i, 0))
    vec = pl.BlockSpec((1, POOL_WIDTH), lambda i: (0, 0))
    return pl.pallas_call(
        body, name="gate_bwd", grid=(t // tm,),
        in_specs=[full, half,
                  pl.BlockSpec((tm, POOL_WIDTH), lambda i: (i, 1)),
                  pl.BlockSpec((tm, SSM_WIDTH), lambda i: (i, 3)),
                  full, vec],
        out_specs=[half, half, half, full, vec],
        out_shape=[jax.ShapeDtypeStruct((t, POOL_WIDTH), BF16), jax.ShapeDtypeStruct((t, POOL_WIDTH), BF16),
                   jax.ShapeDtypeStruct((t, SSM_WIDTH), BF16), jax.ShapeDtypeStruct((t, 2 * SSM_WIDTH), BF16),
                   jax.ShapeDtypeStruct((1, POOL_WIDTH), F32)],
        compiler_params=_params(("arbitrary",)),
    )(dcat, mixed, proj, proj, hg, pool_scale)


def _final(h1, e, q, target, gain):
    t = h1.shape[0]
    tm = _tile(t, 256)

    def body(h1_ref, e_ref, q_ref, tg_ref, g_ref, de_ref, dq_ref, dh2_ref, gg_ref, loss_ref):
        @pl.when(pl.program_id(0) == 0)
        def _():
            gg_ref[...] = jnp.zeros_like(gg_ref)
            loss_ref[...] = jnp.zeros_like(loss_ref)

        ev = e_ref[...]
        sg = _sigmoid(q_ref[...])
        h2 = h1_ref[...] + ev * sg
        r = lax.rsqrt(jnp.mean(h2 * h2, axis=-1, keepdims=True) + EPS)
        n = h2 * r
        gain_v = g_ref[...]
        diff = n * gain_v - tg_ref[...]
        row_loss = jnp.sum(diff * diff, axis=-1, keepdims=True)
        loss_ref[...] += (0.5 / D_MODEL) * jnp.sum(row_loss, axis=0, keepdims=True)
        dout = diff * (1.0 / D_MODEL)
        gg_ref[...] += jnp.sum(dout * n, axis=0, keepdims=True)
        dn = dout * gain_v
        dh2 = r * (dn - n * jnp.mean(dn * n, axis=-1, keepdims=True))
        dh2_ref[...] = dh2
        de_ref[...] = (dh2 * sg).astype(BF16)
        dq_ref[...] = (dh2 * ev * sg * (1.0 - sg)).astype(BF16)

    row = pl.BlockSpec((tm, D_MODEL), lambda i: (i, 0))
    vec = pl.BlockSpec((1, D_MODEL), lambda i: (0, 0))
    return pl.pallas_call(
        body, name="final_norm_loss", grid=(t // tm,),
        in_specs=[row, row, row, row, vec],
        out_specs=[row, row, row, vec, pl.BlockSpec((1, 1), lambda i: (0, 0))],
        out_shape=[jax.ShapeDtypeStruct((t, D_MODEL), BF16), jax.ShapeDtypeStruct((t, D_MODEL), BF16),
                   jax.ShapeDtypeStruct((t, D_MODEL), F32), jax.ShapeDtypeStruct((1, D_MODEL), F32),
                   jax.ShapeDtypeStruct((1, 1), F32)],
        compiler_params=_params(("arbitrary",)),
    )(h1, e, q, target, gain)


def _zoh(a_re, a_im, log_dt, b_re_t, b_im_t):
    lam_re = jnp.minimum(a_re, A_RE_MAX)
    lam_im = a_im
    dt = jnp.exp(log_dt)
    mag = jnp.exp(lam_re * dt)
    ang = lam_im * dt
    ab_re = mag * jnp.cos(ang)
    ab_im = mag * jnp.sin(ang)
    den = lam_re * lam_re + lam_im * lam_im
    n_re = ab_re - 1.0
    n_im = ab_im
    q_re = (n_re * lam_re + n_im * lam_im) / den
    q_im = (n_im * lam_re - n_re * lam_im) / den
    bb_re = q_re[:, None, :] * b_re_t - q_im[:, None, :] * b_im_t
    bb_im = q_re[:, None, :] * b_im_t + q_im[:, None, :] * b_re_t
    return ab_re, ab_im, bb_re, bb_im


def _ssm_params(a_re, a_im, log_dt, b_re_t, b_im_t):
    g, n = a_re.shape

    def body(are_ref, aim_ref, dt_ref, bre_ref, bim_ref, pre_ref, pim_ref, bbre_ref, bbim_ref):
        ab_re, ab_im, bb_re, bb_im = _zoh(are_ref[...], aim_ref[...], dt_ref[...], bre_ref[...], bim_ref[...])
        bbre_ref[...] = bb_re
        bbim_ref[...] = bb_im
        p_re, p_im = ab_re, ab_im
        for r in range(SUBLANES):
            pre_ref[r] = p_re
            pim_ref[r] = p_im
            p_re, p_im = p_re * ab_re - p_im * ab_im, p_re * ab_im + p_im * ab_re

    return pl.pallas_call(
        body, name="ssm_params",
        out_shape=[jax.ShapeDtypeStruct((SUBLANES, g, n), F32), jax.ShapeDtypeStruct((SUBLANES, g, n), F32),
                   jax.ShapeDtypeStruct(b_re_t.shape, F32), jax.ShapeDtypeStruct(b_re_t.shape, F32)],
        compiler_params=_params(None),
    )(a_re, a_im, log_dt, b_re_t, b_im_t)


def _ssm_params_bwd(a_re, a_im, log_dt, b_re_t, b_im_t, g_ab_re, g_ab_im, g_bb_re, g_bb_im):
    def body(are_ref, aim_ref, dt_ref, bre_ref, bim_ref, gar_ref, gai_ref, gbr_ref, gbi_ref,
             o_are, o_aim, o_dt, o_bre, o_bim):
        _, vjp = jax.vjp(_zoh, are_ref[...], aim_ref[...], dt_ref[...], bre_ref[...], bim_ref[...])
        d_are, d_aim, d_dt, d_bre, d_bim = vjp((gar_ref[...], gai_ref[...], gbr_ref[...], gbi_ref[...]))
        o_are[...] = d_are
        o_aim[...] = d_aim
        o_dt[...] = d_dt
        o_bre[...] = d_bre
        o_bim[...] = d_bim

    ins = (a_re, a_im, log_dt, b_re_t, b_im_t)
    return pl.pallas_call(
        body, name="ssm_params_bwd",
        out_shape=[jax.ShapeDtypeStruct(v.shape, F32) for v in ins],
        compiler_params=_params(None),
    )(*ins, g_ab_re, g_ab_im, g_bb_re, g_bb_im)


GROUPS_PER_CHUNK = N_SSM_GROUPS // SSM_CHUNKS


def _diag_mask(rows_per_group, cols_per_group):
    shape = (GROUPS_PER_CHUNK * rows_per_group, GROUPS_PER_CHUNK * cols_per_group)
    r = lax.broadcasted_iota(jnp.int32, shape, 0) // rows_per_group
    c = lax.broadcasted_iota(jnp.int32, shape, 1) // cols_per_group
    return (r == c)[None]


def _blockdiag_in(w):
    tiled = jnp.tile(w.reshape(SSM_CHUNKS, CHUNK_IN, SSM_STATE), (1, 1, GROUPS_PER_CHUNK))
    return jnp.where(_diag_mask(SSM_GROUP, SSM_STATE), tiled, 0.0).astype(BF16)


def _diag_in(g):
    kept = jnp.where(_diag_mask(SSM_GROUP, SSM_STATE), g, 0.0)
    return kept.reshape(SSM_CHUNKS, CHUNK_IN, GROUPS_PER_CHUNK, SSM_STATE).sum(axis=2).reshape(
        N_SSM_GROUPS, SSM_GROUP, SSM_STATE)


def _ssm_in_proj(name, proj, wbd, tm=1024):
    t = proj.shape[0]
    tm = _tile(t, tm)
    col0 = 2 * POOL_WIDTH // CHUNK_IN
    return _mm(name, [(proj, (tm, CHUNK_IN), lambda i, j, s: (i, col0 + j),
                       wbd, (None, CHUNK_IN, CHUNK_STATE), lambda i, j, s: (j, 0, 0))],
               DOT_NN, (t // tm, SSM_CHUNKS, 1),
               [((t, SSM_FLAT), F32, (tm, CHUNK_STATE), lambda i, j, s: (i, j))], 1)[0]


def _ssm_out_proj(s_re, s_im, cbd_re, cbd_imneg, proj, d_skip, tm=1024):
    t = s_re.shape[0]
    tm = _tile(t, tm)
    col0 = 2 * POOL_WIDTH // CHUNK_IN

    def epilogue(acc, ex, out_refs):
        y = acc + ex[1][...] * ex[0][...]
        out_refs[0][...] = y
        out_refs[1][...] = (0.5 * y * (1.0 + jnp.tanh(GELU_C * (y + GELU_A * y * y * y)))).astype(BF16)

    a_map = lambda i, j, s: (i, j)
    b_map = lambda i, j, s: (j, 0, 0)
    o_map = lambda i, j, s: (i, j)
    return _mm("ssm_out_proj",
               [(s_re, (tm, CHUNK_STATE), a_map, cbd_re, (None, CHUNK_IN, CHUNK_STATE), b_map),
                (s_im, (tm, CHUNK_STATE), a_map, cbd_imneg, (None, CHUNK_IN, CHUNK_STATE), b_map)],
               DOT_NT, (t // tm, SSM_CHUNKS, 1),
               [((t, SSM_WIDTH), F32, (tm, CHUNK_IN), o_map), ((t, SSM_WIDTH), BF16, (tm, CHUNK_IN), o_map)], 1,
               [(proj, (tm, CHUNK_IN), lambda i, j, s: (i, col0 + j)), (d_skip, (1, CHUNK_IN), lambda i, j, s: (0, j))],
               epilogue)


def _ssm_dstate(name, dy, cbd, tm=1024):
    t = dy.shape[0]
    tm = _tile(t, tm)
    return _mm(name, [(dy, (tm, CHUNK_IN), lambda i, j, s: (i, j),
                       cbd, (None, CHUNK_IN, CHUNK_STATE), lambda i, j, s: (j, 0, 0))],
               DOT_NN, (t // tm, SSM_CHUNKS, 1),
               [((t, SSM_FLAT), F32, (tm, CHUNK_STATE), lambda i, j, s: (i, j))], 1)[0]


def _ssm_wgrad(name, a, col0, z, tk=1024):
    t = z.shape[0]
    tk = _tile(t, tk)
    return _mm(name, [(a, (tk, CHUNK_IN), lambda i, j, k: (k, col0 + i), z, (tk, CHUNK_STATE), lambda i, j, k: (k, i))],
               DOT_TN, (SSM_CHUNKS, 1, t // tk),
               [((SSM_CHUNKS, CHUNK_IN, CHUNK_STATE), F32, (None, CHUNK_IN, CHUNK_STATE), lambda i, j, k: (i, 0, 0))],
               t // tk)[0]


def _ssm_din(z_re, z_im, wbd_re, wbd_im, dy, d_skip, tm=1024):
    t = z_re.shape[0]
    tm = _tile(t, tm)

    def epilogue(acc, ex, out_refs):
        out_refs[0][...] = (acc + ex[1][...] * ex[0][...]).astype(BF16)

    a_map = lambda i, j, s: (i, j)
    b_map = lambda i, j, s: (j, 0, 0)
    return _mm("ssm_din",
               [(z_re, (tm, CHUNK_STATE), a_map, wbd_re, (None, CHUNK_IN, CHUNK_STATE), b_map),
                (z_im, (tm, CHUNK_STATE), a_map, wbd_im, (None, CHUNK_IN, CHUNK_STATE), b_map)],
               DOT_NT, (t // tm, SSM_CHUNKS, 1),
               [((t, SSM_WIDTH), BF16, (tm, CHUNK_IN), lambda i, j, s: (i, j))], 1,
               [(dy, (tm, CHUNK_IN), lambda i, j, s: (i, j)), (d_skip, (1, CHUNK_IN), lambda i, j, s: (0, j))],
               epilogue)[0]


SCAN_COLS = 256
SCAN_ROWS = 1024
SCAN_UNROLL = 2


def _unrolled(tile):
    def body(i, carry):
        for u in range(SCAN_UNROLL):
            carry = tile(i * SCAN_UNROLL + u, carry)
        return carry
    return body


def _masked_powers(q_re, q_im, row, forward):
    out = []
    for k in (1, 2, 4):
        src = k - 1 if forward else SUBLANES - k
        mask = (row >= k) if forward else (row < SUBLANES - k)
        p_re = jnp.broadcast_to(q_re[src:src + 1, :], q_re.shape)
        p_im = jnp.broadcast_to(q_im[src:src + 1, :], q_im.shape)
        out.append((jnp.where(mask, p_re, 0.0), jnp.where(mask, p_im, 0.0), k if forward else SUBLANES - k))
    return out


def _tile_scan(x_re, x_im, powers):
    for p_re, p_im, shift in powers:
        r_re = pltpu.roll(x_re, shift, 0)
        r_im = pltpu.roll(x_im, shift, 0)
        x_re, x_im = x_re + p_re * r_re - p_im * r_im, x_im + p_re * r_im + p_im * r_re
    return x_re, x_im


def _scan_fwd(bu_re, bu_im, q_re, q_im):
    t = bu_re.shape[0]
    tc, tt = SCAN_COLS, _tile(t, SCAN_ROWS)
    n_tiles = tt // SUBLANES

    def body(bre_ref, bim_ref, qre_ref, qim_ref, sre_ref, sim_ref, cre_ref, cim_ref):
        @pl.when(pl.program_id(1) == 0)
        def _():
            cre_ref[...] = jnp.zeros_like(cre_ref)
            cim_ref[...] = jnp.zeros_like(cim_ref)

        q_re_v, q_im_v = qre_ref[...], qim_ref[...]
        row = lax.broadcasted_iota(jnp.int32, (SUBLANES, tc), 0)
        powers = _masked_powers(q_re_v, q_im_v, row, True)

        def tile(i, carry):
            c_re, c_im = carry
            rows = pl.ds(pl.multiple_of(i * SUBLANES, SUBLANES), SUBLANES)
            x_re, x_im = _tile_scan(bre_ref[rows, :], bim_ref[rows, :], powers)
            s_re = x_re + q_re_v * c_re - q_im_v * c_im
            s_im = x_im + q_re_v * c_im + q_im_v * c_re
            sre_ref[rows, :] = s_re
            sim_ref[rows, :] = s_im
            last = SUBLANES - 1
            return (jnp.broadcast_to(s_re[last:, :], s_re.shape), jnp.broadcast_to(s_im[last:, :], s_im.shape))

        c_re, c_im = lax.fori_loop(0, n_tiles // SCAN_UNROLL, _unrolled(tile), (cre_ref[...], cim_ref[...]))
        cre_ref[...] = c_re
        cim_ref[...] = c_im

    blk = pl.BlockSpec((tt, tc), lambda j, i: (i, j))
    qblk = pl.BlockSpec((SUBLANES, tc), lambda j, i: (0, j))
    return pl.pallas_call(
        body, name="scan_fwd", grid=(SSM_FLAT // tc, t // tt),
        in_specs=[blk, blk, qblk, qblk], out_specs=[blk, blk],
        out_shape=[jax.ShapeDtypeStruct((t, SSM_FLAT), F32)] * 2,
        scratch_shapes=[pltpu.VMEM((SUBLANES, tc), F32)] * 2,
        compiler_params=_params(("arbitrary", "arbitrary")),
    )(bu_re, bu_im, q_re, q_im)


def _scan_bwd(ds_re, ds_im, s_re, s_im, qb_re, qb_im):
    t = ds_re.shape[0]
    tc, tt = SCAN_COLS, _tile(t, SCAN_ROWS)
    n_tiles = tt // SUBLANES
    n_chunks = t // tt

    def body(dre_ref, dim_ref, sre_ref, sim_ref, qre_ref, qim_ref, zre_ref, zim_ref, gre_ref, gim_ref,
             cre_ref, cim_ref):
        @pl.when(pl.program_id(1) == 0)
        def _():
            cre_ref[...] = jnp.zeros_like(cre_ref)
            cim_ref[...] = jnp.zeros_like(cim_ref)
            gre_ref[...] = jnp.zeros_like(gre_ref)
            gim_ref[...] = jnp.zeros_like(gim_ref)

        q_re_v, q_im_v = qre_ref[...], qim_ref[...]
        row = lax.broadcasted_iota(jnp.int32, (SUBLANES, tc), 0)
        powers = _masked_powers(q_re_v, q_im_v, row, False)
        is_last = row == SUBLANES - 1

        def tile(n, carry):
            c_re, c_im, g_re, g_im = carry
            i = n_tiles - 1 - n
            rows = pl.ds(pl.multiple_of(i * SUBLANES, SUBLANES), SUBLANES)
            x_re, x_im = _tile_scan(dre_ref[rows, :], dim_ref[rows, :], powers)
            z_re = x_re + q_re_v * c_re - q_im_v * c_im
            z_im = x_im + q_re_v * c_im + q_im_v * c_re
            zre_ref[rows, :] = z_re
            zim_ref[rows, :] = z_im
            zn_re = jnp.where(is_last, c_re, pltpu.roll(z_re, SUBLANES - 1, 0))
            zn_im = jnp.where(is_last, c_im, pltpu.roll(z_im, SUBLANES - 1, 0))
            s_re_v, s_im_v = sre_ref[rows, :], sim_ref[rows, :]
            g_re = g_re + zn_re * s_re_v + zn_im * s_im_v
            g_im = g_im + zn_im * s_re_v - zn_re * s_im_v
            return (jnp.broadcast_to(z_re[:1, :], z_re.shape), jnp.broadcast_to(z_im[:1, :], z_im.shape), g_re, g_im)

        c_re, c_im, g_re, g_im = lax.fori_loop(
            0, n_tiles // SCAN_UNROLL, _unrolled(tile), (cre_ref[...], cim_ref[...], gre_ref[...], gim_ref[...]))
        cre_ref[...] = c_re
        cim_ref[...] = c_im
        gre_ref[...] = g_re
        gim_ref[...] = g_im

    blk = pl.BlockSpec((tt, tc), lambda j, i: (n_chunks - 1 - i, j))
    qblk = pl.BlockSpec((SUBLANES, tc), lambda j, i: (0, j))
    return pl.pallas_call(
        body, name="scan_bwd", grid=(SSM_FLAT // tc, n_chunks),
        in_specs=[blk, blk, blk, blk, qblk, qblk], out_specs=[blk, blk, qblk, qblk],
        out_shape=[jax.ShapeDtypeStruct((t, SSM_FLAT), F32)] * 2 + [jax.ShapeDtypeStruct((SUBLANES, SSM_FLAT), F32)] * 2,
        scratch_shapes=[pltpu.VMEM((SUBLANES, tc), F32)] * 2,
        compiler_params=_params(("arbitrary", "arbitrary")),
    )(ds_re, ds_im, s_re, s_im, qb_re, qb_im)


def _block(ref, axis, size, index):
    idx = [slice(None)] * len(ref.shape)
    idx[axis] = pl.ds(pl.multiple_of(index * size, size), size)
    return ref.at[tuple(idx)]


def _all_gather(name, shards, axes):
    n = len(shards)
    sizes = [s.shape[a] for s, a in zip(shards, axes)]

    def body(*refs):
        ins, outs = refs[:n], refs[n:2 * n]
        send_sems, recv_sems, local_sems = refs[2 * n:]
        x, y, c = (lax.axis_index(a) for a in MESH_AXES)
        me, sibling = (x, y, c), (x, y, 1 - c)
        chips = [(1 - x, y), (x, 1 - y), (1 - x, 1 - y)]

        def rows(i, dev):
            return _block(outs[i], axes[i], sizes[i], 4 * dev[0] + 2 * dev[1] + dev[2])

        def copy(i, k, block, to, src=None):
            return pltpu.make_async_remote_copy(
                src_ref=rows(i, block) if src is None else src, dst_ref=rows(i, block),
                send_sem=send_sems.at[7 * i + k], recv_sem=recv_sems.at[7 * i + k],
                device_id=to, device_id_type=MESH)

        mine = [pltpu.make_async_copy(ins[i], rows(i, me), local_sems.at[i]) for i in range(n)]
        for cp in mine:
            cp.start()
        first = []
        for i in range(n):
            first.append(copy(i, 0, me, sibling, src=ins[i]))
            first += [copy(i, 1 + j, me, (*chip, c), src=ins[i]) for j, chip in enumerate(chips)]
        for cp in first:
            cp.start()
        passed = []
        for i in range(n):
            for j, chip in enumerate(chips):
                copy(i, 1 + j, (*chip, c), me).wait_recv()
                fwd = copy(i, 4 + j, (*chip, c), sibling)
                fwd.start()
                passed.append(fwd)
        for i in range(n):
            copy(i, 0, sibling, me).wait_recv()
            for j, chip in enumerate(chips):
                copy(i, 4 + j, (*chip, 1 - c), me).wait_recv()
        for cp in first + passed:
            cp.wait_send()
        for cp in mine:
            cp.wait()

    out_shape = []
    for s, a in zip(shards, axes):
        shape = list(s.shape)
        shape[a] *= N_DEV
        out_shape.append(jax.ShapeDtypeStruct(tuple(shape), s.dtype))
    any_spec = pl.BlockSpec(memory_space=pl.ANY)
    return pl.pallas_call(
        body, name=name, out_shape=out_shape,
        in_specs=[any_spec] * n, out_specs=[any_spec] * n,
        scratch_shapes=[pltpu.SemaphoreType.DMA((7 * n,)), pltpu.SemaphoreType.DMA((7 * n,)),
                        pltpu.SemaphoreType.DMA((n,))],
    )(*shards)


def _grad_exchange(name, fulls, axes):
    n = len(fulls)
    sizes = [f.shape[a] // N_DEV for f, a in zip(fulls, axes)]

    def body(*refs):
        ins, outs = refs[:n], refs[n:2 * n]
        send_sems, recv_sems = refs[2 * n:]
        x, y, c = (lax.axis_index(a) for a in MESH_AXES)
        copies = []
        for i in range(n):
            for m in range(1, N_DEV):
                px = 1 - x if m & 4 else x
                py = 1 - y if m & 2 else y
                pc = 1 - c if m & 1 else c
                copies.append(pltpu.make_async_remote_copy(
                    src_ref=_block(ins[i], axes[i], sizes[i], 4 * px + 2 * py + pc), dst_ref=outs[i].at[m - 1],
                    send_sem=send_sems.at[7 * i + m - 1], recv_sem=recv_sems.at[7 * i + m - 1],
                    device_id=(px, py, pc), device_id_type=MESH))
        for cp in copies:
            cp.start()
        for cp in copies:
            cp.wait_recv()
        for cp in copies:
            cp.wait_send()

    out_shape = []
    for f, a, size in zip(fulls, axes, sizes):
        shape = list(f.shape)
        shape[a] = size
        out_shape.append(jax.ShapeDtypeStruct((N_DEV - 1, *shape), f.dtype))
    any_spec = pl.BlockSpec(memory_space=pl.ANY)
    return pl.pallas_call(
        body, name=name, out_shape=out_shape,
        in_specs=[any_spec] * n, out_specs=[any_spec] * n,
        scratch_shapes=[pltpu.SemaphoreType.DMA((7 * n,)), pltpu.SemaphoreType.DMA((7 * n,))],
    )(*fulls)


def _adamw_update(w_ref, m_ref, v_ref, part_refs, g_ref, d_ref, nm_ref, nv_ref):
    c1 = 1.0 - ADAM_B1 ** ADAM_STEP
    c2 = 1.0 - ADAM_B2 ** ADAM_STEP
    g = None
    for p_ref in part_refs:
        stacked = len(p_ref.shape) > len(w_ref.shape)
        terms = [p_ref[s] for s in range(p_ref.shape[0])] if stacked else [p_ref[...]]
        for term in terms:
            term = term.astype(F32)
            g = term if g is None else g + term
    new_m = ADAM_B1 * m_ref[...] + (1.0 - ADAM_B1) * g
    new_v = ADAM_B2 * v_ref[...] + (1.0 - ADAM_B2) * (g * g)
    g_ref[...] = g
    nm_ref[...] = new_m
    nv_ref[...] = new_v
    d_ref[...] = -ADAM_LR * ((new_m / c1) / (jnp.sqrt(new_v / c2) + ADAM_EPS) + ADAM_WD * w_ref[...])


def _adamw_small(ws, ms, vs, stacks):
    n = len(ws)

    def body(*refs):
        ins, outs = refs[:4 * n], refs[4 * n:]
        for i in range(n):
            _adamw_update(ins[i], ins[n + i], ins[2 * n + i], [ins[3 * n + i]],
                          outs[i], outs[n + i], outs[2 * n + i], outs[3 * n + i])

    res = pl.pallas_call(
        body, name="adamw_small",
        out_shape=[jax.ShapeDtypeStruct(w.shape, F32) for w in ws] * 4,
        compiler_params=_params(None),
    )(*ws, *ms, *vs, *stacks)
    return res[:n], res[n:2 * n], res[2 * n:3 * n], res[3 * n:]


def _adamw(name, w, m, v, parts):
    r, c = w.shape
    tr = _tile(r, 256)
    n_parts = len(parts)

    def body(*refs):
        _adamw_update(refs[0], refs[1], refs[2], refs[3:3 + n_parts], *refs[3 + n_parts:])

    row = pl.BlockSpec((tr, c), lambda i: (i, 0))
    in_specs = [row, row, row]
    for p in parts:
        in_specs.append(row if p.ndim == 2 else pl.BlockSpec((p.shape[0], tr, c), lambda i: (0, i, 0)))
    return pl.pallas_call(
        body, name=name, grid=(r // tr,), in_specs=in_specs, out_specs=[row] * 4,
        out_shape=[jax.ShapeDtypeStruct((r, c), F32)] * 4,
        compiler_params=_params(("arbitrary",)),
    )(w, m, v, *parts)


SMALL = ("norm_gain", "pool_scale", "a_re", "a_im", "log_dt", "b_re", "b_im", "c_re", "c_im", "d_skip", "final_gain")
LARGE = ("w_in", "w_pool", "w_glu", "w_out", "w_ple", "w_ple_gate")
LARGE_AXIS = {"w_in": 1, "w_pool": 1, "w_glu": 1, "w_out": 0, "w_ple": 1, "w_ple_gate": 0}
WEIGHTS = ("norm_gain", "w_in", "w_pool", "pool_scale", "a_re", "a_im", "log_dt", "b_re", "b_im", "c_re", "c_im",
           "d_skip", "w_glu", "w_out", "w_ple", "w_ple_gate", "final_gain")


def kernel(x, p, norm_gain, w_in, w_pool, pool_scale, a_re, a_im, log_dt, b_re, b_im, c_re, c_im, d_skip, w_glu, w_out, w_ple, w_ple_gate, final_gain, loss_target, m_norm_gain, m_w_in, m_w_pool, m_pool_scale, m_a_re, m_a_im, m_log_dt, m_b_re, m_b_im, m_c_re, m_c_im, m_d_skip, m_w_glu, m_w_out, m_w_ple, m_w_ple_gate, m_final_gain, v_norm_gain, v_w_in, v_w_pool, v_pool_scale, v_a_re, v_a_im, v_log_dt, v_b_re, v_b_im, v_c_re, v_c_im, v_d_skip, v_w_glu, v_w_out, v_w_ple, v_w_ple_gate, v_final_gain):
    weights = dict(norm_gain=norm_gain, w_in=w_in, w_pool=w_pool, pool_scale=pool_scale, a_re=a_re, a_im=a_im,
                   log_dt=log_dt, b_re=b_re, b_im=b_im, c_re=c_re, c_im=c_im, d_skip=d_skip, w_glu=w_glu,
                   w_out=w_out, w_ple=w_ple, w_ple_gate=w_ple_gate, final_gain=final_gain)
    mom_m = dict(norm_gain=m_norm_gain, w_in=m_w_in, w_pool=m_w_pool, pool_scale=m_pool_scale, a_re=m_a_re,
                 a_im=m_a_im, log_dt=m_log_dt, b_re=m_b_re, b_im=m_b_im, c_re=m_c_re, c_im=m_c_im,
                 d_skip=m_d_skip, w_glu=m_w_glu, w_out=m_w_out, w_ple=m_w_ple, w_ple_gate=m_w_ple_gate,
                 final_gain=m_final_gain)
    mom_v = dict(norm_gain=v_norm_gain, w_in=v_w_in, w_pool=v_w_pool, pool_scale=v_pool_scale, a_re=v_a_re,
                 a_im=v_a_im, log_dt=v_log_dt, b_re=v_b_re, b_im=v_b_im, c_re=v_c_re, c_im=v_c_im,
                 d_skip=v_d_skip, w_glu=v_w_glu, w_out=v_w_out, w_ple=v_w_ple, w_ple_gate=v_w_ple_gate,
                 final_gain=v_final_gain)

    t = x.shape[1]
    xs = x.reshape(t, D_MODEL)
    ps = p.reshape(t, PLE_DIM)
    target = loss_target.reshape(t, D_MODEL)
    gain1 = norm_gain.reshape(1, D_MODEL)
    gain_f = final_gain.reshape(1, D_MODEL)
    scale_p = pool_scale.reshape(1, POOL_WIDTH)
    skip = d_skip.reshape(1, SSM_WIDTH)

    shard2d = {k: weights[k][0] for k in LARGE}
    gathered = _all_gather("weights_all_gather", [shard2d[k].astype(BF16) for k in LARGE],
                           [LARGE_AXIS[k] for k in LARGE])
    full = dict(zip(LARGE, gathered))

    ar, ai = a_re[0], a_im[0]
    ldt = log_dt.reshape(N_SSM_GROUPS, 1)
    br_t = jnp.transpose(b_re[0], (0, 2, 1))
    bi_t = jnp.transpose(b_im[0], (0, 2, 1))
    pw_re, pw_im, bb_re, bb_im = _ssm_params(ar, ai, ldt, br_t, bi_t)
    q_re = pw_re.reshape(SUBLANES, SSM_FLAT)
    q_im = pw_im.reshape(SUBLANES, SSM_FLAT)
    qb_re = q_re[::-1]
    qb_im = -q_im[::-1]
    wbd_re = _blockdiag_in(bb_re)
    wbd_im = _blockdiag_in(bb_im)
    cbd_re = _blockdiag_in(c_re[0])
    cbd_imneg = _blockdiag_in(-c_im[0])

    hn = _norm1_fwd(xs, gain1)
    proj = _mm_nn("in_proj", hn, full["w_in"], [F32], tk=2048)[0]
    pooled = _pool_fwd(proj)
    tm = _tile(t, 1024)
    mixed = _mm("pool_mix", [(pooled, (tm, POOL_GROUP), lambda i, j, s: (i, j),
                              full["w_pool"], (None, POOL_GROUP, POOL_GROUP), lambda i, j, s: (j, 0, 0))],
                DOT_NN, (t // tm, N_POOL_GROUPS, 1),
                [((t, POOL_WIDTH), F32, (tm, POOL_GROUP), lambda i, j, s: (i, j))], 1)[0]
    bu_re = _ssm_in_proj("ssm_in_proj_re", proj, wbd_re)
    bu_im = _ssm_in_proj("ssm_in_proj_im", proj, wbd_im)
    s_re, s_im = _scan_fwd(bu_re, bu_im, q_re, q_im)
    y, gel = _ssm_out_proj(s_re, s_im, cbd_re, cbd_imneg, proj, skip)
    hg = _mm_nn("glu_proj", gel, full["w_glu"], [F32])[0]
    cat = _gate_fwd(mixed, proj, hg, scale_p)

    def residual_epilogue(acc, ex, out_refs):
        h = acc + ex[0][...]
        out_refs[0][...] = h
        out_refs[1][...] = h.astype(BF16)

    h1, h1b = _mm_nn("out_proj", cat, full["w_out"], [F32, BF16], extras=[xs], epilogue=residual_epilogue)
    e = _mm_nn("ple_proj", ps, full["w_ple"], [F32])[0]
    q = _mm_nn("ple_gate_proj", h1b, full["w_ple_gate"], [F32], tk=2048)[0]
    de, dq, dh2, g_final_gain, loss_part = _final(h1, e, q, target, gain_f)
    loss = lax.psum(loss_part[0, 0], MESH_AXES)

    grads = {}
    grads["w_ple_gate"] = _mm_tn("ple_gate_wgrad", h1b, dq, BF16)
    grads["w_ple"] = _mm_tn("ple_wgrad", ps, de, BF16)
    dh1, dh1b = _mm_nt("ple_gate_dgrad", dq, full["w_ple_gate"], [F32, BF16], extras=[dh2],
                       epilogue=residual_epilogue)
    grads["w_out"] = _mm_tn("out_wgrad", cat, dh1b, BF16)
    dcat = _mm_nt("out_dgrad", dh1b, full["w_out"], [F32], tk=2048)[0]
    dmixed, dga, dgb, dhg, g_pool_scale = _gate_bwd(dcat, mixed, proj, hg, scale_p)

    tk = _tile(t, 1024)
    grads["w_pool"] = _mm("pool_wgrad", [(pooled, (tk, POOL_GROUP), lambda i, j, s: (s, i),
                                          dmixed, (tk, POOL_GROUP), lambda i, j, s: (s, i))],
                          DOT_TN, (N_POOL_GROUPS, 1, t // tk),
                          [((N_POOL_GROUPS, POOL_GROUP, POOL_GROUP), BF16, (None, POOL_GROUP, POOL_GROUP),
                            lambda i, j, s: (i, 0, 0))], t // tk)[0]
    dpooled = _mm("pool_dgrad", [(dmixed, (tm, POOL_GROUP), lambda i, j, s: (i, j),
                                  full["w_pool"], (None, POOL_GROUP, POOL_GROUP), lambda i, j, s: (j, 0, 0))],
                  DOT_NT, (t // tm, N_POOL_GROUPS, 1),
                  [((t, POOL_WIDTH), F32, (tm, POOL_GROUP), lambda i, j, s: (i, j))], 1)[0]
    dua = _pool_bwd(dpooled)

    grads["w_glu"] = _mm_tn("glu_wgrad", gel, dhg, BF16)

    def gelu_bwd_epilogue(acc, ex, out_refs):
        yv = ex[0][...]
        th = jnp.tanh(GELU_C * (yv + GELU_A * yv * yv * yv))
        dgelu = 0.5 * (1.0 + th) + 0.5 * yv * (1.0 - th * th) * GELU_C * (1.0 + 3.0 * GELU_A * yv * yv)
        out_refs[0][...] = acc * dgelu

    dy = _mm_nt("glu_dgrad", dhg, full["w_glu"], [F32], tk=2048, extras=[y], epilogue=gelu_bwd_epilogue)[0]
    ds_re = _ssm_dstate("ssm_dstate_re", dy, cbd_re)
    ds_im = _ssm_dstate("ssm_dstate_im", dy, cbd_imneg)
    g_cbd_re = _ssm_wgrad("ssm_c_re_wgrad", dy, 0, s_re)
    g_cbd_imneg = _ssm_wgrad("ssm_c_im_wgrad", dy, 0, s_im)
    z_re, z_im, gab_re8, gab_im8 = _scan_bwd(ds_re, ds_im, s_re, s_im, qb_re, qb_im)
    u_col0 = 2 * POOL_WIDTH // CHUNK_IN
    g_wbd_re = _ssm_wgrad("ssm_b_re_wgrad", proj, u_col0, z_re)
    g_wbd_im = _ssm_wgrad("ssm_b_im_wgrad", proj, u_col0, z_im)
    dub = _ssm_din(z_re, z_im, wbd_re, wbd_im, dy, skip)

    g_ab_re = jnp.sum(gab_re8, axis=0).reshape(N_SSM_GROUPS, SSM_STATE)
    g_ab_im = jnp.sum(gab_im8, axis=0).reshape(N_SSM_GROUPS, SSM_STATE)
    d_ar, d_ai, d_ldt, d_br_t, d_bi_t = _ssm_params_bwd(
        ar, ai, ldt, br_t, bi_t, g_ab_re, g_ab_im, _diag_in(g_wbd_re), _diag_in(g_wbd_im))

    dproj = jnp.concatenate([dua, dga, dub, dgb], axis=1)
    grads["w_in"] = _mm_tn("in_wgrad", hn, dproj, BF16)
    dhn = _mm_nt("in_dgrad", dproj, full["w_in"], [F32], tk=2048)[0]
    grad_x, g_norm_gain = _norm1_bwd(xs, dhn, dh1, gain1)
    g_d_skip = _skip_grad(dy, proj)

    def b_view(a):
        return jnp.transpose(a[0], (0, 2, 1))

    views = dict(norm_gain=lambda a: a, pool_scale=lambda a: a, a_re=lambda a: a[0], a_im=lambda a: a[0],
                 log_dt=lambda a: a, b_re=b_view, b_im=b_view, c_re=lambda a: a[0], c_im=lambda a: a[0],
                 d_skip=lambda a: a, final_gain=lambda a: a.reshape(1, D_MODEL))
    small_grads = dict(
        norm_gain=g_norm_gain, pool_scale=g_pool_scale, a_re=d_ar, a_im=d_ai, log_dt=d_ldt.reshape(1, N_SSM_GROUPS),
        b_re=d_br_t, b_im=d_bi_t, c_re=_diag_in(g_cbd_re), c_im=-_diag_in(g_cbd_imneg), d_skip=g_d_skip,
        final_gain=g_final_gain)
    stacks = _all_gather("small_grads_all_gather", [small_grads[k][None] for k in SMALL], [0] * len(SMALL))
    small_out = _adamw_small([views[k](weights[k]) for k in SMALL], [views[k](mom_m[k]) for k in SMALL],
                             [views[k](mom_v[k]) for k in SMALL], stacks)
    out_g, out_d, out_m, out_v = ({} for _ in range(4))
    for out, res in zip((out_g, out_d, out_m, out_v), small_out):
        for k, r in zip(SMALL, res):
            if k in ("b_re", "b_im"):
                r = jnp.transpose(r, (0, 2, 1))
            out[k] = r.reshape(weights[k].shape)

    landed = _grad_exchange("grads_exchange", [grads[k] for k in LARGE], [LARGE_AXIS[k] for k in LARGE])
    me = 4 * lax.axis_index("x") + 2 * lax.axis_index("y") + lax.axis_index("c")
    for k, land in zip(LARGE, landed):
        shard_shape = shard2d[k].shape
        size = shard_shape[LARGE_AXIS[k]]
        own = lax.dynamic_slice_in_dim(grads[k], me * size, size, axis=LARGE_AXIS[k])
        view = (-1, shard_shape[-1])
        res = _adamw("adamw_" + k, shard2d[k].reshape(view), mom_m[k][0].reshape(view), mom_v[k][0].reshape(view),
                     [own.reshape(view), land.reshape((N_DEV - 1,) + (math.prod(shard_shape[:-1]), shard_shape[-1]))])
        out_g[k], out_d[k], out_m[k], out_v[k] = (r.reshape(weights[k].shape) for r in res)

    return (loss, grad_x.reshape(x.shape), *[out_g[k] for k in WEIGHTS], *[out_d[k] for k in WEIGHTS],
            *[out_m[k] for k in WEIGHTS], *[out_v[k] for k in WEIGHTS])


def _skip_grad(dy, proj):
    t = dy.shape[0]
    tm = _tile(t, 512)

    def body(dy_ref, u_ref, o_ref):
        @pl.when(pl.program_id(0) == 0)
        def _():
            o_ref[...] = jnp.zeros_like(o_ref)

        o_ref[...] += jnp.sum(dy_ref[...] * u_ref[...], axis=0, keepdims=True)

    return pl.pallas_call(
        body, name="skip_grad", grid=(t // tm,),
        in_specs=[pl.BlockSpec((tm, SSM_WIDTH), lambda i: (i, 0)), pl.BlockSpec((tm, SSM_WIDTH), lambda i: (i, 2))],
        out_specs=pl.BlockSpec((1, SSM_WIDTH), lambda i: (0, 0)),
        out_shape=jax.ShapeDtypeStruct((1, SSM_WIDTH), F32),
        compiler_params=_params(("arbitrary",)),
    )(dy, proj)
```

```python
import functools
import math

import jax
import jax.numpy as jnp
from jax import lax
from jax.experimental import pallas as pl
from jax.experimental.pallas import tpu as pltpu

F32 = jnp.float32
BF16 = jnp.bfloat16
MESH = pl.DeviceIdType.MESH
MESH_AXES = ("x", "y", "c")
N_DEV = 8

D_MODEL = 2048
POOL_WIDTH = 1024
SSM_WIDTH = 1024
N_POOL_GROUPS = 4
POOL_GROUP = 256
SSM_GROUP = 16
N_SSM_GROUPS = 64
SSM_STATE = 64
SSM_FLAT = N_SSM_GROUPS * SSM_STATE
SSM_CHUNKS = 4
CHUNK_IN = SSM_WIDTH // SSM_CHUNKS
CHUNK_STATE = SSM_FLAT // SSM_CHUNKS
PLE_DIM = 256
EPS = 1e-6
A_RE_MAX = -1e-4
ADAM_LR = 0.001
ADAM_B1 = 0.9
ADAM_B2 = 0.999
ADAM_EPS = 1e-08
ADAM_WD = 0.01
ADAM_STEP = 10
GELU_C = math.sqrt(2.0 / math.pi)
GELU_A = 0.044715

SUBLANES = 8
LANES = 128
VMEM_LIMIT_BYTES = 48 * 1024 * 1024

DOT_NN = (((1,), (0,)), ((), ()))
DOT_NT = (((1,), (1,)), ((), ()))
DOT_TN = (((0,), (0,)), ((), ()))


def _tile(n, pref):
    return pref if n % pref == 0 else n


def _params(sem):
    return pltpu.CompilerParams(dimension_semantics=sem, vmem_limit_bytes=VMEM_LIMIT_BYTES)


def _sigmoid(v):
    return 1.0 / (1.0 + jnp.exp(-v))


def _silu_and_grad(v):
    s = _sigmoid(v)
    return v * s, s * (1.0 + v * (1.0 - s))


def _mm(name, pairs, dims, grid, outs, k_steps, extras=(), epilogue=None):
    n_pairs, n_ex, n_out = len(pairs), len(extras), len(outs)
    acc_shape = tuple(d for d in outs[0][2] if d is not None)
    if epilogue is None:
        def epilogue(acc, ex, out_refs):
            out_refs[0][...] = acc.astype(out_refs[0].dtype)

    def body(*refs):
        ab = refs[:2 * n_pairs]
        ex = refs[2 * n_pairs:2 * n_pairs + n_ex]
        out_refs = refs[2 * n_pairs + n_ex:2 * n_pairs + n_ex + n_out]
        acc = refs[-1]
        k = pl.program_id(2)

        @pl.when(k == 0)
        def _():
            acc[...] = jnp.zeros_like(acc)

        part = None
        for q in range(n_pairs):
            d = lax.dot_general(ab[2 * q][...].astype(BF16), ab[2 * q + 1][...].astype(BF16), dims,
                                preferred_element_type=F32)
            part = d if part is None else part + d
        acc[...] += part

        @pl.when(k == k_steps - 1)
        def _():
            epilogue(acc[...], ex, out_refs)

    in_specs, operands = [], []
    for a, a_blk, a_map, b, b_blk, b_map in pairs:
        in_specs += [pl.BlockSpec(a_blk, a_map), pl.BlockSpec(b_blk, b_map)]
        operands += [a, b]
    for e, e_blk, e_map in extras:
        in_specs.append(pl.BlockSpec(e_blk, e_map))
        operands.append(e)
    return pl.pallas_call(
        body, name=name, grid=grid, in_specs=in_specs,
        out_specs=[pl.BlockSpec(o[2], o[3]) for o in outs],
        out_shape=[jax.ShapeDtypeStruct(o[0], o[1]) for o in outs],
        scratch_shapes=[pltpu.VMEM(acc_shape, F32)],
        compiler_params=_params(("arbitrary", "arbitrary", "arbitrary")),
    )(*operands)


def _after(tokens):
    return [(tok, tok.shape, lambda i, j, s: (0, 0)) for tok in tokens]


def _mm_nn(name, a, b, out_dtypes, tm=1024, tn=1024, tk=1024, a_col0=0, extras=(), epilogue=None, after=()):
    m, n = a.shape[0], b.shape[1]
    k = b.shape[0]
    tm, tn, tk = _tile(m, tm), _tile(n, tn), _tile(k, tk)
    outs = [((m, n), dt, (tm, tn), lambda i, j, s: (i, j)) for dt in out_dtypes]
    ex = [(e, (tm, tn), lambda i, j, s: (i, j)) for e in extras] + _after(after)
    return _mm(name, [(a, (tm, tk), lambda i, j, s: (i, a_col0 + s), b, (tk, tn), lambda i, j, s: (s, j))],
               DOT_NN, (m // tm, n // tn, k // tk), outs, k // tk, ex, epilogue)


def _mm_nt(name, a, b, out_dtypes, tm=1024, tn=1024, tk=1024, extras=(), epilogue=None, after=()):
    m, kk = a.shape
    n = b.shape[0]
    tm, tn, tk = _tile(m, tm), _tile(n, tn), _tile(kk, tk)
    outs = [((m, n), dt, (tm, tn), lambda i, j, s: (i, j)) for dt in out_dtypes]
    ex = [(e, (tm, tn), lambda i, j, s: (i, j)) for e in extras] + _after(after)
    return _mm(name, [(a, (tm, tk), lambda i, j, s: (i, s), b, (tn, tk), lambda i, j, s: (j, s))],
               DOT_NT, (m // tm, n // tn, kk // tk), outs, kk // tk, ex, epilogue)


def _mm_tn(name, a, b, out_dtype, tm=512, tn=2048, tk=1024):
    m, kk = a.shape
    n = b.shape[1]
    tm, tn, tk = _tile(kk, tm), _tile(n, tn), _tile(m, tk)
    outs = [((kk, n), out_dtype, (tm, tn), lambda i, j, s: (i, j))]
    return _mm(name, [(a, (tk, tm), lambda i, j, s: (s, i), b, (tk, tn), lambda i, j, s: (s, j))],
               DOT_TN, (kk // tm, n // tn, m // tk), outs, m // tk)[0]


def _norm1_fwd(x, gain):
    t = x.shape[0]
    tm = _tile(t, 512)

    def body(x_ref, g_ref, hn_ref):
        xv = x_ref[...]
        r = lax.rsqrt(jnp.mean(xv * xv, axis=-1, keepdims=True) + EPS)
        hn_ref[...] = (xv * r * g_ref[...]).astype(BF16)

    return pl.pallas_call(
        body, name="norm1_fwd", grid=(t // tm,),
        in_specs=[pl.BlockSpec((tm, D_MODEL), lambda i: (i, 0)), pl.BlockSpec((1, D_MODEL), lambda i: (0, 0))],
        out_specs=pl.BlockSpec((tm, D_MODEL), lambda i: (i, 0)),
        out_shape=jax.ShapeDtypeStruct((t, D_MODEL), BF16),
        compiler_params=_params(("arbitrary",)),
    )(x, gain)


def _norm1_bwd(x, dhn, dh1, gain):
    t = x.shape[0]
    tm = _tile(t, 512)

    def body(x_ref, dhn_ref, dh1_ref, g_ref, dx_ref, gg_ref):
        @pl.when(pl.program_id(0) == 0)
        def _():
            gg_ref[...] = jnp.zeros_like(gg_ref)

        xv = x_ref[...]
        r = lax.rsqrt(jnp.mean(xv * xv, axis=-1, keepdims=True) + EPS)
        xh = xv * r
        dhn_v = dhn_ref[...]
        gg_ref[...] += jnp.sum(dhn_v * xh, axis=0, keepdims=True)
        dxh = dhn_v * g_ref[...]
        dx_ref[...] = dh1_ref[...] + r * (dxh - xh * jnp.mean(dxh * xh, axis=-1, keepdims=True))

    row = pl.BlockSpec((tm, D_MODEL), lambda i: (i, 0))
    vec = pl.BlockSpec((1, D_MODEL), lambda i: (0, 0))
    return pl.pallas_call(
        body, name="norm1_bwd", grid=(t // tm,),
        in_specs=[row, row, row, vec], out_specs=[row, vec],
        out_shape=[jax.ShapeDtypeStruct((t, D_MODEL), F32), jax.ShapeDtypeStruct((1, D_MODEL), F32)],
        compiler_params=_params(("arbitrary",)),
    )(x, dhn, dh1, gain)


def _pool_counts(t, width, group):
    row = lax.broadcasted_iota(jnp.int32, (t, width), 0)
    window = jnp.left_shift(jnp.int32(2), group)
    return row, jnp.minimum(row + 1, window).astype(F32)


def _select_window(group, s2, s4, s8, s16):
    return jnp.where(group == 0, s2, jnp.where(group == 1, s4, jnp.where(group == 2, s8, s16)))


def _pool_fwd(proj):
    t = proj.shape[0]
    tc = LANES

    def body(u_ref, o_ref):
        group = pl.program_id(0) // (POOL_GROUP // tc)
        v = u_ref[...]
        row, count = _pool_counts(t, tc, group)

        def down(a, j):
            return jnp.where(row >= j, pltpu.roll(a, j, 0), 0.0)

        s2 = v + down(v, 1)
        s4 = s2 + down(s2, 2)
        s8 = s4 + down(s4, 4)
        s16 = s8 + down(s8, 8)
        o_ref[...] = (_select_window(group, s2, s4, s8, s16) / count - v).astype(BF16)

    return pl.pallas_call(
        body, name="pool_fwd", grid=(POOL_WIDTH // tc,),
        in_specs=[pl.BlockSpec((t, tc), lambda j: (0, j))],
        out_specs=pl.BlockSpec((t, tc), lambda j: (0, j)),
        out_shape=jax.ShapeDtypeStruct((t, POOL_WIDTH), BF16),
        compiler_params=_params(("arbitrary",)),
    )(proj)


def _pool_bwd(dpooled):
    t = dpooled.shape[0]
    tc = LANES

    def body(d_ref, o_ref):
        group = pl.program_id(0) // (POOL_GROUP // tc)
        dp = d_ref[...]
        row, count = _pool_counts(t, tc, group)
        r = dp / count

        def up(a, j):
            return jnp.where(row < t - j, pltpu.roll(a, t - j, 0), 0.0)

        s2 = r + up(r, 1)
        s4 = s2 + up(s2, 2)
        s8 = s4 + up(s4, 4)
        s16 = s8 + up(s8, 8)
        o_ref[...] = (_select_window(group, s2, s4, s8, s16) - dp).astype(BF16)

    return pl.pallas_call(
        body, name="pool_bwd", grid=(POOL_WIDTH // tc,),
        in_specs=[pl.BlockSpec((t, tc), lambda j: (0, j))],
        out_specs=pl.BlockSpec((t, tc), lambda j: (0, j)),
        out_shape=jax.ShapeDtypeStruct((t, POOL_WIDTH), BF16),
        compiler_params=_params(("arbitrary",)),
    )(dpooled)


def _gate_fwd(mixed, proj, hg, pool_scale):
    t = mixed.shape[0]
    tm = _tile(t, 512)

    def body(mx_ref, ga_ref, gb_ref, hg_ref, ps_ref, cat_ref):
        silu_a, _ = _silu_and_grad(ga_ref[...])
        cat_ref[:, :POOL_WIDTH] = (mx_ref[...] * ps_ref[...] * silu_a).astype(BF16)
        silu_b, _ = _silu_and_grad(gb_ref[...])
        sb = hg_ref[:, :SSM_WIDTH] * _sigmoid(hg_ref[:, SSM_WIDTH:])
        cat_ref[:, POOL_WIDTH:] = (sb * silu_b).astype(BF16)

    return pl.pallas_call(
        body, name="gate_fwd", grid=(t // tm,),
        in_specs=[pl.BlockSpec((tm, POOL_WIDTH), lambda i: (i, 0)),
                  pl.BlockSpec((tm, POOL_WIDTH), lambda i: (i, 1)),
                  pl.BlockSpec((tm, SSM_WIDTH), lambda i: (i, 3)),
                  pl.BlockSpec((tm, 2 * SSM_WIDTH), lambda i: (i, 0)),
                  pl.BlockSpec((1, POOL_WIDTH), lambda i: (0, 0))],
        out_specs=pl.BlockSpec((tm, D_MODEL), lambda i: (i, 0)),
        out_shape=jax.ShapeDtypeStruct((t, D_MODEL), BF16),
        compiler_params=_params(("arbitrary",)),
    )(mixed, proj, proj, hg, pool_scale)


def _gate_bwd(dcat, mixed, proj, hg, pool_scale):
    t = mixed.shape[0]
    tm = _tile(t, 512)

    def body(dc_ref, mx_ref, ga_ref, gb_ref, hg_ref, ps_ref, dmx_ref, dga_ref, dgb_ref, dhg_ref, gps_ref):
        @pl.when(pl.program_id(0) == 0)
        def _():
            gps_ref[...] = jnp.zeros_like(gps_ref)

        ps = ps_ref[...]
        mx = mx_ref[...]
        dya = dc_ref[:, :POOL_WIDTH]
        silu_a, dsilu_a = _silu_and_grad(ga_ref[...])
        dpa = dya * silu_a
        gps_ref[...] += jnp.sum(dpa * mx, axis=0, keepdims=True)
        dmx_ref[...] = (dpa * ps).astype(BF16)
        dga_ref[...] = (dya * mx * ps * dsilu_a).astype(BF16)

        dyb = dc_ref[:, POOL_WIDTH:]
        silu_b, dsilu_b = _silu_and_grad(gb_ref[...])
        h_a = hg_ref[:, :SSM_WIDTH]
        sg = _sigmoid(hg_ref[:, SSM_WIDTH:])
        dsb = dyb * silu_b
        dgb_ref[...] = (dyb * h_a * sg * dsilu_b).astype(BF16)
        dhg_ref[:, :SSM_WIDTH] = (dsb * sg).astype(BF16)
        dhg_ref[:, SSM_WIDTH:] = (dsb * h_a * sg * (1.0 - sg)).astype(BF16)

    half = pl.BlockSpec((tm, POOL_WIDTH), lambda i: (i, 0))
    full = pl.BlockSpec((tm, D_MODEL), lambda i: (i, 0))
    vec = pl.BlockSpec((1, POOL_WIDTH), lambda i: (0, 0))
    return pl.pallas_call(
        body, name="gate_bwd", grid=(t // tm,),
        in_specs=[full, half,
                  pl.BlockSpec((tm, POOL_WIDTH), lambda i: (i, 1)),
                  pl.BlockSpec((tm, SSM_WIDTH), lambda i: (i, 3)),
                  full, vec],
        out_specs=[half, half, half, full, vec],
        out_shape=[jax.ShapeDtypeStruct((t, POOL_WIDTH), BF16), jax.ShapeDtypeStruct((t, POOL_WIDTH), BF16),
                   jax.ShapeDtypeStruct((t, SSM_WIDTH), BF16), jax.ShapeDtypeStruct((t, 2 * SSM_WIDTH), BF16),
                   jax.ShapeDtypeStruct((1, POOL_WIDTH), F32)],
        compiler_params=_params(("arbitrary",)),
    )(dcat, mixed, proj, proj, hg, pool_scale)


def _final(h1, e, q, target, gain):
    t = h1.shape[0]
    tm = _tile(t, 256)

    def body(h1_ref, e_ref, q_ref, tg_ref, g_ref, de_ref, dq_ref, dh2_ref, gg_ref, loss_ref):
        @pl.when(pl.program_id(0) == 0)
        def _():
            gg_ref[...] = jnp.zeros_like(gg_ref)
            loss_ref[...] = jnp.zeros_like(loss_ref)

        ev = e_ref[...]
        sg = _sigmoid(q_ref[...])
        h2 = h1_ref[...] + ev * sg
        r = lax.rsqrt(jnp.mean(h2 * h2, axis=-1, keepdims=True) + EPS)
        n = h2 * r
        gain_v = g_ref[...]
        diff = n * gain_v - tg_ref[...]
        row_loss = jnp.sum(diff * diff, axis=-1, keepdims=True)
        loss_ref[...] += (0.5 / D_MODEL) * jnp.sum(row_loss, axis=0, keepdims=True)
        dout = diff * (1.0 / D_MODEL)
        gg_ref[...] += jnp.sum(dout * n, axis=0, keepdims=True)
        dn = dout * gain_v
        dh2 = r * (dn - n * jnp.mean(dn * n, axis=-1, keepdims=True))
        dh2_ref[...] = dh2
        de_ref[...] = (dh2 * sg).astype(BF16)
        dq_ref[...] = (dh2 * ev * sg * (1.0 - sg)).astype(BF16)

    row = pl.BlockSpec((tm, D_MODEL), lambda i: (i, 0))
    vec = pl.BlockSpec((1, D_MODEL), lambda i: (0, 0))
    return pl.pallas_call(
        body, name="final_norm_loss", grid=(t // tm,),
        in_specs=[row, row, row, row, vec],
        out_specs=[row, row, row, vec, pl.BlockSpec((1, 1), lambda i: (0, 0))],
        out_shape=[jax.ShapeDtypeStruct((t, D_MODEL), BF16), jax.ShapeDtypeStruct((t, D_MODEL), BF16),
                   jax.ShapeDtypeStruct((t, D_MODEL), F32), jax.ShapeDtypeStruct((1, D_MODEL), F32),
                   jax.ShapeDtypeStruct((1, 1), F32)],
        compiler_params=_params(("arbitrary",)),
    )(h1, e, q, target, gain)


def _zoh(a_re, a_im, log_dt, b_re_t, b_im_t):
    lam_re = jnp.minimum(a_re, A_RE_MAX)
    lam_im = a_im
    dt = jnp.exp(log_dt)
    mag = jnp.exp(lam_re * dt)
    ang = lam_im * dt
    ab_re = mag * jnp.cos(ang)
    ab_im = mag * jnp.sin(ang)
    den = lam_re * lam_re + lam_im * lam_im
    n_re = ab_re - 1.0
    n_im = ab_im
    q_re = (n_re * lam_re + n_im * lam_im) / den
    q_im = (n_im * lam_re - n_re * lam_im) / den
    bb_re = q_re[:, None, :] * b_re_t - q_im[:, None, :] * b_im_t
    bb_im = q_re[:, None, :] * b_im_t + q_im[:, None, :] * b_re_t
    return ab_re, ab_im, bb_re, bb_im


def _ssm_params(a_re, a_im, log_dt, b_re_t, b_im_t):
    g, n = a_re.shape

    def body(are_ref, aim_ref, dt_ref, bre_ref, bim_ref, pre_ref, pim_ref, bbre_ref, bbim_ref):
        ab_re, ab_im, bb_re, bb_im = _zoh(are_ref[...], aim_ref[...], dt_ref[...], bre_ref[...], bim_ref[...])
        bbre_ref[...] = bb_re
        bbim_ref[...] = bb_im
        p_re, p_im = ab_re, ab_im
        for r in range(SUBLANES):
            pre_ref[r] = p_re
            pim_ref[r] = p_im
            p_re, p_im = p_re * ab_re - p_im * ab_im, p_re * ab_im + p_im * ab_re

    return pl.pallas_call(
        body, name="ssm_params",
        out_shape=[jax.ShapeDtypeStruct((SUBLANES, g, n), F32), jax.ShapeDtypeStruct((SUBLANES, g, n), F32),
                   jax.ShapeDtypeStruct(b_re_t.shape, F32), jax.ShapeDtypeStruct(b_re_t.shape, F32)],
        compiler_params=_params(None),
    )(a_re, a_im, log_dt, b_re_t, b_im_t)


def _ssm_params_bwd(a_re, a_im, log_dt, b_re_t, b_im_t, g_ab_re, g_ab_im, g_bb_re, g_bb_im):
    def body(are_ref, aim_ref, dt_ref, bre_ref, bim_ref, gar_ref, gai_ref, gbr_ref, gbi_ref,
             o_are, o_aim, o_dt, o_bre, o_bim):
        _, vjp = jax.vjp(_zoh, are_ref[...], aim_ref[...], dt_ref[...], bre_ref[...], bim_ref[...])
        d_are, d_aim, d_dt, d_bre, d_bim = vjp((gar_ref[...], gai_ref[...], gbr_ref[...], gbi_ref[...]))
        o_are[...] = d_are
        o_aim[...] = d_aim
        o_dt[...] = d_dt
        o_bre[...] = d_bre
        o_bim[...] = d_bim

    ins = (a_re, a_im, log_dt, b_re_t, b_im_t)
    return pl.pallas_call(
        body, name="ssm_params_bwd",
        out_shape=[jax.ShapeDtypeStruct(v.shape, F32) for v in ins],
        compiler_params=_params(None),
    )(*ins, g_ab_re, g_ab_im, g_bb_re, g_bb_im)


GROUPS_PER_CHUNK = N_SSM_GROUPS // SSM_CHUNKS


def _diag_mask(rows_per_group, cols_per_group):
    shape = (GROUPS_PER_CHUNK * rows_per_group, GROUPS_PER_CHUNK * cols_per_group)
    r = lax.broadcasted_iota(jnp.int32, shape, 0) // rows_per_group
    c = lax.broadcasted_iota(jnp.int32, shape, 1) // cols_per_group
    return (r == c)[None]


def _blockdiag_in(w):
    tiled = jnp.tile(w.reshape(SSM_CHUNKS, CHUNK_IN, SSM_STATE), (1, 1, GROUPS_PER_CHUNK))
    return jnp.where(_diag_mask(SSM_GROUP, SSM_STATE), tiled, 0.0).astype(BF16)


def _diag_in(g):
    kept = jnp.where(_diag_mask(SSM_GROUP, SSM_STATE), g, 0.0)
    return kept.reshape(SSM_CHUNKS, CHUNK_IN, GROUPS_PER_CHUNK, SSM_STATE).sum(axis=2).reshape(
        N_SSM_GROUPS, SSM_GROUP, SSM_STATE)


def _ssm_in_proj(name, proj, wbd, tm=1024):
    t = proj.shape[0]
    tm = _tile(t, tm)
    col0 = 2 * POOL_WIDTH // CHUNK_IN
    return _mm(name, [(proj, (tm, CHUNK_IN), lambda i, j, s: (i, col0 + j),
                       wbd, (None, CHUNK_IN, CHUNK_STATE), lambda i, j, s: (j, 0, 0))],
               DOT_NN, (t // tm, SSM_CHUNKS, 1),
               [((t, SSM_FLAT), F32, (tm, CHUNK_STATE), lambda i, j, s: (i, j))], 1)[0]


def _ssm_out_proj(s_re, s_im, cbd_re, cbd_imneg, proj, d_skip, tm=1024):
    t = s_re.shape[0]
    tm = _tile(t, tm)
    col0 = 2 * POOL_WIDTH // CHUNK_IN

    def epilogue(acc, ex, out_refs):
        y = acc + ex[1][...] * ex[0][...]
        out_refs[0][...] = y
        out_refs[1][...] = (0.5 * y * (1.0 + jnp.tanh(GELU_C * (y + GELU_A * y * y * y)))).astype(BF16)

    a_map = lambda i, j, s: (i, j)
    b_map = lambda i, j, s: (j, 0, 0)
    o_map = lambda i, j, s: (i, j)
    return _mm("ssm_out_proj",
               [(s_re, (tm, CHUNK_STATE), a_map, cbd_re, (None, CHUNK_IN, CHUNK_STATE), b_map),
                (s_im, (tm, CHUNK_STATE), a_map, cbd_imneg, (None, CHUNK_IN, CHUNK_STATE), b_map)],
               DOT_NT, (t // tm, SSM_CHUNKS, 1),
               [((t, SSM_WIDTH), F32, (tm, CHUNK_IN), o_map), ((t, SSM_WIDTH), BF16, (tm, CHUNK_IN), o_map)], 1,
               [(proj, (tm, CHUNK_IN), lambda i, j, s: (i, col0 + j)), (d_skip, (1, CHUNK_IN), lambda i, j, s: (0, j))],
               epilogue)


def _ssm_dstate(name, dy, cbd, tm=1024):
    t = dy.shape[0]
    tm = _tile(t, tm)
    return _mm(name, [(dy, (tm, CHUNK_IN), lambda i, j, s: (i, j),
                       cbd, (None, CHUNK_IN, CHUNK_STATE), lambda i, j, s: (j, 0, 0))],
               DOT_NN, (t // tm, SSM_CHUNKS, 1),
               [((t, SSM_FLAT), F32, (tm, CHUNK_STATE), lambda i, j, s: (i, j))], 1)[0]


def _ssm_wgrad(name, a, col0, z, tk=1024):
    t = z.shape[0]
    tk = _tile(t, tk)
    return _mm(name, [(a, (tk, CHUNK_IN), lambda i, j, k: (k, col0 + i), z, (tk, CHUNK_STATE), lambda i, j, k: (k, i))],
               DOT_TN, (SSM_CHUNKS, 1, t // tk),
               [((SSM_CHUNKS, CHUNK_IN, CHUNK_STATE), F32, (None, CHUNK_IN, CHUNK_STATE), lambda i, j, k: (i, 0, 0))],
               t // tk)[0]


def _ssm_din(z_re, z_im, wbd_re, wbd_im, dy, d_skip, tm=1024):
    t = z_re.shape[0]
    tm = _tile(t, tm)

    def epilogue(acc, ex, out_refs):
        out_refs[0][...] = (acc + ex[1][...] * ex[0][...]).astype(BF16)

    a_map = lambda i, j, s: (i, j)
    b_map = lambda i, j, s: (j, 0, 0)
    return _mm("ssm_din",
               [(z_re, (tm, CHUNK_STATE), a_map, wbd_re, (None, CHUNK_IN, CHUNK_STATE), b_map),
                (z_im, (tm, CHUNK_STATE), a_map, wbd_im, (None, CHUNK_IN, CHUNK_STATE), b_map)],
               DOT_NT, (t // tm, SSM_CHUNKS, 1),
               [((t, SSM_WIDTH), BF16, (tm, CHUNK_IN), lambda i, j, s: (i, j))], 1,
               [(dy, (tm, CHUNK_IN), lambda i, j, s: (i, j)), (d_skip, (1, CHUNK_IN), lambda i, j, s: (0, j))],
               epilogue)[0]


SCAN_COLS = 256
SCAN_ROWS = 1024
SCAN_UNROLL = 2


def _unrolled(tile):
    def body(i, carry):
        for u in range(SCAN_UNROLL):
            carry = tile(i * SCAN_UNROLL + u, carry)
        return carry
    return body


def _masked_powers(q_re, q_im, row, forward):
    out = []
    for k in (1, 2, 4):
        src = k - 1 if forward else SUBLANES - k
        mask = (row >= k) if forward else (row < SUBLANES - k)
        p_re = jnp.broadcast_to(q_re[src:src + 1, :], q_re.shape)
        p_im = jnp.broadcast_to(q_im[src:src + 1, :], q_im.shape)
        out.append((jnp.where(mask, p_re, 0.0), jnp.where(mask, p_im, 0.0), k if forward else SUBLANES - k))
    return out


def _tile_scan(x_re, x_im, powers):
    for p_re, p_im, shift in powers:
        r_re = pltpu.roll(x_re, shift, 0)
        r_im = pltpu.roll(x_im, shift, 0)
        x_re, x_im = x_re + p_re * r_re - p_im * r_im, x_im + p_re * r_im + p_im * r_re
    return x_re, x_im


def _scan_fwd(bu_re, bu_im, q_re, q_im):
    t = bu_re.shape[0]
    tc, tt = SCAN_COLS, _tile(t, SCAN_ROWS)
    n_tiles = tt // SUBLANES

    def body(bre_ref, bim_ref, qre_ref, qim_ref, sre_ref, sim_ref, cre_ref, cim_ref):
        @pl.when(pl.program_id(1) == 0)
        def _():
            cre_ref[...] = jnp.zeros_like(cre_ref)
            cim_ref[...] = jnp.zeros_like(cim_ref)

        q_re_v, q_im_v = qre_ref[...], qim_ref[...]
        row = lax.broadcasted_iota(jnp.int32, (SUBLANES, tc), 0)
        powers = _masked_powers(q_re_v, q_im_v, row, True)

        def tile(i, carry):
            c_re, c_im = carry
            rows = pl.ds(pl.multiple_of(i * SUBLANES, SUBLANES), SUBLANES)
            x_re, x_im = _tile_scan(bre_ref[rows, :], bim_ref[rows, :], powers)
            s_re = x_re + q_re_v * c_re - q_im_v * c_im
            s_im = x_im + q_re_v * c_im + q_im_v * c_re
            sre_ref[rows, :] = s_re
            sim_ref[rows, :] = s_im
            last = SUBLANES - 1
            return (jnp.broadcast_to(s_re[last:, :], s_re.shape), jnp.broadcast_to(s_im[last:, :], s_im.shape))

        c_re, c_im = lax.fori_loop(0, n_tiles // SCAN_UNROLL, _unrolled(tile), (cre_ref[...], cim_ref[...]))
        cre_ref[...] = c_re
        cim_ref[...] = c_im

    blk = pl.BlockSpec((tt, tc), lambda j, i: (i, j))
    qblk = pl.BlockSpec((SUBLANES, tc), lambda j, i: (0, j))
    return pl.pallas_call(
        body, name="scan_fwd", grid=(SSM_FLAT // tc, t // tt),
        in_specs=[blk, blk, qblk, qblk], out_specs=[blk, blk],
        out_shape=[jax.ShapeDtypeStruct((t, SSM_FLAT), F32)] * 2,
        scratch_shapes=[pltpu.VMEM((SUBLANES, tc), F32)] * 2,
        compiler_params=_params(("arbitrary", "arbitrary")),
    )(bu_re, bu_im, q_re, q_im)


def _scan_bwd(ds_re, ds_im, s_re, s_im, qb_re, qb_im):
    t = ds_re.shape[0]
    tc, tt = SCAN_COLS, _tile(t, SCAN_ROWS)
    n_tiles = tt // SUBLANES
    n_chunks = t // tt

    def body(dre_ref, dim_ref, sre_ref, sim_ref, qre_ref, qim_ref, zre_ref, zim_ref, gre_ref, gim_ref,
             cre_ref, cim_ref):
        @pl.when(pl.program_id(1) == 0)
        def _():
            cre_ref[...] = jnp.zeros_like(cre_ref)
            cim_ref[...] = jnp.zeros_like(cim_ref)
            gre_ref[...] = jnp.zeros_like(gre_ref)
            gim_ref[...] = jnp.zeros_like(gim_ref)

        q_re_v, q_im_v = qre_ref[...], qim_ref[...]
        row = lax.broadcasted_iota(jnp.int32, (SUBLANES, tc), 0)
        powers = _masked_powers(q_re_v, q_im_v, row, False)
        is_last = row == SUBLANES - 1

        def tile(n, carry):
            c_re, c_im, g_re, g_im = carry
            i = n_tiles - 1 - n
            rows = pl.ds(pl.multiple_of(i * SUBLANES, SUBLANES), SUBLANES)
            x_re, x_im = _tile_scan(dre_ref[rows, :], dim_ref[rows, :], powers)
            z_re = x_re + q_re_v * c_re - q_im_v * c_im
            z_im = x_im + q_re_v * c_im + q_im_v * c_re
            zre_ref[rows, :] = z_re
            zim_ref[rows, :] = z_im
            zn_re = jnp.where(is_last, c_re, pltpu.roll(z_re, SUBLANES - 1, 0))
            zn_im = jnp.where(is_last, c_im, pltpu.roll(z_im, SUBLANES - 1, 0))
            s_re_v, s_im_v = sre_ref[rows, :], sim_ref[rows, :]
            g_re = g_re + zn_re * s_re_v + zn_im * s_im_v
            g_im = g_im + zn_im * s_re_v - zn_re * s_im_v
            return (jnp.broadcast_to(z_re[:1, :], z_re.shape), jnp.broadcast_to(z_im[:1, :], z_im.shape), g_re, g_im)

        c_re, c_im, g_re, g_im = lax.fori_loop(
            0, n_tiles // SCAN_UNROLL, _unrolled(tile), (cre_ref[...], cim_ref[...], gre_ref[...], gim_ref[...]))
        cre_ref[...] = c_re
        cim_ref[...] = c_im
        gre_ref[...] = g_re
        gim_ref[...] = g_im

    blk = pl.BlockSpec((tt, tc), lambda j, i: (n_chunks - 1 - i, j))
    qblk = pl.BlockSpec((SUBLANES, tc), lambda j, i: (0, j))
    return pl.pallas_call(
        body, name="scan_bwd", grid=(SSM_FLAT // tc, n_chunks),
        in_specs=[blk, blk, blk, blk, qblk, qblk], out_specs=[blk, blk, qblk, qblk],
        out_shape=[jax.ShapeDtypeStruct((t, SSM_FLAT), F32)] * 2 + [jax.ShapeDtypeStruct((SUBLANES, SSM_FLAT), F32)] * 2,
        scratch_shapes=[pltpu.VMEM((SUBLANES, tc), F32)] * 2,
        compiler_params=_params(("arbitrary", "arbitrary")),
    )(ds_re, ds_im, s_re, s_im, qb_re, qb_im)


def _block(ref, axis, size, index):
    idx = [slice(None)] * len(ref.shape)
    idx[axis] = pl.ds(pl.multiple_of(index * size, size), size)
    return ref.at[tuple(idx)]


def _all_gather(name, shards, axes):
    n = len(shards)
    sizes = [s.shape[a] for s, a in zip(shards, axes)]

    def body(*refs):
        ins, outs = refs[:n], refs[n:2 * n]
        send_sems, recv_sems, local_sems = refs[2 * n:]
        x, y, c = (lax.axis_index(a) for a in MESH_AXES)
        me, sibling = (x, y, c), (x, y, 1 - c)
        chips = [(1 - x, y), (x, 1 - y), (1 - x, 1 - y)]

        def rows(i, dev):
            return _block(outs[i], axes[i], sizes[i], 4 * dev[0] + 2 * dev[1] + dev[2])

        def copy(i, k, block, to, src=None):
            return pltpu.make_async_remote_copy(
                src_ref=rows(i, block) if src is None else src, dst_ref=rows(i, block),
                send_sem=send_sems.at[7 * i + k], recv_sem=recv_sems.at[7 * i + k],
                device_id=to, device_id_type=MESH)

        mine = [pltpu.make_async_copy(ins[i], rows(i, me), local_sems.at[i]) for i in range(n)]
        for cp in mine:
            cp.start()
        first = []
        for i in range(n):
            first.append(copy(i, 0, me, sibling, src=ins[i]))
            first += [copy(i, 1 + j, me, (*chip, c), src=ins[i]) for j, chip in enumerate(chips)]
        for cp in first:
            cp.start()
        passed = []
        for i in range(n):
            for j, chip in enumerate(chips):
                copy(i, 1 + j, (*chip, c), me).wait_recv()
                fwd = copy(i, 4 + j, (*chip, c), sibling)
                fwd.start()
                passed.append(fwd)
        for i in range(n):
            copy(i, 0, sibling, me).wait_recv()
            for j, chip in enumerate(chips):
                copy(i, 4 + j, (*chip, 1 - c), me).wait_recv()
        for cp in first + passed:
            cp.wait_send()
        for cp in mine:
            cp.wait()

    out_shape = []
    for s, a in zip(shards, axes):
        shape = list(s.shape)
        shape[a] *= N_DEV
        out_shape.append(jax.ShapeDtypeStruct(tuple(shape), s.dtype))
    any_spec = pl.BlockSpec(memory_space=pl.ANY)
    return pl.pallas_call(
        body, name=name, out_shape=out_shape,
        in_specs=[any_spec] * n, out_specs=[any_spec] * n,
        scratch_shapes=[pltpu.SemaphoreType.DMA((7 * n,)), pltpu.SemaphoreType.DMA((7 * n,)),
                        pltpu.SemaphoreType.DMA((n,))],
    )(*shards)


HBM_SPEC = pl.BlockSpec(memory_space=pltpu.HBM)
SEM_SPEC = pl.BlockSpec(memory_space=pltpu.SEMAPHORE)
ANY_SPEC = pl.BlockSpec(memory_space=pl.ANY)
SPLIT_PARAMS = pltpu.CompilerParams(has_side_effects=pltpu.SideEffectType.DATAFLOW_SIDE_EFFECTING)
N_PEERS = N_DEV - 1
TOKEN = jax.ShapeDtypeStruct((SUBLANES, LANES), F32)
VMEM_SPEC = pl.BlockSpec(memory_space=pltpu.VMEM)


def _in_hbm(arrays):
    return [pltpu.with_memory_space_constraint(a, pltpu.HBM) for a in arrays]


def _peer(m):
    x, y, c = (lax.axis_index(a) for a in MESH_AXES)
    px = 1 - x if m & 4 else x
    py = 1 - y if m & 2 else y
    pc = 1 - c if m & 1 else c
    return (px, py, pc), 4 * px + 2 * py + pc


def _my_index():
    x, y, c = (lax.axis_index(a) for a in MESH_AXES)
    return 4 * x + 2 * y + c


def _gather_copies(shard_refs, full_refs, axes, send_sems, recv_sems):
    copies = []
    for i, (shard, full) in enumerate(zip(shard_refs, full_refs)):
        mine = _block(full, axes[i], shard.shape[axes[i]], _my_index())
        for m in range(1, N_DEV):
            peer, _ = _peer(m)
            copies.append(pltpu.make_async_remote_copy(
                src_ref=shard, dst_ref=mine, send_sem=send_sems.at[N_PEERS * i + m - 1],
                recv_sem=recv_sems.at[N_PEERS * i + m - 1], device_id=peer, device_id_type=MESH))
    return copies


def _gather_start(name, shards, axes, after):
    n = len(shards)

    def body(*refs):
        shard_refs = refs[:n]
        send_sems, recv_sems, local_sems = refs[n + 1:n + 4]
        full_refs = refs[2 * n + 4:3 * n + 4]
        refs[3 * n + 4][...] = jnp.zeros(TOKEN.shape, TOKEN.dtype)
        for i in range(n):
            pltpu.make_async_copy(shard_refs[i], _block(full_refs[i], axes[i], shard_refs[i].shape[axes[i]], _my_index()),
                                  local_sems.at[i]).start()
        for cp in _gather_copies(shard_refs, full_refs, axes, send_sems, recv_sems):
            cp.start()

    fulls = []
    for s, a in zip(shards, axes):
        shape = list(s.shape)
        shape[a] *= N_DEV
        fulls.append(pltpu.HBM(tuple(shape), s.dtype))
    out = pl.pallas_call(
        body, name=name,
        out_shape=(pltpu.SemaphoreType.DMA((N_PEERS * n,)), pltpu.SemaphoreType.DMA((N_PEERS * n,)),
                   pltpu.SemaphoreType.DMA((n,)), *[pltpu.HBM(s.shape, s.dtype) for s in shards], *fulls, TOKEN),
        in_specs=[HBM_SPEC] * n + [ANY_SPEC],
        out_specs=(SEM_SPEC, SEM_SPEC, SEM_SPEC, *[HBM_SPEC] * (2 * n), VMEM_SPEC),
        input_output_aliases={i: 3 + i for i in range(n)},
        compiler_params=SPLIT_PARAMS,
    )(*_in_hbm(shards), after)
    return out[:-1], out[-1]


def _gather_wait(name, started, indices, axes, after):
    send_sems, recv_sems, local_sems = started[:3]
    n_all = (len(started) - 3) // 2
    shards = [started[3 + i] for i in indices]
    fulls = [started[3 + n_all + i] for i in indices]
    n = len(indices)

    def body(*refs):
        shard_refs, full_refs = refs[:n], refs[n:2 * n]
        send_sems, recv_sems, local_sems = refs[2 * n:2 * n + 3]
        for j, i in enumerate(indices):
            mine = _block(full_refs[j], axes[j], shard_refs[j].shape[axes[j]], _my_index())
            pltpu.make_async_copy(shard_refs[j], mine, local_sems.at[i]).wait()
            for m in range(1, N_DEV):
                peer, _ = _peer(m)
                cp = pltpu.make_async_remote_copy(
                    src_ref=shard_refs[j], dst_ref=mine, send_sem=send_sems.at[N_PEERS * i + m - 1],
                    recv_sem=recv_sems.at[N_PEERS * i + m - 1], device_id=peer, device_id_type=MESH)
                cp.wait_send()
                cp.wait_recv()

    out = pl.pallas_call(
        body, name=name,
        out_shape=tuple(pltpu.HBM(a.shape, a.dtype) for a in shards + fulls),
        in_specs=[HBM_SPEC] * (2 * n) + [SEM_SPEC] * 3 + [ANY_SPEC], out_specs=tuple([HBM_SPEC] * (2 * n)),
        input_output_aliases={i: i for i in range(2 * n)},
        compiler_params=SPLIT_PARAMS,
    )(*shards, *fulls, send_sems, recv_sems, local_sems, after)
    return out[n:]


def _exchange_start(name, fulls, axes):
    n = len(fulls)
    sizes = [f.shape[a] // N_DEV for f, a in zip(fulls, axes)]

    def body(*refs):
        ins = refs[:n]
        send_sems, recv_sems = refs[n:n + 2]
        lands = refs[2 * n + 2:3 * n + 2]
        refs[3 * n + 2][...] = jnp.zeros(TOKEN.shape, TOKEN.dtype)
        for i in range(n):
            for m in range(1, N_DEV):
                peer, index = _peer(m)
                pltpu.make_async_remote_copy(
                    src_ref=_block(ins[i], axes[i], sizes[i], index), dst_ref=lands[i].at[m - 1],
                    send_sem=send_sems.at[N_PEERS * i + m - 1], recv_sem=recv_sems.at[N_PEERS * i + m - 1],
                    device_id=peer, device_id_type=MESH).start()

    lands = []
    for f, a, size in zip(fulls, axes, sizes):
        shape = list(f.shape)
        shape[a] = size
        lands.append(pltpu.HBM((N_PEERS, *shape), f.dtype))
    out = pl.pallas_call(
        body, name=name,
        out_shape=(pltpu.SemaphoreType.DMA((N_PEERS * n,)), pltpu.SemaphoreType.DMA((N_PEERS * n,)),
                   *[pltpu.HBM(f.shape, f.dtype) for f in fulls], *lands, TOKEN),
        in_specs=[HBM_SPEC] * n, out_specs=(SEM_SPEC, SEM_SPEC, *[HBM_SPEC] * (2 * n), VMEM_SPEC),
        input_output_aliases={i: 2 + i for i in range(n)},
        compiler_params=SPLIT_PARAMS,
    )(*_in_hbm(fulls))
    return out[:-1], out[-1]


def _exchange_wait(name, started, axes, after):
    send_sems, recv_sems = started[:2]
    n = (len(started) - 2) // 2
    fulls, lands = list(started[2:2 + n]), list(started[2 + n:])
    sizes = [f.shape[a] // N_DEV for f, a in zip(fulls, axes)]

    def body(*refs):
        ins, land_refs = refs[:n], refs[n:2 * n]
        send_sems, recv_sems = refs[2 * n:2 * n + 2]
        for i in range(n):
            for m in range(1, N_DEV):
                peer, index = _peer(m)
                cp = pltpu.make_async_remote_copy(
                    src_ref=_block(ins[i], axes[i], sizes[i], index), dst_ref=land_refs[i].at[m - 1],
                    send_sem=send_sems.at[N_PEERS * i + m - 1], recv_sem=recv_sems.at[N_PEERS * i + m - 1],
                    device_id=peer, device_id_type=MESH)
                cp.wait_send()
                cp.wait_recv()

    out = pl.pallas_call(
        body, name=name,
        out_shape=tuple(pltpu.HBM(a.shape, a.dtype) for a in fulls + lands),
        in_specs=[HBM_SPEC] * (2 * n) + [SEM_SPEC] * 2 + [ANY_SPEC], out_specs=tuple([HBM_SPEC] * (2 * n)),
        input_output_aliases={i: i for i in range(2 * n)},
        compiler_params=SPLIT_PARAMS,
    )(*fulls, *lands, send_sems, recv_sems, after)
    return out[:n], out[n:]


def _adamw_update(w_ref, m_ref, v_ref, part_refs, g_ref, d_ref, nm_ref, nv_ref):
    c1 = 1.0 - ADAM_B1 ** ADAM_STEP
    c2 = 1.0 - ADAM_B2 ** ADAM_STEP
    g = None
    for p_ref in part_refs:
        stacked = len(p_ref.shape) > len(w_ref.shape)
        terms = [p_ref[s] for s in range(p_ref.shape[0])] if stacked else [p_ref[...]]
        for term in terms:
            term = term.astype(F32)
            g = term if g is None else g + term
    new_m = ADAM_B1 * m_ref[...] + (1.0 - ADAM_B1) * g
    new_v = ADAM_B2 * v_ref[...] + (1.0 - ADAM_B2) * (g * g)
    g_ref[...] = g
    nm_ref[...] = new_m
    nv_ref[...] = new_v
    d_ref[...] = -ADAM_LR * ((new_m / c1) / (jnp.sqrt(new_v / c2) + ADAM_EPS) + ADAM_WD * w_ref[...])


def _adamw_small(ws, ms, vs, stacks):
    n = len(ws)

    def body(*refs):
        ins, outs = refs[:4 * n], refs[4 * n:]
        for i in range(n):
            _adamw_update(ins[i], ins[n + i], ins[2 * n + i], [ins[3 * n + i]],
                          outs[i], outs[n + i], outs[2 * n + i], outs[3 * n + i])

    res = pl.pallas_call(
        body, name="adamw_small",
        out_shape=[jax.ShapeDtypeStruct(w.shape, F32) for w in ws] * 4,
        compiler_params=_params(None),
    )(*ws, *ms, *vs, *stacks)
    return res[:n], res[n:2 * n], res[2 * n:3 * n], res[3 * n:]


def _adamw(name, w, m, v, parts):
    r, c = w.shape
    tr = _tile(r, 256)
    n_parts = len(parts)

    def body(*refs):
        _adamw_update(refs[0], refs[1], refs[2], refs[3:3 + n_parts], *refs[3 + n_parts:])

    row = pl.BlockSpec((tr, c), lambda i: (i, 0))
    in_specs = [row, row, row]
    for p in parts:
        in_specs.append(row if p.ndim == 2 else pl.BlockSpec((p.shape[0], tr, c), lambda i: (0, i, 0)))
    return pl.pallas_call(
        body, name=name, grid=(r // tr,), in_specs=in_specs, out_specs=[row] * 4,
        out_shape=[jax.ShapeDtypeStruct((r, c), F32)] * 4,
        compiler_params=_params(("arbitrary",)),
    )(w, m, v, *parts)


SMALL = ("norm_gain", "pool_scale", "a_re", "a_im", "log_dt", "b_re", "b_im", "c_re", "c_im", "d_skip", "final_gain")
LARGE = ("w_in", "w_pool", "w_glu", "w_out", "w_ple", "w_ple_gate")
LARGE_AXIS = {"w_in": 1, "w_pool": 1, "w_glu": 1, "w_out": 0, "w_ple": 1, "w_ple_gate": 0}
WEIGHTS = ("norm_gain", "w_in", "w_pool", "pool_scale", "a_re", "a_im", "log_dt", "b_re", "b_im", "c_re", "c_im",
           "d_skip", "w_glu", "w_out", "w_ple", "w_ple_gate", "final_gain")


def kernel(x, p, norm_gain, w_in, w_pool, pool_scale, a_re, a_im, log_dt, b_re, b_im, c_re, c_im, d_skip, w_glu, w_out, w_ple, w_ple_gate, final_gain, loss_target, m_norm_gain, m_w_in, m_w_pool, m_pool_scale, m_a_re, m_a_im, m_log_dt, m_b_re, m_b_im, m_c_re, m_c_im, m_d_skip, m_w_glu, m_w_out, m_w_ple, m_w_ple_gate, m_final_gain, v_norm_gain, v_w_in, v_w_pool, v_pool_scale, v_a_re, v_a_im, v_log_dt, v_b_re, v_b_im, v_c_re, v_c_im, v_d_skip, v_w_glu, v_w_out, v_w_ple, v_w_ple_gate, v_final_gain):
    weights = dict(norm_gain=norm_gain, w_in=w_in, w_pool=w_pool, pool_scale=pool_scale, a_re=a_re, a_im=a_im,
                   log_dt=log_dt, b_re=b_re, b_im=b_im, c_re=c_re, c_im=c_im, d_skip=d_skip, w_glu=w_glu,
                   w_out=w_out, w_ple=w_ple, w_ple_gate=w_ple_gate, final_gain=final_gain)
    mom_m = dict(norm_gain=m_norm_gain, w_in=m_w_in, w_pool=m_w_pool, pool_scale=m_pool_scale, a_re=m_a_re,
                 a_im=m_a_im, log_dt=m_log_dt, b_re=m_b_re, b_im=m_b_im, c_re=m_c_re, c_im=m_c_im,
                 d_skip=m_d_skip, w_glu=m_w_glu, w_out=m_w_out, w_ple=m_w_ple, w_ple_gate=m_w_ple_gate,
                 final_gain=m_final_gain)
    mom_v = dict(norm_gain=v_norm_gain, w_in=v_w_in, w_pool=v_w_pool, pool_scale=v_pool_scale, a_re=v_a_re,
                 a_im=v_a_im, log_dt=v_log_dt, b_re=v_b_re, b_im=v_b_im, c_re=v_c_re, c_im=v_c_im,
                 d_skip=v_d_skip, w_glu=v_w_glu, w_out=v_w_out, w_ple=v_w_ple, w_ple_gate=v_w_ple_gate,
                 final_gain=v_final_gain)

    t = x.shape[1]
    xs = x.reshape(t, D_MODEL)
    ps = p.reshape(t, PLE_DIM)
    target = loss_target.reshape(t, D_MODEL)
    gain1 = norm_gain.reshape(1, D_MODEL)
    gain_f = final_gain.reshape(1, D_MODEL)
    scale_p = pool_scale.reshape(1, POOL_WIDTH)
    skip = d_skip.reshape(1, SSM_WIDTH)

    shard2d = {k: weights[k][0] for k in LARGE}
    shard_bf = {k: shard2d[k].astype(BF16) for k in LARGE}
    full = {"w_in": _all_gather("w_in_all_gather", [shard_bf["w_in"]], [LARGE_AXIS["w_in"]])[0]}
    later = [k for k in LARGE if k != "w_in"]
    later_axes = [LARGE_AXIS[k] for k in later]
    gather, gather_token = _gather_start("weights_gather_start", [shard_bf[k] for k in later], later_axes,
                                         full["w_in"])

    def arrive(k, after):
        i = later.index(k)
        full[k] = _gather_wait("gather_wait_" + k, gather, [i], [later_axes[i]], after)[0]

    ar, ai = a_re[0], a_im[0]
    ldt = log_dt.reshape(N_SSM_GROUPS, 1)
    br_t = jnp.transpose(b_re[0], (0, 2, 1))
    bi_t = jnp.transpose(b_im[0], (0, 2, 1))
    pw_re, pw_im, bb_re, bb_im = _ssm_params(ar, ai, ldt, br_t, bi_t)
    q_re = pw_re.reshape(SUBLANES, SSM_FLAT)
    q_im = pw_im.reshape(SUBLANES, SSM_FLAT)
    qb_re = q_re[::-1]
    qb_im = -q_im[::-1]
    wbd_re = _blockdiag_in(bb_re)
    wbd_im = _blockdiag_in(bb_im)
    cbd_re = _blockdiag_in(c_re[0])
    cbd_imneg = _blockdiag_in(-c_im[0])

    hn = _norm1_fwd(xs, gain1)
    proj = _mm_nn("in_proj", hn, full["w_in"], [F32], tk=2048, after=[gather_token])[0]
    pooled = _pool_fwd(proj)
    tm = _tile(t, 1024)
    arrive("w_pool", pooled)
    mixed = _mm("pool_mix", [(pooled, (tm, POOL_GROUP), lambda i, j, s: (i, j),
                              full["w_pool"], (None, POOL_GROUP, POOL_GROUP), lambda i, j, s: (j, 0, 0))],
                DOT_NN, (t // tm, N_POOL_GROUPS, 1),
                [((t, POOL_WIDTH), F32, (tm, POOL_GROUP), lambda i, j, s: (i, j))], 1)[0]
    bu_re = _ssm_in_proj("ssm_in_proj_re", proj, wbd_re)
    bu_im = _ssm_in_proj("ssm_in_proj_im", proj, wbd_im)
    s_re, s_im = _scan_fwd(bu_re, bu_im, q_re, q_im)
    y, gel = _ssm_out_proj(s_re, s_im, cbd_re, cbd_imneg, proj, skip)
    arrive("w_glu", gel)
    hg = _mm_nn("glu_proj", gel, full["w_glu"], [F32])[0]
    cat = _gate_fwd(mixed, proj, hg, scale_p)

    def residual_epilogue(acc, ex, out_refs):
        h = acc + ex[0][...]
        out_refs[0][...] = h
        out_refs[1][...] = h.astype(BF16)

    arrive("w_out", cat)
    h1, h1b = _mm_nn("out_proj", cat, full["w_out"], [F32, BF16], extras=[xs], epilogue=residual_epilogue)
    arrive("w_ple", h1b)
    e = _mm_nn("ple_proj", ps, full["w_ple"], [F32])[0]
    arrive("w_ple_gate", e)
    q = _mm_nn("ple_gate_proj", h1b, full["w_ple_gate"], [F32], tk=2048)[0]
    de, dq, dh2, g_final_gain, loss_part = _final(h1, e, q, target, gain_f)
    loss = lax.psum(loss_part[0, 0], MESH_AXES)

    grads = {}
    grads["w_ple_gate"] = _mm_tn("ple_gate_wgrad", h1b, dq, BF16)
    grads["w_ple"] = _mm_tn("ple_wgrad", ps, de, BF16)
    sent, tokens = {}, {}

    def send(names):
        sent[names], tokens[names[0]] = _exchange_start(
            "grads_start_" + names[0], [grads[k] for k in names], [LARGE_AXIS[k] for k in names])

    send(("w_ple_gate", "w_ple"))
    dh1, dh1b = _mm_nt("ple_gate_dgrad", dq, full["w_ple_gate"], [F32, BF16], extras=[dh2],
                       epilogue=residual_epilogue)
    grads["w_out"] = _mm_tn("out_wgrad", cat, dh1b, BF16)
    send(("w_out",))
    dcat = _mm_nt("out_dgrad", dh1b, full["w_out"], [F32], tk=2048, after=[tokens["w_ple_gate"], tokens["w_out"]])[0]
    dmixed, dga, dgb, dhg, g_pool_scale = _gate_bwd(dcat, mixed, proj, hg, scale_p)

    tk = _tile(t, 1024)
    grads["w_pool"] = _mm("pool_wgrad", [(pooled, (tk, POOL_GROUP), lambda i, j, s: (s, i),
                                          dmixed, (tk, POOL_GROUP), lambda i, j, s: (s, i))],
                          DOT_TN, (N_POOL_GROUPS, 1, t // tk),
                          [((N_POOL_GROUPS, POOL_GROUP, POOL_GROUP), BF16, (None, POOL_GROUP, POOL_GROUP),
                            lambda i, j, s: (i, 0, 0))], t // tk)[0]
    dpooled = _mm("pool_dgrad", [(dmixed, (tm, POOL_GROUP), lambda i, j, s: (i, j),
                                  full["w_pool"], (None, POOL_GROUP, POOL_GROUP), lambda i, j, s: (j, 0, 0))],
                  DOT_NT, (t // tm, N_POOL_GROUPS, 1),
                  [((t, POOL_WIDTH), F32, (tm, POOL_GROUP), lambda i, j, s: (i, j))], 1)[0]
    dua = _pool_bwd(dpooled)

    grads["w_glu"] = _mm_tn("glu_wgrad", gel, dhg, BF16)
    send(("w_pool", "w_glu"))

    def gelu_bwd_epilogue(acc, ex, out_refs):
        yv = ex[0][...]
        th = jnp.tanh(GELU_C * (yv + GELU_A * yv * yv * yv))
        dgelu = 0.5 * (1.0 + th) + 0.5 * yv * (1.0 - th * th) * GELU_C * (1.0 + 3.0 * GELU_A * yv * yv)
        out_refs[0][...] = acc * dgelu

    dy = _mm_nt("glu_dgrad", dhg, full["w_glu"], [F32], tk=2048, extras=[y], epilogue=gelu_bwd_epilogue,
                after=[tokens["w_pool"]])[0]
    ds_re = _ssm_dstate("ssm_dstate_re", dy, cbd_re)
    ds_im = _ssm_dstate("ssm_dstate_im", dy, cbd_imneg)
    g_cbd_re = _ssm_wgrad("ssm_c_re_wgrad", dy, 0, s_re)
    g_cbd_imneg = _ssm_wgrad("ssm_c_im_wgrad", dy, 0, s_im)
    z_re, z_im, gab_re8, gab_im8 = _scan_bwd(ds_re, ds_im, s_re, s_im, qb_re, qb_im)
    u_col0 = 2 * POOL_WIDTH // CHUNK_IN
    g_wbd_re = _ssm_wgrad("ssm_b_re_wgrad", proj, u_col0, z_re)
    g_wbd_im = _ssm_wgrad("ssm_b_im_wgrad", proj, u_col0, z_im)
    dub = _ssm_din(z_re, z_im, wbd_re, wbd_im, dy, skip)

    g_ab_re = jnp.sum(gab_re8, axis=0).reshape(N_SSM_GROUPS, SSM_STATE)
    g_ab_im = jnp.sum(gab_im8, axis=0).reshape(N_SSM_GROUPS, SSM_STATE)
    d_ar, d_ai, d_ldt, d_br_t, d_bi_t = _ssm_params_bwd(
        ar, ai, ldt, br_t, bi_t, g_ab_re, g_ab_im, _diag_in(g_wbd_re), _diag_in(g_wbd_im))

    dproj = jnp.concatenate([dua, dga, dub, dgb], axis=1)
    grads["w_in"] = _mm_tn("in_wgrad", hn, dproj, BF16)
    send(("w_in",))
    dhn = _mm_nt("in_dgrad", dproj, full["w_in"], [F32], tk=2048, after=[tokens["w_in"]])[0]
    grad_x, g_norm_gain = _norm1_bwd(xs, dhn, dh1, gain1)
    g_d_skip = _skip_grad(dy, proj)

    def b_view(a):
        return jnp.transpose(a[0], (0, 2, 1))

    views = dict(norm_gain=lambda a: a, pool_scale=lambda a: a, a_re=lambda a: a[0], a_im=lambda a: a[0],
                 log_dt=lambda a: a, b_re=b_view, b_im=b_view, c_re=lambda a: a[0], c_im=lambda a: a[0],
                 d_skip=lambda a: a, final_gain=lambda a: a.reshape(1, D_MODEL))
    small_grads = dict(
        norm_gain=g_norm_gain, pool_scale=g_pool_scale, a_re=d_ar, a_im=d_ai, log_dt=d_ldt.reshape(1, N_SSM_GROUPS),
        b_re=d_br_t, b_im=d_bi_t, c_re=_diag_in(g_cbd_re), c_im=-_diag_in(g_cbd_imneg), d_skip=g_d_skip,
        final_gain=g_final_gain)
    stacks = _all_gather("small_grads_all_gather", [small_grads[k][None] for k in SMALL], [0] * len(SMALL))
    small_out = _adamw_small([views[k](weights[k]) for k in SMALL], [views[k](mom_m[k]) for k in SMALL],
                             [views[k](mom_v[k]) for k in SMALL], stacks)
    out_g, out_d, out_m, out_v = ({} for _ in range(4))
    for out, res in zip((out_g, out_d, out_m, out_v), small_out):
        for k, r in zip(SMALL, res):
            if k in ("b_re", "b_im"):
                r = jnp.transpose(r, (0, 2, 1))
            out[k] = r.reshape(weights[k].shape)

    me = 4 * lax.axis_index("x") + 2 * lax.axis_index("y") + lax.axis_index("c")
    after = grad_x
    for names, started in sent.items():
        axes = [LARGE_AXIS[k] for k in names]
        partials, landed = _exchange_wait("grads_wait_" + names[0], started, axes, after)
        for k, axis, partial, land in zip(names, axes, partials, landed):
            shard_shape = shard2d[k].shape
            size = shard_shape[axis]
            own = lax.dynamic_slice_in_dim(partial, me * size, size, axis=axis)
            view = (-1, shard_shape[-1])
            rows = math.prod(shard_shape[:-1])
            res = _adamw("adamw_" + k, shard2d[k].reshape(view), mom_m[k][0].reshape(view), mom_v[k][0].reshape(view),
                         [own.reshape(view), land.reshape(N_PEERS, rows, shard_shape[-1])])
            out_g[k], out_d[k], out_m[k], out_v[k] = (r.reshape(weights[k].shape) for r in res)
            after = res[0]

    return (loss, grad_x.reshape(x.shape), *[out_g[k] for k in WEIGHTS], *[out_d[k] for k in WEIGHTS],
            *[out_m[k] for k in WEIGHTS], *[out_v[k] for k in WEIGHTS])


def _skip_grad(dy, proj):
    t = dy.shape[0]
    tm = _tile(t, 512)

    def body(dy_ref, u_ref, o_ref):
        @pl.when(pl.program_id(0) == 0)
        def _():
            o_ref[...] = jnp.zeros_like(o_ref)

        o_ref[...] += jnp.sum(dy_ref[...] * u_ref[...], axis=0, keepdims=True)

    return pl.pallas_call(
        body, name="skip_grad", grid=(t // tm,),
        in_specs=[pl.BlockSpec((tm, SSM_WIDTH), lambda i: (i, 0)), pl.BlockSpec((tm, SSM_WIDTH), lambda i: (i, 2))],
        out_specs=pl.BlockSpec((1, SSM_WIDTH), lambda i: (0, 0)),
        out_shape=jax.ShapeDtypeStruct((1, SSM_WIDTH), F32),
        compiler_params=_params(("arbitrary",)),
    )(dy, proj)
```

```python
import functools
import math

import jax
import jax.numpy as jnp
from jax import lax
from jax.experimental import pallas as pl
from jax.experimental.pallas import tpu as pltpu

F32 = jnp.float32
BF16 = jnp.bfloat16
MESH = pl.DeviceIdType.MESH
MESH_AXES = ("x", "y", "c")
N_DEV = 8

D_MODEL = 2048
POOL_WIDTH = 1024
SSM_WIDTH = 1024
N_POOL_GROUPS = 4
POOL_GROUP = 256
SSM_GROUP = 16
N_SSM_GROUPS = 64
SSM_STATE = 64
SSM_FLAT = N_SSM_GROUPS * SSM_STATE
SSM_CHUNKS = 4
CHUNK_IN = SSM_WIDTH // SSM_CHUNKS
CHUNK_STATE = SSM_FLAT // SSM_CHUNKS
PLE_DIM = 256
EPS = 1e-6
A_RE_MAX = -1e-4
ADAM_LR = 0.001
ADAM_B1 = 0.9
ADAM_B2 = 0.999
ADAM_EPS = 1e-08
ADAM_WD = 0.01
ADAM_STEP = 10
GELU_C = math.sqrt(2.0 / math.pi)
GELU_A = 0.044715

SUBLANES = 8
LANES = 128
VMEM_LIMIT_BYTES = 48 * 1024 * 1024

DOT_NN = (((1,), (0,)), ((), ()))
DOT_NT = (((1,), (1,)), ((), ()))
DOT_TN = (((0,), (0,)), ((), ()))


def _tile(n, pref):
    return pref if n % pref == 0 else n


def _params(sem):
    return pltpu.CompilerParams(dimension_semantics=sem, vmem_limit_bytes=VMEM_LIMIT_BYTES)


def _sigmoid(v):
    return 1.0 / (1.0 + jnp.exp(-v))


def _silu_and_grad(v):
    s = _sigmoid(v)
    return v * s, s * (1.0 + v * (1.0 - s))


def _is_lane_split(shape):
    return len(shape) == 3 and tuple(shape[1:]) == (SUBLANES, LANES)


def _load(ref):
    if _is_lane_split(ref.shape):
        return jnp.concatenate([ref[:, j, :] for j in range(SUBLANES)], axis=1)
    return ref[...]


def _store(ref, value):
    if _is_lane_split(ref.shape):
        for j in range(SUBLANES):
            ref[:, j, :] = value[:, j * LANES:(j + 1) * LANES].astype(ref.dtype)
    else:
        ref[...] = value.astype(ref.dtype)


def _mm(name, pairs, dims, grid, outs, k_steps, extras=(), epilogue=None):
    n_pairs, n_ex, n_out = len(pairs), len(extras), len(outs)
    acc_shape = tuple(d for d in outs[0][2] if d is not None)
    if _is_lane_split(acc_shape):
        acc_shape = (acc_shape[0], SUBLANES * LANES)
    if epilogue is None:
        def epilogue(acc, ex, out_refs):
            _store(out_refs[0], acc)

    def body(*refs):
        ab = refs[:2 * n_pairs]
        ex = refs[2 * n_pairs:2 * n_pairs + n_ex]
        out_refs = refs[2 * n_pairs + n_ex:2 * n_pairs + n_ex + n_out]
        acc = refs[-1]
        k = pl.program_id(2)

        @pl.when(k == 0)
        def _():
            acc[...] = jnp.zeros_like(acc)

        part = None
        for q in range(n_pairs):
            d = lax.dot_general(_load(ab[2 * q]).astype(BF16), _load(ab[2 * q + 1]).astype(BF16), dims,
                                preferred_element_type=F32)
            part = d if part is None else part + d
        acc[...] += part

        @pl.when(k == k_steps - 1)
        def _():
            epilogue(acc[...], ex, out_refs)

    in_specs, operands = [], []
    for a, a_blk, a_map, b, b_blk, b_map in pairs:
        in_specs += [pl.BlockSpec(a_blk, a_map), pl.BlockSpec(b_blk, b_map)]
        operands += [a, b]
    for e, e_blk, e_map in extras:
        in_specs.append(pl.BlockSpec(e_blk, e_map))
        operands.append(e)
    return pl.pallas_call(
        body, name=name, grid=grid, in_specs=in_specs,
        out_specs=[pl.BlockSpec(o[2], o[3]) for o in outs],
        out_shape=[jax.ShapeDtypeStruct(o[0], o[1]) for o in outs],
        scratch_shapes=[pltpu.VMEM(acc_shape, F32)],
        compiler_params=_params(("arbitrary", "arbitrary", "arbitrary")),
    )(*operands)


def _after(tokens):
    return [(tok, tok.shape, lambda i, j, s: (0, 0)) for tok in tokens]


def _mm_nn(name, a, b, out_dtypes, tm=1024, tn=1024, tk=1024, a_col0=0, extras=(), epilogue=None, after=()):
    m, n = a.shape[0], b.shape[1]
    k = b.shape[0]
    tm, tn, tk = _tile(m, tm), _tile(n, tn), _tile(k, tk)
    outs = [((m, n), dt, (tm, tn), lambda i, j, s: (i, j)) for dt in out_dtypes]
    ex = [(e, (tm, tn), lambda i, j, s: (i, j)) for e in extras] + _after(after)
    return _mm(name, [(a, (tm, tk), lambda i, j, s: (i, a_col0 + s), b, (tk, tn), lambda i, j, s: (s, j))],
               DOT_NN, (m // tm, n // tn, k // tk), outs, k // tk, ex, epilogue)


def _mm_nt(name, a, b, out_dtypes, tm=1024, tn=1024, tk=1024, extras=(), epilogue=None, after=()):
    m, kk = a.shape
    n = b.shape[0]
    tm, tn, tk = _tile(m, tm), _tile(n, tn), _tile(kk, tk)
    outs = [((m, n), dt, (tm, tn), lambda i, j, s: (i, j)) for dt in out_dtypes]
    ex = [(e, (tm, tn), lambda i, j, s: (i, j)) for e in extras] + _after(after)
    return _mm(name, [(a, (tm, tk), lambda i, j, s: (i, s), b, (tn, tk), lambda i, j, s: (j, s))],
               DOT_NT, (m // tm, n // tn, kk // tk), outs, kk // tk, ex, epilogue)


def _mm_tn(name, a, b, out_dtype, tm=512, tn=2048, tk=1024):
    m, kk = a.shape
    n = b.shape[1]
    tm, tn, tk = _tile(kk, tm), _tile(n, tn), _tile(m, tk)
    outs = [((kk, n), out_dtype, (tm, tn), lambda i, j, s: (i, j))]
    return _mm(name, [(a, (tk, tm), lambda i, j, s: (s, i), b, (tk, tn), lambda i, j, s: (s, j))],
               DOT_TN, (kk // tm, n // tn, m // tk), outs, m // tk)[0]


def _norm1_fwd(x, gain):
    t = x.shape[0]
    tm = _tile(t, 512)

    def body(x_ref, g_ref, hn_ref):
        xv = x_ref[...]
        r = lax.rsqrt(jnp.mean(xv * xv, axis=-1, keepdims=True) + EPS)
        hn_ref[...] = (xv * r * g_ref[...]).astype(BF16)

    return pl.pallas_call(
        body, name="norm1_fwd", grid=(t // tm,),
        in_specs=[pl.BlockSpec((tm, D_MODEL), lambda i: (i, 0)), pl.BlockSpec((1, D_MODEL), lambda i: (0, 0))],
        out_specs=pl.BlockSpec((tm, D_MODEL), lambda i: (i, 0)),
        out_shape=jax.ShapeDtypeStruct((t, D_MODEL), BF16),
        compiler_params=_params(("arbitrary",)),
    )(x, gain)


def _norm1_bwd(x, dhn, dh1, gain):
    t = x.shape[0]
    tm = _tile(t, 512)

    def body(x_ref, dhn_ref, dh1_ref, g_ref, dx_ref, gg_ref):
        @pl.when(pl.program_id(0) == 0)
        def _():
            gg_ref[...] = jnp.zeros_like(gg_ref)

        xv = x_ref[...]
        r = lax.rsqrt(jnp.mean(xv * xv, axis=-1, keepdims=True) + EPS)
        xh = xv * r
        dhn_v = dhn_ref[...]
        gg_ref[...] += jnp.sum(dhn_v * xh, axis=0, keepdims=True)
        dxh = dhn_v * g_ref[...]
        dx_ref[...] = dh1_ref[...] + r * (dxh - xh * jnp.mean(dxh * xh, axis=-1, keepdims=True))

    row = pl.BlockSpec((tm, D_MODEL), lambda i: (i, 0))
    vec = pl.BlockSpec((1, D_MODEL), lambda i: (0, 0))
    return pl.pallas_call(
        body, name="norm1_bwd", grid=(t // tm,),
        in_specs=[row, row, row, vec], out_specs=[row, vec],
        out_shape=[jax.ShapeDtypeStruct((t, D_MODEL), F32), jax.ShapeDtypeStruct((1, D_MODEL), F32)],
        compiler_params=_params(("arbitrary",)),
    )(x, dhn, dh1, gain)


def _pool_counts(t, width, group):
    row = lax.broadcasted_iota(jnp.int32, (t, width), 0)
    window = jnp.left_shift(jnp.int32(2), group)
    return row, jnp.minimum(row + 1, window).astype(F32)


def _select_window(group, s2, s4, s8, s16):
    return jnp.where(group == 0, s2, jnp.where(group == 1, s4, jnp.where(group == 2, s8, s16)))


def _pool_fwd(proj):
    t = proj.shape[0]
    tc = LANES

    def body(u_ref, o_ref):
        group = pl.program_id(0) // (POOL_GROUP // tc)
        v = u_ref[...]
        row, count = _pool_counts(t, tc, group)

        def down(a, j):
            return jnp.where(row >= j, pltpu.roll(a, j, 0), 0.0)

        s2 = v + down(v, 1)
        s4 = s2 + down(s2, 2)
        s8 = s4 + down(s4, 4)
        s16 = s8 + down(s8, 8)
        o_ref[...] = (_select_window(group, s2, s4, s8, s16) / count - v).astype(BF16)

    return pl.pallas_call(
        body, name="pool_fwd", grid=(POOL_WIDTH // tc,),
        in_specs=[pl.BlockSpec((t, tc), lambda j: (0, j))],
        out_specs=pl.BlockSpec((t, tc), lambda j: (0, j)),
        out_shape=jax.ShapeDtypeStruct((t, POOL_WIDTH), BF16),
        compiler_params=_params(("arbitrary",)),
    )(proj)


def _pool_bwd(dpooled):
    t = dpooled.shape[0]
    tc = LANES

    def body(d_ref, o_ref):
        group = pl.program_id(0) // (POOL_GROUP // tc)
        dp = d_ref[...]
        row, count = _pool_counts(t, tc, group)
        r = dp / count

        def up(a, j):
            return jnp.where(row < t - j, pltpu.roll(a, t - j, 0), 0.0)

        s2 = r + up(r, 1)
        s4 = s2 + up(s2, 2)
        s8 = s4 + up(s4, 4)
        s16 = s8 + up(s8, 8)
        o_ref[...] = (_select_window(group, s2, s4, s8, s16) - dp).astype(BF16)

    return pl.pallas_call(
        body, name="pool_bwd", grid=(POOL_WIDTH // tc,),
        in_specs=[pl.BlockSpec((t, tc), lambda j: (0, j))],
        out_specs=pl.BlockSpec((t, tc), lambda j: (0, j)),
        out_shape=jax.ShapeDtypeStruct((t, POOL_WIDTH), BF16),
        compiler_params=_params(("arbitrary",)),
    )(dpooled)


def _gate_fwd(mixed, proj, hg, pool_scale):
    t = mixed.shape[0]
    tm = _tile(t, 512)

    def body(mx_ref, ga_ref, gb_ref, hg_ref, ps_ref, cat_ref):
        silu_a, _ = _silu_and_grad(ga_ref[...])
        cat_ref[:, :POOL_WIDTH] = (mx_ref[...] * ps_ref[...] * silu_a).astype(BF16)
        silu_b, _ = _silu_and_grad(gb_ref[...])
        sb = hg_ref[:, :SSM_WIDTH] * _sigmoid(hg_ref[:, SSM_WIDTH:])
        cat_ref[:, POOL_WIDTH:] = (sb * silu_b).astype(BF16)

    return pl.pallas_call(
        body, name="gate_fwd", grid=(t // tm,),
        in_specs=[pl.BlockSpec((tm, POOL_WIDTH), lambda i: (i, 0)),
                  pl.BlockSpec((tm, POOL_WIDTH), lambda i: (i, 1)),
                  pl.BlockSpec((tm, SSM_WIDTH), lambda i: (i, 3)),
                  pl.BlockSpec((tm, 2 * SSM_WIDTH), lambda i: (i, 0)),
                  pl.BlockSpec((1, POOL_WIDTH), lambda i: (0, 0))],
        out_specs=pl.BlockSpec((tm, D_MODEL), lambda i: (i, 0)),
        out_shape=jax.ShapeDtypeStruct((t, D_MODEL), BF16),
        compiler_params=_params(("arbitrary",)),
    )(mixed, proj, proj, hg, pool_scale)


def _gate_bwd(dcat, mixed, proj, hg, pool_scale):
    t = mixed.shape[0]
    tm = _tile(t, 512)

    def body(dc_ref, mx_ref, ga_ref, gb_ref, hg_ref, ps_ref, dmx_ref, dga_ref, dgb_ref, dhg_ref, gps_ref):
        @pl.when(pl.program_id(0) == 0)
        def _():
            gps_ref[...] = jnp.zeros_like(gps_ref)

        ps = ps_ref[...]
        mx = mx_ref[...]
        dya = dc_ref[:, :POOL_WIDTH]
        silu_a, dsilu_a = _silu_and_grad(ga_ref[...])
        dpa = dya * silu_a
        gps_ref[...] += jnp.sum(dpa * mx, axis=0, keepdims=True)
        dmx_ref[...] = (dpa * ps).astype(BF16)
        dga_ref[...] = (dya * mx * ps * dsilu_a).astype(BF16)

        dyb = dc_ref[:, POOL_WIDTH:]
        silu_b, dsilu_b = _silu_and_grad(gb_ref[...])
        h_a = hg_ref[:, :SSM_WIDTH]
        sg = _sigmoid(hg_ref[:, SSM_WIDTH:])
        dsb = dyb * silu_b
        dgb_ref[...] = (dyb * h_a * sg * dsilu_b).astype(BF16)
        dhg_ref[:, :SSM_WIDTH] = (dsb * sg).astype(BF16)
        dhg_ref[:, SSM_WIDTH:] = (dsb * h_a * sg * (1.0 - sg)).astype(BF16)

    half = pl.BlockSpec((tm, POOL_WIDTH), lambda i: (i, 0))
    full = pl.BlockSpec((tm, D_MODEL), lambda i: (i, 0))
    vec = pl.BlockSpec((1, POOL_WIDTH), lambda i: (0, 0))
    return pl.pallas_call(
        body, name="gate_bwd", grid=(t // tm,),
        in_specs=[full, half,
                  pl.BlockSpec((tm, POOL_WIDTH), lambda i: (i, 1)),
                  pl.BlockSpec((tm, SSM_WIDTH), lambda i: (i, 3)),
                  full, vec],
        out_specs=[half, half, half, full, vec],
        out_shape=[jax.ShapeDtypeStruct((t, POOL_WIDTH), BF16), jax.ShapeDtypeStruct((t, POOL_WIDTH), BF16),
                   jax.ShapeDtypeStruct((t, SSM_WIDTH), BF16), jax.ShapeDtypeStruct((t, 2 * SSM_WIDTH), BF16),
                   jax.ShapeDtypeStruct((1, POOL_WIDTH), F32)],
        compiler_params=_params(("arbitrary",)),
    )(dcat, mixed, proj, proj, hg, pool_scale)


def _final(h1, e, q, target, gain):
    t = h1.shape[0]
    tm = _tile(t, 256)

    def body(h1_ref, e_ref, q_ref, tg_ref, g_ref, de_ref, dq_ref, dh2_ref, gg_ref, loss_ref):
        @pl.when(pl.program_id(0) == 0)
        def _():
            gg_ref[...] = jnp.zeros_like(gg_ref)
            loss_ref[...] = jnp.zeros_like(loss_ref)

        ev = e_ref[...]
        sg = _sigmoid(q_ref[...])
        h2 = h1_ref[...] + ev * sg
        r = lax.rsqrt(jnp.mean(h2 * h2, axis=-1, keepdims=True) + EPS)
        n = h2 * r
        gain_v = g_ref[...]
        diff = n * gain_v - tg_ref[...]
        row_loss = jnp.sum(diff * diff, axis=-1, keepdims=True)
        loss_ref[...] += (0.5 / D_MODEL) * jnp.sum(row_loss, axis=0, keepdims=True)
        dout = diff * (1.0 / D_MODEL)
        gg_ref[...] += jnp.sum(dout * n, axis=0, keepdims=True)
        dn = dout * gain_v
        dh2 = r * (dn - n * jnp.mean(dn * n, axis=-1, keepdims=True))
        dh2_ref[...] = dh2
        de_ref[...] = (dh2 * sg).astype(BF16)
        dq_ref[...] = (dh2 * ev * sg * (1.0 - sg)).astype(BF16)

    row = pl.BlockSpec((tm, D_MODEL), lambda i: (i, 0))
    vec = pl.BlockSpec((1, D_MODEL), lambda i: (0, 0))
    return pl.pallas_call(
        body, name="final_norm_loss", grid=(t // tm,),
        in_specs=[row, row, row, row, vec],
        out_specs=[row, row, row, vec, pl.BlockSpec((1, 1), lambda i: (0, 0))],
        out_shape=[jax.ShapeDtypeStruct((t, D_MODEL), BF16), jax.ShapeDtypeStruct((t, D_MODEL), BF16),
                   jax.ShapeDtypeStruct((t, D_MODEL), F32), jax.ShapeDtypeStruct((1, D_MODEL), F32),
                   jax.ShapeDtypeStruct((1, 1), F32)],
        compiler_params=_params(("arbitrary",)),
    )(h1, e, q, target, gain)


def _zoh(a_re, a_im, log_dt, b_re_t, b_im_t):
    lam_re = jnp.minimum(a_re, A_RE_MAX)
    lam_im = a_im
    dt = jnp.exp(log_dt)
    mag = jnp.exp(lam_re * dt)
    ang = lam_im * dt
    ab_re = mag * jnp.cos(ang)
    ab_im = mag * jnp.sin(ang)
    den = lam_re * lam_re + lam_im * lam_im
    n_re = ab_re - 1.0
    n_im = ab_im
    q_re = (n_re * lam_re + n_im * lam_im) / den
    q_im = (n_im * lam_re - n_re * lam_im) / den
    bb_re = q_re[:, None, :] * b_re_t - q_im[:, None, :] * b_im_t
    bb_im = q_re[:, None, :] * b_im_t + q_im[:, None, :] * b_re_t
    return ab_re, ab_im, bb_re, bb_im


def _ssm_params(a_re, a_im, log_dt, b_re_t, b_im_t):
    def body(are_ref, aim_ref, dt_ref, bre_ref, bim_ref, abre_ref, abim_ref, bbre_ref, bbim_ref):
        ab_re, ab_im, bb_re, bb_im = _zoh(are_ref[...], aim_ref[...], dt_ref[...], bre_ref[...], bim_ref[...])
        abre_ref[...] = ab_re
        abim_ref[...] = ab_im
        bbre_ref[...] = bb_re
        bbim_ref[...] = bb_im

    return pl.pallas_call(
        body, name="ssm_params",
        out_shape=[jax.ShapeDtypeStruct(a_re.shape, F32), jax.ShapeDtypeStruct(a_re.shape, F32),
                   jax.ShapeDtypeStruct(b_re_t.shape, F32), jax.ShapeDtypeStruct(b_re_t.shape, F32)],
        compiler_params=_params(None),
    )(a_re, a_im, log_dt, b_re_t, b_im_t)


def _ssm_params_bwd(a_re, a_im, log_dt, b_re_t, b_im_t, g_ab_re, g_ab_im, g_bb_re, g_bb_im):
    def body(are_ref, aim_ref, dt_ref, bre_ref, bim_ref, gar_ref, gai_ref, gbr_ref, gbi_ref,
             o_are, o_aim, o_dt, o_bre, o_bim):
        _, vjp = jax.vjp(_zoh, are_ref[...], aim_ref[...], dt_ref[...], bre_ref[...], bim_ref[...])
        d_are, d_aim, d_dt, d_bre, d_bim = vjp((gar_ref[...], gai_ref[...], gbr_ref[...], gbi_ref[...]))
        o_are[...] = d_are
        o_aim[...] = d_aim
        o_dt[...] = d_dt
        o_bre[...] = d_bre
        o_bim[...] = d_bim

    ins = (a_re, a_im, log_dt, b_re_t, b_im_t)
    return pl.pallas_call(
        body, name="ssm_params_bwd",
        out_shape=[jax.ShapeDtypeStruct(v.shape, F32) for v in ins],
        compiler_params=_params(None),
    )(*ins, g_ab_re, g_ab_im, g_bb_re, g_bb_im)


GROUPS_PER_CHUNK = N_SSM_GROUPS // SSM_CHUNKS


def _diag_mask(rows_per_group, cols_per_group):
    shape = (GROUPS_PER_CHUNK * rows_per_group, GROUPS_PER_CHUNK * cols_per_group)
    r = lax.broadcasted_iota(jnp.int32, shape, 0) // rows_per_group
    c = lax.broadcasted_iota(jnp.int32, shape, 1) // cols_per_group
    return (r == c)[None]


def _blockdiag_in(w):
    tiled = jnp.tile(w.reshape(SSM_CHUNKS, CHUNK_IN, SSM_STATE), (1, 1, GROUPS_PER_CHUNK))
    return jnp.where(_diag_mask(SSM_GROUP, SSM_STATE), tiled, 0.0).astype(BF16)


def _diag_in(g):
    kept = jnp.where(_diag_mask(SSM_GROUP, SSM_STATE), g, 0.0)
    return kept.reshape(SSM_CHUNKS, CHUNK_IN, GROUPS_PER_CHUNK, SSM_STATE).sum(axis=2).reshape(
        N_SSM_GROUPS, SSM_GROUP, SSM_STATE)


STATE_TILES = SSM_FLAT // LANES
CHUNK_TILES = CHUNK_STATE // LANES
STATE_SPLIT = (SUBLANES, LANES)


def _state_shape(t):
    return (t, STATE_TILES, LANES)


def _ssm_in_proj(name, proj, wbd, tm=1024):
    t = proj.shape[0]
    tm = _tile(t, tm)
    col0 = 2 * POOL_WIDTH // CHUNK_IN
    return _mm(name, [(proj, (tm, CHUNK_IN), lambda i, j, s: (i, col0 + j),
                       wbd, (None, CHUNK_IN, CHUNK_STATE), lambda i, j, s: (j, 0, 0))],
               DOT_NN, (t // tm, SSM_CHUNKS, 1),
               [(_state_shape(t), F32, (tm, *STATE_SPLIT), lambda i, j, s: (i, j, 0))], 1)[0]


def _ssm_out_proj(s_re, s_im, cbd_re, cbd_imneg, proj, d_skip, tm=1024):
    t = s_re.shape[0]
    tm = _tile(t, tm)
    col0 = 2 * POOL_WIDTH // CHUNK_IN

    def epilogue(acc, ex, out_refs):
        y = acc + ex[1][...] * ex[0][...]
        out_refs[0][...] = y
        out_refs[1][...] = (0.5 * y * (1.0 + jnp.tanh(GELU_C * (y + GELU_A * y * y * y)))).astype(BF16)

    a_map = lambda i, j, s: (i, j, 0)
    b_map = lambda i, j, s: (j, 0, 0)
    o_map = lambda i, j, s: (i, j)
    return _mm("ssm_out_proj",
               [(s_re, (tm, *STATE_SPLIT), a_map, cbd_re, (None, CHUNK_IN, CHUNK_STATE), b_map),
                (s_im, (tm, *STATE_SPLIT), a_map, cbd_imneg, (None, CHUNK_IN, CHUNK_STATE), b_map)],
               DOT_NT, (t // tm, SSM_CHUNKS, 1),
               [((t, SSM_WIDTH), F32, (tm, CHUNK_IN), o_map), ((t, SSM_WIDTH), BF16, (tm, CHUNK_IN), o_map)], 1,
               [(proj, (tm, CHUNK_IN), lambda i, j, s: (i, col0 + j)), (d_skip, (1, CHUNK_IN), lambda i, j, s: (0, j))],
               epilogue)


def _ssm_dstate(name, dy, cbd, tm=1024):
    t = dy.shape[0]
    tm = _tile(t, tm)
    return _mm(name, [(dy, (tm, CHUNK_IN), lambda i, j, s: (i, j),
                       cbd, (None, CHUNK_IN, CHUNK_STATE), lambda i, j, s: (j, 0, 0))],
               DOT_NN, (t // tm, SSM_CHUNKS, 1),
               [(_state_shape(t), F32, (tm, *STATE_SPLIT), lambda i, j, s: (i, j, 0))], 1)[0]


def _ssm_wgrad(name, a, col0, z, tk=1024):
    t = z.shape[0]
    tk = _tile(t, tk)
    return _mm(name, [(a, (tk, CHUNK_IN), lambda i, j, k: (k, col0 + i), z, (tk, *STATE_SPLIT), lambda i, j, k: (k, i, 0))],
               DOT_TN, (SSM_CHUNKS, 1, t // tk),
               [((SSM_CHUNKS, CHUNK_IN, CHUNK_STATE), F32, (None, CHUNK_IN, CHUNK_STATE), lambda i, j, k: (i, 0, 0))],
               t // tk)[0]


def _ssm_din(z_re, z_im, wbd_re, wbd_im, dy, d_skip, tm=1024):
    t = z_re.shape[0]
    tm = _tile(t, tm)

    def epilogue(acc, ex, out_refs):
        out_refs[0][...] = (acc + ex[1][...] * ex[0][...]).astype(BF16)

    a_map = lambda i, j, s: (i, j, 0)
    b_map = lambda i, j, s: (j, 0, 0)
    return _mm("ssm_din",
               [(z_re, (tm, *STATE_SPLIT), a_map, wbd_re, (None, CHUNK_IN, CHUNK_STATE), b_map),
                (z_im, (tm, *STATE_SPLIT), a_map, wbd_im, (None, CHUNK_IN, CHUNK_STATE), b_map)],
               DOT_NT, (t // tm, SSM_CHUNKS, 1),
               [((t, SSM_WIDTH), BF16, (tm, CHUNK_IN), lambda i, j, s: (i, j))], 1,
               [(dy, (tm, CHUNK_IN), lambda i, j, s: (i, j)), (d_skip, (1, CHUNK_IN), lambda i, j, s: (0, j))],
               epilogue)[0]


SCAN_FWD_ROWS = 256
SCAN_BWD_ROWS = 128
SCAN_STEPS = 8


def _scan_fwd(bu_re, bu_im, a_re, a_im):
    t = bu_re.shape[0]
    tt = _tile(t, SCAN_FWD_ROWS)

    def body(bre_ref, bim_ref, are_ref, aim_ref, sre_ref, sim_ref, cre_ref, cim_ref):
        @pl.when(pl.program_id(0) == 0)
        def _():
            cre_ref[...] = jnp.zeros_like(cre_ref)
            cim_ref[...] = jnp.zeros_like(cim_ref)

        a_re_v, a_im_v = are_ref[...], aim_ref[...]

        def steps(i, carry):
            s_re, s_im = carry
            for u in range(SCAN_STEPS):
                tok = i * SCAN_STEPS + u
                s_re, s_im = (a_re_v * s_re - a_im_v * s_im + bre_ref[tok],
                              a_re_v * s_im + a_im_v * s_re + bim_ref[tok])
                sre_ref[tok] = s_re
                sim_ref[tok] = s_im
            return s_re, s_im

        s_re, s_im = lax.fori_loop(0, tt // SCAN_STEPS, steps, (cre_ref[...], cim_ref[...]))
        cre_ref[...] = s_re
        cim_ref[...] = s_im

    blk = pl.BlockSpec((tt, STATE_TILES, LANES), lambda i: (i, 0, 0))
    ablk = pl.BlockSpec((STATE_TILES, LANES), lambda i: (0, 0))
    return pl.pallas_call(
        body, name="scan_fwd", grid=(t // tt,),
        in_specs=[blk, blk, ablk, ablk], out_specs=[blk, blk],
        out_shape=[jax.ShapeDtypeStruct(_state_shape(t), F32)] * 2,
        scratch_shapes=[pltpu.VMEM((STATE_TILES, LANES), F32)] * 2,
        compiler_params=_params(("arbitrary",)),
    )(bu_re, bu_im, a_re, a_im)


def _scan_bwd(ds_re, ds_im, s_re, s_im, a_re, a_im):
    t = ds_re.shape[0]
    tt = _tile(t, SCAN_BWD_ROWS)
    n_chunks = t // tt

    def body(dre_ref, dim_ref, sre_ref, sim_ref, are_ref, aim_ref, zre_ref, zim_ref, gre_ref, gim_ref,
             cre_ref, cim_ref):
        @pl.when(pl.program_id(0) == 0)
        def _():
            cre_ref[...] = jnp.zeros_like(cre_ref)
            cim_ref[...] = jnp.zeros_like(cim_ref)
            gre_ref[...] = jnp.zeros_like(gre_ref)
            gim_ref[...] = jnp.zeros_like(gim_ref)

        a_re_v, a_im_v = are_ref[...], aim_ref[...]

        def steps(i, carry):
            z_re, z_im, g_re, g_im = carry
            for u in range(SCAN_STEPS):
                tok = tt - 1 - (i * SCAN_STEPS + u)
                s_re_v, s_im_v = sre_ref[tok], sim_ref[tok]
                g_re = g_re + z_re * s_re_v + z_im * s_im_v
                g_im = g_im + z_im * s_re_v - z_re * s_im_v
                z_re, z_im = (dre_ref[tok] + a_re_v * z_re + a_im_v * z_im,
                              dim_ref[tok] + a_re_v * z_im - a_im_v * z_re)
                zre_ref[tok] = z_re
                zim_ref[tok] = z_im
            return z_re, z_im, g_re, g_im

        z_re, z_im, g_re, g_im = lax.fori_loop(
            0, tt // SCAN_STEPS, steps, (cre_ref[...], cim_ref[...], gre_ref[...], gim_ref[...]))
        cre_ref[...] = z_re
        cim_ref[...] = z_im
        gre_ref[...] = g_re
        gim_ref[...] = g_im

    blk = pl.BlockSpec((tt, STATE_TILES, LANES), lambda i: (n_chunks - 1 - i, 0, 0))
    ablk = pl.BlockSpec((STATE_TILES, LANES), lambda i: (0, 0))
    return pl.pallas_call(
        body, name="scan_bwd", grid=(n_chunks,),
        in_specs=[blk, blk, blk, blk, ablk, ablk], out_specs=[blk, blk, ablk, ablk],
        out_shape=[jax.ShapeDtypeStruct(_state_shape(t), F32)] * 2 + [jax.ShapeDtypeStruct((STATE_TILES, LANES), F32)] * 2,
        scratch_shapes=[pltpu.VMEM((STATE_TILES, LANES), F32)] * 2,
        compiler_params=_params(("arbitrary",)),
    )(ds_re, ds_im, s_re, s_im, a_re, a_im)


def _block(ref, axis, size, index):
    idx = [slice(None)] * len(ref.shape)
    idx[axis] = pl.ds(pl.multiple_of(index * size, size), size)
    return ref.at[tuple(idx)]


def _all_gather(name, shards, axes):
    n = len(shards)
    sizes = [s.shape[a] for s, a in zip(shards, axes)]

    def body(*refs):
        ins, outs = refs[:n], refs[n:2 * n]
        send_sems, recv_sems, local_sems = refs[2 * n:]
        x, y, c = (lax.axis_index(a) for a in MESH_AXES)
        me, sibling = (x, y, c), (x, y, 1 - c)
        chips = [(1 - x, y), (x, 1 - y), (1 - x, 1 - y)]

        def rows(i, dev):
            return _block(outs[i], axes[i], sizes[i], 4 * dev[0] + 2 * dev[1] + dev[2])

        def copy(i, k, block, to, src=None):
            return pltpu.make_async_remote_copy(
                src_ref=rows(i, block) if src is None else src, dst_ref=rows(i, block),
                send_sem=send_sems.at[7 * i + k], recv_sem=recv_sems.at[7 * i + k],
                device_id=to, device_id_type=MESH)

        mine = [pltpu.make_async_copy(ins[i], rows(i, me), local_sems.at[i]) for i in range(n)]
        for cp in mine:
            cp.start()
        first = []
        for i in range(n):
            first.append(copy(i, 0, me, sibling, src=ins[i]))
            first += [copy(i, 1 + j, me, (*chip, c), src=ins[i]) for j, chip in enumerate(chips)]
        for cp in first:
            cp.start()
        passed = []
        for i in range(n):
            for j, chip in enumerate(chips):
                copy(i, 1 + j, (*chip, c), me).wait_recv()
                fwd = copy(i, 4 + j, (*chip, c), sibling)
                fwd.start()
                passed.append(fwd)
        for i in range(n):
            copy(i, 0, sibling, me).wait_recv()
            for j, chip in enumerate(chips):
                copy(i, 4 + j, (*chip, 1 - c), me).wait_recv()
        for cp in first + passed:
            cp.wait_send()
        for cp in mine:
            cp.wait()

    out_shape = []
    for s, a in zip(shards, axes):
        shape = list(s.shape)
        shape[a] *= N_DEV
        out_shape.append(jax.ShapeDtypeStruct(tuple(shape), s.dtype))
    any_spec = pl.BlockSpec(memory_space=pl.ANY)
    return pl.pallas_call(
        body, name=name, out_shape=out_shape,
        in_specs=[any_spec] * n, out_specs=[any_spec] * n,
        scratch_shapes=[pltpu.SemaphoreType.DMA((7 * n,)), pltpu.SemaphoreType.DMA((7 * n,)),
                        pltpu.SemaphoreType.DMA((n,))],
    )(*shards)


HBM_SPEC = pl.BlockSpec(memory_space=pltpu.HBM)
SEM_SPEC = pl.BlockSpec(memory_space=pltpu.SEMAPHORE)
ANY_SPEC = pl.BlockSpec(memory_space=pl.ANY)
SPLIT_PARAMS = pltpu.CompilerParams(has_side_effects=pltpu.SideEffectType.DATAFLOW_SIDE_EFFECTING)
N_PEERS = N_DEV - 1
TOKEN = jax.ShapeDtypeStruct((SUBLANES, LANES), F32)
VMEM_SPEC = pl.BlockSpec(memory_space=pltpu.VMEM)


def _in_hbm(arrays):
    return [pltpu.with_memory_space_constraint(a, pltpu.HBM) for a in arrays]


def _peer(m):
    x, y, c = (lax.axis_index(a) for a in MESH_AXES)
    px = 1 - x if m & 4 else x
    py = 1 - y if m & 2 else y
    pc = 1 - c if m & 1 else c
    return (px, py, pc), 4 * px + 2 * py + pc


def _my_index():
    x, y, c = (lax.axis_index(a) for a in MESH_AXES)
    return 4 * x + 2 * y + c


def _gather_copies(shard_refs, full_refs, axes, send_sems, recv_sems):
    copies = []
    for i, (shard, full) in enumerate(zip(shard_refs, full_refs)):
        mine = _block(full, axes[i], shard.shape[axes[i]], _my_index())
        for m in range(1, N_DEV):
            peer, _ = _peer(m)
            copies.append(pltpu.make_async_remote_copy(
                src_ref=shard, dst_ref=mine, send_sem=send_sems.at[N_PEERS * i + m - 1],
                recv_sem=recv_sems.at[N_PEERS * i + m - 1], device_id=peer, device_id_type=MESH))
    return copies


def _gather_start(name, shards, axes, after):
    n = len(shards)

    def body(*refs):
        shard_refs = refs[:n]
        send_sems, recv_sems, local_sems = refs[n + 1:n + 4]
        full_refs = refs[2 * n + 4:3 * n + 4]
        refs[3 * n + 4][...] = jnp.zeros(TOKEN.shape, TOKEN.dtype)
        for i in range(n):
            pltpu.make_async_copy(shard_refs[i], _block(full_refs[i], axes[i], shard_refs[i].shape[axes[i]], _my_index()),
                                  local_sems.at[i]).start()
        for cp in _gather_copies(shard_refs, full_refs, axes, send_sems, recv_sems):
            cp.start()

    fulls = []
    for s, a in zip(shards, axes):
        shape = list(s.shape)
        shape[a] *= N_DEV
        fulls.append(pltpu.HBM(tuple(shape), s.dtype))
    out = pl.pallas_call(
        body, name=name,
        out_shape=(pltpu.SemaphoreType.DMA((N_PEERS * n,)), pltpu.SemaphoreType.DMA((N_PEERS * n,)),
                   pltpu.SemaphoreType.DMA((n,)), *[pltpu.HBM(s.shape, s.dtype) for s in shards], *fulls, TOKEN),
        in_specs=[HBM_SPEC] * n + [ANY_SPEC],
        out_specs=(SEM_SPEC, SEM_SPEC, SEM_SPEC, *[HBM_SPEC] * (2 * n), VMEM_SPEC),
        input_output_aliases={i: 3 + i for i in range(n)},
        compiler_params=SPLIT_PARAMS,
    )(*_in_hbm(shards), after)
    return out[:-1], out[-1]


def _gather_wait(name, started, indices, axes, after):
    send_sems, recv_sems, local_sems = started[:3]
    n_all = (len(started) - 3) // 2
    shards = [started[3 + i] for i in indices]
    fulls = [started[3 + n_all + i] for i in indices]
    n = len(indices)

    def body(*refs):
        shard_refs, full_refs = refs[:n], refs[n:2 * n]
        send_sems, recv_sems, local_sems = refs[2 * n:2 * n + 3]
        for j, i in enumerate(indices):
            mine = _block(full_refs[j], axes[j], shard_refs[j].shape[axes[j]], _my_index())
            pltpu.make_async_copy(shard_refs[j], mine, local_sems.at[i]).wait()
            for m in range(1, N_DEV):
                peer, _ = _peer(m)
                cp = pltpu.make_async_remote_copy(
                    src_ref=shard_refs[j], dst_ref=mine, send_sem=send_sems.at[N_PEERS * i + m - 1],
                    recv_sem=recv_sems.at[N_PEERS * i + m - 1], device_id=peer, device_id_type=MESH)
                cp.wait_send()
                cp.wait_recv()

    out = pl.pallas_call(
        body, name=name,
        out_shape=tuple(pltpu.HBM(a.shape, a.dtype) for a in shards + fulls),
        in_specs=[HBM_SPEC] * (2 * n) + [SEM_SPEC] * 3 + [ANY_SPEC], out_specs=tuple([HBM_SPEC] * (2 * n)),
        input_output_aliases={i: i for i in range(2 * n)},
        compiler_params=SPLIT_PARAMS,
    )(*shards, *fulls, send_sems, recv_sems, local_sems, after)
    return out[n:]


def _exchange_start(name, fulls, axes):
    n = len(fulls)
    sizes = [f.shape[a] // N_DEV for f, a in zip(fulls, axes)]

    def body(*refs):
        ins = refs[:n]
        send_sems, recv_sems = refs[n:n + 2]
        lands = refs[2 * n + 2:3 * n + 2]
        refs[3 * n + 2][...] = jnp.zeros(TOKEN.shape, TOKEN.dtype)
        for i in range(n):
            for m in range(1, N_DEV):
                peer, index = _peer(m)
                pltpu.make_async_remote_copy(
                    src_ref=_block(ins[i], axes[i], sizes[i], index), dst_ref=lands[i].at[m - 1],
                    send_sem=send_sems.at[N_PEERS * i + m - 1], recv_sem=recv_sems.at[N_PEERS * i + m - 1],
                    device_id=peer, device_id_type=MESH).start()

    lands = []
    for f, a, size in zip(fulls, axes, sizes):
        shape = list(f.shape)
        shape[a] = size
        lands.append(pltpu.HBM((N_PEERS, *shape), f.dtype))
    out = pl.pallas_call(
        body, name=name,
        out_shape=(pltpu.SemaphoreType.DMA((N_PEERS * n,)), pltpu.SemaphoreType.DMA((N_PEERS * n,)),
                   *[pltpu.HBM(f.shape, f.dtype) for f in fulls], *lands, TOKEN),
        in_specs=[HBM_SPEC] * n, out_specs=(SEM_SPEC, SEM_SPEC, *[HBM_SPEC] * (2 * n), VMEM_SPEC),
        input_output_aliases={i: 2 + i for i in range(n)},
        compiler_params=SPLIT_PARAMS,
    )(*_in_hbm(fulls))
    return out[:-1], out[-1]


def _exchange_wait(name, started, axes, after):
    send_sems, recv_sems = started[:2]
    n = (len(started) - 2) // 2
    fulls, lands = list(started[2:2 + n]), list(started[2 + n:])
    sizes = [f.shape[a] // N_DEV for f, a in zip(fulls, axes)]

    def body(*refs):
        ins, land_refs = refs[:n], refs[n:2 * n]
        send_sems, recv_sems = refs[2 * n:2 * n + 2]
        for i in range(n):
            for m in range(1, N_DEV):
                peer, index = _peer(m)
                cp = pltpu.make_async_remote_copy(
                    src_ref=_block(ins[i], axes[i], sizes[i], index), dst_ref=land_refs[i].at[m - 1],
                    send_sem=send_sems.at[N_PEERS * i + m - 1], recv_sem=recv_sems.at[N_PEERS * i + m - 1],
                    device_id=peer, device_id_type=MESH)
                cp.wait_send()
                cp.wait_recv()

    out = pl.pallas_call(
        body, name=name,
        out_shape=tuple(pltpu.HBM(a.shape, a.dtype) for a in fulls + lands),
        in_specs=[HBM_SPEC] * (2 * n) + [SEM_SPEC] * 2 + [ANY_SPEC], out_specs=tuple([HBM_SPEC] * (2 * n)),
        input_output_aliases={i: i for i in range(2 * n)},
        compiler_params=SPLIT_PARAMS,
    )(*fulls, *lands, send_sems, recv_sems, after)
    return out[:n], out[n:]


def _adamw_update(w_ref, m_ref, v_ref, part_refs, g_ref, d_ref, nm_ref, nv_ref):
    c1 = 1.0 - ADAM_B1 ** ADAM_STEP
    c2 = 1.0 - ADAM_B2 ** ADAM_STEP
    g = None
    for p_ref in part_refs:
        stacked = len(p_ref.shape) > len(w_ref.shape)
        terms = [p_ref[s] for s in range(p_ref.shape[0])] if stacked else [p_ref[...]]
        for term in terms:
            term = term.astype(F32)
            g = term if g is None else g + term
    new_m = ADAM_B1 * m_ref[...] + (1.0 - ADAM_B1) * g
    new_v = ADAM_B2 * v_ref[...] + (1.0 - ADAM_B2) * (g * g)
    g_ref[...] = g
    nm_ref[...] = new_m
    nv_ref[...] = new_v
    d_ref[...] = -ADAM_LR * ((new_m / c1) / (jnp.sqrt(new_v / c2) + ADAM_EPS) + ADAM_WD * w_ref[...])


def _adamw_small(ws, ms, vs, stacks):
    n = len(ws)

    def body(*refs):
        ins, outs = refs[:4 * n], refs[4 * n:]
        for i in range(n):
            _adamw_update(ins[i], ins[n + i], ins[2 * n + i], [ins[3 * n + i]],
                          outs[i], outs[n + i], outs[2 * n + i], outs[3 * n + i])

    res = pl.pallas_call(
        body, name="adamw_small",
        out_shape=[jax.ShapeDtypeStruct(w.shape, F32) for w in ws] * 4,
        compiler_params=_params(None),
    )(*ws, *ms, *vs, *stacks)
    return res[:n], res[n:2 * n], res[2 * n:3 * n], res[3 * n:]


def _adamw(name, w, m, v, parts):
    r, c = w.shape
    tr = _tile(r, 256)
    n_parts = len(parts)

    def body(*refs):
        _adamw_update(refs[0], refs[1], refs[2], refs[3:3 + n_parts], *refs[3 + n_parts:])

    row = pl.BlockSpec((tr, c), lambda i: (i, 0))
    in_specs = [row, row, row]
    for p in parts:
        in_specs.append(row if p.ndim == 2 else pl.BlockSpec((p.shape[0], tr, c), lambda i: (0, i, 0)))
    return pl.pallas_call(
        body, name=name, grid=(r // tr,), in_specs=in_specs, out_specs=[row] * 4,
        out_shape=[jax.ShapeDtypeStruct((r, c), F32)] * 4,
        compiler_params=_params(("arbitrary",)),
    )(w, m, v, *parts)


SMALL = ("norm_gain", "pool_scale", "a_re", "a_im", "log_dt", "b_re", "b_im", "c_re", "c_im", "d_skip", "final_gain")
LARGE = ("w_in", "w_pool", "w_glu", "w_out", "w_ple", "w_ple_gate")
LARGE_AXIS = {"w_in": 1, "w_pool": 1, "w_glu": 1, "w_out": 0, "w_ple": 1, "w_ple_gate": 0}
WEIGHTS = ("norm_gain", "w_in", "w_pool", "pool_scale", "a_re", "a_im", "log_dt", "b_re", "b_im", "c_re", "c_im",
           "d_skip", "w_glu", "w_out", "w_ple", "w_ple_gate", "final_gain")


def kernel(x, p, norm_gain, w_in, w_pool, pool_scale, a_re, a_im, log_dt, b_re, b_im, c_re, c_im, d_skip, w_glu, w_out, w_ple, w_ple_gate, final_gain, loss_target, m_norm_gain, m_w_in, m_w_pool, m_pool_scale, m_a_re, m_a_im, m_log_dt, m_b_re, m_b_im, m_c_re, m_c_im, m_d_skip, m_w_glu, m_w_out, m_w_ple, m_w_ple_gate, m_final_gain, v_norm_gain, v_w_in, v_w_pool, v_pool_scale, v_a_re, v_a_im, v_log_dt, v_b_re, v_b_im, v_c_re, v_c_im, v_d_skip, v_w_glu, v_w_out, v_w_ple, v_w_ple_gate, v_final_gain):
    weights = dict(norm_gain=norm_gain, w_in=w_in, w_pool=w_pool, pool_scale=pool_scale, a_re=a_re, a_im=a_im,
                   log_dt=log_dt, b_re=b_re, b_im=b_im, c_re=c_re, c_im=c_im, d_skip=d_skip, w_glu=w_glu,
                   w_out=w_out, w_ple=w_ple, w_ple_gate=w_ple_gate, final_gain=final_gain)
    mom_m = dict(norm_gain=m_norm_gain, w_in=m_w_in, w_pool=m_w_pool, pool_scale=m_pool_scale, a_re=m_a_re,
                 a_im=m_a_im, log_dt=m_log_dt, b_re=m_b_re, b_im=m_b_im, c_re=m_c_re, c_im=m_c_im,
                 d_skip=m_d_skip, w_glu=m_w_glu, w_out=m_w_out, w_ple=m_w_ple, w_ple_gate=m_w_ple_gate,
                 final_gain=m_final_gain)
    mom_v = dict(norm_gain=v_norm_gain, w_in=v_w_in, w_pool=v_w_pool, pool_scale=v_pool_scale, a_re=v_a_re,
                 a_im=v_a_im, log_dt=v_log_dt, b_re=v_b_re, b_im=v_b_im, c_re=v_c_re, c_im=v_c_im,
                 d_skip=v_d_skip, w_glu=v_w_glu, w_out=v_w_out, w_ple=v_w_ple, w_ple_gate=v_w_ple_gate,
                 final_gain=v_final_gain)

    t = x.shape[1]
    xs = x.reshape(t, D_MODEL)
    ps = p.reshape(t, PLE_DIM)
    target = loss_target.reshape(t, D_MODEL)
    gain1 = norm_gain.reshape(1, D_MODEL)
    gain_f = final_gain.reshape(1, D_MODEL)
    scale_p = pool_scale.reshape(1, POOL_WIDTH)
    skip = d_skip.reshape(1, SSM_WIDTH)

    shard2d = {k: weights[k][0] for k in LARGE}
    shard_bf = {k: shard2d[k].astype(BF16) for k in LARGE}
    full = {"w_in": _all_gather("w_in_all_gather", [shard_bf["w_in"]], [LARGE_AXIS["w_in"]])[0]}
    later = [k for k in LARGE if k != "w_in"]
    later_axes = [LARGE_AXIS[k] for k in later]
    gather, gather_token = _gather_start("weights_gather_start", [shard_bf[k] for k in later], later_axes,
                                         full["w_in"])

    def arrive(k, after):
        i = later.index(k)
        full[k] = _gather_wait("gather_wait_" + k, gather, [i], [later_axes[i]], after)[0]

    ar, ai = a_re[0], a_im[0]
    ldt = log_dt.reshape(N_SSM_GROUPS, 1)
    br_t = jnp.transpose(b_re[0], (0, 2, 1))
    bi_t = jnp.transpose(b_im[0], (0, 2, 1))
    ab_re, ab_im, bb_re, bb_im = _ssm_params(ar, ai, ldt, br_t, bi_t)
    ab_re_t = ab_re.reshape(STATE_TILES, LANES)
    ab_im_t = ab_im.reshape(STATE_TILES, LANES)
    wbd_re = _blockdiag_in(bb_re)
    wbd_im = _blockdiag_in(bb_im)
    cbd_re = _blockdiag_in(c_re[0])
    cbd_imneg = _blockdiag_in(-c_im[0])

    hn = _norm1_fwd(xs, gain1)
    proj = _mm_nn("in_proj", hn, full["w_in"], [F32], tk=2048, after=[gather_token])[0]
    pooled = _pool_fwd(proj)
    tm = _tile(t, 1024)
    arrive("w_pool", pooled)
    mixed = _mm("pool_mix", [(pooled, (tm, POOL_GROUP), lambda i, j, s: (i, j),
                              full["w_pool"], (None, POOL_GROUP, POOL_GROUP), lambda i, j, s: (j, 0, 0))],
                DOT_NN, (t // tm, N_POOL_GROUPS, 1),
                [((t, POOL_WIDTH), F32, (tm, POOL_GROUP), lambda i, j, s: (i, j))], 1)[0]
    bu_re = _ssm_in_proj("ssm_in_proj_re", proj, wbd_re)
    bu_im = _ssm_in_proj("ssm_in_proj_im", proj, wbd_im)
    s_re, s_im = _scan_fwd(bu_re, bu_im, ab_re_t, ab_im_t)
    y, gel = _ssm_out_proj(s_re, s_im, cbd_re, cbd_imneg, proj, skip)
    arrive("w_glu", gel)
    hg = _mm_nn("glu_proj", gel, full["w_glu"], [F32])[0]
    cat = _gate_fwd(mixed, proj, hg, scale_p)

    def residual_epilogue(acc, ex, out_refs):
        h = acc + ex[0][...]
        out_refs[0][...] = h
        out_refs[1][...] = h.astype(BF16)

    arrive("w_out", cat)
    h1, h1b = _mm_nn("out_proj", cat, full["w_out"], [F32, BF16], extras=[xs], epilogue=residual_epilogue)
    arrive("w_ple", h1b)
    e = _mm_nn("ple_proj", ps, full["w_ple"], [F32])[0]
    arrive("w_ple_gate", e)
    q = _mm_nn("ple_gate_proj", h1b, full["w_ple_gate"], [F32], tk=2048)[0]
    de, dq, dh2, g_final_gain, loss_part = _final(h1, e, q, target, gain_f)
    loss = lax.psum(loss_part[0, 0], MESH_AXES)

    grads = {}
    grads["w_ple_gate"] = _mm_tn("ple_gate_wgrad", h1b, dq, BF16)
    grads["w_ple"] = _mm_tn("ple_wgrad", ps, de, BF16)
    sent, tokens = {}, {}

    def send(names):
        sent[names], tokens[names[0]] = _exchange_start(
            "grads_start_" + names[0], [grads[k] for k in names], [LARGE_AXIS[k] for k in names])

    send(("w_ple_gate", "w_ple"))
    dh1, dh1b = _mm_nt("ple_gate_dgrad", dq, full["w_ple_gate"], [F32, BF16], extras=[dh2],
                       epilogue=residual_epilogue)
    grads["w_out"] = _mm_tn("out_wgrad", cat, dh1b, BF16)
    send(("w_out",))
    dcat = _mm_nt("out_dgrad", dh1b, full["w_out"], [F32], tk=2048, after=[tokens["w_ple_gate"], tokens["w_out"]])[0]
    dmixed, dga, dgb, dhg, g_pool_scale = _gate_bwd(dcat, mixed, proj, hg, scale_p)

    tk = _tile(t, 1024)
    grads["w_pool"] = _mm("pool_wgrad", [(pooled, (tk, POOL_GROUP), lambda i, j, s: (s, i),
                                          dmixed, (tk, POOL_GROUP), lambda i, j, s: (s, i))],
                          DOT_TN, (N_POOL_GROUPS, 1, t // tk),
                          [((N_POOL_GROUPS, POOL_GROUP, POOL_GROUP), BF16, (None, POOL_GROUP, POOL_GROUP),
                            lambda i, j, s: (i, 0, 0))], t // tk)[0]
    dpooled = _mm("pool_dgrad", [(dmixed, (tm, POOL_GROUP), lambda i, j, s: (i, j),
                                  full["w_pool"], (None, POOL_GROUP, POOL_GROUP), lambda i, j, s: (j, 0, 0))],
                  DOT_NT, (t // tm, N_POOL_GROUPS, 1),
                  [((t, POOL_WIDTH), F32, (tm, POOL_GROUP), lambda i, j, s: (i, j))], 1)[0]
    dua = _pool_bwd(dpooled)

    grads["w_glu"] = _mm_tn("glu_wgrad", gel, dhg, BF16)
    send(("w_pool", "w_glu"))

    def gelu_bwd_epilogue(acc, ex, out_refs):
        yv = ex[0][...]
        th = jnp.tanh(GELU_C * (yv + GELU_A * yv * yv * yv))
        dgelu = 0.5 * (1.0 + th) + 0.5 * yv * (1.0 - th * th) * GELU_C * (1.0 + 3.0 * GELU_A * yv * yv)
        out_refs[0][...] = acc * dgelu

    dy = _mm_nt("glu_dgrad", dhg, full["w_glu"], [F32], tk=2048, extras=[y], epilogue=gelu_bwd_epilogue,
                after=[tokens["w_pool"]])[0]
    ds_re = _ssm_dstate("ssm_dstate_re", dy, cbd_re)
    ds_im = _ssm_dstate("ssm_dstate_im", dy, cbd_imneg)
    g_cbd_re = _ssm_wgrad("ssm_c_re_wgrad", dy, 0, s_re)
    g_cbd_imneg = _ssm_wgrad("ssm_c_im_wgrad", dy, 0, s_im)
    z_re, z_im, g_ab_re_t, g_ab_im_t = _scan_bwd(ds_re, ds_im, s_re, s_im, ab_re_t, ab_im_t)
    u_col0 = 2 * POOL_WIDTH // CHUNK_IN
    g_wbd_re = _ssm_wgrad("ssm_b_re_wgrad", proj, u_col0, z_re)
    g_wbd_im = _ssm_wgrad("ssm_b_im_wgrad", proj, u_col0, z_im)
    dub = _ssm_din(z_re, z_im, wbd_re, wbd_im, dy, skip)

    g_ab_re = g_ab_re_t.reshape(N_SSM_GROUPS, SSM_STATE)
    g_ab_im = g_ab_im_t.reshape(N_SSM_GROUPS, SSM_STATE)
    d_ar, d_ai, d_ldt, d_br_t, d_bi_t = _ssm_params_bwd(
        ar, ai, ldt, br_t, bi_t, g_ab_re, g_ab_im, _diag_in(g_wbd_re), _diag_in(g_wbd_im))

    dproj = jnp.concatenate([dua, dga, dub, dgb], axis=1)
    grads["w_in"] = _mm_tn("in_wgrad", hn, dproj, BF16)
    send(("w_in",))
    dhn = _mm_nt("in_dgrad", dproj, full["w_in"], [F32], tk=2048, after=[tokens["w_in"]])[0]
    grad_x, g_norm_gain = _norm1_bwd(xs, dhn, dh1, gain1)
    g_d_skip = _skip_grad(dy, proj)

    def b_view(a):
        return jnp.transpose(a[0], (0, 2, 1))

    views = dict(norm_gain=lambda a: a, pool_scale=lambda a: a, a_re=lambda a: a[0], a_im=lambda a: a[0],
                 log_dt=lambda a: a, b_re=b_view, b_im=b_view, c_re=lambda a: a[0], c_im=lambda a: a[0],
                 d_skip=lambda a: a, final_gain=lambda a: a.reshape(1, D_MODEL))
    small_grads = dict(
        norm_gain=g_norm_gain, pool_scale=g_pool_scale, a_re=d_ar, a_im=d_ai, log_dt=d_ldt.reshape(1, N_SSM_GROUPS),
        b_re=d_br_t, b_im=d_bi_t, c_re=_diag_in(g_cbd_re), c_im=-_diag_in(g_cbd_imneg), d_skip=g_d_skip,
        final_gain=g_final_gain)
    stacks = _all_gather("small_grads_all_gather", [small_grads[k][None] for k in SMALL], [0] * len(SMALL))
    small_out = _adamw_small([views[k](weights[k]) for k in SMALL], [views[k](mom_m[k]) for k in SMALL],
                             [views[k](mom_v[k]) for k in SMALL], stacks)
    out_g, out_d, out_m, out_v = ({} for _ in range(4))
    for out, res in zip((out_g, out_d, out_m, out_v), small_out):
        for k, r in zip(SMALL, res):
            if k in ("b_re", "b_im"):
                r = jnp.transpose(r, (0, 2, 1))
            out[k] = r.reshape(weights[k].shape)

    me = 4 * lax.axis_index("x") + 2 * lax.axis_index("y") + lax.axis_index("c")
    after = grad_x
    for names, started in sent.items():
        axes = [LARGE_AXIS[k] for k in names]
        partials, landed = _exchange_wait("grads_wait_" + names[0], started, axes, after)
        for k, axis, partial, land in zip(names, axes, partials, landed):
            shard_shape = shard2d[k].shape
            size = shard_shape[axis]
            own = lax.dynamic_slice_in_dim(partial, me * size, size, axis=axis)
            view = (-1, shard_shape[-1])
            rows = math.prod(shard_shape[:-1])
            res = _adamw("adamw_" + k, shard2d[k].reshape(view), mom_m[k][0].reshape(view), mom_v[k][0].reshape(view),
                         [own.reshape(view), land.reshape(N_PEERS, rows, shard_shape[-1])])
            out_g[k], out_d[k], out_m[k], out_v[k] = (r.reshape(weights[k].shape) for r in res)
            after = res[0]

    return (loss, grad_x.reshape(x.shape), *[out_g[k] for k in WEIGHTS], *[out_d[k] for k in WEIGHTS],
            *[out_m[k] for k in WEIGHTS], *[out_v[k] for k in WEIGHTS])


def _skip_grad(dy, proj):
    t = dy.shape[0]
    tm = _tile(t, 512)

    def body(dy_ref, u_ref, o_ref):
        @pl.when(pl.program_id(0) == 0)
        def _():
            o_ref[...] = jnp.zeros_like(o_ref)

        o_ref[...] += jnp.sum(dy_ref[...] * u_ref[...], axis=0, keepdims=True)

    return pl.pallas_call(
        body, name="skip_grad", grid=(t // tm,),
        in_specs=[pl.BlockSpec((tm, SSM_WIDTH), lambda i: (i, 0)), pl.BlockSpec((tm, SSM_WIDTH), lambda i: (i, 2))],
        out_specs=pl.BlockSpec((1, SSM_WIDTH), lambda i: (0, 0)),
        out_shape=jax.ShapeDtypeStruct((1, SSM_WIDTH), F32),
        compiler_params=_params(("arbitrary",)),
    )(dy, proj)
```

```python
import functools
import math

import jax
import jax.numpy as jnp
from jax import lax
from jax.experimental import pallas as pl
from jax.experimental.pallas import tpu as pltpu

F32 = jnp.float32
BF16 = jnp.bfloat16
MESH = pl.DeviceIdType.MESH
MESH_AXES = ("x", "y", "c")
N_DEV = 8

D_MODEL = 2048
POOL_WIDTH = 1024
SSM_WIDTH = 1024
N_POOL_GROUPS = 4
POOL_GROUP = 256
SSM_GROUP = 16
N_SSM_GROUPS = 64
SSM_STATE = 64
SSM_FLAT = N_SSM_GROUPS * SSM_STATE
SSM_CHUNKS = 4
CHUNK_IN = SSM_WIDTH // SSM_CHUNKS
CHUNK_STATE = SSM_FLAT // SSM_CHUNKS
PLE_DIM = 256
EPS = 1e-6
A_RE_MAX = -1e-4
ADAM_LR = 0.001
ADAM_B1 = 0.9
ADAM_B2 = 0.999
ADAM_EPS = 1e-08
ADAM_WD = 0.01
ADAM_STEP = 10
GELU_C = math.sqrt(2.0 / math.pi)
GELU_A = 0.044715

SUBLANES = 8
LANES = 128
VMEM_LIMIT_BYTES = 48 * 1024 * 1024

DOT_NN = (((1,), (0,)), ((), ()))
DOT_NT = (((1,), (1,)), ((), ()))
DOT_TN = (((0,), (0,)), ((), ()))


def _tile(n, pref):
    return pref if n % pref == 0 else n


def _params(sem):
    return pltpu.CompilerParams(dimension_semantics=sem, vmem_limit_bytes=VMEM_LIMIT_BYTES)


def _sigmoid(v):
    return 1.0 / (1.0 + jnp.exp(-v))


def _silu_and_grad(v):
    s = _sigmoid(v)
    return v * s, s * (1.0 + v * (1.0 - s))


def _mm(name, pairs, dims, grid, outs, k_steps, extras=(), epilogue=None):
    n_pairs, n_ex, n_out = len(pairs), len(extras), len(outs)
    acc_shape = tuple(d for d in outs[0][2] if d is not None)
    if epilogue is None:
        def epilogue(acc, ex, out_refs):
            out_refs[0][...] = acc.astype(out_refs[0].dtype)

    def body(*refs):
        ab = refs[:2 * n_pairs]
        ex = refs[2 * n_pairs:2 * n_pairs + n_ex]
        out_refs = refs[2 * n_pairs + n_ex:2 * n_pairs + n_ex + n_out]
        acc = refs[-1]
        k = pl.program_id(2)

        @pl.when(k == 0)
        def _():
            acc[...] = jnp.zeros_like(acc)

        part = None
        for q in range(n_pairs):
            d = lax.dot_general(ab[2 * q][...].astype(BF16), ab[2 * q + 1][...].astype(BF16), dims,
                                preferred_element_type=F32)
            part = d if part is None else part + d
        acc[...] += part

        @pl.when(k == k_steps - 1)
        def _():
            epilogue(acc[...], ex, out_refs)

    in_specs, operands = [], []
    for a, a_blk, a_map, b, b_blk, b_map in pairs:
        in_specs += [pl.BlockSpec(a_blk, a_map), pl.BlockSpec(b_blk, b_map)]
        operands += [a, b]
    for e, e_blk, e_map in extras:
        in_specs.append(pl.BlockSpec(e_blk, e_map))
        operands.append(e)
    return pl.pallas_call(
        body, name=name, grid=grid, in_specs=in_specs,
        out_specs=[pl.BlockSpec(o[2], o[3]) for o in outs],
        out_shape=[jax.ShapeDtypeStruct(o[0], o[1]) for o in outs],
        scratch_shapes=[pltpu.VMEM(acc_shape, F32)],
        compiler_params=_params(("arbitrary", "arbitrary", "arbitrary")),
    )(*operands)


def _after(tokens):
    return [(tok, tok.shape, lambda i, j, s: (0, 0)) for tok in tokens]


def _mm_nn(name, a, b, out_dtypes, tm=1024, tn=1024, tk=1024, a_col0=0, extras=(), epilogue=None, after=()):
    m, n = a.shape[0], b.shape[1]
    k = b.shape[0]
    tm, tn, tk = _tile(m, tm), _tile(n, tn), _tile(k, tk)
    outs = [((m, n), dt, (tm, tn), lambda i, j, s: (i, j)) for dt in out_dtypes]
    ex = [(e, (tm, tn), lambda i, j, s: (i, j)) for e in extras] + _after(after)
    return _mm(name, [(a, (tm, tk), lambda i, j, s: (i, a_col0 + s), b, (tk, tn), lambda i, j, s: (s, j))],
               DOT_NN, (m // tm, n // tn, k // tk), outs, k // tk, ex, epilogue)


def _mm_nt(name, a, b, out_dtypes, tm=1024, tn=1024, tk=1024, extras=(), epilogue=None, after=()):
    m, kk = a.shape
    n = b.shape[0]
    tm, tn, tk = _tile(m, tm), _tile(n, tn), _tile(kk, tk)
    outs = [((m, n), dt, (tm, tn), lambda i, j, s: (i, j)) for dt in out_dtypes]
    ex = [(e, (tm, tn), lambda i, j, s: (i, j)) for e in extras] + _after(after)
    return _mm(name, [(a, (tm, tk), lambda i, j, s: (i, s), b, (tn, tk), lambda i, j, s: (j, s))],
               DOT_NT, (m // tm, n // tn, kk // tk), outs, kk // tk, ex, epilogue)


def _mm_tn(name, a, b, out_dtype, tm=512, tn=2048, tk=1024):
    m, kk = a.shape
    n = b.shape[1]
    tm, tn, tk = _tile(kk, tm), _tile(n, tn), _tile(m, tk)
    outs = [((kk, n), out_dtype, (tm, tn), lambda i, j, s: (i, j))]
    return _mm(name, [(a, (tk, tm), lambda i, j, s: (s, i), b, (tk, tn), lambda i, j, s: (s, j))],
               DOT_TN, (kk // tm, n // tn, m // tk), outs, m // tk)[0]


def _norm1_fwd(x, gain):
    t = x.shape[0]
    tm = _tile(t, 512)

    def body(x_ref, g_ref, hn_ref):
        xv = x_ref[...]
        r = lax.rsqrt(jnp.mean(xv * xv, axis=-1, keepdims=True) + EPS)
        hn_ref[...] = (xv * r * g_ref[...]).astype(BF16)

    return pl.pallas_call(
        body, name="norm1_fwd", grid=(t // tm,),
        in_specs=[pl.BlockSpec((tm, D_MODEL), lambda i: (i, 0)), pl.BlockSpec((1, D_MODEL), lambda i: (0, 0))],
        out_specs=pl.BlockSpec((tm, D_MODEL), lambda i: (i, 0)),
        out_shape=jax.ShapeDtypeStruct((t, D_MODEL), BF16),
        compiler_params=_params(("arbitrary",)),
    )(x, gain)


def _norm1_bwd(x, dhn, dh1, gain):
    t = x.shape[0]
    tm = _tile(t, 512)

    def body(x_ref, dhn_ref, dh1_ref, g_ref, dx_ref, gg_ref):
        @pl.when(pl.program_id(0) == 0)
        def _():
            gg_ref[...] = jnp.zeros_like(gg_ref)

        xv = x_ref[...]
        r = lax.rsqrt(jnp.mean(xv * xv, axis=-1, keepdims=True) + EPS)
        xh = xv * r
        dhn_v = dhn_ref[...]
        gg_ref[...] += jnp.sum(dhn_v * xh, axis=0, keepdims=True)
        dxh = dhn_v * g_ref[...]
        dx_ref[...] = dh1_ref[...] + r * (dxh - xh * jnp.mean(dxh * xh, axis=-1, keepdims=True))

    row = pl.BlockSpec((tm, D_MODEL), lambda i: (i, 0))
    vec = pl.BlockSpec((1, D_MODEL), lambda i: (0, 0))
    return pl.pallas_call(
        body, name="norm1_bwd", grid=(t // tm,),
        in_specs=[row, row, row, vec], out_specs=[row, vec],
        out_shape=[jax.ShapeDtypeStruct((t, D_MODEL), F32), jax.ShapeDtypeStruct((1, D_MODEL), F32)],
        compiler_params=_params(("arbitrary",)),
    )(x, dhn, dh1, gain)


def _pool_counts(t, width, group):
    row = lax.broadcasted_iota(jnp.int32, (t, width), 0)
    window = jnp.left_shift(jnp.int32(2), group)
    return row, jnp.minimum(row + 1, window).astype(F32)


def _select_window(group, s2, s4, s8, s16):
    return jnp.where(group == 0, s2, jnp.where(group == 1, s4, jnp.where(group == 2, s8, s16)))


def _pool_fwd(proj):
    t = proj.shape[0]
    tc = LANES

    def body(u_ref, o_ref):
        group = pl.program_id(0) // (POOL_GROUP // tc)
        v = u_ref[...]
        row, count = _pool_counts(t, tc, group)

        def down(a, j):
            return jnp.where(row >= j, pltpu.roll(a, j, 0), 0.0)

        s2 = v + down(v, 1)
        s4 = s2 + down(s2, 2)
        s8 = s4 + down(s4, 4)
        s16 = s8 + down(s8, 8)
        o_ref[...] = (_select_window(group, s2, s4, s8, s16) / count - v).astype(BF16)

    return pl.pallas_call(
        body, name="pool_fwd", grid=(POOL_WIDTH // tc,),
        in_specs=[pl.BlockSpec((t, tc), lambda j: (0, j))],
        out_specs=pl.BlockSpec((t, tc), lambda j: (0, j)),
        out_shape=jax.ShapeDtypeStruct((t, POOL_WIDTH), BF16),
        compiler_params=_params(("arbitrary",)),
    )(proj)


def _pool_bwd(dpooled):
    t = dpooled.shape[0]
    tc = LANES

    def body(d_ref, o_ref):
        group = pl.program_id(0) // (POOL_GROUP // tc)
        dp = d_ref[...]
        row, count = _pool_counts(t, tc, group)
        r = dp / count

        def up(a, j):
            return jnp.where(row < t - j, pltpu.roll(a, t - j, 0), 0.0)

        s2 = r + up(r, 1)
        s4 = s2 + up(s2, 2)
        s8 = s4 + up(s4, 4)
        s16 = s8 + up(s8, 8)
        o_ref[...] = (_select_window(group, s2, s4, s8, s16) - dp).astype(BF16)

    return pl.pallas_call(
        body, name="pool_bwd", grid=(POOL_WIDTH // tc,),
        in_specs=[pl.BlockSpec((t, tc), lambda j: (0, j))],
        out_specs=pl.BlockSpec((t, tc), lambda j: (0, j)),
        out_shape=jax.ShapeDtypeStruct((t, POOL_WIDTH), BF16),
        compiler_params=_params(("arbitrary",)),
    )(dpooled)


def _gate_fwd(mixed, proj, hg, pool_scale):
    t = mixed.shape[0]
    tm = _tile(t, 512)

    def body(mx_ref, ga_ref, gb_ref, hg_ref, ps_ref, cat_ref):
        silu_a, _ = _silu_and_grad(ga_ref[...])
        cat_ref[:, :POOL_WIDTH] = (mx_ref[...] * ps_ref[...] * silu_a).astype(BF16)
        silu_b, _ = _silu_and_grad(gb_ref[...])
        sb = hg_ref[:, :SSM_WIDTH] * _sigmoid(hg_ref[:, SSM_WIDTH:])
        cat_ref[:, POOL_WIDTH:] = (sb * silu_b).astype(BF16)

    return pl.pallas_call(
        body, name="gate_fwd", grid=(t // tm,),
        in_specs=[pl.BlockSpec((tm, POOL_WIDTH), lambda i: (i, 0)),
                  pl.BlockSpec((tm, POOL_WIDTH), lambda i: (i, 1)),
                  pl.BlockSpec((tm, SSM_WIDTH), lambda i: (i, 3)),
                  pl.BlockSpec((tm, 2 * SSM_WIDTH), lambda i: (i, 0)),
                  pl.BlockSpec((1, POOL_WIDTH), lambda i: (0, 0))],
        out_specs=pl.BlockSpec((tm, D_MODEL), lambda i: (i, 0)),
        out_shape=jax.ShapeDtypeStruct((t, D_MODEL), BF16),
        compiler_params=_params(("arbitrary",)),
    )(mixed, proj, proj, hg, pool_scale)


def _gate_bwd(dcat, mixed, proj, hg, pool_scale):
    t = mixed.shape[0]
    tm = _tile(t, 512)

    def body(dc_ref, mx_ref, ga_ref, gb_ref, hg_ref, ps_ref, dmx_ref, dga_ref, dgb_ref, dhg_ref, gps_ref):
        @pl.when(pl.program_id(0) == 0)
        def _():
            gps_ref[...] = jnp.zeros_like(gps_ref)

        ps = ps_ref[...]
        mx = mx_ref[...]
        dya = dc_ref[:, :POOL_WIDTH]
        silu_a, dsilu_a = _silu_and_grad(ga_ref[...])
        dpa = dya * silu_a
        gps_ref[...] += jnp.sum(dpa * mx, axis=0, keepdims=True)
        dmx_ref[...] = (dpa * ps).astype(BF16)
        dga_ref[...] = (dya * mx * ps * dsilu_a).astype(BF16)

        dyb = dc_ref[:, POOL_WIDTH:]
        silu_b, dsilu_b = _silu_and_grad(gb_ref[...])
        h_a = hg_ref[:, :SSM_WIDTH]
        sg = _sigmoid(hg_ref[:, SSM_WIDTH:])
        dsb = dyb * silu_b
        dgb_ref[...] = (dyb * h_a * sg * dsilu_b).astype(BF16)
        dhg_ref[:, :SSM_WIDTH] = (dsb * sg).astype(BF16)
        dhg_ref[:, SSM_WIDTH:] = (dsb * h_a * sg * (1.0 - sg)).astype(BF16)

    half = pl.BlockSpec((tm, POOL_WIDTH), lambda i: (i, 0))
    full = pl.BlockSpec((tm, D_MODEL), lambda i: (i, 0))
    vec = pl.BlockSpec((1, POOL_WIDTH), lambda i: (0, 0))
    return pl.pallas_call(
        body, name="gate_bwd", grid=(t // tm,),
        in_specs=[full, half,
                  pl.BlockSpec((tm, POOL_WIDTH), lambda i: (i, 1)),
                  pl.BlockSpec((tm, SSM_WIDTH), lambda i: (i, 3)),
                  full, vec],
        out_specs=[half, half, half, full, vec],
        out_shape=[jax.ShapeDtypeStruct((t, POOL_WIDTH), BF16), jax.ShapeDtypeStruct((t, POOL_WIDTH), BF16),
                   jax.ShapeDtypeStruct((t, SSM_WIDTH), BF16), jax.ShapeDtypeStruct((t, 2 * SSM_WIDTH), BF16),
                   jax.ShapeDtypeStruct((1, POOL_WIDTH), F32)],
        compiler_params=_params(("arbitrary",)),
    )(dcat, mixed, proj, proj, hg, pool_scale)


def _final(h1, e, q, target, gain):
    t = h1.shape[0]
    tm = _tile(t, 256)

    def body(h1_ref, e_ref, q_ref, tg_ref, g_ref, de_ref, dq_ref, dh2_ref, gg_ref, loss_ref):
        @pl.when(pl.program_id(0) == 0)
        def _():
            gg_ref[...] = jnp.zeros_like(gg_ref)
            loss_ref[...] = jnp.zeros_like(loss_ref)

        ev = e_ref[...]
        sg = _sigmoid(q_ref[...])
        h2 = h1_ref[...] + ev * sg
        r = lax.rsqrt(jnp.mean(h2 * h2, axis=-1, keepdims=True) + EPS)
        n = h2 * r
        gain_v = g_ref[...]
        diff = n * gain_v - tg_ref[...]
        row_loss = jnp.sum(diff * diff, axis=-1, keepdims=True)
        loss_ref[...] += (0.5 / D_MODEL) * jnp.sum(row_loss, axis=0, keepdims=True)
        dout = diff * (1.0 / D_MODEL)
        gg_ref[...] += jnp.sum(dout * n, axis=0, keepdims=True)
        dn = dout * gain_v
        dh2 = r * (dn - n * jnp.mean(dn * n, axis=-1, keepdims=True))
        dh2_ref[...] = dh2
        de_ref[...] = (dh2 * sg).astype(BF16)
        dq_ref[...] = (dh2 * ev * sg * (1.0 - sg)).astype(BF16)

    row = pl.BlockSpec((tm, D_MODEL), lambda i: (i, 0))
    vec = pl.BlockSpec((1, D_MODEL), lambda i: (0, 0))
    return pl.pallas_call(
        body, name="final_norm_loss", grid=(t // tm,),
        in_specs=[row, row, row, row, vec],
        out_specs=[row, row, row, vec, pl.BlockSpec((1, 1), lambda i: (0, 0))],
        out_shape=[jax.ShapeDtypeStruct((t, D_MODEL), BF16), jax.ShapeDtypeStruct((t, D_MODEL), BF16),
                   jax.ShapeDtypeStruct((t, D_MODEL), F32), jax.ShapeDtypeStruct((1, D_MODEL), F32),
                   jax.ShapeDtypeStruct((1, 1), F32)],
        compiler_params=_params(("arbitrary",)),
    )(h1, e, q, target, gain)


def _zoh(a_re, a_im, log_dt, b_re_t, b_im_t):
    lam_re = jnp.minimum(a_re, A_RE_MAX)
    lam_im = a_im
    dt = jnp.exp(log_dt)
    mag = jnp.exp(lam_re * dt)
    ang = lam_im * dt
    ab_re = mag * jnp.cos(ang)
    ab_im = mag * jnp.sin(ang)
    den = lam_re * lam_re + lam_im * lam_im
    n_re = ab_re - 1.0
    n_im = ab_im
    q_re = (n_re * lam_re + n_im * lam_im) / den
    q_im = (n_im * lam_re - n_re * lam_im) / den
    bb_re = q_re[:, None, :] * b_re_t - q_im[:, None, :] * b_im_t
    bb_im = q_re[:, None, :] * b_im_t + q_im[:, None, :] * b_re_t
    return ab_re, ab_im, bb_re, bb_im


def _ssm_params(a_re, a_im, log_dt, b_re_t, b_im_t):
    def body(are_ref, aim_ref, dt_ref, bre_ref, bim_ref, abre_ref, abim_ref, bbre_ref, bbim_ref):
        ab_re, ab_im, bb_re, bb_im = _zoh(are_ref[...], aim_ref[...], dt_ref[...], bre_ref[...], bim_ref[...])
        abre_ref[...] = ab_re
        abim_ref[...] = ab_im
        bbre_ref[...] = bb_re
        bbim_ref[...] = bb_im

    return pl.pallas_call(
        body, name="ssm_params",
        out_shape=[jax.ShapeDtypeStruct(a_re.shape, F32), jax.ShapeDtypeStruct(a_re.shape, F32),
                   jax.ShapeDtypeStruct(b_re_t.shape, F32), jax.ShapeDtypeStruct(b_re_t.shape, F32)],
        compiler_params=_params(None),
    )(a_re, a_im, log_dt, b_re_t, b_im_t)


def _ssm_params_bwd(a_re, a_im, log_dt, b_re_t, b_im_t, g_ab_re, g_ab_im, g_bb_re, g_bb_im):
    def body(are_ref, aim_ref, dt_ref, bre_ref, bim_ref, gar_ref, gai_ref, gbr_ref, gbi_ref,
             o_are, o_aim, o_dt, o_bre, o_bim):
        _, vjp = jax.vjp(_zoh, are_ref[...], aim_ref[...], dt_ref[...], bre_ref[...], bim_ref[...])
        d_are, d_aim, d_dt, d_bre, d_bim = vjp((gar_ref[...], gai_ref[...], gbr_ref[...], gbi_ref[...]))
        o_are[...] = d_are
        o_aim[...] = d_aim
        o_dt[...] = d_dt
        o_bre[...] = d_bre
        o_bim[...] = d_bim

    ins = (a_re, a_im, log_dt, b_re_t, b_im_t)
    return pl.pallas_call(
        body, name="ssm_params_bwd",
        out_shape=[jax.ShapeDtypeStruct(v.shape, F32) for v in ins],
        compiler_params=_params(None),
    )(*ins, g_ab_re, g_ab_im, g_bb_re, g_bb_im)


GROUPS_PER_CHUNK = N_SSM_GROUPS // SSM_CHUNKS


def _diag_mask(rows_per_group, cols_per_group):
    shape = (GROUPS_PER_CHUNK * rows_per_group, GROUPS_PER_CHUNK * cols_per_group)
    r = lax.broadcasted_iota(jnp.int32, shape, 0) // rows_per_group
    c = lax.broadcasted_iota(jnp.int32, shape, 1) // cols_per_group
    return (r == c)[None]


def _blockdiag_in(w):
    tiled = jnp.tile(w.reshape(SSM_CHUNKS, CHUNK_IN, SSM_STATE), (1, 1, GROUPS_PER_CHUNK))
    return jnp.where(_diag_mask(SSM_GROUP, SSM_STATE), tiled, 0.0).astype(BF16)


def _diag_in(g):
    kept = jnp.where(_diag_mask(SSM_GROUP, SSM_STATE), g, 0.0)
    return kept.reshape(SSM_CHUNKS, CHUNK_IN, GROUPS_PER_CHUNK, SSM_STATE).sum(axis=2).reshape(
        N_SSM_GROUPS, SSM_GROUP, SSM_STATE)


STATE_TILES = SSM_FLAT // LANES
CHUNK_TILES = CHUNK_STATE // LANES
SSM_FWD_ROWS = 256
SCAN_STEPS = 8


def _ssm_fwd(proj, wbd_re, wbd_im, cbd_re, cbd_imneg, a_re, a_im, d_skip):
    t = proj.shape[0]
    tt = _tile(t, SSM_FWD_ROWS)

    def body(u_ref, wre_ref, wim_ref, cre_ref, cim_ref, are_ref, aim_ref, d_ref,
             y_ref, gel_ref, sre_ref, sim_ref, xre_ref, xim_ref, carry_re, carry_im):
        @pl.when(pl.program_id(0) == 0)
        def _():
            carry_re[...] = jnp.zeros_like(carry_re)
            carry_im[...] = jnp.zeros_like(carry_im)

        u = u_ref[...]
        ub = u.astype(BF16)
        for k in range(SSM_CHUNKS):
            uk = ub[:, k * CHUNK_IN:(k + 1) * CHUNK_IN]
            b_re = jnp.dot(uk, wre_ref[k], preferred_element_type=F32)
            b_im = jnp.dot(uk, wim_ref[k], preferred_element_type=F32)
            for j in range(CHUNK_TILES):
                xre_ref[:, CHUNK_TILES * k + j, :] = b_re[:, j * LANES:(j + 1) * LANES]
                xim_ref[:, CHUNK_TILES * k + j, :] = b_im[:, j * LANES:(j + 1) * LANES]

        a_re_v, a_im_v = are_ref[...], aim_ref[...]

        def steps(i, carry):
            s_re, s_im = carry
            for v in range(SCAN_STEPS):
                tok = i * SCAN_STEPS + v
                s_re, s_im = (a_re_v * s_re - a_im_v * s_im + xre_ref[tok],
                              a_re_v * s_im + a_im_v * s_re + xim_ref[tok])
                xre_ref[tok] = s_re
                xim_ref[tok] = s_im
            return s_re, s_im

        s_re, s_im = lax.fori_loop(0, tt // SCAN_STEPS, steps, (carry_re[...], carry_im[...]))
        carry_re[...] = s_re
        carry_im[...] = s_im

        for k in range(SSM_CHUNKS):
            cols = slice(k * CHUNK_IN, (k + 1) * CHUNK_IN)
            states = slice(k * CHUNK_STATE, (k + 1) * CHUNK_STATE)
            sk_re = jnp.concatenate([xre_ref[:, CHUNK_TILES * k + j, :] for j in range(CHUNK_TILES)], axis=1)
            sk_im = jnp.concatenate([xim_ref[:, CHUNK_TILES * k + j, :] for j in range(CHUNK_TILES)], axis=1)
            sre_ref[:, states] = sk_re
            sim_ref[:, states] = sk_im
            acc = (lax.dot_general(sk_re.astype(BF16), cre_ref[k], DOT_NT, preferred_element_type=F32)
                   + lax.dot_general(sk_im.astype(BF16), cim_ref[k], DOT_NT, preferred_element_type=F32))
            y = acc + d_ref[:, cols] * u[:, cols]
            y_ref[:, cols] = y
            gel_ref[:, cols] = (0.5 * y * (1.0 + jnp.tanh(GELU_C * (y + GELU_A * y * y * y)))).astype(BF16)

    def resident(shape):
        return pl.BlockSpec(shape, lambda i: (0,) * len(shape), pipeline_mode=pl.Buffered(1))

    weight = resident((SSM_CHUNKS, CHUNK_IN, CHUNK_STATE))
    abar = resident((STATE_TILES, LANES))
    rows = pl.BlockSpec((tt, SSM_WIDTH), lambda i: (i, 0))
    states = pl.BlockSpec((tt, SSM_FLAT), lambda i: (i, 0))
    return pl.pallas_call(
        body, name="ssm_fwd", grid=(t // tt,),
        in_specs=[pl.BlockSpec((tt, SSM_WIDTH), lambda i: (i, 2 * POOL_WIDTH // SSM_WIDTH)), weight, weight, weight, weight,
                  abar, abar, resident((1, SSM_WIDTH))],
        out_specs=[rows, rows, states, states],
        out_shape=[jax.ShapeDtypeStruct((t, SSM_WIDTH), F32), jax.ShapeDtypeStruct((t, SSM_WIDTH), BF16),
                   jax.ShapeDtypeStruct((t, SSM_FLAT), F32), jax.ShapeDtypeStruct((t, SSM_FLAT), F32)],
        scratch_shapes=[pltpu.VMEM((tt, STATE_TILES, LANES), F32)] * 2 + [pltpu.VMEM((STATE_TILES, LANES), F32)] * 2,
        compiler_params=_params(("arbitrary",)),
    )(proj, wbd_re, wbd_im, cbd_re, cbd_imneg, a_re, a_im, d_skip)


SSM_BWD_ROWS = 128


def _ssm_bwd(dy, proj, s_re, s_im, wbd_re, wbd_im, cbd_re, cbd_imneg, a_re, a_im, d_skip):
    t = dy.shape[0]
    tt = _tile(t, SSM_BWD_ROWS)
    n_chunks = t // tt
    wshape = (SSM_CHUNKS, CHUNK_IN, CHUNK_STATE)

    def body(dy_ref, u_ref, sre_ref, sim_ref, wre_ref, wim_ref, cre_ref, cim_ref, are_ref, aim_ref, d_ref,
             du_ref, gcre_hbm, gcim_hbm, gbre_hbm, gbim_hbm, gare_ref, gaim_ref, gd_ref,
             zre_ref, zim_ref, carry_re, carry_im, next_re, next_im, acc_cre, acc_cim, acc_bre, acc_bim, sem):
        step = pl.program_id(0)

        @pl.when(step == 0)
        def _():
            for r in (carry_re, carry_im, next_re, next_im, acc_cre, acc_cim, acc_bre, acc_bim,
                      gare_ref, gaim_ref, gd_ref):
                r[...] = jnp.zeros_like(r)

        dy_v = dy_ref[...]
        u = u_ref[...]
        dyb = dy_v.astype(BF16)
        ub = u.astype(BF16)
        gd_ref[...] += jnp.sum(dy_v * u, axis=0, keepdims=True)

        for k in range(SSM_CHUNKS):
            dk = dyb[:, k * CHUNK_IN:(k + 1) * CHUNK_IN]
            ds_re = jnp.dot(dk, cre_ref[k], preferred_element_type=F32)
            ds_im = jnp.dot(dk, cim_ref[k], preferred_element_type=F32)
            for j in range(CHUNK_TILES):
                zre_ref[:, CHUNK_TILES * k + j, :] = ds_re[:, j * LANES:(j + 1) * LANES]
                zim_ref[:, CHUNK_TILES * k + j, :] = ds_im[:, j * LANES:(j + 1) * LANES]

        a_re_v, a_im_v = are_ref[...], aim_ref[...]

        def steps(i, carry):
            z_re, z_im = carry
            for v in range(SCAN_STEPS):
                tok = tt - 1 - (i * SCAN_STEPS + v)
                z_re, z_im = (zre_ref[tok] + a_re_v * z_re + a_im_v * z_im,
                              zim_ref[tok] + a_re_v * z_im - a_im_v * z_re)
                zre_ref[tok] = z_re
                zim_ref[tok] = z_im
            return z_re, z_im

        z_re, z_im = lax.fori_loop(0, tt // SCAN_STEPS, steps, (carry_re[...], carry_im[...]))
        carry_re[...] = z_re
        carry_im[...] = z_im

        row = lax.broadcasted_iota(jnp.int32, (tt, CHUNK_STATE), 0)
        for k in range(SSM_CHUNKS):
            cols = slice(k * CHUNK_IN, (k + 1) * CHUNK_IN)
            states = slice(k * CHUNK_STATE, (k + 1) * CHUNK_STATE)
            zk_re = jnp.concatenate([zre_ref[:, CHUNK_TILES * k + j, :] for j in range(CHUNK_TILES)], axis=1)
            zk_im = jnp.concatenate([zim_ref[:, CHUNK_TILES * k + j, :] for j in range(CHUNK_TILES)], axis=1)
            sk_re, sk_im = sre_ref[:, states], sim_ref[:, states]
            up_re = jnp.where(row == tt - 1, next_re[:, states], pltpu.roll(zk_re, tt - 1, 0))
            up_im = jnp.where(row == tt - 1, next_im[:, states], pltpu.roll(zk_im, tt - 1, 0))
            next_re[:, states] = zk_re[:1, :]
            next_im[:, states] = zk_im[:1, :]
            g_re = up_re * sk_re + up_im * sk_im
            g_im = up_im * sk_re - up_re * sk_im
            gare_ref[:, states] += g_re.reshape(tt // SUBLANES, SUBLANES, CHUNK_STATE).sum(axis=0)
            gaim_ref[:, states] += g_im.reshape(tt // SUBLANES, SUBLANES, CHUNK_STATE).sum(axis=0)
            zb_re, zb_im = zk_re.astype(BF16), zk_im.astype(BF16)
            du = (lax.dot_general(zb_re, wre_ref[k], DOT_NT, preferred_element_type=F32)
                  + lax.dot_general(zb_im, wim_ref[k], DOT_NT, preferred_element_type=F32))
            du_ref[:, cols] = (du + d_ref[:, cols] * dy_v[:, cols]).astype(BF16)
            uk, dk = ub[:, cols], dyb[:, cols]
            acc_bre[k] += lax.dot_general(uk, zb_re, DOT_TN, preferred_element_type=F32)
            acc_bim[k] += lax.dot_general(uk, zb_im, DOT_TN, preferred_element_type=F32)
            acc_cre[k] += lax.dot_general(dk, sk_re.astype(BF16), DOT_TN, preferred_element_type=F32)
            acc_cim[k] += lax.dot_general(dk, sk_im.astype(BF16), DOT_TN, preferred_element_type=F32)

        @pl.when(step == n_chunks - 1)
        def _():
            copies = [pltpu.make_async_copy(src, dst, sem.at[i]) for i, (src, dst) in enumerate(
                ((acc_cre, gcre_hbm), (acc_cim, gcim_hbm), (acc_bre, gbre_hbm), (acc_bim, gbim_hbm)))]
            for cp in copies:
                cp.start()
            for cp in copies:
                cp.wait()

    def resident(shape):
        return pl.BlockSpec(shape, lambda i: (0,) * len(shape), pipeline_mode=pl.Buffered(1))

    rev = lambda i: (n_chunks - 1 - i, 0)
    weight = resident(wshape)
    abar = resident((STATE_TILES, LANES))
    in_hbm = pl.BlockSpec(memory_space=pl.ANY)
    partial = pl.BlockSpec((SUBLANES, SSM_FLAT), lambda i: (0, 0))
    grad = jax.ShapeDtypeStruct(wshape, F32)
    return pl.pallas_call(
        body, name="ssm_bwd", grid=(n_chunks,),
        in_specs=[pl.BlockSpec((tt, SSM_WIDTH), rev),
                  pl.BlockSpec((tt, SSM_WIDTH), lambda i: (n_chunks - 1 - i, 2 * POOL_WIDTH // SSM_WIDTH)),
                  pl.BlockSpec((tt, SSM_FLAT), rev), pl.BlockSpec((tt, SSM_FLAT), rev),
                  weight, weight, weight, weight, abar, abar, resident((1, SSM_WIDTH))],
        out_specs=[pl.BlockSpec((tt, SSM_WIDTH), rev), in_hbm, in_hbm, in_hbm, in_hbm, partial, partial,
                   pl.BlockSpec((1, SSM_WIDTH), lambda i: (0, 0))],
        out_shape=[jax.ShapeDtypeStruct((t, SSM_WIDTH), BF16), grad, grad, grad, grad,
                   jax.ShapeDtypeStruct((SUBLANES, SSM_FLAT), F32), jax.ShapeDtypeStruct((SUBLANES, SSM_FLAT), F32),
                   jax.ShapeDtypeStruct((1, SSM_WIDTH), F32)],
        scratch_shapes=[pltpu.VMEM((tt, STATE_TILES, LANES), F32)] * 2 + [pltpu.VMEM((STATE_TILES, LANES), F32)] * 2
                       + [pltpu.VMEM((1, SSM_FLAT), F32)] * 2 + [pltpu.VMEM(wshape, F32)] * 4
                       + [pltpu.SemaphoreType.DMA((4,))],
        compiler_params=_params(("arbitrary",)),
    )(dy, proj, s_re, s_im, wbd_re, wbd_im, cbd_re, cbd_imneg, a_re, a_im, d_skip)


def _block(ref, axis, size, index):
    idx = [slice(None)] * len(ref.shape)
    idx[axis] = pl.ds(pl.multiple_of(index * size, size), size)
    return ref.at[tuple(idx)]


def _all_gather(name, shards, axes):
    n = len(shards)
    sizes = [s.shape[a] for s, a in zip(shards, axes)]

    def body(*refs):
        ins, outs = refs[:n], refs[n:2 * n]
        send_sems, recv_sems, local_sems = refs[2 * n:]
        x, y, c = (lax.axis_index(a) for a in MESH_AXES)
        me, sibling = (x, y, c), (x, y, 1 - c)
        chips = [(1 - x, y), (x, 1 - y), (1 - x, 1 - y)]

        def rows(i, dev):
            return _block(outs[i], axes[i], sizes[i], 4 * dev[0] + 2 * dev[1] + dev[2])

        def copy(i, k, block, to, src=None):
            return pltpu.make_async_remote_copy(
                src_ref=rows(i, block) if src is None else src, dst_ref=rows(i, block),
                send_sem=send_sems.at[7 * i + k], recv_sem=recv_sems.at[7 * i + k],
                device_id=to, device_id_type=MESH)

        mine = [pltpu.make_async_copy(ins[i], rows(i, me), local_sems.at[i]) for i in range(n)]
        for cp in mine:
            cp.start()
        first = []
        for i in range(n):
            first.append(copy(i, 0, me, sibling, src=ins[i]))
            first += [copy(i, 1 + j, me, (*chip, c), src=ins[i]) for j, chip in enumerate(chips)]
        for cp in first:
            cp.start()
        passed = []
        for i in range(n):
            for j, chip in enumerate(chips):
                copy(i, 1 + j, (*chip, c), me).wait_recv()
                fwd = copy(i, 4 + j, (*chip, c), sibling)
                fwd.start()
                passed.append(fwd)
        for i in range(n):
            copy(i, 0, sibling, me).wait_recv()
            for j, chip in enumerate(chips):
                copy(i, 4 + j, (*chip, 1 - c), me).wait_recv()
        for cp in first + passed:
            cp.wait_send()
        for cp in mine:
            cp.wait()

    out_shape = []
    for s, a in zip(shards, axes):
        shape = list(s.shape)
        shape[a] *= N_DEV
        out_shape.append(jax.ShapeDtypeStruct(tuple(shape), s.dtype))
    any_spec = pl.BlockSpec(memory_space=pl.ANY)
    return pl.pallas_call(
        body, name=name, out_shape=out_shape,
        in_specs=[any_spec] * n, out_specs=[any_spec] * n,
        scratch_shapes=[pltpu.SemaphoreType.DMA((7 * n,)), pltpu.SemaphoreType.DMA((7 * n,)),
                        pltpu.SemaphoreType.DMA((n,))],
    )(*shards)


HBM_SPEC = pl.BlockSpec(memory_space=pltpu.HBM)
SEM_SPEC = pl.BlockSpec(memory_space=pltpu.SEMAPHORE)
ANY_SPEC = pl.BlockSpec(memory_space=pl.ANY)
SPLIT_PARAMS = pltpu.CompilerParams(has_side_effects=pltpu.SideEffectType.DATAFLOW_SIDE_EFFECTING)
N_PEERS = N_DEV - 1
TOKEN = jax.ShapeDtypeStruct((SUBLANES, LANES), F32)
VMEM_SPEC = pl.BlockSpec(memory_space=pltpu.VMEM)


def _in_hbm(arrays):
    return [pltpu.with_memory_space_constraint(a, pltpu.HBM) for a in arrays]


def _peer(m):
    x, y, c = (lax.axis_index(a) for a in MESH_AXES)
    px = 1 - x if m & 4 else x
    py = 1 - y if m & 2 else y
    pc = 1 - c if m & 1 else c
    return (px, py, pc), 4 * px + 2 * py + pc


def _my_index():
    x, y, c = (lax.axis_index(a) for a in MESH_AXES)
    return 4 * x + 2 * y + c


def _gather_copies(shard_refs, full_refs, axes, send_sems, recv_sems):
    copies = []
    for i, (shard, full) in enumerate(zip(shard_refs, full_refs)):
        mine = _block(full, axes[i], shard.shape[axes[i]], _my_index())
        for m in range(1, N_DEV):
            peer, _ = _peer(m)
            copies.append(pltpu.make_async_remote_copy(
                src_ref=shard, dst_ref=mine, send_sem=send_sems.at[N_PEERS * i + m - 1],
                recv_sem=recv_sems.at[N_PEERS * i + m - 1], device_id=peer, device_id_type=MESH))
    return copies


def _gather_start(name, shards, axes, after):
    n = len(shards)

    def body(*refs):
        shard_refs = refs[:n]
        send_sems, recv_sems, local_sems = refs[n + 1:n + 4]
        full_refs = refs[2 * n + 4:3 * n + 4]
        refs[3 * n + 4][...] = jnp.zeros(TOKEN.shape, TOKEN.dtype)
        for i in range(n):
            pltpu.make_async_copy(shard_refs[i], _block(full_refs[i], axes[i], shard_refs[i].shape[axes[i]], _my_index()),
                                  local_sems.at[i]).start()
        for cp in _gather_copies(shard_refs, full_refs, axes, send_sems, recv_sems):
            cp.start()

    fulls = []
    for s, a in zip(shards, axes):
        shape = list(s.shape)
        shape[a] *= N_DEV
        fulls.append(pltpu.HBM(tuple(shape), s.dtype))
    out = pl.pallas_call(
        body, name=name,
        out_shape=(pltpu.SemaphoreType.DMA((N_PEERS * n,)), pltpu.SemaphoreType.DMA((N_PEERS * n,)),
                   pltpu.SemaphoreType.DMA((n,)), *[pltpu.HBM(s.shape, s.dtype) for s in shards], *fulls, TOKEN),
        in_specs=[HBM_SPEC] * n + [ANY_SPEC],
        out_specs=(SEM_SPEC, SEM_SPEC, SEM_SPEC, *[HBM_SPEC] * (2 * n), VMEM_SPEC),
        input_output_aliases={i: 3 + i for i in range(n)},
        compiler_params=SPLIT_PARAMS,
    )(*_in_hbm(shards), after)
    return out[:-1], out[-1]


def _gather_wait(name, started, indices, axes, after):
    send_sems, recv_sems, local_sems = started[:3]
    n_all = (len(started) - 3) // 2
    shards = [started[3 + i] for i in indices]
    fulls = [started[3 + n_all + i] for i in indices]
    n = len(indices)

    def body(*refs):
        shard_refs, full_refs = refs[:n], refs[n:2 * n]
        send_sems, recv_sems, local_sems = refs[2 * n:2 * n + 3]
        for j, i in enumerate(indices):
            mine = _block(full_refs[j], axes[j], shard_refs[j].shape[axes[j]], _my_index())
            pltpu.make_async_copy(shard_refs[j], mine, local_sems.at[i]).wait()
            for m in range(1, N_DEV):
                peer, _ = _peer(m)
                cp = pltpu.make_async_remote_copy(
                    src_ref=shard_refs[j], dst_ref=mine, send_sem=send_sems.at[N_PEERS * i + m - 1],
                    recv_sem=recv_sems.at[N_PEERS * i + m - 1], device_id=peer, device_id_type=MESH)
                cp.wait_send()
                cp.wait_recv()

    out = pl.pallas_call(
        body, name=name,
        out_shape=tuple(pltpu.HBM(a.shape, a.dtype) for a in shards + fulls),
        in_specs=[HBM_SPEC] * (2 * n) + [SEM_SPEC] * 3 + [ANY_SPEC], out_specs=tuple([HBM_SPEC] * (2 * n)),
        input_output_aliases={i: i for i in range(2 * n)},
        compiler_params=SPLIT_PARAMS,
    )(*shards, *fulls, send_sems, recv_sems, local_sems, after)
    return out[n:]


def _exchange_start(name, fulls, axes):
    n = len(fulls)
    sizes = [f.shape[a] // N_DEV for f, a in zip(fulls, axes)]

    def body(*refs):
        ins = refs[:n]
        send_sems, recv_sems = refs[n:n + 2]
        lands = refs[2 * n + 2:3 * n + 2]
        refs[3 * n + 2][...] = jnp.zeros(TOKEN.shape, TOKEN.dtype)
        for i in range(n):
            for m in range(1, N_DEV):
                peer, index = _peer(m)
                pltpu.make_async_remote_copy(
                    src_ref=_block(ins[i], axes[i], sizes[i], index), dst_ref=lands[i].at[m - 1],
                    send_sem=send_sems.at[N_PEERS * i + m - 1], recv_sem=recv_sems.at[N_PEERS * i + m - 1],
                    device_id=peer, device_id_type=MESH).start()

    lands = []
    for f, a, size in zip(fulls, axes, sizes):
        shape = list(f.shape)
        shape[a] = size
        lands.append(pltpu.HBM((N_PEERS, *shape), f.dtype))
    out = pl.pallas_call(
        body, name=name,
        out_shape=(pltpu.SemaphoreType.DMA((N_PEERS * n,)), pltpu.SemaphoreType.DMA((N_PEERS * n,)),
                   *[pltpu.HBM(f.shape, f.dtype) for f in fulls], *lands, TOKEN),
        in_specs=[HBM_SPEC] * n, out_specs=(SEM_SPEC, SEM_SPEC, *[HBM_SPEC] * (2 * n), VMEM_SPEC),
        input_output_aliases={i: 2 + i for i in range(n)},
        compiler_params=SPLIT_PARAMS,
    )(*_in_hbm(fulls))
    return out[:-1], out[-1]


def _exchange_wait(name, started, axes, after):
    send_sems, recv_sems = started[:2]
    n = (len(started) - 2) // 2
    fulls, lands = list(started[2:2 + n]), list(started[2 + n:])
    sizes = [f.shape[a] // N_DEV for f, a in zip(fulls, axes)]

    def body(*refs):
        ins, land_refs = refs[:n], refs[n:2 * n]
        send_sems, recv_sems = refs[2 * n:2 * n + 2]
        for i in range(n):
            for m in range(1, N_DEV):
                peer, index = _peer(m)
                cp = pltpu.make_async_remote_copy(
                    src_ref=_block(ins[i], axes[i], sizes[i], index), dst_ref=land_refs[i].at[m - 1],
                    send_sem=send_sems.at[N_PEERS * i + m - 1], recv_sem=recv_sems.at[N_PEERS * i + m - 1],
                    device_id=peer, device_id_type=MESH)
                cp.wait_send()
                cp.wait_recv()

    out = pl.pallas_call(
        body, name=name,
        out_shape=tuple(pltpu.HBM(a.shape, a.dtype) for a in fulls + lands),
        in_specs=[HBM_SPEC] * (2 * n) + [SEM_SPEC] * 2 + [ANY_SPEC], out_specs=tuple([HBM_SPEC] * (2 * n)),
        input_output_aliases={i: i for i in range(2 * n)},
        compiler_params=SPLIT_PARAMS,
    )(*fulls, *lands, send_sems, recv_sems, after)
    return out[:n], out[n:]


def _adamw_update(w_ref, m_ref, v_ref, part_refs, g_ref, d_ref, nm_ref, nv_ref):
    c1 = 1.0 - ADAM_B1 ** ADAM_STEP
    c2 = 1.0 - ADAM_B2 ** ADAM_STEP
    g = None
    for p_ref in part_refs:
        stacked = len(p_ref.shape) > len(w_ref.shape)
        terms = [p_ref[s] for s in range(p_ref.shape[0])] if stacked else [p_ref[...]]
        for term in terms:
            term = term.astype(F32)
            g = term if g is None else g + term
    new_m = ADAM_B1 * m_ref[...] + (1.0 - ADAM_B1) * g
    new_v = ADAM_B2 * v_ref[...] + (1.0 - ADAM_B2) * (g * g)
    g_ref[...] = g
    nm_ref[...] = new_m
    nv_ref[...] = new_v
    d_ref[...] = -ADAM_LR * ((new_m / c1) / (jnp.sqrt(new_v / c2) + ADAM_EPS) + ADAM_WD * w_ref[...])


def _adamw_small(ws, ms, vs, stacks):
    n = len(ws)

    def body(*refs):
        ins, outs = refs[:4 * n], refs[4 * n:]
        for i in range(n):
            _adamw_update(ins[i], ins[n + i], ins[2 * n + i], [ins[3 * n + i]],
                          outs[i], outs[n + i], outs[2 * n + i], outs[3 * n + i])

    res = pl.pallas_call(
        body, name="adamw_small",
        out_shape=[jax.ShapeDtypeStruct(w.shape, F32) for w in ws] * 4,
        compiler_params=_params(None),
    )(*ws, *ms, *vs, *stacks)
    return res[:n], res[n:2 * n], res[2 * n:3 * n], res[3 * n:]


def _adamw(name, w, m, v, parts):
    r, c = w.shape
    tr = _tile(r, 256)
    n_parts = len(parts)

    def body(*refs):
        _adamw_update(refs[0], refs[1], refs[2], refs[3:3 + n_parts], *refs[3 + n_parts:])

    row = pl.BlockSpec((tr, c), lambda i: (i, 0))
    in_specs = [row, row, row]
    for p in parts:
        in_specs.append(row if p.ndim == 2 else pl.BlockSpec((p.shape[0], tr, c), lambda i: (0, i, 0)))
    return pl.pallas_call(
        body, name=name, grid=(r // tr,), in_specs=in_specs, out_specs=[row] * 4,
        out_shape=[jax.ShapeDtypeStruct((r, c), F32)] * 4,
        compiler_params=_params(("arbitrary",)),
    )(w, m, v, *parts)


SMALL = ("norm_gain", "pool_scale", "a_re", "a_im", "log_dt", "b_re", "b_im", "c_re", "c_im", "d_skip", "final_gain")
LARGE = ("w_in", "w_pool", "w_glu", "w_out", "w_ple", "w_ple_gate")
LARGE_AXIS = {"w_in": 1, "w_pool": 1, "w_glu": 1, "w_out": 0, "w_ple": 1, "w_ple_gate": 0}
WEIGHTS = ("norm_gain", "w_in", "w_pool", "pool_scale", "a_re", "a_im", "log_dt", "b_re", "b_im", "c_re", "c_im",
           "d_skip", "w_glu", "w_out", "w_ple", "w_ple_gate", "final_gain")


def kernel(x, p, norm_gain, w_in, w_pool, pool_scale, a_re, a_im, log_dt, b_re, b_im, c_re, c_im, d_skip, w_glu, w_out, w_ple, w_ple_gate, final_gain, loss_target, m_norm_gain, m_w_in, m_w_pool, m_pool_scale, m_a_re, m_a_im, m_log_dt, m_b_re, m_b_im, m_c_re, m_c_im, m_d_skip, m_w_glu, m_w_out, m_w_ple, m_w_ple_gate, m_final_gain, v_norm_gain, v_w_in, v_w_pool, v_pool_scale, v_a_re, v_a_im, v_log_dt, v_b_re, v_b_im, v_c_re, v_c_im, v_d_skip, v_w_glu, v_w_out, v_w_ple, v_w_ple_gate, v_final_gain):
    weights = dict(norm_gain=norm_gain, w_in=w_in, w_pool=w_pool, pool_scale=pool_scale, a_re=a_re, a_im=a_im,
                   log_dt=log_dt, b_re=b_re, b_im=b_im, c_re=c_re, c_im=c_im, d_skip=d_skip, w_glu=w_glu,
                   w_out=w_out, w_ple=w_ple, w_ple_gate=w_ple_gate, final_gain=final_gain)
    mom_m = dict(norm_gain=m_norm_gain, w_in=m_w_in, w_pool=m_w_pool, pool_scale=m_pool_scale, a_re=m_a_re,
                 a_im=m_a_im, log_dt=m_log_dt, b_re=m_b_re, b_im=m_b_im, c_re=m_c_re, c_im=m_c_im,
                 d_skip=m_d_skip, w_glu=m_w_glu, w_out=m_w_out, w_ple=m_w_ple, w_ple_gate=m_w_ple_gate,
                 final_gain=m_final_gain)
    mom_v = dict(norm_gain=v_norm_gain, w_in=v_w_in, w_pool=v_w_pool, pool_scale=v_pool_scale, a_re=v_a_re,
                 a_im=v_a_im, log_dt=v_log_dt, b_re=v_b_re, b_im=v_b_im, c_re=v_c_re, c_im=v_c_im,
                 d_skip=v_d_skip, w_glu=v_w_glu, w_out=v_w_out, w_ple=v_w_ple, w_ple_gate=v_w_ple_gate,
                 final_gain=v_final_gain)

    t = x.shape[1]
    xs = x.reshape(t, D_MODEL)
    ps = p.reshape(t, PLE_DIM)
    target = loss_target.reshape(t, D_MODEL)
    gain1 = norm_gain.reshape(1, D_MODEL)
    gain_f = final_gain.reshape(1, D_MODEL)
    scale_p = pool_scale.reshape(1, POOL_WIDTH)
    skip = d_skip.reshape(1, SSM_WIDTH)

    shard2d = {k: weights[k][0] for k in LARGE}
    shard_bf = {k: shard2d[k].astype(BF16) for k in LARGE}
    full = {"w_in": _all_gather("w_in_all_gather", [shard_bf["w_in"]], [LARGE_AXIS["w_in"]])[0]}
    later = [k for k in LARGE if k != "w_in"]
    later_axes = [LARGE_AXIS[k] for k in later]
    gather, gather_token = _gather_start("weights_gather_start", [shard_bf[k] for k in later], later_axes,
                                         full["w_in"])

    def arrive(k, after):
        i = later.index(k)
        full[k] = _gather_wait("gather_wait_" + k, gather, [i], [later_axes[i]], after)[0]

    ar, ai = a_re[0], a_im[0]
    ldt = log_dt.reshape(N_SSM_GROUPS, 1)
    br_t = jnp.transpose(b_re[0], (0, 2, 1))
    bi_t = jnp.transpose(b_im[0], (0, 2, 1))
    ab_re, ab_im, bb_re, bb_im = _ssm_params(ar, ai, ldt, br_t, bi_t)
    ab_re_t = ab_re.reshape(STATE_TILES, LANES)
    ab_im_t = ab_im.reshape(STATE_TILES, LANES)
    wbd_re = _blockdiag_in(bb_re)
    wbd_im = _blockdiag_in(bb_im)
    cbd_re = _blockdiag_in(c_re[0])
    cbd_imneg = _blockdiag_in(-c_im[0])

    hn = _norm1_fwd(xs, gain1)
    proj = _mm_nn("in_proj", hn, full["w_in"], [F32], tk=2048, after=[gather_token])[0]
    pooled = _pool_fwd(proj)
    tm = _tile(t, 1024)
    arrive("w_pool", pooled)
    mixed = _mm("pool_mix", [(pooled, (tm, POOL_GROUP), lambda i, j, s: (i, j),
                              full["w_pool"], (None, POOL_GROUP, POOL_GROUP), lambda i, j, s: (j, 0, 0))],
                DOT_NN, (t // tm, N_POOL_GROUPS, 1),
                [((t, POOL_WIDTH), F32, (tm, POOL_GROUP), lambda i, j, s: (i, j))], 1)[0]
    y, gel, s_re, s_im = _ssm_fwd(proj, wbd_re, wbd_im, cbd_re, cbd_imneg, ab_re_t, ab_im_t, skip)
    arrive("w_glu", gel)
    hg = _mm_nn("glu_proj", gel, full["w_glu"], [F32])[0]
    cat = _gate_fwd(mixed, proj, hg, scale_p)

    def residual_epilogue(acc, ex, out_refs):
        h = acc + ex[0][...]
        out_refs[0][...] = h
        out_refs[1][...] = h.astype(BF16)

    arrive("w_out", cat)
    h1, h1b = _mm_nn("out_proj", cat, full["w_out"], [F32, BF16], extras=[xs], epilogue=residual_epilogue)
    arrive("w_ple", h1b)
    e = _mm_nn("ple_proj", ps, full["w_ple"], [F32])[0]
    arrive("w_ple_gate", e)
    q = _mm_nn("ple_gate_proj", h1b, full["w_ple_gate"], [F32], tk=2048)[0]
    de, dq, dh2, g_final_gain, loss_part = _final(h1, e, q, target, gain_f)
    loss = lax.psum(loss_part[0, 0], MESH_AXES)

    grads = {}
    grads["w_ple_gate"] = _mm_tn("ple_gate_wgrad", h1b, dq, BF16)
    grads["w_ple"] = _mm_tn("ple_wgrad", ps, de, BF16)
    sent, tokens = {}, {}

    def send(names):
        sent[names], tokens[names[0]] = _exchange_start(
            "grads_start_" + names[0], [grads[k] for k in names], [LARGE_AXIS[k] for k in names])

    send(("w_ple_gate", "w_ple"))
    dh1, dh1b = _mm_nt("ple_gate_dgrad", dq, full["w_ple_gate"], [F32, BF16], extras=[dh2],
                       epilogue=residual_epilogue)
    grads["w_out"] = _mm_tn("out_wgrad", cat, dh1b, BF16)
    send(("w_out",))
    dcat = _mm_nt("out_dgrad", dh1b, full["w_out"], [F32], tk=2048, after=[tokens["w_ple_gate"], tokens["w_out"]])[0]
    dmixed, dga, dgb, dhg, g_pool_scale = _gate_bwd(dcat, mixed, proj, hg, scale_p)

    tk = _tile(t, 1024)
    grads["w_pool"] = _mm("pool_wgrad", [(pooled, (tk, POOL_GROUP), lambda i, j, s: (s, i),
                                          dmixed, (tk, POOL_GROUP), lambda i, j, s: (s, i))],
                          DOT_TN, (N_POOL_GROUPS, 1, t // tk),
                          [((N_POOL_GROUPS, POOL_GROUP, POOL_GROUP), BF16, (None, POOL_GROUP, POOL_GROUP),
                            lambda i, j, s: (i, 0, 0))], t // tk)[0]
    dpooled = _mm("pool_dgrad", [(dmixed, (tm, POOL_GROUP), lambda i, j, s: (i, j),
                                  full["w_pool"], (None, POOL_GROUP, POOL_GROUP), lambda i, j, s: (j, 0, 0))],
                  DOT_NT, (t // tm, N_POOL_GROUPS, 1),
                  [((t, POOL_WIDTH), F32, (tm, POOL_GROUP), lambda i, j, s: (i, j))], 1)[0]
    dua = _pool_bwd(dpooled)

    grads["w_glu"] = _mm_tn("glu_wgrad", gel, dhg, BF16)
    send(("w_pool", "w_glu"))

    def gelu_bwd_epilogue(acc, ex, out_refs):
        yv = ex[0][...]
        th = jnp.tanh(GELU_C * (yv + GELU_A * yv * yv * yv))
        dgelu = 0.5 * (1.0 + th) + 0.5 * yv * (1.0 - th * th) * GELU_C * (1.0 + 3.0 * GELU_A * yv * yv)
        out_refs[0][...] = acc * dgelu

    dy = _mm_nt("glu_dgrad", dhg, full["w_glu"], [F32], tk=2048, extras=[y], epilogue=gelu_bwd_epilogue,
                after=[tokens["w_pool"]])[0]
    dub, g_cbd_re, g_cbd_imneg, g_wbd_re, g_wbd_im, gab_re8, gab_im8, g_d_skip = _ssm_bwd(
        dy, proj, s_re, s_im, wbd_re, wbd_im, cbd_re, cbd_imneg, ab_re_t, ab_im_t, skip)

    g_ab_re = jnp.sum(gab_re8, axis=0).reshape(N_SSM_GROUPS, SSM_STATE)
    g_ab_im = jnp.sum(gab_im8, axis=0).reshape(N_SSM_GROUPS, SSM_STATE)
    d_ar, d_ai, d_ldt, d_br_t, d_bi_t = _ssm_params_bwd(
        ar, ai, ldt, br_t, bi_t, g_ab_re, g_ab_im, _diag_in(g_wbd_re), _diag_in(g_wbd_im))

    dproj = jnp.concatenate([dua, dga, dub, dgb], axis=1)
    grads["w_in"] = _mm_tn("in_wgrad", hn, dproj, BF16)
    send(("w_in",))
    dhn = _mm_nt("in_dgrad", dproj, full["w_in"], [F32], tk=2048, after=[tokens["w_in"]])[0]
    grad_x, g_norm_gain = _norm1_bwd(xs, dhn, dh1, gain1)

    def b_view(a):
        return jnp.transpose(a[0], (0, 2, 1))

    views = dict(norm_gain=lambda a: a, pool_scale=lambda a: a, a_re=lambda a: a[0], a_im=lambda a: a[0],
                 log_dt=lambda a: a, b_re=b_view, b_im=b_view, c_re=lambda a: a[0], c_im=lambda a: a[0],
                 d_skip=lambda a: a, final_gain=lambda a: a.reshape(1, D_MODEL))
    small_grads = dict(
        norm_gain=g_norm_gain, pool_scale=g_pool_scale, a_re=d_ar, a_im=d_ai, log_dt=d_ldt.reshape(1, N_SSM_GROUPS),
        b_re=d_br_t, b_im=d_bi_t, c_re=_diag_in(g_cbd_re), c_im=-_diag_in(g_cbd_imneg), d_skip=g_d_skip,
        final_gain=g_final_gain)
    stacks = _all_gather("small_grads_all_gather", [small_grads[k][None] for k in SMALL], [0] * len(SMALL))
    small_out = _adamw_small([views[k](weights[k]) for k in SMALL], [views[k](mom_m[k]) for k in SMALL],
                             [views[k](mom_v[k]) for k in SMALL], stacks)
    out_g, out_d, out_m, out_v = ({} for _ in range(4))
    for out, res in zip((out_g, out_d, out_m, out_v), small_out):
        for k, r in zip(SMALL, res):
            if k in ("b_re", "b_im"):
                r = jnp.transpose(r, (0, 2, 1))
            out[k] = r.reshape(weights[k].shape)

    me = 4 * lax.axis_index("x") + 2 * lax.axis_index("y") + lax.axis_index("c")
    after = grad_x
    for names, started in sent.items():
        axes = [LARGE_AXIS[k] for k in names]
        partials, landed = _exchange_wait("grads_wait_" + names[0], started, axes, after)
        for k, axis, partial, land in zip(names, axes, partials, landed):
            shard_shape = shard2d[k].shape
            size = shard_shape[axis]
            own = lax.dynamic_slice_in_dim(partial, me * size, size, axis=axis)
            view = (-1, shard_shape[-1])
            rows = math.prod(shard_shape[:-1])
            res = _adamw("adamw_" + k, shard2d[k].reshape(view), mom_m[k][0].reshape(view), mom_v[k][0].reshape(view),
                         [own.reshape(view), land.reshape(N_PEERS, rows, shard_shape[-1])])
            out_g[k], out_d[k], out_m[k], out_v[k] = (r.reshape(weights[k].shape) for r in res)
            after = res[0]

    return (loss, grad_x.reshape(x.shape), *[out_g[k] for k in WEIGHTS], *[out_d[k] for k in WEIGHTS],
            *[out_m[k] for k in WEIGHTS], *[out_v[k] for k in WEIGHTS])
```

```python
import math

import jax
import jax.numpy as jnp
from jax import lax
from jax.experimental import pallas as pl
from jax.experimental.pallas import tpu as pltpu

F32 = jnp.float32
BF16 = jnp.bfloat16
MESH = pl.DeviceIdType.MESH
MESH_AXES = ("x", "y", "c")
N_DEV = 8

D_MODEL = 2048
POOL_WIDTH = 1024
SSM_WIDTH = 1024
N_POOL_GROUPS = 4
POOL_GROUP = 256
SSM_GROUP = 16
N_SSM_GROUPS = 64
SSM_STATE = 64
SSM_FLAT = N_SSM_GROUPS * SSM_STATE
SSM_CHUNKS = 4
CHUNK_IN = SSM_WIDTH // SSM_CHUNKS
CHUNK_STATE = SSM_FLAT // SSM_CHUNKS
PLE_DIM = 256
EPS = 1e-6
A_RE_MAX = -1e-4
ADAM_LR = 0.001
ADAM_B1 = 0.9
ADAM_B2 = 0.999
ADAM_EPS = 1e-08
ADAM_WD = 0.01
ADAM_STEP = 10
GELU_C = math.sqrt(2.0 / math.pi)
GELU_A = 0.044715

SUBLANES = 8
LANES = 128
VMEM_LIMIT_BYTES = 48 * 1024 * 1024

DOT_NN = (((1,), (0,)), ((), ()))
DOT_NT = (((1,), (1,)), ((), ()))
DOT_TN = (((0,), (0,)), ((), ()))


def _tile(n, pref):
    return pref if n % pref == 0 else n


def _params(sem):
    return pltpu.CompilerParams(dimension_semantics=sem, vmem_limit_bytes=VMEM_LIMIT_BYTES)


def _sigmoid(v):
    return 1.0 / (1.0 + jnp.exp(-v))


def _silu_and_grad(v):
    s = _sigmoid(v)
    return v * s, s * (1.0 + v * (1.0 - s))


def _mm(name, pairs, dims, grid, outs, k_steps, extras=(), epilogue=None):
    n_pairs, n_ex, n_out = len(pairs), len(extras), len(outs)
    acc_shape = tuple(d for d in outs[0][2] if d is not None)
    if epilogue is None:
        def epilogue(acc, ex, out_refs):
            out_refs[0][...] = acc.astype(out_refs[0].dtype)

    def body(*refs):
        ab = refs[:2 * n_pairs]
        ex = refs[2 * n_pairs:2 * n_pairs + n_ex]
        out_refs = refs[2 * n_pairs + n_ex:2 * n_pairs + n_ex + n_out]
        acc = refs[-1]
        k = pl.program_id(2)

        @pl.when(k == 0)
        def _():
            acc[...] = jnp.zeros_like(acc)

        part = None
        for q in range(n_pairs):
            d = lax.dot_general(ab[2 * q][...].astype(BF16), ab[2 * q + 1][...].astype(BF16), dims,
                                preferred_element_type=F32)
            part = d if part is None else part + d
        acc[...] += part

        @pl.when(k == k_steps - 1)
        def _():
            epilogue(acc[...], ex, out_refs)

    in_specs, operands = [], []
    for a, a_blk, a_map, b, b_blk, b_map in pairs:
        in_specs += [pl.BlockSpec(a_blk, a_map), pl.BlockSpec(b_blk, b_map)]
        operands += [a, b]
    for e, e_blk, e_map in extras:
        in_specs.append(pl.BlockSpec(e_blk, e_map))
        operands.append(e)
    return pl.pallas_call(
        body, name=name, grid=grid, in_specs=in_specs,
        out_specs=[pl.BlockSpec(o[2], o[3]) for o in outs],
        out_shape=[jax.ShapeDtypeStruct(o[0], o[1]) for o in outs],
        scratch_shapes=[pltpu.VMEM(acc_shape, F32)],
        compiler_params=_params(("arbitrary", "arbitrary", "arbitrary")),
    )(*operands)


def _after(tokens):
    return [(tok, tok.shape, lambda i, j, s: (0, 0)) for tok in tokens]


def _mm_nn(name, a, b, out_dtypes, tm=1024, tn=1024, tk=1024, a_col0=0, extras=(), epilogue=None, after=()):
    m, n = a.shape[0], b.shape[1]
    k = b.shape[0]
    tm, tn, tk = _tile(m, tm), _tile(n, tn), _tile(k, tk)
    outs = [((m, n), dt, (tm, tn), lambda i, j, s: (i, j)) for dt in out_dtypes]
    ex = [(e, (tm, tn), lambda i, j, s: (i, j)) for e in extras] + _after(after)
    return _mm(name, [(a, (tm, tk), lambda i, j, s: (i, a_col0 + s), b, (tk, tn), lambda i, j, s: (s, j))],
               DOT_NN, (m // tm, n // tn, k // tk), outs, k // tk, ex, epilogue)


def _mm_nt(name, a, b, out_dtypes, tm=1024, tn=1024, tk=1024, extras=(), epilogue=None, after=()):
    m, kk = a.shape
    n = b.shape[0]
    tm, tn, tk = _tile(m, tm), _tile(n, tn), _tile(kk, tk)
    outs = [((m, n), dt, (tm, tn), lambda i, j, s: (i, j)) for dt in out_dtypes]
    ex = [(e, (tm, tn), lambda i, j, s: (i, j)) for e in extras] + _after(after)
    return _mm(name, [(a, (tm, tk), lambda i, j, s: (i, s), b, (tn, tk), lambda i, j, s: (j, s))],
               DOT_NT, (m // tm, n // tn, kk // tk), outs, kk // tk, ex, epilogue)


def _mm_tn(name, a, b, out_dtype, tm=512, tn=2048, tk=1024):
    m, kk = a.shape
    n = b.shape[1]
    tm, tn, tk = _tile(kk, tm), _tile(n, tn), _tile(m, tk)
    outs = [((kk, n), out_dtype, (tm, tn), lambda i, j, s: (i, j))]
    return _mm(name, [(a, (tk, tm), lambda i, j, s: (s, i), b, (tk, tn), lambda i, j, s: (s, j))],
               DOT_TN, (kk // tm, n // tn, m // tk), outs, m // tk)[0]


def _norm1_fwd(x, gain):
    t = x.shape[0]
    tm = _tile(t, 512)

    def body(x_ref, g_ref, hn_ref):
        xv = x_ref[...]
        r = lax.rsqrt(jnp.mean(xv * xv, axis=-1, keepdims=True) + EPS)
        hn_ref[...] = (xv * r * g_ref[...]).astype(BF16)

    return pl.pallas_call(
        body, name="norm1_fwd", grid=(t // tm,),
        in_specs=[pl.BlockSpec((tm, D_MODEL), lambda i: (i, 0)), pl.BlockSpec((1, D_MODEL), lambda i: (0, 0))],
        out_specs=pl.BlockSpec((tm, D_MODEL), lambda i: (i, 0)),
        out_shape=jax.ShapeDtypeStruct((t, D_MODEL), BF16),
        compiler_params=_params(("arbitrary",)),
    )(x, gain)


def _norm1_bwd(x, dhn, dh1, gain):
    t = x.shape[0]
    tm = _tile(t, 512)

    def body(x_ref, dhn_ref, dh1_ref, g_ref, dx_ref, gg_ref):
        @pl.when(pl.program_id(0) == 0)
        def _():
            gg_ref[...] = jnp.zeros_like(gg_ref)

        xv = x_ref[...]
        r = lax.rsqrt(jnp.mean(xv * xv, axis=-1, keepdims=True) + EPS)
        xh = xv * r
        dhn_v = dhn_ref[...]
        gg_ref[...] += jnp.sum(dhn_v * xh, axis=0, keepdims=True)
        dxh = dhn_v * g_ref[...]
        dx_ref[...] = dh1_ref[...] + r * (dxh - xh * jnp.mean(dxh * xh, axis=-1, keepdims=True))

    row = pl.BlockSpec((tm, D_MODEL), lambda i: (i, 0))
    vec = pl.BlockSpec((1, D_MODEL), lambda i: (0, 0))
    return pl.pallas_call(
        body, name="norm1_bwd", grid=(t // tm,),
        in_specs=[row, row, row, vec], out_specs=[row, vec],
        out_shape=[jax.ShapeDtypeStruct((t, D_MODEL), F32), jax.ShapeDtypeStruct((1, D_MODEL), F32)],
        compiler_params=_params(("arbitrary",)),
    )(x, dhn, dh1, gain)


def _pool_counts(t, width, group):
    row = lax.broadcasted_iota(jnp.int32, (t, width), 0)
    window = jnp.left_shift(jnp.int32(2), group)
    return row, jnp.minimum(row + 1, window).astype(F32)


def _select_window(group, s2, s4, s8, s16):
    return jnp.where(group == 0, s2, jnp.where(group == 1, s4, jnp.where(group == 2, s8, s16)))


def _pool_fwd(proj):
    t = proj.shape[0]
    tc = LANES

    def body(u_ref, o_ref):
        group = pl.program_id(0) // (POOL_GROUP // tc)
        v = u_ref[...]
        row, count = _pool_counts(t, tc, group)

        def down(a, j):
            return jnp.where(row >= j, pltpu.roll(a, j, 0), 0.0)

        s2 = v + down(v, 1)
        s4 = s2 + down(s2, 2)
        s8 = s4 + down(s4, 4)
        s16 = s8 + down(s8, 8)
        o_ref[...] = (_select_window(group, s2, s4, s8, s16) / count - v).astype(BF16)

    return pl.pallas_call(
        body, name="pool_fwd", grid=(POOL_WIDTH // tc,),
        in_specs=[pl.BlockSpec((t, tc), lambda j: (0, j))],
        out_specs=pl.BlockSpec((t, tc), lambda j: (0, j)),
        out_shape=jax.ShapeDtypeStruct((t, POOL_WIDTH), BF16),
        compiler_params=_params(("arbitrary",)),
    )(proj)


def _pool_bwd(dpooled):
    t = dpooled.shape[0]
    tc = LANES

    def body(d_ref, o_ref):
        group = pl.program_id(0) // (POOL_GROUP // tc)
        dp = d_ref[...]
        row, count = _pool_counts(t, tc, group)
        r = dp / count

        def up(a, j):
            return jnp.where(row < t - j, pltpu.roll(a, t - j, 0), 0.0)

        s2 = r + up(r, 1)
        s4 = s2 + up(s2, 2)
        s8 = s4 + up(s4, 4)
        s16 = s8 + up(s8, 8)
        o_ref[...] = (_select_window(group, s2, s4, s8, s16) - dp).astype(BF16)

    return pl.pallas_call(
        body, name="pool_bwd", grid=(POOL_WIDTH // tc,),
        in_specs=[pl.BlockSpec((t, tc), lambda j: (0, j))],
        out_specs=pl.BlockSpec((t, tc), lambda j: (0, j)),
        out_shape=jax.ShapeDtypeStruct((t, POOL_WIDTH), BF16),
        compiler_params=_params(("arbitrary",)),
    )(dpooled)


def _gate_fwd(mixed, proj, hg, pool_scale):
    t = mixed.shape[0]
    tm = _tile(t, 512)

    def body(mx_ref, ga_ref, gb_ref, hg_ref, ps_ref, cat_ref):
        silu_a, _ = _silu_and_grad(ga_ref[...])
        cat_ref[:, :POOL_WIDTH] = (mx_ref[...] * ps_ref[...] * silu_a).astype(BF16)
        silu_b, _ = _silu_and_grad(gb_ref[...])
        sb = hg_ref[:, :SSM_WIDTH] * _sigmoid(hg_ref[:, SSM_WIDTH:])
        cat_ref[:, POOL_WIDTH:] = (sb * silu_b).astype(BF16)

    return pl.pallas_call(
        body, name="gate_fwd", grid=(t // tm,),
        in_specs=[pl.BlockSpec((tm, POOL_WIDTH), lambda i: (i, 0)),
                  pl.BlockSpec((tm, POOL_WIDTH), lambda i: (i, 1)),
                  pl.BlockSpec((tm, SSM_WIDTH), lambda i: (i, 3)),
                  pl.BlockSpec((tm, 2 * SSM_WIDTH), lambda i: (i, 0)),
                  pl.BlockSpec((1, POOL_WIDTH), lambda i: (0, 0))],
        out_specs=pl.BlockSpec((tm, D_MODEL), lambda i: (i, 0)),
        out_shape=jax.ShapeDtypeStruct((t, D_MODEL), BF16),
        compiler_params=_params(("arbitrary",)),
    )(mixed, proj, proj, hg, pool_scale)


def _gate_bwd(dcat, mixed, proj, hg, pool_scale):
    t = mixed.shape[0]
    tm = _tile(t, 512)

    def body(dc_ref, mx_ref, ga_ref, gb_ref, hg_ref, ps_ref, dmx_ref, dga_ref, dgb_ref, dhg_ref, gps_ref):
        @pl.when(pl.program_id(0) == 0)
        def _():
            gps_ref[...] = jnp.zeros_like(gps_ref)

        ps = ps_ref[...]
        mx = mx_ref[...]
        dya = dc_ref[:, :POOL_WIDTH]
        silu_a, dsilu_a = _silu_and_grad(ga_ref[...])
        dpa = dya * silu_a
        gps_ref[...] += jnp.sum(dpa * mx, axis=0, keepdims=True)
        dmx_ref[...] = (dpa * ps).astype(BF16)
        dga_ref[...] = (dya * mx * ps * dsilu_a).astype(BF16)

        dyb = dc_ref[:, POOL_WIDTH:]
        silu_b, dsilu_b = _silu_and_grad(gb_ref[...])
        h_a = hg_ref[:, :SSM_WIDTH]
        sg = _sigmoid(hg_ref[:, SSM_WIDTH:])
        dsb = dyb * silu_b
        dgb_ref[...] = (dyb * h_a * sg * dsilu_b).astype(BF16)
        dhg_ref[:, :SSM_WIDTH] = (dsb * sg).astype(BF16)
        dhg_ref[:, SSM_WIDTH:] = (dsb * h_a * sg * (1.0 - sg)).astype(BF16)

    half = pl.BlockSpec((tm, POOL_WIDTH), lambda i: (i, 0))
    full = pl.BlockSpec((tm, D_MODEL), lambda i: (i, 0))
    vec = pl.BlockSpec((1, POOL_WIDTH), lambda i: (0, 0))
    return pl.pallas_call(
        body, name="gate_bwd", grid=(t // tm,),
        in_specs=[full, half,
                  pl.BlockSpec((tm, POOL_WIDTH), lambda i: (i, 1)),
                  pl.BlockSpec((tm, SSM_WIDTH), lambda i: (i, 3)),
                  full, vec],
        out_specs=[half, half, half, full, vec],
        out_shape=[jax.ShapeDtypeStruct((t, POOL_WIDTH), BF16), jax.ShapeDtypeStruct((t, POOL_WIDTH), BF16),
                   jax.ShapeDtypeStruct((t, SSM_WIDTH), BF16), jax.ShapeDtypeStruct((t, 2 * SSM_WIDTH), BF16),
                   jax.ShapeDtypeStruct((1, POOL_WIDTH), F32)],
        compiler_params=_params(("arbitrary",)),
    )(dcat, mixed, proj, proj, hg, pool_scale)


def _final(h1, e, q, target, gain):
    t = h1.shape[0]
    tm = _tile(t, 256)

    def body(h1_ref, e_ref, q_ref, tg_ref, g_ref, de_ref, dq_ref, dh2_ref, gg_ref, loss_ref):
        @pl.when(pl.program_id(0) == 0)
        def _():
            gg_ref[...] = jnp.zeros_like(gg_ref)
            loss_ref[...] = jnp.zeros_like(loss_ref)

        ev = e_ref[...]
        sg = _sigmoid(q_ref[...])
        h2 = h1_ref[...] + ev * sg
        r = lax.rsqrt(jnp.mean(h2 * h2, axis=-1, keepdims=True) + EPS)
        n = h2 * r
        gain_v = g_ref[...]
        diff = n * gain_v - tg_ref[...]
        row_loss = jnp.sum(diff * diff, axis=-1, keepdims=True)
        loss_ref[...] += (0.5 / D_MODEL) * jnp.sum(row_loss, axis=0, keepdims=True)
        dout = diff * (1.0 / D_MODEL)
        gg_ref[...] += jnp.sum(dout * n, axis=0, keepdims=True)
        dn = dout * gain_v
        dh2 = r * (dn - n * jnp.mean(dn * n, axis=-1, keepdims=True))
        dh2_ref[...] = dh2
        de_ref[...] = (dh2 * sg).astype(BF16)
        dq_ref[...] = (dh2 * ev * sg * (1.0 - sg)).astype(BF16)

    row = pl.BlockSpec((tm, D_MODEL), lambda i: (i, 0))
    vec = pl.BlockSpec((1, D_MODEL), lambda i: (0, 0))
    return pl.pallas_call(
        body, name="final_norm_loss", grid=(t // tm,),
        in_specs=[row, row, row, row, vec],
        out_specs=[row, row, row, vec, pl.BlockSpec((1, 1), lambda i: (0, 0))],
        out_shape=[jax.ShapeDtypeStruct((t, D_MODEL), BF16), jax.ShapeDtypeStruct((t, D_MODEL), BF16),
                   jax.ShapeDtypeStruct((t, D_MODEL), F32), jax.ShapeDtypeStruct((1, D_MODEL), F32),
                   jax.ShapeDtypeStruct((1, 1), F32)],
        compiler_params=_params(("arbitrary",)),
    )(h1, e, q, target, gain)


def _zoh(a_re, a_im, log_dt, b_re_t, b_im_t):
    lam_re = jnp.minimum(a_re, A_RE_MAX)
    lam_im = a_im
    dt = jnp.exp(log_dt)
    mag = jnp.exp(lam_re * dt)
    ang = lam_im * dt
    ab_re = mag * jnp.cos(ang)
    ab_im = mag * jnp.sin(ang)
    den = lam_re * lam_re + lam_im * lam_im
    n_re = ab_re - 1.0
    n_im = ab_im
    q_re = (n_re * lam_re + n_im * lam_im) / den
    q_im = (n_im * lam_re - n_re * lam_im) / den
    bb_re = q_re[:, None, :] * b_re_t - q_im[:, None, :] * b_im_t
    bb_im = q_re[:, None, :] * b_im_t + q_im[:, None, :] * b_re_t
    return ab_re, ab_im, bb_re, bb_im


def _ssm_params(a_re, a_im, log_dt, b_re_t, b_im_t):
    def body(are_ref, aim_ref, dt_ref, bre_ref, bim_ref, abre_ref, abim_ref, bbre_ref, bbim_ref):
        ab_re, ab_im, bb_re, bb_im = _zoh(are_ref[...], aim_ref[...], dt_ref[...], bre_ref[...], bim_ref[...])
        abre_ref[...] = ab_re
        abim_ref[...] = ab_im
        bbre_ref[...] = bb_re
        bbim_ref[...] = bb_im

    return pl.pallas_call(
        body, name="ssm_params",
        out_shape=[jax.ShapeDtypeStruct(a_re.shape, F32), jax.ShapeDtypeStruct(a_re.shape, F32),
                   jax.ShapeDtypeStruct(b_re_t.shape, F32), jax.ShapeDtypeStruct(b_re_t.shape, F32)],
        compiler_params=_params(None),
    )(a_re, a_im, log_dt, b_re_t, b_im_t)


def _ssm_params_bwd(a_re, a_im, log_dt, b_re_t, b_im_t, g_ab_re, g_ab_im, g_bb_re, g_bb_im):
    def body(are_ref, aim_ref, dt_ref, bre_ref, bim_ref, gar_ref, gai_ref, gbr_ref, gbi_ref,
             o_are, o_aim, o_dt, o_bre, o_bim):
        _, vjp = jax.vjp(_zoh, are_ref[...], aim_ref[...], dt_ref[...], bre_ref[...], bim_ref[...])
        d_are, d_aim, d_dt, d_bre, d_bim = vjp((gar_ref[...], gai_ref[...], gbr_ref[...], gbi_ref[...]))
        o_are[...] = d_are
        o_aim[...] = d_aim
        o_dt[...] = d_dt
        o_bre[...] = d_bre
        o_bim[...] = d_bim

    ins = (a_re, a_im, log_dt, b_re_t, b_im_t)
    return pl.pallas_call(
        body, name="ssm_params_bwd",
        out_shape=[jax.ShapeDtypeStruct(v.shape, F32) for v in ins],
        compiler_params=_params(None),
    )(*ins, g_ab_re, g_ab_im, g_bb_re, g_bb_im)


CHUNK_TILES = CHUNK_STATE // LANES
CH_PER_TILE = CHUNK_IN // CHUNK_TILES
PAIR = 2 * LANES
SSM_ROWS = 256
SCAN_STEPS = 8
U_COLUMN_BLOCK = 2 * POOL_WIDTH // SSM_WIDTH


def _own_half():
    r = lax.broadcasted_iota(jnp.int32, (CHUNK_IN, LANES), 0) // SSM_GROUP % 2
    c = lax.broadcasted_iota(jnp.int32, (CHUNK_IN, LANES), 1) // SSM_STATE
    return (r == c)[None]


def _compact_weight(w):
    tiled = jnp.tile(w.reshape(SSM_CHUNKS, CHUNK_IN, SSM_STATE), (1, 1, 2))
    return jnp.where(_own_half(), tiled, 0.0)


def _compact_pair(w_a, w_b):
    return jnp.concatenate([_compact_weight(w_a), _compact_weight(w_b)], axis=-1).astype(BF16)


def _expand_grad(g):
    kept = jnp.where(_own_half(), g, 0.0)
    return kept.reshape(SSM_CHUNKS, CHUNK_IN, 2, SSM_STATE).sum(axis=2).reshape(N_SSM_GROUPS, SSM_GROUP, SSM_STATE)


def _tile_mask():
    j = lax.broadcasted_iota(jnp.int32, (CHUNK_TILES, CHUNK_IN), 0)
    ch = lax.broadcasted_iota(jnp.int32, (CHUNK_TILES, CHUNK_IN), 1) // CH_PER_TILE
    return (j == ch).astype(F32)


def _spread(v, mask):
    tt = v.shape[0]
    return (v[:, None, :] * mask[None, :, :]).reshape(tt * CHUNK_TILES, CHUNK_IN)


def _gather(v, mask):
    tt = v.shape[0] // CHUNK_TILES
    return (v.reshape(tt, CHUNK_TILES, CHUNK_IN) * mask[None, :, :]).sum(axis=1)


def _resident(shape):
    return pl.BlockSpec(shape, lambda i: (0,) * len(shape), pipeline_mode=pl.Buffered(1))


def _halves(ref, k, rows=slice(None)):
    return ref[k, rows, :LANES], ref[k, rows, LANES:]


def _ssm_fwd(proj, w2, c2, a2, d_skip):
    t = proj.shape[0]
    tt = _tile(t, SSM_ROWS)
    rows = tt * CHUNK_TILES

    def body(u_ref, w_ref, c_ref, a_ref, d_ref, y_ref, gel_ref, s_ref, carry):
        @pl.when(pl.program_id(0) == 0)
        def _():
            carry[...] = jnp.zeros_like(carry)

        mask = _tile_mask()
        u = u_ref[...]
        for k in range(SSM_CHUNKS):
            uk = _spread(u[:, k * CHUNK_IN:(k + 1) * CHUNK_IN], mask).astype(BF16)
            s_ref[k] = jnp.dot(uk, w_ref[k], preferred_element_type=F32)

        abar = [_halves(a_ref, k) for k in range(SSM_CHUNKS)]

        def steps(i, state):
            for v in range(SCAN_STEPS):
                r = pl.ds(pl.multiple_of((i * SCAN_STEPS + v) * CHUNK_TILES, CHUNK_TILES), CHUNK_TILES)
                new = []
                for k, ((a_re, a_im), (s_re, s_im)) in enumerate(zip(abar, state)):
                    b_re, b_im = _halves(s_ref, k, r)
                    s_re, s_im = a_re * s_re - a_im * s_im + b_re, a_re * s_im + a_im * s_re + b_im
                    s_ref[k, r, :LANES] = s_re
                    s_ref[k, r, LANES:] = s_im
                    new.append((s_re, s_im))
                state = tuple(new)
            return state

        state = lax.fori_loop(0, tt // SCAN_STEPS, steps, tuple(_halves(carry, k) for k in range(SSM_CHUNKS)))
        for k, (s_re, s_im) in enumerate(state):
            carry[k, :, :LANES] = s_re
            carry[k, :, LANES:] = s_im

        for k in range(SSM_CHUNKS):
            cols = slice(k * CHUNK_IN, (k + 1) * CHUNK_IN)
            full = lax.dot_general(s_ref[k].astype(BF16), c_ref[k], DOT_NT, preferred_element_type=F32)
            y = _gather(full, mask) + d_ref[:, cols] * u[:, cols]
            y_ref[:, cols] = y
            gel_ref[:, cols] = (0.5 * y * (1.0 + jnp.tanh(GELU_C * (y + GELU_A * y * y * y)))).astype(BF16)

    weight = _resident((SSM_CHUNKS, CHUNK_IN, PAIR))
    tokens = pl.BlockSpec((tt, SSM_WIDTH), lambda i: (i, 0))
    return pl.pallas_call(
        body, name="ssm_fwd", grid=(t // tt,),
        in_specs=[pl.BlockSpec((tt, SSM_WIDTH), lambda i: (i, U_COLUMN_BLOCK)), weight, weight,
                  _resident((SSM_CHUNKS, CHUNK_TILES, PAIR)), _resident((1, SSM_WIDTH))],
        out_specs=[tokens, tokens, pl.BlockSpec((SSM_CHUNKS, rows, PAIR), lambda i: (0, i, 0))],
        out_shape=[jax.ShapeDtypeStruct((t, SSM_WIDTH), F32), jax.ShapeDtypeStruct((t, SSM_WIDTH), BF16),
                   jax.ShapeDtypeStruct((SSM_CHUNKS, t * CHUNK_TILES, PAIR), F32)],
        scratch_shapes=[pltpu.VMEM((SSM_CHUNKS, CHUNK_TILES, PAIR), F32)],
        compiler_params=_params(("arbitrary",)),
    )(proj, w2, c2, a2, d_skip)


def _ssm_bwd(dy, proj, s, w2, c2, a2, d_skip):
    t = dy.shape[0]
    tt = _tile(t, SSM_ROWS)
    rows = tt * CHUNK_TILES
    n_chunks = t // tt

    def body(dy_ref, u_ref, s_ref, w_ref, c_ref, a_ref, d_ref, du_ref, gc_ref, gw_ref, ga_ref, gd_ref, z_ref, carry):
        @pl.when(pl.program_id(0) == 0)
        def _():
            for r in (carry, gc_ref, gw_ref, ga_ref, gd_ref):
                r[...] = jnp.zeros_like(r)

        mask = _tile_mask()
        dy_v = dy_ref[...]
        u = u_ref[...]
        gd_ref[...] += jnp.sum(dy_v * u, axis=0, keepdims=True)
        for k in range(SSM_CHUNKS):
            dk = _spread(dy_v[:, k * CHUNK_IN:(k + 1) * CHUNK_IN], mask).astype(BF16)
            z_ref[k] = jnp.dot(dk, c_ref[k], preferred_element_type=F32)
            gc_ref[k] += lax.dot_general(dk, s_ref[k].astype(BF16), DOT_TN, preferred_element_type=F32)

        abar = [_halves(a_ref, k) for k in range(SSM_CHUNKS)]

        def steps(i, state):
            zs, gs = state
            for v in range(SCAN_STEPS):
                tok = tt - 1 - (i * SCAN_STEPS + v)
                r = pl.ds(pl.multiple_of(tok * CHUNK_TILES, CHUNK_TILES), CHUNK_TILES)
                new_z, new_g = [], []
                for k, ((a_re, a_im), (z_re, z_im), (g_re, g_im)) in enumerate(zip(abar, zs, gs)):
                    s_re, s_im = _halves(s_ref, k, r)
                    g_re = g_re + z_re * s_re + z_im * s_im
                    g_im = g_im + z_im * s_re - z_re * s_im
                    d_re, d_im = _halves(z_ref, k, r)
                    z_re, z_im = d_re + a_re * z_re + a_im * z_im, d_im + a_re * z_im - a_im * z_re
                    z_ref[k, r, :LANES] = z_re
                    z_ref[k, r, LANES:] = z_im
                    new_z.append((z_re, z_im))
                    new_g.append((g_re, g_im))
                zs, gs = tuple(new_z), tuple(new_g)
            return zs, gs

        zs, gs = lax.fori_loop(0, tt // SCAN_STEPS, steps,
                               (tuple(_halves(carry, k) for k in range(SSM_CHUNKS)),
                                tuple(_halves(ga_ref, k) for k in range(SSM_CHUNKS))))
        for k in range(SSM_CHUNKS):
            carry[k, :, :LANES], carry[k, :, LANES:] = zs[k]
            ga_ref[k, :, :LANES], ga_ref[k, :, LANES:] = gs[k]

        for k in range(SSM_CHUNKS):
            cols = slice(k * CHUNK_IN, (k + 1) * CHUNK_IN)
            zb = z_ref[k].astype(BF16)
            full = lax.dot_general(zb, w_ref[k], DOT_NT, preferred_element_type=F32)
            du_ref[:, cols] = (_gather(full, mask) + d_ref[:, cols] * dy_v[:, cols]).astype(BF16)
            uk = _spread(u[:, cols], mask).astype(BF16)
            gw_ref[k] += lax.dot_general(uk, zb, DOT_TN, preferred_element_type=F32)

    weight = _resident((SSM_CHUNKS, CHUNK_IN, PAIR))
    tokens = pl.BlockSpec((tt, SSM_WIDTH), lambda i: (n_chunks - 1 - i, 0))
    grad = pl.BlockSpec((SSM_CHUNKS, CHUNK_IN, PAIR), lambda i: (0, 0, 0))
    return pl.pallas_call(
        body, name="ssm_bwd", grid=(n_chunks,),
        in_specs=[tokens, pl.BlockSpec((tt, SSM_WIDTH), lambda i: (n_chunks - 1 - i, U_COLUMN_BLOCK)),
                  pl.BlockSpec((SSM_CHUNKS, rows, PAIR), lambda i: (0, n_chunks - 1 - i, 0)), weight, weight,
                  _resident((SSM_CHUNKS, CHUNK_TILES, PAIR)), _resident((1, SSM_WIDTH))],
        out_specs=[tokens, grad, grad, pl.BlockSpec((SSM_CHUNKS, CHUNK_TILES, PAIR), lambda i: (0, 0, 0)),
                   pl.BlockSpec((1, SSM_WIDTH), lambda i: (0, 0))],
        out_shape=[jax.ShapeDtypeStruct((t, SSM_WIDTH), BF16), jax.ShapeDtypeStruct((SSM_CHUNKS, CHUNK_IN, PAIR), F32),
                   jax.ShapeDtypeStruct((SSM_CHUNKS, CHUNK_IN, PAIR), F32),
                   jax.ShapeDtypeStruct((SSM_CHUNKS, CHUNK_TILES, PAIR), F32), jax.ShapeDtypeStruct((1, SSM_WIDTH), F32)],
        scratch_shapes=[pltpu.VMEM((SSM_CHUNKS, rows, PAIR), F32), pltpu.VMEM((SSM_CHUNKS, CHUNK_TILES, PAIR), F32)],
        compiler_params=_params(("arbitrary",)),
    )(dy, proj, s, w2, c2, a2, d_skip)


def _block(ref, axis, size, index):
    idx = [slice(None)] * len(ref.shape)
    idx[axis] = pl.ds(pl.multiple_of(index * size, size), size)
    return ref.at[tuple(idx)]


def _all_gather(name, shards, axes):
    n = len(shards)
    sizes = [s.shape[a] for s, a in zip(shards, axes)]

    def body(*refs):
        ins, outs = refs[:n], refs[n:2 * n]
        send_sems, recv_sems, local_sems = refs[2 * n:]
        x, y, c = (lax.axis_index(a) for a in MESH_AXES)
        me, sibling = (x, y, c), (x, y, 1 - c)
        chips = [(1 - x, y), (x, 1 - y), (1 - x, 1 - y)]

        def rows(i, dev):
            return _block(outs[i], axes[i], sizes[i], 4 * dev[0] + 2 * dev[1] + dev[2])

        def copy(i, k, block, to, src=None):
            return pltpu.make_async_remote_copy(
                src_ref=rows(i, block) if src is None else src, dst_ref=rows(i, block),
                send_sem=send_sems.at[7 * i + k], recv_sem=recv_sems.at[7 * i + k],
                device_id=to, device_id_type=MESH)

        mine = [pltpu.make_async_copy(ins[i], rows(i, me), local_sems.at[i]) for i in range(n)]
        for cp in mine:
            cp.start()
        first = []
        for i in range(n):
            first.append(copy(i, 0, me, sibling, src=ins[i]))
            first += [copy(i, 1 + j, me, (*chip, c), src=ins[i]) for j, chip in enumerate(chips)]
        for cp in first:
            cp.start()
        passed = []
        for i in range(n):
            for j, chip in enumerate(chips):
                copy(i, 1 + j, (*chip, c), me).wait_recv()
                fwd = copy(i, 4 + j, (*chip, c), sibling)
                fwd.start()
                passed.append(fwd)
        for i in range(n):
            copy(i, 0, sibling, me).wait_recv()
            for j, chip in enumerate(chips):
                copy(i, 4 + j, (*chip, 1 - c), me).wait_recv()
        for cp in first + passed:
            cp.wait_send()
        for cp in mine:
            cp.wait()

    out_shape = []
    for s, a in zip(shards, axes):
        shape = list(s.shape)
        shape[a] *= N_DEV
        out_shape.append(jax.ShapeDtypeStruct(tuple(shape), s.dtype))
    any_spec = pl.BlockSpec(memory_space=pl.ANY)
    return pl.pallas_call(
        body, name=name, out_shape=out_shape,
        in_specs=[any_spec] * n, out_specs=[any_spec] * n,
        scratch_shapes=[pltpu.SemaphoreType.DMA((7 * n,)), pltpu.SemaphoreType.DMA((7 * n,)),
                        pltpu.SemaphoreType.DMA((n,))],
    )(*shards)


HBM_SPEC = pl.BlockSpec(memory_space=pltpu.HBM)
SEM_SPEC = pl.BlockSpec(memory_space=pltpu.SEMAPHORE)
ANY_SPEC = pl.BlockSpec(memory_space=pl.ANY)
SPLIT_PARAMS = pltpu.CompilerParams(has_side_effects=pltpu.SideEffectType.DATAFLOW_SIDE_EFFECTING)
N_PEERS = N_DEV - 1
TOKEN = jax.ShapeDtypeStruct((SUBLANES, LANES), F32)
VMEM_SPEC = pl.BlockSpec(memory_space=pltpu.VMEM)


def _in_hbm(arrays):
    return [pltpu.with_memory_space_constraint(a, pltpu.HBM) for a in arrays]


def _peer(m):
    x, y, c = (lax.axis_index(a) for a in MESH_AXES)
    px = 1 - x if m & 4 else x
    py = 1 - y if m & 2 else y
    pc = 1 - c if m & 1 else c
    return (px, py, pc), 4 * px + 2 * py + pc


def _my_index():
    x, y, c = (lax.axis_index(a) for a in MESH_AXES)
    return 4 * x + 2 * y + c


def _gather_copies(shard_refs, full_refs, axes, send_sems, recv_sems):
    copies = []
    for i, (shard, full) in enumerate(zip(shard_refs, full_refs)):
        mine = _block(full, axes[i], shard.shape[axes[i]], _my_index())
        for m in range(1, N_DEV):
            peer, _ = _peer(m)
            copies.append(pltpu.make_async_remote_copy(
                src_ref=shard, dst_ref=mine, send_sem=send_sems.at[N_PEERS * i + m - 1],
                recv_sem=recv_sems.at[N_PEERS * i + m - 1], device_id=peer, device_id_type=MESH))
    return copies


def _gather_start(name, shards, axes, after):
    n = len(shards)

    def body(*refs):
        shard_refs = refs[:n]
        send_sems, recv_sems, local_sems = refs[n + 1:n + 4]
        full_refs = refs[2 * n + 4:3 * n + 4]
        refs[3 * n + 4][...] = jnp.zeros(TOKEN.shape, TOKEN.dtype)
        for i in range(n):
            pltpu.make_async_copy(shard_refs[i], _block(full_refs[i], axes[i], shard_refs[i].shape[axes[i]], _my_index()),
                                  local_sems.at[i]).start()
        for cp in _gather_copies(shard_refs, full_refs, axes, send_sems, recv_sems):
            cp.start()

    fulls = []
    for s, a in zip(shards, axes):
        shape = list(s.shape)
        shape[a] *= N_DEV
        fulls.append(pltpu.HBM(tuple(shape), s.dtype))
    out = pl.pallas_call(
        body, name=name,
        out_shape=(pltpu.SemaphoreType.DMA((N_PEERS * n,)), pltpu.SemaphoreType.DMA((N_PEERS * n,)),
                   pltpu.SemaphoreType.DMA((n,)), *[pltpu.HBM(s.shape, s.dtype) for s in shards], *fulls, TOKEN),
        in_specs=[HBM_SPEC] * n + [ANY_SPEC],
        out_specs=(SEM_SPEC, SEM_SPEC, SEM_SPEC, *[HBM_SPEC] * (2 * n), VMEM_SPEC),
        input_output_aliases={i: 3 + i for i in range(n)},
        compiler_params=SPLIT_PARAMS,
    )(*_in_hbm(shards), after)
    return out[:-1], out[-1]


def _gather_wait(name, started, indices, axes, after):
    send_sems, recv_sems, local_sems = started[:3]
    n_all = (len(started) - 3) // 2
    shards = [started[3 + i] for i in indices]
    fulls = [started[3 + n_all + i] for i in indices]
    n = len(indices)

    def body(*refs):
        shard_refs, full_refs = refs[:n], refs[n:2 * n]
        send_sems, recv_sems, local_sems = refs[2 * n:2 * n + 3]
        for j, i in enumerate(indices):
            mine = _block(full_refs[j], axes[j], shard_refs[j].shape[axes[j]], _my_index())
            pltpu.make_async_copy(shard_refs[j], mine, local_sems.at[i]).wait()
            for m in range(1, N_DEV):
                peer, _ = _peer(m)
                cp = pltpu.make_async_remote_copy(
                    src_ref=shard_refs[j], dst_ref=mine, send_sem=send_sems.at[N_PEERS * i + m - 1],
                    recv_sem=recv_sems.at[N_PEERS * i + m - 1], device_id=peer, device_id_type=MESH)
                cp.wait_send()
                cp.wait_recv()

    out = pl.pallas_call(
        body, name=name,
        out_shape=tuple(pltpu.HBM(a.shape, a.dtype) for a in shards + fulls),
        in_specs=[HBM_SPEC] * (2 * n) + [SEM_SPEC] * 3 + [ANY_SPEC], out_specs=tuple([HBM_SPEC] * (2 * n)),
        input_output_aliases={i: i for i in range(2 * n)},
        compiler_params=SPLIT_PARAMS,
    )(*shards, *fulls, send_sems, recv_sems, local_sems, after)
    return out[n:]


def _exchange_start(name, fulls, axes):
    n = len(fulls)
    sizes = [f.shape[a] // N_DEV for f, a in zip(fulls, axes)]

    def body(*refs):
        ins = refs[:n]
        send_sems, recv_sems = refs[n:n + 2]
        lands = refs[2 * n + 2:3 * n + 2]
        refs[3 * n + 2][...] = jnp.zeros(TOKEN.shape, TOKEN.dtype)
        for i in range(n):
            for m in range(1, N_DEV):
                peer, index = _peer(m)
                pltpu.make_async_remote_copy(
                    src_ref=_block(ins[i], axes[i], sizes[i], index), dst_ref=lands[i].at[m - 1],
                    send_sem=send_sems.at[N_PEERS * i + m - 1], recv_sem=recv_sems.at[N_PEERS * i + m - 1],
                    device_id=peer, device_id_type=MESH).start()

    lands = []
    for f, a, size in zip(fulls, axes, sizes):
        shape = list(f.shape)
        shape[a] = size
        lands.append(pltpu.HBM((N_PEERS, *shape), f.dtype))
    out = pl.pallas_call(
        body, name=name,
        out_shape=(pltpu.SemaphoreType.DMA((N_PEERS * n,)), pltpu.SemaphoreType.DMA((N_PEERS * n,)),
                   *[pltpu.HBM(f.shape, f.dtype) for f in fulls], *lands, TOKEN),
        in_specs=[HBM_SPEC] * n, out_specs=(SEM_SPEC, SEM_SPEC, *[HBM_SPEC] * (2 * n), VMEM_SPEC),
        input_output_aliases={i: 2 + i for i in range(n)},
        compiler_params=SPLIT_PARAMS,
    )(*_in_hbm(fulls))
    return out[:-1], out[-1]


def _exchange_wait(name, started, axes, after):
    send_sems, recv_sems = started[:2]
    n = (len(started) - 2) // 2
    fulls, lands = list(started[2:2 + n]), list(started[2 + n:])
    sizes = [f.shape[a] // N_DEV for f, a in zip(fulls, axes)]

    def body(*refs):
        ins, land_refs = refs[:n], refs[n:2 * n]
        send_sems, recv_sems = refs[2 * n:2 * n + 2]
        for i in range(n):
            for m in range(1, N_DEV):
                peer, index = _peer(m)
                cp = pltpu.make_async_remote_copy(
                    src_ref=_block(ins[i], axes[i], sizes[i], index), dst_ref=land_refs[i].at[m - 1],
                    send_sem=send_sems.at[N_PEERS * i + m - 1], recv_sem=recv_sems.at[N_PEERS * i + m - 1],
                    device_id=peer, device_id_type=MESH)
                cp.wait_send()
                cp.wait_recv()

    out = pl.pallas_call(
        body, name=name,
        out_shape=tuple(pltpu.HBM(a.shape, a.dtype) for a in fulls + lands),
        in_specs=[HBM_SPEC] * (2 * n) + [SEM_SPEC] * 2 + [ANY_SPEC], out_specs=tuple([HBM_SPEC] * (2 * n)),
        input_output_aliases={i: i for i in range(2 * n)},
        compiler_params=SPLIT_PARAMS,
    )(*fulls, *lands, send_sems, recv_sems, after)
    return out[:n], out[n:]


def _adamw_update(w_ref, m_ref, v_ref, part_refs, g_ref, d_ref, nm_ref, nv_ref):
    c1 = 1.0 - ADAM_B1 ** ADAM_STEP
    c2 = 1.0 - ADAM_B2 ** ADAM_STEP
    g = None
    for p_ref in part_refs:
        stacked = len(p_ref.shape) > len(w_ref.shape)
        terms = [p_ref[s] for s in range(p_ref.shape[0])] if stacked else [p_ref[...]]
        for term in terms:
            term = term.astype(F32)
            g = term if g is None else g + term
    new_m = ADAM_B1 * m_ref[...] + (1.0 - ADAM_B1) * g
    new_v = ADAM_B2 * v_ref[...] + (1.0 - ADAM_B2) * (g * g)
    g_ref[...] = g
    nm_ref[...] = new_m
    nv_ref[...] = new_v
    d_ref[...] = -ADAM_LR * ((new_m / c1) / (jnp.sqrt(new_v / c2) + ADAM_EPS) + ADAM_WD * w_ref[...])


def _adamw_small(ws, ms, vs, stacks):
    n = len(ws)

    def body(*refs):
        ins, outs = refs[:4 * n], refs[4 * n:]
        for i in range(n):
            _adamw_update(ins[i], ins[n + i], ins[2 * n + i], [ins[3 * n + i]],
                          outs[i], outs[n + i], outs[2 * n + i], outs[3 * n + i])

    res = pl.pallas_call(
        body, name="adamw_small",
        out_shape=[jax.ShapeDtypeStruct(w.shape, F32) for w in ws] * 4,
        compiler_params=_params(None),
    )(*ws, *ms, *vs, *stacks)
    return res[:n], res[n:2 * n], res[2 * n:3 * n], res[3 * n:]


def _adamw(name, w, m, v, parts):
    r, c = w.shape
    tr = _tile(r, 256)
    n_parts = len(parts)

    def body(*refs):
        _adamw_update(refs[0], refs[1], refs[2], refs[3:3 + n_parts], *refs[3 + n_parts:])

    row = pl.BlockSpec((tr, c), lambda i: (i, 0))
    in_specs = [row, row, row]
    for p in parts:
        in_specs.append(row if p.ndim == 2 else pl.BlockSpec((p.shape[0], tr, c), lambda i: (0, i, 0)))
    return pl.pallas_call(
        body, name=name, grid=(r // tr,), in_specs=in_specs, out_specs=[row] * 4,
        out_shape=[jax.ShapeDtypeStruct((r, c), F32)] * 4,
        compiler_params=_params(("arbitrary",)),
    )(w, m, v, *parts)


SMALL = ("norm_gain", "pool_scale", "a_re", "a_im", "log_dt", "b_re", "b_im", "c_re", "c_im", "d_skip", "final_gain")
LARGE = ("w_in", "w_pool", "w_glu", "w_out", "w_ple", "w_ple_gate")
LARGE_AXIS = {"w_in": 1, "w_pool": 1, "w_glu": 1, "w_out": 0, "w_ple": 1, "w_ple_gate": 0}
WEIGHTS = ("norm_gain", "w_in", "w_pool", "pool_scale", "a_re", "a_im", "log_dt", "b_re", "b_im", "c_re", "c_im",
           "d_skip", "w_glu", "w_out", "w_ple", "w_ple_gate", "final_gain")


def kernel(x, p, norm_gain, w_in, w_pool, pool_scale, a_re, a_im, log_dt, b_re, b_im, c_re, c_im, d_skip, w_glu, w_out, w_ple, w_ple_gate, final_gain, loss_target, m_norm_gain, m_w_in, m_w_pool, m_pool_scale, m_a_re, m_a_im, m_log_dt, m_b_re, m_b_im, m_c_re, m_c_im, m_d_skip, m_w_glu, m_w_out, m_w_ple, m_w_ple_gate, m_final_gain, v_norm_gain, v_w_in, v_w_pool, v_pool_scale, v_a_re, v_a_im, v_log_dt, v_b_re, v_b_im, v_c_re, v_c_im, v_d_skip, v_w_glu, v_w_out, v_w_ple, v_w_ple_gate, v_final_gain):
    weights = dict(norm_gain=norm_gain, w_in=w_in, w_pool=w_pool, pool_scale=pool_scale, a_re=a_re, a_im=a_im,
                   log_dt=log_dt, b_re=b_re, b_im=b_im, c_re=c_re, c_im=c_im, d_skip=d_skip, w_glu=w_glu,
                   w_out=w_out, w_ple=w_ple, w_ple_gate=w_ple_gate, final_gain=final_gain)
    mom_m = dict(norm_gain=m_norm_gain, w_in=m_w_in, w_pool=m_w_pool, pool_scale=m_pool_scale, a_re=m_a_re,
                 a_im=m_a_im, log_dt=m_log_dt, b_re=m_b_re, b_im=m_b_im, c_re=m_c_re, c_im=m_c_im,
                 d_skip=m_d_skip, w_glu=m_w_glu, w_out=m_w_out, w_ple=m_w_ple, w_ple_gate=m_w_ple_gate,
                 final_gain=m_final_gain)
    mom_v = dict(norm_gain=v_norm_gain, w_in=v_w_in, w_pool=v_w_pool, pool_scale=v_pool_scale, a_re=v_a_re,
                 a_im=v_a_im, log_dt=v_log_dt, b_re=v_b_re, b_im=v_b_im, c_re=v_c_re, c_im=v_c_im,
                 d_skip=v_d_skip, w_glu=v_w_glu, w_out=v_w_out, w_ple=v_w_ple, w_ple_gate=v_w_ple_gate,
                 final_gain=v_final_gain)

    t = x.shape[1]
    xs = x.reshape(t, D_MODEL)
    ps = p.reshape(t, PLE_DIM)
    target = loss_target.reshape(t, D_MODEL)
    gain1 = norm_gain.reshape(1, D_MODEL)
    gain_f = final_gain.reshape(1, D_MODEL)
    scale_p = pool_scale.reshape(1, POOL_WIDTH)
    skip = d_skip.reshape(1, SSM_WIDTH)

    shard2d = {k: weights[k][0] for k in LARGE}
    shard_bf = {k: shard2d[k].astype(BF16) for k in LARGE}
    full = {"w_in": _all_gather("w_in_all_gather", [shard_bf["w_in"]], [LARGE_AXIS["w_in"]])[0]}
    later = [k for k in LARGE if k != "w_in"]
    later_axes = [LARGE_AXIS[k] for k in later]
    gather, gather_token = _gather_start("weights_gather_start", [shard_bf[k] for k in later], later_axes,
                                         full["w_in"])

    def arrive(k, after):
        i = later.index(k)
        full[k] = _gather_wait("gather_wait_" + k, gather, [i], [later_axes[i]], after)[0]

    ar, ai = a_re[0], a_im[0]
    ldt = log_dt.reshape(N_SSM_GROUPS, 1)
    br_t = jnp.transpose(b_re[0], (0, 2, 1))
    bi_t = jnp.transpose(b_im[0], (0, 2, 1))
    ab_re, ab_im, bb_re, bb_im = _ssm_params(ar, ai, ldt, br_t, bi_t)
    tiles = (SSM_CHUNKS, CHUNK_TILES, LANES)
    abar = jnp.concatenate([ab_re.reshape(tiles), ab_im.reshape(tiles)], axis=-1)
    w_pair = _compact_pair(bb_re, bb_im)
    c_pair = _compact_pair(c_re[0], -c_im[0])

    hn = _norm1_fwd(xs, gain1)
    proj = _mm_nn("in_proj", hn, full["w_in"], [F32], tk=2048, after=[gather_token])[0]
    pooled = _pool_fwd(proj)
    tm = _tile(t, 1024)
    arrive("w_pool", pooled)
    mixed = _mm("pool_mix", [(pooled, (tm, POOL_GROUP), lambda i, j, s: (i, j),
                              full["w_pool"], (None, POOL_GROUP, POOL_GROUP), lambda i, j, s: (j, 0, 0))],
                DOT_NN, (t // tm, N_POOL_GROUPS, 1),
                [((t, POOL_WIDTH), F32, (tm, POOL_GROUP), lambda i, j, s: (i, j))], 1)[0]
    y, gel, states = _ssm_fwd(proj, w_pair, c_pair, abar, skip)
    arrive("w_glu", gel)
    hg = _mm_nn("glu_proj", gel, full["w_glu"], [F32])[0]
    cat = _gate_fwd(mixed, proj, hg, scale_p)

    def residual_epilogue(acc, ex, out_refs):
        h = acc + ex[0][...]
        out_refs[0][...] = h
        out_refs[1][...] = h.astype(BF16)

    arrive("w_out", cat)
    h1, h1b = _mm_nn("out_proj", cat, full["w_out"], [F32, BF16], extras=[xs], epilogue=residual_epilogue)
    arrive("w_ple", h1b)
    e = _mm_nn("ple_proj", ps, full["w_ple"], [F32])[0]
    arrive("w_ple_gate", e)
    q = _mm_nn("ple_gate_proj", h1b, full["w_ple_gate"], [F32], tk=2048)[0]
    de, dq, dh2, g_final_gain, loss_part = _final(h1, e, q, target, gain_f)
    loss = lax.psum(loss_part[0, 0], MESH_AXES)

    grads = {}
    grads["w_ple_gate"] = _mm_tn("ple_gate_wgrad", h1b, dq, BF16)
    grads["w_ple"] = _mm_tn("ple_wgrad", ps, de, BF16)
    sent, tokens = {}, {}

    def send(names):
        sent[names], tokens[names[0]] = _exchange_start(
            "grads_start_" + names[0], [grads[k] for k in names], [LARGE_AXIS[k] for k in names])

    send(("w_ple_gate", "w_ple"))
    dh1, dh1b = _mm_nt("ple_gate_dgrad", dq, full["w_ple_gate"], [F32, BF16], extras=[dh2],
                       epilogue=residual_epilogue)
    grads["w_out"] = _mm_tn("out_wgrad", cat, dh1b, BF16)
    send(("w_out",))
    dcat = _mm_nt("out_dgrad", dh1b, full["w_out"], [F32], tk=2048, after=[tokens["w_ple_gate"], tokens["w_out"]])[0]
    dmixed, dga, dgb, dhg, g_pool_scale = _gate_bwd(dcat, mixed, proj, hg, scale_p)

    tk = _tile(t, 1024)
    grads["w_pool"] = _mm("pool_wgrad", [(pooled, (tk, POOL_GROUP), lambda i, j, s: (s, i),
                                          dmixed, (tk, POOL_GROUP), lambda i, j, s: (s, i))],
                          DOT_TN, (N_POOL_GROUPS, 1, t // tk),
                          [((N_POOL_GROUPS, POOL_GROUP, POOL_GROUP), BF16, (None, POOL_GROUP, POOL_GROUP),
                            lambda i, j, s: (i, 0, 0))], t // tk)[0]
    dpooled = _mm("pool_dgrad", [(dmixed, (tm, POOL_GROUP), lambda i, j, s: (i, j),
                                  full["w_pool"], (None, POOL_GROUP, POOL_GROUP), lambda i, j, s: (j, 0, 0))],
                  DOT_NT, (t // tm, N_POOL_GROUPS, 1),
                  [((t, POOL_WIDTH), F32, (tm, POOL_GROUP), lambda i, j, s: (i, j))], 1)[0]
    dua = _pool_bwd(dpooled)

    grads["w_glu"] = _mm_tn("glu_wgrad", gel, dhg, BF16)
    send(("w_pool", "w_glu"))

    def gelu_bwd_epilogue(acc, ex, out_refs):
        yv = ex[0][...]
        th = jnp.tanh(GELU_C * (yv + GELU_A * yv * yv * yv))
        dgelu = 0.5 * (1.0 + th) + 0.5 * yv * (1.0 - th * th) * GELU_C * (1.0 + 3.0 * GELU_A * yv * yv)
        out_refs[0][...] = acc * dgelu

    dy = _mm_nt("glu_dgrad", dhg, full["w_glu"], [F32], tk=2048, extras=[y], epilogue=gelu_bwd_epilogue,
                after=[tokens["w_pool"]])[0]
    dub, g_c_pair, g_w_pair, g_abar, g_d_skip = _ssm_bwd(dy, proj, states, w_pair, c_pair, abar, skip)

    g_ab_re = g_abar[..., :LANES].reshape(N_SSM_GROUPS, SSM_STATE)
    g_ab_im = g_abar[..., LANES:].reshape(N_SSM_GROUPS, SSM_STATE)
    d_ar, d_ai, d_ldt, d_br_t, d_bi_t = _ssm_params_bwd(
        ar, ai, ldt, br_t, bi_t, g_ab_re, g_ab_im,
        _expand_grad(g_w_pair[..., :LANES]), _expand_grad(g_w_pair[..., LANES:]))

    dproj = jnp.concatenate([dua, dga, dub, dgb], axis=1)
    grads["w_in"] = _mm_tn("in_wgrad", hn, dproj, BF16)
    send(("w_in",))
    dhn = _mm_nt("in_dgrad", dproj, full["w_in"], [F32], tk=2048, after=[tokens["w_in"]])[0]
    grad_x, g_norm_gain = _norm1_bwd(xs, dhn, dh1, gain1)

    def b_view(a):
        return jnp.transpose(a[0], (0, 2, 1))

    views = dict(norm_gain=lambda a: a, pool_scale=lambda a: a, a_re=lambda a: a[0], a_im=lambda a: a[0],
                 log_dt=lambda a: a, b_re=b_view, b_im=b_view, c_re=lambda a: a[0], c_im=lambda a: a[0],
                 d_skip=lambda a: a, final_gain=lambda a: a.reshape(1, D_MODEL))
    small_grads = dict(
        norm_gain=g_norm_gain, pool_scale=g_pool_scale, a_re=d_ar, a_im=d_ai, log_dt=d_ldt.reshape(1, N_SSM_GROUPS),
        b_re=d_br_t, b_im=d_bi_t, c_re=_expand_grad(g_c_pair[..., :LANES]),
        c_im=-_expand_grad(g_c_pair[..., LANES:]), d_skip=g_d_skip,
        final_gain=g_final_gain)
    stacks = _all_gather("small_grads_all_gather", [small_grads[k][None] for k in SMALL], [0] * len(SMALL))
    small_out = _adamw_small([views[k](weights[k]) for k in SMALL], [views[k](mom_m[k]) for k in SMALL],
                             [views[k](mom_v[k]) for k in SMALL], stacks)
    out_g, out_d, out_m, out_v = ({} for _ in range(4))
    for out, res in zip((out_g, out_d, out_m, out_v), small_out):
        for k, r in zip(SMALL, res):
            if k in ("b_re", "b_im"):
                r = jnp.transpose(r, (0, 2, 1))
            out[k] = r.reshape(weights[k].shape)

    me = 4 * lax.axis_index("x") + 2 * lax.axis_index("y") + lax.axis_index("c")
    after = grad_x
    for names, started in sent.items():
        axes = [LARGE_AXIS[k] for k in names]
        partials, landed = _exchange_wait("grads_wait_" + names[0], started, axes, after)
        for k, axis, partial, land in zip(names, axes, partials, landed):
            shard_shape = shard2d[k].shape
            size = shard_shape[axis]
            own = lax.dynamic_slice_in_dim(partial, me * size, size, axis=axis)
            view = (-1, shard_shape[-1])
            rows = math.prod(shard_shape[:-1])
            res = _adamw("adamw_" + k, shard2d[k].reshape(view), mom_m[k][0].reshape(view), mom_v[k][0].reshape(view),
                         [own.reshape(view), land.reshape(N_PEERS, rows, shard_shape[-1])])
            out_g[k], out_d[k], out_m[k], out_v[k] = (r.reshape(weights[k].shape) for r in res)
            after = res[0]

    return (loss, grad_x.reshape(x.shape), *[out_g[k] for k in WEIGHTS], *[out_d[k] for k in WEIGHTS],
            *[out_m[k] for k in WEIGHTS], *[out_v[k] for k in WEIGHTS])
```

```python
import math

import jax
import jax.numpy as jnp
from jax import lax
from jax.experimental import pallas as pl
from jax.experimental.pallas import tpu as pltpu

F32 = jnp.float32
BF16 = jnp.bfloat16
MESH = pl.DeviceIdType.MESH
MESH_AXES = ("x", "y", "c")
N_DEV = 8

D_MODEL = 2048
POOL_WIDTH = 1024
SSM_WIDTH = 1024
N_POOL_GROUPS = 4
POOL_GROUP = 256
SSM_GROUP = 16
N_SSM_GROUPS = 64
SSM_STATE = 64
SSM_FLAT = N_SSM_GROUPS * SSM_STATE
SSM_CHUNKS = 4
CHUNK_IN = SSM_WIDTH // SSM_CHUNKS
CHUNK_STATE = SSM_FLAT // SSM_CHUNKS
PLE_DIM = 256
EPS = 1e-6
A_RE_MAX = -1e-4
ADAM_LR = 0.001
ADAM_B1 = 0.9
ADAM_B2 = 0.999
ADAM_EPS = 1e-08
ADAM_WD = 0.01
ADAM_STEP = 10
GELU_C = math.sqrt(2.0 / math.pi)
GELU_A = 0.044715

SUBLANES = 8
LANES = 128
VMEM_LIMIT_BYTES = 48 * 1024 * 1024

DOT_NN = (((1,), (0,)), ((), ()))
DOT_NT = (((1,), (1,)), ((), ()))
DOT_TN = (((0,), (0,)), ((), ()))


def _tile(n, pref):
    return pref if n % pref == 0 else n


def _params(sem):
    return pltpu.CompilerParams(dimension_semantics=sem, vmem_limit_bytes=VMEM_LIMIT_BYTES)


def _sigmoid(v):
    return 1.0 / (1.0 + jnp.exp(-v))


def _silu_and_grad(v):
    s = _sigmoid(v)
    return v * s, s * (1.0 + v * (1.0 - s))


def _mm(name, pairs, dims, grid, outs, k_steps, extras=(), epilogue=None):
    n_pairs, n_ex, n_out = len(pairs), len(extras), len(outs)
    acc_shape = tuple(d for d in outs[0][2] if d is not None)
    if epilogue is None:
        def epilogue(acc, ex, out_refs):
            out_refs[0][...] = acc.astype(out_refs[0].dtype)

    def body(*refs):
        ab = refs[:2 * n_pairs]
        ex = refs[2 * n_pairs:2 * n_pairs + n_ex]
        out_refs = refs[2 * n_pairs + n_ex:2 * n_pairs + n_ex + n_out]
        acc = refs[-1]
        k = pl.program_id(2)

        @pl.when(k == 0)
        def _():
            acc[...] = jnp.zeros_like(acc)

        part = None
        for q in range(n_pairs):
            d = lax.dot_general(ab[2 * q][...].astype(BF16), ab[2 * q + 1][...].astype(BF16), dims,
                                preferred_element_type=F32)
            part = d if part is None else part + d
        acc[...] += part

        @pl.when(k == k_steps - 1)
        def _():
            epilogue(acc[...], ex, out_refs)

    in_specs, operands = [], []
    for a, a_blk, a_map, b, b_blk, b_map in pairs:
        in_specs += [pl.BlockSpec(a_blk, a_map), pl.BlockSpec(b_blk, b_map)]
        operands += [a, b]
    for e, e_blk, e_map in extras:
        in_specs.append(pl.BlockSpec(e_blk, e_map))
        operands.append(e)
    return pl.pallas_call(
        body, name=name, grid=grid, in_specs=in_specs,
        out_specs=[pl.BlockSpec(o[2], o[3]) for o in outs],
        out_shape=[jax.ShapeDtypeStruct(o[0], o[1]) for o in outs],
        scratch_shapes=[pltpu.VMEM(acc_shape, F32)],
        compiler_params=_params(("arbitrary", "arbitrary", "arbitrary")),
    )(*operands)


def _after(tokens):
    return [(tok, tok.shape, lambda i, j, s: (0, 0)) for tok in tokens]


def _mm_nn(name, a, b, out_dtypes, tm=1024, tn=1024, tk=1024, a_col0=0, extras=(), epilogue=None, after=()):
    m, n = a.shape[0], b.shape[1]
    k = b.shape[0]
    tm, tn, tk = _tile(m, tm), _tile(n, tn), _tile(k, tk)
    outs = [((m, n), dt, (tm, tn), lambda i, j, s: (i, j)) for dt in out_dtypes]
    ex = [(e, (tm, tn), lambda i, j, s: (i, j)) for e in extras] + _after(after)
    return _mm(name, [(a, (tm, tk), lambda i, j, s: (i, a_col0 + s), b, (tk, tn), lambda i, j, s: (s, j))],
               DOT_NN, (m // tm, n // tn, k // tk), outs, k // tk, ex, epilogue)


def _mm_nt(name, a, b, out_dtypes, tm=1024, tn=1024, tk=1024, extras=(), epilogue=None, after=()):
    m, kk = a.shape
    n = b.shape[0]
    tm, tn, tk = _tile(m, tm), _tile(n, tn), _tile(kk, tk)
    outs = [((m, n), dt, (tm, tn), lambda i, j, s: (i, j)) for dt in out_dtypes]
    ex = [(e, (tm, tn), lambda i, j, s: (i, j)) for e in extras] + _after(after)
    return _mm(name, [(a, (tm, tk), lambda i, j, s: (i, s), b, (tn, tk), lambda i, j, s: (j, s))],
               DOT_NT, (m // tm, n // tn, kk // tk), outs, kk // tk, ex, epilogue)


def _mm_tn(name, a, b, out_dtype, tm=512, tn=2048, tk=1024):
    m, kk = a.shape
    n = b.shape[1]
    tm, tn, tk = _tile(kk, tm), _tile(n, tn), _tile(m, tk)
    outs = [((kk, n), out_dtype, (tm, tn), lambda i, j, s: (i, j))]
    return _mm(name, [(a, (tk, tm), lambda i, j, s: (s, i), b, (tk, tn), lambda i, j, s: (s, j))],
               DOT_TN, (kk // tm, n // tn, m // tk), outs, m // tk)[0]


def _norm1_fwd(x, gain):
    t = x.shape[0]
    tm = _tile(t, 512)

    def body(x_ref, g_ref, hn_ref):
        xv = x_ref[...]
        r = lax.rsqrt(jnp.mean(xv * xv, axis=-1, keepdims=True) + EPS)
        hn_ref[...] = (xv * r * g_ref[...]).astype(BF16)

    return pl.pallas_call(
        body, name="norm1_fwd", grid=(t // tm,),
        in_specs=[pl.BlockSpec((tm, D_MODEL), lambda i: (i, 0)), pl.BlockSpec((1, D_MODEL), lambda i: (0, 0))],
        out_specs=pl.BlockSpec((tm, D_MODEL), lambda i: (i, 0)),
        out_shape=jax.ShapeDtypeStruct((t, D_MODEL), BF16),
        compiler_params=_params(("arbitrary",)),
    )(x, gain)


def _norm1_bwd(x, dhn, dh1, gain):
    t = x.shape[0]
    tm = _tile(t, 512)

    def body(x_ref, dhn_ref, dh1_ref, g_ref, dx_ref, gg_ref):
        @pl.when(pl.program_id(0) == 0)
        def _():
            gg_ref[...] = jnp.zeros_like(gg_ref)

        xv = x_ref[...]
        r = lax.rsqrt(jnp.mean(xv * xv, axis=-1, keepdims=True) + EPS)
        xh = xv * r
        dhn_v = dhn_ref[...]
        gg_ref[...] += jnp.sum(dhn_v * xh, axis=0, keepdims=True)
        dxh = dhn_v * g_ref[...]
        dx_ref[...] = dh1_ref[...] + r * (dxh - xh * jnp.mean(dxh * xh, axis=-1, keepdims=True))

    row = pl.BlockSpec((tm, D_MODEL), lambda i: (i, 0))
    vec = pl.BlockSpec((1, D_MODEL), lambda i: (0, 0))
    return pl.pallas_call(
        body, name="norm1_bwd", grid=(t // tm,),
        in_specs=[row, row, row, vec], out_specs=[row, vec],
        out_shape=[jax.ShapeDtypeStruct((t, D_MODEL), F32), jax.ShapeDtypeStruct((1, D_MODEL), F32)],
        compiler_params=_params(("arbitrary",)),
    )(x, dhn, dh1, gain)


def _pool_counts(t, width, group):
    row = lax.broadcasted_iota(jnp.int32, (t, width), 0)
    window = jnp.left_shift(jnp.int32(2), group)
    return row, jnp.minimum(row + 1, window).astype(F32)


def _select_window(group, s2, s4, s8, s16):
    return jnp.where(group == 0, s2, jnp.where(group == 1, s4, jnp.where(group == 2, s8, s16)))


def _pool_fwd(proj):
    t = proj.shape[0]
    tc = LANES

    def body(u_ref, o_ref):
        group = pl.program_id(0) // (POOL_GROUP // tc)
        v = u_ref[...]
        row, count = _pool_counts(t, tc, group)

        def down(a, j):
            return jnp.where(row >= j, pltpu.roll(a, j, 0), 0.0)

        s2 = v + down(v, 1)
        s4 = s2 + down(s2, 2)
        s8 = s4 + down(s4, 4)
        s16 = s8 + down(s8, 8)
        o_ref[...] = (_select_window(group, s2, s4, s8, s16) / count - v).astype(BF16)

    return pl.pallas_call(
        body, name="pool_fwd", grid=(POOL_WIDTH // tc,),
        in_specs=[pl.BlockSpec((t, tc), lambda j: (0, j))],
        out_specs=pl.BlockSpec((t, tc), lambda j: (0, j)),
        out_shape=jax.ShapeDtypeStruct((t, POOL_WIDTH), BF16),
        compiler_params=_params(("arbitrary",)),
    )(proj)


def _pool_bwd(dpooled):
    t = dpooled.shape[0]
    tc = LANES

    def body(d_ref, o_ref):
        group = pl.program_id(0) // (POOL_GROUP // tc)
        dp = d_ref[...]
        row, count = _pool_counts(t, tc, group)
        r = dp / count

        def up(a, j):
            return jnp.where(row < t - j, pltpu.roll(a, t - j, 0), 0.0)

        s2 = r + up(r, 1)
        s4 = s2 + up(s2, 2)
        s8 = s4 + up(s4, 4)
        s16 = s8 + up(s8, 8)
        o_ref[...] = (_select_window(group, s2, s4, s8, s16) - dp).astype(BF16)

    return pl.pallas_call(
        body, name="pool_bwd", grid=(POOL_WIDTH // tc,),
        in_specs=[pl.BlockSpec((t, tc), lambda j: (0, j))],
        out_specs=pl.BlockSpec((t, tc), lambda j: (0, j)),
        out_shape=jax.ShapeDtypeStruct((t, POOL_WIDTH), BF16),
        compiler_params=_params(("arbitrary",)),
    )(dpooled)


def _gate_fwd(mixed, proj, hg, pool_scale):
    t = mixed.shape[0]
    tm = _tile(t, 512)

    def body(mx_ref, ga_ref, gb_ref, hg_ref, ps_ref, cat_ref):
        silu_a, _ = _silu_and_grad(ga_ref[...])
        cat_ref[:, :POOL_WIDTH] = (mx_ref[...] * ps_ref[...] * silu_a).astype(BF16)
        silu_b, _ = _silu_and_grad(gb_ref[...])
        sb = hg_ref[:, :SSM_WIDTH] * _sigmoid(hg_ref[:, SSM_WIDTH:])
        cat_ref[:, POOL_WIDTH:] = (sb * silu_b).astype(BF16)

    return pl.pallas_call(
        body, name="gate_fwd", grid=(t // tm,),
        in_specs=[pl.BlockSpec((tm, POOL_WIDTH), lambda i: (i, 0)),
                  pl.BlockSpec((tm, POOL_WIDTH), lambda i: (i, 1)),
                  pl.BlockSpec((tm, SSM_WIDTH), lambda i: (i, 3)),
                  pl.BlockSpec((tm, 2 * SSM_WIDTH), lambda i: (i, 0)),
                  pl.BlockSpec((1, POOL_WIDTH), lambda i: (0, 0))],
        out_specs=pl.BlockSpec((tm, D_MODEL), lambda i: (i, 0)),
        out_shape=jax.ShapeDtypeStruct((t, D_MODEL), BF16),
        compiler_params=_params(("arbitrary",)),
    )(mixed, proj, proj, hg, pool_scale)


def _gate_bwd(dcat, mixed, proj, hg, pool_scale):
    t = mixed.shape[0]
    tm = _tile(t, 512)

    def body(dc_ref, mx_ref, ga_ref, gb_ref, hg_ref, ps_ref, dmx_ref, dga_ref, dgb_ref, dhg_ref, gps_ref):
        @pl.when(pl.program_id(0) == 0)
        def _():
            gps_ref[...] = jnp.zeros_like(gps_ref)

        ps = ps_ref[...]
        mx = mx_ref[...]
        dya = dc_ref[:, :POOL_WIDTH]
        silu_a, dsilu_a = _silu_and_grad(ga_ref[...])
        dpa = dya * silu_a
        gps_ref[...] += jnp.sum(dpa * mx, axis=0, keepdims=True)
        dmx_ref[...] = (dpa * ps).astype(BF16)
        dga_ref[...] = (dya * mx * ps * dsilu_a).astype(BF16)

        dyb = dc_ref[:, POOL_WIDTH:]
        silu_b, dsilu_b = _silu_and_grad(gb_ref[...])
        h_a = hg_ref[:, :SSM_WIDTH]
        sg = _sigmoid(hg_ref[:, SSM_WIDTH:])
        dsb = dyb * silu_b
        dgb_ref[...] = (dyb * h_a * sg * dsilu_b).astype(BF16)
        dhg_ref[:, :SSM_WIDTH] = (dsb * sg).astype(BF16)
        dhg_ref[:, SSM_WIDTH:] = (dsb * h_a * sg * (1.0 - sg)).astype(BF16)

    half = pl.BlockSpec((tm, POOL_WIDTH), lambda i: (i, 0))
    full = pl.BlockSpec((tm, D_MODEL), lambda i: (i, 0))
    vec = pl.BlockSpec((1, POOL_WIDTH), lambda i: (0, 0))
    return pl.pallas_call(
        body, name="gate_bwd", grid=(t // tm,),
        in_specs=[full, half,
                  pl.BlockSpec((tm, POOL_WIDTH), lambda i: (i, 1)),
                  pl.BlockSpec((tm, SSM_WIDTH), lambda i: (i, 3)),
                  full, vec],
        out_specs=[half, half, half, full, vec],
        out_shape=[jax.ShapeDtypeStruct((t, POOL_WIDTH), BF16), jax.ShapeDtypeStruct((t, POOL_WIDTH), BF16),
                   jax.ShapeDtypeStruct((t, SSM_WIDTH), BF16), jax.ShapeDtypeStruct((t, 2 * SSM_WIDTH), BF16),
                   jax.ShapeDtypeStruct((1, POOL_WIDTH), F32)],
        compiler_params=_params(("arbitrary",)),
    )(dcat, mixed, proj, proj, hg, pool_scale)


def _final(h1, e, q, target, gain):
    t = h1.shape[0]
    tm = _tile(t, 256)

    def body(h1_ref, e_ref, q_ref, tg_ref, g_ref, de_ref, dq_ref, dh2_ref, gg_ref, loss_ref):
        @pl.when(pl.program_id(0) == 0)
        def _():
            gg_ref[...] = jnp.zeros_like(gg_ref)
            loss_ref[...] = jnp.zeros_like(loss_ref)

        ev = e_ref[...]
        sg = _sigmoid(q_ref[...])
        h2 = h1_ref[...] + ev * sg
        r = lax.rsqrt(jnp.mean(h2 * h2, axis=-1, keepdims=True) + EPS)
        n = h2 * r
        gain_v = g_ref[...]
        diff = n * gain_v - tg_ref[...]
        row_loss = jnp.sum(diff * diff, axis=-1, keepdims=True)
        loss_ref[...] += (0.5 / D_MODEL) * jnp.sum(row_loss, axis=0, keepdims=True)
        dout = diff * (1.0 / D_MODEL)
        gg_ref[...] += jnp.sum(dout * n, axis=0, keepdims=True)
        dn = dout * gain_v
        dh2 = r * (dn - n * jnp.mean(dn * n, axis=-1, keepdims=True))
        dh2_ref[...] = dh2
        de_ref[...] = (dh2 * sg).astype(BF16)
        dq_ref[...] = (dh2 * ev * sg * (1.0 - sg)).astype(BF16)

    row = pl.BlockSpec((tm, D_MODEL), lambda i: (i, 0))
    vec = pl.BlockSpec((1, D_MODEL), lambda i: (0, 0))
    return pl.pallas_call(
        body, name="final_norm_loss", grid=(t // tm,),
        in_specs=[row, row, row, row, vec],
        out_specs=[row, row, row, vec, pl.BlockSpec((1, 1), lambda i: (0, 0))],
        out_shape=[jax.ShapeDtypeStruct((t, D_MODEL), BF16), jax.ShapeDtypeStruct((t, D_MODEL), BF16),
                   jax.ShapeDtypeStruct((t, D_MODEL), F32), jax.ShapeDtypeStruct((1, D_MODEL), F32),
                   jax.ShapeDtypeStruct((1, 1), F32)],
        compiler_params=_params(("arbitrary",)),
    )(h1, e, q, target, gain)


def _zoh(a_re, a_im, log_dt, b_re_t, b_im_t):
    lam_re = jnp.minimum(a_re, A_RE_MAX)
    lam_im = a_im
    dt = jnp.exp(log_dt)
    mag = jnp.exp(lam_re * dt)
    ang = lam_im * dt
    ab_re = mag * jnp.cos(ang)
    ab_im = mag * jnp.sin(ang)
    den = lam_re * lam_re + lam_im * lam_im
    n_re = ab_re - 1.0
    n_im = ab_im
    q_re = (n_re * lam_re + n_im * lam_im) / den
    q_im = (n_im * lam_re - n_re * lam_im) / den
    bb_re = q_re[:, None, :] * b_re_t - q_im[:, None, :] * b_im_t
    bb_im = q_re[:, None, :] * b_im_t + q_im[:, None, :] * b_re_t
    return ab_re, ab_im, bb_re, bb_im


def _ssm_params(a_re, a_im, log_dt, b_re_t, b_im_t):
    def body(are_ref, aim_ref, dt_ref, bre_ref, bim_ref, abre_ref, abim_ref, bbre_ref, bbim_ref):
        ab_re, ab_im, bb_re, bb_im = _zoh(are_ref[...], aim_ref[...], dt_ref[...], bre_ref[...], bim_ref[...])
        abre_ref[...] = ab_re
        abim_ref[...] = ab_im
        bbre_ref[...] = bb_re
        bbim_ref[...] = bb_im

    return pl.pallas_call(
        body, name="ssm_params",
        out_shape=[jax.ShapeDtypeStruct(a_re.shape, F32), jax.ShapeDtypeStruct(a_re.shape, F32),
                   jax.ShapeDtypeStruct(b_re_t.shape, F32), jax.ShapeDtypeStruct(b_re_t.shape, F32)],
        compiler_params=_params(None),
    )(a_re, a_im, log_dt, b_re_t, b_im_t)


def _ssm_params_bwd(a_re, a_im, log_dt, b_re_t, b_im_t, g_ab_re, g_ab_im, g_bb_re, g_bb_im):
    def body(are_ref, aim_ref, dt_ref, bre_ref, bim_ref, gar_ref, gai_ref, gbr_ref, gbi_ref,
             o_are, o_aim, o_dt, o_bre, o_bim):
        _, vjp = jax.vjp(_zoh, are_ref[...], aim_ref[...], dt_ref[...], bre_ref[...], bim_ref[...])
        d_are, d_aim, d_dt, d_bre, d_bim = vjp((gar_ref[...], gai_ref[...], gbr_ref[...], gbi_ref[...]))
        o_are[...] = d_are
        o_aim[...] = d_aim
        o_dt[...] = d_dt
        o_bre[...] = d_bre
        o_bim[...] = d_bim

    ins = (a_re, a_im, log_dt, b_re_t, b_im_t)
    return pl.pallas_call(
        body, name="ssm_params_bwd",
        out_shape=[jax.ShapeDtypeStruct(v.shape, F32) for v in ins],
        compiler_params=_params(None),
    )(*ins, g_ab_re, g_ab_im, g_bb_re, g_bb_im)


CHUNK_TILES = CHUNK_STATE // LANES
CH_PER_TILE = CHUNK_IN // CHUNK_TILES
PAIR = 2 * LANES
SSM_ROWS = 256
SCAN_STEPS = 8
U_COLUMN_BLOCK = 2 * POOL_WIDTH // SSM_WIDTH


def _own_half():
    r = lax.broadcasted_iota(jnp.int32, (CHUNK_IN, LANES), 0) // SSM_GROUP % 2
    c = lax.broadcasted_iota(jnp.int32, (CHUNK_IN, LANES), 1) // SSM_STATE
    return (r == c)[None]


def _compact_weight(w):
    tiled = jnp.tile(w.reshape(SSM_CHUNKS, CHUNK_IN, SSM_STATE), (1, 1, 2))
    return jnp.where(_own_half(), tiled, 0.0)


def _compact_pair(w_a, w_b):
    return jnp.concatenate([_compact_weight(w_a), _compact_weight(w_b)], axis=-1).astype(BF16)


def _expand_grad(g):
    kept = jnp.where(_own_half(), g, 0.0)
    return kept.reshape(SSM_CHUNKS, CHUNK_IN, 2, SSM_STATE).sum(axis=2).reshape(N_SSM_GROUPS, SSM_GROUP, SSM_STATE)


TILES_PER_BLOCK = LANES // CH_PER_TILE
IN_BLOCKS = CHUNK_IN // LANES


def _tile_masks():
    j = lax.broadcasted_iota(jnp.int32, (CHUNK_TILES, LANES), 0) % TILES_PER_BLOCK
    lane = lax.broadcasted_iota(jnp.int32, (CHUNK_TILES, LANES), 1) // CH_PER_TILE
    return (j == lane).astype(F32)


def _tile_rows(ref, j, tt):
    return ref.at[j // TILES_PER_BLOCK, pl.ds(j, tt, stride=CHUNK_TILES), :]


def _spread(ref, v, masks):
    tt = v.shape[0]
    for j in range(CHUNK_TILES):
        block = LANES * (j // TILES_PER_BLOCK)
        _tile_rows(ref, j, tt)[...] = v[:, block:block + LANES] * masks[j:j + 1, :]
    return jnp.concatenate([ref[b] for b in range(IN_BLOCKS)], axis=1).astype(BF16)


def _gather(ref, full, masks):
    tt = full.shape[0] // CHUNK_TILES
    for b in range(IN_BLOCKS):
        ref[b] = full[:, b * LANES:(b + 1) * LANES]
    out = []
    for b in range(IN_BLOCKS):
        acc = None
        for j in range(b * TILES_PER_BLOCK, (b + 1) * TILES_PER_BLOCK):
            part = _tile_rows(ref, j, tt)[...] * masks[j:j + 1, :]
            acc = part if acc is None else acc + part
        out.append(acc)
    return jnp.concatenate(out, axis=1)


def _resident(shape):
    return pl.BlockSpec(shape, lambda i: (0,) * len(shape), pipeline_mode=pl.Buffered(1))


def _halves(ref, k, rows=slice(None)):
    return ref[k, rows, :LANES], ref[k, rows, LANES:]


def _ssm_fwd(proj, w2, c2, a2, d_skip):
    t = proj.shape[0]
    tt = _tile(t, SSM_ROWS)
    rows = tt * CHUNK_TILES

    def body(u_ref, w_ref, c_ref, a_ref, d_ref, y_ref, gel_ref, s_ref, carry, spread_ref, full_ref):
        @pl.when(pl.program_id(0) == 0)
        def _():
            carry[...] = jnp.zeros_like(carry)
            spread_ref[...] = jnp.zeros_like(spread_ref)

        mask = _tile_masks()
        u = u_ref[...]
        for k in range(SSM_CHUNKS):
            uk = _spread(spread_ref, u[:, k * CHUNK_IN:(k + 1) * CHUNK_IN], mask)
            s_ref[k] = jnp.dot(uk, w_ref[k], preferred_element_type=F32)

        abar = [_halves(a_ref, k) for k in range(SSM_CHUNKS)]

        def steps(i, state):
            for v in range(SCAN_STEPS):
                r = pl.ds(pl.multiple_of((i * SCAN_STEPS + v) * CHUNK_TILES, CHUNK_TILES), CHUNK_TILES)
                new = []
                for k, ((a_re, a_im), (s_re, s_im)) in enumerate(zip(abar, state)):
                    b_re, b_im = _halves(s_ref, k, r)
                    s_re, s_im = a_re * s_re - a_im * s_im + b_re, a_re * s_im + a_im * s_re + b_im
                    s_ref[k, r, :LANES] = s_re
                    s_ref[k, r, LANES:] = s_im
                    new.append((s_re, s_im))
                state = tuple(new)
            return state

        state = lax.fori_loop(0, tt // SCAN_STEPS, steps, tuple(_halves(carry, k) for k in range(SSM_CHUNKS)))
        for k, (s_re, s_im) in enumerate(state):
            carry[k, :, :LANES] = s_re
            carry[k, :, LANES:] = s_im

        for k in range(SSM_CHUNKS):
            cols = slice(k * CHUNK_IN, (k + 1) * CHUNK_IN)
            full = lax.dot_general(s_ref[k].astype(BF16), c_ref[k], DOT_NT, preferred_element_type=F32)
            y = _gather(full_ref, full, mask) + d_ref[:, cols] * u[:, cols]
            y_ref[:, cols] = y
            gel_ref[:, cols] = (0.5 * y * (1.0 + jnp.tanh(GELU_C * (y + GELU_A * y * y * y)))).astype(BF16)

    weight = _resident((SSM_CHUNKS, CHUNK_IN, PAIR))
    tokens = pl.BlockSpec((tt, SSM_WIDTH), lambda i: (i, 0))
    return pl.pallas_call(
        body, name="ssm_fwd", grid=(t // tt,),
        in_specs=[pl.BlockSpec((tt, SSM_WIDTH), lambda i: (i, U_COLUMN_BLOCK)), weight, weight,
                  _resident((SSM_CHUNKS, CHUNK_TILES, PAIR)), _resident((1, SSM_WIDTH))],
        out_specs=[tokens, tokens, pl.BlockSpec((SSM_CHUNKS, rows, PAIR), lambda i: (0, i, 0))],
        out_shape=[jax.ShapeDtypeStruct((t, SSM_WIDTH), F32), jax.ShapeDtypeStruct((t, SSM_WIDTH), BF16),
                   jax.ShapeDtypeStruct((SSM_CHUNKS, t * CHUNK_TILES, PAIR), F32)],
        scratch_shapes=[pltpu.VMEM((SSM_CHUNKS, CHUNK_TILES, PAIR), F32), pltpu.VMEM((IN_BLOCKS, rows, LANES), F32),
                        pltpu.VMEM((IN_BLOCKS, rows, LANES), F32)],
        compiler_params=_params(("arbitrary",)),
    )(proj, w2, c2, a2, d_skip)


def _ssm_bwd(dy, proj, s, w2, c2, a2, d_skip):
    t = dy.shape[0]
    tt = _tile(t, SSM_ROWS)
    rows = tt * CHUNK_TILES
    n_chunks = t // tt

    def body(dy_ref, u_ref, s_ref, w_ref, c_ref, a_ref, d_ref, du_ref, gc_ref, gw_ref, ga_ref, gd_ref, z_ref, carry,
             spread_ref, full_ref):
        @pl.when(pl.program_id(0) == 0)
        def _():
            for r in (carry, gc_ref, gw_ref, ga_ref, gd_ref, spread_ref):
                r[...] = jnp.zeros_like(r)

        mask = _tile_masks()
        dy_v = dy_ref[...]
        u = u_ref[...]
        gd_ref[...] += jnp.sum(dy_v * u, axis=0, keepdims=True)
        for k in range(SSM_CHUNKS):
            dk = _spread(spread_ref, dy_v[:, k * CHUNK_IN:(k + 1) * CHUNK_IN], mask)
            z_ref[k] = jnp.dot(dk, c_ref[k], preferred_element_type=F32)
            gc_ref[k] += lax.dot_general(dk, s_ref[k].astype(BF16), DOT_TN, preferred_element_type=F32)

        abar = [_halves(a_ref, k) for k in range(SSM_CHUNKS)]

        def steps(i, state):
            zs, gs = state
            for v in range(SCAN_STEPS):
                tok = tt - 1 - (i * SCAN_STEPS + v)
                r = pl.ds(pl.multiple_of(tok * CHUNK_TILES, CHUNK_TILES), CHUNK_TILES)
                new_z, new_g = [], []
                for k, ((a_re, a_im), (z_re, z_im), (g_re, g_im)) in enumerate(zip(abar, zs, gs)):
                    s_re, s_im = _halves(s_ref, k, r)
                    g_re = g_re + z_re * s_re + z_im * s_im
                    g_im = g_im + z_im * s_re - z_re * s_im
                    d_re, d_im = _halves(z_ref, k, r)
                    z_re, z_im = d_re + a_re * z_re + a_im * z_im, d_im + a_re * z_im - a_im * z_re
                    z_ref[k, r, :LANES] = z_re
                    z_ref[k, r, LANES:] = z_im
                    new_z.append((z_re, z_im))
                    new_g.append((g_re, g_im))
                zs, gs = tuple(new_z), tuple(new_g)
            return zs, gs

        zs, gs = lax.fori_loop(0, tt // SCAN_STEPS, steps,
                               (tuple(_halves(carry, k) for k in range(SSM_CHUNKS)),
                                tuple(_halves(ga_ref, k) for k in range(SSM_CHUNKS))))
        for k in range(SSM_CHUNKS):
            carry[k, :, :LANES], carry[k, :, LANES:] = zs[k]
            ga_ref[k, :, :LANES], ga_ref[k, :, LANES:] = gs[k]

        for k in range(SSM_CHUNKS):
            cols = slice(k * CHUNK_IN, (k + 1) * CHUNK_IN)
            zb = z_ref[k].astype(BF16)
            full = lax.dot_general(zb, w_ref[k], DOT_NT, preferred_element_type=F32)
            du_ref[:, cols] = (_gather(full_ref, full, mask) + d_ref[:, cols] * dy_v[:, cols]).astype(BF16)
            uk = _spread(spread_ref, u[:, cols], mask)
            gw_ref[k] += lax.dot_general(uk, zb, DOT_TN, preferred_element_type=F32)

    weight = _resident((SSM_CHUNKS, CHUNK_IN, PAIR))
    tokens = pl.BlockSpec((tt, SSM_WIDTH), lambda i: (n_chunks - 1 - i, 0))
    grad = pl.BlockSpec((SSM_CHUNKS, CHUNK_IN, PAIR), lambda i: (0, 0, 0))
    return pl.pallas_call(
        body, name="ssm_bwd", grid=(n_chunks,),
        in_specs=[tokens, pl.BlockSpec((tt, SSM_WIDTH), lambda i: (n_chunks - 1 - i, U_COLUMN_BLOCK)),
                  pl.BlockSpec((SSM_CHUNKS, rows, PAIR), lambda i: (0, n_chunks - 1 - i, 0)), weight, weight,
                  _resident((SSM_CHUNKS, CHUNK_TILES, PAIR)), _resident((1, SSM_WIDTH))],
        out_specs=[tokens, grad, grad, pl.BlockSpec((SSM_CHUNKS, CHUNK_TILES, PAIR), lambda i: (0, 0, 0)),
                   pl.BlockSpec((1, SSM_WIDTH), lambda i: (0, 0))],
        out_shape=[jax.ShapeDtypeStruct((t, SSM_WIDTH), BF16), jax.ShapeDtypeStruct((SSM_CHUNKS, CHUNK_IN, PAIR), F32),
                   jax.ShapeDtypeStruct((SSM_CHUNKS, CHUNK_IN, PAIR), F32),
                   jax.ShapeDtypeStruct((SSM_CHUNKS, CHUNK_TILES, PAIR), F32), jax.ShapeDtypeStruct((1, SSM_WIDTH), F32)],
        scratch_shapes=[pltpu.VMEM((SSM_CHUNKS, rows, PAIR), F32), pltpu.VMEM((SSM_CHUNKS, CHUNK_TILES, PAIR), F32),
                        pltpu.VMEM((IN_BLOCKS, rows, LANES), F32), pltpu.VMEM((IN_BLOCKS, rows, LANES), F32)],
        compiler_params=_params(("arbitrary",)),
    )(dy, proj, s, w2, c2, a2, d_skip)


def _block(ref, axis, size, index):
    idx = [slice(None)] * len(ref.shape)
    idx[axis] = pl.ds(pl.multiple_of(index * size, size), size)
    return ref.at[tuple(idx)]


def _all_gather(name, shards, axes):
    n = len(shards)
    sizes = [s.shape[a] for s, a in zip(shards, axes)]

    def body(*refs):
        ins, outs = refs[:n], refs[n:2 * n]
        send_sems, recv_sems, local_sems = refs[2 * n:]
        x, y, c = (lax.axis_index(a) for a in MESH_AXES)
        me, sibling = (x, y, c), (x, y, 1 - c)
        chips = [(1 - x, y), (x, 1 - y), (1 - x, 1 - y)]

        def rows(i, dev):
            return _block(outs[i], axes[i], sizes[i], 4 * dev[0] + 2 * dev[1] + dev[2])

        def copy(i, k, block, to, src=None):
            return pltpu.make_async_remote_copy(
                src_ref=rows(i, block) if src is None else src, dst_ref=rows(i, block),
                send_sem=send_sems.at[7 * i + k], recv_sem=recv_sems.at[7 * i + k],
                device_id=to, device_id_type=MESH)

        mine = [pltpu.make_async_copy(ins[i], rows(i, me), local_sems.at[i]) for i in range(n)]
        for cp in mine:
            cp.start()
        first = []
        for i in range(n):
            first.append(copy(i, 0, me, sibling, src=ins[i]))
            first += [copy(i, 1 + j, me, (*chip, c), src=ins[i]) for j, chip in enumerate(chips)]
        for cp in first:
            cp.start()
        passed = []
        for i in range(n):
            for j, chip in enumerate(chips):
                copy(i, 1 + j, (*chip, c), me).wait_recv()
                fwd = copy(i, 4 + j, (*chip, c), sibling)
                fwd.start()
                passed.append(fwd)
        for i in range(n):
            copy(i, 0, sibling, me).wait_recv()
            for j, chip in enumerate(chips):
                copy(i, 4 + j, (*chip, 1 - c), me).wait_recv()
        for cp in first + passed:
            cp.wait_send()
        for cp in mine:
            cp.wait()

    out_shape = []
    for s, a in zip(shards, axes):
        shape = list(s.shape)
        shape[a] *= N_DEV
        out_shape.append(jax.ShapeDtypeStruct(tuple(shape), s.dtype))
    any_spec = pl.BlockSpec(memory_space=pl.ANY)
    return pl.pallas_call(
        body, name=name, out_shape=out_shape,
        in_specs=[any_spec] * n, out_specs=[any_spec] * n,
        scratch_shapes=[pltpu.SemaphoreType.DMA((7 * n,)), pltpu.SemaphoreType.DMA((7 * n,)),
                        pltpu.SemaphoreType.DMA((n,))],
    )(*shards)


HBM_SPEC = pl.BlockSpec(memory_space=pltpu.HBM)
SEM_SPEC = pl.BlockSpec(memory_space=pltpu.SEMAPHORE)
ANY_SPEC = pl.BlockSpec(memory_space=pl.ANY)
SPLIT_PARAMS = pltpu.CompilerParams(has_side_effects=pltpu.SideEffectType.DATAFLOW_SIDE_EFFECTING)
N_PEERS = N_DEV - 1
TOKEN = jax.ShapeDtypeStruct((SUBLANES, LANES), F32)
VMEM_SPEC = pl.BlockSpec(memory_space=pltpu.VMEM)


def _in_hbm(arrays):
    return [pltpu.with_memory_space_constraint(a, pltpu.HBM) for a in arrays]


def _peer(m):
    x, y, c = (lax.axis_index(a) for a in MESH_AXES)
    px = 1 - x if m & 4 else x
    py = 1 - y if m & 2 else y
    pc = 1 - c if m & 1 else c
    return (px, py, pc), 4 * px + 2 * py + pc


def _my_index():
    x, y, c = (lax.axis_index(a) for a in MESH_AXES)
    return 4 * x + 2 * y + c


def _gather_copies(shard_refs, full_refs, axes, send_sems, recv_sems):
    copies = []
    for i, (shard, full) in enumerate(zip(shard_refs, full_refs)):
        mine = _block(full, axes[i], shard.shape[axes[i]], _my_index())
        for m in range(1, N_DEV):
            peer, _ = _peer(m)
            copies.append(pltpu.make_async_remote_copy(
                src_ref=shard, dst_ref=mine, send_sem=send_sems.at[N_PEERS * i + m - 1],
                recv_sem=recv_sems.at[N_PEERS * i + m - 1], device_id=peer, device_id_type=MESH))
    return copies


def _gather_start(name, shards, axes, after):
    n = len(shards)

    def body(*refs):
        shard_refs = refs[:n]
        send_sems, recv_sems, local_sems = refs[n + 1:n + 4]
        full_refs = refs[2 * n + 4:3 * n + 4]
        refs[3 * n + 4][...] = jnp.zeros(TOKEN.shape, TOKEN.dtype)
        for i in range(n):
            pltpu.make_async_copy(shard_refs[i], _block(full_refs[i], axes[i], shard_refs[i].shape[axes[i]], _my_index()),
                                  local_sems.at[i]).start()
        for cp in _gather_copies(shard_refs, full_refs, axes, send_sems, recv_sems):
            cp.start()

    fulls = []
    for s, a in zip(shards, axes):
        shape = list(s.shape)
        shape[a] *= N_DEV
        fulls.append(pltpu.HBM(tuple(shape), s.dtype))
    out = pl.pallas_call(
        body, name=name,
        out_shape=(pltpu.SemaphoreType.DMA((N_PEERS * n,)), pltpu.SemaphoreType.DMA((N_PEERS * n,)),
                   pltpu.SemaphoreType.DMA((n,)), *[pltpu.HBM(s.shape, s.dtype) for s in shards], *fulls, TOKEN),
        in_specs=[HBM_SPEC] * n + [ANY_SPEC],
        out_specs=(SEM_SPEC, SEM_SPEC, SEM_SPEC, *[HBM_SPEC] * (2 * n), VMEM_SPEC),
        input_output_aliases={i: 3 + i for i in range(n)},
        compiler_params=SPLIT_PARAMS,
    )(*_in_hbm(shards), after)
    return out[:-1], out[-1]


def _gather_wait(name, started, indices, axes, after):
    send_sems, recv_sems, local_sems = started[:3]
    n_all = (len(started) - 3) // 2
    shards = [started[3 + i] for i in indices]
    fulls = [started[3 + n_all + i] for i in indices]
    n = len(indices)

    def body(*refs):
        shard_refs, full_refs = refs[:n], refs[n:2 * n]
        send_sems, recv_sems, local_sems = refs[2 * n:2 * n + 3]
        for j, i in enumerate(indices):
            mine = _block(full_refs[j], axes[j], shard_refs[j].shape[axes[j]], _my_index())
            pltpu.make_async_copy(shard_refs[j], mine, local_sems.at[i]).wait()
            for m in range(1, N_DEV):
                peer, _ = _peer(m)
                cp = pltpu.make_async_remote_copy(
                    src_ref=shard_refs[j], dst_ref=mine, send_sem=send_sems.at[N_PEERS * i + m - 1],
                    recv_sem=recv_sems.at[N_PEERS * i + m - 1], device_id=peer, device_id_type=MESH)
                cp.wait_send()
                cp.wait_recv()

    out = pl.pallas_call(
        body, name=name,
        out_shape=tuple(pltpu.HBM(a.shape, a.dtype) for a in shards + fulls),
        in_specs=[HBM_SPEC] * (2 * n) + [SEM_SPEC] * 3 + [ANY_SPEC], out_specs=tuple([HBM_SPEC] * (2 * n)),
        input_output_aliases={i: i for i in range(2 * n)},
        compiler_params=SPLIT_PARAMS,
    )(*shards, *fulls, send_sems, recv_sems, local_sems, after)
    return out[n:]


def _exchange_start(name, fulls, axes):
    n = len(fulls)
    sizes = [f.shape[a] // N_DEV for f, a in zip(fulls, axes)]

    def body(*refs):
        ins = refs[:n]
        send_sems, recv_sems = refs[n:n + 2]
        lands = refs[2 * n + 2:3 * n + 2]
        refs[3 * n + 2][...] = jnp.zeros(TOKEN.shape, TOKEN.dtype)
        for i in range(n):
            for m in range(1, N_DEV):
                peer, index = _peer(m)
                pltpu.make_async_remote_copy(
                    src_ref=_block(ins[i], axes[i], sizes[i], index), dst_ref=lands[i].at[m - 1],
                    send_sem=send_sems.at[N_PEERS * i + m - 1], recv_sem=recv_sems.at[N_PEERS * i + m - 1],
                    device_id=peer, device_id_type=MESH).start()

    lands = []
    for f, a, size in zip(fulls, axes, sizes):
        shape = list(f.shape)
        shape[a] = size
        lands.append(pltpu.HBM((N_PEERS, *shape), f.dtype))
    out = pl.pallas_call(
        body, name=name,
        out_shape=(pltpu.SemaphoreType.DMA((N_PEERS * n,)), pltpu.SemaphoreType.DMA((N_PEERS * n,)),
                   *[pltpu.HBM(f.shape, f.dtype) for f in fulls], *lands, TOKEN),
        in_specs=[HBM_SPEC] * n, out_specs=(SEM_SPEC, SEM_SPEC, *[HBM_SPEC] * (2 * n), VMEM_SPEC),
        input_output_aliases={i: 2 + i for i in range(n)},
        compiler_params=SPLIT_PARAMS,
    )(*_in_hbm(fulls))
    return out[:-1], out[-1]


def _exchange_wait(name, started, axes, after):
    send_sems, recv_sems = started[:2]
    n = (len(started) - 2) // 2
    fulls, lands = list(started[2:2 + n]), list(started[2 + n:])
    sizes = [f.shape[a] // N_DEV for f, a in zip(fulls, axes)]

    def body(*refs):
        ins, land_refs = refs[:n], refs[n:2 * n]
        send_sems, recv_sems = refs[2 * n:2 * n + 2]
        for i in range(n):
            for m in range(1, N_DEV):
                peer, index = _peer(m)
                cp = pltpu.make_async_remote_copy(
                    src_ref=_block(ins[i], axes[i], sizes[i], index), dst_ref=land_refs[i].at[m - 1],
                    send_sem=send_sems.at[N_PEERS * i + m - 1], recv_sem=recv_sems.at[N_PEERS * i + m - 1],
                    device_id=peer, device_id_type=MESH)
                cp.wait_send()
                cp.wait_recv()

    out = pl.pallas_call(
        body, name=name,
        out_shape=tuple(pltpu.HBM(a.shape, a.dtype) for a in fulls + lands),
        in_specs=[HBM_SPEC] * (2 * n) + [SEM_SPEC] * 2 + [ANY_SPEC], out_specs=tuple([HBM_SPEC] * (2 * n)),
        input_output_aliases={i: i for i in range(2 * n)},
        compiler_params=SPLIT_PARAMS,
    )(*fulls, *lands, send_sems, recv_sems, after)
    return out[:n], out[n:]


def _adamw_update(w_ref, m_ref, v_ref, part_refs, g_ref, d_ref, nm_ref, nv_ref):
    c1 = 1.0 - ADAM_B1 ** ADAM_STEP
    c2 = 1.0 - ADAM_B2 ** ADAM_STEP
    g = None
    for p_ref in part_refs:
        stacked = len(p_ref.shape) > len(w_ref.shape)
        terms = [p_ref[s] for s in range(p_ref.shape[0])] if stacked else [p_ref[...]]
        for term in terms:
            term = term.astype(F32)
            g = term if g is None else g + term
    new_m = ADAM_B1 * m_ref[...] + (1.0 - ADAM_B1) * g
    new_v = ADAM_B2 * v_ref[...] + (1.0 - ADAM_B2) * (g * g)
    g_ref[...] = g
    nm_ref[...] = new_m
    nv_ref[...] = new_v
    d_ref[...] = -ADAM_LR * ((new_m / c1) / (jnp.sqrt(new_v / c2) + ADAM_EPS) + ADAM_WD * w_ref[...])


def _adamw_small(ws, ms, vs, stacks):
    n = len(ws)

    def body(*refs):
        ins, outs = refs[:4 * n], refs[4 * n:]
        for i in range(n):
            _adamw_update(ins[i], ins[n + i], ins[2 * n + i], [ins[3 * n + i]],
                          outs[i], outs[n + i], outs[2 * n + i], outs[3 * n + i])

    res = pl.pallas_call(
        body, name="adamw_small",
        out_shape=[jax.ShapeDtypeStruct(w.shape, F32) for w in ws] * 4,
        compiler_params=_params(None),
    )(*ws, *ms, *vs, *stacks)
    return res[:n], res[n:2 * n], res[2 * n:3 * n], res[3 * n:]


def _adamw(name, w, m, v, parts):
    r, c = w.shape
    tr = _tile(r, 256)
    n_parts = len(parts)

    def body(*refs):
        _adamw_update(refs[0], refs[1], refs[2], refs[3:3 + n_parts], *refs[3 + n_parts:])

    row = pl.BlockSpec((tr, c), lambda i: (i, 0))
    in_specs = [row, row, row]
    for p in parts:
        in_specs.append(row if p.ndim == 2 else pl.BlockSpec((p.shape[0], tr, c), lambda i: (0, i, 0)))
    return pl.pallas_call(
        body, name=name, grid=(r // tr,), in_specs=in_specs, out_specs=[row] * 4,
        out_shape=[jax.ShapeDtypeStruct((r, c), F32)] * 4,
        compiler_params=_params(("arbitrary",)),
    )(w, m, v, *parts)


SMALL = ("norm_gain", "pool_scale", "a_re", "a_im", "log_dt", "b_re", "b_im", "c_re", "c_im", "d_skip", "final_gain")
LARGE = ("w_in", "w_pool", "w_glu", "w_out", "w_ple", "w_ple_gate")
LARGE_AXIS = {"w_in": 1, "w_pool": 1, "w_glu": 1, "w_out": 0, "w_ple": 1, "w_ple_gate": 0}
WEIGHTS = ("norm_gain", "w_in", "w_pool", "pool_scale", "a_re", "a_im", "log_dt", "b_re", "b_im", "c_re", "c_im",
           "d_skip", "w_glu", "w_out", "w_ple", "w_ple_gate", "final_gain")


def kernel(x, p, norm_gain, w_in, w_pool, pool_scale, a_re, a_im, log_dt, b_re, b_im, c_re, c_im, d_skip, w_glu, w_out, w_ple, w_ple_gate, final_gain, loss_target, m_norm_gain, m_w_in, m_w_pool, m_pool_scale, m_a_re, m_a_im, m_log_dt, m_b_re, m_b_im, m_c_re, m_c_im, m_d_skip, m_w_glu, m_w_out, m_w_ple, m_w_ple_gate, m_final_gain, v_norm_gain, v_w_in, v_w_pool, v_pool_scale, v_a_re, v_a_im, v_log_dt, v_b_re, v_b_im, v_c_re, v_c_im, v_d_skip, v_w_glu, v_w_out, v_w_ple, v_w_ple_gate, v_final_gain):
    weights = dict(norm_gain=norm_gain, w_in=w_in, w_pool=w_pool, pool_scale=pool_scale, a_re=a_re, a_im=a_im,
                   log_dt=log_dt, b_re=b_re, b_im=b_im, c_re=c_re, c_im=c_im, d_skip=d_skip, w_glu=w_glu,
                   w_out=w_out, w_ple=w_ple, w_ple_gate=w_ple_gate, final_gain=final_gain)
    mom_m = dict(norm_gain=m_norm_gain, w_in=m_w_in, w_pool=m_w_pool, pool_scale=m_pool_scale, a_re=m_a_re,
                 a_im=m_a_im, log_dt=m_log_dt, b_re=m_b_re, b_im=m_b_im, c_re=m_c_re, c_im=m_c_im,
                 d_skip=m_d_skip, w_glu=m_w_glu, w_out=m_w_out, w_ple=m_w_ple, w_ple_gate=m_w_ple_gate,
                 final_gain=m_final_gain)
    mom_v = dict(norm_gain=v_norm_gain, w_in=v_w_in, w_pool=v_w_pool, pool_scale=v_pool_scale, a_re=v_a_re,
                 a_im=v_a_im, log_dt=v_log_dt, b_re=v_b_re, b_im=v_b_im, c_re=v_c_re, c_im=v_c_im,
                 d_skip=v_d_skip, w_glu=v_w_glu, w_out=v_w_out, w_ple=v_w_ple, w_ple_gate=v_w_ple_gate,
                 final_gain=v_final_gain)

    t = x.shape[1]
    xs = x.reshape(t, D_MODEL)
    ps = p.reshape(t, PLE_DIM)
    target = loss_target.reshape(t, D_MODEL)
    gain1 = norm_gain.reshape(1, D_MODEL)
    gain_f = final_gain.reshape(1, D_MODEL)
    scale_p = pool_scale.reshape(1, POOL_WIDTH)
    skip = d_skip.reshape(1, SSM_WIDTH)

    shard2d = {k: weights[k][0] for k in LARGE}
    shard_bf = {k: shard2d[k].astype(BF16) for k in LARGE}
    full = {"w_in": _all_gather("w_in_all_gather", [shard_bf["w_in"]], [LARGE_AXIS["w_in"]])[0]}
    later = [k for k in LARGE if k != "w_in"]
    later_axes = [LARGE_AXIS[k] for k in later]
    gather, gather_token = _gather_start("weights_gather_start", [shard_bf[k] for k in later], later_axes,
                                         full["w_in"])

    def arrive(k, after):
        i = later.index(k)
        full[k] = _gather_wait("gather_wait_" + k, gather, [i], [later_axes[i]], after)[0]

    ar, ai = a_re[0], a_im[0]
    ldt = log_dt.reshape(N_SSM_GROUPS, 1)
    br_t = jnp.transpose(b_re[0], (0, 2, 1))
    bi_t = jnp.transpose(b_im[0], (0, 2, 1))
    ab_re, ab_im, bb_re, bb_im = _ssm_params(ar, ai, ldt, br_t, bi_t)
    tiles = (SSM_CHUNKS, CHUNK_TILES, LANES)
    abar = jnp.concatenate([ab_re.reshape(tiles), ab_im.reshape(tiles)], axis=-1)
    w_pair = _compact_pair(bb_re, bb_im)
    c_pair = _compact_pair(c_re[0], -c_im[0])

    hn = _norm1_fwd(xs, gain1)
    proj = _mm_nn("in_proj", hn, full["w_in"], [F32], tk=2048, after=[gather_token])[0]
    pooled = _pool_fwd(proj)
    tm = _tile(t, 1024)
    arrive("w_pool", pooled)
    mixed = _mm("pool_mix", [(pooled, (tm, POOL_GROUP), lambda i, j, s: (i, j),
                              full["w_pool"], (None, POOL_GROUP, POOL_GROUP), lambda i, j, s: (j, 0, 0))],
                DOT_NN, (t // tm, N_POOL_GROUPS, 1),
                [((t, POOL_WIDTH), F32, (tm, POOL_GROUP), lambda i, j, s: (i, j))], 1)[0]
    y, gel, states = _ssm_fwd(proj, w_pair, c_pair, abar, skip)
    arrive("w_glu", gel)
    hg = _mm_nn("glu_proj", gel, full["w_glu"], [F32])[0]
    cat = _gate_fwd(mixed, proj, hg, scale_p)

    def residual_epilogue(acc, ex, out_refs):
        h = acc + ex[0][...]
        out_refs[0][...] = h
        out_refs[1][...] = h.astype(BF16)

    arrive("w_out", cat)
    h1, h1b = _mm_nn("out_proj", cat, full["w_out"], [F32, BF16], extras=[xs], epilogue=residual_epilogue)
    arrive("w_ple", h1b)
    e = _mm_nn("ple_proj", ps, full["w_ple"], [F32])[0]
    arrive("w_ple_gate", e)
    q = _mm_nn("ple_gate_proj", h1b, full["w_ple_gate"], [F32], tk=2048)[0]
    de, dq, dh2, g_final_gain, loss_part = _final(h1, e, q, target, gain_f)
    loss = lax.psum(loss_part[0, 0], MESH_AXES)

    grads = {}
    grads["w_ple_gate"] = _mm_tn("ple_gate_wgrad", h1b, dq, BF16)
    grads["w_ple"] = _mm_tn("ple_wgrad", ps, de, BF16)
    sent, tokens = {}, {}

    def send(names):
        sent[names], tokens[names[0]] = _exchange_start(
            "grads_start_" + names[0], [grads[k] for k in names], [LARGE_AXIS[k] for k in names])

    send(("w_ple_gate", "w_ple"))
    dh1, dh1b = _mm_nt("ple_gate_dgrad", dq, full["w_ple_gate"], [F32, BF16], extras=[dh2],
                       epilogue=residual_epilogue)
    grads["w_out"] = _mm_tn("out_wgrad", cat, dh1b, BF16)
    send(("w_out",))
    dcat = _mm_nt("out_dgrad", dh1b, full["w_out"], [F32], tk=2048, after=[tokens["w_ple_gate"], tokens["w_out"]])[0]
    dmixed, dga, dgb, dhg, g_pool_scale = _gate_bwd(dcat, mixed, proj, hg, scale_p)

    tk = _tile(t, 1024)
    grads["w_pool"] = _mm("pool_wgrad", [(pooled, (tk, POOL_GROUP), lambda i, j, s: (s, i),
                                          dmixed, (tk, POOL_GROUP), lambda i, j, s: (s, i))],
                          DOT_TN, (N_POOL_GROUPS, 1, t // tk),
                          [((N_POOL_GROUPS, POOL_GROUP, POOL_GROUP), BF16, (None, POOL_GROUP, POOL_GROUP),
                            lambda i, j, s: (i, 0, 0))], t // tk)[0]
    dpooled = _mm("pool_dgrad", [(dmixed, (tm, POOL_GROUP), lambda i, j, s: (i, j),
                                  full["w_pool"], (None, POOL_GROUP, POOL_GROUP), lambda i, j, s: (j, 0, 0))],
                  DOT_NT, (t // tm, N_POOL_GROUPS, 1),
                  [((t, POOL_WIDTH), F32, (tm, POOL_GROUP), lambda i, j, s: (i, j))], 1)[0]
    dua = _pool_bwd(dpooled)

    grads["w_glu"] = _mm_tn("glu_wgrad", gel, dhg, BF16)
    send(("w_pool", "w_glu"))

    def gelu_bwd_epilogue(acc, ex, out_refs):
        yv = ex[0][...]
        th = jnp.tanh(GELU_C * (yv + GELU_A * yv * yv * yv))
        dgelu = 0.5 * (1.0 + th) + 0.5 * yv * (1.0 - th * th) * GELU_C * (1.0 + 3.0 * GELU_A * yv * yv)
        out_refs[0][...] = acc * dgelu

    dy = _mm_nt("glu_dgrad", dhg, full["w_glu"], [F32], tk=2048, extras=[y], epilogue=gelu_bwd_epilogue,
                after=[tokens["w_pool"]])[0]
    dub, g_c_pair, g_w_pair, g_abar, g_d_skip = _ssm_bwd(dy, proj, states, w_pair, c_pair, abar, skip)

    g_ab_re = g_abar[..., :LANES].reshape(N_SSM_GROUPS, SSM_STATE)
    g_ab_im = g_abar[..., LANES:].reshape(N_SSM_GROUPS, SSM_STATE)
    d_ar, d_ai, d_ldt, d_br_t, d_bi_t = _ssm_params_bwd(
        ar, ai, ldt, br_t, bi_t, g_ab_re, g_ab_im,
        _expand_grad(g_w_pair[..., :LANES]), _expand_grad(g_w_pair[..., LANES:]))

    dproj = jnp.concatenate([dua, dga, dub, dgb], axis=1)
    grads["w_in"] = _mm_tn("in_wgrad", hn, dproj, BF16)
    send(("w_in",))
    dhn = _mm_nt("in_dgrad", dproj, full["w_in"], [F32], tk=2048, after=[tokens["w_in"]])[0]
    grad_x, g_norm_gain = _norm1_bwd(xs, dhn, dh1, gain1)

    def b_view(a):
        return jnp.transpose(a[0], (0, 2, 1))

    views = dict(norm_gain=lambda a: a, pool_scale=lambda a: a, a_re=lambda a: a[0], a_im=lambda a: a[0],
                 log_dt=lambda a: a, b_re=b_view, b_im=b_view, c_re=lambda a: a[0], c_im=lambda a: a[0],
                 d_skip=lambda a: a, final_gain=lambda a: a.reshape(1, D_MODEL))
    small_grads = dict(
        norm_gain=g_norm_gain, pool_scale=g_pool_scale, a_re=d_ar, a_im=d_ai, log_dt=d_ldt.reshape(1, N_SSM_GROUPS),
        b_re=d_br_t, b_im=d_bi_t, c_re=_expand_grad(g_c_pair[..., :LANES]),
        c_im=-_expand_grad(g_c_pair[..., LANES:]), d_skip=g_d_skip,
        final_gain=g_final_gain)
    stacks = _all_gather("small_grads_all_gather", [small_grads[k][None] for k in SMALL], [0] * len(SMALL))
    small_out = _adamw_small([views[k](weights[k]) for k in SMALL], [views[k](mom_m[k]) for k in SMALL],
                             [views[k](mom_v[k]) for k in SMALL], stacks)
    out_g, out_d, out_m, out_v = ({} for _ in range(4))
    for out, res in zip((out_g, out_d, out_m, out_v), small_out):
        for k, r in zip(SMALL, res):
            if k in ("b_re", "b_im"):
                r = jnp.transpose(r, (0, 2, 1))
            out[k] = r.reshape(weights[k].shape)

    me = 4 * lax.axis_index("x") + 2 * lax.axis_index("y") + lax.axis_index("c")
    after = grad_x
    for names, started in sent.items():
        axes = [LARGE_AXIS[k] for k in names]
        partials, landed = _exchange_wait("grads_wait_" + names[0], started, axes, after)
        for k, axis, partial, land in zip(names, axes, partials, landed):
            shard_shape = shard2d[k].shape
            size = shard_shape[axis]
            own = lax.dynamic_slice_in_dim(partial, me * size, size, axis=axis)
            view = (-1, shard_shape[-1])
            rows = math.prod(shard_shape[:-1])
            res = _adamw("adamw_" + k, shard2d[k].reshape(view), mom_m[k][0].reshape(view), mom_v[k][0].reshape(view),
                         [own.reshape(view), land.reshape(N_PEERS, rows, shard_shape[-1])])
            out_g[k], out_d[k], out_m[k], out_v[k] = (r.reshape(weights[k].shape) for r in res)
            after = res[0]

    return (loss, grad_x.reshape(x.shape), *[out_g[k] for k in WEIGHTS], *[out_d[k] for k in WEIGHTS],
            *[out_m[k] for k in WEIGHTS], *[out_v[k] for k in WEIGHTS])
```

```python
import math

import jax
import jax.numpy as jnp
from jax import lax
from jax.experimental import pallas as pl
from jax.experimental.pallas import tpu as pltpu

F32 = jnp.float32
BF16 = jnp.bfloat16
MESH = pl.DeviceIdType.MESH
MESH_AXES = ("x", "y", "c")
N_DEV = 8

D_MODEL = 2048
POOL_WIDTH = 1024
SSM_WIDTH = 1024
N_POOL_GROUPS = 4
POOL_GROUP = 256
SSM_GROUP = 16
N_SSM_GROUPS = 64
SSM_STATE = 64
SSM_FLAT = N_SSM_GROUPS * SSM_STATE
SSM_CHUNKS = 4
CHUNK_IN = SSM_WIDTH // SSM_CHUNKS
CHUNK_STATE = SSM_FLAT // SSM_CHUNKS
PLE_DIM = 256
EPS = 1e-6
A_RE_MAX = -1e-4
ADAM_LR = 0.001
ADAM_B1 = 0.9
ADAM_B2 = 0.999
ADAM_EPS = 1e-08
ADAM_WD = 0.01
ADAM_STEP = 10
GELU_C = math.sqrt(2.0 / math.pi)
GELU_A = 0.044715

SUBLANES = 8
LANES = 128
VMEM_LIMIT_BYTES = 48 * 1024 * 1024

DOT_NN = (((1,), (0,)), ((), ()))
DOT_NT = (((1,), (1,)), ((), ()))
DOT_TN = (((0,), (0,)), ((), ()))


def _tile(n, pref):
    return pref if n % pref == 0 else n


def _params(sem):
    return pltpu.CompilerParams(dimension_semantics=sem, vmem_limit_bytes=VMEM_LIMIT_BYTES)


def _sigmoid(v):
    return 1.0 / (1.0 + jnp.exp(-v))


def _silu_and_grad(v):
    s = _sigmoid(v)
    return v * s, s * (1.0 + v * (1.0 - s))


def _mm(name, pairs, dims, grid, outs, k_steps, extras=(), epilogue=None):
    n_pairs, n_ex, n_out = len(pairs), len(extras), len(outs)
    acc_shape = tuple(d for d in outs[0][2] if d is not None)
    if epilogue is None:
        def epilogue(acc, ex, out_refs):
            out_refs[0][...] = acc.astype(out_refs[0].dtype)

    def body(*refs):
        ab = refs[:2 * n_pairs]
        ex = refs[2 * n_pairs:2 * n_pairs + n_ex]
        out_refs = refs[2 * n_pairs + n_ex:2 * n_pairs + n_ex + n_out]
        acc = refs[-1]
        k = pl.program_id(2)

        @pl.when(k == 0)
        def _():
            acc[...] = jnp.zeros_like(acc)

        part = None
        for q in range(n_pairs):
            d = lax.dot_general(ab[2 * q][...].astype(BF16), ab[2 * q + 1][...].astype(BF16), dims,
                                preferred_element_type=F32)
            part = d if part is None else part + d
        acc[...] += part

        @pl.when(k == k_steps - 1)
        def _():
            epilogue(acc[...], ex, out_refs)

    in_specs, operands = [], []
    for a, a_blk, a_map, b, b_blk, b_map in pairs:
        in_specs += [pl.BlockSpec(a_blk, a_map), pl.BlockSpec(b_blk, b_map)]
        operands += [a, b]
    for e, e_blk, e_map in extras:
        in_specs.append(pl.BlockSpec(e_blk, e_map))
        operands.append(e)
    return pl.pallas_call(
        body, name=name, grid=grid, in_specs=in_specs,
        out_specs=[pl.BlockSpec(o[2], o[3]) for o in outs],
        out_shape=[jax.ShapeDtypeStruct(o[0], o[1]) for o in outs],
        scratch_shapes=[pltpu.VMEM(acc_shape, F32)],
        compiler_params=_params(("arbitrary", "arbitrary", "arbitrary")),
    )(*operands)


def _after(tokens):
    return [(tok, tok.shape, lambda i, j, s: (0, 0)) for tok in tokens]


def _mm_nn(name, a, b, out_dtypes, tm=1024, tn=1024, tk=1024, a_col0=0, extras=(), epilogue=None, after=()):
    m, n = a.shape[0], b.shape[1]
    k = b.shape[0]
    tm, tn, tk = _tile(m, tm), _tile(n, tn), _tile(k, tk)
    outs = [((m, n), dt, (tm, tn), lambda i, j, s: (i, j)) for dt in out_dtypes]
    ex = [(e, (tm, tn), lambda i, j, s: (i, j)) for e in extras] + _after(after)
    return _mm(name, [(a, (tm, tk), lambda i, j, s: (i, a_col0 + s), b, (tk, tn), lambda i, j, s: (s, j))],
               DOT_NN, (m // tm, n // tn, k // tk), outs, k // tk, ex, epilogue)


def _mm_nt(name, a, b, out_dtypes, tm=1024, tn=1024, tk=1024, extras=(), epilogue=None, after=()):
    m, kk = a.shape
    n = b.shape[0]
    tm, tn, tk = _tile(m, tm), _tile(n, tn), _tile(kk, tk)
    outs = [((m, n), dt, (tm, tn), lambda i, j, s: (i, j)) for dt in out_dtypes]
    ex = [(e, (tm, tn), lambda i, j, s: (i, j)) for e in extras] + _after(after)
    return _mm(name, [(a, (tm, tk), lambda i, j, s: (i, s), b, (tn, tk), lambda i, j, s: (j, s))],
               DOT_NT, (m // tm, n // tn, kk // tk), outs, kk // tk, ex, epilogue)


def _mm_tn(name, a, b, out_dtype, tm=512, tn=2048, tk=1024, after=()):
    m, kk = a.shape
    n = b.shape[1]
    tm, tn, tk = _tile(kk, tm), _tile(n, tn), _tile(m, tk)
    outs = [((kk, n), out_dtype, (tm, tn), lambda i, j, s: (i, j))]
    return _mm(name, [(a, (tk, tm), lambda i, j, s: (s, i), b, (tk, tn), lambda i, j, s: (s, j))],
               DOT_TN, (kk // tm, n // tn, m // tk), outs, m // tk, _after(after))[0]


def _norm1_fwd(x, gain):
    t = x.shape[0]
    tm = _tile(t, 512)

    def body(x_ref, g_ref, hn_ref):
        xv = x_ref[...]
        r = lax.rsqrt(jnp.mean(xv * xv, axis=-1, keepdims=True) + EPS)
        hn_ref[...] = (xv * r * g_ref[...]).astype(BF16)

    return pl.pallas_call(
        body, name="norm1_fwd", grid=(t // tm,),
        in_specs=[pl.BlockSpec((tm, D_MODEL), lambda i: (i, 0)), pl.BlockSpec((1, D_MODEL), lambda i: (0, 0))],
        out_specs=pl.BlockSpec((tm, D_MODEL), lambda i: (i, 0)),
        out_shape=jax.ShapeDtypeStruct((t, D_MODEL), BF16),
        compiler_params=_params(("arbitrary",)),
    )(x, gain)


def _norm1_bwd(x, dhn, dh1, gain):
    t = x.shape[0]
    tm = _tile(t, 512)

    def body(x_ref, dhn_ref, dh1_ref, g_ref, dx_ref, gg_ref):
        @pl.when(pl.program_id(0) == 0)
        def _():
            gg_ref[...] = jnp.zeros_like(gg_ref)

        xv = x_ref[...]
        r = lax.rsqrt(jnp.mean(xv * xv, axis=-1, keepdims=True) + EPS)
        xh = xv * r
        dhn_v = dhn_ref[...]
        gg_ref[...] += jnp.sum(dhn_v * xh, axis=0, keepdims=True)
        dxh = dhn_v * g_ref[...]
        dx_ref[...] = dh1_ref[...] + r * (dxh - xh * jnp.mean(dxh * xh, axis=-1, keepdims=True))

    row = pl.BlockSpec((tm, D_MODEL), lambda i: (i, 0))
    vec = pl.BlockSpec((1, D_MODEL), lambda i: (0, 0))
    return pl.pallas_call(
        body, name="norm1_bwd", grid=(t // tm,),
        in_specs=[row, row, row, vec], out_specs=[row, vec],
        out_shape=[jax.ShapeDtypeStruct((t, D_MODEL), F32), jax.ShapeDtypeStruct((1, D_MODEL), F32)],
        compiler_params=_params(("arbitrary",)),
    )(x, dhn, dh1, gain)


def _pool_counts(t, width, group):
    row = lax.broadcasted_iota(jnp.int32, (t, width), 0)
    window = jnp.left_shift(jnp.int32(2), group)
    return row, jnp.minimum(row + 1, window).astype(F32)


def _select_window(group, s2, s4, s8, s16):
    return jnp.where(group == 0, s2, jnp.where(group == 1, s4, jnp.where(group == 2, s8, s16)))


def _pool_fwd(proj):
    t = proj.shape[0]
    tc = LANES

    def body(u_ref, o_ref):
        group = pl.program_id(0) // (POOL_GROUP // tc)
        v = u_ref[...]
        row, count = _pool_counts(t, tc, group)

        def down(a, j):
            return jnp.where(row >= j, pltpu.roll(a, j, 0), 0.0)

        s2 = v + down(v, 1)
        s4 = s2 + down(s2, 2)
        s8 = s4 + down(s4, 4)
        s16 = s8 + down(s8, 8)
        o_ref[...] = (_select_window(group, s2, s4, s8, s16) / count - v).astype(BF16)

    return pl.pallas_call(
        body, name="pool_fwd", grid=(POOL_WIDTH // tc,),
        in_specs=[pl.BlockSpec((t, tc), lambda j: (0, j))],
        out_specs=pl.BlockSpec((t, tc), lambda j: (0, j)),
        out_shape=jax.ShapeDtypeStruct((t, POOL_WIDTH), BF16),
        compiler_params=_params(("arbitrary",)),
    )(proj)


def _pool_bwd(dpooled):
    t = dpooled.shape[0]
    tc = LANES

    def body(d_ref, o_ref):
        group = pl.program_id(0) // (POOL_GROUP // tc)
        dp = d_ref[...]
        row, count = _pool_counts(t, tc, group)
        r = dp / count

        def up(a, j):
            return jnp.where(row < t - j, pltpu.roll(a, t - j, 0), 0.0)

        s2 = r + up(r, 1)
        s4 = s2 + up(s2, 2)
        s8 = s4 + up(s4, 4)
        s16 = s8 + up(s8, 8)
        o_ref[...] = (_select_window(group, s2, s4, s8, s16) - dp).astype(BF16)

    return pl.pallas_call(
        body, name="pool_bwd", grid=(POOL_WIDTH // tc,),
        in_specs=[pl.BlockSpec((t, tc), lambda j: (0, j))],
        out_specs=pl.BlockSpec((t, tc), lambda j: (0, j)),
        out_shape=jax.ShapeDtypeStruct((t, POOL_WIDTH), BF16),
        compiler_params=_params(("arbitrary",)),
    )(dpooled)


def _gate_fwd(mixed, proj, hg, pool_scale):
    t = mixed.shape[0]
    tm = _tile(t, 512)

    def body(mx_ref, ga_ref, gb_ref, hg_ref, ps_ref, cat_ref):
        silu_a, _ = _silu_and_grad(ga_ref[...])
        cat_ref[:, :POOL_WIDTH] = (mx_ref[...] * ps_ref[...] * silu_a).astype(BF16)
        silu_b, _ = _silu_and_grad(gb_ref[...])
        sb = hg_ref[:, :SSM_WIDTH] * _sigmoid(hg_ref[:, SSM_WIDTH:])
        cat_ref[:, POOL_WIDTH:] = (sb * silu_b).astype(BF16)

    return pl.pallas_call(
        body, name="gate_fwd", grid=(t // tm,),
        in_specs=[pl.BlockSpec((tm, POOL_WIDTH), lambda i: (i, 0)),
                  pl.BlockSpec((tm, POOL_WIDTH), lambda i: (i, 1)),
                  pl.BlockSpec((tm, SSM_WIDTH), lambda i: (i, 3)),
                  pl.BlockSpec((tm, 2 * SSM_WIDTH), lambda i: (i, 0)),
                  pl.BlockSpec((1, POOL_WIDTH), lambda i: (0, 0))],
        out_specs=pl.BlockSpec((tm, D_MODEL), lambda i: (i, 0)),
        out_shape=jax.ShapeDtypeStruct((t, D_MODEL), BF16),
        compiler_params=_params(("arbitrary",)),
    )(mixed, proj, proj, hg, pool_scale)


def _gate_bwd(dcat, mixed, proj, hg, pool_scale):
    t = mixed.shape[0]
    tm = _tile(t, 512)

    def body(dc_ref, mx_ref, ga_ref, gb_ref, hg_ref, ps_ref, dmx_ref, dga_ref, dgb_ref, dhg_ref, gps_ref):
        @pl.when(pl.program_id(0) == 0)
        def _():
            gps_ref[...] = jnp.zeros_like(gps_ref)

        ps = ps_ref[...]
        mx = mx_ref[...]
        dya = dc_ref[:, :POOL_WIDTH]
        silu_a, dsilu_a = _silu_and_grad(ga_ref[...])
        dpa = dya * silu_a
        gps_ref[...] += jnp.sum(dpa * mx, axis=0, keepdims=True)
        dmx_ref[...] = (dpa * ps).astype(BF16)
        dga_ref[...] = (dya * mx * ps * dsilu_a).astype(BF16)

        dyb = dc_ref[:, POOL_WIDTH:]
        silu_b, dsilu_b = _silu_and_grad(gb_ref[...])
        h_a = hg_ref[:, :SSM_WIDTH]
        sg = _sigmoid(hg_ref[:, SSM_WIDTH:])
        dsb = dyb * silu_b
        dgb_ref[...] = (dyb * h_a * sg * dsilu_b).astype(BF16)
        dhg_ref[:, :SSM_WIDTH] = (dsb * sg).astype(BF16)
        dhg_ref[:, SSM_WIDTH:] = (dsb * h_a * sg * (1.0 - sg)).astype(BF16)

    half = pl.BlockSpec((tm, POOL_WIDTH), lambda i: (i, 0))
    full = pl.BlockSpec((tm, D_MODEL), lambda i: (i, 0))
    vec = pl.BlockSpec((1, POOL_WIDTH), lambda i: (0, 0))
    return pl.pallas_call(
        body, name="gate_bwd", grid=(t // tm,),
        in_specs=[full, half,
                  pl.BlockSpec((tm, POOL_WIDTH), lambda i: (i, 1)),
                  pl.BlockSpec((tm, SSM_WIDTH), lambda i: (i, 3)),
                  full, vec],
        out_specs=[half, half, half, full, vec],
        out_shape=[jax.ShapeDtypeStruct((t, POOL_WIDTH), BF16), jax.ShapeDtypeStruct((t, POOL_WIDTH), BF16),
                   jax.ShapeDtypeStruct((t, SSM_WIDTH), BF16), jax.ShapeDtypeStruct((t, 2 * SSM_WIDTH), BF16),
                   jax.ShapeDtypeStruct((1, POOL_WIDTH), F32)],
        compiler_params=_params(("arbitrary",)),
    )(dcat, mixed, proj, proj, hg, pool_scale)


def _final(h1, e, q, target, gain):
    t = h1.shape[0]
    tm = _tile(t, 256)

    def body(h1_ref, e_ref, q_ref, tg_ref, g_ref, de_ref, dq_ref, dh2_ref, gg_ref, loss_ref):
        @pl.when(pl.program_id(0) == 0)
        def _():
            gg_ref[...] = jnp.zeros_like(gg_ref)
            loss_ref[...] = jnp.zeros_like(loss_ref)

        ev = e_ref[...]
        sg = _sigmoid(q_ref[...])
        h2 = h1_ref[...] + ev * sg
        r = lax.rsqrt(jnp.mean(h2 * h2, axis=-1, keepdims=True) + EPS)
        n = h2 * r
        gain_v = g_ref[...]
        diff = n * gain_v - tg_ref[...]
        row_loss = jnp.sum(diff * diff, axis=-1, keepdims=True)
        loss_ref[...] += (0.5 / D_MODEL) * jnp.sum(row_loss, axis=0, keepdims=True)
        dout = diff * (1.0 / D_MODEL)
        gg_ref[...] += jnp.sum(dout * n, axis=0, keepdims=True)
        dn = dout * gain_v
        dh2 = r * (dn - n * jnp.mean(dn * n, axis=-1, keepdims=True))
        dh2_ref[...] = dh2
        de_ref[...] = (dh2 * sg).astype(BF16)
        dq_ref[...] = (dh2 * ev * sg * (1.0 - sg)).astype(BF16)

    row = pl.BlockSpec((tm, D_MODEL), lambda i: (i, 0))
    vec = pl.BlockSpec((1, D_MODEL), lambda i: (0, 0))
    return pl.pallas_call(
        body, name="final_norm_loss", grid=(t // tm,),
        in_specs=[row, row, row, row, vec],
        out_specs=[row, row, row, vec, pl.BlockSpec((1, 1), lambda i: (0, 0))],
        out_shape=[jax.ShapeDtypeStruct((t, D_MODEL), BF16), jax.ShapeDtypeStruct((t, D_MODEL), BF16),
                   jax.ShapeDtypeStruct((t, D_MODEL), F32), jax.ShapeDtypeStruct((1, D_MODEL), F32),
                   jax.ShapeDtypeStruct((1, 1), F32)],
        compiler_params=_params(("arbitrary",)),
    )(h1, e, q, target, gain)


def _zoh(a_re, a_im, log_dt, b_re_t, b_im_t):
    lam_re = jnp.minimum(a_re, A_RE_MAX)
    lam_im = a_im
    dt = jnp.exp(log_dt)
    mag = jnp.exp(lam_re * dt)
    ang = lam_im * dt
    ab_re = mag * jnp.cos(ang)
    ab_im = mag * jnp.sin(ang)
    den = lam_re * lam_re + lam_im * lam_im
    n_re = ab_re - 1.0
    n_im = ab_im
    q_re = (n_re * lam_re + n_im * lam_im) / den
    q_im = (n_im * lam_re - n_re * lam_im) / den
    bb_re = q_re[:, None, :] * b_re_t - q_im[:, None, :] * b_im_t
    bb_im = q_re[:, None, :] * b_im_t + q_im[:, None, :] * b_re_t
    return ab_re, ab_im, bb_re, bb_im


def _ssm_params(a_re, a_im, log_dt, b_re_t, b_im_t):
    def body(are_ref, aim_ref, dt_ref, bre_ref, bim_ref, abre_ref, abim_ref, bbre_ref, bbim_ref):
        ab_re, ab_im, bb_re, bb_im = _zoh(are_ref[...], aim_ref[...], dt_ref[...], bre_ref[...], bim_ref[...])
        abre_ref[...] = ab_re
        abim_ref[...] = ab_im
        bbre_ref[...] = bb_re
        bbim_ref[...] = bb_im

    return pl.pallas_call(
        body, name="ssm_params",
        out_shape=[jax.ShapeDtypeStruct(a_re.shape, F32), jax.ShapeDtypeStruct(a_re.shape, F32),
                   jax.ShapeDtypeStruct(b_re_t.shape, F32), jax.ShapeDtypeStruct(b_re_t.shape, F32)],
        compiler_params=_params(None),
    )(a_re, a_im, log_dt, b_re_t, b_im_t)


def _ssm_params_bwd(a_re, a_im, log_dt, b_re_t, b_im_t, g_ab_re, g_ab_im, g_bb_re, g_bb_im):
    def body(are_ref, aim_ref, dt_ref, bre_ref, bim_ref, gar_ref, gai_ref, gbr_ref, gbi_ref,
             o_are, o_aim, o_dt, o_bre, o_bim):
        _, vjp = jax.vjp(_zoh, are_ref[...], aim_ref[...], dt_ref[...], bre_ref[...], bim_ref[...])
        d_are, d_aim, d_dt, d_bre, d_bim = vjp((gar_ref[...], gai_ref[...], gbr_ref[...], gbi_ref[...]))
        o_are[...] = d_are
        o_aim[...] = d_aim
        o_dt[...] = d_dt
        o_bre[...] = d_bre
        o_bim[...] = d_bim

    ins = (a_re, a_im, log_dt, b_re_t, b_im_t)
    return pl.pallas_call(
        body, name="ssm_params_bwd",
        out_shape=[jax.ShapeDtypeStruct(v.shape, F32) for v in ins],
        compiler_params=_params(None),
    )(*ins, g_ab_re, g_ab_im, g_bb_re, g_bb_im)


CHUNK_TILES = CHUNK_STATE // LANES
CH_PER_TILE = CHUNK_IN // CHUNK_TILES
PAIR = 2 * LANES
SSM_ROWS = 256
SCAN_STEPS = 8
U_COLUMN_BLOCK = 2 * POOL_WIDTH // SSM_WIDTH


def _own_half():
    r = lax.broadcasted_iota(jnp.int32, (CHUNK_IN, LANES), 0) // SSM_GROUP % 2
    c = lax.broadcasted_iota(jnp.int32, (CHUNK_IN, LANES), 1) // SSM_STATE
    return (r == c)[None]


def _compact_weight(w):
    tiled = jnp.tile(w.reshape(SSM_CHUNKS, CHUNK_IN, SSM_STATE), (1, 1, 2))
    return jnp.where(_own_half(), tiled, 0.0)


def _compact_pair(w_a, w_b):
    return jnp.concatenate([_compact_weight(w_a), _compact_weight(w_b)], axis=-1).astype(BF16)


def _expand_grad(g):
    kept = jnp.where(_own_half(), g, 0.0)
    return kept.reshape(SSM_CHUNKS, CHUNK_IN, 2, SSM_STATE).sum(axis=2).reshape(N_SSM_GROUPS, SSM_GROUP, SSM_STATE)


TILES_PER_BLOCK = LANES // CH_PER_TILE
IN_BLOCKS = CHUNK_IN // LANES


def _tile_masks():
    j = lax.broadcasted_iota(jnp.int32, (CHUNK_TILES, LANES), 0) % TILES_PER_BLOCK
    lane = lax.broadcasted_iota(jnp.int32, (CHUNK_TILES, LANES), 1) // CH_PER_TILE
    return (j == lane).astype(F32)


def _tile_rows(ref, j, tt):
    return ref.at[j // TILES_PER_BLOCK, pl.ds(j, tt, stride=CHUNK_TILES), :]


def _spread(ref, v, masks):
    tt = v.shape[0]
    for j in range(CHUNK_TILES):
        block = LANES * (j // TILES_PER_BLOCK)
        _tile_rows(ref, j, tt)[...] = v[:, block:block + LANES] * masks[j:j + 1, :]
    return jnp.concatenate([ref[b] for b in range(IN_BLOCKS)], axis=1).astype(BF16)


def _gather(ref, full, masks):
    tt = full.shape[0] // CHUNK_TILES
    for b in range(IN_BLOCKS):
        ref[b] = full[:, b * LANES:(b + 1) * LANES]
    out = []
    for b in range(IN_BLOCKS):
        acc = None
        for j in range(b * TILES_PER_BLOCK, (b + 1) * TILES_PER_BLOCK):
            part = _tile_rows(ref, j, tt)[...] * masks[j:j + 1, :]
            acc = part if acc is None else acc + part
        out.append(acc)
    return jnp.concatenate(out, axis=1)


def _resident(shape):
    return pl.BlockSpec(shape, lambda i: (0,) * len(shape), pipeline_mode=pl.Buffered(1))


def _halves(ref, k, rows=slice(None)):
    return ref[k, rows, :LANES], ref[k, rows, LANES:]


def _ssm_fwd(proj, w2, c2, a2, d_skip):
    t = proj.shape[0]
    tt = _tile(t, SSM_ROWS)
    rows = tt * CHUNK_TILES

    def body(u_ref, w_ref, c_ref, a_ref, d_ref, y_ref, gel_ref, s_ref, carry, spread_ref, full_ref):
        @pl.when(pl.program_id(0) == 0)
        def _():
            carry[...] = jnp.zeros_like(carry)
            spread_ref[...] = jnp.zeros_like(spread_ref)

        mask = _tile_masks()
        u = u_ref[...]
        for k in range(SSM_CHUNKS):
            uk = _spread(spread_ref, u[:, k * CHUNK_IN:(k + 1) * CHUNK_IN], mask)
            s_ref[k] = jnp.dot(uk, w_ref[k], preferred_element_type=F32)

        abar = [_halves(a_ref, k) for k in range(SSM_CHUNKS)]

        def steps(i, state):
            for v in range(SCAN_STEPS):
                r = pl.ds(pl.multiple_of((i * SCAN_STEPS + v) * CHUNK_TILES, CHUNK_TILES), CHUNK_TILES)
                new = []
                for k, ((a_re, a_im), (s_re, s_im)) in enumerate(zip(abar, state)):
                    b_re, b_im = _halves(s_ref, k, r)
                    s_re, s_im = a_re * s_re - a_im * s_im + b_re, a_re * s_im + a_im * s_re + b_im
                    s_ref[k, r, :LANES] = s_re
                    s_ref[k, r, LANES:] = s_im
                    new.append((s_re, s_im))
                state = tuple(new)
            return state

        state = lax.fori_loop(0, tt // SCAN_STEPS, steps, tuple(_halves(carry, k) for k in range(SSM_CHUNKS)))
        for k, (s_re, s_im) in enumerate(state):
            carry[k, :, :LANES] = s_re
            carry[k, :, LANES:] = s_im

        for k in range(SSM_CHUNKS):
            cols = slice(k * CHUNK_IN, (k + 1) * CHUNK_IN)
            full = lax.dot_general(s_ref[k].astype(BF16), c_ref[k], DOT_NT, preferred_element_type=F32)
            y = _gather(full_ref, full, mask) + d_ref[:, cols] * u[:, cols]
            y_ref[:, cols] = y
            gel_ref[:, cols] = (0.5 * y * (1.0 + jnp.tanh(GELU_C * (y + GELU_A * y * y * y)))).astype(BF16)

    weight = _resident((SSM_CHUNKS, CHUNK_IN, PAIR))
    tokens = pl.BlockSpec((tt, SSM_WIDTH), lambda i: (i, 0))
    return pl.pallas_call(
        body, name="ssm_fwd", grid=(t // tt,),
        in_specs=[pl.BlockSpec((tt, SSM_WIDTH), lambda i: (i, U_COLUMN_BLOCK)), weight, weight,
                  _resident((SSM_CHUNKS, CHUNK_TILES, PAIR)), _resident((1, SSM_WIDTH))],
        out_specs=[tokens, tokens, pl.BlockSpec((SSM_CHUNKS, rows, PAIR), lambda i: (0, i, 0))],
        out_shape=[jax.ShapeDtypeStruct((t, SSM_WIDTH), F32), jax.ShapeDtypeStruct((t, SSM_WIDTH), BF16),
                   jax.ShapeDtypeStruct((SSM_CHUNKS, t * CHUNK_TILES, PAIR), F32)],
        scratch_shapes=[pltpu.VMEM((SSM_CHUNKS, CHUNK_TILES, PAIR), F32), pltpu.VMEM((IN_BLOCKS, rows, LANES), F32),
                        pltpu.VMEM((IN_BLOCKS, rows, LANES), F32)],
        compiler_params=_params(("arbitrary",)),
    )(proj, w2, c2, a2, d_skip)


def _ssm_bwd(dy, proj, s, w2, c2, a2, d_skip):
    t = dy.shape[0]
    tt = _tile(t, SSM_ROWS)
    rows = tt * CHUNK_TILES
    n_chunks = t // tt

    def body(dy_ref, u_ref, s_ref, w_ref, c_ref, a_ref, d_ref, du_ref, gc_ref, gw_ref, ga_ref, gd_ref, z_ref, carry,
             spread_ref, full_ref):
        @pl.when(pl.program_id(0) == 0)
        def _():
            for r in (carry, gc_ref, gw_ref, ga_ref, gd_ref, spread_ref):
                r[...] = jnp.zeros_like(r)

        mask = _tile_masks()
        dy_v = dy_ref[...]
        u = u_ref[...]
        gd_ref[...] += jnp.sum(dy_v * u, axis=0, keepdims=True)
        for k in range(SSM_CHUNKS):
            dk = _spread(spread_ref, dy_v[:, k * CHUNK_IN:(k + 1) * CHUNK_IN], mask)
            z_ref[k] = jnp.dot(dk, c_ref[k], preferred_element_type=F32)
            gc_ref[k] += lax.dot_general(dk, s_ref[k].astype(BF16), DOT_TN, preferred_element_type=F32)

        abar = [_halves(a_ref, k) for k in range(SSM_CHUNKS)]

        def steps(i, state):
            zs, gs = state
            for v in range(SCAN_STEPS):
                tok = tt - 1 - (i * SCAN_STEPS + v)
                r = pl.ds(pl.multiple_of(tok * CHUNK_TILES, CHUNK_TILES), CHUNK_TILES)
                new_z, new_g = [], []
                for k, ((a_re, a_im), (z_re, z_im), (g_re, g_im)) in enumerate(zip(abar, zs, gs)):
                    s_re, s_im = _halves(s_ref, k, r)
                    g_re = g_re + z_re * s_re + z_im * s_im
                    g_im = g_im + z_im * s_re - z_re * s_im
                    d_re, d_im = _halves(z_ref, k, r)
                    z_re, z_im = d_re + a_re * z_re + a_im * z_im, d_im + a_re * z_im - a_im * z_re
                    z_ref[k, r, :LANES] = z_re
                    z_ref[k, r, LANES:] = z_im
                    new_z.append((z_re, z_im))
                    new_g.append((g_re, g_im))
                zs, gs = tuple(new_z), tuple(new_g)
            return zs, gs

        zs, gs = lax.fori_loop(0, tt // SCAN_STEPS, steps,
                               (tuple(_halves(carry, k) for k in range(SSM_CHUNKS)),
                                tuple(_halves(ga_ref, k) for k in range(SSM_CHUNKS))))
        for k in range(SSM_CHUNKS):
            carry[k, :, :LANES], carry[k, :, LANES:] = zs[k]
            ga_ref[k, :, :LANES], ga_ref[k, :, LANES:] = gs[k]

        for k in range(SSM_CHUNKS):
            cols = slice(k * CHUNK_IN, (k + 1) * CHUNK_IN)
            zb = z_ref[k].astype(BF16)
            full = lax.dot_general(zb, w_ref[k], DOT_NT, preferred_element_type=F32)
            du_ref[:, cols] = (_gather(full_ref, full, mask) + d_ref[:, cols] * dy_v[:, cols]).astype(BF16)
            uk = _spread(spread_ref, u[:, cols], mask)
            gw_ref[k] += lax.dot_general(uk, zb, DOT_TN, preferred_element_type=F32)

    weight = _resident((SSM_CHUNKS, CHUNK_IN, PAIR))
    tokens = pl.BlockSpec((tt, SSM_WIDTH), lambda i: (n_chunks - 1 - i, 0))
    grad = pl.BlockSpec((SSM_CHUNKS, CHUNK_IN, PAIR), lambda i: (0, 0, 0))
    return pl.pallas_call(
        body, name="ssm_bwd", grid=(n_chunks,),
        in_specs=[tokens, pl.BlockSpec((tt, SSM_WIDTH), lambda i: (n_chunks - 1 - i, U_COLUMN_BLOCK)),
                  pl.BlockSpec((SSM_CHUNKS, rows, PAIR), lambda i: (0, n_chunks - 1 - i, 0)), weight, weight,
                  _resident((SSM_CHUNKS, CHUNK_TILES, PAIR)), _resident((1, SSM_WIDTH))],
        out_specs=[tokens, grad, grad, pl.BlockSpec((SSM_CHUNKS, CHUNK_TILES, PAIR), lambda i: (0, 0, 0)),
                   pl.BlockSpec((1, SSM_WIDTH), lambda i: (0, 0))],
        out_shape=[jax.ShapeDtypeStruct((t, SSM_WIDTH), BF16), jax.ShapeDtypeStruct((SSM_CHUNKS, CHUNK_IN, PAIR), F32),
                   jax.ShapeDtypeStruct((SSM_CHUNKS, CHUNK_IN, PAIR), F32),
                   jax.ShapeDtypeStruct((SSM_CHUNKS, CHUNK_TILES, PAIR), F32), jax.ShapeDtypeStruct((1, SSM_WIDTH), F32)],
        scratch_shapes=[pltpu.VMEM((SSM_CHUNKS, rows, PAIR), F32), pltpu.VMEM((SSM_CHUNKS, CHUNK_TILES, PAIR), F32),
                        pltpu.VMEM((IN_BLOCKS, rows, LANES), F32), pltpu.VMEM((IN_BLOCKS, rows, LANES), F32)],
        compiler_params=_params(("arbitrary",)),
    )(dy, proj, s, w2, c2, a2, d_skip)


def _block(ref, axis, size, index):
    idx = [slice(None)] * len(ref.shape)
    idx[axis] = pl.ds(pl.multiple_of(index * size, size), size)
    return ref.at[tuple(idx)]


def _all_gather(name, shards, axes):
    n = len(shards)
    sizes = [s.shape[a] for s, a in zip(shards, axes)]

    def body(*refs):
        ins, outs = refs[:n], refs[n:2 * n]
        send_sems, recv_sems, local_sems = refs[2 * n:]
        x, y, c = (lax.axis_index(a) for a in MESH_AXES)
        me, sibling = (x, y, c), (x, y, 1 - c)
        chips = [(1 - x, y), (x, 1 - y), (1 - x, 1 - y)]

        def rows(i, dev):
            return _block(outs[i], axes[i], sizes[i], 4 * dev[0] + 2 * dev[1] + dev[2])

        def copy(i, k, block, to, src=None):
            return pltpu.make_async_remote_copy(
                src_ref=rows(i, block) if src is None else src, dst_ref=rows(i, block),
                send_sem=send_sems.at[7 * i + k], recv_sem=recv_sems.at[7 * i + k],
                device_id=to, device_id_type=MESH)

        mine = [pltpu.make_async_copy(ins[i], rows(i, me), local_sems.at[i]) for i in range(n)]
        for cp in mine:
            cp.start()
        first = []
        for i in range(n):
            first.append(copy(i, 0, me, sibling, src=ins[i]))
            first += [copy(i, 1 + j, me, (*chip, c), src=ins[i]) for j, chip in enumerate(chips)]
        for cp in first:
            cp.start()
        passed = []
        for i in range(n):
            for j, chip in enumerate(chips):
                copy(i, 1 + j, (*chip, c), me).wait_recv()
                fwd = copy(i, 4 + j, (*chip, c), sibling)
                fwd.start()
                passed.append(fwd)
        for i in range(n):
            copy(i, 0, sibling, me).wait_recv()
            for j, chip in enumerate(chips):
                copy(i, 4 + j, (*chip, 1 - c), me).wait_recv()
        for cp in first + passed:
            cp.wait_send()
        for cp in mine:
            cp.wait()

    out_shape = []
    for s, a in zip(shards, axes):
        shape = list(s.shape)
        shape[a] *= N_DEV
        out_shape.append(jax.ShapeDtypeStruct(tuple(shape), s.dtype))
    any_spec = pl.BlockSpec(memory_space=pl.ANY)
    return pl.pallas_call(
        body, name=name, out_shape=out_shape,
        in_specs=[any_spec] * n, out_specs=[any_spec] * n,
        scratch_shapes=[pltpu.SemaphoreType.DMA((7 * n,)), pltpu.SemaphoreType.DMA((7 * n,)),
                        pltpu.SemaphoreType.DMA((n,))],
    )(*shards)


HBM_SPEC = pl.BlockSpec(memory_space=pltpu.HBM)
SEM_SPEC = pl.BlockSpec(memory_space=pltpu.SEMAPHORE)
ANY_SPEC = pl.BlockSpec(memory_space=pl.ANY)
SPLIT_PARAMS = pltpu.CompilerParams(has_side_effects=pltpu.SideEffectType.DATAFLOW_SIDE_EFFECTING)
N_PEERS = N_DEV - 1
TOKEN = jax.ShapeDtypeStruct((SUBLANES, LANES), F32)
VMEM_SPEC = pl.BlockSpec(memory_space=pltpu.VMEM)


def _in_hbm(arrays):
    return [pltpu.with_memory_space_constraint(a, pltpu.HBM) for a in arrays]


def _peer(m):
    x, y, c = (lax.axis_index(a) for a in MESH_AXES)
    px = 1 - x if m & 4 else x
    py = 1 - y if m & 2 else y
    pc = 1 - c if m & 1 else c
    return (px, py, pc), 4 * px + 2 * py + pc


def _my_index():
    x, y, c = (lax.axis_index(a) for a in MESH_AXES)
    return 4 * x + 2 * y + c


def _gather_copies(shard_refs, full_refs, axes, send_sems, recv_sems):
    copies = []
    for i, (shard, full) in enumerate(zip(shard_refs, full_refs)):
        mine = _block(full, axes[i], shard.shape[axes[i]], _my_index())
        for m in range(1, N_DEV):
            peer, _ = _peer(m)
            copies.append(pltpu.make_async_remote_copy(
                src_ref=shard, dst_ref=mine, send_sem=send_sems.at[N_PEERS * i + m - 1],
                recv_sem=recv_sems.at[N_PEERS * i + m - 1], device_id=peer, device_id_type=MESH))
    return copies


def _gather_start(name, shards, axes, after):
    n = len(shards)

    def body(*refs):
        shard_refs = refs[:n]
        send_sems, recv_sems, local_sems = refs[n + 1:n + 4]
        full_refs = refs[2 * n + 4:3 * n + 4]
        refs[3 * n + 4][...] = jnp.zeros(TOKEN.shape, TOKEN.dtype)
        for i in range(n):
            pltpu.make_async_copy(shard_refs[i], _block(full_refs[i], axes[i], shard_refs[i].shape[axes[i]], _my_index()),
                                  local_sems.at[i]).start()
        for cp in _gather_copies(shard_refs, full_refs, axes, send_sems, recv_sems):
            cp.start()

    fulls = []
    for s, a in zip(shards, axes):
        shape = list(s.shape)
        shape[a] *= N_DEV
        fulls.append(pltpu.HBM(tuple(shape), s.dtype))
    out = pl.pallas_call(
        body, name=name,
        out_shape=(pltpu.SemaphoreType.DMA((N_PEERS * n,)), pltpu.SemaphoreType.DMA((N_PEERS * n,)),
                   pltpu.SemaphoreType.DMA((n,)), *[pltpu.HBM(s.shape, s.dtype) for s in shards], *fulls, TOKEN),
        in_specs=[HBM_SPEC] * n + [ANY_SPEC],
        out_specs=(SEM_SPEC, SEM_SPEC, SEM_SPEC, *[HBM_SPEC] * (2 * n), VMEM_SPEC),
        input_output_aliases={i: 3 + i for i in range(n)},
        compiler_params=SPLIT_PARAMS,
    )(*_in_hbm(shards), after)
    return out[:-1], out[-1]


def _gather_wait(name, started, indices, axes, after):
    send_sems, recv_sems, local_sems = started[:3]
    n_all = (len(started) - 3) // 2
    shards = [started[3 + i] for i in indices]
    fulls = [started[3 + n_all + i] for i in indices]
    n = len(indices)

    def body(*refs):
        shard_refs, full_refs = refs[:n], refs[n:2 * n]
        send_sems, recv_sems, local_sems = refs[2 * n:2 * n + 3]
        for j, i in enumerate(indices):
            mine = _block(full_refs[j], axes[j], shard_refs[j].shape[axes[j]], _my_index())
            pltpu.make_async_copy(shard_refs[j], mine, local_sems.at[i]).wait()
            for m in range(1, N_DEV):
                peer, _ = _peer(m)
                cp = pltpu.make_async_remote_copy(
                    src_ref=shard_refs[j], dst_ref=mine, send_sem=send_sems.at[N_PEERS * i + m - 1],
                    recv_sem=recv_sems.at[N_PEERS * i + m - 1], device_id=peer, device_id_type=MESH)
                cp.wait_send()
                cp.wait_recv()

    out = pl.pallas_call(
        body, name=name,
        out_shape=tuple(pltpu.HBM(a.shape, a.dtype) for a in shards + fulls),
        in_specs=[HBM_SPEC] * (2 * n) + [SEM_SPEC] * 3 + [ANY_SPEC], out_specs=tuple([HBM_SPEC] * (2 * n)),
        input_output_aliases={i: i for i in range(2 * n)},
        compiler_params=SPLIT_PARAMS,
    )(*shards, *fulls, send_sems, recv_sems, local_sems, after)
    return out[n:]


def _exchange_start(name, fulls, axes):
    n = len(fulls)
    sizes = [f.shape[a] // N_DEV for f, a in zip(fulls, axes)]

    def body(*refs):
        ins = refs[:n]
        send_sems, recv_sems = refs[n:n + 2]
        lands = refs[2 * n + 2:3 * n + 2]
        refs[3 * n + 2][...] = jnp.zeros(TOKEN.shape, TOKEN.dtype)
        for i in range(n):
            for m in range(1, N_DEV):
                peer, index = _peer(m)
                pltpu.make_async_remote_copy(
                    src_ref=_block(ins[i], axes[i], sizes[i], index), dst_ref=lands[i].at[m - 1],
                    send_sem=send_sems.at[N_PEERS * i + m - 1], recv_sem=recv_sems.at[N_PEERS * i + m - 1],
                    device_id=peer, device_id_type=MESH).start()

    lands = []
    for f, a, size in zip(fulls, axes, sizes):
        shape = list(f.shape)
        shape[a] = size
        lands.append(pltpu.HBM((N_PEERS, *shape), f.dtype))
    out = pl.pallas_call(
        body, name=name,
        out_shape=(pltpu.SemaphoreType.DMA((N_PEERS * n,)), pltpu.SemaphoreType.DMA((N_PEERS * n,)),
                   *[pltpu.HBM(f.shape, f.dtype) for f in fulls], *lands, TOKEN),
        in_specs=[HBM_SPEC] * n, out_specs=(SEM_SPEC, SEM_SPEC, *[HBM_SPEC] * (2 * n), VMEM_SPEC),
        input_output_aliases={i: 2 + i for i in range(n)},
        compiler_params=SPLIT_PARAMS,
    )(*_in_hbm(fulls))
    return out[:-1], out[-1]


def _exchange_wait(name, started, axes, after):
    send_sems, recv_sems = started[:2]
    n = (len(started) - 2) // 2
    fulls, lands = list(started[2:2 + n]), list(started[2 + n:])
    sizes = [f.shape[a] // N_DEV for f, a in zip(fulls, axes)]

    def body(*refs):
        ins, land_refs = refs[:n], refs[n:2 * n]
        send_sems, recv_sems = refs[2 * n:2 * n + 2]
        for i in range(n):
            for m in range(1, N_DEV):
                peer, index = _peer(m)
                cp = pltpu.make_async_remote_copy(
                    src_ref=_block(ins[i], axes[i], sizes[i], index), dst_ref=land_refs[i].at[m - 1],
                    send_sem=send_sems.at[N_PEERS * i + m - 1], recv_sem=recv_sems.at[N_PEERS * i + m - 1],
                    device_id=peer, device_id_type=MESH)
                cp.wait_send()
                cp.wait_recv()

    out = pl.pallas_call(
        body, name=name,
        out_shape=tuple(pltpu.HBM(a.shape, a.dtype) for a in fulls + lands),
        in_specs=[HBM_SPEC] * (2 * n) + [SEM_SPEC] * 2 + [ANY_SPEC], out_specs=tuple([HBM_SPEC] * (2 * n)),
        input_output_aliases={i: i for i in range(2 * n)},
        compiler_params=SPLIT_PARAMS,
    )(*fulls, *lands, send_sems, recv_sems, after)
    return out[:n], out[n:]


def _adamw_update(w_ref, m_ref, v_ref, part_refs, g_ref, d_ref, nm_ref, nv_ref):
    c1 = 1.0 - ADAM_B1 ** ADAM_STEP
    c2 = 1.0 - ADAM_B2 ** ADAM_STEP
    g = None
    for p_ref in part_refs:
        stacked = len(p_ref.shape) > len(w_ref.shape)
        terms = [p_ref[s] for s in range(p_ref.shape[0])] if stacked else [p_ref[...]]
        for term in terms:
            term = term.astype(F32)
            g = term if g is None else g + term
    new_m = ADAM_B1 * m_ref[...] + (1.0 - ADAM_B1) * g
    new_v = ADAM_B2 * v_ref[...] + (1.0 - ADAM_B2) * (g * g)
    g_ref[...] = g
    nm_ref[...] = new_m
    nv_ref[...] = new_v
    d_ref[...] = -ADAM_LR * ((new_m / c1) / (jnp.sqrt(new_v / c2) + ADAM_EPS) + ADAM_WD * w_ref[...])


def _adamw_small(ws, ms, vs, stacks, loss_stack):
    n = len(ws)

    def body(*refs):
        ins, outs = refs[:4 * n + 1], refs[4 * n + 1:]
        for i in range(n):
            _adamw_update(ins[i], ins[n + i], ins[2 * n + i], [ins[3 * n + i]],
                          outs[i], outs[n + i], outs[2 * n + i], outs[3 * n + i])
        total = ins[4 * n][0]
        for dev in range(1, N_DEV):
            total = total + ins[4 * n][dev]
        outs[4 * n][...] = total

    res = pl.pallas_call(
        body, name="adamw_small",
        out_shape=[jax.ShapeDtypeStruct(w.shape, F32) for w in ws] * 4 + [jax.ShapeDtypeStruct((1, LANES), F32)],
        compiler_params=_params(None),
    )(*ws, *ms, *vs, *stacks, loss_stack)
    return res[:n], res[n:2 * n], res[2 * n:3 * n], res[3 * n:4 * n], res[4 * n]


def _adamw(name, w, m, v, parts):
    r, c = w.shape
    tr = _tile(r, 256)
    n_parts = len(parts)

    def body(*refs):
        _adamw_update(refs[0], refs[1], refs[2], refs[3:3 + n_parts], *refs[3 + n_parts:])

    row = pl.BlockSpec((tr, c), lambda i: (i, 0))
    in_specs = [row, row, row]
    for p in parts:
        in_specs.append(row if p.ndim == 2 else pl.BlockSpec((p.shape[0], tr, c), lambda i: (0, i, 0)))
    return pl.pallas_call(
        body, name=name, grid=(r // tr,), in_specs=in_specs, out_specs=[row] * 4,
        out_shape=[jax.ShapeDtypeStruct((r, c), F32)] * 4,
        compiler_params=_params(("arbitrary",)),
    )(w, m, v, *parts)


SMALL = ("norm_gain", "pool_scale", "a_re", "a_im", "log_dt", "b_re", "b_im", "c_re", "c_im", "d_skip", "final_gain")
LARGE = ("w_in", "w_pool", "w_glu", "w_out", "w_ple", "w_ple_gate")
LARGE_AXIS = {"w_in": 1, "w_pool": 1, "w_glu": 1, "w_out": 0, "w_ple": 1, "w_ple_gate": 0}
WEIGHTS = ("norm_gain", "w_in", "w_pool", "pool_scale", "a_re", "a_im", "log_dt", "b_re", "b_im", "c_re", "c_im",
           "d_skip", "w_glu", "w_out", "w_ple", "w_ple_gate", "final_gain")


def kernel(x, p, norm_gain, w_in, w_pool, pool_scale, a_re, a_im, log_dt, b_re, b_im, c_re, c_im, d_skip, w_glu, w_out, w_ple, w_ple_gate, final_gain, loss_target, m_norm_gain, m_w_in, m_w_pool, m_pool_scale, m_a_re, m_a_im, m_log_dt, m_b_re, m_b_im, m_c_re, m_c_im, m_d_skip, m_w_glu, m_w_out, m_w_ple, m_w_ple_gate, m_final_gain, v_norm_gain, v_w_in, v_w_pool, v_pool_scale, v_a_re, v_a_im, v_log_dt, v_b_re, v_b_im, v_c_re, v_c_im, v_d_skip, v_w_glu, v_w_out, v_w_ple, v_w_ple_gate, v_final_gain):
    weights = dict(norm_gain=norm_gain, w_in=w_in, w_pool=w_pool, pool_scale=pool_scale, a_re=a_re, a_im=a_im,
                   log_dt=log_dt, b_re=b_re, b_im=b_im, c_re=c_re, c_im=c_im, d_skip=d_skip, w_glu=w_glu,
                   w_out=w_out, w_ple=w_ple, w_ple_gate=w_ple_gate, final_gain=final_gain)
    mom_m = dict(norm_gain=m_norm_gain, w_in=m_w_in, w_pool=m_w_pool, pool_scale=m_pool_scale, a_re=m_a_re,
                 a_im=m_a_im, log_dt=m_log_dt, b_re=m_b_re, b_im=m_b_im, c_re=m_c_re, c_im=m_c_im,
                 d_skip=m_d_skip, w_glu=m_w_glu, w_out=m_w_out, w_ple=m_w_ple, w_ple_gate=m_w_ple_gate,
                 final_gain=m_final_gain)
    mom_v = dict(norm_gain=v_norm_gain, w_in=v_w_in, w_pool=v_w_pool, pool_scale=v_pool_scale, a_re=v_a_re,
                 a_im=v_a_im, log_dt=v_log_dt, b_re=v_b_re, b_im=v_b_im, c_re=v_c_re, c_im=v_c_im,
                 d_skip=v_d_skip, w_glu=v_w_glu, w_out=v_w_out, w_ple=v_w_ple, w_ple_gate=v_w_ple_gate,
                 final_gain=v_final_gain)

    t = x.shape[1]
    xs = x.reshape(t, D_MODEL)
    ps = p.reshape(t, PLE_DIM)
    target = loss_target.reshape(t, D_MODEL)
    gain1 = norm_gain.reshape(1, D_MODEL)
    gain_f = final_gain.reshape(1, D_MODEL)
    scale_p = pool_scale.reshape(1, POOL_WIDTH)
    skip = d_skip.reshape(1, SSM_WIDTH)

    shard2d = {k: weights[k][0] for k in LARGE}
    shard_bf = {k: shard2d[k].astype(BF16) for k in LARGE}
    full = {"w_in": _all_gather("w_in_all_gather", [shard_bf["w_in"]], [LARGE_AXIS["w_in"]])[0]}
    later = [k for k in LARGE if k != "w_in"]
    later_axes = [LARGE_AXIS[k] for k in later]
    gather, gather_token = _gather_start("weights_gather_start", [shard_bf[k] for k in later], later_axes,
                                         full["w_in"])

    def arrive(k, after):
        i = later.index(k)
        full[k] = _gather_wait("gather_wait_" + k, gather, [i], [later_axes[i]], after)[0]

    ar, ai = a_re[0], a_im[0]
    ldt = log_dt.reshape(N_SSM_GROUPS, 1)
    br_t = jnp.transpose(b_re[0], (0, 2, 1))
    bi_t = jnp.transpose(b_im[0], (0, 2, 1))
    ab_re, ab_im, bb_re, bb_im = _ssm_params(ar, ai, ldt, br_t, bi_t)
    tiles = (SSM_CHUNKS, CHUNK_TILES, LANES)
    abar = jnp.concatenate([ab_re.reshape(tiles), ab_im.reshape(tiles)], axis=-1)
    w_pair = _compact_pair(bb_re, bb_im)
    c_pair = _compact_pair(c_re[0], -c_im[0])

    hn = _norm1_fwd(xs, gain1)
    proj = _mm_nn("in_proj", hn, full["w_in"], [F32], tk=2048, after=[gather_token])[0]
    pooled = _pool_fwd(proj)
    tm = _tile(t, 1024)
    arrive("w_pool", pooled)
    mixed = _mm("pool_mix", [(pooled, (tm, POOL_GROUP), lambda i, j, s: (i, j),
                              full["w_pool"], (None, POOL_GROUP, POOL_GROUP), lambda i, j, s: (j, 0, 0))],
                DOT_NN, (t // tm, N_POOL_GROUPS, 1),
                [((t, POOL_WIDTH), F32, (tm, POOL_GROUP), lambda i, j, s: (i, j))], 1)[0]
    y, gel, states = _ssm_fwd(proj, w_pair, c_pair, abar, skip)
    arrive("w_glu", gel)
    hg = _mm_nn("glu_proj", gel, full["w_glu"], [F32])[0]
    cat = _gate_fwd(mixed, proj, hg, scale_p)

    def residual_epilogue(acc, ex, out_refs):
        h = acc + ex[0][...]
        out_refs[0][...] = h
        out_refs[1][...] = h.astype(BF16)

    arrive("w_out", cat)
    h1, h1b = _mm_nn("out_proj", cat, full["w_out"], [F32, BF16], extras=[xs], epilogue=residual_epilogue)
    arrive("w_ple", h1b)
    e = _mm_nn("ple_proj", ps, full["w_ple"], [F32])[0]
    arrive("w_ple_gate", e)
    q = _mm_nn("ple_gate_proj", h1b, full["w_ple_gate"], [F32], tk=2048)[0]
    de, dq, dh2, g_final_gain, loss_part = _final(h1, e, q, target, gain_f)

    grads = {}
    grads["w_ple_gate"] = _mm_tn("ple_gate_wgrad", h1b, dq, BF16)
    grads["w_ple"] = _mm_tn("ple_wgrad", ps, de, BF16)
    sent, tokens = {}, {}

    def send(names):
        sent[names], tokens[names[0]] = _exchange_start(
            "grads_start_" + names[0], [grads[k] for k in names], [LARGE_AXIS[k] for k in names])

    send(("w_ple_gate", "w_ple"))
    dh1, dh1b = _mm_nt("ple_gate_dgrad", dq, full["w_ple_gate"], [F32, BF16], extras=[dh2],
                       epilogue=residual_epilogue)
    grads["w_out"] = _mm_tn("out_wgrad", cat, dh1b, BF16)
    send(("w_out",))
    dcat = _mm_nt("out_dgrad", dh1b, full["w_out"], [F32], tk=2048, after=[tokens["w_ple_gate"], tokens["w_out"]])[0]
    dmixed, dga, dgb, dhg, g_pool_scale = _gate_bwd(dcat, mixed, proj, hg, scale_p)

    tk = _tile(t, 1024)
    grads["w_pool"] = _mm("pool_wgrad", [(pooled, (tk, POOL_GROUP), lambda i, j, s: (s, i),
                                          dmixed, (tk, POOL_GROUP), lambda i, j, s: (s, i))],
                          DOT_TN, (N_POOL_GROUPS, 1, t // tk),
                          [((N_POOL_GROUPS, POOL_GROUP, POOL_GROUP), BF16, (None, POOL_GROUP, POOL_GROUP),
                            lambda i, j, s: (i, 0, 0))], t // tk)[0]
    dpooled = _mm("pool_dgrad", [(dmixed, (tm, POOL_GROUP), lambda i, j, s: (i, j),
                                  full["w_pool"], (None, POOL_GROUP, POOL_GROUP), lambda i, j, s: (j, 0, 0))],
                  DOT_NT, (t // tm, N_POOL_GROUPS, 1),
                  [((t, POOL_WIDTH), F32, (tm, POOL_GROUP), lambda i, j, s: (i, j))], 1)[0]
    dua = _pool_bwd(dpooled)

    grads["w_glu"] = _mm_tn("glu_wgrad", gel, dhg, BF16)
    send(("w_pool", "w_glu"))

    def gelu_bwd_epilogue(acc, ex, out_refs):
        yv = ex[0][...]
        th = jnp.tanh(GELU_C * (yv + GELU_A * yv * yv * yv))
        dgelu = 0.5 * (1.0 + th) + 0.5 * yv * (1.0 - th * th) * GELU_C * (1.0 + 3.0 * GELU_A * yv * yv)
        out_refs[0][...] = acc * dgelu

    dy = _mm_nt("glu_dgrad", dhg, full["w_glu"], [F32], tk=2048, extras=[y], epilogue=gelu_bwd_epilogue,
                after=[tokens["w_pool"]])[0]
    dub, g_c_pair, g_w_pair, g_abar, g_d_skip = _ssm_bwd(dy, proj, states, w_pair, c_pair, abar, skip)

    g_ab_re = g_abar[..., :LANES].reshape(N_SSM_GROUPS, SSM_STATE)
    g_ab_im = g_abar[..., LANES:].reshape(N_SSM_GROUPS, SSM_STATE)
    d_ar, d_ai, d_ldt, d_br_t, d_bi_t = _ssm_params_bwd(
        ar, ai, ldt, br_t, bi_t, g_ab_re, g_ab_im,
        _expand_grad(g_w_pair[..., :LANES]), _expand_grad(g_w_pair[..., LANES:]))

    small_grads = dict(
        pool_scale=g_pool_scale, a_re=d_ar, a_im=d_ai, log_dt=d_ldt.reshape(1, N_SSM_GROUPS),
        b_re=d_br_t.astype(BF16), b_im=d_bi_t.astype(BF16), c_re=_expand_grad(g_c_pair[..., :LANES]).astype(BF16),
        c_im=(-_expand_grad(g_c_pair[..., LANES:])).astype(BF16), d_skip=g_d_skip, final_gain=g_final_gain)
    early = [k for k in SMALL if k != "norm_gain"]
    early_sent, early_token = _gather_start(
        "small_grads_start", [small_grads[k][None] for k in early] + [jnp.broadcast_to(loss_part, (1, 1, LANES))],
        [0] * (len(early) + 1), d_ar)

    dproj = jnp.concatenate([dua, dga, dub, dgb], axis=1)
    grads["w_in"] = _mm_tn("in_wgrad", hn, dproj, BF16, after=[early_token])
    send(("w_in",))
    dhn = _mm_nt("in_dgrad", dproj, full["w_in"], [F32], tk=2048, after=[tokens["w_in"]])[0]
    grad_x, g_norm_gain = _norm1_bwd(xs, dhn, dh1, gain1)
    late_sent, late_token = _gather_start("norm_gain_grad_start", [g_norm_gain[None]], [0], g_norm_gain)

    out_g, out_d, out_m, out_v = ({} for _ in range(4))
    me = 4 * lax.axis_index("x") + 2 * lax.axis_index("y") + lax.axis_index("c")
    after = late_token
    for names, started in sent.items():
        axes = [LARGE_AXIS[k] for k in names]
        partials, landed = _exchange_wait("grads_wait_" + names[0], started, axes, after)
        for k, axis, partial, land in zip(names, axes, partials, landed):
            shard_shape = shard2d[k].shape
            size = shard_shape[axis]
            own = lax.dynamic_slice_in_dim(partial, me * size, size, axis=axis)
            view = (-1, shard_shape[-1])
            rows = math.prod(shard_shape[:-1])
            res = _adamw("adamw_" + k, shard2d[k].reshape(view), mom_m[k][0].reshape(view), mom_v[k][0].reshape(view),
                         [own.reshape(view), land.reshape(N_PEERS, rows, shard_shape[-1])])
            out_g[k], out_d[k], out_m[k], out_v[k] = (r.reshape(weights[k].shape) for r in res)
            after = res[0]

    def b_view(a):
        return jnp.transpose(a[0], (0, 2, 1))

    views = dict(norm_gain=lambda a: a, pool_scale=lambda a: a, a_re=lambda a: a[0], a_im=lambda a: a[0],
                 log_dt=lambda a: a, b_re=b_view, b_im=b_view, c_re=lambda a: a[0], c_im=lambda a: a[0],
                 d_skip=lambda a: a, final_gain=lambda a: a.reshape(1, D_MODEL))
    landed = _gather_wait("small_grads_wait", early_sent, list(range(len(early) + 1)), [0] * (len(early) + 1), after)
    stack = dict(zip(early, landed))
    stack["norm_gain"] = _gather_wait("norm_gain_grad_wait", late_sent, [0], [0], after)[0]
    *small_out, loss_row = _adamw_small(
        [views[k](weights[k]) for k in SMALL], [views[k](mom_m[k]) for k in SMALL],
        [views[k](mom_v[k]) for k in SMALL], [stack[k] for k in SMALL], landed[-1])
    loss = loss_row[0, 0]
    for out, res in zip((out_g, out_d, out_m, out_v), small_out):
        for k, r in zip(SMALL, res):
            if k in ("b_re", "b_im"):
                r = jnp.transpose(r, (0, 2, 1))
            out[k] = r.reshape(weights[k].shape)

    return (loss, grad_x.reshape(x.shape), *[out_g[k] for k in WEIGHTS], *[out_d[k] for k in WEIGHTS],
            *[out_m[k] for k in WEIGHTS], *[out_v[k] for k in WEIGHTS])
```

```python
import math

import jax
import jax.numpy as jnp
from jax import lax
from jax.experimental import pallas as pl
from jax.experimental.pallas import tpu as pltpu

F32 = jnp.float32
BF16 = jnp.bfloat16
MESH = pl.DeviceIdType.MESH
MESH_AXES = ("x", "y", "c")
N_DEV = 8

D_MODEL = 2048
POOL_WIDTH = 1024
SSM_WIDTH = 1024
N_POOL_GROUPS = 4
POOL_GROUP = 256
SSM_GROUP = 16
N_SSM_GROUPS = 64
SSM_STATE = 64
SSM_FLAT = N_SSM_GROUPS * SSM_STATE
SSM_CHUNKS = 4
CHUNK_IN = SSM_WIDTH // SSM_CHUNKS
CHUNK_STATE = SSM_FLAT // SSM_CHUNKS
PLE_DIM = 256
EPS = 1e-6
A_RE_MAX = -1e-4
ADAM_LR = 0.001
ADAM_B1 = 0.9
ADAM_B2 = 0.999
ADAM_EPS = 1e-08
ADAM_WD = 0.01
ADAM_STEP = 10
GELU_C = math.sqrt(2.0 / math.pi)
GELU_A = 0.044715

SUBLANES = 8
LANES = 128
VMEM_LIMIT_BYTES = 48 * 1024 * 1024

DOT_NN = (((1,), (0,)), ((), ()))
DOT_NT = (((1,), (1,)), ((), ()))
DOT_TN = (((0,), (0,)), ((), ()))


def _tile(n, pref):
    return pref if n % pref == 0 else n


def _params(sem):
    return pltpu.CompilerParams(dimension_semantics=sem, vmem_limit_bytes=VMEM_LIMIT_BYTES)


def _sigmoid(v):
    return 1.0 / (1.0 + jnp.exp(-v))


def _silu_and_grad(v):
    s = _sigmoid(v)
    return v * s, s * (1.0 + v * (1.0 - s))


def _mm(name, pairs, dims, grid, outs, k_steps, extras=(), epilogue=None):
    n_pairs, n_ex, n_out = len(pairs), len(extras), len(outs)
    acc_shape = tuple(d for d in outs[0][2] if d is not None)
    if epilogue is None:
        def epilogue(acc, ex, out_refs):
            out_refs[0][...] = acc.astype(out_refs[0].dtype)

    def body(*refs):
        ab = refs[:2 * n_pairs]
        ex = refs[2 * n_pairs:2 * n_pairs + n_ex]
        out_refs = refs[2 * n_pairs + n_ex:2 * n_pairs + n_ex + n_out]
        acc = refs[-1]
        k = pl.program_id(2)

        @pl.when(k == 0)
        def _():
            acc[...] = jnp.zeros_like(acc)

        part = None
        for q in range(n_pairs):
            d = lax.dot_general(ab[2 * q][...].astype(BF16), ab[2 * q + 1][...].astype(BF16), dims,
                                preferred_element_type=F32)
            part = d if part is None else part + d
        acc[...] += part

        @pl.when(k == k_steps - 1)
        def _():
            epilogue(acc[...], ex, out_refs)

    in_specs, operands = [], []
    for a, a_blk, a_map, b, b_blk, b_map in pairs:
        in_specs += [pl.BlockSpec(a_blk, a_map), pl.BlockSpec(b_blk, b_map)]
        operands += [a, b]
    for e, e_blk, e_map in extras:
        in_specs.append(pl.BlockSpec(e_blk, e_map))
        operands.append(e)
    return pl.pallas_call(
        body, name=name, grid=grid, in_specs=in_specs,
        out_specs=[pl.BlockSpec(o[2], o[3]) for o in outs],
        out_shape=[jax.ShapeDtypeStruct(o[0], o[1]) for o in outs],
        scratch_shapes=[pltpu.VMEM(acc_shape, F32)],
        compiler_params=_params(("arbitrary", "arbitrary", "arbitrary")),
    )(*operands)


def _after(tokens):
    return [(tok, tok.shape, lambda i, j, s: (0, 0)) for tok in tokens]


def _mm_nn(name, a, b, out_dtypes, tm=1024, tn=1024, tk=1024, a_col0=0, extras=(), epilogue=None, after=()):
    m, n = a.shape[0], b.shape[1]
    k = b.shape[0]
    tm, tn, tk = _tile(m, tm), _tile(n, tn), _tile(k, tk)
    outs = [((m, n), dt, (tm, tn), lambda i, j, s: (i, j)) for dt in out_dtypes]
    ex = [(e, (tm, tn), lambda i, j, s: (i, j)) for e in extras] + _after(after)
    return _mm(name, [(a, (tm, tk), lambda i, j, s: (i, a_col0 + s), b, (tk, tn), lambda i, j, s: (s, j))],
               DOT_NN, (m // tm, n // tn, k // tk), outs, k // tk, ex, epilogue)


def _mm_nt(name, a, b, out_dtypes, tm=1024, tn=1024, tk=1024, extras=(), epilogue=None, after=()):
    m, kk = a.shape
    n = b.shape[0]
    tm, tn, tk = _tile(m, tm), _tile(n, tn), _tile(kk, tk)
    outs = [((m, n), dt, (tm, tn), lambda i, j, s: (i, j)) for dt in out_dtypes]
    ex = [(e, (tm, tn), lambda i, j, s: (i, j)) for e in extras] + _after(after)
    return _mm(name, [(a, (tm, tk), lambda i, j, s: (i, s), b, (tn, tk), lambda i, j, s: (j, s))],
               DOT_NT, (m // tm, n // tn, kk // tk), outs, kk // tk, ex, epilogue)


def _mm_tn(name, a, b, out_dtype, tm=512, tn=2048, tk=1024, after=()):
    m, kk = a.shape
    n = b.shape[1]
    tm, tn, tk = _tile(kk, tm), _tile(n, tn), _tile(m, tk)
    outs = [((kk, n), out_dtype, (tm, tn), lambda i, j, s: (i, j))]
    return _mm(name, [(a, (tk, tm), lambda i, j, s: (s, i), b, (tk, tn), lambda i, j, s: (s, j))],
               DOT_TN, (kk // tm, n // tn, m // tk), outs, m // tk, _after(after))[0]


def _norm1_fwd(x, gain):
    t = x.shape[0]
    tm = _tile(t, 512)

    def body(x_ref, g_ref, hn_ref):
        xv = x_ref[...]
        r = lax.rsqrt(jnp.mean(xv * xv, axis=-1, keepdims=True) + EPS)
        hn_ref[...] = (xv * r * g_ref[...]).astype(BF16)

    return pl.pallas_call(
        body, name="norm1_fwd", grid=(t // tm,),
        in_specs=[pl.BlockSpec((tm, D_MODEL), lambda i: (i, 0)), pl.BlockSpec((1, D_MODEL), lambda i: (0, 0))],
        out_specs=pl.BlockSpec((tm, D_MODEL), lambda i: (i, 0)),
        out_shape=jax.ShapeDtypeStruct((t, D_MODEL), BF16),
        compiler_params=_params(("arbitrary",)),
    )(x, gain)


def _norm1_bwd(x, dhn, dh1, gain):
    t = x.shape[0]
    tm = _tile(t, 512)

    def body(x_ref, dhn_ref, dh1_ref, g_ref, dx_ref, gg_ref):
        @pl.when(pl.program_id(0) == 0)
        def _():
            gg_ref[...] = jnp.zeros_like(gg_ref)

        xv = x_ref[...]
        r = lax.rsqrt(jnp.mean(xv * xv, axis=-1, keepdims=True) + EPS)
        xh = xv * r
        dhn_v = dhn_ref[...]
        gg_ref[...] += jnp.sum(dhn_v * xh, axis=0, keepdims=True)
        dxh = dhn_v * g_ref[...]
        dx_ref[...] = dh1_ref[...] + r * (dxh - xh * jnp.mean(dxh * xh, axis=-1, keepdims=True))

    row = pl.BlockSpec((tm, D_MODEL), lambda i: (i, 0))
    vec = pl.BlockSpec((1, D_MODEL), lambda i: (0, 0))
    return pl.pallas_call(
        body, name="norm1_bwd", grid=(t // tm,),
        in_specs=[row, row, row, vec], out_specs=[row, vec],
        out_shape=[jax.ShapeDtypeStruct((t, D_MODEL), F32), jax.ShapeDtypeStruct((1, D_MODEL), F32)],
        compiler_params=_params(("arbitrary",)),
    )(x, dhn, dh1, gain)


def _pool_counts(t, width, group):
    row = lax.broadcasted_iota(jnp.int32, (t, width), 0)
    window = jnp.left_shift(jnp.int32(2), group)
    return row, jnp.minimum(row + 1, window).astype(F32)


def _select_window(group, s2, s4, s8, s16):
    return jnp.where(group == 0, s2, jnp.where(group == 1, s4, jnp.where(group == 2, s8, s16)))


def _pool_fwd(proj):
    t = proj.shape[0]
    tc = LANES

    def body(u_ref, o_ref):
        group = pl.program_id(0) // (POOL_GROUP // tc)
        v = u_ref[...]
        row, count = _pool_counts(t, tc, group)

        def down(a, j):
            return jnp.where(row >= j, pltpu.roll(a, j, 0), 0.0)

        s2 = v + down(v, 1)
        s4 = s2 + down(s2, 2)
        s8 = s4 + down(s4, 4)
        s16 = s8 + down(s8, 8)
        o_ref[...] = (_select_window(group, s2, s4, s8, s16) / count - v).astype(BF16)

    return pl.pallas_call(
        body, name="pool_fwd", grid=(POOL_WIDTH // tc,),
        in_specs=[pl.BlockSpec((t, tc), lambda j: (0, j))],
        out_specs=pl.BlockSpec((t, tc), lambda j: (0, j)),
        out_shape=jax.ShapeDtypeStruct((t, POOL_WIDTH), BF16),
        compiler_params=_params(("arbitrary",)),
    )(proj)


def _pool_bwd(dpooled, dproj):
    t = dpooled.shape[0]
    tc = LANES

    def body(d_ref, _, o_ref):
        group = pl.program_id(0) // (POOL_GROUP // tc)
        dp = d_ref[...]
        row, count = _pool_counts(t, tc, group)
        r = dp / count

        def up(a, j):
            return jnp.where(row < t - j, pltpu.roll(a, t - j, 0), 0.0)

        s2 = r + up(r, 1)
        s4 = s2 + up(s2, 2)
        s8 = s4 + up(s4, 4)
        s16 = s8 + up(s8, 8)
        o_ref[...] = (_select_window(group, s2, s4, s8, s16) - dp).astype(BF16)

    return pl.pallas_call(
        body, name="pool_bwd", grid=(POOL_WIDTH // tc,),
        in_specs=[pl.BlockSpec((t, tc), lambda j: (0, j)), pl.BlockSpec(memory_space=pl.ANY)],
        out_specs=pl.BlockSpec((t, tc), lambda j: (0, j)),
        out_shape=jax.ShapeDtypeStruct(dproj.shape, dproj.dtype),
        input_output_aliases={1: 0},
        compiler_params=_params(("arbitrary",)),
    )(dpooled, dproj)


def _gate_fwd(mixed, proj, hg, pool_scale):
    t = mixed.shape[0]
    tm = _tile(t, 512)

    def body(mx_ref, ga_ref, gb_ref, hg_ref, ps_ref, cat_ref):
        silu_a, _ = _silu_and_grad(ga_ref[...])
        cat_ref[:, :POOL_WIDTH] = (mx_ref[...] * ps_ref[...] * silu_a).astype(BF16)
        silu_b, _ = _silu_and_grad(gb_ref[...])
        sb = hg_ref[:, :SSM_WIDTH] * _sigmoid(hg_ref[:, SSM_WIDTH:])
        cat_ref[:, POOL_WIDTH:] = (sb * silu_b).astype(BF16)

    return pl.pallas_call(
        body, name="gate_fwd", grid=(t // tm,),
        in_specs=[pl.BlockSpec((tm, POOL_WIDTH), lambda i: (i, 0)),
                  pl.BlockSpec((tm, POOL_WIDTH), lambda i: (i, 1)),
                  pl.BlockSpec((tm, SSM_WIDTH), lambda i: (i, 3)),
                  pl.BlockSpec((tm, 2 * SSM_WIDTH), lambda i: (i, 0)),
                  pl.BlockSpec((1, POOL_WIDTH), lambda i: (0, 0))],
        out_specs=pl.BlockSpec((tm, D_MODEL), lambda i: (i, 0)),
        out_shape=jax.ShapeDtypeStruct((t, D_MODEL), BF16),
        compiler_params=_params(("arbitrary",)),
    )(mixed, proj, proj, hg, pool_scale)


def _gate_bwd(dcat, mixed, proj, hg, pool_scale):
    t = mixed.shape[0]
    tm = _tile(t, 512)

    def body(dc_ref, mx_ref, ga_ref, gb_ref, hg_ref, ps_ref, dmx_ref, dp_ref, dhg_ref, gps_ref):
        @pl.when(pl.program_id(0) == 0)
        def _():
            gps_ref[...] = jnp.zeros_like(gps_ref)

        ps = ps_ref[...]
        mx = mx_ref[...]
        dya = dc_ref[:, :POOL_WIDTH]
        silu_a, dsilu_a = _silu_and_grad(ga_ref[...])
        dpa = dya * silu_a
        gps_ref[...] += jnp.sum(dpa * mx, axis=0, keepdims=True)
        dmx_ref[...] = (dpa * ps).astype(BF16)
        dp_ref[:, :POOL_WIDTH] = jnp.zeros((tm, POOL_WIDTH), BF16)
        dp_ref[:, POOL_WIDTH:2 * POOL_WIDTH] = (dya * mx * ps * dsilu_a).astype(BF16)

        dyb = dc_ref[:, POOL_WIDTH:]
        silu_b, dsilu_b = _silu_and_grad(gb_ref[...])
        h_a = hg_ref[:, :SSM_WIDTH]
        sg = _sigmoid(hg_ref[:, SSM_WIDTH:])
        dsb = dyb * silu_b
        dp_ref[:, 2 * POOL_WIDTH:2 * POOL_WIDTH + SSM_WIDTH] = jnp.zeros((tm, SSM_WIDTH), BF16)
        dp_ref[:, 2 * POOL_WIDTH + SSM_WIDTH:] = (dyb * h_a * sg * dsilu_b).astype(BF16)
        dhg_ref[:, :SSM_WIDTH] = (dsb * sg).astype(BF16)
        dhg_ref[:, SSM_WIDTH:] = (dsb * h_a * sg * (1.0 - sg)).astype(BF16)

    half = pl.BlockSpec((tm, POOL_WIDTH), lambda i: (i, 0))
    full = pl.BlockSpec((tm, D_MODEL), lambda i: (i, 0))
    vec = pl.BlockSpec((1, POOL_WIDTH), lambda i: (0, 0))
    proj_width = 2 * POOL_WIDTH + 2 * SSM_WIDTH
    return pl.pallas_call(
        body, name="gate_bwd", grid=(t // tm,),
        in_specs=[full, half,
                  pl.BlockSpec((tm, POOL_WIDTH), lambda i: (i, 1)),
                  pl.BlockSpec((tm, SSM_WIDTH), lambda i: (i, 3)),
                  full, vec],
        out_specs=[half, pl.BlockSpec((tm, proj_width), lambda i: (i, 0)), full, vec],
        out_shape=[jax.ShapeDtypeStruct((t, POOL_WIDTH), BF16), jax.ShapeDtypeStruct((t, proj_width), BF16),
                   jax.ShapeDtypeStruct((t, 2 * SSM_WIDTH), BF16),
                   jax.ShapeDtypeStruct((1, POOL_WIDTH), F32)],
        compiler_params=_params(("arbitrary",)),
    )(dcat, mixed, proj, proj, hg, pool_scale)


def _ple_final(h1, h1b, p, w_gate, w_ple, target, gain):
    t = h1.shape[0]
    tm = _tile(t, 256)

    def body(h1_ref, h1b_ref, p_ref, wg_ref, wp_ref, tg_ref, g_ref, de_ref, dq_ref, dh2_ref, gg_ref, loss_ref):
        @pl.when(pl.program_id(0) == 0)
        def _():
            gg_ref[...] = jnp.zeros_like(gg_ref)
            loss_ref[...] = jnp.zeros_like(loss_ref)

        ev = jnp.dot(p_ref[...].astype(BF16), wp_ref[...], preferred_element_type=F32)
        sg = _sigmoid(jnp.dot(h1b_ref[...], wg_ref[...], preferred_element_type=F32))
        h2 = h1_ref[...] + ev * sg
        r = lax.rsqrt(jnp.mean(h2 * h2, axis=-1, keepdims=True) + EPS)
        n = h2 * r
        gain_v = g_ref[...]
        diff = n * gain_v - tg_ref[...]
        row_loss = jnp.sum(diff * diff, axis=-1, keepdims=True)
        loss_ref[...] += (0.5 / D_MODEL) * jnp.sum(row_loss, axis=0, keepdims=True)
        dout = diff * (1.0 / D_MODEL)
        gg_ref[...] += jnp.sum(dout * n, axis=0, keepdims=True)
        dn = dout * gain_v
        dh2 = r * (dn - n * jnp.mean(dn * n, axis=-1, keepdims=True))
        dh2_ref[...] = dh2
        de_ref[...] = (dh2 * sg).astype(BF16)
        dq_ref[...] = (dh2 * ev * sg * (1.0 - sg)).astype(BF16)

    row = pl.BlockSpec((tm, D_MODEL), lambda i: (i, 0))
    vec = pl.BlockSpec((1, D_MODEL), lambda i: (0, 0))
    return pl.pallas_call(
        body, name="ple_final", grid=(t // tm,),
        in_specs=[row, row, pl.BlockSpec((tm, PLE_DIM), lambda i: (i, 0)), _resident((D_MODEL, D_MODEL)),
                  _resident((PLE_DIM, D_MODEL)), row, vec],
        out_specs=[row, row, row, vec, pl.BlockSpec((1, 1), lambda i: (0, 0))],
        out_shape=[jax.ShapeDtypeStruct((t, D_MODEL), BF16), jax.ShapeDtypeStruct((t, D_MODEL), BF16),
                   jax.ShapeDtypeStruct((t, D_MODEL), F32), jax.ShapeDtypeStruct((1, D_MODEL), F32),
                   jax.ShapeDtypeStruct((1, 1), F32)],
        compiler_params=_params(("arbitrary",)),
    )(h1, h1b, p, w_gate, w_ple, target, gain)


def _zoh(a_re, a_im, log_dt, b_re_t, b_im_t):
    lam_re = jnp.minimum(a_re, A_RE_MAX)
    lam_im = a_im
    dt = jnp.exp(log_dt)
    mag = jnp.exp(lam_re * dt)
    ang = lam_im * dt
    ab_re = mag * jnp.cos(ang)
    ab_im = mag * jnp.sin(ang)
    den = lam_re * lam_re + lam_im * lam_im
    n_re = ab_re - 1.0
    n_im = ab_im
    q_re = (n_re * lam_re + n_im * lam_im) / den
    q_im = (n_im * lam_re - n_re * lam_im) / den
    bb_re = q_re[:, None, :] * b_re_t - q_im[:, None, :] * b_im_t
    bb_im = q_re[:, None, :] * b_im_t + q_im[:, None, :] * b_re_t
    return ab_re, ab_im, bb_re, bb_im


def _ssm_params(a_re, a_im, log_dt, b_re_t, b_im_t):
    def body(are_ref, aim_ref, dt_ref, bre_ref, bim_ref, abre_ref, abim_ref, bbre_ref, bbim_ref):
        ab_re, ab_im, bb_re, bb_im = _zoh(are_ref[...], aim_ref[...], dt_ref[...], bre_ref[...], bim_ref[...])
        abre_ref[...] = ab_re
        abim_ref[...] = ab_im
        bbre_ref[...] = bb_re
        bbim_ref[...] = bb_im

    return pl.pallas_call(
        body, name="ssm_params",
        out_shape=[jax.ShapeDtypeStruct(a_re.shape, F32), jax.ShapeDtypeStruct(a_re.shape, F32),
                   jax.ShapeDtypeStruct(b_re_t.shape, F32), jax.ShapeDtypeStruct(b_re_t.shape, F32)],
        compiler_params=_params(None),
    )(a_re, a_im, log_dt, b_re_t, b_im_t)


def _ssm_params_bwd(a_re, a_im, log_dt, b_re_t, b_im_t, g_ab_re, g_ab_im, g_bb_re, g_bb_im):
    def body(are_ref, aim_ref, dt_ref, bre_ref, bim_ref, gar_ref, gai_ref, gbr_ref, gbi_ref,
             o_are, o_aim, o_dt, o_bre, o_bim):
        _, vjp = jax.vjp(_zoh, are_ref[...], aim_ref[...], dt_ref[...], bre_ref[...], bim_ref[...])
        d_are, d_aim, d_dt, d_bre, d_bim = vjp((gar_ref[...], gai_ref[...], gbr_ref[...], gbi_ref[...]))
        o_are[...] = d_are
        o_aim[...] = d_aim
        o_dt[...] = d_dt
        o_bre[...] = d_bre
        o_bim[...] = d_bim

    ins = (a_re, a_im, log_dt, b_re_t, b_im_t)
    return pl.pallas_call(
        body, name="ssm_params_bwd",
        out_shape=[jax.ShapeDtypeStruct(v.shape, F32) for v in ins],
        compiler_params=_params(None),
    )(*ins, g_ab_re, g_ab_im, g_bb_re, g_bb_im)


CHUNK_TILES = CHUNK_STATE // LANES
CH_PER_TILE = CHUNK_IN // CHUNK_TILES
PAIR = 2 * LANES
SSM_ROWS = 256
SCAN_STEPS = 8
U_COLUMN_BLOCK = 2 * POOL_WIDTH // SSM_WIDTH


def _own_half():
    r = lax.broadcasted_iota(jnp.int32, (CHUNK_IN, LANES), 0) // SSM_GROUP % 2
    c = lax.broadcasted_iota(jnp.int32, (CHUNK_IN, LANES), 1) // SSM_STATE
    return (r == c)[None]


def _compact_weight(w):
    tiled = jnp.tile(w.reshape(SSM_CHUNKS, CHUNK_IN, SSM_STATE), (1, 1, 2))
    return jnp.where(_own_half(), tiled, 0.0)


def _compact_pair(w_a, w_b):
    return jnp.concatenate([_compact_weight(w_a), _compact_weight(w_b)], axis=-1).astype(BF16)


def _expand_grad(g):
    kept = jnp.where(_own_half(), g, 0.0)
    return kept.reshape(SSM_CHUNKS, CHUNK_IN, 2, SSM_STATE).sum(axis=2).reshape(N_SSM_GROUPS, SSM_GROUP, SSM_STATE)


TILES_PER_BLOCK = LANES // CH_PER_TILE
IN_BLOCKS = CHUNK_IN // LANES


def _tile_masks():
    j = lax.broadcasted_iota(jnp.int32, (CHUNK_TILES, LANES), 0) % TILES_PER_BLOCK
    lane = lax.broadcasted_iota(jnp.int32, (CHUNK_TILES, LANES), 1) // CH_PER_TILE
    return (j == lane).astype(F32)


def _tile_rows(ref, j, tt):
    return ref.at[j // TILES_PER_BLOCK, pl.ds(j, tt, stride=CHUNK_TILES), :]


def _spread(ref, v, masks):
    tt = v.shape[0]
    for j in range(CHUNK_TILES):
        block = LANES * (j // TILES_PER_BLOCK)
        _tile_rows(ref, j, tt)[...] = v[:, block:block + LANES] * masks[j:j + 1, :]
    return jnp.concatenate([ref[b] for b in range(IN_BLOCKS)], axis=1).astype(BF16)


def _gather(ref, full, masks):
    tt = full.shape[0] // CHUNK_TILES
    for b in range(IN_BLOCKS):
        ref[b] = full[:, b * LANES:(b + 1) * LANES]
    out = []
    for b in range(IN_BLOCKS):
        acc = None
        for j in range(b * TILES_PER_BLOCK, (b + 1) * TILES_PER_BLOCK):
            part = _tile_rows(ref, j, tt)[...] * masks[j:j + 1, :]
            acc = part if acc is None else acc + part
        out.append(acc)
    return jnp.concatenate(out, axis=1)


def _resident(shape):
    return pl.BlockSpec(shape, lambda i: (0,) * len(shape), pipeline_mode=pl.Buffered(1))


def _halves(ref, k, rows=slice(None)):
    return ref[k, rows, :LANES], ref[k, rows, LANES:]


def _ssm_fwd(proj, w2, c2, a2, d_skip):
    t = proj.shape[0]
    tt = _tile(t, SSM_ROWS)
    rows = tt * CHUNK_TILES

    def body(u_ref, w_ref, c_ref, a_ref, d_ref, y_ref, gel_ref, s_ref, carry, spread_ref, full_ref):
        @pl.when(pl.program_id(0) == 0)
        def _():
            carry[...] = jnp.zeros_like(carry)
            spread_ref[...] = jnp.zeros_like(spread_ref)

        mask = _tile_masks()
        u = u_ref[...]
        for k in range(SSM_CHUNKS):
            uk = _spread(spread_ref, u[:, k * CHUNK_IN:(k + 1) * CHUNK_IN], mask)
            s_ref[k] = jnp.dot(uk, w_ref[k], preferred_element_type=F32)

        abar = [_halves(a_ref, k) for k in range(SSM_CHUNKS)]

        def steps(i, state):
            for v in range(SCAN_STEPS):
                r = pl.ds(pl.multiple_of((i * SCAN_STEPS + v) * CHUNK_TILES, CHUNK_TILES), CHUNK_TILES)
                new = []
                for k, ((a_re, a_im), (s_re, s_im)) in enumerate(zip(abar, state)):
                    b_re, b_im = _halves(s_ref, k, r)
                    s_re, s_im = a_re * s_re - a_im * s_im + b_re, a_re * s_im + a_im * s_re + b_im
                    s_ref[k, r, :LANES] = s_re
                    s_ref[k, r, LANES:] = s_im
                    new.append((s_re, s_im))
                state = tuple(new)
            return state

        state = lax.fori_loop(0, tt // SCAN_STEPS, steps, tuple(_halves(carry, k) for k in range(SSM_CHUNKS)))
        for k, (s_re, s_im) in enumerate(state):
            carry[k, :, :LANES] = s_re
            carry[k, :, LANES:] = s_im

        for k in range(SSM_CHUNKS):
            cols = slice(k * CHUNK_IN, (k + 1) * CHUNK_IN)
            full = lax.dot_general(s_ref[k].astype(BF16), c_ref[k], DOT_NT, preferred_element_type=F32)
            y = _gather(full_ref, full, mask) + d_ref[:, cols] * u[:, cols]
            y_ref[:, cols] = y
            gel_ref[:, cols] = (0.5 * y * (1.0 + jnp.tanh(GELU_C * (y + GELU_A * y * y * y)))).astype(BF16)

    weight = _resident((SSM_CHUNKS, CHUNK_IN, PAIR))
    tokens = pl.BlockSpec((tt, SSM_WIDTH), lambda i: (i, 0))
    return pl.pallas_call(
        body, name="ssm_fwd", grid=(t // tt,),
        in_specs=[pl.BlockSpec((tt, SSM_WIDTH), lambda i: (i, U_COLUMN_BLOCK)), weight, weight,
                  _resident((SSM_CHUNKS, CHUNK_TILES, PAIR)), _resident((1, SSM_WIDTH))],
        out_specs=[tokens, tokens, pl.BlockSpec((SSM_CHUNKS, rows, PAIR), lambda i: (0, i, 0))],
        out_shape=[jax.ShapeDtypeStruct((t, SSM_WIDTH), F32), jax.ShapeDtypeStruct((t, SSM_WIDTH), BF16),
                   jax.ShapeDtypeStruct((SSM_CHUNKS, t * CHUNK_TILES, PAIR), F32)],
        scratch_shapes=[pltpu.VMEM((SSM_CHUNKS, CHUNK_TILES, PAIR), F32), pltpu.VMEM((IN_BLOCKS, rows, LANES), F32),
                        pltpu.VMEM((IN_BLOCKS, rows, LANES), F32)],
        compiler_params=_params(("arbitrary",)),
    )(proj, w2, c2, a2, d_skip)


def _ssm_bwd(dy, proj, s, w2, c2, a2, d_skip, dproj):
    t = dy.shape[0]
    tt = _tile(t, SSM_ROWS)
    rows = tt * CHUNK_TILES
    n_chunks = t // tt

    def body(dy_ref, u_ref, s_ref, w_ref, c_ref, a_ref, d_ref, _, du_ref, gc_ref, gw_ref, ga_ref, gd_ref, z_ref, carry,
             spread_ref, full_ref):
        @pl.when(pl.program_id(0) == 0)
        def _():
            for r in (carry, gc_ref, gw_ref, ga_ref, gd_ref, spread_ref):
                r[...] = jnp.zeros_like(r)

        mask = _tile_masks()
        dy_v = dy_ref[...]
        u = u_ref[...]
        gd_ref[...] += jnp.sum(dy_v * u, axis=0, keepdims=True)
        for k in range(SSM_CHUNKS):
            dk = _spread(spread_ref, dy_v[:, k * CHUNK_IN:(k + 1) * CHUNK_IN], mask)
            z_ref[k] = jnp.dot(dk, c_ref[k], preferred_element_type=F32)
            gc_ref[k] += lax.dot_general(dk, s_ref[k].astype(BF16), DOT_TN, preferred_element_type=F32)

        abar = [_halves(a_ref, k) for k in range(SSM_CHUNKS)]

        def steps(i, state):
            zs, gs = state
            for v in range(SCAN_STEPS):
                tok = tt - 1 - (i * SCAN_STEPS + v)
                r = pl.ds(pl.multiple_of(tok * CHUNK_TILES, CHUNK_TILES), CHUNK_TILES)
                new_z, new_g = [], []
                for k, ((a_re, a_im), (z_re, z_im), (g_re, g_im)) in enumerate(zip(abar, zs, gs)):
                    s_re, s_im = _halves(s_ref, k, r)
                    g_re = g_re + z_re * s_re + z_im * s_im
                    g_im = g_im + z_im * s_re - z_re * s_im
                    d_re, d_im = _halves(z_ref, k, r)
                    z_re, z_im = d_re + a_re * z_re + a_im * z_im, d_im + a_re * z_im - a_im * z_re
                    z_ref[k, r, :LANES] = z_re
                    z_ref[k, r, LANES:] = z_im
                    new_z.append((z_re, z_im))
                    new_g.append((g_re, g_im))
                zs, gs = tuple(new_z), tuple(new_g)
            return zs, gs

        zs, gs = lax.fori_loop(0, tt // SCAN_STEPS, steps,
                               (tuple(_halves(carry, k) for k in range(SSM_CHUNKS)),
                                tuple(_halves(ga_ref, k) for k in range(SSM_CHUNKS))))
        for k in range(SSM_CHUNKS):
            carry[k, :, :LANES], carry[k, :, LANES:] = zs[k]
            ga_ref[k, :, :LANES], ga_ref[k, :, LANES:] = gs[k]

        for k in range(SSM_CHUNKS):
            cols = slice(k * CHUNK_IN, (k + 1) * CHUNK_IN)
            zb = z_ref[k].astype(BF16)
            full = lax.dot_general(zb, w_ref[k], DOT_NT, preferred_element_type=F32)
            du_ref[:, cols] = (_gather(full_ref, full, mask) + d_ref[:, cols] * dy_v[:, cols]).astype(BF16)
            uk = _spread(spread_ref, u[:, cols], mask)
            gw_ref[k] += lax.dot_general(uk, zb, DOT_TN, preferred_element_type=F32)

    weight = _resident((SSM_CHUNKS, CHUNK_IN, PAIR))
    tokens = pl.BlockSpec((tt, SSM_WIDTH), lambda i: (n_chunks - 1 - i, 0))
    grad = pl.BlockSpec((SSM_CHUNKS, CHUNK_IN, PAIR), lambda i: (0, 0, 0))
    return pl.pallas_call(
        body, name="ssm_bwd", grid=(n_chunks,),
        in_specs=[tokens, pl.BlockSpec((tt, SSM_WIDTH), lambda i: (n_chunks - 1 - i, U_COLUMN_BLOCK)),
                  pl.BlockSpec((SSM_CHUNKS, rows, PAIR), lambda i: (0, n_chunks - 1 - i, 0)), weight, weight,
                  _resident((SSM_CHUNKS, CHUNK_TILES, PAIR)), _resident((1, SSM_WIDTH)),
                  pl.BlockSpec(memory_space=pl.ANY)],
        out_specs=[pl.BlockSpec((tt, SSM_WIDTH), lambda i: (n_chunks - 1 - i, U_COLUMN_BLOCK)), grad, grad,
                   pl.BlockSpec((SSM_CHUNKS, CHUNK_TILES, PAIR), lambda i: (0, 0, 0)),
                   pl.BlockSpec((1, SSM_WIDTH), lambda i: (0, 0))],
        out_shape=[jax.ShapeDtypeStruct(dproj.shape, dproj.dtype), jax.ShapeDtypeStruct((SSM_CHUNKS, CHUNK_IN, PAIR), F32),
                   jax.ShapeDtypeStruct((SSM_CHUNKS, CHUNK_IN, PAIR), F32),
                   jax.ShapeDtypeStruct((SSM_CHUNKS, CHUNK_TILES, PAIR), F32), jax.ShapeDtypeStruct((1, SSM_WIDTH), F32)],
        input_output_aliases={7: 0},
        scratch_shapes=[pltpu.VMEM((SSM_CHUNKS, rows, PAIR), F32), pltpu.VMEM((SSM_CHUNKS, CHUNK_TILES, PAIR), F32),
                        pltpu.VMEM((IN_BLOCKS, rows, LANES), F32), pltpu.VMEM((IN_BLOCKS, rows, LANES), F32)],
        compiler_params=_params(("arbitrary",)),
    )(dy, proj, s, w2, c2, a2, d_skip, dproj)


def _block(ref, axis, size, index):
    idx = [slice(None)] * len(ref.shape)
    idx[axis] = pl.ds(pl.multiple_of(index * size, size), size)
    return ref.at[tuple(idx)]


def _all_gather(name, shards, axes):
    n = len(shards)
    sizes = [s.shape[a] for s, a in zip(shards, axes)]

    def body(*refs):
        ins, outs = refs[:n], refs[n:2 * n]
        send_sems, recv_sems, local_sems = refs[2 * n:]
        x, y, c = (lax.axis_index(a) for a in MESH_AXES)
        me, sibling = (x, y, c), (x, y, 1 - c)
        chips = [(1 - x, y), (x, 1 - y), (1 - x, 1 - y)]

        def rows(i, dev):
            return _block(outs[i], axes[i], sizes[i], 4 * dev[0] + 2 * dev[1] + dev[2])

        def copy(i, k, block, to, src=None):
            return pltpu.make_async_remote_copy(
                src_ref=rows(i, block) if src is None else src, dst_ref=rows(i, block),
                send_sem=send_sems.at[7 * i + k], recv_sem=recv_sems.at[7 * i + k],
                device_id=to, device_id_type=MESH)

        mine = [pltpu.make_async_copy(ins[i], rows(i, me), local_sems.at[i]) for i in range(n)]
        for cp in mine:
            cp.start()
        first = []
        for i in range(n):
            first.append(copy(i, 0, me, sibling, src=ins[i]))
            first += [copy(i, 1 + j, me, (*chip, c), src=ins[i]) for j, chip in enumerate(chips)]
        for cp in first:
            cp.start()
        passed = []
        for i in range(n):
            for j, chip in enumerate(chips):
                copy(i, 1 + j, (*chip, c), me).wait_recv()
                fwd = copy(i, 4 + j, (*chip, c), sibling)
                fwd.start()
                passed.append(fwd)
        for i in range(n):
            copy(i, 0, sibling, me).wait_recv()
            for j, chip in enumerate(chips):
                copy(i, 4 + j, (*chip, 1 - c), me).wait_recv()
        for cp in first + passed:
            cp.wait_send()
        for cp in mine:
            cp.wait()

    out_shape = []
    for s, a in zip(shards, axes):
        shape = list(s.shape)
        shape[a] *= N_DEV
        out_shape.append(jax.ShapeDtypeStruct(tuple(shape), s.dtype))
    any_spec = pl.BlockSpec(memory_space=pl.ANY)
    return pl.pallas_call(
        body, name=name, out_shape=out_shape,
        in_specs=[any_spec] * n, out_specs=[any_spec] * n,
        scratch_shapes=[pltpu.SemaphoreType.DMA((7 * n,)), pltpu.SemaphoreType.DMA((7 * n,)),
                        pltpu.SemaphoreType.DMA((n,))],
    )(*shards)


HBM_SPEC = pl.BlockSpec(memory_space=pltpu.HBM)
SEM_SPEC = pl.BlockSpec(memory_space=pltpu.SEMAPHORE)
ANY_SPEC = pl.BlockSpec(memory_space=pl.ANY)
SPLIT_PARAMS = pltpu.CompilerParams(has_side_effects=pltpu.SideEffectType.DATAFLOW_SIDE_EFFECTING)
N_PEERS = N_DEV - 1
TOKEN = jax.ShapeDtypeStruct((SUBLANES, LANES), F32)
VMEM_SPEC = pl.BlockSpec(memory_space=pltpu.VMEM)


def _in_hbm(arrays):
    return [pltpu.with_memory_space_constraint(a, pltpu.HBM) for a in arrays]


def _peer(m):
    x, y, c = (lax.axis_index(a) for a in MESH_AXES)
    px = 1 - x if m & 4 else x
    py = 1 - y if m & 2 else y
    pc = 1 - c if m & 1 else c
    return (px, py, pc), 4 * px + 2 * py + pc


def _my_index():
    x, y, c = (lax.axis_index(a) for a in MESH_AXES)
    return 4 * x + 2 * y + c


def _gather_copies(shard_refs, full_refs, axes, send_sems, recv_sems):
    copies = []
    for i, (shard, full) in enumerate(zip(shard_refs, full_refs)):
        mine = _block(full, axes[i], shard.shape[axes[i]], _my_index())
        for m in range(1, N_DEV):
            peer, _ = _peer(m)
            copies.append(pltpu.make_async_remote_copy(
                src_ref=shard, dst_ref=mine, send_sem=send_sems.at[N_PEERS * i + m - 1],
                recv_sem=recv_sems.at[N_PEERS * i + m - 1], device_id=peer, device_id_type=MESH))
    return copies


def _gather_start(name, shards, axes, after):
    n = len(shards)

    def body(*refs):
        shard_refs = refs[:n]
        send_sems, recv_sems, local_sems = refs[n + 1:n + 4]
        full_refs = refs[2 * n + 4:3 * n + 4]
        refs[3 * n + 4][...] = jnp.zeros(TOKEN.shape, TOKEN.dtype)
        for i in range(n):
            pltpu.make_async_copy(shard_refs[i], _block(full_refs[i], axes[i], shard_refs[i].shape[axes[i]], _my_index()),
                                  local_sems.at[i]).start()
        for cp in _gather_copies(shard_refs, full_refs, axes, send_sems, recv_sems):
            cp.start()

    fulls = []
    for s, a in zip(shards, axes):
        shape = list(s.shape)
        shape[a] *= N_DEV
        fulls.append(pltpu.HBM(tuple(shape), s.dtype))
    out = pl.pallas_call(
        body, name=name,
        out_shape=(pltpu.SemaphoreType.DMA((N_PEERS * n,)), pltpu.SemaphoreType.DMA((N_PEERS * n,)),
                   pltpu.SemaphoreType.DMA((n,)), *[pltpu.HBM(s.shape, s.dtype) for s in shards], *fulls, TOKEN),
        in_specs=[HBM_SPEC] * n + [ANY_SPEC],
        out_specs=(SEM_SPEC, SEM_SPEC, SEM_SPEC, *[HBM_SPEC] * (2 * n), VMEM_SPEC),
        input_output_aliases={i: 3 + i for i in range(n)},
        compiler_params=SPLIT_PARAMS,
    )(*_in_hbm(shards), after)
    return out[:-1], out[-1]


def _gather_wait(name, started, indices, axes, after):
    send_sems, recv_sems, local_sems = started[:3]
    n_all = (len(started) - 3) // 2
    shards = [started[3 + i] for i in indices]
    fulls = [started[3 + n_all + i] for i in indices]
    n = len(indices)

    def body(*refs):
        shard_refs, full_refs = refs[:n], refs[n:2 * n]
        send_sems, recv_sems, local_sems = refs[2 * n:2 * n + 3]
        for j, i in enumerate(indices):
            mine = _block(full_refs[j], axes[j], shard_refs[j].shape[axes[j]], _my_index())
            pltpu.make_async_copy(shard_refs[j], mine, local_sems.at[i]).wait()
            for m in range(1, N_DEV):
                peer, _ = _peer(m)
                cp = pltpu.make_async_remote_copy(
                    src_ref=shard_refs[j], dst_ref=mine, send_sem=send_sems.at[N_PEERS * i + m - 1],
                    recv_sem=recv_sems.at[N_PEERS * i + m - 1], device_id=peer, device_id_type=MESH)
                cp.wait_send()
                cp.wait_recv()

    out = pl.pallas_call(
        body, name=name,
        out_shape=tuple(pltpu.HBM(a.shape, a.dtype) for a in shards + fulls),
        in_specs=[HBM_SPEC] * (2 * n) + [SEM_SPEC] * 3 + [ANY_SPEC], out_specs=tuple([HBM_SPEC] * (2 * n)),
        input_output_aliases={i: i for i in range(2 * n)},
        compiler_params=SPLIT_PARAMS,
    )(*shards, *fulls, send_sems, recv_sems, local_sems, after)
    return out[n:]


def _exchange_start(name, fulls, axes):
    n = len(fulls)
    sizes = [f.shape[a] // N_DEV for f, a in zip(fulls, axes)]

    def body(*refs):
        ins = refs[:n]
        send_sems, recv_sems = refs[n:n + 2]
        lands = refs[2 * n + 2:3 * n + 2]
        refs[3 * n + 2][...] = jnp.zeros(TOKEN.shape, TOKEN.dtype)
        for i in range(n):
            for m in range(1, N_DEV):
                peer, index = _peer(m)
                pltpu.make_async_remote_copy(
                    src_ref=_block(ins[i], axes[i], sizes[i], index), dst_ref=lands[i].at[m - 1],
                    send_sem=send_sems.at[N_PEERS * i + m - 1], recv_sem=recv_sems.at[N_PEERS * i + m - 1],
                    device_id=peer, device_id_type=MESH).start()

    lands = []
    for f, a, size in zip(fulls, axes, sizes):
        shape = list(f.shape)
        shape[a] = size
        lands.append(pltpu.HBM((N_PEERS, *shape), f.dtype))
    out = pl.pallas_call(
        body, name=name,
        out_shape=(pltpu.SemaphoreType.DMA((N_PEERS * n,)), pltpu.SemaphoreType.DMA((N_PEERS * n,)),
                   *[pltpu.HBM(f.shape, f.dtype) for f in fulls], *lands, TOKEN),
        in_specs=[HBM_SPEC] * n, out_specs=(SEM_SPEC, SEM_SPEC, *[HBM_SPEC] * (2 * n), VMEM_SPEC),
        input_output_aliases={i: 2 + i for i in range(n)},
        compiler_params=SPLIT_PARAMS,
    )(*_in_hbm(fulls))
    return out[:-1], out[-1]


def _exchange_wait(name, started, axes, after):
    send_sems, recv_sems = started[:2]
    n = (len(started) - 2) // 2
    fulls, lands = list(started[2:2 + n]), list(started[2 + n:])
    sizes = [f.shape[a] // N_DEV for f, a in zip(fulls, axes)]

    def body(*refs):
        ins, land_refs = refs[:n], refs[n:2 * n]
        send_sems, recv_sems = refs[2 * n:2 * n + 2]
        for i in range(n):
            for m in range(1, N_DEV):
                peer, index = _peer(m)
                cp = pltpu.make_async_remote_copy(
                    src_ref=_block(ins[i], axes[i], sizes[i], index), dst_ref=land_refs[i].at[m - 1],
                    send_sem=send_sems.at[N_PEERS * i + m - 1], recv_sem=recv_sems.at[N_PEERS * i + m - 1],
                    device_id=peer, device_id_type=MESH)
                cp.wait_send()
                cp.wait_recv()

    out = pl.pallas_call(
        body, name=name,
        out_shape=tuple(pltpu.HBM(a.shape, a.dtype) for a in fulls + lands),
        in_specs=[HBM_SPEC] * (2 * n) + [SEM_SPEC] * 2 + [ANY_SPEC], out_specs=tuple([HBM_SPEC] * (2 * n)),
        input_output_aliases={i: i for i in range(2 * n)},
        compiler_params=SPLIT_PARAMS,
    )(*fulls, *lands, send_sems, recv_sems, after)
    return out[:n], out[n:]


def _adamw_update(w_ref, m_ref, v_ref, part_refs, g_ref, d_ref, nm_ref, nv_ref):
    c1 = 1.0 - ADAM_B1 ** ADAM_STEP
    c2 = 1.0 - ADAM_B2 ** ADAM_STEP
    g = None
    for p_ref in part_refs:
        stacked = len(p_ref.shape) > len(w_ref.shape)
        terms = [p_ref[s] for s in range(p_ref.shape[0])] if stacked else [p_ref[...]]
        for term in terms:
            term = term.astype(F32)
            g = term if g is None else g + term
    new_m = ADAM_B1 * m_ref[...] + (1.0 - ADAM_B1) * g
    new_v = ADAM_B2 * v_ref[...] + (1.0 - ADAM_B2) * (g * g)
    g_ref[...] = g
    nm_ref[...] = new_m
    nv_ref[...] = new_v
    d_ref[...] = -ADAM_LR * ((new_m / c1) / (jnp.sqrt(new_v / c2) + ADAM_EPS) + ADAM_WD * w_ref[...])


def _adamw_small(ws, ms, vs, stacks, loss_stack):
    n = len(ws)

    def body(*refs):
        ins, outs = refs[:4 * n + 1], refs[4 * n + 1:]
        for i in range(n):
            _adamw_update(ins[i], ins[n + i], ins[2 * n + i], [ins[3 * n + i]],
                          outs[i], outs[n + i], outs[2 * n + i], outs[3 * n + i])
        total = ins[4 * n][0]
        for dev in range(1, N_DEV):
            total = total + ins[4 * n][dev]
        outs[4 * n][...] = total

    res = pl.pallas_call(
        body, name="adamw_small",
        out_shape=[jax.ShapeDtypeStruct(w.shape, F32) for w in ws] * 4 + [jax.ShapeDtypeStruct((1, LANES), F32)],
        compiler_params=_params(None),
    )(*ws, *ms, *vs, *stacks, loss_stack)
    return res[:n], res[n:2 * n], res[2 * n:3 * n], res[3 * n:4 * n], res[4 * n]


def _adamw(name, w, m, v, parts):
    r, c = w.shape
    tr = _tile(r, 256)
    n_parts = len(parts)

    def body(*refs):
        _adamw_update(refs[0], refs[1], refs[2], refs[3:3 + n_parts], *refs[3 + n_parts:])

    row = pl.BlockSpec((tr, c), lambda i: (i, 0))
    in_specs = [row, row, row]
    for p in parts:
        in_specs.append(row if p.ndim == 2 else pl.BlockSpec((p.shape[0], tr, c), lambda i: (0, i, 0)))
    return pl.pallas_call(
        body, name=name, grid=(r // tr,), in_specs=in_specs, out_specs=[row] * 4,
        out_shape=[jax.ShapeDtypeStruct((r, c), F32)] * 4,
        compiler_params=_params(("arbitrary",)),
    )(w, m, v, *parts)


SMALL = ("norm_gain", "pool_scale", "a_re", "a_im", "log_dt", "b_re", "b_im", "c_re", "c_im", "d_skip", "final_gain")
LARGE = ("w_in", "w_pool", "w_glu", "w_out", "w_ple", "w_ple_gate")
LARGE_AXIS = {"w_in": 1, "w_pool": 1, "w_glu": 1, "w_out": 0, "w_ple": 1, "w_ple_gate": 0}
WEIGHTS = ("norm_gain", "w_in", "w_pool", "pool_scale", "a_re", "a_im", "log_dt", "b_re", "b_im", "c_re", "c_im",
           "d_skip", "w_glu", "w_out", "w_ple", "w_ple_gate", "final_gain")


def kernel(x, p, norm_gain, w_in, w_pool, pool_scale, a_re, a_im, log_dt, b_re, b_im, c_re, c_im, d_skip, w_glu, w_out, w_ple, w_ple_gate, final_gain, loss_target, m_norm_gain, m_w_in, m_w_pool, m_pool_scale, m_a_re, m_a_im, m_log_dt, m_b_re, m_b_im, m_c_re, m_c_im, m_d_skip, m_w_glu, m_w_out, m_w_ple, m_w_ple_gate, m_final_gain, v_norm_gain, v_w_in, v_w_pool, v_pool_scale, v_a_re, v_a_im, v_log_dt, v_b_re, v_b_im, v_c_re, v_c_im, v_d_skip, v_w_glu, v_w_out, v_w_ple, v_w_ple_gate, v_final_gain):
    weights = dict(norm_gain=norm_gain, w_in=w_in, w_pool=w_pool, pool_scale=pool_scale, a_re=a_re, a_im=a_im,
                   log_dt=log_dt, b_re=b_re, b_im=b_im, c_re=c_re, c_im=c_im, d_skip=d_skip, w_glu=w_glu,
                   w_out=w_out, w_ple=w_ple, w_ple_gate=w_ple_gate, final_gain=final_gain)
    mom_m = dict(norm_gain=m_norm_gain, w_in=m_w_in, w_pool=m_w_pool, pool_scale=m_pool_scale, a_re=m_a_re,
                 a_im=m_a_im, log_dt=m_log_dt, b_re=m_b_re, b_im=m_b_im, c_re=m_c_re, c_im=m_c_im,
                 d_skip=m_d_skip, w_glu=m_w_glu, w_out=m_w_out, w_ple=m_w_ple, w_ple_gate=m_w_ple_gate,
                 final_gain=m_final_gain)
    mom_v = dict(norm_gain=v_norm_gain, w_in=v_w_in, w_pool=v_w_pool, pool_scale=v_pool_scale, a_re=v_a_re,
                 a_im=v_a_im, log_dt=v_log_dt, b_re=v_b_re, b_im=v_b_im, c_re=v_c_re, c_im=v_c_im,
                 d_skip=v_d_skip, w_glu=v_w_glu, w_out=v_w_out, w_ple=v_w_ple, w_ple_gate=v_w_ple_gate,
                 final_gain=v_final_gain)

    t = x.shape[1]
    xs = x.reshape(t, D_MODEL)
    ps = p.reshape(t, PLE_DIM)
    target = loss_target.reshape(t, D_MODEL)
    gain1 = norm_gain.reshape(1, D_MODEL)
    gain_f = final_gain.reshape(1, D_MODEL)
    scale_p = pool_scale.reshape(1, POOL_WIDTH)
    skip = d_skip.reshape(1, SSM_WIDTH)

    shard2d = {k: weights[k][0] for k in LARGE}
    shard_bf = {k: shard2d[k].astype(BF16) for k in LARGE}
    full = {"w_in": _all_gather("w_in_all_gather", [shard_bf["w_in"]], [LARGE_AXIS["w_in"]])[0]}
    later = [k for k in LARGE if k != "w_in"]
    later_axes = [LARGE_AXIS[k] for k in later]
    gather, gather_token = _gather_start("weights_gather_start", [shard_bf[k] for k in later], later_axes,
                                         full["w_in"])

    def arrive(k, after):
        i = later.index(k)
        full[k] = _gather_wait("gather_wait_" + k, gather, [i], [later_axes[i]], after)[0]

    ar, ai = a_re[0], a_im[0]
    ldt = log_dt.reshape(N_SSM_GROUPS, 1)
    br_t = jnp.transpose(b_re[0], (0, 2, 1))
    bi_t = jnp.transpose(b_im[0], (0, 2, 1))
    ab_re, ab_im, bb_re, bb_im = _ssm_params(ar, ai, ldt, br_t, bi_t)
    tiles = (SSM_CHUNKS, CHUNK_TILES, LANES)
    abar = jnp.concatenate([ab_re.reshape(tiles), ab_im.reshape(tiles)], axis=-1)
    w_pair = _compact_pair(bb_re, bb_im)
    c_pair = _compact_pair(c_re[0], -c_im[0])

    hn = _norm1_fwd(xs, gain1)
    proj = _mm_nn("in_proj", hn, full["w_in"], [F32], tk=2048, after=[gather_token])[0]
    pooled = _pool_fwd(proj)
    tm = _tile(t, 1024)
    arrive("w_pool", pooled)
    mixed = _mm("pool_mix", [(pooled, (tm, POOL_GROUP), lambda i, j, s: (i, j),
                              full["w_pool"], (None, POOL_GROUP, POOL_GROUP), lambda i, j, s: (j, 0, 0))],
                DOT_NN, (t // tm, N_POOL_GROUPS, 1),
                [((t, POOL_WIDTH), F32, (tm, POOL_GROUP), lambda i, j, s: (i, j))], 1)[0]
    y, gel, states = _ssm_fwd(proj, w_pair, c_pair, abar, skip)
    arrive("w_glu", gel)
    hg = _mm_nn("glu_proj", gel, full["w_glu"], [F32])[0]
    cat = _gate_fwd(mixed, proj, hg, scale_p)

    def residual_epilogue(acc, ex, out_refs):
        h = acc + ex[0][...]
        out_refs[0][...] = h
        out_refs[1][...] = h.astype(BF16)

    arrive("w_out", cat)
    h1, h1b = _mm_nn("out_proj", cat, full["w_out"], [F32, BF16], extras=[xs], epilogue=residual_epilogue)
    arrive("w_ple", h1b)
    arrive("w_ple_gate", h1b)
    de, dq, dh2, g_final_gain, loss_part = _ple_final(h1, h1b, ps, full["w_ple_gate"], full["w_ple"], target, gain_f)

    grads = {}
    grads["w_ple_gate"] = _mm_tn("ple_gate_wgrad", h1b, dq, BF16)
    grads["w_ple"] = _mm_tn("ple_wgrad", ps, de, BF16)
    sent, tokens = {}, {}

    def send(names):
        sent[names], tokens[names[0]] = _exchange_start(
            "grads_start_" + names[0], [grads[k] for k in names], [LARGE_AXIS[k] for k in names])

    send(("w_ple_gate", "w_ple"))
    dh1, dh1b = _mm_nt("ple_gate_dgrad", dq, full["w_ple_gate"], [F32, BF16], extras=[dh2],
                       epilogue=residual_epilogue)
    grads["w_out"] = _mm_tn("out_wgrad", cat, dh1b, BF16)
    send(("w_out",))
    dcat = _mm_nt("out_dgrad", dh1b, full["w_out"], [F32], tk=2048, after=[tokens["w_ple_gate"], tokens["w_out"]])[0]
    dmixed, dproj, dhg, g_pool_scale = _gate_bwd(dcat, mixed, proj, hg, scale_p)

    tk = _tile(t, 1024)
    grads["w_pool"] = _mm("pool_wgrad", [(pooled, (tk, POOL_GROUP), lambda i, j, s: (s, i),
                                          dmixed, (tk, POOL_GROUP), lambda i, j, s: (s, i))],
                          DOT_TN, (N_POOL_GROUPS, 1, t // tk),
                          [((N_POOL_GROUPS, POOL_GROUP, POOL_GROUP), BF16, (None, POOL_GROUP, POOL_GROUP),
                            lambda i, j, s: (i, 0, 0))], t // tk)[0]
    dpooled = _mm("pool_dgrad", [(dmixed, (tm, POOL_GROUP), lambda i, j, s: (i, j),
                                  full["w_pool"], (None, POOL_GROUP, POOL_GROUP), lambda i, j, s: (j, 0, 0))],
                  DOT_NT, (t // tm, N_POOL_GROUPS, 1),
                  [((t, POOL_WIDTH), F32, (tm, POOL_GROUP), lambda i, j, s: (i, j))], 1)[0]
    dproj = _pool_bwd(dpooled, dproj)

    grads["w_glu"] = _mm_tn("glu_wgrad", gel, dhg, BF16)
    send(("w_pool", "w_glu"))

    def gelu_bwd_epilogue(acc, ex, out_refs):
        yv = ex[0][...]
        th = jnp.tanh(GELU_C * (yv + GELU_A * yv * yv * yv))
        dgelu = 0.5 * (1.0 + th) + 0.5 * yv * (1.0 - th * th) * GELU_C * (1.0 + 3.0 * GELU_A * yv * yv)
        out_refs[0][...] = acc * dgelu

    dy = _mm_nt("glu_dgrad", dhg, full["w_glu"], [F32], tk=2048, extras=[y], epilogue=gelu_bwd_epilogue,
                after=[tokens["w_pool"]])[0]
    dproj, g_c_pair, g_w_pair, g_abar, g_d_skip = _ssm_bwd(dy, proj, states, w_pair, c_pair, abar, skip, dproj)

    g_ab_re = g_abar[..., :LANES].reshape(N_SSM_GROUPS, SSM_STATE)
    g_ab_im = g_abar[..., LANES:].reshape(N_SSM_GROUPS, SSM_STATE)
    d_ar, d_ai, d_ldt, d_br_t, d_bi_t = _ssm_params_bwd(
        ar, ai, ldt, br_t, bi_t, g_ab_re, g_ab_im,
        _expand_grad(g_w_pair[..., :LANES]), _expand_grad(g_w_pair[..., LANES:]))

    small_grads = dict(
        pool_scale=g_pool_scale, a_re=d_ar, a_im=d_ai, log_dt=d_ldt.reshape(1, N_SSM_GROUPS),
        b_re=d_br_t.astype(BF16), b_im=d_bi_t.astype(BF16), c_re=_expand_grad(g_c_pair[..., :LANES]).astype(BF16),
        c_im=(-_expand_grad(g_c_pair[..., LANES:])).astype(BF16), d_skip=g_d_skip, final_gain=g_final_gain)
    early = [k for k in SMALL if k != "norm_gain"]
    early_sent, early_token = _gather_start(
        "small_grads_start", [small_grads[k][None] for k in early] + [jnp.broadcast_to(loss_part, (1, 1, LANES))],
        [0] * (len(early) + 1), d_ar)

    grads["w_in"] = _mm_tn("in_wgrad", hn, dproj, BF16, after=[early_token])
    send(("w_in",))
    dhn = _mm_nt("in_dgrad", dproj, full["w_in"], [F32], tk=2048, after=[tokens["w_in"]])[0]
    grad_x, g_norm_gain = _norm1_bwd(xs, dhn, dh1, gain1)
    late_sent, late_token = _gather_start("norm_gain_grad_start", [g_norm_gain[None]], [0], g_norm_gain)

    out_g, out_d, out_m, out_v = ({} for _ in range(4))
    me = 4 * lax.axis_index("x") + 2 * lax.axis_index("y") + lax.axis_index("c")
    after = late_token
    for names, started in sent.items():
        axes = [LARGE_AXIS[k] for k in names]
        partials, landed = _exchange_wait("grads_wait_" + names[0], started, axes, after)
        for k, axis, partial, land in zip(names, axes, partials, landed):
            shard_shape = shard2d[k].shape
            size = shard_shape[axis]
            own = lax.dynamic_slice_in_dim(partial, me * size, size, axis=axis)
            view = (-1, shard_shape[-1])
            rows = math.prod(shard_shape[:-1])
            res = _adamw("adamw_" + k, shard2d[k].reshape(view), mom_m[k][0].reshape(view), mom_v[k][0].reshape(view),
                         [own.reshape(view), land.reshape(N_PEERS, rows, shard_shape[-1])])
            out_g[k], out_d[k], out_m[k], out_v[k] = (r.reshape(weights[k].shape) for r in res)
            after = res[0]

    def b_view(a):
        return jnp.transpose(a[0], (0, 2, 1))

    views = dict(norm_gain=lambda a: a, pool_scale=lambda a: a, a_re=lambda a: a[0], a_im=lambda a: a[0],
                 log_dt=lambda a: a, b_re=b_view, b_im=b_view, c_re=lambda a: a[0], c_im=lambda a: a[0],
                 d_skip=lambda a: a, final_gain=lambda a: a.reshape(1, D_MODEL))
    landed = _gather_wait("small_grads_wait", early_sent, list(range(len(early) + 1)), [0] * (len(early) + 1), after)
    stack = dict(zip(early, landed))
    stack["norm_gain"] = _gather_wait("norm_gain_grad_wait", late_sent, [0], [0], after)[0]
    *small_out, loss_row = _adamw_small(
        [views[k](weights[k]) for k in SMALL], [views[k](mom_m[k]) for k in SMALL],
        [views[k](mom_v[k]) for k in SMALL], [stack[k] for k in SMALL], landed[-1])
    loss = loss_row[0, 0]
    for out, res in zip((out_g, out_d, out_m, out_v), small_out):
        for k, r in zip(SMALL, res):
            if k in ("b_re", "b_im"):
                r = jnp.transpose(r, (0, 2, 1))
            out[k] = r.reshape(weights[k].shape)

    return (loss, grad_x.reshape(x.shape), *[out_g[k] for k in WEIGHTS], *[out_d[k] for k in WEIGHTS],
            *[out_m[k] for k in WEIGHTS], *[out_v[k] for k in WEIGHTS])
```

```python
import math

import jax
import jax.numpy as jnp
from jax import lax
from jax.experimental import pallas as pl
from jax.experimental.pallas import tpu as pltpu

F32 = jnp.float32
BF16 = jnp.bfloat16
MESH = pl.DeviceIdType.MESH
MESH_AXES = ("x", "y", "c")
N_DEV = 8

D_MODEL = 2048
POOL_WIDTH = 1024
SSM_WIDTH = 1024
N_POOL_GROUPS = 4
POOL_GROUP = 256
SSM_GROUP = 16
N_SSM_GROUPS = 64
SSM_STATE = 64
SSM_FLAT = N_SSM_GROUPS * SSM_STATE
SSM_CHUNKS = 4
CHUNK_IN = SSM_WIDTH // SSM_CHUNKS
CHUNK_STATE = SSM_FLAT // SSM_CHUNKS
PLE_DIM = 256
EPS = 1e-6
A_RE_MAX = -1e-4
ADAM_LR = 0.001
ADAM_B1 = 0.9
ADAM_B2 = 0.999
ADAM_EPS = 1e-08
ADAM_WD = 0.01
ADAM_STEP = 10
GELU_C = math.sqrt(2.0 / math.pi)
GELU_A = 0.044715

SUBLANES = 8
LANES = 128
VMEM_LIMIT_BYTES = 48 * 1024 * 1024

DOT_NN = (((1,), (0,)), ((), ()))
DOT_NT = (((1,), (1,)), ((), ()))
DOT_TN = (((0,), (0,)), ((), ()))


def _tile(n, pref):
    return pref if n % pref == 0 else n


def _params(sem):
    return pltpu.CompilerParams(dimension_semantics=sem, vmem_limit_bytes=VMEM_LIMIT_BYTES)


def _sigmoid(v):
    return 1.0 / (1.0 + jnp.exp(-v))


def _silu_and_grad(v):
    s = _sigmoid(v)
    return v * s, s * (1.0 + v * (1.0 - s))


def _mm(name, pairs, dims, grid, outs, k_steps, extras=(), epilogue=None):
    n_pairs, n_ex, n_out = len(pairs), len(extras), len(outs)
    acc_shape = tuple(d for d in outs[0][2] if d is not None)
    if epilogue is None:
        def epilogue(acc, ex, out_refs):
            out_refs[0][...] = acc.astype(out_refs[0].dtype)

    def body(*refs):
        ab = refs[:2 * n_pairs]
        ex = refs[2 * n_pairs:2 * n_pairs + n_ex]
        out_refs = refs[2 * n_pairs + n_ex:2 * n_pairs + n_ex + n_out]
        acc = refs[-1]
        k = pl.program_id(2)

        @pl.when(k == 0)
        def _():
            acc[...] = jnp.zeros_like(acc)

        part = None
        for q in range(n_pairs):
            d = lax.dot_general(ab[2 * q][...].astype(BF16), ab[2 * q + 1][...].astype(BF16), dims,
                                preferred_element_type=F32)
            part = d if part is None else part + d
        acc[...] += part

        @pl.when(k == k_steps - 1)
        def _():
            epilogue(acc[...], ex, out_refs)

    in_specs, operands = [], []
    for a, a_blk, a_map, b, b_blk, b_map in pairs:
        in_specs += [pl.BlockSpec(a_blk, a_map), pl.BlockSpec(b_blk, b_map)]
        operands += [a, b]
    for e, e_blk, e_map in extras:
        in_specs.append(pl.BlockSpec(e_blk, e_map))
        operands.append(e)
    return pl.pallas_call(
        body, name=name, grid=grid, in_specs=in_specs,
        out_specs=[pl.BlockSpec(o[2], o[3]) for o in outs],
        out_shape=[jax.ShapeDtypeStruct(o[0], o[1]) for o in outs],
        scratch_shapes=[pltpu.VMEM(acc_shape, F32)],
        compiler_params=_params(("arbitrary", "arbitrary", "arbitrary")),
    )(*operands)


def _after(tokens):
    return [(tok, tok.shape, lambda i, j, s: (0, 0)) for tok in tokens]


def _mm_nn(name, a, b, out_dtypes, tm=1024, tn=1024, tk=1024, a_col0=0, extras=(), epilogue=None, after=()):
    m, n = a.shape[0], b.shape[1]
    k = b.shape[0]
    tm, tn, tk = _tile(m, tm), _tile(n, tn), _tile(k, tk)
    outs = [((m, n), dt, (tm, tn), lambda i, j, s: (i, j)) for dt in out_dtypes]
    ex = [(e, (tm, tn), lambda i, j, s: (i, j)) for e in extras] + _after(after)
    return _mm(name, [(a, (tm, tk), lambda i, j, s: (i, a_col0 + s), b, (tk, tn), lambda i, j, s: (s, j))],
               DOT_NN, (m // tm, n // tn, k // tk), outs, k // tk, ex, epilogue)


def _mm_nt(name, a, b, out_dtypes, tm=1024, tn=1024, tk=1024, extras=(), epilogue=None, after=()):
    m, kk = a.shape
    n = b.shape[0]
    tm, tn, tk = _tile(m, tm), _tile(n, tn), _tile(kk, tk)
    outs = [((m, n), dt, (tm, tn), lambda i, j, s: (i, j)) for dt in out_dtypes]
    ex = [(e, (tm, tn), lambda i, j, s: (i, j)) for e in extras] + _after(after)
    return _mm(name, [(a, (tm, tk), lambda i, j, s: (i, s), b, (tn, tk), lambda i, j, s: (j, s))],
               DOT_NT, (m // tm, n // tn, kk // tk), outs, kk // tk, ex, epilogue)


def _mm_tn(name, a, b, out_dtype, tm=512, tn=2048, tk=1024, after=()):
    m, kk = a.shape
    n = b.shape[1]
    tm, tn, tk = _tile(kk, tm), _tile(n, tn), _tile(m, tk)
    outs = [((kk, n), out_dtype, (tm, tn), lambda i, j, s: (i, j))]
    return _mm(name, [(a, (tk, tm), lambda i, j, s: (s, i), b, (tk, tn), lambda i, j, s: (s, j))],
               DOT_TN, (kk // tm, n // tn, m // tk), outs, m // tk, _after(after))[0]


def _norm1_fwd(x, gain):
    t = x.shape[0]
    tm = _tile(t, 512)

    def body(x_ref, g_ref, hn_ref):
        xv = x_ref[...]
        r = lax.rsqrt(jnp.mean(xv * xv, axis=-1, keepdims=True) + EPS)
        hn_ref[...] = (xv * r * g_ref[...]).astype(BF16)

    return pl.pallas_call(
        body, name="norm1_fwd", grid=(t // tm,),
        in_specs=[pl.BlockSpec((tm, D_MODEL), lambda i: (i, 0)), pl.BlockSpec((1, D_MODEL), lambda i: (0, 0))],
        out_specs=pl.BlockSpec((tm, D_MODEL), lambda i: (i, 0)),
        out_shape=jax.ShapeDtypeStruct((t, D_MODEL), BF16),
        compiler_params=_params(("arbitrary",)),
    )(x, gain)


ROW_TILE = 256


def _in_dgrad_norm1_bwd(dproj, w_in, x, dh1, gain, after):
    t = x.shape[0]
    tm = _tile(t, ROW_TILE)

    def body(dp_ref, w_ref, x_ref, dh1_ref, g_ref, _, dx_ref, gg_ref):
        @pl.when(pl.program_id(0) == 0)
        def _():
            gg_ref[...] = jnp.zeros_like(gg_ref)

        dhn = lax.dot_general(dp_ref[...], w_ref[...], DOT_NT, preferred_element_type=F32)
        xv = x_ref[...]
        r = lax.rsqrt(jnp.mean(xv * xv, axis=-1, keepdims=True) + EPS)
        xh = xv * r
        gg_ref[...] += jnp.sum(dhn * xh, axis=0, keepdims=True)
        dxh = dhn * g_ref[...]
        dx_ref[...] = dh1_ref[...] + r * (dxh - xh * jnp.mean(dxh * xh, axis=-1, keepdims=True))

    row = pl.BlockSpec((tm, D_MODEL), lambda i: (i, 0))
    vec = pl.BlockSpec((1, D_MODEL), lambda i: (0, 0))
    return pl.pallas_call(
        body, name="in_dgrad_norm1_bwd", grid=(t // tm,),
        in_specs=[pl.BlockSpec((tm, dproj.shape[1]), lambda i: (i, 0)), _resident(w_in.shape), row, row, vec,
                  pl.BlockSpec(after.shape, lambda i: (0, 0))],
        out_specs=[row, vec],
        out_shape=[jax.ShapeDtypeStruct((t, D_MODEL), F32), jax.ShapeDtypeStruct((1, D_MODEL), F32)],
        compiler_params=_params(("arbitrary",)),
    )(dproj, w_in, x, dh1, gain, after)


def _pool_counts(t, width, group):
    row = lax.broadcasted_iota(jnp.int32, (t, width), 0)
    window = jnp.left_shift(jnp.int32(2), group)
    return row, jnp.minimum(row + 1, window).astype(F32)


def _select_window(group, s2, s4, s8, s16):
    return jnp.where(group == 0, s2, jnp.where(group == 1, s4, jnp.where(group == 2, s8, s16)))


def _pool_fwd(proj):
    t = proj.shape[0]
    tc = LANES

    def body(u_ref, o_ref):
        group = pl.program_id(0) // (POOL_GROUP // tc)
        v = u_ref[...]
        row, count = _pool_counts(t, tc, group)

        def down(a, j):
            return jnp.where(row >= j, pltpu.roll(a, j, 0), 0.0)

        s2 = v + down(v, 1)
        s4 = s2 + down(s2, 2)
        s8 = s4 + down(s4, 4)
        s16 = s8 + down(s8, 8)
        o_ref[...] = (_select_window(group, s2, s4, s8, s16) / count - v).astype(BF16)

    return pl.pallas_call(
        body, name="pool_fwd", grid=(POOL_WIDTH // tc,),
        in_specs=[pl.BlockSpec((t, tc), lambda j: (0, j))],
        out_specs=pl.BlockSpec((t, tc), lambda j: (0, j)),
        out_shape=jax.ShapeDtypeStruct((t, POOL_WIDTH), BF16),
        compiler_params=_params(("arbitrary",)),
    )(proj)


def _pool_bwd(dpooled, dproj):
    t = dpooled.shape[0]
    tc = LANES

    def body(d_ref, _, o_ref):
        group = pl.program_id(0) // (POOL_GROUP // tc)
        dp = d_ref[...]
        row, count = _pool_counts(t, tc, group)
        r = dp / count

        def up(a, j):
            return jnp.where(row < t - j, pltpu.roll(a, t - j, 0), 0.0)

        s2 = r + up(r, 1)
        s4 = s2 + up(s2, 2)
        s8 = s4 + up(s4, 4)
        s16 = s8 + up(s8, 8)
        o_ref[...] = (_select_window(group, s2, s4, s8, s16) - dp).astype(BF16)

    return pl.pallas_call(
        body, name="pool_bwd", grid=(POOL_WIDTH // tc,),
        in_specs=[pl.BlockSpec((t, tc), lambda j: (0, j)), pl.BlockSpec(memory_space=pl.ANY)],
        out_specs=pl.BlockSpec((t, tc), lambda j: (0, j)),
        out_shape=jax.ShapeDtypeStruct(dproj.shape, dproj.dtype),
        input_output_aliases={1: 0},
        compiler_params=_params(("arbitrary",)),
    )(dpooled, dproj)


def _gate_fwd(mixed, proj, hg, pool_scale):
    t = mixed.shape[0]
    tm = _tile(t, 512)

    def body(mx_ref, ga_ref, gb_ref, hg_ref, ps_ref, cat_ref):
        silu_a, _ = _silu_and_grad(ga_ref[...])
        cat_ref[:, :POOL_WIDTH] = (mx_ref[...] * ps_ref[...] * silu_a).astype(BF16)
        silu_b, _ = _silu_and_grad(gb_ref[...])
        sb = hg_ref[:, :SSM_WIDTH] * _sigmoid(hg_ref[:, SSM_WIDTH:])
        cat_ref[:, POOL_WIDTH:] = (sb * silu_b).astype(BF16)

    return pl.pallas_call(
        body, name="gate_fwd", grid=(t // tm,),
        in_specs=[pl.BlockSpec((tm, POOL_WIDTH), lambda i: (i, 0)),
                  pl.BlockSpec((tm, POOL_WIDTH), lambda i: (i, 1)),
                  pl.BlockSpec((tm, SSM_WIDTH), lambda i: (i, 3)),
                  pl.BlockSpec((tm, 2 * SSM_WIDTH), lambda i: (i, 0)),
                  pl.BlockSpec((1, POOL_WIDTH), lambda i: (0, 0))],
        out_specs=pl.BlockSpec((tm, D_MODEL), lambda i: (i, 0)),
        out_shape=jax.ShapeDtypeStruct((t, D_MODEL), BF16),
        compiler_params=_params(("arbitrary",)),
    )(mixed, proj, proj, hg, pool_scale)


def _out_dgrad_gate_bwd(dh1b, w_out, mixed, proj, hg, pool_scale, after):
    t = mixed.shape[0]
    tm = _tile(t, ROW_TILE)
    n_after = len(after)

    def body(dh_ref, w_ref, mx_ref, ga_ref, gb_ref, hg_ref, ps_ref, *rest):
        dmx_ref, dp_ref, dhg_ref, gps_ref = rest[n_after:]

        @pl.when(pl.program_id(0) == 0)
        def _():
            gps_ref[...] = jnp.zeros_like(gps_ref)

        dcat = lax.dot_general(dh_ref[...], w_ref[...], DOT_NT, preferred_element_type=F32)
        ps = ps_ref[...]
        mx = mx_ref[...]
        dya = dcat[:, :POOL_WIDTH]
        silu_a, dsilu_a = _silu_and_grad(ga_ref[...])
        dpa = dya * silu_a
        gps_ref[...] += jnp.sum(dpa * mx, axis=0, keepdims=True)
        dmx_ref[...] = (dpa * ps).astype(BF16)
        dp_ref[:, :POOL_WIDTH] = jnp.zeros((tm, POOL_WIDTH), BF16)
        dp_ref[:, POOL_WIDTH:2 * POOL_WIDTH] = (dya * mx * ps * dsilu_a).astype(BF16)

        dyb = dcat[:, POOL_WIDTH:]
        silu_b, dsilu_b = _silu_and_grad(gb_ref[...])
        h_a = hg_ref[:, :SSM_WIDTH]
        sg = _sigmoid(hg_ref[:, SSM_WIDTH:])
        dsb = dyb * silu_b
        dp_ref[:, 2 * POOL_WIDTH:2 * POOL_WIDTH + SSM_WIDTH] = jnp.zeros((tm, SSM_WIDTH), BF16)
        dp_ref[:, 2 * POOL_WIDTH + SSM_WIDTH:] = (dyb * h_a * sg * dsilu_b).astype(BF16)
        dhg_ref[:, :SSM_WIDTH] = (dsb * sg).astype(BF16)
        dhg_ref[:, SSM_WIDTH:] = (dsb * h_a * sg * (1.0 - sg)).astype(BF16)

    half = pl.BlockSpec((tm, POOL_WIDTH), lambda i: (i, 0))
    full = pl.BlockSpec((tm, D_MODEL), lambda i: (i, 0))
    vec = pl.BlockSpec((1, POOL_WIDTH), lambda i: (0, 0))
    proj_width = 2 * POOL_WIDTH + 2 * SSM_WIDTH
    return pl.pallas_call(
        body, name="out_dgrad_gate_bwd", grid=(t // tm,),
        in_specs=[full, _resident(w_out.shape), half,
                  pl.BlockSpec((tm, POOL_WIDTH), lambda i: (i, 1)),
                  pl.BlockSpec((tm, SSM_WIDTH), lambda i: (i, 3)),
                  full, vec] + [pl.BlockSpec(tok.shape, lambda i: (0, 0)) for tok in after],
        out_specs=[half, pl.BlockSpec((tm, proj_width), lambda i: (i, 0)), full, vec],
        out_shape=[jax.ShapeDtypeStruct((t, POOL_WIDTH), BF16), jax.ShapeDtypeStruct((t, proj_width), BF16),
                   jax.ShapeDtypeStruct((t, 2 * SSM_WIDTH), BF16),
                   jax.ShapeDtypeStruct((1, POOL_WIDTH), F32)],
        compiler_params=_params(("arbitrary",)),
    )(dh1b, w_out, mixed, proj, proj, hg, pool_scale, *after)


def _ple_final(h1, h1b, p, w_gate, w_ple, target, gain):
    t = h1.shape[0]
    tm = _tile(t, 256)

    def body(h1_ref, h1b_ref, p_ref, wg_ref, wp_ref, tg_ref, g_ref, de_ref, dq_ref, dh2_ref, gg_ref, loss_ref):
        @pl.when(pl.program_id(0) == 0)
        def _():
            gg_ref[...] = jnp.zeros_like(gg_ref)
            loss_ref[...] = jnp.zeros_like(loss_ref)

        ev = jnp.dot(p_ref[...].astype(BF16), wp_ref[...], preferred_element_type=F32)
        sg = _sigmoid(jnp.dot(h1b_ref[...], wg_ref[...], preferred_element_type=F32))
        h2 = h1_ref[...] + ev * sg
        r = lax.rsqrt(jnp.mean(h2 * h2, axis=-1, keepdims=True) + EPS)
        n = h2 * r
        gain_v = g_ref[...]
        diff = n * gain_v - tg_ref[...]
        row_loss = jnp.sum(diff * diff, axis=-1, keepdims=True)
        loss_ref[...] += (0.5 / D_MODEL) * jnp.sum(row_loss, axis=0, keepdims=True)
        dout = diff * (1.0 / D_MODEL)
        gg_ref[...] += jnp.sum(dout * n, axis=0, keepdims=True)
        dn = dout * gain_v
        dh2 = r * (dn - n * jnp.mean(dn * n, axis=-1, keepdims=True))
        dh2_ref[...] = dh2
        de_ref[...] = (dh2 * sg).astype(BF16)
        dq_ref[...] = (dh2 * ev * sg * (1.0 - sg)).astype(BF16)

    row = pl.BlockSpec((tm, D_MODEL), lambda i: (i, 0))
    vec = pl.BlockSpec((1, D_MODEL), lambda i: (0, 0))
    return pl.pallas_call(
        body, name="ple_final", grid=(t // tm,),
        in_specs=[row, row, pl.BlockSpec((tm, PLE_DIM), lambda i: (i, 0)), _resident((D_MODEL, D_MODEL)),
                  _resident((PLE_DIM, D_MODEL)), row, vec],
        out_specs=[row, row, row, vec, pl.BlockSpec((1, 1), lambda i: (0, 0))],
        out_shape=[jax.ShapeDtypeStruct((t, D_MODEL), BF16), jax.ShapeDtypeStruct((t, D_MODEL), BF16),
                   jax.ShapeDtypeStruct((t, D_MODEL), F32), jax.ShapeDtypeStruct((1, D_MODEL), F32),
                   jax.ShapeDtypeStruct((1, 1), F32)],
        compiler_params=_params(("arbitrary",)),
    )(h1, h1b, p, w_gate, w_ple, target, gain)


def _zoh(a_re, a_im, log_dt, b_re_t, b_im_t):
    lam_re = jnp.minimum(a_re, A_RE_MAX)
    lam_im = a_im
    dt = jnp.exp(log_dt)
    mag = jnp.exp(lam_re * dt)
    ang = lam_im * dt
    ab_re = mag * jnp.cos(ang)
    ab_im = mag * jnp.sin(ang)
    den = lam_re * lam_re + lam_im * lam_im
    n_re = ab_re - 1.0
    n_im = ab_im
    q_re = (n_re * lam_re + n_im * lam_im) / den
    q_im = (n_im * lam_re - n_re * lam_im) / den
    bb_re = q_re[:, None, :] * b_re_t - q_im[:, None, :] * b_im_t
    bb_im = q_re[:, None, :] * b_im_t + q_im[:, None, :] * b_re_t
    return ab_re, ab_im, bb_re, bb_im


def _ssm_params(a_re, a_im, log_dt, b_re_t, b_im_t):
    def body(are_ref, aim_ref, dt_ref, bre_ref, bim_ref, abre_ref, abim_ref, bbre_ref, bbim_ref):
        ab_re, ab_im, bb_re, bb_im = _zoh(are_ref[...], aim_ref[...], dt_ref[...], bre_ref[...], bim_ref[...])
        abre_ref[...] = ab_re
        abim_ref[...] = ab_im
        bbre_ref[...] = bb_re
        bbim_ref[...] = bb_im

    return pl.pallas_call(
        body, name="ssm_params",
        out_shape=[jax.ShapeDtypeStruct(a_re.shape, F32), jax.ShapeDtypeStruct(a_re.shape, F32),
                   jax.ShapeDtypeStruct(b_re_t.shape, F32), jax.ShapeDtypeStruct(b_re_t.shape, F32)],
        compiler_params=_params(None),
    )(a_re, a_im, log_dt, b_re_t, b_im_t)


def _ssm_params_bwd(a_re, a_im, log_dt, b_re_t, b_im_t, g_ab_re, g_ab_im, g_bb_re, g_bb_im):
    def body(are_ref, aim_ref, dt_ref, bre_ref, bim_ref, gar_ref, gai_ref, gbr_ref, gbi_ref,
             o_are, o_aim, o_dt, o_bre, o_bim):
        _, vjp = jax.vjp(_zoh, are_ref[...], aim_ref[...], dt_ref[...], bre_ref[...], bim_ref[...])
        d_are, d_aim, d_dt, d_bre, d_bim = vjp((gar_ref[...], gai_ref[...], gbr_ref[...], gbi_ref[...]))
        o_are[...] = d_are
        o_aim[...] = d_aim
        o_dt[...] = d_dt
        o_bre[...] = d_bre
        o_bim[...] = d_bim

    ins = (a_re, a_im, log_dt, b_re_t, b_im_t)
    return pl.pallas_call(
        body, name="ssm_params_bwd",
        out_shape=[jax.ShapeDtypeStruct(v.shape, F32) for v in ins],
        compiler_params=_params(None),
    )(*ins, g_ab_re, g_ab_im, g_bb_re, g_bb_im)


CHUNK_TILES = CHUNK_STATE // LANES
CH_PER_TILE = CHUNK_IN // CHUNK_TILES
PAIR = 2 * LANES
SSM_ROWS = 256
SCAN_STEPS = 8
U_COLUMN_BLOCK = 2 * POOL_WIDTH // SSM_WIDTH


def _own_half():
    r = lax.broadcasted_iota(jnp.int32, (CHUNK_IN, LANES), 0) // SSM_GROUP % 2
    c = lax.broadcasted_iota(jnp.int32, (CHUNK_IN, LANES), 1) // SSM_STATE
    return (r == c)[None]


def _compact_weight(w):
    tiled = jnp.tile(w.reshape(SSM_CHUNKS, CHUNK_IN, SSM_STATE), (1, 1, 2))
    return jnp.where(_own_half(), tiled, 0.0)


def _compact_pair(w_a, w_b):
    return jnp.concatenate([_compact_weight(w_a), _compact_weight(w_b)], axis=-1).astype(BF16)


def _expand_grad(g):
    kept = jnp.where(_own_half(), g, 0.0)
    return kept.reshape(SSM_CHUNKS, CHUNK_IN, 2, SSM_STATE).sum(axis=2).reshape(N_SSM_GROUPS, SSM_GROUP, SSM_STATE)


TILES_PER_BLOCK = LANES // CH_PER_TILE
IN_BLOCKS = CHUNK_IN // LANES


def _tile_masks():
    j = lax.broadcasted_iota(jnp.int32, (CHUNK_TILES, LANES), 0) % TILES_PER_BLOCK
    lane = lax.broadcasted_iota(jnp.int32, (CHUNK_TILES, LANES), 1) // CH_PER_TILE
    return (j == lane).astype(F32)


def _tile_rows(ref, j, tt):
    return ref.at[j // TILES_PER_BLOCK, pl.ds(j, tt, stride=CHUNK_TILES), :]


def _spread(ref, v, masks):
    tt = v.shape[0]
    for j in range(CHUNK_TILES):
        block = LANES * (j // TILES_PER_BLOCK)
        _tile_rows(ref, j, tt)[...] = v[:, block:block + LANES] * masks[j:j + 1, :]
    return jnp.concatenate([ref[b] for b in range(IN_BLOCKS)], axis=1).astype(BF16)


def _gather(ref, full, masks):
    tt = full.shape[0] // CHUNK_TILES
    for b in range(IN_BLOCKS):
        ref[b] = full[:, b * LANES:(b + 1) * LANES]
    out = []
    for b in range(IN_BLOCKS):
        acc = None
        for j in range(b * TILES_PER_BLOCK, (b + 1) * TILES_PER_BLOCK):
            part = _tile_rows(ref, j, tt)[...] * masks[j:j + 1, :]
            acc = part if acc is None else acc + part
        out.append(acc)
    return jnp.concatenate(out, axis=1)


def _resident(shape):
    return pl.BlockSpec(shape, lambda i: (0,) * len(shape), pipeline_mode=pl.Buffered(1))


def _halves(ref, k, rows=slice(None)):
    return ref[k, rows, :LANES], ref[k, rows, LANES:]


def _ssm_fwd(proj, w2, c2, a2, d_skip):
    t = proj.shape[0]
    tt = _tile(t, SSM_ROWS)
    rows = tt * CHUNK_TILES

    def body(u_ref, w_ref, c_ref, a_ref, d_ref, y_ref, gel_ref, s_ref, carry, spread_ref, full_ref):
        @pl.when(pl.program_id(0) == 0)
        def _():
            carry[...] = jnp.zeros_like(carry)
            spread_ref[...] = jnp.zeros_like(spread_ref)

        mask = _tile_masks()
        u = u_ref[...]
        for k in range(SSM_CHUNKS):
            uk = _spread(spread_ref, u[:, k * CHUNK_IN:(k + 1) * CHUNK_IN], mask)
            s_ref[k] = jnp.dot(uk, w_ref[k], preferred_element_type=F32)

        abar = [_halves(a_ref, k) for k in range(SSM_CHUNKS)]

        def steps(i, state):
            for v in range(SCAN_STEPS):
                r = pl.ds(pl.multiple_of((i * SCAN_STEPS + v) * CHUNK_TILES, CHUNK_TILES), CHUNK_TILES)
                new = []
                for k, ((a_re, a_im), (s_re, s_im)) in enumerate(zip(abar, state)):
                    b_re, b_im = _halves(s_ref, k, r)
                    s_re, s_im = a_re * s_re - a_im * s_im + b_re, a_re * s_im + a_im * s_re + b_im
                    s_ref[k, r, :LANES] = s_re
                    s_ref[k, r, LANES:] = s_im
                    new.append((s_re, s_im))
                state = tuple(new)
            return state

        state = lax.fori_loop(0, tt // SCAN_STEPS, steps, tuple(_halves(carry, k) for k in range(SSM_CHUNKS)))
        for k, (s_re, s_im) in enumerate(state):
            carry[k, :, :LANES] = s_re
            carry[k, :, LANES:] = s_im

        for k in range(SSM_CHUNKS):
            cols = slice(k * CHUNK_IN, (k + 1) * CHUNK_IN)
            full = lax.dot_general(s_ref[k].astype(BF16), c_ref[k], DOT_NT, preferred_element_type=F32)
            y = _gather(full_ref, full, mask) + d_ref[:, cols] * u[:, cols]
            y_ref[:, cols] = y
            gel_ref[:, cols] = (0.5 * y * (1.0 + jnp.tanh(GELU_C * (y + GELU_A * y * y * y)))).astype(BF16)

    weight = _resident((SSM_CHUNKS, CHUNK_IN, PAIR))
    tokens = pl.BlockSpec((tt, SSM_WIDTH), lambda i: (i, 0))
    return pl.pallas_call(
        body, name="ssm_fwd", grid=(t // tt,),
        in_specs=[pl.BlockSpec((tt, SSM_WIDTH), lambda i: (i, U_COLUMN_BLOCK)), weight, weight,
                  _resident((SSM_CHUNKS, CHUNK_TILES, PAIR)), _resident((1, SSM_WIDTH))],
        out_specs=[tokens, tokens, pl.BlockSpec((SSM_CHUNKS, rows, PAIR), lambda i: (0, i, 0))],
        out_shape=[jax.ShapeDtypeStruct((t, SSM_WIDTH), F32), jax.ShapeDtypeStruct((t, SSM_WIDTH), BF16),
                   jax.ShapeDtypeStruct((SSM_CHUNKS, t * CHUNK_TILES, PAIR), F32)],
        scratch_shapes=[pltpu.VMEM((SSM_CHUNKS, CHUNK_TILES, PAIR), F32), pltpu.VMEM((IN_BLOCKS, rows, LANES), F32),
                        pltpu.VMEM((IN_BLOCKS, rows, LANES), F32)],
        compiler_params=_params(("arbitrary",)),
    )(proj, w2, c2, a2, d_skip)


def _ssm_bwd(dy, proj, s, w2, c2, a2, d_skip, dproj):
    t = dy.shape[0]
    tt = _tile(t, SSM_ROWS)
    rows = tt * CHUNK_TILES
    n_chunks = t // tt

    def body(dy_ref, u_ref, s_ref, w_ref, c_ref, a_ref, d_ref, _, du_ref, gc_ref, gw_ref, ga_ref, gd_ref, z_ref, carry,
             spread_ref, full_ref):
        @pl.when(pl.program_id(0) == 0)
        def _():
            for r in (carry, gc_ref, gw_ref, ga_ref, gd_ref, spread_ref):
                r[...] = jnp.zeros_like(r)

        mask = _tile_masks()
        dy_v = dy_ref[...]
        u = u_ref[...]
        gd_ref[...] += jnp.sum(dy_v * u, axis=0, keepdims=True)
        for k in range(SSM_CHUNKS):
            dk = _spread(spread_ref, dy_v[:, k * CHUNK_IN:(k + 1) * CHUNK_IN], mask)
            z_ref[k] = jnp.dot(dk, c_ref[k], preferred_element_type=F32)
            gc_ref[k] += lax.dot_general(dk, s_ref[k].astype(BF16), DOT_TN, preferred_element_type=F32)

        abar = [_halves(a_ref, k) for k in range(SSM_CHUNKS)]

        def steps(i, state):
            zs, gs = state
            for v in range(SCAN_STEPS):
                tok = tt - 1 - (i * SCAN_STEPS + v)
                r = pl.ds(pl.multiple_of(tok * CHUNK_TILES, CHUNK_TILES), CHUNK_TILES)
                new_z, new_g = [], []
                for k, ((a_re, a_im), (z_re, z_im), (g_re, g_im)) in enumerate(zip(abar, zs, gs)):
                    s_re, s_im = _halves(s_ref, k, r)
                    g_re = g_re + z_re * s_re + z_im * s_im
                    g_im = g_im + z_im * s_re - z_re * s_im
                    d_re, d_im = _halves(z_ref, k, r)
                    z_re, z_im = d_re + a_re * z_re + a_im * z_im, d_im + a_re * z_im - a_im * z_re
                    z_ref[k, r, :LANES] = z_re
                    z_ref[k, r, LANES:] = z_im
                    new_z.append((z_re, z_im))
                    new_g.append((g_re, g_im))
                zs, gs = tuple(new_z), tuple(new_g)
            return zs, gs

        zs, gs = lax.fori_loop(0, tt // SCAN_STEPS, steps,
                               (tuple(_halves(carry, k) for k in range(SSM_CHUNKS)),
                                tuple(_halves(ga_ref, k) for k in range(SSM_CHUNKS))))
        for k in range(SSM_CHUNKS):
            carry[k, :, :LANES], carry[k, :, LANES:] = zs[k]
            ga_ref[k, :, :LANES], ga_ref[k, :, LANES:] = gs[k]

        for k in range(SSM_CHUNKS):
            cols = slice(k * CHUNK_IN, (k + 1) * CHUNK_IN)
            zb = z_ref[k].astype(BF16)
            full = lax.dot_general(zb, w_ref[k], DOT_NT, preferred_element_type=F32)
            du_ref[:, cols] = (_gather(full_ref, full, mask) + d_ref[:, cols] * dy_v[:, cols]).astype(BF16)
            uk = _spread(spread_ref, u[:, cols], mask)
            gw_ref[k] += lax.dot_general(uk, zb, DOT_TN, preferred_element_type=F32)

    weight = _resident((SSM_CHUNKS, CHUNK_IN, PAIR))
    tokens = pl.BlockSpec((tt, SSM_WIDTH), lambda i: (n_chunks - 1 - i, 0))
    grad = pl.BlockSpec((SSM_CHUNKS, CHUNK_IN, PAIR), lambda i: (0, 0, 0))
    return pl.pallas_call(
        body, name="ssm_bwd", grid=(n_chunks,),
        in_specs=[tokens, pl.BlockSpec((tt, SSM_WIDTH), lambda i: (n_chunks - 1 - i, U_COLUMN_BLOCK)),
                  pl.BlockSpec((SSM_CHUNKS, rows, PAIR), lambda i: (0, n_chunks - 1 - i, 0)), weight, weight,
                  _resident((SSM_CHUNKS, CHUNK_TILES, PAIR)), _resident((1, SSM_WIDTH)),
                  pl.BlockSpec(memory_space=pl.ANY)],
        out_specs=[pl.BlockSpec((tt, SSM_WIDTH), lambda i: (n_chunks - 1 - i, U_COLUMN_BLOCK)), grad, grad,
                   pl.BlockSpec((SSM_CHUNKS, CHUNK_TILES, PAIR), lambda i: (0, 0, 0)),
                   pl.BlockSpec((1, SSM_WIDTH), lambda i: (0, 0))],
        out_shape=[jax.ShapeDtypeStruct(dproj.shape, dproj.dtype), jax.ShapeDtypeStruct((SSM_CHUNKS, CHUNK_IN, PAIR), F32),
                   jax.ShapeDtypeStruct((SSM_CHUNKS, CHUNK_IN, PAIR), F32),
                   jax.ShapeDtypeStruct((SSM_CHUNKS, CHUNK_TILES, PAIR), F32), jax.ShapeDtypeStruct((1, SSM_WIDTH), F32)],
        input_output_aliases={7: 0},
        scratch_shapes=[pltpu.VMEM((SSM_CHUNKS, rows, PAIR), F32), pltpu.VMEM((SSM_CHUNKS, CHUNK_TILES, PAIR), F32),
                        pltpu.VMEM((IN_BLOCKS, rows, LANES), F32), pltpu.VMEM((IN_BLOCKS, rows, LANES), F32)],
        compiler_params=_params(("arbitrary",)),
    )(dy, proj, s, w2, c2, a2, d_skip, dproj)


def _block(ref, axis, size, index):
    idx = [slice(None)] * len(ref.shape)
    idx[axis] = pl.ds(pl.multiple_of(index * size, size), size)
    return ref.at[tuple(idx)]


def _all_gather(name, shards, axes):
    n = len(shards)
    sizes = [s.shape[a] for s, a in zip(shards, axes)]

    def body(*refs):
        ins, outs = refs[:n], refs[n:2 * n]
        send_sems, recv_sems, local_sems = refs[2 * n:]
        x, y, c = (lax.axis_index(a) for a in MESH_AXES)
        me, sibling = (x, y, c), (x, y, 1 - c)
        chips = [(1 - x, y), (x, 1 - y), (1 - x, 1 - y)]

        def rows(i, dev):
            return _block(outs[i], axes[i], sizes[i], 4 * dev[0] + 2 * dev[1] + dev[2])

        def copy(i, k, block, to, src=None):
            return pltpu.make_async_remote_copy(
                src_ref=rows(i, block) if src is None else src, dst_ref=rows(i, block),
                send_sem=send_sems.at[7 * i + k], recv_sem=recv_sems.at[7 * i + k],
                device_id=to, device_id_type=MESH)

        mine = [pltpu.make_async_copy(ins[i], rows(i, me), local_sems.at[i]) for i in range(n)]
        for cp in mine:
            cp.start()
        first = []
        for i in range(n):
            first.append(copy(i, 0, me, sibling, src=ins[i]))
            first += [copy(i, 1 + j, me, (*chip, c), src=ins[i]) for j, chip in enumerate(chips)]
        for cp in first:
            cp.start()
        passed = []
        for i in range(n):
            for j, chip in enumerate(chips):
                copy(i, 1 + j, (*chip, c), me).wait_recv()
                fwd = copy(i, 4 + j, (*chip, c), sibling)
                fwd.start()
                passed.append(fwd)
        for i in range(n):
            copy(i, 0, sibling, me).wait_recv()
            for j, chip in enumerate(chips):
                copy(i, 4 + j, (*chip, 1 - c), me).wait_recv()
        for cp in first + passed:
            cp.wait_send()
        for cp in mine:
            cp.wait()

    out_shape = []
    for s, a in zip(shards, axes):
        shape = list(s.shape)
        shape[a] *= N_DEV
        out_shape.append(jax.ShapeDtypeStruct(tuple(shape), s.dtype))
    any_spec = pl.BlockSpec(memory_space=pl.ANY)
    return pl.pallas_call(
        body, name=name, out_shape=out_shape,
        in_specs=[any_spec] * n, out_specs=[any_spec] * n,
        scratch_shapes=[pltpu.SemaphoreType.DMA((7 * n,)), pltpu.SemaphoreType.DMA((7 * n,)),
                        pltpu.SemaphoreType.DMA((n,))],
    )(*shards)


HBM_SPEC = pl.BlockSpec(memory_space=pltpu.HBM)
SEM_SPEC = pl.BlockSpec(memory_space=pltpu.SEMAPHORE)
ANY_SPEC = pl.BlockSpec(memory_space=pl.ANY)
SPLIT_PARAMS = pltpu.CompilerParams(has_side_effects=pltpu.SideEffectType.DATAFLOW_SIDE_EFFECTING)
N_PEERS = N_DEV - 1
TOKEN = jax.ShapeDtypeStruct((SUBLANES, LANES), F32)
VMEM_SPEC = pl.BlockSpec(memory_space=pltpu.VMEM)


def _in_hbm(arrays):
    return [pltpu.with_memory_space_constraint(a, pltpu.HBM) for a in arrays]


def _peer(m):
    x, y, c = (lax.axis_index(a) for a in MESH_AXES)
    px = 1 - x if m & 4 else x
    py = 1 - y if m & 2 else y
    pc = 1 - c if m & 1 else c
    return (px, py, pc), 4 * px + 2 * py + pc


def _my_index():
    x, y, c = (lax.axis_index(a) for a in MESH_AXES)
    return 4 * x + 2 * y + c


def _gather_copies(shard_refs, full_refs, axes, send_sems, recv_sems):
    copies = []
    for i, (shard, full) in enumerate(zip(shard_refs, full_refs)):
        mine = _block(full, axes[i], shard.shape[axes[i]], _my_index())
        for m in range(1, N_DEV):
            peer, _ = _peer(m)
            copies.append(pltpu.make_async_remote_copy(
                src_ref=shard, dst_ref=mine, send_sem=send_sems.at[N_PEERS * i + m - 1],
                recv_sem=recv_sems.at[N_PEERS * i + m - 1], device_id=peer, device_id_type=MESH))
    return copies


def _gather_start(name, shards, axes, after):
    n = len(shards)

    def body(*refs):
        shard_refs = refs[:n]
        send_sems, recv_sems, local_sems = refs[n + 1:n + 4]
        full_refs = refs[2 * n + 4:3 * n + 4]
        refs[3 * n + 4][...] = jnp.zeros(TOKEN.shape, TOKEN.dtype)
        for i in range(n):
            pltpu.make_async_copy(shard_refs[i], _block(full_refs[i], axes[i], shard_refs[i].shape[axes[i]], _my_index()),
                                  local_sems.at[i]).start()
        for cp in _gather_copies(shard_refs, full_refs, axes, send_sems, recv_sems):
            cp.start()

    fulls = []
    for s, a in zip(shards, axes):
        shape = list(s.shape)
        shape[a] *= N_DEV
        fulls.append(pltpu.HBM(tuple(shape), s.dtype))
    out = pl.pallas_call(
        body, name=name,
        out_shape=(pltpu.SemaphoreType.DMA((N_PEERS * n,)), pltpu.SemaphoreType.DMA((N_PEERS * n,)),
                   pltpu.SemaphoreType.DMA((n,)), *[pltpu.HBM(s.shape, s.dtype) for s in shards], *fulls, TOKEN),
        in_specs=[HBM_SPEC] * n + [ANY_SPEC],
        out_specs=(SEM_SPEC, SEM_SPEC, SEM_SPEC, *[HBM_SPEC] * (2 * n), VMEM_SPEC),
        input_output_aliases={i: 3 + i for i in range(n)},
        compiler_params=SPLIT_PARAMS,
    )(*_in_hbm(shards), after)
    return out[:-1], out[-1]


def _gather_wait(name, started, indices, axes, after):
    send_sems, recv_sems, local_sems = started[:3]
    n_all = (len(started) - 3) // 2
    shards = [started[3 + i] for i in indices]
    fulls = [started[3 + n_all + i] for i in indices]
    n = len(indices)

    def body(*refs):
        shard_refs, full_refs = refs[:n], refs[n:2 * n]
        send_sems, recv_sems, local_sems = refs[2 * n:2 * n + 3]
        for j, i in enumerate(indices):
            mine = _block(full_refs[j], axes[j], shard_refs[j].shape[axes[j]], _my_index())
            pltpu.make_async_copy(shard_refs[j], mine, local_sems.at[i]).wait()
            for m in range(1, N_DEV):
                peer, _ = _peer(m)
                cp = pltpu.make_async_remote_copy(
                    src_ref=shard_refs[j], dst_ref=mine, send_sem=send_sems.at[N_PEERS * i + m - 1],
                    recv_sem=recv_sems.at[N_PEERS * i + m - 1], device_id=peer, device_id_type=MESH)
                cp.wait_send()
                cp.wait_recv()

    out = pl.pallas_call(
        body, name=name,
        out_shape=tuple(pltpu.HBM(a.shape, a.dtype) for a in shards + fulls),
        in_specs=[HBM_SPEC] * (2 * n) + [SEM_SPEC] * 3 + [ANY_SPEC], out_specs=tuple([HBM_SPEC] * (2 * n)),
        input_output_aliases={i: i for i in range(2 * n)},
        compiler_params=SPLIT_PARAMS,
    )(*shards, *fulls, send_sems, recv_sems, local_sems, after)
    return out[n:]


def _exchange_start(name, fulls, axes):
    n = len(fulls)
    sizes = [f.shape[a] // N_DEV for f, a in zip(fulls, axes)]

    def body(*refs):
        ins = refs[:n]
        send_sems, recv_sems = refs[n:n + 2]
        lands = refs[2 * n + 2:3 * n + 2]
        refs[3 * n + 2][...] = jnp.zeros(TOKEN.shape, TOKEN.dtype)
        for i in range(n):
            for m in range(1, N_DEV):
                peer, index = _peer(m)
                pltpu.make_async_remote_copy(
                    src_ref=_block(ins[i], axes[i], sizes[i], index), dst_ref=lands[i].at[m - 1],
                    send_sem=send_sems.at[N_PEERS * i + m - 1], recv_sem=recv_sems.at[N_PEERS * i + m - 1],
                    device_id=peer, device_id_type=MESH).start()

    lands = []
    for f, a, size in zip(fulls, axes, sizes):
        shape = list(f.shape)
        shape[a] = size
        lands.append(pltpu.HBM((N_PEERS, *shape), f.dtype))
    out = pl.pallas_call(
        body, name=name,
        out_shape=(pltpu.SemaphoreType.DMA((N_PEERS * n,)), pltpu.SemaphoreType.DMA((N_PEERS * n,)),
                   *[pltpu.HBM(f.shape, f.dtype) for f in fulls], *lands, TOKEN),
        in_specs=[HBM_SPEC] * n, out_specs=(SEM_SPEC, SEM_SPEC, *[HBM_SPEC] * (2 * n), VMEM_SPEC),
        input_output_aliases={i: 2 + i for i in range(n)},
        compiler_params=SPLIT_PARAMS,
    )(*_in_hbm(fulls))
    return out[:-1], out[-1]


def _exchange_wait(name, started, axes, after):
    send_sems, recv_sems = started[:2]
    n = (len(started) - 2) // 2
    fulls, lands = list(started[2:2 + n]), list(started[2 + n:])
    sizes = [f.shape[a] // N_DEV for f, a in zip(fulls, axes)]

    def body(*refs):
        ins, land_refs = refs[:n], refs[n:2 * n]
        send_sems, recv_sems = refs[2 * n:2 * n + 2]
        for i in range(n):
            for m in range(1, N_DEV):
                peer, index = _peer(m)
                cp = pltpu.make_async_remote_copy(
                    src_ref=_block(ins[i], axes[i], sizes[i], index), dst_ref=land_refs[i].at[m - 1],
                    send_sem=send_sems.at[N_PEERS * i + m - 1], recv_sem=recv_sems.at[N_PEERS * i + m - 1],
                    device_id=peer, device_id_type=MESH)
                cp.wait_send()
                cp.wait_recv()

    out = pl.pallas_call(
        body, name=name,
        out_shape=tuple(pltpu.HBM(a.shape, a.dtype) for a in fulls + lands),
        in_specs=[HBM_SPEC] * (2 * n) + [SEM_SPEC] * 2 + [ANY_SPEC], out_specs=tuple([HBM_SPEC] * (2 * n)),
        input_output_aliases={i: i for i in range(2 * n)},
        compiler_params=SPLIT_PARAMS,
    )(*fulls, *lands, send_sems, recv_sems, after)
    return out[:n], out[n:]


def _adamw_update(w_ref, m_ref, v_ref, part_refs, g_ref, d_ref, nm_ref, nv_ref):
    c1 = 1.0 - ADAM_B1 ** ADAM_STEP
    c2 = 1.0 - ADAM_B2 ** ADAM_STEP
    g = None
    for p_ref in part_refs:
        stacked = len(p_ref.shape) > len(w_ref.shape)
        terms = [p_ref[s] for s in range(p_ref.shape[0])] if stacked else [p_ref[...]]
        for term in terms:
            term = term.astype(F32)
            g = term if g is None else g + term
    new_m = ADAM_B1 * m_ref[...] + (1.0 - ADAM_B1) * g
    new_v = ADAM_B2 * v_ref[...] + (1.0 - ADAM_B2) * (g * g)
    g_ref[...] = g
    nm_ref[...] = new_m
    nv_ref[...] = new_v
    d_ref[...] = -ADAM_LR * ((new_m / c1) / (jnp.sqrt(new_v / c2) + ADAM_EPS) + ADAM_WD * w_ref[...])


def _adamw_small(ws, ms, vs, stacks, loss_stack):
    n = len(ws)

    def body(*refs):
        ins, outs = refs[:4 * n + 1], refs[4 * n + 1:]
        for i in range(n):
            _adamw_update(ins[i], ins[n + i], ins[2 * n + i], [ins[3 * n + i]],
                          outs[i], outs[n + i], outs[2 * n + i], outs[3 * n + i])
        total = ins[4 * n][0]
        for dev in range(1, N_DEV):
            total = total + ins[4 * n][dev]
        outs[4 * n][...] = total

    res = pl.pallas_call(
        body, name="adamw_small",
        out_shape=[jax.ShapeDtypeStruct(w.shape, F32) for w in ws] * 4 + [jax.ShapeDtypeStruct((1, LANES), F32)],
        compiler_params=_params(None),
    )(*ws, *ms, *vs, *stacks, loss_stack)
    return res[:n], res[n:2 * n], res[2 * n:3 * n], res[3 * n:4 * n], res[4 * n]


def _adamw(name, w, m, v, parts):
    r, c = w.shape
    tr = _tile(r, 256)
    n_parts = len(parts)

    def body(*refs):
        _adamw_update(refs[0], refs[1], refs[2], refs[3:3 + n_parts], *refs[3 + n_parts:])

    row = pl.BlockSpec((tr, c), lambda i: (i, 0))
    in_specs = [row, row, row]
    for p in parts:
        in_specs.append(row if p.ndim == 2 else pl.BlockSpec((p.shape[0], tr, c), lambda i: (0, i, 0)))
    return pl.pallas_call(
        body, name=name, grid=(r // tr,), in_specs=in_specs, out_specs=[row] * 4,
        out_shape=[jax.ShapeDtypeStruct((r, c), F32)] * 4,
        compiler_params=_params(("arbitrary",)),
    )(w, m, v, *parts)


SMALL = ("norm_gain", "pool_scale", "a_re", "a_im", "log_dt", "b_re", "b_im", "c_re", "c_im", "d_skip", "final_gain")
LARGE = ("w_in", "w_pool", "w_glu", "w_out", "w_ple", "w_ple_gate")
LARGE_AXIS = {"w_in": 1, "w_pool": 1, "w_glu": 1, "w_out": 0, "w_ple": 1, "w_ple_gate": 0}
WEIGHTS = ("norm_gain", "w_in", "w_pool", "pool_scale", "a_re", "a_im", "log_dt", "b_re", "b_im", "c_re", "c_im",
           "d_skip", "w_glu", "w_out", "w_ple", "w_ple_gate", "final_gain")


def kernel(x, p, norm_gain, w_in, w_pool, pool_scale, a_re, a_im, log_dt, b_re, b_im, c_re, c_im, d_skip, w_glu, w_out, w_ple, w_ple_gate, final_gain, loss_target, m_norm_gain, m_w_in, m_w_pool, m_pool_scale, m_a_re, m_a_im, m_log_dt, m_b_re, m_b_im, m_c_re, m_c_im, m_d_skip, m_w_glu, m_w_out, m_w_ple, m_w_ple_gate, m_final_gain, v_norm_gain, v_w_in, v_w_pool, v_pool_scale, v_a_re, v_a_im, v_log_dt, v_b_re, v_b_im, v_c_re, v_c_im, v_d_skip, v_w_glu, v_w_out, v_w_ple, v_w_ple_gate, v_final_gain):
    weights = dict(norm_gain=norm_gain, w_in=w_in, w_pool=w_pool, pool_scale=pool_scale, a_re=a_re, a_im=a_im,
                   log_dt=log_dt, b_re=b_re, b_im=b_im, c_re=c_re, c_im=c_im, d_skip=d_skip, w_glu=w_glu,
                   w_out=w_out, w_ple=w_ple, w_ple_gate=w_ple_gate, final_gain=final_gain)
    mom_m = dict(norm_gain=m_norm_gain, w_in=m_w_in, w_pool=m_w_pool, pool_scale=m_pool_scale, a_re=m_a_re,
                 a_im=m_a_im, log_dt=m_log_dt, b_re=m_b_re, b_im=m_b_im, c_re=m_c_re, c_im=m_c_im,
                 d_skip=m_d_skip, w_glu=m_w_glu, w_out=m_w_out, w_ple=m_w_ple, w_ple_gate=m_w_ple_gate,
                 final_gain=m_final_gain)
    mom_v = dict(norm_gain=v_norm_gain, w_in=v_w_in, w_pool=v_w_pool, pool_scale=v_pool_scale, a_re=v_a_re,
                 a_im=v_a_im, log_dt=v_log_dt, b_re=v_b_re, b_im=v_b_im, c_re=v_c_re, c_im=v_c_im,
                 d_skip=v_d_skip, w_glu=v_w_glu, w_out=v_w_out, w_ple=v_w_ple, w_ple_gate=v_w_ple_gate,
                 final_gain=v_final_gain)

    t = x.shape[1]
    xs = x.reshape(t, D_MODEL)
    ps = p.reshape(t, PLE_DIM)
    target = loss_target.reshape(t, D_MODEL)
    gain1 = norm_gain.reshape(1, D_MODEL)
    gain_f = final_gain.reshape(1, D_MODEL)
    scale_p = pool_scale.reshape(1, POOL_WIDTH)
    skip = d_skip.reshape(1, SSM_WIDTH)

    shard2d = {k: weights[k][0] for k in LARGE}
    shard_bf = {k: shard2d[k].astype(BF16) for k in LARGE}
    full = {"w_in": _all_gather("w_in_all_gather", [shard_bf["w_in"]], [LARGE_AXIS["w_in"]])[0]}
    later = [k for k in LARGE if k != "w_in"]
    later_axes = [LARGE_AXIS[k] for k in later]
    gather, gather_token = _gather_start("weights_gather_start", [shard_bf[k] for k in later], later_axes,
                                         full["w_in"])

    def arrive(k, after):
        i = later.index(k)
        full[k] = _gather_wait("gather_wait_" + k, gather, [i], [later_axes[i]], after)[0]

    ar, ai = a_re[0], a_im[0]
    ldt = log_dt.reshape(N_SSM_GROUPS, 1)
    br_t = jnp.transpose(b_re[0], (0, 2, 1))
    bi_t = jnp.transpose(b_im[0], (0, 2, 1))
    ab_re, ab_im, bb_re, bb_im = _ssm_params(ar, ai, ldt, br_t, bi_t)
    tiles = (SSM_CHUNKS, CHUNK_TILES, LANES)
    abar = jnp.concatenate([ab_re.reshape(tiles), ab_im.reshape(tiles)], axis=-1)
    w_pair = _compact_pair(bb_re, bb_im)
    c_pair = _compact_pair(c_re[0], -c_im[0])

    hn = _norm1_fwd(xs, gain1)
    proj = _mm_nn("in_proj", hn, full["w_in"], [F32], tk=2048, after=[gather_token])[0]
    pooled = _pool_fwd(proj)
    tm = _tile(t, 1024)
    arrive("w_pool", pooled)
    mixed = _mm("pool_mix", [(pooled, (tm, POOL_GROUP), lambda i, j, s: (i, j),
                              full["w_pool"], (None, POOL_GROUP, POOL_GROUP), lambda i, j, s: (j, 0, 0))],
                DOT_NN, (t // tm, N_POOL_GROUPS, 1),
                [((t, POOL_WIDTH), F32, (tm, POOL_GROUP), lambda i, j, s: (i, j))], 1)[0]
    y, gel, states = _ssm_fwd(proj, w_pair, c_pair, abar, skip)
    arrive("w_glu", gel)
    hg = _mm_nn("glu_proj", gel, full["w_glu"], [F32])[0]
    cat = _gate_fwd(mixed, proj, hg, scale_p)

    def residual_epilogue(acc, ex, out_refs):
        h = acc + ex[0][...]
        out_refs[0][...] = h
        out_refs[1][...] = h.astype(BF16)

    arrive("w_out", cat)
    h1, h1b = _mm_nn("out_proj", cat, full["w_out"], [F32, BF16], extras=[xs], epilogue=residual_epilogue)
    arrive("w_ple", h1b)
    arrive("w_ple_gate", h1b)
    de, dq, dh2, g_final_gain, loss_part = _ple_final(h1, h1b, ps, full["w_ple_gate"], full["w_ple"], target, gain_f)

    grads = {}
    grads["w_ple_gate"] = _mm_tn("ple_gate_wgrad", h1b, dq, BF16)
    grads["w_ple"] = _mm_tn("ple_wgrad", ps, de, BF16)
    sent, tokens = {}, {}

    def send(names):
        sent[names], tokens[names[0]] = _exchange_start(
            "grads_start_" + names[0], [grads[k] for k in names], [LARGE_AXIS[k] for k in names])

    send(("w_ple_gate", "w_ple"))
    dh1, dh1b = _mm_nt("ple_gate_dgrad", dq, full["w_ple_gate"], [F32, BF16], extras=[dh2],
                       epilogue=residual_epilogue)
    grads["w_out"] = _mm_tn("out_wgrad", cat, dh1b, BF16)
    send(("w_out",))
    dmixed, dproj, dhg, g_pool_scale = _out_dgrad_gate_bwd(
        dh1b, full["w_out"], mixed, proj, hg, scale_p, [tokens["w_ple_gate"], tokens["w_out"]])

    tk = _tile(t, 1024)
    grads["w_pool"] = _mm("pool_wgrad", [(pooled, (tk, POOL_GROUP), lambda i, j, s: (s, i),
                                          dmixed, (tk, POOL_GROUP), lambda i, j, s: (s, i))],
                          DOT_TN, (N_POOL_GROUPS, 1, t // tk),
                          [((N_POOL_GROUPS, POOL_GROUP, POOL_GROUP), BF16, (None, POOL_GROUP, POOL_GROUP),
                            lambda i, j, s: (i, 0, 0))], t // tk)[0]
    dpooled = _mm("pool_dgrad", [(dmixed, (tm, POOL_GROUP), lambda i, j, s: (i, j),
                                  full["w_pool"], (None, POOL_GROUP, POOL_GROUP), lambda i, j, s: (j, 0, 0))],
                  DOT_NT, (t // tm, N_POOL_GROUPS, 1),
                  [((t, POOL_WIDTH), F32, (tm, POOL_GROUP), lambda i, j, s: (i, j))], 1)[0]
    dproj = _pool_bwd(dpooled, dproj)

    grads["w_glu"] = _mm_tn("glu_wgrad", gel, dhg, BF16)
    send(("w_pool", "w_glu"))

    def gelu_bwd_epilogue(acc, ex, out_refs):
        yv = ex[0][...]
        th = jnp.tanh(GELU_C * (yv + GELU_A * yv * yv * yv))
        dgelu = 0.5 * (1.0 + th) + 0.5 * yv * (1.0 - th * th) * GELU_C * (1.0 + 3.0 * GELU_A * yv * yv)
        out_refs[0][...] = acc * dgelu

    dy = _mm_nt("glu_dgrad", dhg, full["w_glu"], [F32], tk=2048, extras=[y], epilogue=gelu_bwd_epilogue,
                after=[tokens["w_pool"]])[0]
    dproj, g_c_pair, g_w_pair, g_abar, g_d_skip = _ssm_bwd(dy, proj, states, w_pair, c_pair, abar, skip, dproj)

    g_ab_re = g_abar[..., :LANES].reshape(N_SSM_GROUPS, SSM_STATE)
    g_ab_im = g_abar[..., LANES:].reshape(N_SSM_GROUPS, SSM_STATE)
    d_ar, d_ai, d_ldt, d_br_t, d_bi_t = _ssm_params_bwd(
        ar, ai, ldt, br_t, bi_t, g_ab_re, g_ab_im,
        _expand_grad(g_w_pair[..., :LANES]), _expand_grad(g_w_pair[..., LANES:]))

    small_grads = dict(
        pool_scale=g_pool_scale, a_re=d_ar, a_im=d_ai, log_dt=d_ldt.reshape(1, N_SSM_GROUPS),
        b_re=d_br_t.astype(BF16), b_im=d_bi_t.astype(BF16), c_re=_expand_grad(g_c_pair[..., :LANES]).astype(BF16),
        c_im=(-_expand_grad(g_c_pair[..., LANES:])).astype(BF16), d_skip=g_d_skip, final_gain=g_final_gain)
    early = [k for k in SMALL if k != "norm_gain"]
    early_sent, early_token = _gather_start(
        "small_grads_start", [small_grads[k][None] for k in early] + [jnp.broadcast_to(loss_part, (1, 1, LANES))],
        [0] * (len(early) + 1), d_ar)

    grads["w_in"] = _mm_tn("in_wgrad", hn, dproj, BF16, after=[early_token])
    send(("w_in",))
    grad_x, g_norm_gain = _in_dgrad_norm1_bwd(dproj, full["w_in"], xs, dh1, gain1, tokens["w_in"])
    late_sent, late_token = _gather_start("norm_gain_grad_start", [g_norm_gain[None]], [0], g_norm_gain)

    out_g, out_d, out_m, out_v = ({} for _ in range(4))
    me = 4 * lax.axis_index("x") + 2 * lax.axis_index("y") + lax.axis_index("c")
    after = late_token
    for names, started in sent.items():
        axes = [LARGE_AXIS[k] for k in names]
        partials, landed = _exchange_wait("grads_wait_" + names[0], started, axes, after)
        for k, axis, partial, land in zip(names, axes, partials, landed):
            shard_shape = shard2d[k].shape
            size = shard_shape[axis]
            own = lax.dynamic_slice_in_dim(partial, me * size, size, axis=axis)
            view = (-1, shard_shape[-1])
            rows = math.prod(shard_shape[:-1])
            res = _adamw("adamw_" + k, shard2d[k].reshape(view), mom_m[k][0].reshape(view), mom_v[k][0].reshape(view),
                         [own.reshape(view), land.reshape(N_PEERS, rows, shard_shape[-1])])
            out_g[k], out_d[k], out_m[k], out_v[k] = (r.reshape(weights[k].shape) for r in res)
            after = res[0]

    def b_view(a):
        return jnp.transpose(a[0], (0, 2, 1))

    views = dict(norm_gain=lambda a: a, pool_scale=lambda a: a, a_re=lambda a: a[0], a_im=lambda a: a[0],
                 log_dt=lambda a: a, b_re=b_view, b_im=b_view, c_re=lambda a: a[0], c_im=lambda a: a[0],
                 d_skip=lambda a: a, final_gain=lambda a: a.reshape(1, D_MODEL))
    landed = _gather_wait("small_grads_wait", early_sent, list(range(len(early) + 1)), [0] * (len(early) + 1), after)
    stack = dict(zip(early, landed))
    stack["norm_gain"] = _gather_wait("norm_gain_grad_wait", late_sent, [0], [0], after)[0]
    *small_out, loss_row = _adamw_small(
        [views[k](weights[k]) for k in SMALL], [views[k](mom_m[k]) for k in SMALL],
        [views[k](mom_v[k]) for k in SMALL], [stack[k] for k in SMALL], landed[-1])
    loss = loss_row[0, 0]
    for out, res in zip((out_g, out_d, out_m, out_v), small_out):
        for k, r in zip(SMALL, res):
            if k in ("b_re", "b_im"):
                r = jnp.transpose(r, (0, 2, 1))
            out[k] = r.reshape(weights[k].shape)

    return (loss, grad_x.reshape(x.shape), *[out_g[k] for k in WEIGHTS], *[out_d[k] for k in WEIGHTS],
            *[out_m[k] for k in WEIGHTS], *[out_v[k] for k in WEIGHTS])
```

```python
import math

import jax
import jax.numpy as jnp
from jax import lax
from jax.experimental import pallas as pl
from jax.experimental.pallas import tpu as pltpu

F32 = jnp.float32
BF16 = jnp.bfloat16
MESH = pl.DeviceIdType.MESH
MESH_AXES = ("x", "y", "c")
N_DEV = 8

D_MODEL = 2048
POOL_WIDTH = 1024
SSM_WIDTH = 1024
N_POOL_GROUPS = 4
POOL_GROUP = 256
SSM_GROUP = 16
N_SSM_GROUPS = 64
SSM_STATE = 64
SSM_FLAT = N_SSM_GROUPS * SSM_STATE
SSM_CHUNKS = 4
CHUNK_IN = SSM_WIDTH // SSM_CHUNKS
CHUNK_STATE = SSM_FLAT // SSM_CHUNKS
PLE_DIM = 256
EPS = 1e-6
A_RE_MAX = -1e-4
ADAM_LR = 0.001
ADAM_B1 = 0.9
ADAM_B2 = 0.999
ADAM_EPS = 1e-08
ADAM_WD = 0.01
ADAM_STEP = 10
GELU_C = math.sqrt(2.0 / math.pi)
GELU_A = 0.044715

SUBLANES = 8
LANES = 128
VMEM_LIMIT_BYTES = 48 * 1024 * 1024

DOT_NN = (((1,), (0,)), ((), ()))
DOT_NT = (((1,), (1,)), ((), ()))
DOT_TN = (((0,), (0,)), ((), ()))


def _tile(n, pref):
    return pref if n % pref == 0 else n


def _params(sem):
    return pltpu.CompilerParams(dimension_semantics=sem, vmem_limit_bytes=VMEM_LIMIT_BYTES)


def _sigmoid(v):
    return 1.0 / (1.0 + jnp.exp(-v))


def _silu_and_grad(v):
    s = _sigmoid(v)
    return v * s, s * (1.0 + v * (1.0 - s))


def _mm(name, pairs, dims, grid, outs, k_steps, extras=(), epilogue=None):
    n_pairs, n_ex, n_out = len(pairs), len(extras), len(outs)
    acc_shape = tuple(d for d in outs[0][2] if d is not None)
    if epilogue is None:
        def epilogue(acc, ex, out_refs):
            out_refs[0][...] = acc.astype(out_refs[0].dtype)

    def body(*refs):
        ab = refs[:2 * n_pairs]
        ex = refs[2 * n_pairs:2 * n_pairs + n_ex]
        out_refs = refs[2 * n_pairs + n_ex:2 * n_pairs + n_ex + n_out]
        acc = refs[-1]
        k = pl.program_id(2)

        @pl.when(k == 0)
        def _():
            acc[...] = jnp.zeros_like(acc)

        part = None
        for q in range(n_pairs):
            d = lax.dot_general(ab[2 * q][...].astype(BF16), ab[2 * q + 1][...].astype(BF16), dims,
                                preferred_element_type=F32)
            part = d if part is None else part + d
        acc[...] += part

        @pl.when(k == k_steps - 1)
        def _():
            epilogue(acc[...], ex, out_refs)

    in_specs, operands = [], []
    for a, a_blk, a_map, b, b_blk, b_map in pairs:
        in_specs += [pl.BlockSpec(a_blk, a_map), pl.BlockSpec(b_blk, b_map)]
        operands += [a, b]
    for e, e_blk, e_map in extras:
        in_specs.append(pl.BlockSpec(e_blk, e_map))
        operands.append(e)
    return pl.pallas_call(
        body, name=name, grid=grid, in_specs=in_specs,
        out_specs=[pl.BlockSpec(o[2], o[3]) for o in outs],
        out_shape=[jax.ShapeDtypeStruct(o[0], o[1]) for o in outs],
        scratch_shapes=[pltpu.VMEM(acc_shape, F32)],
        compiler_params=_params(("arbitrary", "arbitrary", "arbitrary")),
    )(*operands)


def _after(tokens):
    return [(tok, tok.shape, lambda i, j, s: (0, 0)) for tok in tokens]


def _mm_nn(name, a, b, out_dtypes, tm=1024, tn=1024, tk=1024, a_col0=0, extras=(), epilogue=None, after=()):
    m, n = a.shape[0], b.shape[1]
    k = b.shape[0]
    tm, tn, tk = _tile(m, tm), _tile(n, tn), _tile(k, tk)
    outs = [((m, n), dt, (tm, tn), lambda i, j, s: (i, j)) for dt in out_dtypes]
    ex = [(e, (tm, tn), lambda i, j, s: (i, j)) for e in extras] + _after(after)
    return _mm(name, [(a, (tm, tk), lambda i, j, s: (i, a_col0 + s), b, (tk, tn), lambda i, j, s: (s, j))],
               DOT_NN, (m // tm, n // tn, k // tk), outs, k // tk, ex, epilogue)


def _mm_nt(name, a, b, out_dtypes, tm=1024, tn=1024, tk=1024, extras=(), epilogue=None, after=()):
    m, kk = a.shape
    n = b.shape[0]
    tm, tn, tk = _tile(m, tm), _tile(n, tn), _tile(kk, tk)
    outs = [((m, n), dt, (tm, tn), lambda i, j, s: (i, j)) for dt in out_dtypes]
    ex = [(e, (tm, tn), lambda i, j, s: (i, j)) for e in extras] + _after(after)
    return _mm(name, [(a, (tm, tk), lambda i, j, s: (i, s), b, (tn, tk), lambda i, j, s: (j, s))],
               DOT_NT, (m // tm, n // tn, kk // tk), outs, kk // tk, ex, epilogue)


def _mm_tn(name, a, b, out_dtype, tm=512, tn=2048, tk=1024, after=(), a_cols=None):
    m = a.shape[0]
    col0, kk = a_cols if a_cols else (0, a.shape[1])
    n = b.shape[1]
    tm, tn, tk = _tile(kk, tm), _tile(n, tn), _tile(m, tk)
    first = col0 // tm
    outs = [((kk, n), out_dtype, (tm, tn), lambda i, j, s: (i, j))]
    return _mm(name, [(a, (tk, tm), lambda i, j, s: (s, first + i), b, (tk, tn), lambda i, j, s: (s, j))],
               DOT_TN, (kk // tm, n // tn, m // tk), outs, m // tk, _after(after))[0]


def _norm1_fwd(x, gain):
    t = x.shape[0]
    tm = _tile(t, 512)

    def body(x_ref, g_ref, hn_ref):
        xv = x_ref[...]
        r = lax.rsqrt(jnp.mean(xv * xv, axis=-1, keepdims=True) + EPS)
        hn_ref[...] = (xv * r * g_ref[...]).astype(BF16)

    return pl.pallas_call(
        body, name="norm1_fwd", grid=(t // tm,),
        in_specs=[pl.BlockSpec((tm, D_MODEL), lambda i: (i, 0)), pl.BlockSpec((1, D_MODEL), lambda i: (0, 0))],
        out_specs=pl.BlockSpec((tm, D_MODEL), lambda i: (i, 0)),
        out_shape=jax.ShapeDtypeStruct((t, D_MODEL), BF16),
        compiler_params=_params(("arbitrary",)),
    )(x, gain)


ROW_TILE = 256


def _in_dgrad_norm1_bwd(dproj, w_in, x, dh1, gain, after):
    t = x.shape[0]
    tm = _tile(t, ROW_TILE)

    def body(dp_ref, w_ref, x_ref, dh1_ref, g_ref, _, dx_ref, gg_ref):
        @pl.when(pl.program_id(0) == 0)
        def _():
            gg_ref[...] = jnp.zeros_like(gg_ref)

        dhn = lax.dot_general(dp_ref[...], w_ref[...], DOT_NT, preferred_element_type=F32)
        xv = x_ref[...]
        r = lax.rsqrt(jnp.mean(xv * xv, axis=-1, keepdims=True) + EPS)
        xh = xv * r
        gg_ref[...] += jnp.sum(dhn * xh, axis=0, keepdims=True)
        dxh = dhn * g_ref[...]
        dx_ref[...] = dh1_ref[...] + r * (dxh - xh * jnp.mean(dxh * xh, axis=-1, keepdims=True))

    row = pl.BlockSpec((tm, D_MODEL), lambda i: (i, 0))
    vec = pl.BlockSpec((1, D_MODEL), lambda i: (0, 0))
    return pl.pallas_call(
        body, name="in_dgrad_norm1_bwd", grid=(t // tm,),
        in_specs=[pl.BlockSpec((tm, dproj.shape[1]), lambda i: (i, 0)), _resident(w_in.shape), row, row, vec,
                  pl.BlockSpec(after.shape, lambda i: (0, 0))],
        out_specs=[row, vec],
        out_shape=[jax.ShapeDtypeStruct((t, D_MODEL), F32), jax.ShapeDtypeStruct((1, D_MODEL), F32)],
        compiler_params=_params(("arbitrary",)),
    )(dproj, w_in, x, dh1, gain, after)


def _pool_counts(t, width, group):
    row = lax.broadcasted_iota(jnp.int32, (t, width), 0)
    window = jnp.left_shift(jnp.int32(2), group)
    return row, jnp.minimum(row + 1, window).astype(F32)


def _select_window(group, s2, s4, s8, s16):
    return jnp.where(group == 0, s2, jnp.where(group == 1, s4, jnp.where(group == 2, s8, s16)))


def _pool_fwd(proj):
    t = proj.shape[0]
    tc = LANES

    def body(u_ref, o_ref):
        group = pl.program_id(0) // (POOL_GROUP // tc)
        v = u_ref[...]
        row, count = _pool_counts(t, tc, group)

        def down(a, j):
            return jnp.where(row >= j, pltpu.roll(a, j, 0), 0.0)

        s2 = v + down(v, 1)
        s4 = s2 + down(s2, 2)
        s8 = s4 + down(s4, 4)
        s16 = s8 + down(s8, 8)
        o_ref[...] = (_select_window(group, s2, s4, s8, s16) / count - v).astype(BF16)

    return pl.pallas_call(
        body, name="pool_fwd", grid=(POOL_WIDTH // tc,),
        in_specs=[pl.BlockSpec((t, tc), lambda j: (0, j))],
        out_specs=pl.BlockSpec((t, tc), lambda j: (0, j)),
        out_shape=jax.ShapeDtypeStruct((t, POOL_WIDTH), BF16),
        compiler_params=_params(("arbitrary",)),
    )(proj)


def _pool_bwd(dpooled, dproj):
    t = dpooled.shape[0]
    tc = LANES

    def body(d_ref, _, o_ref):
        group = pl.program_id(0) // (POOL_GROUP // tc)
        dp = d_ref[...]
        row, count = _pool_counts(t, tc, group)
        r = dp / count

        def up(a, j):
            return jnp.where(row < t - j, pltpu.roll(a, t - j, 0), 0.0)

        s2 = r + up(r, 1)
        s4 = s2 + up(s2, 2)
        s8 = s4 + up(s4, 4)
        s16 = s8 + up(s8, 8)
        o_ref[...] = (_select_window(group, s2, s4, s8, s16) - dp).astype(BF16)

    return pl.pallas_call(
        body, name="pool_bwd", grid=(POOL_WIDTH // tc,),
        in_specs=[pl.BlockSpec((t, tc), lambda j: (0, j)), pl.BlockSpec(memory_space=pl.ANY)],
        out_specs=pl.BlockSpec((t, tc), lambda j: (0, j)),
        out_shape=jax.ShapeDtypeStruct(dproj.shape, dproj.dtype),
        input_output_aliases={1: 0},
        compiler_params=_params(("arbitrary",)),
    )(dpooled, dproj)


def _gate_fwd(mixed, proj, hg, pool_scale):
    t = mixed.shape[0]
    tm = _tile(t, 512)

    def body(mx_ref, ga_ref, gb_ref, hg_ref, ps_ref, cat_ref):
        silu_a, _ = _silu_and_grad(ga_ref[...])
        cat_ref[:, :POOL_WIDTH] = (mx_ref[...] * ps_ref[...] * silu_a).astype(BF16)
        silu_b, _ = _silu_and_grad(gb_ref[...])
        sb = hg_ref[:, :SSM_WIDTH] * _sigmoid(hg_ref[:, SSM_WIDTH:])
        cat_ref[:, POOL_WIDTH:] = (sb * silu_b).astype(BF16)

    return pl.pallas_call(
        body, name="gate_fwd", grid=(t // tm,),
        in_specs=[pl.BlockSpec((tm, POOL_WIDTH), lambda i: (i, 0)),
                  pl.BlockSpec((tm, POOL_WIDTH), lambda i: (i, 1)),
                  pl.BlockSpec((tm, SSM_WIDTH), lambda i: (i, 3)),
                  pl.BlockSpec((tm, 2 * SSM_WIDTH), lambda i: (i, 0)),
                  pl.BlockSpec((1, POOL_WIDTH), lambda i: (0, 0))],
        out_specs=pl.BlockSpec((tm, D_MODEL), lambda i: (i, 0)),
        out_shape=jax.ShapeDtypeStruct((t, D_MODEL), BF16),
        compiler_params=_params(("arbitrary",)),
    )(mixed, proj, proj, hg, pool_scale)


def _out_dgrad_gate_bwd(dh1b, w_out, mixed, proj, hg, pool_scale, after):
    t = mixed.shape[0]
    tm = _tile(t, ROW_TILE)
    n_after = len(after)

    def body(dh_ref, w_ref, mx_ref, ga_ref, gb_ref, hg_ref, ps_ref, *rest):
        dmx_ref, dp_ref, dhg_ref, gps_ref = rest[n_after:]

        @pl.when(pl.program_id(0) == 0)
        def _():
            gps_ref[...] = jnp.zeros_like(gps_ref)

        dcat = lax.dot_general(dh_ref[...], w_ref[...], DOT_NT, preferred_element_type=F32)
        ps = ps_ref[...]
        mx = mx_ref[...]
        dya = dcat[:, :POOL_WIDTH]
        silu_a, dsilu_a = _silu_and_grad(ga_ref[...])
        dpa = dya * silu_a
        gps_ref[...] += jnp.sum(dpa * mx, axis=0, keepdims=True)
        dmx_ref[...] = (dpa * ps).astype(BF16)
        dp_ref[:, :POOL_WIDTH] = jnp.zeros((tm, POOL_WIDTH), BF16)
        dp_ref[:, POOL_WIDTH:2 * POOL_WIDTH] = (dya * mx * ps * dsilu_a).astype(BF16)

        dyb = dcat[:, POOL_WIDTH:]
        silu_b, dsilu_b = _silu_and_grad(gb_ref[...])
        h_a = hg_ref[:, :SSM_WIDTH]
        sg = _sigmoid(hg_ref[:, SSM_WIDTH:])
        dsb = dyb * silu_b
        dp_ref[:, 2 * POOL_WIDTH:2 * POOL_WIDTH + SSM_WIDTH] = jnp.zeros((tm, SSM_WIDTH), BF16)
        dp_ref[:, 2 * POOL_WIDTH + SSM_WIDTH:] = (dyb * h_a * sg * dsilu_b).astype(BF16)
        dhg_ref[:, :SSM_WIDTH] = (dsb * sg).astype(BF16)
        dhg_ref[:, SSM_WIDTH:] = (dsb * h_a * sg * (1.0 - sg)).astype(BF16)

    half = pl.BlockSpec((tm, POOL_WIDTH), lambda i: (i, 0))
    full = pl.BlockSpec((tm, D_MODEL), lambda i: (i, 0))
    vec = pl.BlockSpec((1, POOL_WIDTH), lambda i: (0, 0))
    proj_width = 2 * POOL_WIDTH + 2 * SSM_WIDTH
    return pl.pallas_call(
        body, name="out_dgrad_gate_bwd", grid=(t // tm,),
        in_specs=[full, _resident(w_out.shape), half,
                  pl.BlockSpec((tm, POOL_WIDTH), lambda i: (i, 1)),
                  pl.BlockSpec((tm, SSM_WIDTH), lambda i: (i, 3)),
                  full, vec] + [pl.BlockSpec(tok.shape, lambda i: (0, 0)) for tok in after],
        out_specs=[half, pl.BlockSpec((tm, proj_width), lambda i: (i, 0)), full, vec],
        out_shape=[jax.ShapeDtypeStruct((t, POOL_WIDTH), BF16), jax.ShapeDtypeStruct((t, proj_width), BF16),
                   jax.ShapeDtypeStruct((t, 2 * SSM_WIDTH), BF16),
                   jax.ShapeDtypeStruct((1, POOL_WIDTH), F32)],
        compiler_params=_params(("arbitrary",)),
    )(dh1b, w_out, mixed, proj, proj, hg, pool_scale, *after)


def _ple_final(h1, h1b, p, w_gate, w_ple, target, gain):
    t = h1.shape[0]
    tm = _tile(t, 256)

    def body(h1_ref, h1b_ref, p_ref, wg_ref, wp_ref, tg_ref, g_ref, de_ref, dq_ref, dh2_ref, gg_ref, loss_ref):
        @pl.when(pl.program_id(0) == 0)
        def _():
            gg_ref[...] = jnp.zeros_like(gg_ref)
            loss_ref[...] = jnp.zeros_like(loss_ref)

        ev = jnp.dot(p_ref[...].astype(BF16), wp_ref[...], preferred_element_type=F32)
        sg = _sigmoid(jnp.dot(h1b_ref[...], wg_ref[...], preferred_element_type=F32))
        h2 = h1_ref[...] + ev * sg
        r = lax.rsqrt(jnp.mean(h2 * h2, axis=-1, keepdims=True) + EPS)
        n = h2 * r
        gain_v = g_ref[...]
        diff = n * gain_v - tg_ref[...]
        row_loss = jnp.sum(diff * diff, axis=-1, keepdims=True)
        loss_ref[...] += (0.5 / D_MODEL) * jnp.sum(row_loss, axis=0, keepdims=True)
        dout = diff * (1.0 / D_MODEL)
        gg_ref[...] += jnp.sum(dout * n, axis=0, keepdims=True)
        dn = dout * gain_v
        dh2 = r * (dn - n * jnp.mean(dn * n, axis=-1, keepdims=True))
        dh2_ref[...] = dh2
        de_ref[...] = (dh2 * sg).astype(BF16)
        dq_ref[...] = (dh2 * ev * sg * (1.0 - sg)).astype(BF16)

    row = pl.BlockSpec((tm, D_MODEL), lambda i: (i, 0))
    vec = pl.BlockSpec((1, D_MODEL), lambda i: (0, 0))
    return pl.pallas_call(
        body, name="ple_final", grid=(t // tm,),
        in_specs=[row, row, pl.BlockSpec((tm, PLE_DIM), lambda i: (i, 0)), _resident((D_MODEL, D_MODEL)),
                  _resident((PLE_DIM, D_MODEL)), row, vec],
        out_specs=[row, row, row, vec, pl.BlockSpec((1, 1), lambda i: (0, 0))],
        out_shape=[jax.ShapeDtypeStruct((t, D_MODEL), BF16), jax.ShapeDtypeStruct((t, D_MODEL), BF16),
                   jax.ShapeDtypeStruct((t, D_MODEL), F32), jax.ShapeDtypeStruct((1, D_MODEL), F32),
                   jax.ShapeDtypeStruct((1, 1), F32)],
        compiler_params=_params(("arbitrary",)),
    )(h1, h1b, p, w_gate, w_ple, target, gain)


def _zoh(a_re, a_im, log_dt, b_re_t, b_im_t):
    lam_re = jnp.minimum(a_re, A_RE_MAX)
    lam_im = a_im
    dt = jnp.exp(log_dt)
    mag = jnp.exp(lam_re * dt)
    ang = lam_im * dt
    ab_re = mag * jnp.cos(ang)
    ab_im = mag * jnp.sin(ang)
    den = lam_re * lam_re + lam_im * lam_im
    n_re = ab_re - 1.0
    n_im = ab_im
    q_re = (n_re * lam_re + n_im * lam_im) / den
    q_im = (n_im * lam_re - n_re * lam_im) / den
    bb_re = q_re[:, None, :] * b_re_t - q_im[:, None, :] * b_im_t
    bb_im = q_re[:, None, :] * b_im_t + q_im[:, None, :] * b_re_t
    return ab_re, ab_im, bb_re, bb_im


def _ssm_params(a_re, a_im, log_dt, b_re_t, b_im_t):
    def body(are_ref, aim_ref, dt_ref, bre_ref, bim_ref, abre_ref, abim_ref, bbre_ref, bbim_ref):
        ab_re, ab_im, bb_re, bb_im = _zoh(are_ref[...], aim_ref[...], dt_ref[...], bre_ref[...], bim_ref[...])
        abre_ref[...] = ab_re
        abim_ref[...] = ab_im
        bbre_ref[...] = bb_re
        bbim_ref[...] = bb_im

    return pl.pallas_call(
        body, name="ssm_params",
        out_shape=[jax.ShapeDtypeStruct(a_re.shape, F32), jax.ShapeDtypeStruct(a_re.shape, F32),
                   jax.ShapeDtypeStruct(b_re_t.shape, F32), jax.ShapeDtypeStruct(b_re_t.shape, F32)],
        compiler_params=_params(None),
    )(a_re, a_im, log_dt, b_re_t, b_im_t)


def _ssm_params_bwd(a_re, a_im, log_dt, b_re_t, b_im_t, g_ab_re, g_ab_im, g_bb_re, g_bb_im):
    def body(are_ref, aim_ref, dt_ref, bre_ref, bim_ref, gar_ref, gai_ref, gbr_ref, gbi_ref,
             o_are, o_aim, o_dt, o_bre, o_bim):
        _, vjp = jax.vjp(_zoh, are_ref[...], aim_ref[...], dt_ref[...], bre_ref[...], bim_ref[...])
        d_are, d_aim, d_dt, d_bre, d_bim = vjp((gar_ref[...], gai_ref[...], gbr_ref[...], gbi_ref[...]))
        o_are[...] = d_are
        o_aim[...] = d_aim
        o_dt[...] = d_dt
        o_bre[...] = d_bre
        o_bim[...] = d_bim

    ins = (a_re, a_im, log_dt, b_re_t, b_im_t)
    return pl.pallas_call(
        body, name="ssm_params_bwd",
        out_shape=[jax.ShapeDtypeStruct(v.shape, F32) for v in ins],
        compiler_params=_params(None),
    )(*ins, g_ab_re, g_ab_im, g_bb_re, g_bb_im)


CHUNK_TILES = CHUNK_STATE // LANES
CH_PER_TILE = CHUNK_IN // CHUNK_TILES
PAIR = 2 * LANES
SSM_ROWS = 256
SCAN_STEPS = 8
U_COLUMN_BLOCK = 2 * POOL_WIDTH // SSM_WIDTH


def _own_half():
    r = lax.broadcasted_iota(jnp.int32, (CHUNK_IN, LANES), 0) // SSM_GROUP % 2
    c = lax.broadcasted_iota(jnp.int32, (CHUNK_IN, LANES), 1) // SSM_STATE
    return (r == c)[None]


def _compact_weight(w):
    tiled = jnp.tile(w.reshape(SSM_CHUNKS, CHUNK_IN, SSM_STATE), (1, 1, 2))
    return jnp.where(_own_half(), tiled, 0.0)


def _compact_pair(w_a, w_b):
    return jnp.concatenate([_compact_weight(w_a), _compact_weight(w_b)], axis=-1).astype(BF16)


def _expand_grad(g):
    kept = jnp.where(_own_half(), g, 0.0)
    return kept.reshape(SSM_CHUNKS, CHUNK_IN, 2, SSM_STATE).sum(axis=2).reshape(N_SSM_GROUPS, SSM_GROUP, SSM_STATE)


TILES_PER_BLOCK = LANES // CH_PER_TILE
IN_BLOCKS = CHUNK_IN // LANES


def _tile_masks():
    j = lax.broadcasted_iota(jnp.int32, (CHUNK_TILES, LANES), 0) % TILES_PER_BLOCK
    lane = lax.broadcasted_iota(jnp.int32, (CHUNK_TILES, LANES), 1) // CH_PER_TILE
    return (j == lane).astype(F32)


def _tile_rows(ref, j, tt):
    return ref.at[j // TILES_PER_BLOCK, pl.ds(j, tt, stride=CHUNK_TILES), :]


def _spread(ref, v, masks):
    tt = v.shape[0]
    for j in range(CHUNK_TILES):
        block = LANES * (j // TILES_PER_BLOCK)
        _tile_rows(ref, j, tt)[...] = v[:, block:block + LANES] * masks[j:j + 1, :]
    return jnp.concatenate([ref[b] for b in range(IN_BLOCKS)], axis=1).astype(BF16)


def _gather(ref, full, masks):
    tt = full.shape[0] // CHUNK_TILES
    for b in range(IN_BLOCKS):
        ref[b] = full[:, b * LANES:(b + 1) * LANES]
    out = []
    for b in range(IN_BLOCKS):
        acc = None
        for j in range(b * TILES_PER_BLOCK, (b + 1) * TILES_PER_BLOCK):
            part = _tile_rows(ref, j, tt)[...] * masks[j:j + 1, :]
            acc = part if acc is None else acc + part
        out.append(acc)
    return jnp.concatenate(out, axis=1)


def _resident(shape):
    return pl.BlockSpec(shape, lambda i: (0,) * len(shape), pipeline_mode=pl.Buffered(1))


def _halves(ref, k, rows=slice(None)):
    return ref[k, rows, :LANES], ref[k, rows, LANES:]


def _ssm_fwd(proj, w2, c2, a2, d_skip):
    t = proj.shape[0]
    tt = _tile(t, SSM_ROWS)
    rows = tt * CHUNK_TILES

    def body(u_ref, w_ref, c_ref, a_ref, d_ref, y_ref, gel_ref, s_ref, carry, spread_ref, full_ref):
        @pl.when(pl.program_id(0) == 0)
        def _():
            carry[...] = jnp.zeros_like(carry)
            spread_ref[...] = jnp.zeros_like(spread_ref)

        mask = _tile_masks()
        u = u_ref[...]
        for k in range(SSM_CHUNKS):
            uk = _spread(spread_ref, u[:, k * CHUNK_IN:(k + 1) * CHUNK_IN], mask)
            s_ref[k] = jnp.dot(uk, w_ref[k], preferred_element_type=F32)

        abar = [_halves(a_ref, k) for k in range(SSM_CHUNKS)]

        def steps(i, state):
            for v in range(SCAN_STEPS):
                r = pl.ds(pl.multiple_of((i * SCAN_STEPS + v) * CHUNK_TILES, CHUNK_TILES), CHUNK_TILES)
                new = []
                for k, ((a_re, a_im), (s_re, s_im)) in enumerate(zip(abar, state)):
                    b_re, b_im = _halves(s_ref, k, r)
                    s_re, s_im = a_re * s_re - a_im * s_im + b_re, a_re * s_im + a_im * s_re + b_im
                    s_ref[k, r, :LANES] = s_re
                    s_ref[k, r, LANES:] = s_im
                    new.append((s_re, s_im))
                state = tuple(new)
            return state

        state = lax.fori_loop(0, tt // SCAN_STEPS, steps, tuple(_halves(carry, k) for k in range(SSM_CHUNKS)))
        for k, (s_re, s_im) in enumerate(state):
            carry[k, :, :LANES] = s_re
            carry[k, :, LANES:] = s_im

        for k in range(SSM_CHUNKS):
            cols = slice(k * CHUNK_IN, (k + 1) * CHUNK_IN)
            full = lax.dot_general(s_ref[k].astype(BF16), c_ref[k], DOT_NT, preferred_element_type=F32)
            y = _gather(full_ref, full, mask) + d_ref[:, cols] * u[:, cols]
            y_ref[:, cols] = y
            gel_ref[:, cols] = (0.5 * y * (1.0 + jnp.tanh(GELU_C * (y + GELU_A * y * y * y)))).astype(BF16)

    weight = _resident((SSM_CHUNKS, CHUNK_IN, PAIR))
    tokens = pl.BlockSpec((tt, SSM_WIDTH), lambda i: (i, 0))
    return pl.pallas_call(
        body, name="ssm_fwd", grid=(t // tt,),
        in_specs=[pl.BlockSpec((tt, SSM_WIDTH), lambda i: (i, U_COLUMN_BLOCK)), weight, weight,
                  _resident((SSM_CHUNKS, CHUNK_TILES, PAIR)), _resident((1, SSM_WIDTH))],
        out_specs=[tokens, tokens, pl.BlockSpec((SSM_CHUNKS, rows, PAIR), lambda i: (0, i, 0))],
        out_shape=[jax.ShapeDtypeStruct((t, SSM_WIDTH), F32), jax.ShapeDtypeStruct((t, SSM_WIDTH), BF16),
                   jax.ShapeDtypeStruct((SSM_CHUNKS, t * CHUNK_TILES, PAIR), F32)],
        scratch_shapes=[pltpu.VMEM((SSM_CHUNKS, CHUNK_TILES, PAIR), F32), pltpu.VMEM((IN_BLOCKS, rows, LANES), F32),
                        pltpu.VMEM((IN_BLOCKS, rows, LANES), F32)],
        compiler_params=_params(("arbitrary",)),
    )(proj, w2, c2, a2, d_skip)


def _ssm_bwd(dy, proj, s, w2, c2, a2, d_skip, dproj):
    t = dy.shape[0]
    tt = _tile(t, SSM_ROWS)
    rows = tt * CHUNK_TILES
    n_chunks = t // tt

    def body(dy_ref, u_ref, s_ref, w_ref, c_ref, a_ref, d_ref, _, du_ref, gc_ref, gw_ref, ga_ref, gd_ref, z_ref, carry,
             spread_ref, full_ref):
        @pl.when(pl.program_id(0) == 0)
        def _():
            for r in (carry, gc_ref, gw_ref, ga_ref, gd_ref, spread_ref):
                r[...] = jnp.zeros_like(r)

        mask = _tile_masks()
        dy_v = dy_ref[...]
        u = u_ref[...]
        gd_ref[...] += jnp.sum(dy_v * u, axis=0, keepdims=True)
        for k in range(SSM_CHUNKS):
            dk = _spread(spread_ref, dy_v[:, k * CHUNK_IN:(k + 1) * CHUNK_IN], mask)
            z_ref[k] = jnp.dot(dk, c_ref[k], preferred_element_type=F32)
            gc_ref[k] += lax.dot_general(dk, s_ref[k].astype(BF16), DOT_TN, preferred_element_type=F32)

        abar = [_halves(a_ref, k) for k in range(SSM_CHUNKS)]

        def steps(i, state):
            zs, gs = state
            for v in range(SCAN_STEPS):
                tok = tt - 1 - (i * SCAN_STEPS + v)
                r = pl.ds(pl.multiple_of(tok * CHUNK_TILES, CHUNK_TILES), CHUNK_TILES)
                new_z, new_g = [], []
                for k, ((a_re, a_im), (z_re, z_im), (g_re, g_im)) in enumerate(zip(abar, zs, gs)):
                    s_re, s_im = _halves(s_ref, k, r)
                    g_re = g_re + z_re * s_re + z_im * s_im
                    g_im = g_im + z_im * s_re - z_re * s_im
                    d_re, d_im = _halves(z_ref, k, r)
                    z_re, z_im = d_re + a_re * z_re + a_im * z_im, d_im + a_re * z_im - a_im * z_re
                    z_ref[k, r, :LANES] = z_re
                    z_ref[k, r, LANES:] = z_im
                    new_z.append((z_re, z_im))
                    new_g.append((g_re, g_im))
                zs, gs = tuple(new_z), tuple(new_g)
            return zs, gs

        zs, gs = lax.fori_loop(0, tt // SCAN_STEPS, steps,
                               (tuple(_halves(carry, k) for k in range(SSM_CHUNKS)),
                                tuple(_halves(ga_ref, k) for k in range(SSM_CHUNKS))))
        for k in range(SSM_CHUNKS):
            carry[k, :, :LANES], carry[k, :, LANES:] = zs[k]
            ga_ref[k, :, :LANES], ga_ref[k, :, LANES:] = gs[k]

        for k in range(SSM_CHUNKS):
            cols = slice(k * CHUNK_IN, (k + 1) * CHUNK_IN)
            zb = z_ref[k].astype(BF16)
            full = lax.dot_general(zb, w_ref[k], DOT_NT, preferred_element_type=F32)
            du_ref[:, cols] = (_gather(full_ref, full, mask) + d_ref[:, cols] * dy_v[:, cols]).astype(BF16)
            uk = _spread(spread_ref, u[:, cols], mask)
            gw_ref[k] += lax.dot_general(uk, zb, DOT_TN, preferred_element_type=F32)

    weight = _resident((SSM_CHUNKS, CHUNK_IN, PAIR))
    tokens = pl.BlockSpec((tt, SSM_WIDTH), lambda i: (n_chunks - 1 - i, 0))
    grad = pl.BlockSpec((SSM_CHUNKS, CHUNK_IN, PAIR), lambda i: (0, 0, 0))
    return pl.pallas_call(
        body, name="ssm_bwd", grid=(n_chunks,),
        in_specs=[tokens, pl.BlockSpec((tt, SSM_WIDTH), lambda i: (n_chunks - 1 - i, U_COLUMN_BLOCK)),
                  pl.BlockSpec((SSM_CHUNKS, rows, PAIR), lambda i: (0, n_chunks - 1 - i, 0)), weight, weight,
                  _resident((SSM_CHUNKS, CHUNK_TILES, PAIR)), _resident((1, SSM_WIDTH)),
                  pl.BlockSpec(memory_space=pl.ANY)],
        out_specs=[pl.BlockSpec((tt, SSM_WIDTH), lambda i: (n_chunks - 1 - i, U_COLUMN_BLOCK)), grad, grad,
                   pl.BlockSpec((SSM_CHUNKS, CHUNK_TILES, PAIR), lambda i: (0, 0, 0)),
                   pl.BlockSpec((1, SSM_WIDTH), lambda i: (0, 0))],
        out_shape=[jax.ShapeDtypeStruct(dproj.shape, dproj.dtype), jax.ShapeDtypeStruct((SSM_CHUNKS, CHUNK_IN, PAIR), F32),
                   jax.ShapeDtypeStruct((SSM_CHUNKS, CHUNK_IN, PAIR), F32),
                   jax.ShapeDtypeStruct((SSM_CHUNKS, CHUNK_TILES, PAIR), F32), jax.ShapeDtypeStruct((1, SSM_WIDTH), F32)],
        input_output_aliases={7: 0},
        scratch_shapes=[pltpu.VMEM((SSM_CHUNKS, rows, PAIR), F32), pltpu.VMEM((SSM_CHUNKS, CHUNK_TILES, PAIR), F32),
                        pltpu.VMEM((IN_BLOCKS, rows, LANES), F32), pltpu.VMEM((IN_BLOCKS, rows, LANES), F32)],
        compiler_params=_params(("arbitrary",)),
    )(dy, proj, s, w2, c2, a2, d_skip, dproj)


def _block(ref, axis, size, index):
    idx = [slice(None)] * len(ref.shape)
    idx[axis] = pl.ds(pl.multiple_of(index * size, size), size)
    return ref.at[tuple(idx)]


def _all_gather(name, shards, axes):
    n = len(shards)
    sizes = [s.shape[a] for s, a in zip(shards, axes)]

    def body(*refs):
        ins, outs = refs[:n], refs[n:2 * n]
        send_sems, recv_sems, local_sems = refs[2 * n:]
        x, y, c = (lax.axis_index(a) for a in MESH_AXES)
        me, sibling = (x, y, c), (x, y, 1 - c)
        chips = [(1 - x, y), (x, 1 - y), (1 - x, 1 - y)]

        def rows(i, dev):
            return _block(outs[i], axes[i], sizes[i], 4 * dev[0] + 2 * dev[1] + dev[2])

        def copy(i, k, block, to, src=None):
            return pltpu.make_async_remote_copy(
                src_ref=rows(i, block) if src is None else src, dst_ref=rows(i, block),
                send_sem=send_sems.at[7 * i + k], recv_sem=recv_sems.at[7 * i + k],
                device_id=to, device_id_type=MESH)

        mine = [pltpu.make_async_copy(ins[i], rows(i, me), local_sems.at[i]) for i in range(n)]
        for cp in mine:
            cp.start()
        first = []
        for i in range(n):
            first.append(copy(i, 0, me, sibling, src=ins[i]))
            first += [copy(i, 1 + j, me, (*chip, c), src=ins[i]) for j, chip in enumerate(chips)]
        for cp in first:
            cp.start()
        passed = []
        for i in range(n):
            for j, chip in enumerate(chips):
                copy(i, 1 + j, (*chip, c), me).wait_recv()
                fwd = copy(i, 4 + j, (*chip, c), sibling)
                fwd.start()
                passed.append(fwd)
        for i in range(n):
            copy(i, 0, sibling, me).wait_recv()
            for j, chip in enumerate(chips):
                copy(i, 4 + j, (*chip, 1 - c), me).wait_recv()
        for cp in first + passed:
            cp.wait_send()
        for cp in mine:
            cp.wait()

    out_shape = []
    for s, a in zip(shards, axes):
        shape = list(s.shape)
        shape[a] *= N_DEV
        out_shape.append(jax.ShapeDtypeStruct(tuple(shape), s.dtype))
    any_spec = pl.BlockSpec(memory_space=pl.ANY)
    return pl.pallas_call(
        body, name=name, out_shape=out_shape,
        in_specs=[any_spec] * n, out_specs=[any_spec] * n,
        scratch_shapes=[pltpu.SemaphoreType.DMA((7 * n,)), pltpu.SemaphoreType.DMA((7 * n,)),
                        pltpu.SemaphoreType.DMA((n,))],
    )(*shards)


HBM_SPEC = pl.BlockSpec(memory_space=pltpu.HBM)
SEM_SPEC = pl.BlockSpec(memory_space=pltpu.SEMAPHORE)
ANY_SPEC = pl.BlockSpec(memory_space=pl.ANY)
SPLIT_PARAMS = pltpu.CompilerParams(has_side_effects=pltpu.SideEffectType.DATAFLOW_SIDE_EFFECTING)
N_PEERS = N_DEV - 1
TOKEN = jax.ShapeDtypeStruct((SUBLANES, LANES), F32)
VMEM_SPEC = pl.BlockSpec(memory_space=pltpu.VMEM)


def _in_hbm(arrays):
    return [pltpu.with_memory_space_constraint(a, pltpu.HBM) for a in arrays]


def _peer(m):
    x, y, c = (lax.axis_index(a) for a in MESH_AXES)
    px = 1 - x if m & 4 else x
    py = 1 - y if m & 2 else y
    pc = 1 - c if m & 1 else c
    return (px, py, pc), 4 * px + 2 * py + pc


def _my_index():
    x, y, c = (lax.axis_index(a) for a in MESH_AXES)
    return 4 * x + 2 * y + c


def _gather_copies(shard_refs, full_refs, axes, send_sems, recv_sems):
    copies = []
    for i, (shard, full) in enumerate(zip(shard_refs, full_refs)):
        mine = _block(full, axes[i], shard.shape[axes[i]], _my_index())
        for m in range(1, N_DEV):
            peer, _ = _peer(m)
            copies.append(pltpu.make_async_remote_copy(
                src_ref=shard, dst_ref=mine, send_sem=send_sems.at[N_PEERS * i + m - 1],
                recv_sem=recv_sems.at[N_PEERS * i + m - 1], device_id=peer, device_id_type=MESH))
    return copies


def _gather_start(name, shards, axes, after):
    n = len(shards)

    def body(*refs):
        shard_refs = refs[:n]
        send_sems, recv_sems, local_sems = refs[n + 1:n + 4]
        full_refs = refs[2 * n + 4:3 * n + 4]
        refs[3 * n + 4][...] = jnp.zeros(TOKEN.shape, TOKEN.dtype)
        for i in range(n):
            pltpu.make_async_copy(shard_refs[i], _block(full_refs[i], axes[i], shard_refs[i].shape[axes[i]], _my_index()),
                                  local_sems.at[i]).start()
        for cp in _gather_copies(shard_refs, full_refs, axes, send_sems, recv_sems):
            cp.start()

    fulls = []
    for s, a in zip(shards, axes):
        shape = list(s.shape)
        shape[a] *= N_DEV
        fulls.append(pltpu.HBM(tuple(shape), s.dtype))
    out = pl.pallas_call(
        body, name=name,
        out_shape=(pltpu.SemaphoreType.DMA((N_PEERS * n,)), pltpu.SemaphoreType.DMA((N_PEERS * n,)),
                   pltpu.SemaphoreType.DMA((n,)), *[pltpu.HBM(s.shape, s.dtype) for s in shards], *fulls, TOKEN),
        in_specs=[HBM_SPEC] * n + [ANY_SPEC],
        out_specs=(SEM_SPEC, SEM_SPEC, SEM_SPEC, *[HBM_SPEC] * (2 * n), VMEM_SPEC),
        input_output_aliases={i: 3 + i for i in range(n)},
        compiler_params=SPLIT_PARAMS,
    )(*_in_hbm(shards), after)
    return out[:-1], out[-1]


def _gather_wait(name, started, indices, axes, after):
    send_sems, recv_sems, local_sems = started[:3]
    n_all = (len(started) - 3) // 2
    shards = [started[3 + i] for i in indices]
    fulls = [started[3 + n_all + i] for i in indices]
    n = len(indices)

    def body(*refs):
        shard_refs, full_refs = refs[:n], refs[n:2 * n]
        send_sems, recv_sems, local_sems = refs[2 * n:2 * n + 3]
        for j, i in enumerate(indices):
            mine = _block(full_refs[j], axes[j], shard_refs[j].shape[axes[j]], _my_index())
            pltpu.make_async_copy(shard_refs[j], mine, local_sems.at[i]).wait()
            for m in range(1, N_DEV):
                peer, _ = _peer(m)
                cp = pltpu.make_async_remote_copy(
                    src_ref=shard_refs[j], dst_ref=mine, send_sem=send_sems.at[N_PEERS * i + m - 1],
                    recv_sem=recv_sems.at[N_PEERS * i + m - 1], device_id=peer, device_id_type=MESH)
                cp.wait_send()
                cp.wait_recv()

    out = pl.pallas_call(
        body, name=name,
        out_shape=tuple(pltpu.HBM(a.shape, a.dtype) for a in shards + fulls),
        in_specs=[HBM_SPEC] * (2 * n) + [SEM_SPEC] * 3 + [ANY_SPEC], out_specs=tuple([HBM_SPEC] * (2 * n)),
        input_output_aliases={i: i for i in range(2 * n)},
        compiler_params=SPLIT_PARAMS,
    )(*shards, *fulls, send_sems, recv_sems, local_sems, after)
    return out[n:]


def _exchange_start(name, fulls, axes):
    n = len(fulls)
    sizes = [f.shape[a] // N_DEV for f, a in zip(fulls, axes)]

    def body(*refs):
        ins = refs[:n]
        send_sems, recv_sems = refs[n:n + 2]
        lands = refs[2 * n + 2:3 * n + 2]
        refs[3 * n + 2][...] = jnp.zeros(TOKEN.shape, TOKEN.dtype)
        for i in range(n):
            for m in range(1, N_DEV):
                peer, index = _peer(m)
                pltpu.make_async_remote_copy(
                    src_ref=_block(ins[i], axes[i], sizes[i], index), dst_ref=lands[i].at[m - 1],
                    send_sem=send_sems.at[N_PEERS * i + m - 1], recv_sem=recv_sems.at[N_PEERS * i + m - 1],
                    device_id=peer, device_id_type=MESH).start()

    lands = []
    for f, a, size in zip(fulls, axes, sizes):
        shape = list(f.shape)
        shape[a] = size
        lands.append(pltpu.HBM((N_PEERS, *shape), f.dtype))
    out = pl.pallas_call(
        body, name=name,
        out_shape=(pltpu.SemaphoreType.DMA((N_PEERS * n,)), pltpu.SemaphoreType.DMA((N_PEERS * n,)),
                   *[pltpu.HBM(f.shape, f.dtype) for f in fulls], *lands, TOKEN),
        in_specs=[HBM_SPEC] * n, out_specs=(SEM_SPEC, SEM_SPEC, *[HBM_SPEC] * (2 * n), VMEM_SPEC),
        input_output_aliases={i: 2 + i for i in range(n)},
        compiler_params=SPLIT_PARAMS,
    )(*_in_hbm(fulls))
    return out[:-1], out[-1]


def _exchange_wait(name, started, axes, after):
    send_sems, recv_sems = started[:2]
    n = (len(started) - 2) // 2
    fulls, lands = list(started[2:2 + n]), list(started[2 + n:])
    sizes = [f.shape[a] // N_DEV for f, a in zip(fulls, axes)]

    def body(*refs):
        ins, land_refs = refs[:n], refs[n:2 * n]
        send_sems, recv_sems = refs[2 * n:2 * n + 2]
        for i in range(n):
            for m in range(1, N_DEV):
                peer, index = _peer(m)
                cp = pltpu.make_async_remote_copy(
                    src_ref=_block(ins[i], axes[i], sizes[i], index), dst_ref=land_refs[i].at[m - 1],
                    send_sem=send_sems.at[N_PEERS * i + m - 1], recv_sem=recv_sems.at[N_PEERS * i + m - 1],
                    device_id=peer, device_id_type=MESH)
                cp.wait_send()
                cp.wait_recv()

    out = pl.pallas_call(
        body, name=name,
        out_shape=tuple(pltpu.HBM(a.shape, a.dtype) for a in fulls + lands),
        in_specs=[HBM_SPEC] * (2 * n) + [SEM_SPEC] * 2 + [ANY_SPEC], out_specs=tuple([HBM_SPEC] * (2 * n)),
        input_output_aliases={i: i for i in range(2 * n)},
        compiler_params=SPLIT_PARAMS,
    )(*fulls, *lands, send_sems, recv_sems, after)
    return out[:n], out[n:]


def _sum_parts(part_refs, ndim):
    g = None
    for p_ref in part_refs:
        stacked = len(p_ref.shape) > ndim
        terms = [p_ref[s] for s in range(p_ref.shape[0])] if stacked else [p_ref[...]]
        for term in terms:
            term = term.astype(F32)
            g = term if g is None else g + term
    return g


def _adamw_update(w_ref, m_ref, v_ref, g, g_ref, d_ref, nm_ref, nv_ref):
    c1 = 1.0 - ADAM_B1 ** ADAM_STEP
    c2 = 1.0 - ADAM_B2 ** ADAM_STEP
    new_m = ADAM_B1 * m_ref[...] + (1.0 - ADAM_B1) * g
    new_v = ADAM_B2 * v_ref[...] + (1.0 - ADAM_B2) * (g * g)
    g_ref[...] = g
    nm_ref[...] = new_m
    nv_ref[...] = new_v
    d_ref[...] = -ADAM_LR * ((new_m / c1) / (jnp.sqrt(new_v / c2) + ADAM_EPS) + ADAM_WD * w_ref[...])


def _adamw_small(ws, ms, vs, stacks, loss_stack):
    n = len(ws)

    def body(*refs):
        ins, outs = refs[:4 * n + 1], refs[4 * n + 1:]
        for i in range(n):
            _adamw_update(ins[i], ins[n + i], ins[2 * n + i], _sum_parts([ins[3 * n + i]], len(ins[i].shape)),
                          outs[i], outs[n + i], outs[2 * n + i], outs[3 * n + i])
        total = ins[4 * n][0]
        for dev in range(1, N_DEV):
            total = total + ins[4 * n][dev]
        outs[4 * n][...] = total

    res = pl.pallas_call(
        body, name="adamw_small",
        out_shape=[jax.ShapeDtypeStruct(w.shape, F32) for w in ws] * 4 + [jax.ShapeDtypeStruct((1, LANES), F32)],
        compiler_params=_params(None),
    )(*ws, *ms, *vs, *stacks, loss_stack)
    return res[:n], res[n:2 * n], res[2 * n:3 * n], res[3 * n:4 * n], res[4 * n]


def _adamw(name, w, m, v, parts, segments=1):
    r, c = w.shape
    tr = _tile(r // segments, 256)
    tiles = r // segments // tr
    flat = [seg for p in parts for seg in (p if segments > 1 else (p,))]
    n_flat = len(flat)

    def body(*refs):
        seg_refs = refs[3:3 + n_flat]
        g = _sum_parts(seg_refs[0::segments], 2)
        for sgm in range(1, segments):
            g = jnp.where(pl.program_id(0) // tiles == sgm, _sum_parts(seg_refs[sgm::segments], 2), g)
        _adamw_update(refs[0], refs[1], refs[2], g, *refs[3 + n_flat:])

    row = pl.BlockSpec((tr, c), lambda i: (i, 0))
    in_specs = [row, row, row]
    for p in flat:
        in_specs.append(pl.BlockSpec((tr, c), lambda i: (i % tiles, 0)) if p.ndim == 2
                        else pl.BlockSpec((p.shape[0], tr, c), lambda i: (0, i % tiles, 0)))
    return pl.pallas_call(
        body, name=name, grid=(r // tr,), in_specs=in_specs, out_specs=[row] * 4,
        out_shape=[jax.ShapeDtypeStruct((r, c), F32)] * 4,
        compiler_params=_params(("arbitrary",)),
    )(w, m, v, *flat)


SMALL = ("norm_gain", "pool_scale", "a_re", "a_im", "log_dt", "b_re", "b_im", "c_re", "c_im", "d_skip", "final_gain")
LARGE = ("w_in", "w_pool", "w_glu", "w_out", "w_ple", "w_ple_gate")
LARGE_AXIS = {"w_in": 1, "w_pool": 1, "w_glu": 1, "w_out": 0, "w_ple": 1, "w_ple_gate": 0}
W_IN_HALVES = ("w_in_top", "w_in_bottom")
GRAD_AXIS = {**LARGE_AXIS, **{k: LARGE_AXIS["w_in"] for k in W_IN_HALVES}}
WEIGHTS = ("norm_gain", "w_in", "w_pool", "pool_scale", "a_re", "a_im", "log_dt", "b_re", "b_im", "c_re", "c_im",
           "d_skip", "w_glu", "w_out", "w_ple", "w_ple_gate", "final_gain")


def kernel(x, p, norm_gain, w_in, w_pool, pool_scale, a_re, a_im, log_dt, b_re, b_im, c_re, c_im, d_skip, w_glu, w_out, w_ple, w_ple_gate, final_gain, loss_target, m_norm_gain, m_w_in, m_w_pool, m_pool_scale, m_a_re, m_a_im, m_log_dt, m_b_re, m_b_im, m_c_re, m_c_im, m_d_skip, m_w_glu, m_w_out, m_w_ple, m_w_ple_gate, m_final_gain, v_norm_gain, v_w_in, v_w_pool, v_pool_scale, v_a_re, v_a_im, v_log_dt, v_b_re, v_b_im, v_c_re, v_c_im, v_d_skip, v_w_glu, v_w_out, v_w_ple, v_w_ple_gate, v_final_gain):
    weights = dict(norm_gain=norm_gain, w_in=w_in, w_pool=w_pool, pool_scale=pool_scale, a_re=a_re, a_im=a_im,
                   log_dt=log_dt, b_re=b_re, b_im=b_im, c_re=c_re, c_im=c_im, d_skip=d_skip, w_glu=w_glu,
                   w_out=w_out, w_ple=w_ple, w_ple_gate=w_ple_gate, final_gain=final_gain)
    mom_m = dict(norm_gain=m_norm_gain, w_in=m_w_in, w_pool=m_w_pool, pool_scale=m_pool_scale, a_re=m_a_re,
                 a_im=m_a_im, log_dt=m_log_dt, b_re=m_b_re, b_im=m_b_im, c_re=m_c_re, c_im=m_c_im,
                 d_skip=m_d_skip, w_glu=m_w_glu, w_out=m_w_out, w_ple=m_w_ple, w_ple_gate=m_w_ple_gate,
                 final_gain=m_final_gain)
    mom_v = dict(norm_gain=v_norm_gain, w_in=v_w_in, w_pool=v_w_pool, pool_scale=v_pool_scale, a_re=v_a_re,
                 a_im=v_a_im, log_dt=v_log_dt, b_re=v_b_re, b_im=v_b_im, c_re=v_c_re, c_im=v_c_im,
                 d_skip=v_d_skip, w_glu=v_w_glu, w_out=v_w_out, w_ple=v_w_ple, w_ple_gate=v_w_ple_gate,
                 final_gain=v_final_gain)

    t = x.shape[1]
    xs = x.reshape(t, D_MODEL)
    ps = p.reshape(t, PLE_DIM)
    target = loss_target.reshape(t, D_MODEL)
    gain1 = norm_gain.reshape(1, D_MODEL)
    gain_f = final_gain.reshape(1, D_MODEL)
    scale_p = pool_scale.reshape(1, POOL_WIDTH)
    skip = d_skip.reshape(1, SSM_WIDTH)

    shard2d = {k: weights[k][0] for k in LARGE}
    shard_bf = {k: shard2d[k].astype(BF16) for k in LARGE}
    full = {"w_in": _all_gather("w_in_all_gather", [shard_bf["w_in"]], [LARGE_AXIS["w_in"]])[0]}
    later = [k for k in LARGE if k != "w_in"]
    later_axes = [LARGE_AXIS[k] for k in later]
    gather, gather_token = _gather_start("weights_gather_start", [shard_bf[k] for k in later], later_axes,
                                         full["w_in"])

    def arrive(k, after):
        i = later.index(k)
        full[k] = _gather_wait("gather_wait_" + k, gather, [i], [later_axes[i]], after)[0]

    ar, ai = a_re[0], a_im[0]
    ldt = log_dt.reshape(N_SSM_GROUPS, 1)
    br_t = jnp.transpose(b_re[0], (0, 2, 1))
    bi_t = jnp.transpose(b_im[0], (0, 2, 1))
    ab_re, ab_im, bb_re, bb_im = _ssm_params(ar, ai, ldt, br_t, bi_t)
    tiles = (SSM_CHUNKS, CHUNK_TILES, LANES)
    abar = jnp.concatenate([ab_re.reshape(tiles), ab_im.reshape(tiles)], axis=-1)
    w_pair = _compact_pair(bb_re, bb_im)
    c_pair = _compact_pair(c_re[0], -c_im[0])

    hn = _norm1_fwd(xs, gain1)
    proj = _mm_nn("in_proj", hn, full["w_in"], [F32], tk=2048, after=[gather_token])[0]
    pooled = _pool_fwd(proj)
    tm = _tile(t, 1024)
    arrive("w_pool", pooled)
    mixed = _mm("pool_mix", [(pooled, (tm, POOL_GROUP), lambda i, j, s: (i, j),
                              full["w_pool"], (None, POOL_GROUP, POOL_GROUP), lambda i, j, s: (j, 0, 0))],
                DOT_NN, (t // tm, N_POOL_GROUPS, 1),
                [((t, POOL_WIDTH), F32, (tm, POOL_GROUP), lambda i, j, s: (i, j))], 1)[0]
    y, gel, states = _ssm_fwd(proj, w_pair, c_pair, abar, skip)
    arrive("w_glu", gel)
    hg = _mm_nn("glu_proj", gel, full["w_glu"], [F32])[0]
    cat = _gate_fwd(mixed, proj, hg, scale_p)

    def residual_epilogue(acc, ex, out_refs):
        h = acc + ex[0][...]
        out_refs[0][...] = h
        out_refs[1][...] = h.astype(BF16)

    arrive("w_out", cat)
    h1, h1b = _mm_nn("out_proj", cat, full["w_out"], [F32, BF16], extras=[xs], epilogue=residual_epilogue)
    arrive("w_ple", h1b)
    arrive("w_ple_gate", h1b)
    de, dq, dh2, g_final_gain, loss_part = _ple_final(h1, h1b, ps, full["w_ple_gate"], full["w_ple"], target, gain_f)

    grads = {}
    grads["w_ple_gate"] = _mm_tn("ple_gate_wgrad", h1b, dq, BF16)
    grads["w_ple"] = _mm_tn("ple_wgrad", ps, de, BF16)
    sent, tokens = {}, {}

    def send(names):
        sent[names], tokens[names[0]] = _exchange_start(
            "grads_start_" + names[0], [grads[k] for k in names], [GRAD_AXIS[k] for k in names])

    send(("w_ple_gate", "w_ple"))
    dh1, dh1b = _mm_nt("ple_gate_dgrad", dq, full["w_ple_gate"], [F32, BF16], extras=[dh2],
                       epilogue=residual_epilogue)
    grads["w_out"] = _mm_tn("out_wgrad", cat, dh1b, BF16)
    send(("w_out",))
    dmixed, dproj, dhg, g_pool_scale = _out_dgrad_gate_bwd(
        dh1b, full["w_out"], mixed, proj, hg, scale_p, [tokens["w_ple_gate"], tokens["w_out"]])

    tk = _tile(t, 1024)
    grads["w_pool"] = _mm("pool_wgrad", [(pooled, (tk, POOL_GROUP), lambda i, j, s: (s, i),
                                          dmixed, (tk, POOL_GROUP), lambda i, j, s: (s, i))],
                          DOT_TN, (N_POOL_GROUPS, 1, t // tk),
                          [((N_POOL_GROUPS, POOL_GROUP, POOL_GROUP), BF16, (None, POOL_GROUP, POOL_GROUP),
                            lambda i, j, s: (i, 0, 0))], t // tk)[0]
    dpooled = _mm("pool_dgrad", [(dmixed, (tm, POOL_GROUP), lambda i, j, s: (i, j),
                                  full["w_pool"], (None, POOL_GROUP, POOL_GROUP), lambda i, j, s: (j, 0, 0))],
                  DOT_NT, (t // tm, N_POOL_GROUPS, 1),
                  [((t, POOL_WIDTH), F32, (tm, POOL_GROUP), lambda i, j, s: (i, j))], 1)[0]
    dproj = _pool_bwd(dpooled, dproj)

    grads["w_glu"] = _mm_tn("glu_wgrad", gel, dhg, BF16)
    send(("w_pool", "w_glu"))

    def gelu_bwd_epilogue(acc, ex, out_refs):
        yv = ex[0][...]
        th = jnp.tanh(GELU_C * (yv + GELU_A * yv * yv * yv))
        dgelu = 0.5 * (1.0 + th) + 0.5 * yv * (1.0 - th * th) * GELU_C * (1.0 + 3.0 * GELU_A * yv * yv)
        out_refs[0][...] = acc * dgelu

    dy = _mm_nt("glu_dgrad", dhg, full["w_glu"], [F32], tk=2048, extras=[y], epilogue=gelu_bwd_epilogue,
                after=[tokens["w_pool"]])[0]
    dproj, g_c_pair, g_w_pair, g_abar, g_d_skip = _ssm_bwd(dy, proj, states, w_pair, c_pair, abar, skip, dproj)

    g_ab_re = g_abar[..., :LANES].reshape(N_SSM_GROUPS, SSM_STATE)
    g_ab_im = g_abar[..., LANES:].reshape(N_SSM_GROUPS, SSM_STATE)
    d_ar, d_ai, d_ldt, d_br_t, d_bi_t = _ssm_params_bwd(
        ar, ai, ldt, br_t, bi_t, g_ab_re, g_ab_im,
        _expand_grad(g_w_pair[..., :LANES]), _expand_grad(g_w_pair[..., LANES:]))

    small_grads = dict(
        pool_scale=g_pool_scale, a_re=d_ar, a_im=d_ai, log_dt=d_ldt.reshape(1, N_SSM_GROUPS),
        b_re=d_br_t.astype(BF16), b_im=d_bi_t.astype(BF16), c_re=_expand_grad(g_c_pair[..., :LANES]).astype(BF16),
        c_im=(-_expand_grad(g_c_pair[..., LANES:])).astype(BF16), d_skip=g_d_skip, final_gain=g_final_gain)
    early = [k for k in SMALL if k != "norm_gain"]
    early_sent, early_token = _gather_start(
        "small_grads_start", [small_grads[k][None] for k in early] + [jnp.broadcast_to(loss_part, (1, 1, LANES))],
        [0] * (len(early) + 1), d_ar)

    half_rows = D_MODEL // len(W_IN_HALVES)
    order = [early_token]
    for h, k in enumerate(W_IN_HALVES):
        grads[k] = _mm_tn("in_wgrad_" + k, hn, dproj, BF16, after=order, a_cols=(h * half_rows, half_rows))
        send((k,))
        order = [tokens[k]]
    grad_x, g_norm_gain = _in_dgrad_norm1_bwd(dproj, full["w_in"], xs, dh1, gain1, order[0])
    late_sent, late_token = _gather_start("norm_gain_grad_start", [g_norm_gain[None]], [0], g_norm_gain)

    out_g, out_d, out_m, out_v = ({} for _ in range(4))
    me = 4 * lax.axis_index("x") + 2 * lax.axis_index("y") + lax.axis_index("c")
    after = late_token
    received = {}
    for names, started in sent.items():
        axes = [GRAD_AXIS[k] for k in names]
        partials, landed = _exchange_wait("grads_wait_" + names[0], started, axes, after)
        for k, axis, partial, land in zip(names, axes, partials, landed):
            size = partial.shape[axis] // N_DEV
            own = lax.dynamic_slice_in_dim(partial, me * size, size, axis=axis)
            received[k] = (own.reshape(-1, own.shape[-1]), land.reshape(N_PEERS, -1, land.shape[-1]))
        for k in names:
            if k == W_IN_HALVES[0]:
                continue
            if k == W_IN_HALVES[-1]:
                k, segments = "w_in", len(W_IN_HALVES)
                parts = [tuple(received[h][0] for h in W_IN_HALVES), tuple(received[h][1] for h in W_IN_HALVES)]
            else:
                segments, parts = 1, list(received[k])
            view = (-1, shard2d[k].shape[-1])
            res = _adamw("adamw_" + k, shard2d[k].reshape(view), mom_m[k][0].reshape(view), mom_v[k][0].reshape(view),
                         parts, segments)
            out_g[k], out_d[k], out_m[k], out_v[k] = (r.reshape(weights[k].shape) for r in res)
            after = res[0]

    def b_view(a):
        return jnp.transpose(a[0], (0, 2, 1))

    views = dict(norm_gain=lambda a: a, pool_scale=lambda a: a, a_re=lambda a: a[0], a_im=lambda a: a[0],
                 log_dt=lambda a: a, b_re=b_view, b_im=b_view, c_re=lambda a: a[0], c_im=lambda a: a[0],
                 d_skip=lambda a: a, final_gain=lambda a: a.reshape(1, D_MODEL))
    landed = _gather_wait("small_grads_wait", early_sent, list(range(len(early) + 1)), [0] * (len(early) + 1), after)
    stack = dict(zip(early, landed))
    stack["norm_gain"] = _gather_wait("norm_gain_grad_wait", late_sent, [0], [0], after)[0]
    *small_out, loss_row = _adamw_small(
        [views[k](weights[k]) for k in SMALL], [views[k](mom_m[k]) for k in SMALL],
        [views[k](mom_v[k]) for k in SMALL], [stack[k] for k in SMALL], landed[-1])
    loss = loss_row[0, 0]
    for out, res in zip((out_g, out_d, out_m, out_v), small_out):
        for k, r in zip(SMALL, res):
            if k in ("b_re", "b_im"):
                r = jnp.transpose(r, (0, 2, 1))
            out[k] = r.reshape(weights[k].shape)

    return (loss, grad_x.reshape(x.shape), *[out_g[k] for k in WEIGHTS], *[out_d[k] for k in WEIGHTS],
            *[out_m[k] for k in WEIGHTS], *[out_v[k] for k in WEIGHTS])
```

```python
import math

import jax
import jax.numpy as jnp
from jax import lax
from jax.experimental import pallas as pl
from jax.experimental.pallas import tpu as pltpu

F32 = jnp.float32
BF16 = jnp.bfloat16
MESH = pl.DeviceIdType.MESH
MESH_AXES = ("x", "y", "c")
N_DEV = 8

D_MODEL = 2048
POOL_WIDTH = 1024
SSM_WIDTH = 1024
N_POOL_GROUPS = 4
POOL_GROUP = 256
SSM_GROUP = 16
N_SSM_GROUPS = 64
SSM_STATE = 64
SSM_FLAT = N_SSM_GROUPS * SSM_STATE
SSM_CHUNKS = 4
CHUNK_IN = SSM_WIDTH // SSM_CHUNKS
CHUNK_STATE = SSM_FLAT // SSM_CHUNKS
PLE_DIM = 256
EPS = 1e-6
A_RE_MAX = -1e-4
ADAM_LR = 0.001
ADAM_B1 = 0.9
ADAM_B2 = 0.999
ADAM_EPS = 1e-08
ADAM_WD = 0.01
ADAM_STEP = 10
GELU_C = math.sqrt(2.0 / math.pi)
GELU_A = 0.044715

SUBLANES = 8
LANES = 128
VMEM_LIMIT_BYTES = 48 * 1024 * 1024

DOT_NN = (((1,), (0,)), ((), ()))
DOT_NT = (((1,), (1,)), ((), ()))
DOT_TN = (((0,), (0,)), ((), ()))


def _tile(n, pref):
    return pref if n % pref == 0 else n


def _params(sem):
    return pltpu.CompilerParams(dimension_semantics=sem, vmem_limit_bytes=VMEM_LIMIT_BYTES)


def _sigmoid(v):
    return 1.0 / (1.0 + jnp.exp(-v))


def _silu_and_grad(v):
    s = _sigmoid(v)
    return v * s, s * (1.0 + v * (1.0 - s))


def _mm(name, pairs, dims, grid, outs, k_steps, extras=(), epilogue=None):
    n_pairs, n_ex, n_out = len(pairs), len(extras), len(outs)
    acc_shape = tuple(d for d in outs[0][2] if d is not None)
    if epilogue is None:
        def epilogue(acc, ex, out_refs):
            out_refs[0][...] = acc.astype(out_refs[0].dtype)

    def body(*refs):
        ab = refs[:2 * n_pairs]
        ex = refs[2 * n_pairs:2 * n_pairs + n_ex]
        out_refs = refs[2 * n_pairs + n_ex:2 * n_pairs + n_ex + n_out]
        acc = refs[-1]
        k = pl.program_id(2)

        @pl.when(k == 0)
        def _():
            acc[...] = jnp.zeros_like(acc)

        part = None
        for q in range(n_pairs):
            d = lax.dot_general(ab[2 * q][...].astype(BF16), ab[2 * q + 1][...].astype(BF16), dims,
                                preferred_element_type=F32)
            part = d if part is None else part + d
        acc[...] += part

        @pl.when(k == k_steps - 1)
        def _():
            epilogue(acc[...], ex, out_refs)

    in_specs, operands = [], []
    for a, a_blk, a_map, b, b_blk, b_map in pairs:
        in_specs += [pl.BlockSpec(a_blk, a_map), pl.BlockSpec(b_blk, b_map)]
        operands += [a, b]
    for e, e_blk, e_map in extras:
        in_specs.append(pl.BlockSpec(e_blk, e_map))
        operands.append(e)
    return pl.pallas_call(
        body, name=name, grid=grid, in_specs=in_specs,
        out_specs=[pl.BlockSpec(o[2], o[3]) for o in outs],
        out_shape=[jax.ShapeDtypeStruct(o[0], o[1]) for o in outs],
        scratch_shapes=[pltpu.VMEM(acc_shape, F32)],
        compiler_params=_params(("arbitrary", "arbitrary", "arbitrary")),
    )(*operands)


def _after(tokens):
    return [(tok, tok.shape, lambda i, j, s: (0, 0)) for tok in tokens]


def _mm_nn(name, a, b, out_dtypes, tm=1024, tn=1024, tk=1024, a_col0=0, extras=(), epilogue=None, after=()):
    m, n = a.shape[0], b.shape[1]
    k = b.shape[0]
    tm, tn, tk = _tile(m, tm), _tile(n, tn), _tile(k, tk)
    outs = [((m, n), dt, (tm, tn), lambda i, j, s: (i, j)) for dt in out_dtypes]
    ex = [(e, (tm, tn), lambda i, j, s: (i, j)) for e in extras] + _after(after)
    return _mm(name, [(a, (tm, tk), lambda i, j, s: (i, a_col0 + s), b, (tk, tn), lambda i, j, s: (s, j))],
               DOT_NN, (m // tm, n // tn, k // tk), outs, k // tk, ex, epilogue)


def _mm_nt(name, a, b, out_dtypes, tm=1024, tn=1024, tk=1024, extras=(), epilogue=None, after=()):
    m, kk = a.shape
    n = b.shape[0]
    tm, tn, tk = _tile(m, tm), _tile(n, tn), _tile(kk, tk)
    outs = [((m, n), dt, (tm, tn), lambda i, j, s: (i, j)) for dt in out_dtypes]
    ex = [(e, (tm, tn), lambda i, j, s: (i, j)) for e in extras] + _after(after)
    return _mm(name, [(a, (tm, tk), lambda i, j, s: (i, s), b, (tn, tk), lambda i, j, s: (j, s))],
               DOT_NT, (m // tm, n // tn, kk // tk), outs, kk // tk, ex, epilogue)


def _mm_tn(name, a, b, out_dtype, tm=512, tn=2048, tk=1024, after=()):
    m, kk = a.shape
    n = b.shape[1]
    tm, tn, tk = _tile(kk, tm), _tile(n, tn), _tile(m, tk)
    outs = [((kk, n), out_dtype, (tm, tn), lambda i, j, s: (i, j))]
    return _mm(name, [(a, (tk, tm), lambda i, j, s: (s, i), b, (tk, tn), lambda i, j, s: (s, j))],
               DOT_TN, (kk // tm, n // tn, m // tk), outs, m // tk, _after(after))[0]


ROW_TILE = 256


def _norm1_in_proj(x, gain, w_in, after):
    t = x.shape[0]
    tm = _tile(t, ROW_TILE)
    n = w_in.shape[1]

    def body(x_ref, g_ref, w_ref, _, hn_ref, proj_ref):
        xv = x_ref[...]
        r = lax.rsqrt(jnp.mean(xv * xv, axis=-1, keepdims=True) + EPS)
        hn = (xv * r * g_ref[...]).astype(BF16)
        hn_ref[...] = hn
        proj_ref[...] = jnp.dot(hn, w_ref[...], preferred_element_type=F32)

    row = pl.BlockSpec((tm, D_MODEL), lambda i: (i, 0))
    return pl.pallas_call(
        body, name="norm1_in_proj", grid=(t // tm,),
        in_specs=[row, pl.BlockSpec((1, D_MODEL), lambda i: (0, 0)), _resident(w_in.shape),
                  pl.BlockSpec(after.shape, lambda i: (0, 0))],
        out_specs=[row, pl.BlockSpec((tm, n), lambda i: (i, 0))],
        out_shape=[jax.ShapeDtypeStruct((t, D_MODEL), BF16), jax.ShapeDtypeStruct((t, n), F32)],
        compiler_params=_params(("arbitrary",)),
    )(x, gain, w_in, after)


def _in_dgrad_norm1_bwd(dproj, w_in, x, dh1, gain, after):
    t = x.shape[0]
    tm = _tile(t, ROW_TILE)

    def body(dp_ref, w_ref, x_ref, dh1_ref, g_ref, _, dx_ref, gg_ref):
        @pl.when(pl.program_id(0) == 0)
        def _():
            gg_ref[...] = jnp.zeros_like(gg_ref)

        dhn = lax.dot_general(dp_ref[...], w_ref[...], DOT_NT, preferred_element_type=F32)
        xv = x_ref[...]
        r = lax.rsqrt(jnp.mean(xv * xv, axis=-1, keepdims=True) + EPS)
        xh = xv * r
        gg_ref[...] += jnp.sum(dhn * xh, axis=0, keepdims=True)
        dxh = dhn * g_ref[...]
        dx_ref[...] = dh1_ref[...] + r * (dxh - xh * jnp.mean(dxh * xh, axis=-1, keepdims=True))

    row = pl.BlockSpec((tm, D_MODEL), lambda i: (i, 0))
    vec = pl.BlockSpec((1, D_MODEL), lambda i: (0, 0))
    return pl.pallas_call(
        body, name="in_dgrad_norm1_bwd", grid=(t // tm,),
        in_specs=[pl.BlockSpec((tm, dproj.shape[1]), lambda i: (i, 0)), _resident(w_in.shape), row, row, vec,
                  pl.BlockSpec(after.shape, lambda i: (0, 0))],
        out_specs=[row, vec],
        out_shape=[jax.ShapeDtypeStruct((t, D_MODEL), F32), jax.ShapeDtypeStruct((1, D_MODEL), F32)],
        compiler_params=_params(("arbitrary",)),
    )(dproj, w_in, x, dh1, gain, after)


def _pool_counts(t, width, group):
    row = lax.broadcasted_iota(jnp.int32, (t, width), 0)
    window = jnp.left_shift(jnp.int32(2), group)
    return row, jnp.minimum(row + 1, window).astype(F32)


def _select_window(group, s2, s4, s8, s16):
    return jnp.where(group == 0, s2, jnp.where(group == 1, s4, jnp.where(group == 2, s8, s16)))


def _pool_fwd(proj):
    t = proj.shape[0]
    tc = LANES

    def body(u_ref, o_ref):
        group = pl.program_id(0) // (POOL_GROUP // tc)
        v = u_ref[...]
        row, count = _pool_counts(t, tc, group)

        def down(a, j):
            return jnp.where(row >= j, pltpu.roll(a, j, 0), 0.0)

        s2 = v + down(v, 1)
        s4 = s2 + down(s2, 2)
        s8 = s4 + down(s4, 4)
        s16 = s8 + down(s8, 8)
        o_ref[...] = (_select_window(group, s2, s4, s8, s16) / count - v).astype(BF16)

    return pl.pallas_call(
        body, name="pool_fwd", grid=(POOL_WIDTH // tc,),
        in_specs=[pl.BlockSpec((t, tc), lambda j: (0, j))],
        out_specs=pl.BlockSpec((t, tc), lambda j: (0, j)),
        out_shape=jax.ShapeDtypeStruct((t, POOL_WIDTH), BF16),
        compiler_params=_params(("arbitrary",)),
    )(proj)


def _pool_bwd(dpooled, dproj):
    t = dpooled.shape[0]
    tc = LANES

    def body(d_ref, _, o_ref):
        group = pl.program_id(0) // (POOL_GROUP // tc)
        dp = d_ref[...]
        row, count = _pool_counts(t, tc, group)
        r = dp / count

        def up(a, j):
            return jnp.where(row < t - j, pltpu.roll(a, t - j, 0), 0.0)

        s2 = r + up(r, 1)
        s4 = s2 + up(s2, 2)
        s8 = s4 + up(s4, 4)
        s16 = s8 + up(s8, 8)
        o_ref[...] = (_select_window(group, s2, s4, s8, s16) - dp).astype(BF16)

    return pl.pallas_call(
        body, name="pool_bwd", grid=(POOL_WIDTH // tc,),
        in_specs=[pl.BlockSpec((t, tc), lambda j: (0, j)), pl.BlockSpec(memory_space=pl.ANY)],
        out_specs=pl.BlockSpec((t, tc), lambda j: (0, j)),
        out_shape=jax.ShapeDtypeStruct(dproj.shape, dproj.dtype),
        input_output_aliases={1: 0},
        compiler_params=_params(("arbitrary",)),
    )(dpooled, dproj)


def _gate_out_proj(mixed, proj, hg, pool_scale, w_out, x):
    t = mixed.shape[0]
    tm = _tile(t, ROW_TILE)

    def body(mx_ref, ga_ref, gb_ref, hg_ref, ps_ref, w_ref, x_ref, cat_ref, h1_ref, h1b_ref):
        silu_a, _ = _silu_and_grad(ga_ref[...])
        cat_ref[:, :POOL_WIDTH] = (mx_ref[...] * ps_ref[...] * silu_a).astype(BF16)
        silu_b, _ = _silu_and_grad(gb_ref[...])
        sb = hg_ref[:, :SSM_WIDTH] * _sigmoid(hg_ref[:, SSM_WIDTH:])
        cat_ref[:, POOL_WIDTH:] = (sb * silu_b).astype(BF16)
        h1 = x_ref[...] + jnp.dot(cat_ref[...], w_ref[...], preferred_element_type=F32)
        h1_ref[...] = h1
        h1b_ref[...] = h1.astype(BF16)

    row = pl.BlockSpec((tm, D_MODEL), lambda i: (i, 0))
    return pl.pallas_call(
        body, name="gate_out_proj", grid=(t // tm,),
        in_specs=[pl.BlockSpec((tm, POOL_WIDTH), lambda i: (i, 0)),
                  pl.BlockSpec((tm, POOL_WIDTH), lambda i: (i, 1)),
                  pl.BlockSpec((tm, SSM_WIDTH), lambda i: (i, 3)),
                  pl.BlockSpec((tm, 2 * SSM_WIDTH), lambda i: (i, 0)),
                  pl.BlockSpec((1, POOL_WIDTH), lambda i: (0, 0)), _resident(w_out.shape), row],
        out_specs=[row, row, row],
        out_shape=[jax.ShapeDtypeStruct((t, D_MODEL), BF16), jax.ShapeDtypeStruct((t, D_MODEL), F32),
                   jax.ShapeDtypeStruct((t, D_MODEL), BF16)],
        compiler_params=_params(("arbitrary",)),
    )(mixed, proj, proj, hg, pool_scale, w_out, x)


def _residual_dgrad(name, dy, w, residual):
    t = dy.shape[0]
    tm = _tile(t, ROW_TILE)
    n = w.shape[0]

    def body(dy_ref, w_ref, r_ref, o_ref, ob_ref):
        o = r_ref[...] + lax.dot_general(dy_ref[...], w_ref[...], DOT_NT, preferred_element_type=F32)
        o_ref[...] = o
        ob_ref[...] = o.astype(BF16)

    out = pl.BlockSpec((tm, n), lambda i: (i, 0))
    return pl.pallas_call(
        body, name=name, grid=(t // tm,),
        in_specs=[pl.BlockSpec((tm, dy.shape[1]), lambda i: (i, 0)), _resident(w.shape), out],
        out_specs=[out, out],
        out_shape=[jax.ShapeDtypeStruct((t, n), F32), jax.ShapeDtypeStruct((t, n), BF16)],
        compiler_params=_params(("arbitrary",)),
    )(dy, w, residual)


def _out_dgrad_gate_bwd(dh1b, w_out, mixed, proj, hg, pool_scale, after):
    t = mixed.shape[0]
    tm = _tile(t, ROW_TILE)
    n_after = len(after)

    def body(dh_ref, w_ref, mx_ref, ga_ref, gb_ref, hg_ref, ps_ref, *rest):
        dmx_ref, dp_ref, dhg_ref, gps_ref = rest[n_after:]

        @pl.when(pl.program_id(0) == 0)
        def _():
            gps_ref[...] = jnp.zeros_like(gps_ref)

        dcat = lax.dot_general(dh_ref[...], w_ref[...], DOT_NT, preferred_element_type=F32)
        ps = ps_ref[...]
        mx = mx_ref[...]
        dya = dcat[:, :POOL_WIDTH]
        silu_a, dsilu_a = _silu_and_grad(ga_ref[...])
        dpa = dya * silu_a
        gps_ref[...] += jnp.sum(dpa * mx, axis=0, keepdims=True)
        dmx_ref[...] = (dpa * ps).astype(BF16)
        dp_ref[:, :POOL_WIDTH] = jnp.zeros((tm, POOL_WIDTH), BF16)
        dp_ref[:, POOL_WIDTH:2 * POOL_WIDTH] = (dya * mx * ps * dsilu_a).astype(BF16)

        dyb = dcat[:, POOL_WIDTH:]
        silu_b, dsilu_b = _silu_and_grad(gb_ref[...])
        h_a = hg_ref[:, :SSM_WIDTH]
        sg = _sigmoid(hg_ref[:, SSM_WIDTH:])
        dsb = dyb * silu_b
        dp_ref[:, 2 * POOL_WIDTH:2 * POOL_WIDTH + SSM_WIDTH] = jnp.zeros((tm, SSM_WIDTH), BF16)
        dp_ref[:, 2 * POOL_WIDTH + SSM_WIDTH:] = (dyb * h_a * sg * dsilu_b).astype(BF16)
        dhg_ref[:, :SSM_WIDTH] = (dsb * sg).astype(BF16)
        dhg_ref[:, SSM_WIDTH:] = (dsb * h_a * sg * (1.0 - sg)).astype(BF16)

    half = pl.BlockSpec((tm, POOL_WIDTH), lambda i: (i, 0))
    full = pl.BlockSpec((tm, D_MODEL), lambda i: (i, 0))
    vec = pl.BlockSpec((1, POOL_WIDTH), lambda i: (0, 0))
    proj_width = 2 * POOL_WIDTH + 2 * SSM_WIDTH
    return pl.pallas_call(
        body, name="out_dgrad_gate_bwd", grid=(t // tm,),
        in_specs=[full, _resident(w_out.shape), half,
                  pl.BlockSpec((tm, POOL_WIDTH), lambda i: (i, 1)),
                  pl.BlockSpec((tm, SSM_WIDTH), lambda i: (i, 3)),
                  full, vec] + [pl.BlockSpec(tok.shape, lambda i: (0, 0)) for tok in after],
        out_specs=[half, pl.BlockSpec((tm, proj_width), lambda i: (i, 0)), full, vec],
        out_shape=[jax.ShapeDtypeStruct((t, POOL_WIDTH), BF16), jax.ShapeDtypeStruct((t, proj_width), BF16),
                   jax.ShapeDtypeStruct((t, 2 * SSM_WIDTH), BF16),
                   jax.ShapeDtypeStruct((1, POOL_WIDTH), F32)],
        compiler_params=_params(("arbitrary",)),
    )(dh1b, w_out, mixed, proj, proj, hg, pool_scale, *after)


def _ple_final(h1, h1b, p, w_gate, w_ple, target, gain):
    t = h1.shape[0]
    tm = _tile(t, 256)

    def body(h1_ref, h1b_ref, p_ref, wg_ref, wp_ref, tg_ref, g_ref, de_ref, dq_ref, dh2_ref, gg_ref, loss_ref):
        @pl.when(pl.program_id(0) == 0)
        def _():
            gg_ref[...] = jnp.zeros_like(gg_ref)
            loss_ref[...] = jnp.zeros_like(loss_ref)

        ev = jnp.dot(p_ref[...].astype(BF16), wp_ref[...], preferred_element_type=F32)
        sg = _sigmoid(jnp.dot(h1b_ref[...], wg_ref[...], preferred_element_type=F32))
        h2 = h1_ref[...] + ev * sg
        r = lax.rsqrt(jnp.mean(h2 * h2, axis=-1, keepdims=True) + EPS)
        n = h2 * r
        gain_v = g_ref[...]
        diff = n * gain_v - tg_ref[...]
        row_loss = jnp.sum(diff * diff, axis=-1, keepdims=True)
        loss_ref[...] += (0.5 / D_MODEL) * jnp.sum(row_loss, axis=0, keepdims=True)
        dout = diff * (1.0 / D_MODEL)
        gg_ref[...] += jnp.sum(dout * n, axis=0, keepdims=True)
        dn = dout * gain_v
        dh2 = r * (dn - n * jnp.mean(dn * n, axis=-1, keepdims=True))
        dh2_ref[...] = dh2
        de_ref[...] = (dh2 * sg).astype(BF16)
        dq_ref[...] = (dh2 * ev * sg * (1.0 - sg)).astype(BF16)

    row = pl.BlockSpec((tm, D_MODEL), lambda i: (i, 0))
    vec = pl.BlockSpec((1, D_MODEL), lambda i: (0, 0))
    return pl.pallas_call(
        body, name="ple_final", grid=(t // tm,),
        in_specs=[row, row, pl.BlockSpec((tm, PLE_DIM), lambda i: (i, 0)), _resident((D_MODEL, D_MODEL)),
                  _resident((PLE_DIM, D_MODEL)), row, vec],
        out_specs=[row, row, row, vec, pl.BlockSpec((1, 1), lambda i: (0, 0))],
        out_shape=[jax.ShapeDtypeStruct((t, D_MODEL), BF16), jax.ShapeDtypeStruct((t, D_MODEL), BF16),
                   jax.ShapeDtypeStruct((t, D_MODEL), F32), jax.ShapeDtypeStruct((1, D_MODEL), F32),
                   jax.ShapeDtypeStruct((1, 1), F32)],
        compiler_params=_params(("arbitrary",)),
    )(h1, h1b, p, w_gate, w_ple, target, gain)


def _zoh(a_re, a_im, log_dt, b_re_t, b_im_t):
    lam_re = jnp.minimum(a_re, A_RE_MAX)
    lam_im = a_im
    dt = jnp.exp(log_dt)
    mag = jnp.exp(lam_re * dt)
    ang = lam_im * dt
    ab_re = mag * jnp.cos(ang)
    ab_im = mag * jnp.sin(ang)
    den = lam_re * lam_re + lam_im * lam_im
    n_re = ab_re - 1.0
    n_im = ab_im
    q_re = (n_re * lam_re + n_im * lam_im) / den
    q_im = (n_im * lam_re - n_re * lam_im) / den
    bb_re = q_re[:, None, :] * b_re_t - q_im[:, None, :] * b_im_t
    bb_im = q_re[:, None, :] * b_im_t + q_im[:, None, :] * b_re_t
    return ab_re, ab_im, bb_re, bb_im


def _ssm_params(a_re, a_im, log_dt, b_re_t, b_im_t):
    def body(are_ref, aim_ref, dt_ref, bre_ref, bim_ref, abre_ref, abim_ref, bbre_ref, bbim_ref):
        ab_re, ab_im, bb_re, bb_im = _zoh(are_ref[...], aim_ref[...], dt_ref[...], bre_ref[...], bim_ref[...])
        abre_ref[...] = ab_re
        abim_ref[...] = ab_im
        bbre_ref[...] = bb_re
        bbim_ref[...] = bb_im

    return pl.pallas_call(
        body, name="ssm_params",
        out_shape=[jax.ShapeDtypeStruct(a_re.shape, F32), jax.ShapeDtypeStruct(a_re.shape, F32),
                   jax.ShapeDtypeStruct(b_re_t.shape, F32), jax.ShapeDtypeStruct(b_re_t.shape, F32)],
        compiler_params=_params(None),
    )(a_re, a_im, log_dt, b_re_t, b_im_t)


def _ssm_params_bwd(a_re, a_im, log_dt, b_re_t, b_im_t, g_ab_re, g_ab_im, g_bb_re, g_bb_im):
    def body(are_ref, aim_ref, dt_ref, bre_ref, bim_ref, gar_ref, gai_ref, gbr_ref, gbi_ref,
             o_are, o_aim, o_dt, o_bre, o_bim):
        _, vjp = jax.vjp(_zoh, are_ref[...], aim_ref[...], dt_ref[...], bre_ref[...], bim_ref[...])
        d_are, d_aim, d_dt, d_bre, d_bim = vjp((gar_ref[...], gai_ref[...], gbr_ref[...], gbi_ref[...]))
        o_are[...] = d_are
        o_aim[...] = d_aim
        o_dt[...] = d_dt
        o_bre[...] = d_bre
        o_bim[...] = d_bim

    ins = (a_re, a_im, log_dt, b_re_t, b_im_t)
    return pl.pallas_call(
        body, name="ssm_params_bwd",
        out_shape=[jax.ShapeDtypeStruct(v.shape, F32) for v in ins],
        compiler_params=_params(None),
    )(*ins, g_ab_re, g_ab_im, g_bb_re, g_bb_im)


CHUNK_TILES = CHUNK_STATE // LANES
CH_PER_TILE = CHUNK_IN // CHUNK_TILES
PAIR = 2 * LANES
SSM_ROWS = 256
SCAN_STEPS = 8
U_COLUMN_BLOCK = 2 * POOL_WIDTH // SSM_WIDTH


def _own_half():
    r = lax.broadcasted_iota(jnp.int32, (CHUNK_IN, LANES), 0) // SSM_GROUP % 2
    c = lax.broadcasted_iota(jnp.int32, (CHUNK_IN, LANES), 1) // SSM_STATE
    return (r == c)[None]


def _compact_weight(w):
    tiled = jnp.tile(w.reshape(SSM_CHUNKS, CHUNK_IN, SSM_STATE), (1, 1, 2))
    return jnp.where(_own_half(), tiled, 0.0)


def _compact_pair(w_a, w_b):
    return jnp.concatenate([_compact_weight(w_a), _compact_weight(w_b)], axis=-1).astype(BF16)


def _expand_grad(g):
    kept = jnp.where(_own_half(), g, 0.0)
    return kept.reshape(SSM_CHUNKS, CHUNK_IN, 2, SSM_STATE).sum(axis=2).reshape(N_SSM_GROUPS, SSM_GROUP, SSM_STATE)


TILES_PER_BLOCK = LANES // CH_PER_TILE
IN_BLOCKS = CHUNK_IN // LANES


def _tile_masks():
    j = lax.broadcasted_iota(jnp.int32, (CHUNK_TILES, LANES), 0) % TILES_PER_BLOCK
    lane = lax.broadcasted_iota(jnp.int32, (CHUNK_TILES, LANES), 1) // CH_PER_TILE
    return (j == lane).astype(F32)


def _tile_rows(ref, j, tt):
    return ref.at[j // TILES_PER_BLOCK, pl.ds(j, tt, stride=CHUNK_TILES), :]


def _spread(ref, v, masks):
    tt = v.shape[0]
    for j in range(CHUNK_TILES):
        block = LANES * (j // TILES_PER_BLOCK)
        _tile_rows(ref, j, tt)[...] = v[:, block:block + LANES] * masks[j:j + 1, :]
    return jnp.concatenate([ref[b] for b in range(IN_BLOCKS)], axis=1).astype(BF16)


def _gather(ref, full, masks):
    tt = full.shape[0] // CHUNK_TILES
    for b in range(IN_BLOCKS):
        ref[b] = full[:, b * LANES:(b + 1) * LANES]
    out = []
    for b in range(IN_BLOCKS):
        acc = None
        for j in range(b * TILES_PER_BLOCK, (b + 1) * TILES_PER_BLOCK):
            part = _tile_rows(ref, j, tt)[...] * masks[j:j + 1, :]
            acc = part if acc is None else acc + part
        out.append(acc)
    return jnp.concatenate(out, axis=1)


def _resident(shape):
    return pl.BlockSpec(shape, lambda i: (0,) * len(shape), pipeline_mode=pl.Buffered(1))


def _halves(ref, k, rows=slice(None)):
    return ref[k, rows, :LANES], ref[k, rows, LANES:]


def _ssm_fwd(proj, w2, c2, a2, d_skip):
    t = proj.shape[0]
    tt = _tile(t, SSM_ROWS)
    rows = tt * CHUNK_TILES

    def body(u_ref, w_ref, c_ref, a_ref, d_ref, y_ref, gel_ref, s_ref, carry, spread_ref, full_ref):
        @pl.when(pl.program_id(0) == 0)
        def _():
            carry[...] = jnp.zeros_like(carry)
            spread_ref[...] = jnp.zeros_like(spread_ref)

        mask = _tile_masks()
        u = u_ref[...]
        for k in range(SSM_CHUNKS):
            uk = _spread(spread_ref, u[:, k * CHUNK_IN:(k + 1) * CHUNK_IN], mask)
            s_ref[k] = jnp.dot(uk, w_ref[k], preferred_element_type=F32)

        abar = [_halves(a_ref, k) for k in range(SSM_CHUNKS)]

        def steps(i, state):
            for v in range(SCAN_STEPS):
                r = pl.ds(pl.multiple_of((i * SCAN_STEPS + v) * CHUNK_TILES, CHUNK_TILES), CHUNK_TILES)
                new = []
                for k, ((a_re, a_im), (s_re, s_im)) in enumerate(zip(abar, state)):
                    b_re, b_im = _halves(s_ref, k, r)
                    s_re, s_im = a_re * s_re - a_im * s_im + b_re, a_re * s_im + a_im * s_re + b_im
                    s_ref[k, r, :LANES] = s_re
                    s_ref[k, r, LANES:] = s_im
                    new.append((s_re, s_im))
                state = tuple(new)
            return state

        state = lax.fori_loop(0, tt // SCAN_STEPS, steps, tuple(_halves(carry, k) for k in range(SSM_CHUNKS)))
        for k, (s_re, s_im) in enumerate(state):
            carry[k, :, :LANES] = s_re
            carry[k, :, LANES:] = s_im

        for k in range(SSM_CHUNKS):
            cols = slice(k * CHUNK_IN, (k + 1) * CHUNK_IN)
            full = lax.dot_general(s_ref[k].astype(BF16), c_ref[k], DOT_NT, preferred_element_type=F32)
            y = _gather(full_ref, full, mask) + d_ref[:, cols] * u[:, cols]
            y_ref[:, cols] = y
            gel_ref[:, cols] = (0.5 * y * (1.0 + jnp.tanh(GELU_C * (y + GELU_A * y * y * y)))).astype(BF16)

    weight = _resident((SSM_CHUNKS, CHUNK_IN, PAIR))
    tokens = pl.BlockSpec((tt, SSM_WIDTH), lambda i: (i, 0))
    return pl.pallas_call(
        body, name="ssm_fwd", grid=(t // tt,),
        in_specs=[pl.BlockSpec((tt, SSM_WIDTH), lambda i: (i, U_COLUMN_BLOCK)), weight, weight,
                  _resident((SSM_CHUNKS, CHUNK_TILES, PAIR)), _resident((1, SSM_WIDTH))],
        out_specs=[tokens, tokens, pl.BlockSpec((SSM_CHUNKS, rows, PAIR), lambda i: (0, i, 0))],
        out_shape=[jax.ShapeDtypeStruct((t, SSM_WIDTH), F32), jax.ShapeDtypeStruct((t, SSM_WIDTH), BF16),
                   jax.ShapeDtypeStruct((SSM_CHUNKS, t * CHUNK_TILES, PAIR), F32)],
        scratch_shapes=[pltpu.VMEM((SSM_CHUNKS, CHUNK_TILES, PAIR), F32), pltpu.VMEM((IN_BLOCKS, rows, LANES), F32),
                        pltpu.VMEM((IN_BLOCKS, rows, LANES), F32)],
        compiler_params=_params(("arbitrary",)),
    )(proj, w2, c2, a2, d_skip)


def _ssm_bwd(dy, proj, s, w2, c2, a2, d_skip, dproj):
    t = dy.shape[0]
    tt = _tile(t, SSM_ROWS)
    rows = tt * CHUNK_TILES
    n_chunks = t // tt

    def body(dy_ref, u_ref, s_ref, w_ref, c_ref, a_ref, d_ref, _, du_ref, gc_ref, gw_ref, ga_ref, gd_ref, z_ref, carry,
             spread_ref, full_ref):
        @pl.when(pl.program_id(0) == 0)
        def _():
            for r in (carry, gc_ref, gw_ref, ga_ref, gd_ref, spread_ref):
                r[...] = jnp.zeros_like(r)

        mask = _tile_masks()
        dy_v = dy_ref[...]
        u = u_ref[...]
        gd_ref[...] += jnp.sum(dy_v * u, axis=0, keepdims=True)
        for k in range(SSM_CHUNKS):
            dk = _spread(spread_ref, dy_v[:, k * CHUNK_IN:(k + 1) * CHUNK_IN], mask)
            z_ref[k] = jnp.dot(dk, c_ref[k], preferred_element_type=F32)
            gc_ref[k] += lax.dot_general(dk, s_ref[k].astype(BF16), DOT_TN, preferred_element_type=F32)

        abar = [_halves(a_ref, k) for k in range(SSM_CHUNKS)]

        def steps(i, state):
            zs, gs = state
            for v in range(SCAN_STEPS):
                tok = tt - 1 - (i * SCAN_STEPS + v)
                r = pl.ds(pl.multiple_of(tok * CHUNK_TILES, CHUNK_TILES), CHUNK_TILES)
                new_z, new_g = [], []
                for k, ((a_re, a_im), (z_re, z_im), (g_re, g_im)) in enumerate(zip(abar, zs, gs)):
                    s_re, s_im = _halves(s_ref, k, r)
                    g_re = g_re + z_re * s_re + z_im * s_im
                    g_im = g_im + z_im * s_re - z_re * s_im
                    d_re, d_im = _halves(z_ref, k, r)
                    z_re, z_im = d_re + a_re * z_re + a_im * z_im, d_im + a_re * z_im - a_im * z_re
                    z_ref[k, r, :LANES] = z_re
                    z_ref[k, r, LANES:] = z_im
                    new_z.append((z_re, z_im))
                    new_g.append((g_re, g_im))
                zs, gs = tuple(new_z), tuple(new_g)
            return zs, gs

        zs, gs = lax.fori_loop(0, tt // SCAN_STEPS, steps,
                               (tuple(_halves(carry, k) for k in range(SSM_CHUNKS)),
                                tuple(_halves(ga_ref, k) for k in range(SSM_CHUNKS))))
        for k in range(SSM_CHUNKS):
            carry[k, :, :LANES], carry[k, :, LANES:] = zs[k]
            ga_ref[k, :, :LANES], ga_ref[k, :, LANES:] = gs[k]

        for k in range(SSM_CHUNKS):
            cols = slice(k * CHUNK_IN, (k + 1) * CHUNK_IN)
            zb = z_ref[k].astype(BF16)
            full = lax.dot_general(zb, w_ref[k], DOT_NT, preferred_element_type=F32)
            du_ref[:, cols] = (_gather(full_ref, full, mask) + d_ref[:, cols] * dy_v[:, cols]).astype(BF16)
            uk = _spread(spread_ref, u[:, cols], mask)
            gw_ref[k] += lax.dot_general(uk, zb, DOT_TN, preferred_element_type=F32)

    weight = _resident((SSM_CHUNKS, CHUNK_IN, PAIR))
    tokens = pl.BlockSpec((tt, SSM_WIDTH), lambda i: (n_chunks - 1 - i, 0))
    grad = pl.BlockSpec((SSM_CHUNKS, CHUNK_IN, PAIR), lambda i: (0, 0, 0))
    return pl.pallas_call(
        body, name="ssm_bwd", grid=(n_chunks,),
        in_specs=[tokens, pl.BlockSpec((tt, SSM_WIDTH), lambda i: (n_chunks - 1 - i, U_COLUMN_BLOCK)),
                  pl.BlockSpec((SSM_CHUNKS, rows, PAIR), lambda i: (0, n_chunks - 1 - i, 0)), weight, weight,
                  _resident((SSM_CHUNKS, CHUNK_TILES, PAIR)), _resident((1, SSM_WIDTH)),
                  pl.BlockSpec(memory_space=pl.ANY)],
        out_specs=[pl.BlockSpec((tt, SSM_WIDTH), lambda i: (n_chunks - 1 - i, U_COLUMN_BLOCK)), grad, grad,
                   pl.BlockSpec((SSM_CHUNKS, CHUNK_TILES, PAIR), lambda i: (0, 0, 0)),
                   pl.BlockSpec((1, SSM_WIDTH), lambda i: (0, 0))],
        out_shape=[jax.ShapeDtypeStruct(dproj.shape, dproj.dtype), jax.ShapeDtypeStruct((SSM_CHUNKS, CHUNK_IN, PAIR), F32),
                   jax.ShapeDtypeStruct((SSM_CHUNKS, CHUNK_IN, PAIR), F32),
                   jax.ShapeDtypeStruct((SSM_CHUNKS, CHUNK_TILES, PAIR), F32), jax.ShapeDtypeStruct((1, SSM_WIDTH), F32)],
        input_output_aliases={7: 0},
        scratch_shapes=[pltpu.VMEM((SSM_CHUNKS, rows, PAIR), F32), pltpu.VMEM((SSM_CHUNKS, CHUNK_TILES, PAIR), F32),
                        pltpu.VMEM((IN_BLOCKS, rows, LANES), F32), pltpu.VMEM((IN_BLOCKS, rows, LANES), F32)],
        compiler_params=_params(("arbitrary",)),
    )(dy, proj, s, w2, c2, a2, d_skip, dproj)


def _block(ref, axis, size, index):
    idx = [slice(None)] * len(ref.shape)
    idx[axis] = pl.ds(pl.multiple_of(index * size, size), size)
    return ref.at[tuple(idx)]


def _all_gather(name, shards, axes):
    n = len(shards)
    sizes = [s.shape[a] for s, a in zip(shards, axes)]

    def body(*refs):
        ins, outs = refs[:n], refs[n:2 * n]
        send_sems, recv_sems, local_sems = refs[2 * n:]
        x, y, c = (lax.axis_index(a) for a in MESH_AXES)
        me, sibling = (x, y, c), (x, y, 1 - c)
        chips = [(1 - x, y), (x, 1 - y), (1 - x, 1 - y)]

        def rows(i, dev):
            return _block(outs[i], axes[i], sizes[i], 4 * dev[0] + 2 * dev[1] + dev[2])

        def copy(i, k, block, to, src=None):
            return pltpu.make_async_remote_copy(
                src_ref=rows(i, block) if src is None else src, dst_ref=rows(i, block),
                send_sem=send_sems.at[7 * i + k], recv_sem=recv_sems.at[7 * i + k],
                device_id=to, device_id_type=MESH)

        mine = [pltpu.make_async_copy(ins[i], rows(i, me), local_sems.at[i]) for i in range(n)]
        for cp in mine:
            cp.start()
        first = []
        for i in range(n):
            first.append(copy(i, 0, me, sibling, src=ins[i]))
            first += [copy(i, 1 + j, me, (*chip, c), src=ins[i]) for j, chip in enumerate(chips)]
        for cp in first:
            cp.start()
        passed = []
        for i in range(n):
            for j, chip in enumerate(chips):
                copy(i, 1 + j, (*chip, c), me).wait_recv()
                fwd = copy(i, 4 + j, (*chip, c), sibling)
                fwd.start()
                passed.append(fwd)
        for i in range(n):
            copy(i, 0, sibling, me).wait_recv()
            for j, chip in enumerate(chips):
                copy(i, 4 + j, (*chip, 1 - c), me).wait_recv()
        for cp in first + passed:
            cp.wait_send()
        for cp in mine:
            cp.wait()

    out_shape = []
    for s, a in zip(shards, axes):
        shape = list(s.shape)
        shape[a] *= N_DEV
        out_shape.append(jax.ShapeDtypeStruct(tuple(shape), s.dtype))
    any_spec = pl.BlockSpec(memory_space=pl.ANY)
    return pl.pallas_call(
        body, name=name, out_shape=out_shape,
        in_specs=[any_spec] * n, out_specs=[any_spec] * n,
        scratch_shapes=[pltpu.SemaphoreType.DMA((7 * n,)), pltpu.SemaphoreType.DMA((7 * n,)),
                        pltpu.SemaphoreType.DMA((n,))],
    )(*shards)


HBM_SPEC = pl.BlockSpec(memory_space=pltpu.HBM)
SEM_SPEC = pl.BlockSpec(memory_space=pltpu.SEMAPHORE)
ANY_SPEC = pl.BlockSpec(memory_space=pl.ANY)
SPLIT_PARAMS = pltpu.CompilerParams(has_side_effects=pltpu.SideEffectType.DATAFLOW_SIDE_EFFECTING)
N_PEERS = N_DEV - 1
TOKEN = jax.ShapeDtypeStruct((SUBLANES, LANES), F32)
VMEM_SPEC = pl.BlockSpec(memory_space=pltpu.VMEM)


def _in_hbm(arrays):
    return [pltpu.with_memory_space_constraint(a, pltpu.HBM) for a in arrays]


def _peer(m):
    x, y, c = (lax.axis_index(a) for a in MESH_AXES)
    px = 1 - x if m & 4 else x
    py = 1 - y if m & 2 else y
    pc = 1 - c if m & 1 else c
    return (px, py, pc), 4 * px + 2 * py + pc


def _my_index():
    x, y, c = (lax.axis_index(a) for a in MESH_AXES)
    return 4 * x + 2 * y + c


def _gather_copies(shard_refs, full_refs, axes, send_sems, recv_sems):
    copies = []
    for i, (shard, full) in enumerate(zip(shard_refs, full_refs)):
        mine = _block(full, axes[i], shard.shape[axes[i]], _my_index())
        for m in range(1, N_DEV):
            peer, _ = _peer(m)
            copies.append(pltpu.make_async_remote_copy(
                src_ref=shard, dst_ref=mine, send_sem=send_sems.at[N_PEERS * i + m - 1],
                recv_sem=recv_sems.at[N_PEERS * i + m - 1], device_id=peer, device_id_type=MESH))
    return copies


def _gather_start(name, shards, axes, after):
    n = len(shards)

    def body(*refs):
        shard_refs = refs[:n]
        send_sems, recv_sems, local_sems = refs[n + 1:n + 4]
        full_refs = refs[2 * n + 4:3 * n + 4]
        refs[3 * n + 4][...] = jnp.zeros(TOKEN.shape, TOKEN.dtype)
        for i in range(n):
            pltpu.make_async_copy(shard_refs[i], _block(full_refs[i], axes[i], shard_refs[i].shape[axes[i]], _my_index()),
                                  local_sems.at[i]).start()
        for cp in _gather_copies(shard_refs, full_refs, axes, send_sems, recv_sems):
            cp.start()

    fulls = []
    for s, a in zip(shards, axes):
        shape = list(s.shape)
        shape[a] *= N_DEV
        fulls.append(pltpu.HBM(tuple(shape), s.dtype))
    out = pl.pallas_call(
        body, name=name,
        out_shape=(pltpu.SemaphoreType.DMA((N_PEERS * n,)), pltpu.SemaphoreType.DMA((N_PEERS * n,)),
                   pltpu.SemaphoreType.DMA((n,)), *[pltpu.HBM(s.shape, s.dtype) for s in shards], *fulls, TOKEN),
        in_specs=[HBM_SPEC] * n + [ANY_SPEC],
        out_specs=(SEM_SPEC, SEM_SPEC, SEM_SPEC, *[HBM_SPEC] * (2 * n), VMEM_SPEC),
        input_output_aliases={i: 3 + i for i in range(n)},
        compiler_params=SPLIT_PARAMS,
    )(*_in_hbm(shards), after)
    return out[:-1], out[-1]


def _gather_wait(name, started, indices, axes, after):
    send_sems, recv_sems, local_sems = started[:3]
    n_all = (len(started) - 3) // 2
    shards = [started[3 + i] for i in indices]
    fulls = [started[3 + n_all + i] for i in indices]
    n = len(indices)

    def body(*refs):
        shard_refs, full_refs = refs[:n], refs[n:2 * n]
        send_sems, recv_sems, local_sems = refs[2 * n:2 * n + 3]
        for j, i in enumerate(indices):
            mine = _block(full_refs[j], axes[j], shard_refs[j].shape[axes[j]], _my_index())
            pltpu.make_async_copy(shard_refs[j], mine, local_sems.at[i]).wait()
            for m in range(1, N_DEV):
                peer, _ = _peer(m)
                cp = pltpu.make_async_remote_copy(
                    src_ref=shard_refs[j], dst_ref=mine, send_sem=send_sems.at[N_PEERS * i + m - 1],
                    recv_sem=recv_sems.at[N_PEERS * i + m - 1], device_id=peer, device_id_type=MESH)
                cp.wait_send()
                cp.wait_recv()

    out = pl.pallas_call(
        body, name=name,
        out_shape=tuple(pltpu.HBM(a.shape, a.dtype) for a in shards + fulls),
        in_specs=[HBM_SPEC] * (2 * n) + [SEM_SPEC] * 3 + [ANY_SPEC], out_specs=tuple([HBM_SPEC] * (2 * n)),
        input_output_aliases={i: i for i in range(2 * n)},
        compiler_params=SPLIT_PARAMS,
    )(*shards, *fulls, send_sems, recv_sems, local_sems, after)
    return out[n:]


def _exchange_start(name, fulls, axes):
    n = len(fulls)
    sizes = [f.shape[a] // N_DEV for f, a in zip(fulls, axes)]

    def body(*refs):
        ins = refs[:n]
        send_sems, recv_sems = refs[n:n + 2]
        lands = refs[2 * n + 2:3 * n + 2]
        refs[3 * n + 2][...] = jnp.zeros(TOKEN.shape, TOKEN.dtype)
        for i in range(n):
            for m in range(1, N_DEV):
                peer, index = _peer(m)
                pltpu.make_async_remote_copy(
                    src_ref=_block(ins[i], axes[i], sizes[i], index), dst_ref=lands[i].at[m - 1],
                    send_sem=send_sems.at[N_PEERS * i + m - 1], recv_sem=recv_sems.at[N_PEERS * i + m - 1],
                    device_id=peer, device_id_type=MESH).start()

    lands = []
    for f, a, size in zip(fulls, axes, sizes):
        shape = list(f.shape)
        shape[a] = size
        lands.append(pltpu.HBM((N_PEERS, *shape), f.dtype))
    out = pl.pallas_call(
        body, name=name,
        out_shape=(pltpu.SemaphoreType.DMA((N_PEERS * n,)), pltpu.SemaphoreType.DMA((N_PEERS * n,)),
                   *[pltpu.HBM(f.shape, f.dtype) for f in fulls], *lands, TOKEN),
        in_specs=[HBM_SPEC] * n, out_specs=(SEM_SPEC, SEM_SPEC, *[HBM_SPEC] * (2 * n), VMEM_SPEC),
        input_output_aliases={i: 2 + i for i in range(n)},
        compiler_params=SPLIT_PARAMS,
    )(*_in_hbm(fulls))
    return out[:-1], out[-1]


def _exchange_wait(name, started, axes, after):
    send_sems, recv_sems = started[:2]
    n = (len(started) - 2) // 2
    fulls, lands = list(started[2:2 + n]), list(started[2 + n:])
    sizes = [f.shape[a] // N_DEV for f, a in zip(fulls, axes)]

    def body(*refs):
        ins, land_refs = refs[:n], refs[n:2 * n]
        send_sems, recv_sems = refs[2 * n:2 * n + 2]
        for i in range(n):
            for m in range(1, N_DEV):
                peer, index = _peer(m)
                cp = pltpu.make_async_remote_copy(
                    src_ref=_block(ins[i], axes[i], sizes[i], index), dst_ref=land_refs[i].at[m - 1],
                    send_sem=send_sems.at[N_PEERS * i + m - 1], recv_sem=recv_sems.at[N_PEERS * i + m - 1],
                    device_id=peer, device_id_type=MESH)
                cp.wait_send()
                cp.wait_recv()

    out = pl.pallas_call(
        body, name=name,
        out_shape=tuple(pltpu.HBM(a.shape, a.dtype) for a in fulls + lands),
        in_specs=[HBM_SPEC] * (2 * n) + [SEM_SPEC] * 2 + [ANY_SPEC], out_specs=tuple([HBM_SPEC] * (2 * n)),
        input_output_aliases={i: i for i in range(2 * n)},
        compiler_params=SPLIT_PARAMS,
    )(*fulls, *lands, send_sems, recv_sems, after)
    return out[:n], out[n:]


def _sum_parts(part_refs, ndim):
    g = None
    for p_ref in part_refs:
        stacked = len(p_ref.shape) > ndim
        terms = [p_ref[s] for s in range(p_ref.shape[0])] if stacked else [p_ref[...]]
        for term in terms:
            term = term.astype(F32)
            g = term if g is None else g + term
    return g


def _adamw_update(w_ref, m_ref, v_ref, g, g_ref, d_ref, nm_ref, nv_ref):
    c1 = 1.0 - ADAM_B1 ** ADAM_STEP
    c2 = 1.0 - ADAM_B2 ** ADAM_STEP
    new_m = ADAM_B1 * m_ref[...] + (1.0 - ADAM_B1) * g
    new_v = ADAM_B2 * v_ref[...] + (1.0 - ADAM_B2) * (g * g)
    g_ref[...] = g
    nm_ref[...] = new_m
    nv_ref[...] = new_v
    d_ref[...] = -ADAM_LR * ((new_m / c1) / (jnp.sqrt(new_v / c2) + ADAM_EPS) + ADAM_WD * w_ref[...])


def _adamw_small(ws, ms, vs, stacks, loss_stack):
    n = len(ws)

    def body(*refs):
        ins, outs = refs[:4 * n + 1], refs[4 * n + 1:]
        for i in range(n):
            _adamw_update(ins[i], ins[n + i], ins[2 * n + i], _sum_parts([ins[3 * n + i]], len(ins[i].shape)),
                          outs[i], outs[n + i], outs[2 * n + i], outs[3 * n + i])
        total = ins[4 * n][0]
        for dev in range(1, N_DEV):
            total = total + ins[4 * n][dev]
        outs[4 * n][...] = total

    res = pl.pallas_call(
        body, name="adamw_small",
        out_shape=[jax.ShapeDtypeStruct(w.shape, F32) for w in ws] * 4 + [jax.ShapeDtypeStruct((1, LANES), F32)],
        compiler_params=_params(None),
    )(*ws, *ms, *vs, *stacks, loss_stack)
    return res[:n], res[n:2 * n], res[2 * n:3 * n], res[3 * n:4 * n], res[4 * n]


def _adamw(name, w, m, v, parts):
    r, c = w.shape
    tr = _tile(r, 256)
    n_parts = len(parts)

    def body(*refs):
        _adamw_update(refs[0], refs[1], refs[2], _sum_parts(refs[3:3 + n_parts], 2), *refs[3 + n_parts:])

    row = pl.BlockSpec((tr, c), lambda i: (i, 0))
    in_specs = [row, row, row]
    for p in parts:
        in_specs.append(row if p.ndim == 2 else pl.BlockSpec((p.shape[0], tr, c), lambda i: (0, i, 0)))
    return pl.pallas_call(
        body, name=name, grid=(r // tr,), in_specs=in_specs, out_specs=[row] * 4,
        out_shape=[jax.ShapeDtypeStruct((r, c), F32)] * 4,
        compiler_params=_params(("arbitrary",)),
    )(w, m, v, *parts)


SMALL = ("norm_gain", "pool_scale", "a_re", "a_im", "log_dt", "b_re", "b_im", "c_re", "c_im", "d_skip", "final_gain")
LARGE = ("w_in", "w_pool", "w_glu", "w_out", "w_ple", "w_ple_gate")
LARGE_AXIS = {"w_in": 1, "w_pool": 1, "w_glu": 1, "w_out": 0, "w_ple": 1, "w_ple_gate": 0}
WEIGHTS = ("norm_gain", "w_in", "w_pool", "pool_scale", "a_re", "a_im", "log_dt", "b_re", "b_im", "c_re", "c_im",
           "d_skip", "w_glu", "w_out", "w_ple", "w_ple_gate", "final_gain")


def kernel(x, p, norm_gain, w_in, w_pool, pool_scale, a_re, a_im, log_dt, b_re, b_im, c_re, c_im, d_skip, w_glu, w_out, w_ple, w_ple_gate, final_gain, loss_target, m_norm_gain, m_w_in, m_w_pool, m_pool_scale, m_a_re, m_a_im, m_log_dt, m_b_re, m_b_im, m_c_re, m_c_im, m_d_skip, m_w_glu, m_w_out, m_w_ple, m_w_ple_gate, m_final_gain, v_norm_gain, v_w_in, v_w_pool, v_pool_scale, v_a_re, v_a_im, v_log_dt, v_b_re, v_b_im, v_c_re, v_c_im, v_d_skip, v_w_glu, v_w_out, v_w_ple, v_w_ple_gate, v_final_gain):
    weights = dict(norm_gain=norm_gain, w_in=w_in, w_pool=w_pool, pool_scale=pool_scale, a_re=a_re, a_im=a_im,
                   log_dt=log_dt, b_re=b_re, b_im=b_im, c_re=c_re, c_im=c_im, d_skip=d_skip, w_glu=w_glu,
                   w_out=w_out, w_ple=w_ple, w_ple_gate=w_ple_gate, final_gain=final_gain)
    mom_m = dict(norm_gain=m_norm_gain, w_in=m_w_in, w_pool=m_w_pool, pool_scale=m_pool_scale, a_re=m_a_re,
                 a_im=m_a_im, log_dt=m_log_dt, b_re=m_b_re, b_im=m_b_im, c_re=m_c_re, c_im=m_c_im,
                 d_skip=m_d_skip, w_glu=m_w_glu, w_out=m_w_out, w_ple=m_w_ple, w_ple_gate=m_w_ple_gate,
                 final_gain=m_final_gain)
    mom_v = dict(norm_gain=v_norm_gain, w_in=v_w_in, w_pool=v_w_pool, pool_scale=v_pool_scale, a_re=v_a_re,
                 a_im=v_a_im, log_dt=v_log_dt, b_re=v_b_re, b_im=v_b_im, c_re=v_c_re, c_im=v_c_im,
                 d_skip=v_d_skip, w_glu=v_w_glu, w_out=v_w_out, w_ple=v_w_ple, w_ple_gate=v_w_ple_gate,
                 final_gain=v_final_gain)

    t = x.shape[1]
    xs = x.reshape(t, D_MODEL)
    ps = p.reshape(t, PLE_DIM)
    target = loss_target.reshape(t, D_MODEL)
    gain1 = norm_gain.reshape(1, D_MODEL)
    gain_f = final_gain.reshape(1, D_MODEL)
    scale_p = pool_scale.reshape(1, POOL_WIDTH)
    skip = d_skip.reshape(1, SSM_WIDTH)

    shard2d = {k: weights[k][0] for k in LARGE}
    shard_bf = {k: shard2d[k].astype(BF16) for k in LARGE}
    full = {"w_in": _all_gather("w_in_all_gather", [shard_bf["w_in"]], [LARGE_AXIS["w_in"]])[0]}
    later = [k for k in LARGE if k != "w_in"]
    later_axes = [LARGE_AXIS[k] for k in later]
    gather, gather_token = _gather_start("weights_gather_start", [shard_bf[k] for k in later], later_axes,
                                         full["w_in"])

    def arrive(k, after):
        i = later.index(k)
        full[k] = _gather_wait("gather_wait_" + k, gather, [i], [later_axes[i]], after)[0]

    ar, ai = a_re[0], a_im[0]
    ldt = log_dt.reshape(N_SSM_GROUPS, 1)
    br_t = jnp.transpose(b_re[0], (0, 2, 1))
    bi_t = jnp.transpose(b_im[0], (0, 2, 1))
    ab_re, ab_im, bb_re, bb_im = _ssm_params(ar, ai, ldt, br_t, bi_t)
    tiles = (SSM_CHUNKS, CHUNK_TILES, LANES)
    abar = jnp.concatenate([ab_re.reshape(tiles), ab_im.reshape(tiles)], axis=-1)
    w_pair = _compact_pair(bb_re, bb_im)
    c_pair = _compact_pair(c_re[0], -c_im[0])

    hn, proj = _norm1_in_proj(xs, gain1, full["w_in"], gather_token)
    pooled = _pool_fwd(proj)
    tm = _tile(t, 1024)
    arrive("w_pool", pooled)
    mixed = _mm("pool_mix", [(pooled, (tm, POOL_GROUP), lambda i, j, s: (i, j),
                              full["w_pool"], (None, POOL_GROUP, POOL_GROUP), lambda i, j, s: (j, 0, 0))],
                DOT_NN, (t // tm, N_POOL_GROUPS, 1),
                [((t, POOL_WIDTH), F32, (tm, POOL_GROUP), lambda i, j, s: (i, j))], 1)[0]
    y, gel, states = _ssm_fwd(proj, w_pair, c_pair, abar, skip)
    arrive("w_glu", gel)
    hg = _mm_nn("glu_proj", gel, full["w_glu"], [F32])[0]
    arrive("w_out", hg)
    cat, h1, h1b = _gate_out_proj(mixed, proj, hg, scale_p, full["w_out"], xs)
    arrive("w_ple", h1b)
    arrive("w_ple_gate", h1b)
    de, dq, dh2, g_final_gain, loss_part = _ple_final(h1, h1b, ps, full["w_ple_gate"], full["w_ple"], target, gain_f)

    grads = {}
    grads["w_ple_gate"] = _mm_tn("ple_gate_wgrad", h1b, dq, BF16)
    grads["w_ple"] = _mm_tn("ple_wgrad", ps, de, BF16)
    sent, tokens = {}, {}

    def send(names):
        sent[names], tokens[names[0]] = _exchange_start(
            "grads_start_" + names[0], [grads[k] for k in names], [LARGE_AXIS[k] for k in names])

    send(("w_ple_gate", "w_ple"))
    dh1, dh1b = _residual_dgrad("ple_gate_dgrad", dq, full["w_ple_gate"], dh2)
    grads["w_out"] = _mm_tn("out_wgrad", cat, dh1b, BF16)
    send(("w_out",))
    dmixed, dproj, dhg, g_pool_scale = _out_dgrad_gate_bwd(
        dh1b, full["w_out"], mixed, proj, hg, scale_p, [tokens["w_ple_gate"], tokens["w_out"]])

    tk = _tile(t, 1024)
    grads["w_pool"] = _mm("pool_wgrad", [(pooled, (tk, POOL_GROUP), lambda i, j, s: (s, i),
                                          dmixed, (tk, POOL_GROUP), lambda i, j, s: (s, i))],
                          DOT_TN, (N_POOL_GROUPS, 1, t // tk),
                          [((N_POOL_GROUPS, POOL_GROUP, POOL_GROUP), BF16, (None, POOL_GROUP, POOL_GROUP),
                            lambda i, j, s: (i, 0, 0))], t // tk)[0]
    dpooled = _mm("pool_dgrad", [(dmixed, (tm, POOL_GROUP), lambda i, j, s: (i, j),
                                  full["w_pool"], (None, POOL_GROUP, POOL_GROUP), lambda i, j, s: (j, 0, 0))],
                  DOT_NT, (t // tm, N_POOL_GROUPS, 1),
                  [((t, POOL_WIDTH), F32, (tm, POOL_GROUP), lambda i, j, s: (i, j))], 1)[0]
    dproj = _pool_bwd(dpooled, dproj)

    grads["w_glu"] = _mm_tn("glu_wgrad", gel, dhg, BF16)
    send(("w_pool", "w_glu"))

    def gelu_bwd_epilogue(acc, ex, out_refs):
        yv = ex[0][...]
        th = jnp.tanh(GELU_C * (yv + GELU_A * yv * yv * yv))
        dgelu = 0.5 * (1.0 + th) + 0.5 * yv * (1.0 - th * th) * GELU_C * (1.0 + 3.0 * GELU_A * yv * yv)
        out_refs[0][...] = acc * dgelu

    dy = _mm_nt("glu_dgrad", dhg, full["w_glu"], [F32], tk=2048, extras=[y], epilogue=gelu_bwd_epilogue,
                after=[tokens["w_pool"]])[0]
    dproj, g_c_pair, g_w_pair, g_abar, g_d_skip = _ssm_bwd(dy, proj, states, w_pair, c_pair, abar, skip, dproj)

    g_ab_re = g_abar[..., :LANES].reshape(N_SSM_GROUPS, SSM_STATE)
    g_ab_im = g_abar[..., LANES:].reshape(N_SSM_GROUPS, SSM_STATE)
    d_ar, d_ai, d_ldt, d_br_t, d_bi_t = _ssm_params_bwd(
        ar, ai, ldt, br_t, bi_t, g_ab_re, g_ab_im,
        _expand_grad(g_w_pair[..., :LANES]), _expand_grad(g_w_pair[..., LANES:]))

    small_grads = dict(
        pool_scale=g_pool_scale, a_re=d_ar, a_im=d_ai, log_dt=d_ldt.reshape(1, N_SSM_GROUPS),
        b_re=d_br_t.astype(BF16), b_im=d_bi_t.astype(BF16), c_re=_expand_grad(g_c_pair[..., :LANES]).astype(BF16),
        c_im=(-_expand_grad(g_c_pair[..., LANES:])).astype(BF16), d_skip=g_d_skip, final_gain=g_final_gain)
    early = [k for k in SMALL if k != "norm_gain"]
    early_sent, early_token = _gather_start(
        "small_grads_start", [small_grads[k][None] for k in early] + [jnp.broadcast_to(loss_part, (1, 1, LANES))],
        [0] * (len(early) + 1), d_ar)

    grads["w_in"] = _mm_tn("in_wgrad", hn, dproj, BF16, after=[early_token])
    send(("w_in",))
    grad_x, g_norm_gain = _in_dgrad_norm1_bwd(dproj, full["w_in"], xs, dh1, gain1, tokens["w_in"])
    late_sent, late_token = _gather_start("norm_gain_grad_start", [g_norm_gain[None]], [0], g_norm_gain)

    out_g, out_d, out_m, out_v = ({} for _ in range(4))
    me = 4 * lax.axis_index("x") + 2 * lax.axis_index("y") + lax.axis_index("c")
    after = late_token
    for names, started in sent.items():
        axes = [LARGE_AXIS[k] for k in names]
        partials, landed = _exchange_wait("grads_wait_" + names[0], started, axes, after)
        for k, axis, partial, land in zip(names, axes, partials, landed):
            shard_shape = shard2d[k].shape
            size = shard_shape[axis]
            own = lax.dynamic_slice_in_dim(partial, me * size, size, axis=axis)
            view = (-1, shard_shape[-1])
            rows = math.prod(shard_shape[:-1])
            res = _adamw("adamw_" + k, shard2d[k].reshape(view), mom_m[k][0].reshape(view), mom_v[k][0].reshape(view),
                         [own.reshape(view), land.reshape(N_PEERS, rows, shard_shape[-1])])
            out_g[k], out_d[k], out_m[k], out_v[k] = (r.reshape(weights[k].shape) for r in res)
            after = res[0]

    def b_view(a):
        return jnp.transpose(a[0], (0, 2, 1))

    views = dict(norm_gain=lambda a: a, pool_scale=lambda a: a, a_re=lambda a: a[0], a_im=lambda a: a[0],
                 log_dt=lambda a: a, b_re=b_view, b_im=b_view, c_re=lambda a: a[0], c_im=lambda a: a[0],
                 d_skip=lambda a: a, final_gain=lambda a: a.reshape(1, D_MODEL))
    landed = _gather_wait("small_grads_wait", early_sent, list(range(len(early) + 1)), [0] * (len(early) + 1), after)
    stack = dict(zip(early, landed))
    stack["norm_gain"] = _gather_wait("norm_gain_grad_wait", late_sent, [0], [0], after)[0]
    *small_out, loss_row = _adamw_small(
        [views[k](weights[k]) for k in SMALL], [views[k](mom_m[k]) for k in SMALL],
        [views[k](mom_v[k]) for k in SMALL], [stack[k] for k in SMALL], landed[-1])
    loss = loss_row[0, 0]
    for out, res in zip((out_g, out_d, out_m, out_v), small_out):
        for k, r in zip(SMALL, res):
            if k in ("b_re", "b_im"):
                r = jnp.transpose(r, (0, 2, 1))
            out[k] = r.reshape(weights[k].shape)

    return (loss, grad_x.reshape(x.shape), *[out_g[k] for k in WEIGHTS], *[out_d[k] for k in WEIGHTS],
            *[out_m[k] for k in WEIGHTS], *[out_v[k] for k in WEIGHTS])
```

```python
import math

import jax
import jax.numpy as jnp
from jax import lax
from jax.experimental import pallas as pl
from jax.experimental.pallas import tpu as pltpu

F32 = jnp.float32
BF16 = jnp.bfloat16
MESH = pl.DeviceIdType.MESH
MESH_AXES = ("x", "y", "c")
N_DEV = 8

D_MODEL = 2048
POOL_WIDTH = 1024
SSM_WIDTH = 1024
N_POOL_GROUPS = 4
POOL_GROUP = 256
SSM_GROUP = 16
N_SSM_GROUPS = 64
SSM_STATE = 64
SSM_FLAT = N_SSM_GROUPS * SSM_STATE
SSM_CHUNKS = 4
CHUNK_IN = SSM_WIDTH // SSM_CHUNKS
CHUNK_STATE = SSM_FLAT // SSM_CHUNKS
PLE_DIM = 256
EPS = 1e-6
A_RE_MAX = -1e-4
ADAM_LR = 0.001
ADAM_B1 = 0.9
ADAM_B2 = 0.999
ADAM_EPS = 1e-08
ADAM_WD = 0.01
ADAM_STEP = 10
GELU_C = math.sqrt(2.0 / math.pi)
GELU_A = 0.044715

SUBLANES = 8
LANES = 128
VMEM_LIMIT_BYTES = 48 * 1024 * 1024

DOT_NN = (((1,), (0,)), ((), ()))
DOT_NT = (((1,), (1,)), ((), ()))
DOT_TN = (((0,), (0,)), ((), ()))


def _tile(n, pref):
    return pref if n % pref == 0 else n


def _params(sem):
    return pltpu.CompilerParams(dimension_semantics=sem, vmem_limit_bytes=VMEM_LIMIT_BYTES)


def _sigmoid(v):
    return 1.0 / (1.0 + jnp.exp(-v))


def _silu_and_grad(v):
    s = _sigmoid(v)
    return v * s, s * (1.0 + v * (1.0 - s))


def _mm(name, pairs, dims, grid, outs, k_steps, extras=(), epilogue=None):
    n_pairs, n_ex, n_out = len(pairs), len(extras), len(outs)
    acc_shape = tuple(d for d in outs[0][2] if d is not None)
    if epilogue is None:
        def epilogue(acc, ex, out_refs):
            out_refs[0][...] = acc.astype(out_refs[0].dtype)

    def body(*refs):
        ab = refs[:2 * n_pairs]
        ex = refs[2 * n_pairs:2 * n_pairs + n_ex]
        out_refs = refs[2 * n_pairs + n_ex:2 * n_pairs + n_ex + n_out]
        acc = refs[-1]
        k = pl.program_id(2)

        @pl.when(k == 0)
        def _():
            acc[...] = jnp.zeros_like(acc)

        part = None
        for q in range(n_pairs):
            d = lax.dot_general(ab[2 * q][...].astype(BF16), ab[2 * q + 1][...].astype(BF16), dims,
                                preferred_element_type=F32)
            part = d if part is None else part + d
        acc[...] += part

        @pl.when(k == k_steps - 1)
        def _():
            epilogue(acc[...], ex, out_refs)

    in_specs, operands = [], []
    for a, a_blk, a_map, b, b_blk, b_map in pairs:
        in_specs += [pl.BlockSpec(a_blk, a_map), pl.BlockSpec(b_blk, b_map)]
        operands += [a, b]
    for e, e_blk, e_map in extras:
        in_specs.append(pl.BlockSpec(e_blk, e_map))
        operands.append(e)
    return pl.pallas_call(
        body, name=name, grid=grid, in_specs=in_specs,
        out_specs=[pl.BlockSpec(o[2], o[3]) for o in outs],
        out_shape=[jax.ShapeDtypeStruct(o[0], o[1]) for o in outs],
        scratch_shapes=[pltpu.VMEM(acc_shape, F32)],
        compiler_params=_params(("arbitrary", "arbitrary", "arbitrary")),
    )(*operands)


def _after(tokens):
    return [(tok, tok.shape, lambda i, j, s: (0, 0)) for tok in tokens]


def _mm_nn(name, a, b, out_dtypes, tm=1024, tn=1024, tk=1024, a_col0=0, extras=(), epilogue=None, after=()):
    m, n = a.shape[0], b.shape[1]
    k = b.shape[0]
    tm, tn, tk = _tile(m, tm), _tile(n, tn), _tile(k, tk)
    outs = [((m, n), dt, (tm, tn), lambda i, j, s: (i, j)) for dt in out_dtypes]
    ex = [(e, (tm, tn), lambda i, j, s: (i, j)) for e in extras] + _after(after)
    return _mm(name, [(a, (tm, tk), lambda i, j, s: (i, a_col0 + s), b, (tk, tn), lambda i, j, s: (s, j))],
               DOT_NN, (m // tm, n // tn, k // tk), outs, k // tk, ex, epilogue)


def _mm_nt(name, a, b, out_dtypes, tm=1024, tn=1024, tk=1024, extras=(), epilogue=None, after=()):
    m, kk = a.shape
    n = b.shape[0]
    tm, tn, tk = _tile(m, tm), _tile(n, tn), _tile(kk, tk)
    outs = [((m, n), dt, (tm, tn), lambda i, j, s: (i, j)) for dt in out_dtypes]
    ex = [(e, (tm, tn), lambda i, j, s: (i, j)) for e in extras] + _after(after)
    return _mm(name, [(a, (tm, tk), lambda i, j, s: (i, s), b, (tn, tk), lambda i, j, s: (j, s))],
               DOT_NT, (m // tm, n // tn, kk // tk), outs, kk // tk, ex, epilogue)


def _mm_tn(name, a, b, out_dtype, tm=512, tn=2048, tk=1024, after=()):
    m, kk = a.shape
    n = b.shape[1]
    tm, tn, tk = _tile(kk, tm), _tile(n, tn), _tile(m, tk)
    outs = [((kk, n), out_dtype, (tm, tn), lambda i, j, s: (i, j))]
    return _mm(name, [(a, (tk, tm), lambda i, j, s: (s, i), b, (tk, tn), lambda i, j, s: (s, j))],
               DOT_TN, (kk // tm, n // tn, m // tk), outs, m // tk, _after(after))[0]


ROW_TILE = 256


def _norm1_in_proj(x, gain, w_in, after):
    t = x.shape[0]
    tm = _tile(t, ROW_TILE)
    n = w_in.shape[1]

    def body(x_ref, g_ref, w_ref, _, hn_ref, proj_ref):
        xv = x_ref[...]
        r = lax.rsqrt(jnp.mean(xv * xv, axis=-1, keepdims=True) + EPS)
        hn = (xv * r * g_ref[...]).astype(BF16)
        hn_ref[...] = hn
        proj_ref[...] = jnp.dot(hn, w_ref[...], preferred_element_type=F32)

    row = pl.BlockSpec((tm, D_MODEL), lambda i: (i, 0))
    return pl.pallas_call(
        body, name="norm1_in_proj", grid=(t // tm,),
        in_specs=[row, pl.BlockSpec((1, D_MODEL), lambda i: (0, 0)), _resident(w_in.shape),
                  pl.BlockSpec(after.shape, lambda i: (0, 0))],
        out_specs=[row, pl.BlockSpec((tm, n), lambda i: (i, 0))],
        out_shape=[jax.ShapeDtypeStruct((t, D_MODEL), BF16), jax.ShapeDtypeStruct((t, n), F32)],
        compiler_params=_params(("arbitrary",)),
    )(x, gain, w_in, after)


def _in_dgrad_norm1_bwd(dproj, w_in, x, dh1, gain, after):
    t = x.shape[0]
    tm = _tile(t, ROW_TILE)

    def body(dp_ref, w_ref, x_ref, dh1_ref, g_ref, _, dx_ref, gg_ref):
        @pl.when(pl.program_id(0) == 0)
        def _():
            gg_ref[...] = jnp.zeros_like(gg_ref)

        dhn = lax.dot_general(dp_ref[...], w_ref[...], DOT_NT, preferred_element_type=F32)
        xv = x_ref[...]
        r = lax.rsqrt(jnp.mean(xv * xv, axis=-1, keepdims=True) + EPS)
        xh = xv * r
        gg_ref[...] += jnp.sum(dhn * xh, axis=0, keepdims=True)
        dxh = dhn * g_ref[...]
        dx_ref[...] = dh1_ref[...] + r * (dxh - xh * jnp.mean(dxh * xh, axis=-1, keepdims=True))

    row = pl.BlockSpec((tm, D_MODEL), lambda i: (i, 0))
    vec = pl.BlockSpec((1, D_MODEL), lambda i: (0, 0))
    return pl.pallas_call(
        body, name="in_dgrad_norm1_bwd", grid=(t // tm,),
        in_specs=[pl.BlockSpec((tm, dproj.shape[1]), lambda i: (i, 0)), _resident(w_in.shape), row, row, vec,
                  pl.BlockSpec(after.shape, lambda i: (0, 0))],
        out_specs=[row, vec],
        out_shape=[jax.ShapeDtypeStruct((t, D_MODEL), F32), jax.ShapeDtypeStruct((1, D_MODEL), F32)],
        compiler_params=_params(("arbitrary",)),
    )(dproj, w_in, x, dh1, gain, after)


def _pool_counts(t, width, group):
    row = lax.broadcasted_iota(jnp.int32, (t, width), 0)
    window = jnp.left_shift(jnp.int32(2), group)
    return row, jnp.minimum(row + 1, window).astype(F32)


def _select_window(group, s2, s4, s8, s16):
    return jnp.where(group == 0, s2, jnp.where(group == 1, s4, jnp.where(group == 2, s8, s16)))


def _pool_fwd(proj):
    t = proj.shape[0]
    tc = LANES

    def body(u_ref, o_ref):
        group = pl.program_id(0) // (POOL_GROUP // tc)
        v = u_ref[...]
        row, count = _pool_counts(t, tc, group)

        def down(a, j):
            return jnp.where(row >= j, pltpu.roll(a, j, 0), 0.0)

        s2 = v + down(v, 1)
        s4 = s2 + down(s2, 2)
        s8 = s4 + down(s4, 4)
        s16 = s8 + down(s8, 8)
        o_ref[...] = (_select_window(group, s2, s4, s8, s16) / count - v).astype(BF16)

    return pl.pallas_call(
        body, name="pool_fwd", grid=(POOL_WIDTH // tc,),
        in_specs=[pl.BlockSpec((t, tc), lambda j: (0, j))],
        out_specs=pl.BlockSpec((t, tc), lambda j: (0, j)),
        out_shape=jax.ShapeDtypeStruct((t, POOL_WIDTH), BF16),
        compiler_params=_params(("arbitrary",)),
    )(proj)


def _pool_bwd(dpooled, dproj):
    t = dpooled.shape[0]
    tc = LANES

    def body(d_ref, _, o_ref):
        group = pl.program_id(0) // (POOL_GROUP // tc)
        dp = d_ref[...]
        row, count = _pool_counts(t, tc, group)
        r = dp / count

        def up(a, j):
            return jnp.where(row < t - j, pltpu.roll(a, t - j, 0), 0.0)

        s2 = r + up(r, 1)
        s4 = s2 + up(s2, 2)
        s8 = s4 + up(s4, 4)
        s16 = s8 + up(s8, 8)
        o_ref[...] = (_select_window(group, s2, s4, s8, s16) - dp).astype(BF16)

    return pl.pallas_call(
        body, name="pool_bwd", grid=(POOL_WIDTH // tc,),
        in_specs=[pl.BlockSpec((t, tc), lambda j: (0, j)), pl.BlockSpec(memory_space=pl.ANY)],
        out_specs=pl.BlockSpec((t, tc), lambda j: (0, j)),
        out_shape=jax.ShapeDtypeStruct(dproj.shape, dproj.dtype),
        input_output_aliases={1: 0},
        compiler_params=_params(("arbitrary",)),
    )(dpooled, dproj)


def _gate_out_proj(mixed, proj, hg, pool_scale, w_out, x):
    t = mixed.shape[0]
    tm = _tile(t, ROW_TILE)

    def body(mx_ref, ga_ref, gb_ref, hg_ref, ps_ref, w_ref, x_ref, cat_ref, h1_ref, h1b_ref):
        silu_a, _ = _silu_and_grad(ga_ref[...])
        cat_ref[:, :POOL_WIDTH] = (mx_ref[...] * ps_ref[...] * silu_a).astype(BF16)
        silu_b, _ = _silu_and_grad(gb_ref[...])
        sb = hg_ref[:, :SSM_WIDTH] * _sigmoid(hg_ref[:, SSM_WIDTH:])
        cat_ref[:, POOL_WIDTH:] = (sb * silu_b).astype(BF16)
        h1 = x_ref[...] + jnp.dot(cat_ref[...], w_ref[...], preferred_element_type=F32)
        h1_ref[...] = h1
        h1b_ref[...] = h1.astype(BF16)

    row = pl.BlockSpec((tm, D_MODEL), lambda i: (i, 0))
    return pl.pallas_call(
        body, name="gate_out_proj", grid=(t // tm,),
        in_specs=[pl.BlockSpec((tm, POOL_WIDTH), lambda i: (i, 0)),
                  pl.BlockSpec((tm, POOL_WIDTH), lambda i: (i, 1)),
                  pl.BlockSpec((tm, SSM_WIDTH), lambda i: (i, 3)),
                  pl.BlockSpec((tm, 2 * SSM_WIDTH), lambda i: (i, 0)),
                  pl.BlockSpec((1, POOL_WIDTH), lambda i: (0, 0)), _resident(w_out.shape), row],
        out_specs=[row, row, row],
        out_shape=[jax.ShapeDtypeStruct((t, D_MODEL), BF16), jax.ShapeDtypeStruct((t, D_MODEL), F32),
                   jax.ShapeDtypeStruct((t, D_MODEL), BF16)],
        compiler_params=_params(("arbitrary",)),
    )(mixed, proj, proj, hg, pool_scale, w_out, x)


def _residual_dgrad(name, dy, w, residual):
    t = dy.shape[0]
    tm = _tile(t, ROW_TILE)
    n = w.shape[0]

    def body(dy_ref, w_ref, r_ref, o_ref, ob_ref):
        o = r_ref[...] + lax.dot_general(dy_ref[...], w_ref[...], DOT_NT, preferred_element_type=F32)
        o_ref[...] = o
        ob_ref[...] = o.astype(BF16)

    out = pl.BlockSpec((tm, n), lambda i: (i, 0))
    return pl.pallas_call(
        body, name=name, grid=(t // tm,),
        in_specs=[pl.BlockSpec((tm, dy.shape[1]), lambda i: (i, 0)), _resident(w.shape), out],
        out_specs=[out, out],
        out_shape=[jax.ShapeDtypeStruct((t, n), F32), jax.ShapeDtypeStruct((t, n), BF16)],
        compiler_params=_params(("arbitrary",)),
    )(dy, w, residual)


def _out_dgrad_gate_bwd(dh1b, w_out, mixed, proj, hg, pool_scale, after):
    t = mixed.shape[0]
    tm = _tile(t, ROW_TILE)
    n_after = len(after)

    def body(dh_ref, w_ref, mx_ref, ga_ref, gb_ref, hg_ref, ps_ref, *rest):
        dmx_ref, dp_ref, dhg_ref, gps_ref = rest[n_after:]

        @pl.when(pl.program_id(0) == 0)
        def _():
            gps_ref[...] = jnp.zeros_like(gps_ref)

        dcat = lax.dot_general(dh_ref[...], w_ref[...], DOT_NT, preferred_element_type=F32)
        ps = ps_ref[...]
        mx = mx_ref[...]
        dya = dcat[:, :POOL_WIDTH]
        silu_a, dsilu_a = _silu_and_grad(ga_ref[...])
        dpa = dya * silu_a
        gps_ref[...] += jnp.sum(dpa * mx, axis=0, keepdims=True)
        dmx_ref[...] = (dpa * ps).astype(BF16)
        dp_ref[:, :POOL_WIDTH] = jnp.zeros((tm, POOL_WIDTH), BF16)
        dp_ref[:, POOL_WIDTH:2 * POOL_WIDTH] = (dya * mx * ps * dsilu_a).astype(BF16)

        dyb = dcat[:, POOL_WIDTH:]
        silu_b, dsilu_b = _silu_and_grad(gb_ref[...])
        h_a = hg_ref[:, :SSM_WIDTH]
        sg = _sigmoid(hg_ref[:, SSM_WIDTH:])
        dsb = dyb * silu_b
        dp_ref[:, 2 * POOL_WIDTH:2 * POOL_WIDTH + SSM_WIDTH] = jnp.zeros((tm, SSM_WIDTH), BF16)
        dp_ref[:, 2 * POOL_WIDTH + SSM_WIDTH:] = (dyb * h_a * sg * dsilu_b).astype(BF16)
        dhg_ref[:, :SSM_WIDTH] = (dsb * sg).astype(BF16)
        dhg_ref[:, SSM_WIDTH:] = (dsb * h_a * sg * (1.0 - sg)).astype(BF16)

    half = pl.BlockSpec((tm, POOL_WIDTH), lambda i: (i, 0))
    full = pl.BlockSpec((tm, D_MODEL), lambda i: (i, 0))
    vec = pl.BlockSpec((1, POOL_WIDTH), lambda i: (0, 0))
    proj_width = 2 * POOL_WIDTH + 2 * SSM_WIDTH
    return pl.pallas_call(
        body, name="out_dgrad_gate_bwd", grid=(t // tm,),
        in_specs=[full, _resident(w_out.shape), half,
                  pl.BlockSpec((tm, POOL_WIDTH), lambda i: (i, 1)),
                  pl.BlockSpec((tm, SSM_WIDTH), lambda i: (i, 3)),
                  full, vec] + [pl.BlockSpec(tok.shape, lambda i: (0, 0)) for tok in after],
        out_specs=[half, pl.BlockSpec((tm, proj_width), lambda i: (i, 0)), full, vec],
        out_shape=[jax.ShapeDtypeStruct((t, POOL_WIDTH), BF16), jax.ShapeDtypeStruct((t, proj_width), BF16),
                   jax.ShapeDtypeStruct((t, 2 * SSM_WIDTH), BF16),
                   jax.ShapeDtypeStruct((1, POOL_WIDTH), F32)],
        compiler_params=_params(("arbitrary",)),
    )(dh1b, w_out, mixed, proj, proj, hg, pool_scale, *after)


def _ple_final(h1, h1b, p, w_gate, w_ple, target, gain):
    t = h1.shape[0]
    tm = _tile(t, 256)

    def body(h1_ref, h1b_ref, p_ref, wg_ref, wp_ref, tg_ref, g_ref, de_ref, dq_ref, dh2_ref, gg_ref, loss_ref):
        @pl.when(pl.program_id(0) == 0)
        def _():
            gg_ref[...] = jnp.zeros_like(gg_ref)
            loss_ref[...] = jnp.zeros_like(loss_ref)

        ev = jnp.dot(p_ref[...].astype(BF16), wp_ref[...], preferred_element_type=F32)
        sg = _sigmoid(jnp.dot(h1b_ref[...], wg_ref[...], preferred_element_type=F32))
        h2 = h1_ref[...] + ev * sg
        r = lax.rsqrt(jnp.mean(h2 * h2, axis=-1, keepdims=True) + EPS)
        n = h2 * r
        gain_v = g_ref[...]
        diff = n * gain_v - tg_ref[...]
        row_loss = jnp.sum(diff * diff, axis=-1, keepdims=True)
        loss_ref[...] += (0.5 / D_MODEL) * jnp.sum(row_loss, axis=0, keepdims=True)
        dout = diff * (1.0 / D_MODEL)
        gg_ref[...] += jnp.sum(dout * n, axis=0, keepdims=True)
        dn = dout * gain_v
        dh2 = r * (dn - n * jnp.mean(dn * n, axis=-1, keepdims=True))
        dh2_ref[...] = dh2
        de_ref[...] = (dh2 * sg).astype(BF16)
        dq_ref[...] = (dh2 * ev * sg * (1.0 - sg)).astype(BF16)

    row = pl.BlockSpec((tm, D_MODEL), lambda i: (i, 0))
    vec = pl.BlockSpec((1, D_MODEL), lambda i: (0, 0))
    return pl.pallas_call(
        body, name="ple_final", grid=(t // tm,),
        in_specs=[row, row, pl.BlockSpec((tm, PLE_DIM), lambda i: (i, 0)), _resident((D_MODEL, D_MODEL)),
                  _resident((PLE_DIM, D_MODEL)), row, vec],
        out_specs=[row, row, row, vec, pl.BlockSpec((1, 1), lambda i: (0, 0))],
        out_shape=[jax.ShapeDtypeStruct((t, D_MODEL), BF16), jax.ShapeDtypeStruct((t, D_MODEL), BF16),
                   jax.ShapeDtypeStruct((t, D_MODEL), F32), jax.ShapeDtypeStruct((1, D_MODEL), F32),
                   jax.ShapeDtypeStruct((1, 1), F32)],
        compiler_params=_params(("arbitrary",)),
    )(h1, h1b, p, w_gate, w_ple, target, gain)


def _zoh(a_re, a_im, log_dt, b_re_t, b_im_t):
    lam_re = jnp.minimum(a_re, A_RE_MAX)
    lam_im = a_im
    dt = jnp.exp(log_dt)
    mag = jnp.exp(lam_re * dt)
    ang = lam_im * dt
    ab_re = mag * jnp.cos(ang)
    ab_im = mag * jnp.sin(ang)
    den = lam_re * lam_re + lam_im * lam_im
    n_re = ab_re - 1.0
    n_im = ab_im
    q_re = (n_re * lam_re + n_im * lam_im) / den
    q_im = (n_im * lam_re - n_re * lam_im) / den
    bb_re = q_re[:, None, :] * b_re_t - q_im[:, None, :] * b_im_t
    bb_im = q_re[:, None, :] * b_im_t + q_im[:, None, :] * b_re_t
    return ab_re, ab_im, bb_re, bb_im


def _ssm_params(a_re, a_im, log_dt, b_re_t, b_im_t):
    def body(are_ref, aim_ref, dt_ref, bre_ref, bim_ref, abre_ref, abim_ref, bbre_ref, bbim_ref):
        ab_re, ab_im, bb_re, bb_im = _zoh(are_ref[...], aim_ref[...], dt_ref[...], bre_ref[...], bim_ref[...])
        abre_ref[...] = ab_re
        abim_ref[...] = ab_im
        bbre_ref[...] = bb_re
        bbim_ref[...] = bb_im

    return pl.pallas_call(
        body, name="ssm_params",
        out_shape=[jax.ShapeDtypeStruct(a_re.shape, F32), jax.ShapeDtypeStruct(a_re.shape, F32),
                   jax.ShapeDtypeStruct(b_re_t.shape, F32), jax.ShapeDtypeStruct(b_re_t.shape, F32)],
        compiler_params=_params(None),
    )(a_re, a_im, log_dt, b_re_t, b_im_t)


def _ssm_params_bwd(a_re, a_im, log_dt, b_re_t, b_im_t, g_ab_re, g_ab_im, g_bb_re, g_bb_im):
    def body(are_ref, aim_ref, dt_ref, bre_ref, bim_ref, gar_ref, gai_ref, gbr_ref, gbi_ref,
             o_are, o_aim, o_dt, o_bre, o_bim):
        _, vjp = jax.vjp(_zoh, are_ref[...], aim_ref[...], dt_ref[...], bre_ref[...], bim_ref[...])
        d_are, d_aim, d_dt, d_bre, d_bim = vjp((gar_ref[...], gai_ref[...], gbr_ref[...], gbi_ref[...]))
        o_are[...] = d_are
        o_aim[...] = d_aim
        o_dt[...] = d_dt
        o_bre[...] = d_bre
        o_bim[...] = d_bim

    ins = (a_re, a_im, log_dt, b_re_t, b_im_t)
    return pl.pallas_call(
        body, name="ssm_params_bwd",
        out_shape=[jax.ShapeDtypeStruct(v.shape, F32) for v in ins],
        compiler_params=_params(None),
    )(*ins, g_ab_re, g_ab_im, g_bb_re, g_bb_im)


CHUNK_TILES = CHUNK_STATE // LANES
CH_PER_TILE = CHUNK_IN // CHUNK_TILES
PAIR = 2 * LANES
SSM_ROWS = 256
SCAN_STEPS = 8
U_COLUMN_BLOCK = 2 * POOL_WIDTH // SSM_WIDTH


def _own_half():
    r = lax.broadcasted_iota(jnp.int32, (CHUNK_IN, LANES), 0) // SSM_GROUP % 2
    c = lax.broadcasted_iota(jnp.int32, (CHUNK_IN, LANES), 1) // SSM_STATE
    return (r == c)[None]


def _compact_weight(w):
    tiled = jnp.tile(w.reshape(SSM_CHUNKS, CHUNK_IN, SSM_STATE), (1, 1, 2))
    return jnp.where(_own_half(), tiled, 0.0)


def _compact_pair(w_a, w_b):
    return jnp.concatenate([_compact_weight(w_a), _compact_weight(w_b)], axis=-1).astype(BF16)


def _expand_grad(g):
    kept = jnp.where(_own_half(), g, 0.0)
    return kept.reshape(SSM_CHUNKS, CHUNK_IN, 2, SSM_STATE).sum(axis=2).reshape(N_SSM_GROUPS, SSM_GROUP, SSM_STATE)


TILES_PER_BLOCK = LANES // CH_PER_TILE
IN_BLOCKS = CHUNK_IN // LANES


def _tile_masks():
    j = lax.broadcasted_iota(jnp.int32, (CHUNK_TILES, LANES), 0) % TILES_PER_BLOCK
    lane = lax.broadcasted_iota(jnp.int32, (CHUNK_TILES, LANES), 1) // CH_PER_TILE
    return (j == lane).astype(F32)


def _tile_rows(ref, j, tt):
    return ref.at[j // TILES_PER_BLOCK, pl.ds(j, tt, stride=CHUNK_TILES), :]


def _spread(ref, v, masks):
    tt = v.shape[0]
    for j in range(CHUNK_TILES):
        block = LANES * (j // TILES_PER_BLOCK)
        _tile_rows(ref, j, tt)[...] = v[:, block:block + LANES] * masks[j:j + 1, :]
    return jnp.concatenate([ref[b] for b in range(IN_BLOCKS)], axis=1).astype(BF16)


def _gather(ref, full, masks):
    tt = full.shape[0] // CHUNK_TILES
    for b in range(IN_BLOCKS):
        ref[b] = full[:, b * LANES:(b + 1) * LANES]
    out = []
    for b in range(IN_BLOCKS):
        acc = None
        for j in range(b * TILES_PER_BLOCK, (b + 1) * TILES_PER_BLOCK):
            part = _tile_rows(ref, j, tt)[...] * masks[j:j + 1, :]
            acc = part if acc is None else acc + part
        out.append(acc)
    return jnp.concatenate(out, axis=1)


def _resident(shape):
    return pl.BlockSpec(shape, lambda i: (0,) * len(shape), pipeline_mode=pl.Buffered(1))


def _halves(ref, k, rows=slice(None)):
    return ref[k, rows, :LANES], ref[k, rows, LANES:]


def _ssm_fwd(proj, w2, c2, a2, d_skip):
    t = proj.shape[0]
    tt = _tile(t, SSM_ROWS)
    rows = tt * CHUNK_TILES

    def body(u_ref, w_ref, c_ref, a_ref, d_ref, y_ref, gel_ref, s_ref, carry, spread_ref, full_ref):
        @pl.when(pl.program_id(0) == 0)
        def _():
            carry[...] = jnp.zeros_like(carry)
            spread_ref[...] = jnp.zeros_like(spread_ref)

        mask = _tile_masks()
        u = u_ref[...]
        for k in range(SSM_CHUNKS):
            uk = _spread(spread_ref, u[:, k * CHUNK_IN:(k + 1) * CHUNK_IN], mask)
            s_ref[k] = jnp.dot(uk, w_ref[k], preferred_element_type=F32)

        abar = [_halves(a_ref, k) for k in range(SSM_CHUNKS)]

        def steps(i, state):
            for v in range(SCAN_STEPS):
                r = pl.ds(pl.multiple_of((i * SCAN_STEPS + v) * CHUNK_TILES, CHUNK_TILES), CHUNK_TILES)
                new = []
                for k, ((a_re, a_im), (s_re, s_im)) in enumerate(zip(abar, state)):
                    b_re, b_im = _halves(s_ref, k, r)
                    s_re, s_im = a_re * s_re - a_im * s_im + b_re, a_re * s_im + a_im * s_re + b_im
                    s_ref[k, r, :LANES] = s_re
                    s_ref[k, r, LANES:] = s_im
                    new.append((s_re, s_im))
                state = tuple(new)
            return state

        state = lax.fori_loop(0, tt // SCAN_STEPS, steps, tuple(_halves(carry, k) for k in range(SSM_CHUNKS)))
        for k, (s_re, s_im) in enumerate(state):
            carry[k, :, :LANES] = s_re
            carry[k, :, LANES:] = s_im

        for k in range(SSM_CHUNKS):
            cols = slice(k * CHUNK_IN, (k + 1) * CHUNK_IN)
            full = lax.dot_general(s_ref[k].astype(BF16), c_ref[k], DOT_NT, preferred_element_type=F32)
            y = _gather(full_ref, full, mask) + d_ref[:, cols] * u[:, cols]
            y_ref[:, cols] = y
            gel_ref[:, cols] = (0.5 * y * (1.0 + jnp.tanh(GELU_C * (y + GELU_A * y * y * y)))).astype(BF16)

    weight = _resident((SSM_CHUNKS, CHUNK_IN, PAIR))
    tokens = pl.BlockSpec((tt, SSM_WIDTH), lambda i: (i, 0))
    return pl.pallas_call(
        body, name="ssm_fwd", grid=(t // tt,),
        in_specs=[pl.BlockSpec((tt, SSM_WIDTH), lambda i: (i, U_COLUMN_BLOCK)), weight, weight,
                  _resident((SSM_CHUNKS, CHUNK_TILES, PAIR)), _resident((1, SSM_WIDTH))],
        out_specs=[tokens, tokens, pl.BlockSpec((SSM_CHUNKS, rows, PAIR), lambda i: (0, i, 0))],
        out_shape=[jax.ShapeDtypeStruct((t, SSM_WIDTH), F32), jax.ShapeDtypeStruct((t, SSM_WIDTH), BF16),
                   jax.ShapeDtypeStruct((SSM_CHUNKS, t * CHUNK_TILES, PAIR), F32)],
        scratch_shapes=[pltpu.VMEM((SSM_CHUNKS, CHUNK_TILES, PAIR), F32), pltpu.VMEM((IN_BLOCKS, rows, LANES), F32),
                        pltpu.VMEM((IN_BLOCKS, rows, LANES), F32)],
        compiler_params=_params(("arbitrary",)),
    )(proj, w2, c2, a2, d_skip)


def _ssm_bwd(dy, proj, s, w2, c2, a2, d_skip, dproj):
    t = dy.shape[0]
    tt = _tile(t, SSM_ROWS)
    rows = tt * CHUNK_TILES
    n_chunks = t // tt

    def body(dy_ref, u_ref, s_ref, w_ref, c_ref, a_ref, d_ref, _, du_ref, gc_ref, gw_ref, ga_ref, gd_ref, z_ref, carry,
             spread_ref, full_ref):
        @pl.when(pl.program_id(0) == 0)
        def _():
            for r in (carry, gc_ref, gw_ref, ga_ref, gd_ref, spread_ref):
                r[...] = jnp.zeros_like(r)

        mask = _tile_masks()
        dy_v = dy_ref[...]
        u = u_ref[...]
        gd_ref[...] += jnp.sum(dy_v * u, axis=0, keepdims=True)
        for k in range(SSM_CHUNKS):
            dk = _spread(spread_ref, dy_v[:, k * CHUNK_IN:(k + 1) * CHUNK_IN], mask)
            z_ref[k] = jnp.dot(dk, c_ref[k], preferred_element_type=F32)
            gc_ref[k] += lax.dot_general(dk, s_ref[k].astype(BF16), DOT_TN, preferred_element_type=F32)

        abar = [_halves(a_ref, k) for k in range(SSM_CHUNKS)]

        def steps(i, state):
            zs, gs = state
            for v in range(SCAN_STEPS):
                tok = tt - 1 - (i * SCAN_STEPS + v)
                r = pl.ds(pl.multiple_of(tok * CHUNK_TILES, CHUNK_TILES), CHUNK_TILES)
                new_z, new_g = [], []
                for k, ((a_re, a_im), (z_re, z_im), (g_re, g_im)) in enumerate(zip(abar, zs, gs)):
                    s_re, s_im = _halves(s_ref, k, r)
                    g_re = g_re + z_re * s_re + z_im * s_im
                    g_im = g_im + z_im * s_re - z_re * s_im
                    d_re, d_im = _halves(z_ref, k, r)
                    z_re, z_im = d_re + a_re * z_re + a_im * z_im, d_im + a_re * z_im - a_im * z_re
                    z_ref[k, r, :LANES] = z_re
                    z_ref[k, r, LANES:] = z_im
                    new_z.append((z_re, z_im))
                    new_g.append((g_re, g_im))
                zs, gs = tuple(new_z), tuple(new_g)
            return zs, gs

        zs, gs = lax.fori_loop(0, tt // SCAN_STEPS, steps,
                               (tuple(_halves(carry, k) for k in range(SSM_CHUNKS)),
                                tuple(_halves(ga_ref, k) for k in range(SSM_CHUNKS))))
        for k in range(SSM_CHUNKS):
            carry[k, :, :LANES], carry[k, :, LANES:] = zs[k]
            ga_ref[k, :, :LANES], ga_ref[k, :, LANES:] = gs[k]

        for k in range(SSM_CHUNKS):
            cols = slice(k * CHUNK_IN, (k + 1) * CHUNK_IN)
            zb = z_ref[k].astype(BF16)
            full = lax.dot_general(zb, w_ref[k], DOT_NT, preferred_element_type=F32)
            du_ref[:, cols] = (_gather(full_ref, full, mask) + d_ref[:, cols] * dy_v[:, cols]).astype(BF16)
            uk = _spread(spread_ref, u[:, cols], mask)
            gw_ref[k] += lax.dot_general(uk, zb, DOT_TN, preferred_element_type=F32)

    weight = _resident((SSM_CHUNKS, CHUNK_IN, PAIR))
    tokens = pl.BlockSpec((tt, SSM_WIDTH), lambda i: (n_chunks - 1 - i, 0))
    grad = pl.BlockSpec((SSM_CHUNKS, CHUNK_IN, PAIR), lambda i: (0, 0, 0))
    return pl.pallas_call(
        body, name="ssm_bwd", grid=(n_chunks,),
        in_specs=[tokens, pl.BlockSpec((tt, SSM_WIDTH), lambda i: (n_chunks - 1 - i, U_COLUMN_BLOCK)),
                  pl.BlockSpec((SSM_CHUNKS, rows, PAIR), lambda i: (0, n_chunks - 1 - i, 0)), weight, weight,
                  _resident((SSM_CHUNKS, CHUNK_TILES, PAIR)), _resident((1, SSM_WIDTH)),
                  pl.BlockSpec(memory_space=pl.ANY)],
        out_specs=[pl.BlockSpec((tt, SSM_WIDTH), lambda i: (n_chunks - 1 - i, U_COLUMN_BLOCK)), grad, grad,
                   pl.BlockSpec((SSM_CHUNKS, CHUNK_TILES, PAIR), lambda i: (0, 0, 0)),
                   pl.BlockSpec((1, SSM_WIDTH), lambda i: (0, 0))],
        out_shape=[jax.ShapeDtypeStruct(dproj.shape, dproj.dtype), jax.ShapeDtypeStruct((SSM_CHUNKS, CHUNK_IN, PAIR), F32),
                   jax.ShapeDtypeStruct((SSM_CHUNKS, CHUNK_IN, PAIR), F32),
                   jax.ShapeDtypeStruct((SSM_CHUNKS, CHUNK_TILES, PAIR), F32), jax.ShapeDtypeStruct((1, SSM_WIDTH), F32)],
        input_output_aliases={7: 0},
        scratch_shapes=[pltpu.VMEM((SSM_CHUNKS, rows, PAIR), F32), pltpu.VMEM((SSM_CHUNKS, CHUNK_TILES, PAIR), F32),
                        pltpu.VMEM((IN_BLOCKS, rows, LANES), F32), pltpu.VMEM((IN_BLOCKS, rows, LANES), F32)],
        compiler_params=_params(("arbitrary",)),
    )(dy, proj, s, w2, c2, a2, d_skip, dproj)


def _block(ref, axis, size, index):
    idx = [slice(None)] * len(ref.shape)
    idx[axis] = pl.ds(pl.multiple_of(index * size, size), size)
    return ref.at[tuple(idx)]


def _all_gather(name, shards, axes):
    n = len(shards)
    sizes = [s.shape[a] for s, a in zip(shards, axes)]

    def body(*refs):
        ins, outs = refs[:n], refs[n:2 * n]
        send_sems, recv_sems, local_sems = refs[2 * n:]
        x, y, c = (lax.axis_index(a) for a in MESH_AXES)
        me, sibling = (x, y, c), (x, y, 1 - c)
        chips = [(1 - x, y), (x, 1 - y), (1 - x, 1 - y)]

        def rows(i, dev):
            return _block(outs[i], axes[i], sizes[i], 4 * dev[0] + 2 * dev[1] + dev[2])

        def copy(i, k, block, to, src=None):
            return pltpu.make_async_remote_copy(
                src_ref=rows(i, block) if src is None else src, dst_ref=rows(i, block),
                send_sem=send_sems.at[7 * i + k], recv_sem=recv_sems.at[7 * i + k],
                device_id=to, device_id_type=MESH)

        mine = [pltpu.make_async_copy(ins[i], rows(i, me), local_sems.at[i]) for i in range(n)]
        for cp in mine:
            cp.start()
        first = []
        for i in range(n):
            first.append(copy(i, 0, me, sibling, src=ins[i]))
            first += [copy(i, 1 + j, me, (*chip, c), src=ins[i]) for j, chip in enumerate(chips)]
        for cp in first:
            cp.start()
        passed = []
        for i in range(n):
            for j, chip in enumerate(chips):
                copy(i, 1 + j, (*chip, c), me).wait_recv()
                fwd = copy(i, 4 + j, (*chip, c), sibling)
                fwd.start()
                passed.append(fwd)
        for i in range(n):
            copy(i, 0, sibling, me).wait_recv()
            for j, chip in enumerate(chips):
                copy(i, 4 + j, (*chip, 1 - c), me).wait_recv()
        for cp in first + passed:
            cp.wait_send()
        for cp in mine:
            cp.wait()

    out_shape = []
    for s, a in zip(shards, axes):
        shape = list(s.shape)
        shape[a] *= N_DEV
        out_shape.append(jax.ShapeDtypeStruct(tuple(shape), s.dtype))
    any_spec = pl.BlockSpec(memory_space=pl.ANY)
    return pl.pallas_call(
        body, name=name, out_shape=out_shape,
        in_specs=[any_spec] * n, out_specs=[any_spec] * n,
        scratch_shapes=[pltpu.SemaphoreType.DMA((7 * n,)), pltpu.SemaphoreType.DMA((7 * n,)),
                        pltpu.SemaphoreType.DMA((n,))],
    )(*shards)


HBM_SPEC = pl.BlockSpec(memory_space=pltpu.HBM)
SEM_SPEC = pl.BlockSpec(memory_space=pltpu.SEMAPHORE)
ANY_SPEC = pl.BlockSpec(memory_space=pl.ANY)
SPLIT_PARAMS = pltpu.CompilerParams(has_side_effects=pltpu.SideEffectType.DATAFLOW_SIDE_EFFECTING)
N_PEERS = N_DEV - 1
TOKEN = jax.ShapeDtypeStruct((SUBLANES, LANES), F32)
VMEM_SPEC = pl.BlockSpec(memory_space=pltpu.VMEM)


def _in_hbm(arrays):
    return [pltpu.with_memory_space_constraint(a, pltpu.HBM) for a in arrays]


def _peer(m):
    x, y, c = (lax.axis_index(a) for a in MESH_AXES)
    px = 1 - x if m & 4 else x
    py = 1 - y if m & 2 else y
    pc = 1 - c if m & 1 else c
    return (px, py, pc), 4 * px + 2 * py + pc


def _my_index():
    x, y, c = (lax.axis_index(a) for a in MESH_AXES)
    return 4 * x + 2 * y + c


def _gather_copies(shard_refs, full_refs, axes, send_sems, recv_sems):
    copies = []
    for i, (shard, full) in enumerate(zip(shard_refs, full_refs)):
        mine = _block(full, axes[i], shard.shape[axes[i]], _my_index())
        for m in range(1, N_DEV):
            peer, _ = _peer(m)
            copies.append(pltpu.make_async_remote_copy(
                src_ref=shard, dst_ref=mine, send_sem=send_sems.at[N_PEERS * i + m - 1],
                recv_sem=recv_sems.at[N_PEERS * i + m - 1], device_id=peer, device_id_type=MESH))
    return copies


def _gather_start(name, shards, axes, after):
    n = len(shards)

    def body(*refs):
        shard_refs = refs[:n]
        send_sems, recv_sems, local_sems = refs[n + 1:n + 4]
        full_refs = refs[2 * n + 4:3 * n + 4]
        refs[3 * n + 4][...] = jnp.zeros(TOKEN.shape, TOKEN.dtype)
        for i in range(n):
            pltpu.make_async_copy(shard_refs[i], _block(full_refs[i], axes[i], shard_refs[i].shape[axes[i]], _my_index()),
                                  local_sems.at[i]).start()
        for cp in _gather_copies(shard_refs, full_refs, axes, send_sems, recv_sems):
            cp.start()

    fulls = []
    for s, a in zip(shards, axes):
        shape = list(s.shape)
        shape[a] *= N_DEV
        fulls.append(pltpu.HBM(tuple(shape), s.dtype))
    out = pl.pallas_call(
        body, name=name,
        out_shape=(pltpu.SemaphoreType.DMA((N_PEERS * n,)), pltpu.SemaphoreType.DMA((N_PEERS * n,)),
                   pltpu.SemaphoreType.DMA((n,)), *[pltpu.HBM(s.shape, s.dtype) for s in shards], *fulls, TOKEN),
        in_specs=[HBM_SPEC] * n + [ANY_SPEC],
        out_specs=(SEM_SPEC, SEM_SPEC, SEM_SPEC, *[HBM_SPEC] * (2 * n), VMEM_SPEC),
        input_output_aliases={i: 3 + i for i in range(n)},
        compiler_params=SPLIT_PARAMS,
    )(*_in_hbm(shards), after)
    return out[:-1], out[-1]


def _gather_wait(name, started, indices, axes, after):
    send_sems, recv_sems, local_sems = started[:3]
    n_all = (len(started) - 3) // 2
    shards = [started[3 + i] for i in indices]
    fulls = [started[3 + n_all + i] for i in indices]
    n = len(indices)

    def body(*refs):
        shard_refs, full_refs = refs[:n], refs[n:2 * n]
        send_sems, recv_sems, local_sems = refs[2 * n:2 * n + 3]
        for j, i in enumerate(indices):
            mine = _block(full_refs[j], axes[j], shard_refs[j].shape[axes[j]], _my_index())
            pltpu.make_async_copy(shard_refs[j], mine, local_sems.at[i]).wait()
            for m in range(1, N_DEV):
                peer, _ = _peer(m)
                cp = pltpu.make_async_remote_copy(
                    src_ref=shard_refs[j], dst_ref=mine, send_sem=send_sems.at[N_PEERS * i + m - 1],
                    recv_sem=recv_sems.at[N_PEERS * i + m - 1], device_id=peer, device_id_type=MESH)
                cp.wait_send()
                cp.wait_recv()

    out = pl.pallas_call(
        body, name=name,
        out_shape=tuple(pltpu.HBM(a.shape, a.dtype) for a in shards + fulls),
        in_specs=[HBM_SPEC] * (2 * n) + [SEM_SPEC] * 3 + [ANY_SPEC], out_specs=tuple([HBM_SPEC] * (2 * n)),
        input_output_aliases={i: i for i in range(2 * n)},
        compiler_params=SPLIT_PARAMS,
    )(*shards, *fulls, send_sems, recv_sems, local_sems, after)
    return out[n:]


def _exchange_start(name, fulls, axes):
    n = len(fulls)
    sizes = [f.shape[a] // N_DEV for f, a in zip(fulls, axes)]

    def body(*refs):
        ins = refs[:n]
        send_sems, recv_sems = refs[n:n + 2]
        lands = refs[2 * n + 2:3 * n + 2]
        refs[3 * n + 2][...] = jnp.zeros(TOKEN.shape, TOKEN.dtype)
        for i in range(n):
            for m in range(1, N_DEV):
                peer, index = _peer(m)
                pltpu.make_async_remote_copy(
                    src_ref=_block(ins[i], axes[i], sizes[i], index), dst_ref=lands[i].at[m - 1],
                    send_sem=send_sems.at[N_PEERS * i + m - 1], recv_sem=recv_sems.at[N_PEERS * i + m - 1],
                    device_id=peer, device_id_type=MESH).start()

    lands = []
    for f, a, size in zip(fulls, axes, sizes):
        shape = list(f.shape)
        shape[a] = size
        lands.append(pltpu.HBM((N_PEERS, *shape), f.dtype))
    out = pl.pallas_call(
        body, name=name,
        out_shape=(pltpu.SemaphoreType.DMA((N_PEERS * n,)), pltpu.SemaphoreType.DMA((N_PEERS * n,)),
                   *[pltpu.HBM(f.shape, f.dtype) for f in fulls], *lands, TOKEN),
        in_specs=[HBM_SPEC] * n, out_specs=(SEM_SPEC, SEM_SPEC, *[HBM_SPEC] * (2 * n), VMEM_SPEC),
        input_output_aliases={i: 2 + i for i in range(n)},
        compiler_params=SPLIT_PARAMS,
    )(*_in_hbm(fulls))
    return out[:-1], out[-1]


def _exchange_wait(name, started, axes, after):
    send_sems, recv_sems = started[:2]
    n = (len(started) - 2) // 2
    fulls, lands = list(started[2:2 + n]), list(started[2 + n:])
    sizes = [f.shape[a] // N_DEV for f, a in zip(fulls, axes)]

    def body(*refs):
        ins, land_refs = refs[:n], refs[n:2 * n]
        send_sems, recv_sems = refs[2 * n:2 * n + 2]
        for i in range(n):
            for m in range(1, N_DEV):
                peer, index = _peer(m)
                cp = pltpu.make_async_remote_copy(
                    src_ref=_block(ins[i], axes[i], sizes[i], index), dst_ref=land_refs[i].at[m - 1],
                    send_sem=send_sems.at[N_PEERS * i + m - 1], recv_sem=recv_sems.at[N_PEERS * i + m - 1],
                    device_id=peer, device_id_type=MESH)
                cp.wait_send()
                cp.wait_recv()

    out = pl.pallas_call(
        body, name=name,
        out_shape=tuple(pltpu.HBM(a.shape, a.dtype) for a in fulls + lands),
        in_specs=[HBM_SPEC] * (2 * n) + [SEM_SPEC] * 2 + [ANY_SPEC], out_specs=tuple([HBM_SPEC] * (2 * n)),
        input_output_aliases={i: i for i in range(2 * n)},
        compiler_params=SPLIT_PARAMS,
    )(*fulls, *lands, send_sems, recv_sems, after)
    return out[:n], out[n:]


def _sum_parts(part_refs, ndim):
    g = None
    for p_ref in part_refs:
        stacked = len(p_ref.shape) > ndim
        terms = [p_ref[s] for s in range(p_ref.shape[0])] if stacked else [p_ref[...]]
        for term in terms:
            term = term.astype(F32)
            g = term if g is None else g + term
    return g


def _adamw_update(w_ref, m_ref, v_ref, g, g_ref, d_ref, nm_ref, nv_ref):
    c1 = 1.0 - ADAM_B1 ** ADAM_STEP
    c2 = 1.0 - ADAM_B2 ** ADAM_STEP
    new_m = ADAM_B1 * m_ref[...] + (1.0 - ADAM_B1) * g
    new_v = ADAM_B2 * v_ref[...] + (1.0 - ADAM_B2) * (g * g)
    g_ref[...] = g
    nm_ref[...] = new_m
    nv_ref[...] = new_v
    d_ref[...] = -ADAM_LR * ((new_m / c1) / (jnp.sqrt(new_v / c2) + ADAM_EPS) + ADAM_WD * w_ref[...])


def _adamw_small(ws, ms, vs, stacks, loss_stack):
    n = len(ws)

    def body(*refs):
        ins, outs = refs[:4 * n + 1], refs[4 * n + 1:]
        for i in range(n):
            _adamw_update(ins[i], ins[n + i], ins[2 * n + i], _sum_parts([ins[3 * n + i]], len(ins[i].shape)),
                          outs[i], outs[n + i], outs[2 * n + i], outs[3 * n + i])
        total = ins[4 * n][0]
        for dev in range(1, N_DEV):
            total = total + ins[4 * n][dev]
        outs[4 * n][...] = total

    res = pl.pallas_call(
        body, name="adamw_small",
        out_shape=[jax.ShapeDtypeStruct(w.shape, F32) for w in ws] * 4 + [jax.ShapeDtypeStruct((1, LANES), F32)],
        compiler_params=_params(None),
    )(*ws, *ms, *vs, *stacks, loss_stack)
    return res[:n], res[n:2 * n], res[2 * n:3 * n], res[3 * n:4 * n], res[4 * n]


def _adamw(name, w, m, v, parts):
    r, c = w.shape
    tr = _tile(r, 256)
    n_parts = len(parts)

    def body(*refs):
        _adamw_update(refs[0], refs[1], refs[2], _sum_parts(refs[3:3 + n_parts], 2), *refs[3 + n_parts:])

    row = pl.BlockSpec((tr, c), lambda i: (i, 0))
    in_specs = [row, row, row]
    for p in parts:
        in_specs.append(row if p.ndim == 2 else pl.BlockSpec((p.shape[0], tr, c), lambda i: (0, i, 0)))
    return pl.pallas_call(
        body, name=name, grid=(r // tr,), in_specs=in_specs, out_specs=[row] * 4,
        out_shape=[jax.ShapeDtypeStruct((r, c), F32)] * 4,
        compiler_params=_params(("arbitrary",)),
    )(w, m, v, *parts)


SMALL = ("norm_gain", "pool_scale", "a_re", "a_im", "log_dt", "b_re", "b_im", "c_re", "c_im", "d_skip", "final_gain")
LARGE = ("w_in", "w_pool", "w_glu", "w_out", "w_ple", "w_ple_gate")
LARGE_AXIS = {"w_in": 1, "w_pool": 1, "w_glu": 1, "w_out": 0, "w_ple": 1, "w_ple_gate": 0}
WEIGHTS = ("norm_gain", "w_in", "w_pool", "pool_scale", "a_re", "a_im", "log_dt", "b_re", "b_im", "c_re", "c_im",
           "d_skip", "w_glu", "w_out", "w_ple", "w_ple_gate", "final_gain")


def kernel(x, p, norm_gain, w_in, w_pool, pool_scale, a_re, a_im, log_dt, b_re, b_im, c_re, c_im, d_skip, w_glu, w_out, w_ple, w_ple_gate, final_gain, loss_target, m_norm_gain, m_w_in, m_w_pool, m_pool_scale, m_a_re, m_a_im, m_log_dt, m_b_re, m_b_im, m_c_re, m_c_im, m_d_skip, m_w_glu, m_w_out, m_w_ple, m_w_ple_gate, m_final_gain, v_norm_gain, v_w_in, v_w_pool, v_pool_scale, v_a_re, v_a_im, v_log_dt, v_b_re, v_b_im, v_c_re, v_c_im, v_d_skip, v_w_glu, v_w_out, v_w_ple, v_w_ple_gate, v_final_gain):
    weights = dict(norm_gain=norm_gain, w_in=w_in, w_pool=w_pool, pool_scale=pool_scale, a_re=a_re, a_im=a_im,
                   log_dt=log_dt, b_re=b_re, b_im=b_im, c_re=c_re, c_im=c_im, d_skip=d_skip, w_glu=w_glu,
                   w_out=w_out, w_ple=w_ple, w_ple_gate=w_ple_gate, final_gain=final_gain)
    mom_m = dict(norm_gain=m_norm_gain, w_in=m_w_in, w_pool=m_w_pool, pool_scale=m_pool_scale, a_re=m_a_re,
                 a_im=m_a_im, log_dt=m_log_dt, b_re=m_b_re, b_im=m_b_im, c_re=m_c_re, c_im=m_c_im,
                 d_skip=m_d_skip, w_glu=m_w_glu, w_out=m_w_out, w_ple=m_w_ple, w_ple_gate=m_w_ple_gate,
                 final_gain=m_final_gain)
    mom_v = dict(norm_gain=v_norm_gain, w_in=v_w_in, w_pool=v_w_pool, pool_scale=v_pool_scale, a_re=v_a_re,
                 a_im=v_a_im, log_dt=v_log_dt, b_re=v_b_re, b_im=v_b_im, c_re=v_c_re, c_im=v_c_im,
                 d_skip=v_d_skip, w_glu=v_w_glu, w_out=v_w_out, w_ple=v_w_ple, w_ple_gate=v_w_ple_gate,
                 final_gain=v_final_gain)

    t = x.shape[1]
    xs = x.reshape(t, D_MODEL)
    ps = p.reshape(t, PLE_DIM)
    target = loss_target.reshape(t, D_MODEL)
    gain1 = norm_gain.reshape(1, D_MODEL)
    gain_f = final_gain.reshape(1, D_MODEL)
    scale_p = pool_scale.reshape(1, POOL_WIDTH)
    skip = d_skip.reshape(1, SSM_WIDTH)

    shard2d = {k: weights[k][0] for k in LARGE}
    shard_bf = {k: shard2d[k].astype(BF16) for k in LARGE}
    full = {"w_in": _all_gather("w_in_all_gather", [shard_bf["w_in"]], [LARGE_AXIS["w_in"]])[0]}
    later = [k for k in LARGE if k != "w_in"]
    later_axes = [LARGE_AXIS[k] for k in later]
    gather, gather_token = _gather_start("weights_gather_start", [shard_bf[k] for k in later], later_axes,
                                         full["w_in"])

    def arrive(k, after):
        i = later.index(k)
        full[k] = _gather_wait("gather_wait_" + k, gather, [i], [later_axes[i]], after)[0]

    ar, ai = a_re[0], a_im[0]
    ldt = log_dt.reshape(N_SSM_GROUPS, 1)
    br_t = jnp.transpose(b_re[0], (0, 2, 1))
    bi_t = jnp.transpose(b_im[0], (0, 2, 1))
    ab_re, ab_im, bb_re, bb_im = _ssm_params(ar, ai, ldt, br_t, bi_t)
    tiles = (SSM_CHUNKS, CHUNK_TILES, LANES)
    abar = jnp.concatenate([ab_re.reshape(tiles), ab_im.reshape(tiles)], axis=-1)
    w_pair = _compact_pair(bb_re, bb_im)
    c_pair = _compact_pair(c_re[0], -c_im[0])

    hn, proj = _norm1_in_proj(xs, gain1, full["w_in"], gather_token)
    pooled = _pool_fwd(proj)
    tm = _tile(t, 1024)
    arrive("w_pool", pooled)
    mixed = _mm("pool_mix", [(pooled, (tm, POOL_GROUP), lambda i, j, s: (i, j),
                              full["w_pool"], (None, POOL_GROUP, POOL_GROUP), lambda i, j, s: (j, 0, 0))],
                DOT_NN, (t // tm, N_POOL_GROUPS, 1),
                [((t, POOL_WIDTH), F32, (tm, POOL_GROUP), lambda i, j, s: (i, j))], 1)[0]
    y, gel, states = _ssm_fwd(proj, w_pair, c_pair, abar, skip)
    arrive("w_glu", gel)
    hg = _mm_nn("glu_proj", gel, full["w_glu"], [F32])[0]
    arrive("w_out", hg)
    cat, h1, h1b = _gate_out_proj(mixed, proj, hg, scale_p, full["w_out"], xs)
    arrive("w_ple", h1b)
    arrive("w_ple_gate", h1b)
    de, dq, dh2, g_final_gain, loss_part = _ple_final(h1, h1b, ps, full["w_ple_gate"], full["w_ple"], target, gain_f)

    grads = {}
    grads["w_ple_gate"] = _mm_tn("ple_gate_wgrad", h1b, dq, BF16)
    grads["w_ple"] = _mm_tn("ple_wgrad", ps, de, BF16)
    sent, tokens = {}, {}

    def send(names):
        sent[names], tokens[names[0]] = _exchange_start(
            "grads_start_" + names[0], [grads[k] for k in names], [LARGE_AXIS[k] for k in names])

    send(("w_ple_gate", "w_ple"))
    dh1, dh1b = _residual_dgrad("ple_gate_dgrad", dq, full["w_ple_gate"], dh2)
    grads["w_out"] = _mm_tn("out_wgrad", cat, dh1b, BF16)
    send(("w_out",))
    dmixed, dproj, dhg, g_pool_scale = _out_dgrad_gate_bwd(
        dh1b, full["w_out"], mixed, proj, hg, scale_p, [tokens["w_ple_gate"], tokens["w_out"]])

    tk = _tile(t, 1024)
    grads["w_pool"] = _mm("pool_wgrad", [(pooled, (tk, POOL_GROUP), lambda i, j, s: (s, i),
                                          dmixed, (tk, POOL_GROUP), lambda i, j, s: (s, i))],
                          DOT_TN, (N_POOL_GROUPS, 1, t // tk),
                          [((N_POOL_GROUPS, POOL_GROUP, POOL_GROUP), BF16, (None, POOL_GROUP, POOL_GROUP),
                            lambda i, j, s: (i, 0, 0))], t // tk)[0]
    dpooled = _mm("pool_dgrad", [(dmixed, (tm, POOL_GROUP), lambda i, j, s: (i, j),
                                  full["w_pool"], (None, POOL_GROUP, POOL_GROUP), lambda i, j, s: (j, 0, 0))],
                  DOT_NT, (t // tm, N_POOL_GROUPS, 1),
                  [((t, POOL_WIDTH), F32, (tm, POOL_GROUP), lambda i, j, s: (i, j))], 1)[0]
    dproj = _pool_bwd(dpooled, dproj)

    grads["w_glu"] = _mm_tn("glu_wgrad", gel, dhg, BF16)
    send(("w_pool", "w_glu"))

    def gelu_bwd_epilogue(acc, ex, out_refs):
        yv = ex[0][...]
        th = jnp.tanh(GELU_C * (yv + GELU_A * yv * yv * yv))
        dgelu = 0.5 * (1.0 + th) + 0.5 * yv * (1.0 - th * th) * GELU_C * (1.0 + 3.0 * GELU_A * yv * yv)
        out_refs[0][...] = acc * dgelu

    dy = _mm_nt("glu_dgrad", dhg, full["w_glu"], [F32], tk=2048, extras=[y], epilogue=gelu_bwd_epilogue,
                after=[tokens["w_pool"]])[0]
    dproj, g_c_pair, g_w_pair, g_abar, g_d_skip = _ssm_bwd(dy, proj, states, w_pair, c_pair, abar, skip, dproj)

    g_ab_re = g_abar[..., :LANES].reshape(N_SSM_GROUPS, SSM_STATE)
    g_ab_im = g_abar[..., LANES:].reshape(N_SSM_GROUPS, SSM_STATE)
    d_ar, d_ai, d_ldt, d_br_t, d_bi_t = _ssm_params_bwd(
        ar, ai, ldt, br_t, bi_t, g_ab_re, g_ab_im,
        _expand_grad(g_w_pair[..., :LANES]), _expand_grad(g_w_pair[..., LANES:]))

    def packed(a):
        return a.astype(BF16).reshape(-1, LANES)

    small_grads = dict(
        pool_scale=g_pool_scale, a_re=d_ar, a_im=d_ai, log_dt=d_ldt.reshape(1, N_SSM_GROUPS),
        b_re=packed(d_br_t), b_im=packed(d_bi_t), c_re=packed(_expand_grad(g_c_pair[..., :LANES])),
        c_im=packed(-_expand_grad(g_c_pair[..., LANES:])), d_skip=g_d_skip, final_gain=g_final_gain)
    early = [k for k in SMALL if k != "norm_gain"]
    early_sent, early_token = _gather_start(
        "small_grads_start", [small_grads[k][None] for k in early] + [jnp.broadcast_to(loss_part, (1, 1, LANES))],
        [0] * (len(early) + 1), d_ar)

    grads["w_in"] = _mm_tn("in_wgrad", hn, dproj, BF16, after=[early_token])
    send(("w_in",))
    grad_x, g_norm_gain = _in_dgrad_norm1_bwd(dproj, full["w_in"], xs, dh1, gain1, tokens["w_in"])
    late_sent, late_token = _gather_start("norm_gain_grad_start", [g_norm_gain[None]], [0], g_norm_gain)

    out_g, out_d, out_m, out_v = ({} for _ in range(4))
    me = 4 * lax.axis_index("x") + 2 * lax.axis_index("y") + lax.axis_index("c")
    after = late_token
    for names, started in sent.items():
        axes = [LARGE_AXIS[k] for k in names]
        partials, landed = _exchange_wait("grads_wait_" + names[0], started, axes, after)
        for k, axis, partial, land in zip(names, axes, partials, landed):
            shard_shape = shard2d[k].shape
            size = shard_shape[axis]
            own = lax.dynamic_slice_in_dim(partial, me * size, size, axis=axis)
            view = (-1, shard_shape[-1])
            rows = math.prod(shard_shape[:-1])
            res = _adamw("adamw_" + k, shard2d[k].reshape(view), mom_m[k][0].reshape(view), mom_v[k][0].reshape(view),
                         [own.reshape(view), land.reshape(N_PEERS, rows, shard_shape[-1])])
            out_g[k], out_d[k], out_m[k], out_v[k] = (r.reshape(weights[k].shape) for r in res)
            after = res[0]

    def b_view(a):
        return jnp.transpose(a[0], (0, 2, 1))

    views = dict(norm_gain=lambda a: a, pool_scale=lambda a: a, a_re=lambda a: a[0], a_im=lambda a: a[0],
                 log_dt=lambda a: a, b_re=b_view, b_im=b_view, c_re=lambda a: a[0], c_im=lambda a: a[0],
                 d_skip=lambda a: a, final_gain=lambda a: a.reshape(1, D_MODEL))
    landed = _gather_wait("small_grads_wait", early_sent, list(range(len(early) + 1)), [0] * (len(early) + 1), after)
    stack = dict(zip(early, landed))
    for k in ("b_re", "b_im", "c_re", "c_im"):
        stack[k] = stack[k].reshape(N_DEV, N_SSM_GROUPS, SSM_GROUP, SSM_STATE)
    stack["norm_gain"] = _gather_wait("norm_gain_grad_wait", late_sent, [0], [0], after)[0]
    *small_out, loss_row = _adamw_small(
        [views[k](weights[k]) for k in SMALL], [views[k](mom_m[k]) for k in SMALL],
        [views[k](mom_v[k]) for k in SMALL], [stack[k] for k in SMALL], landed[-1])
    loss = loss_row[0, 0]
    for out, res in zip((out_g, out_d, out_m, out_v), small_out):
        for k, r in zip(SMALL, res):
            if k in ("b_re", "b_im"):
                r = jnp.transpose(r, (0, 2, 1))
            out[k] = r.reshape(weights[k].shape)

    return (loss, grad_x.reshape(x.shape), *[out_g[k] for k in WEIGHTS], *[out_d[k] for k in WEIGHTS],
            *[out_m[k] for k in WEIGHTS], *[out_v[k] for k in WEIGHTS])
```

```python
import functools
import math

import jax
import jax.numpy as jnp
from jax import lax
from jax.experimental import pallas as pl
from jax.experimental.pallas import tpu as pltpu

F32 = jnp.float32
BF16 = jnp.bfloat16
MESH = pl.DeviceIdType.MESH
MESH_AXES = ("x", "y", "c")
N_DEV = 8

D_MODEL = 2048
POOL_WIDTH = 1024
SSM_WIDTH = 1024
N_POOL_GROUPS = 4
POOL_GROUP = 256
SSM_GROUP = 16
N_SSM_GROUPS = 64
SSM_STATE = 64
SSM_FLAT = N_SSM_GROUPS * SSM_STATE
SSM_CHUNKS = 4
CHUNK_IN = SSM_WIDTH // SSM_CHUNKS
CHUNK_STATE = SSM_FLAT // SSM_CHUNKS
PLE_DIM = 256
EPS = 1e-6
A_RE_MAX = -1e-4
ADAM_LR = 0.001
ADAM_B1 = 0.9
ADAM_B2 = 0.999
ADAM_EPS = 1e-08
ADAM_WD = 0.01
ADAM_STEP = 10
GELU_C = math.sqrt(2.0 / math.pi)
GELU_A = 0.044715

SUBLANES = 8
LANES = 128
VMEM_LIMIT_BYTES = 48 * 1024 * 1024

DOT_NN = (((1,), (0,)), ((), ()))
DOT_NT = (((1,), (1,)), ((), ()))
DOT_TN = (((0,), (0,)), ((), ()))


def _tile(n, pref):
    return pref if n % pref == 0 else n


def _params(sem):
    return pltpu.CompilerParams(dimension_semantics=sem, vmem_limit_bytes=VMEM_LIMIT_BYTES)


def _sigmoid(v):
    return 1.0 / (1.0 + jnp.exp(-v))


def _silu_and_grad(v):
    s = _sigmoid(v)
    return v * s, s * (1.0 + v * (1.0 - s))


def _mm(name, pairs, dims, grid, outs, k_steps, extras=(), epilogue=None):
    n_pairs, n_ex, n_out = len(pairs), len(extras), len(outs)
    acc_shape = tuple(d for d in outs[0][2] if d is not None)
    if epilogue is None:
        def epilogue(acc, ex, out_refs):
            out_refs[0][...] = acc.astype(out_refs[0].dtype)

    def body(*refs):
        ab = refs[:2 * n_pairs]
        ex = refs[2 * n_pairs:2 * n_pairs + n_ex]
        out_refs = refs[2 * n_pairs + n_ex:2 * n_pairs + n_ex + n_out]
        acc = refs[-1]
        k = pl.program_id(2)

        @pl.when(k == 0)
        def _():
            acc[...] = jnp.zeros_like(acc)

        part = None
        for q in range(n_pairs):
            d = lax.dot_general(ab[2 * q][...].astype(BF16), ab[2 * q + 1][...].astype(BF16), dims,
                                preferred_element_type=F32)
            part = d if part is None else part + d
        acc[...] += part

        @pl.when(k == k_steps - 1)
        def _():
            epilogue(acc[...], ex, out_refs)

    in_specs, operands = [], []
    for a, a_blk, a_map, b, b_blk, b_map in pairs:
        in_specs += [pl.BlockSpec(a_blk, a_map), pl.BlockSpec(b_blk, b_map)]
        operands += [a, b]
    for e, e_blk, e_map in extras:
        in_specs.append(pl.BlockSpec(e_blk, e_map))
        operands.append(e)
    return pl.pallas_call(
        body, name=name, grid=grid, in_specs=in_specs,
        out_specs=[pl.BlockSpec(o[2], o[3]) for o in outs],
        out_shape=[jax.ShapeDtypeStruct(o[0], o[1]) for o in outs],
        scratch_shapes=[pltpu.VMEM(acc_shape, F32)],
        compiler_params=_params(("arbitrary", "arbitrary", "arbitrary")),
    )(*operands)


def _after(tokens):
    return [(tok, tok.shape, lambda i, j, s: (0, 0)) for tok in tokens]


def _mm_nn(name, a, b, out_dtypes, tm=1024, tn=1024, tk=1024, a_col0=0, extras=(), epilogue=None, after=()):
    m, n = a.shape[0], b.shape[1]
    k = b.shape[0]
    tm, tn, tk = _tile(m, tm), _tile(n, tn), _tile(k, tk)
    outs = [((m, n), dt, (tm, tn), lambda i, j, s: (i, j)) for dt in out_dtypes]
    ex = [(e, (tm, tn), lambda i, j, s: (i, j)) for e in extras] + _after(after)
    return _mm(name, [(a, (tm, tk), lambda i, j, s: (i, a_col0 + s), b, (tk, tn), lambda i, j, s: (s, j))],
               DOT_NN, (m // tm, n // tn, k // tk), outs, k // tk, ex, epilogue)


def _mm_nt(name, a, b, out_dtypes, tm=1024, tn=1024, tk=1024, extras=(), epilogue=None, after=()):
    m, kk = a.shape
    n = b.shape[0]
    tm, tn, tk = _tile(m, tm), _tile(n, tn), _tile(kk, tk)
    outs = [((m, n), dt, (tm, tn), lambda i, j, s: (i, j)) for dt in out_dtypes]
    ex = [(e, (tm, tn), lambda i, j, s: (i, j)) for e in extras] + _after(after)
    return _mm(name, [(a, (tm, tk), lambda i, j, s: (i, s), b, (tn, tk), lambda i, j, s: (j, s))],
               DOT_NT, (m // tm, n // tn, kk // tk), outs, kk // tk, ex, epilogue)


def _mm_tn(name, a, b, out_dtype, tm=512, tn=2048, tk=1024, after=(), b_blocks=None):
    m, kk = a.shape
    n = b.shape[1]
    first, step = 0, 1
    if b_blocks:
        tn, first, step = b_blocks
        n = (b.shape[1] // tn - first + step - 1) // step * tn
    tm, tn, tk = _tile(kk, tm), _tile(n, tn), _tile(m, tk)
    outs = [((kk, n), out_dtype, (tm, tn), lambda i, j, s: (i, j))]
    return _mm(name, [(a, (tk, tm), lambda i, j, s: (s, i), b, (tk, tn), lambda i, j, s: (s, first + step * j))],
               DOT_TN, (kk // tm, n // tn, m // tk), outs, m // tk, _after(after))[0]


ROW_TILE = 256


def _norm1_in_proj(x, gain, w_in, after):
    t = x.shape[0]
    tm = _tile(t, ROW_TILE)
    n = w_in.shape[1]

    def body(x_ref, g_ref, w_ref, _, hn_ref, proj_ref):
        xv = x_ref[...]
        r = lax.rsqrt(jnp.mean(xv * xv, axis=-1, keepdims=True) + EPS)
        hn = (xv * r * g_ref[...]).astype(BF16)
        hn_ref[...] = hn
        proj_ref[...] = jnp.dot(hn, w_ref[...], preferred_element_type=F32)

    row = pl.BlockSpec((tm, D_MODEL), lambda i: (i, 0))
    return pl.pallas_call(
        body, name="norm1_in_proj", grid=(t // tm,),
        in_specs=[row, pl.BlockSpec((1, D_MODEL), lambda i: (0, 0)), _resident(w_in.shape),
                  pl.BlockSpec(after.shape, lambda i: (0, 0))],
        out_specs=[row, pl.BlockSpec((tm, n), lambda i: (i, 0))],
        out_shape=[jax.ShapeDtypeStruct((t, D_MODEL), BF16), jax.ShapeDtypeStruct((t, n), F32)],
        compiler_params=_params(("arbitrary",)),
    )(x, gain, w_in, after)


def _in_dgrad_norm1_bwd(dproj, w_in, x, dh1, gain, after):
    t = x.shape[0]
    tm = _tile(t, ROW_TILE)

    def body(dp_ref, w_ref, x_ref, dh1_ref, g_ref, _, dx_ref, gg_ref):
        @pl.when(pl.program_id(0) == 0)
        def _():
            gg_ref[...] = jnp.zeros_like(gg_ref)

        dhn = lax.dot_general(dp_ref[...], w_ref[...], DOT_NT, preferred_element_type=F32)
        xv = x_ref[...]
        r = lax.rsqrt(jnp.mean(xv * xv, axis=-1, keepdims=True) + EPS)
        xh = xv * r
        gg_ref[...] += jnp.sum(dhn * xh, axis=0, keepdims=True)
        dxh = dhn * g_ref[...]
        dx_ref[...] = dh1_ref[...] + r * (dxh - xh * jnp.mean(dxh * xh, axis=-1, keepdims=True))

    row = pl.BlockSpec((tm, D_MODEL), lambda i: (i, 0))
    vec = pl.BlockSpec((1, D_MODEL), lambda i: (0, 0))
    return pl.pallas_call(
        body, name="in_dgrad_norm1_bwd", grid=(t // tm,),
        in_specs=[pl.BlockSpec((tm, dproj.shape[1]), lambda i: (i, 0)), _resident(w_in.shape), row, row, vec,
                  pl.BlockSpec(after.shape, lambda i: (0, 0))],
        out_specs=[row, vec],
        out_shape=[jax.ShapeDtypeStruct((t, D_MODEL), F32), jax.ShapeDtypeStruct((1, D_MODEL), F32)],
        compiler_params=_params(("arbitrary",)),
    )(dproj, w_in, x, dh1, gain, after)


def _pool_counts(t, width, group):
    row = lax.broadcasted_iota(jnp.int32, (t, width), 0)
    window = jnp.left_shift(jnp.int32(2), group)
    return row, jnp.minimum(row + 1, window).astype(F32)


def _select_window(group, s2, s4, s8, s16):
    return jnp.where(group == 0, s2, jnp.where(group == 1, s4, jnp.where(group == 2, s8, s16)))


def _pool_fwd(proj):
    t = proj.shape[0]
    tc = LANES

    def body(u_ref, o_ref):
        group = pl.program_id(0) // (POOL_GROUP // tc)
        v = u_ref[...]
        row, count = _pool_counts(t, tc, group)

        def down(a, j):
            return jnp.where(row >= j, pltpu.roll(a, j, 0), 0.0)

        s2 = v + down(v, 1)
        s4 = s2 + down(s2, 2)
        s8 = s4 + down(s4, 4)
        s16 = s8 + down(s8, 8)
        o_ref[...] = (_select_window(group, s2, s4, s8, s16) / count - v).astype(BF16)

    return pl.pallas_call(
        body, name="pool_fwd", grid=(POOL_WIDTH // tc,),
        in_specs=[pl.BlockSpec((t, tc), lambda j: (0, j))],
        out_specs=pl.BlockSpec((t, tc), lambda j: (0, j)),
        out_shape=jax.ShapeDtypeStruct((t, POOL_WIDTH), BF16),
        compiler_params=_params(("arbitrary",)),
    )(proj)


def _pool_bwd(dpooled, dproj, after):
    t = dpooled.shape[0]
    tc = LANES

    def body(d_ref, _, __, o_ref):
        group = pl.program_id(0) // (POOL_GROUP // tc)
        dp = d_ref[...]
        row, count = _pool_counts(t, tc, group)
        r = dp / count

        def up(a, j):
            return jnp.where(row < t - j, pltpu.roll(a, t - j, 0), 0.0)

        s2 = r + up(r, 1)
        s4 = s2 + up(s2, 2)
        s8 = s4 + up(s4, 4)
        s16 = s8 + up(s8, 8)
        o_ref[...] = (_select_window(group, s2, s4, s8, s16) - dp).astype(BF16)

    return pl.pallas_call(
        body, name="pool_bwd", grid=(POOL_WIDTH // tc,),
        in_specs=[pl.BlockSpec((t, tc), lambda j: (0, j)), pl.BlockSpec(memory_space=pl.ANY),
                  pl.BlockSpec(after.shape, lambda j: (0, 0))],
        out_specs=pl.BlockSpec((t, tc), lambda j: (0, j)),
        out_shape=jax.ShapeDtypeStruct(dproj.shape, dproj.dtype),
        input_output_aliases={1: 0},
        compiler_params=_params(("arbitrary",)),
    )(dpooled, dproj, after)


def _gate_out_proj(mixed, proj, hg, pool_scale, w_out, x):
    t = mixed.shape[0]
    tm = _tile(t, ROW_TILE)

    def body(mx_ref, ga_ref, gb_ref, hg_ref, ps_ref, w_ref, x_ref, cat_ref, h1_ref, h1b_ref):
        silu_a, _ = _silu_and_grad(ga_ref[...])
        cat_ref[:, :POOL_WIDTH] = (mx_ref[...] * ps_ref[...] * silu_a).astype(BF16)
        silu_b, _ = _silu_and_grad(gb_ref[...])
        sb = hg_ref[:, :SSM_WIDTH] * _sigmoid(hg_ref[:, SSM_WIDTH:])
        cat_ref[:, POOL_WIDTH:] = (sb * silu_b).astype(BF16)
        h1 = x_ref[...] + jnp.dot(cat_ref[...], w_ref[...], preferred_element_type=F32)
        h1_ref[...] = h1
        h1b_ref[...] = h1.astype(BF16)

    row = pl.BlockSpec((tm, D_MODEL), lambda i: (i, 0))
    return pl.pallas_call(
        body, name="gate_out_proj", grid=(t // tm,),
        in_specs=[pl.BlockSpec((tm, POOL_WIDTH), lambda i: (i, 0)),
                  pl.BlockSpec((tm, POOL_WIDTH), lambda i: (i, 1)),
                  pl.BlockSpec((tm, SSM_WIDTH), lambda i: (i, 3)),
                  pl.BlockSpec((tm, 2 * SSM_WIDTH), lambda i: (i, 0)),
                  pl.BlockSpec((1, POOL_WIDTH), lambda i: (0, 0)), _resident(w_out.shape), row],
        out_specs=[row, row, row],
        out_shape=[jax.ShapeDtypeStruct((t, D_MODEL), BF16), jax.ShapeDtypeStruct((t, D_MODEL), F32),
                   jax.ShapeDtypeStruct((t, D_MODEL), BF16)],
        compiler_params=_params(("arbitrary",)),
    )(mixed, proj, proj, hg, pool_scale, w_out, x)


def _residual_dgrad(name, dy, w, residual):
    t = dy.shape[0]
    tm = _tile(t, ROW_TILE)
    n = w.shape[0]

    def body(dy_ref, w_ref, r_ref, o_ref, ob_ref):
        o = r_ref[...] + lax.dot_general(dy_ref[...], w_ref[...], DOT_NT, preferred_element_type=F32)
        o_ref[...] = o
        ob_ref[...] = o.astype(BF16)

    out = pl.BlockSpec((tm, n), lambda i: (i, 0))
    return pl.pallas_call(
        body, name=name, grid=(t // tm,),
        in_specs=[pl.BlockSpec((tm, dy.shape[1]), lambda i: (i, 0)), _resident(w.shape), out],
        out_specs=[out, out],
        out_shape=[jax.ShapeDtypeStruct((t, n), F32), jax.ShapeDtypeStruct((t, n), BF16)],
        compiler_params=_params(("arbitrary",)),
    )(dy, w, residual)


def _out_dgrad_gate_bwd(dh1b, w_out, mixed, proj, hg, pool_scale, after):
    t = mixed.shape[0]
    tm = _tile(t, ROW_TILE)
    n_after = len(after)

    def body(dh_ref, w_ref, mx_ref, ga_ref, gb_ref, hg_ref, ps_ref, *rest):
        dmx_ref, dp_ref, dhg_ref, gps_ref = rest[n_after:]

        @pl.when(pl.program_id(0) == 0)
        def _():
            gps_ref[...] = jnp.zeros_like(gps_ref)

        dcat = lax.dot_general(dh_ref[...], w_ref[...], DOT_NT, preferred_element_type=F32)
        ps = ps_ref[...]
        mx = mx_ref[...]
        dya = dcat[:, :POOL_WIDTH]
        silu_a, dsilu_a = _silu_and_grad(ga_ref[...])
        dpa = dya * silu_a
        gps_ref[...] += jnp.sum(dpa * mx, axis=0, keepdims=True)
        dmx_ref[...] = (dpa * ps).astype(BF16)
        dp_ref[:, :POOL_WIDTH] = jnp.zeros((tm, POOL_WIDTH), BF16)
        dp_ref[:, POOL_WIDTH:2 * POOL_WIDTH] = (dya * mx * ps * dsilu_a).astype(BF16)

        dyb = dcat[:, POOL_WIDTH:]
        silu_b, dsilu_b = _silu_and_grad(gb_ref[...])
        h_a = hg_ref[:, :SSM_WIDTH]
        sg = _sigmoid(hg_ref[:, SSM_WIDTH:])
        dsb = dyb * silu_b
        dp_ref[:, 2 * POOL_WIDTH:2 * POOL_WIDTH + SSM_WIDTH] = jnp.zeros((tm, SSM_WIDTH), BF16)
        dp_ref[:, 2 * POOL_WIDTH + SSM_WIDTH:] = (dyb * h_a * sg * dsilu_b).astype(BF16)
        dhg_ref[:, :SSM_WIDTH] = (dsb * sg).astype(BF16)
        dhg_ref[:, SSM_WIDTH:] = (dsb * h_a * sg * (1.0 - sg)).astype(BF16)

    half = pl.BlockSpec((tm, POOL_WIDTH), lambda i: (i, 0))
    full = pl.BlockSpec((tm, D_MODEL), lambda i: (i, 0))
    vec = pl.BlockSpec((1, POOL_WIDTH), lambda i: (0, 0))
    proj_width = 2 * POOL_WIDTH + 2 * SSM_WIDTH
    return pl.pallas_call(
        body, name="out_dgrad_gate_bwd", grid=(t // tm,),
        in_specs=[full, _resident(w_out.shape), half,
                  pl.BlockSpec((tm, POOL_WIDTH), lambda i: (i, 1)),
                  pl.BlockSpec((tm, SSM_WIDTH), lambda i: (i, 3)),
                  full, vec] + [pl.BlockSpec(tok.shape, lambda i: (0, 0)) for tok in after],
        out_specs=[half, pl.BlockSpec((tm, proj_width), lambda i: (i, 0)), full, vec],
        out_shape=[jax.ShapeDtypeStruct((t, POOL_WIDTH), BF16), jax.ShapeDtypeStruct((t, proj_width), BF16),
                   jax.ShapeDtypeStruct((t, 2 * SSM_WIDTH), BF16),
                   jax.ShapeDtypeStruct((1, POOL_WIDTH), F32)],
        compiler_params=_params(("arbitrary",)),
    )(dh1b, w_out, mixed, proj, proj, hg, pool_scale, *after)


def _ple_final(h1, h1b, p, w_gate, w_ple, target, gain):
    t = h1.shape[0]
    tm = _tile(t, 256)

    def body(h1_ref, h1b_ref, p_ref, wg_ref, wp_ref, tg_ref, g_ref, de_ref, dq_ref, dh2_ref, gg_ref, loss_ref):
        @pl.when(pl.program_id(0) == 0)
        def _():
            gg_ref[...] = jnp.zeros_like(gg_ref)
            loss_ref[...] = jnp.zeros_like(loss_ref)

        ev = jnp.dot(p_ref[...].astype(BF16), wp_ref[...], preferred_element_type=F32)
        sg = _sigmoid(jnp.dot(h1b_ref[...], wg_ref[...], preferred_element_type=F32))
        h2 = h1_ref[...] + ev * sg
        r = lax.rsqrt(jnp.mean(h2 * h2, axis=-1, keepdims=True) + EPS)
        n = h2 * r
        gain_v = g_ref[...]
        diff = n * gain_v - tg_ref[...]
        row_loss = jnp.sum(diff * diff, axis=-1, keepdims=True)
        loss_ref[...] += (0.5 / D_MODEL) * jnp.sum(row_loss, axis=0, keepdims=True)
        dout = diff * (1.0 / D_MODEL)
        gg_ref[...] += jnp.sum(dout * n, axis=0, keepdims=True)
        dn = dout * gain_v
        dh2 = r * (dn - n * jnp.mean(dn * n, axis=-1, keepdims=True))
        dh2_ref[...] = dh2
        de_ref[...] = (dh2 * sg).astype(BF16)
        dq_ref[...] = (dh2 * ev * sg * (1.0 - sg)).astype(BF16)

    row = pl.BlockSpec((tm, D_MODEL), lambda i: (i, 0))
    vec = pl.BlockSpec((1, D_MODEL), lambda i: (0, 0))
    return pl.pallas_call(
        body, name="ple_final", grid=(t // tm,),
        in_specs=[row, row, pl.BlockSpec((tm, PLE_DIM), lambda i: (i, 0)), _resident((D_MODEL, D_MODEL)),
                  _resident((PLE_DIM, D_MODEL)), row, vec],
        out_specs=[row, row, row, vec, pl.BlockSpec((1, 1), lambda i: (0, 0))],
        out_shape=[jax.ShapeDtypeStruct((t, D_MODEL), BF16), jax.ShapeDtypeStruct((t, D_MODEL), BF16),
                   jax.ShapeDtypeStruct((t, D_MODEL), F32), jax.ShapeDtypeStruct((1, D_MODEL), F32),
                   jax.ShapeDtypeStruct((1, 1), F32)],
        compiler_params=_params(("arbitrary",)),
    )(h1, h1b, p, w_gate, w_ple, target, gain)


def _zoh(a_re, a_im, log_dt, b_re_t, b_im_t):
    lam_re = jnp.minimum(a_re, A_RE_MAX)
    lam_im = a_im
    dt = jnp.exp(log_dt)
    mag = jnp.exp(lam_re * dt)
    ang = lam_im * dt
    ab_re = mag * jnp.cos(ang)
    ab_im = mag * jnp.sin(ang)
    den = lam_re * lam_re + lam_im * lam_im
    n_re = ab_re - 1.0
    n_im = ab_im
    q_re = (n_re * lam_re + n_im * lam_im) / den
    q_im = (n_im * lam_re - n_re * lam_im) / den
    bb_re = q_re[:, None, :] * b_re_t - q_im[:, None, :] * b_im_t
    bb_im = q_re[:, None, :] * b_im_t + q_im[:, None, :] * b_re_t
    return ab_re, ab_im, bb_re, bb_im


def _ssm_params(a_re, a_im, log_dt, b_re_t, b_im_t):
    def body(are_ref, aim_ref, dt_ref, bre_ref, bim_ref, abre_ref, abim_ref, bbre_ref, bbim_ref):
        ab_re, ab_im, bb_re, bb_im = _zoh(are_ref[...], aim_ref[...], dt_ref[...], bre_ref[...], bim_ref[...])
        abre_ref[...] = ab_re
        abim_ref[...] = ab_im
        bbre_ref[...] = bb_re
        bbim_ref[...] = bb_im

    return pl.pallas_call(
        body, name="ssm_params",
        out_shape=[jax.ShapeDtypeStruct(a_re.shape, F32), jax.ShapeDtypeStruct(a_re.shape, F32),
                   jax.ShapeDtypeStruct(b_re_t.shape, F32), jax.ShapeDtypeStruct(b_re_t.shape, F32)],
        compiler_params=_params(None),
    )(a_re, a_im, log_dt, b_re_t, b_im_t)


def _ssm_params_bwd(a_re, a_im, log_dt, b_re_t, b_im_t, g_ab_re, g_ab_im, g_bb_re, g_bb_im):
    def body(are_ref, aim_ref, dt_ref, bre_ref, bim_ref, gar_ref, gai_ref, gbr_ref, gbi_ref,
             o_are, o_aim, o_dt, o_bre, o_bim):
        _, vjp = jax.vjp(_zoh, are_ref[...], aim_ref[...], dt_ref[...], bre_ref[...], bim_ref[...])
        d_are, d_aim, d_dt, d_bre, d_bim = vjp((gar_ref[...], gai_ref[...], gbr_ref[...], gbi_ref[...]))
        o_are[...] = d_are
        o_aim[...] = d_aim
        o_dt[...] = d_dt
        o_bre[...] = d_bre
        o_bim[...] = d_bim

    ins = (a_re, a_im, log_dt, b_re_t, b_im_t)
    return pl.pallas_call(
        body, name="ssm_params_bwd",
        out_shape=[jax.ShapeDtypeStruct(v.shape, F32) for v in ins],
        compiler_params=_params(None),
    )(*ins, g_ab_re, g_ab_im, g_bb_re, g_bb_im)


CHUNK_TILES = CHUNK_STATE // LANES
CH_PER_TILE = CHUNK_IN // CHUNK_TILES
PAIR = 2 * LANES
SSM_ROWS = 256
SCAN_STEPS = 8
U_COLUMN_BLOCK = 2 * POOL_WIDTH // SSM_WIDTH


def _own_half():
    r = lax.broadcasted_iota(jnp.int32, (CHUNK_IN, LANES), 0) // SSM_GROUP % 2
    c = lax.broadcasted_iota(jnp.int32, (CHUNK_IN, LANES), 1) // SSM_STATE
    return (r == c)[None]


def _compact_weight(w):
    tiled = jnp.tile(w.reshape(SSM_CHUNKS, CHUNK_IN, SSM_STATE), (1, 1, 2))
    return jnp.where(_own_half(), tiled, 0.0)


def _compact_pair(w_a, w_b):
    return jnp.concatenate([_compact_weight(w_a), _compact_weight(w_b)], axis=-1).astype(BF16)


def _expand_grad(g):
    kept = jnp.where(_own_half(), g, 0.0)
    return kept.reshape(SSM_CHUNKS, CHUNK_IN, 2, SSM_STATE).sum(axis=2).reshape(N_SSM_GROUPS, SSM_GROUP, SSM_STATE)


TILES_PER_BLOCK = LANES // CH_PER_TILE
IN_BLOCKS = CHUNK_IN // LANES


def _tile_masks():
    j = lax.broadcasted_iota(jnp.int32, (CHUNK_TILES, LANES), 0) % TILES_PER_BLOCK
    lane = lax.broadcasted_iota(jnp.int32, (CHUNK_TILES, LANES), 1) // CH_PER_TILE
    return (j == lane).astype(F32)


def _tile_rows(ref, j, tt):
    return ref.at[j // TILES_PER_BLOCK, pl.ds(j, tt, stride=CHUNK_TILES), :]


def _spread(ref, v, masks):
    tt = v.shape[0]
    for j in range(CHUNK_TILES):
        block = LANES * (j // TILES_PER_BLOCK)
        _tile_rows(ref, j, tt)[...] = v[:, block:block + LANES] * masks[j:j + 1, :]
    return jnp.concatenate([ref[b] for b in range(IN_BLOCKS)], axis=1).astype(BF16)


def _gather(ref, full, masks):
    tt = full.shape[0] // CHUNK_TILES
    for b in range(IN_BLOCKS):
        ref[b] = full[:, b * LANES:(b + 1) * LANES]
    out = []
    for b in range(IN_BLOCKS):
        acc = None
        for j in range(b * TILES_PER_BLOCK, (b + 1) * TILES_PER_BLOCK):
            part = _tile_rows(ref, j, tt)[...] * masks[j:j + 1, :]
            acc = part if acc is None else acc + part
        out.append(acc)
    return jnp.concatenate(out, axis=1)


def _resident(shape):
    return pl.BlockSpec(shape, lambda i: (0,) * len(shape), pipeline_mode=pl.Buffered(1))


def _halves(ref, k, rows=slice(None)):
    return ref[k, rows, :LANES], ref[k, rows, LANES:]


def _ssm_fwd(proj, w2, c2, a2, d_skip):
    t = proj.shape[0]
    tt = _tile(t, SSM_ROWS)
    rows = tt * CHUNK_TILES

    def body(u_ref, w_ref, c_ref, a_ref, d_ref, y_ref, gel_ref, s_ref, carry, spread_ref, full_ref):
        @pl.when(pl.program_id(0) == 0)
        def _():
            carry[...] = jnp.zeros_like(carry)
            spread_ref[...] = jnp.zeros_like(spread_ref)

        mask = _tile_masks()
        u = u_ref[...]
        for k in range(SSM_CHUNKS):
            uk = _spread(spread_ref, u[:, k * CHUNK_IN:(k + 1) * CHUNK_IN], mask)
            s_ref[k] = jnp.dot(uk, w_ref[k], preferred_element_type=F32)

        abar = [_halves(a_ref, k) for k in range(SSM_CHUNKS)]

        def steps(i, state):
            for v in range(SCAN_STEPS):
                r = pl.ds(pl.multiple_of((i * SCAN_STEPS + v) * CHUNK_TILES, CHUNK_TILES), CHUNK_TILES)
                new = []
                for k, ((a_re, a_im), (s_re, s_im)) in enumerate(zip(abar, state)):
                    b_re, b_im = _halves(s_ref, k, r)
                    s_re, s_im = a_re * s_re - a_im * s_im + b_re, a_re * s_im + a_im * s_re + b_im
                    s_ref[k, r, :LANES] = s_re
                    s_ref[k, r, LANES:] = s_im
                    new.append((s_re, s_im))
                state = tuple(new)
            return state

        state = lax.fori_loop(0, tt // SCAN_STEPS, steps, tuple(_halves(carry, k) for k in range(SSM_CHUNKS)))
        for k, (s_re, s_im) in enumerate(state):
            carry[k, :, :LANES] = s_re
            carry[k, :, LANES:] = s_im

        for k in range(SSM_CHUNKS):
            cols = slice(k * CHUNK_IN, (k + 1) * CHUNK_IN)
            full = lax.dot_general(s_ref[k].astype(BF16), c_ref[k], DOT_NT, preferred_element_type=F32)
            y = _gather(full_ref, full, mask) + d_ref[:, cols] * u[:, cols]
            y_ref[:, cols] = y
            gel_ref[:, cols] = (0.5 * y * (1.0 + jnp.tanh(GELU_C * (y + GELU_A * y * y * y)))).astype(BF16)

    weight = _resident((SSM_CHUNKS, CHUNK_IN, PAIR))
    tokens = pl.BlockSpec((tt, SSM_WIDTH), lambda i: (i, 0))
    return pl.pallas_call(
        body, name="ssm_fwd", grid=(t // tt,),
        in_specs=[pl.BlockSpec((tt, SSM_WIDTH), lambda i: (i, U_COLUMN_BLOCK)), weight, weight,
                  _resident((SSM_CHUNKS, CHUNK_TILES, PAIR)), _resident((1, SSM_WIDTH))],
        out_specs=[tokens, tokens, pl.BlockSpec((SSM_CHUNKS, rows, PAIR), lambda i: (0, i, 0))],
        out_shape=[jax.ShapeDtypeStruct((t, SSM_WIDTH), F32), jax.ShapeDtypeStruct((t, SSM_WIDTH), BF16),
                   jax.ShapeDtypeStruct((SSM_CHUNKS, t * CHUNK_TILES, PAIR), F32)],
        scratch_shapes=[pltpu.VMEM((SSM_CHUNKS, CHUNK_TILES, PAIR), F32), pltpu.VMEM((IN_BLOCKS, rows, LANES), F32),
                        pltpu.VMEM((IN_BLOCKS, rows, LANES), F32)],
        compiler_params=_params(("arbitrary",)),
    )(proj, w2, c2, a2, d_skip)


def _ssm_bwd(dy, proj, s, w2, c2, a2, d_skip, dproj):
    t = dy.shape[0]
    tt = _tile(t, SSM_ROWS)
    rows = tt * CHUNK_TILES
    n_chunks = t // tt

    def body(dy_ref, u_ref, s_ref, w_ref, c_ref, a_ref, d_ref, _, du_ref, gc_ref, gw_ref, ga_ref, gd_ref, z_ref, carry,
             spread_ref, full_ref):
        @pl.when(pl.program_id(0) == 0)
        def _():
            for r in (carry, gc_ref, gw_ref, ga_ref, gd_ref, spread_ref):
                r[...] = jnp.zeros_like(r)

        mask = _tile_masks()
        dy_v = dy_ref[...]
        u = u_ref[...]
        gd_ref[...] += jnp.sum(dy_v * u, axis=0, keepdims=True)
        for k in range(SSM_CHUNKS):
            dk = _spread(spread_ref, dy_v[:, k * CHUNK_IN:(k + 1) * CHUNK_IN], mask)
            z_ref[k] = jnp.dot(dk, c_ref[k], preferred_element_type=F32)
            gc_ref[k] += lax.dot_general(dk, s_ref[k].astype(BF16), DOT_TN, preferred_element_type=F32)

        abar = [_halves(a_ref, k) for k in range(SSM_CHUNKS)]

        def steps(i, state):
            zs, gs = state
            for v in range(SCAN_STEPS):
                tok = tt - 1 - (i * SCAN_STEPS + v)
                r = pl.ds(pl.multiple_of(tok * CHUNK_TILES, CHUNK_TILES), CHUNK_TILES)
                new_z, new_g = [], []
                for k, ((a_re, a_im), (z_re, z_im), (g_re, g_im)) in enumerate(zip(abar, zs, gs)):
                    s_re, s_im = _halves(s_ref, k, r)
                    g_re = g_re + z_re * s_re + z_im * s_im
                    g_im = g_im + z_im * s_re - z_re * s_im
                    d_re, d_im = _halves(z_ref, k, r)
                    z_re, z_im = d_re + a_re * z_re + a_im * z_im, d_im + a_re * z_im - a_im * z_re
                    z_ref[k, r, :LANES] = z_re
                    z_ref[k, r, LANES:] = z_im
                    new_z.append((z_re, z_im))
                    new_g.append((g_re, g_im))
                zs, gs = tuple(new_z), tuple(new_g)
            return zs, gs

        zs, gs = lax.fori_loop(0, tt // SCAN_STEPS, steps,
                               (tuple(_halves(carry, k) for k in range(SSM_CHUNKS)),
                                tuple(_halves(ga_ref, k) for k in range(SSM_CHUNKS))))
        for k in range(SSM_CHUNKS):
            carry[k, :, :LANES], carry[k, :, LANES:] = zs[k]
            ga_ref[k, :, :LANES], ga_ref[k, :, LANES:] = gs[k]

        for k in range(SSM_CHUNKS):
            cols = slice(k * CHUNK_IN, (k + 1) * CHUNK_IN)
            zb = z_ref[k].astype(BF16)
            full = lax.dot_general(zb, w_ref[k], DOT_NT, preferred_element_type=F32)
            du_ref[:, cols] = (_gather(full_ref, full, mask) + d_ref[:, cols] * dy_v[:, cols]).astype(BF16)
            uk = _spread(spread_ref, u[:, cols], mask)
            gw_ref[k] += lax.dot_general(uk, zb, DOT_TN, preferred_element_type=F32)

    weight = _resident((SSM_CHUNKS, CHUNK_IN, PAIR))
    tokens = pl.BlockSpec((tt, SSM_WIDTH), lambda i: (n_chunks - 1 - i, 0))
    grad = pl.BlockSpec((SSM_CHUNKS, CHUNK_IN, PAIR), lambda i: (0, 0, 0))
    return pl.pallas_call(
        body, name="ssm_bwd", grid=(n_chunks,),
        in_specs=[tokens, pl.BlockSpec((tt, SSM_WIDTH), lambda i: (n_chunks - 1 - i, U_COLUMN_BLOCK)),
                  pl.BlockSpec((SSM_CHUNKS, rows, PAIR), lambda i: (0, n_chunks - 1 - i, 0)), weight, weight,
                  _resident((SSM_CHUNKS, CHUNK_TILES, PAIR)), _resident((1, SSM_WIDTH)),
                  pl.BlockSpec(memory_space=pl.ANY)],
        out_specs=[pl.BlockSpec((tt, SSM_WIDTH), lambda i: (n_chunks - 1 - i, U_COLUMN_BLOCK)), grad, grad,
                   pl.BlockSpec((SSM_CHUNKS, CHUNK_TILES, PAIR), lambda i: (0, 0, 0)),
                   pl.BlockSpec((1, SSM_WIDTH), lambda i: (0, 0))],
        out_shape=[jax.ShapeDtypeStruct(dproj.shape, dproj.dtype), jax.ShapeDtypeStruct((SSM_CHUNKS, CHUNK_IN, PAIR), F32),
                   jax.ShapeDtypeStruct((SSM_CHUNKS, CHUNK_IN, PAIR), F32),
                   jax.ShapeDtypeStruct((SSM_CHUNKS, CHUNK_TILES, PAIR), F32), jax.ShapeDtypeStruct((1, SSM_WIDTH), F32)],
        input_output_aliases={7: 0},
        scratch_shapes=[pltpu.VMEM((SSM_CHUNKS, rows, PAIR), F32), pltpu.VMEM((SSM_CHUNKS, CHUNK_TILES, PAIR), F32),
                        pltpu.VMEM((IN_BLOCKS, rows, LANES), F32), pltpu.VMEM((IN_BLOCKS, rows, LANES), F32)],
        compiler_params=_params(("arbitrary",)),
    )(dy, proj, s, w2, c2, a2, d_skip, dproj)


def _block(ref, axis, size, index):
    idx = [slice(None)] * len(ref.shape)
    idx[axis] = pl.ds(index * size if isinstance(index, int) else pl.multiple_of(index * size, size), size)
    return ref.at[tuple(idx)]


def _all_gather(name, shards, axes):
    n = len(shards)
    sizes = [s.shape[a] for s, a in zip(shards, axes)]

    def body(*refs):
        ins, outs = refs[:n], refs[n:2 * n]
        send_sems, recv_sems, local_sems = refs[2 * n:]
        x, y, c = (lax.axis_index(a) for a in MESH_AXES)
        me, sibling = (x, y, c), (x, y, 1 - c)
        chips = [(1 - x, y), (x, 1 - y), (1 - x, 1 - y)]

        def rows(i, dev):
            return _block(outs[i], axes[i], sizes[i], 4 * dev[0] + 2 * dev[1] + dev[2])

        def copy(i, k, block, to, src=None):
            return pltpu.make_async_remote_copy(
                src_ref=rows(i, block) if src is None else src, dst_ref=rows(i, block),
                send_sem=send_sems.at[7 * i + k], recv_sem=recv_sems.at[7 * i + k],
                device_id=to, device_id_type=MESH)

        mine = [pltpu.make_async_copy(ins[i], rows(i, me), local_sems.at[i]) for i in range(n)]
        for cp in mine:
            cp.start()
        first = []
        for i in range(n):
            first.append(copy(i, 0, me, sibling, src=ins[i]))
            first += [copy(i, 1 + j, me, (*chip, c), src=ins[i]) for j, chip in enumerate(chips)]
        for cp in first:
            cp.start()
        passed = []
        for i in range(n):
            for j, chip in enumerate(chips):
                copy(i, 1 + j, (*chip, c), me).wait_recv()
                fwd = copy(i, 4 + j, (*chip, c), sibling)
                fwd.start()
                passed.append(fwd)
        for i in range(n):
            copy(i, 0, sibling, me).wait_recv()
            for j, chip in enumerate(chips):
                copy(i, 4 + j, (*chip, 1 - c), me).wait_recv()
        for cp in first + passed:
            cp.wait_send()
        for cp in mine:
            cp.wait()

    out_shape = []
    for s, a in zip(shards, axes):
        shape = list(s.shape)
        shape[a] *= N_DEV
        out_shape.append(jax.ShapeDtypeStruct(tuple(shape), s.dtype))
    any_spec = pl.BlockSpec(memory_space=pl.ANY)
    return pl.pallas_call(
        body, name=name, out_shape=out_shape,
        in_specs=[any_spec] * n, out_specs=[any_spec] * n,
        scratch_shapes=[pltpu.SemaphoreType.DMA((7 * n,)), pltpu.SemaphoreType.DMA((7 * n,)),
                        pltpu.SemaphoreType.DMA((n,))],
    )(*shards)


HBM_SPEC = pl.BlockSpec(memory_space=pltpu.HBM)
SEM_SPEC = pl.BlockSpec(memory_space=pltpu.SEMAPHORE)
ANY_SPEC = pl.BlockSpec(memory_space=pl.ANY)
SPLIT_PARAMS = pltpu.CompilerParams(has_side_effects=pltpu.SideEffectType.DATAFLOW_SIDE_EFFECTING)
N_PEERS = N_DEV - 1
TOKEN = jax.ShapeDtypeStruct((SUBLANES, LANES), F32)
VMEM_SPEC = pl.BlockSpec(memory_space=pltpu.VMEM)


def _in_hbm(arrays):
    return [pltpu.with_memory_space_constraint(a, pltpu.HBM) for a in arrays]


def _peer(m):
    x, y, c = (lax.axis_index(a) for a in MESH_AXES)
    px = 1 - x if m & 4 else x
    py = 1 - y if m & 2 else y
    pc = 1 - c if m & 1 else c
    return (px, py, pc), 4 * px + 2 * py + pc


def _my_index():
    x, y, c = (lax.axis_index(a) for a in MESH_AXES)
    return 4 * x + 2 * y + c


def _gather_copies(shard_refs, full_refs, axes, send_sems, recv_sems):
    copies = []
    for i, (shard, full) in enumerate(zip(shard_refs, full_refs)):
        mine = _block(full, axes[i], shard.shape[axes[i]], _my_index())
        for m in range(1, N_DEV):
            peer, _ = _peer(m)
            copies.append(pltpu.make_async_remote_copy(
                src_ref=shard, dst_ref=mine, send_sem=send_sems.at[N_PEERS * i + m - 1],
                recv_sem=recv_sems.at[N_PEERS * i + m - 1], device_id=peer, device_id_type=MESH))
    return copies


def _gather_start(name, shards, axes, after):
    n = len(shards)

    def body(*refs):
        shard_refs = refs[:n]
        send_sems, recv_sems, local_sems = refs[n + 1:n + 4]
        full_refs = refs[2 * n + 4:3 * n + 4]
        refs[3 * n + 4][...] = jnp.zeros(TOKEN.shape, TOKEN.dtype)
        for i in range(n):
            pltpu.make_async_copy(shard_refs[i], _block(full_refs[i], axes[i], shard_refs[i].shape[axes[i]], _my_index()),
                                  local_sems.at[i]).start()
        for cp in _gather_copies(shard_refs, full_refs, axes, send_sems, recv_sems):
            cp.start()

    fulls = []
    for s, a in zip(shards, axes):
        shape = list(s.shape)
        shape[a] *= N_DEV
        fulls.append(pltpu.HBM(tuple(shape), s.dtype))
    out = pl.pallas_call(
        body, name=name,
        out_shape=(pltpu.SemaphoreType.DMA((N_PEERS * n,)), pltpu.SemaphoreType.DMA((N_PEERS * n,)),
                   pltpu.SemaphoreType.DMA((n,)), *[pltpu.HBM(s.shape, s.dtype) for s in shards], *fulls, TOKEN),
        in_specs=[HBM_SPEC] * n + [ANY_SPEC],
        out_specs=(SEM_SPEC, SEM_SPEC, SEM_SPEC, *[HBM_SPEC] * (2 * n), VMEM_SPEC),
        input_output_aliases={i: 3 + i for i in range(n)},
        compiler_params=SPLIT_PARAMS,
    )(*_in_hbm(shards), after)
    return out[:-1], out[-1]


def _gather_wait(name, started, indices, axes, after):
    send_sems, recv_sems, local_sems = started[:3]
    n_all = (len(started) - 3) // 2
    shards = [started[3 + i] for i in indices]
    fulls = [started[3 + n_all + i] for i in indices]
    n = len(indices)

    def body(*refs):
        shard_refs, full_refs = refs[:n], refs[n:2 * n]
        send_sems, recv_sems, local_sems = refs[2 * n:2 * n + 3]
        for j, i in enumerate(indices):
            mine = _block(full_refs[j], axes[j], shard_refs[j].shape[axes[j]], _my_index())
            pltpu.make_async_copy(shard_refs[j], mine, local_sems.at[i]).wait()
            for m in range(1, N_DEV):
                peer, _ = _peer(m)
                cp = pltpu.make_async_remote_copy(
                    src_ref=shard_refs[j], dst_ref=mine, send_sem=send_sems.at[N_PEERS * i + m - 1],
                    recv_sem=recv_sems.at[N_PEERS * i + m - 1], device_id=peer, device_id_type=MESH)
                cp.wait_send()
                cp.wait_recv()

    out = pl.pallas_call(
        body, name=name,
        out_shape=tuple(pltpu.HBM(a.shape, a.dtype) for a in shards + fulls),
        in_specs=[HBM_SPEC] * (2 * n) + [SEM_SPEC] * 3 + [ANY_SPEC], out_specs=tuple([HBM_SPEC] * (2 * n)),
        input_output_aliases={i: i for i in range(2 * n)},
        compiler_params=SPLIT_PARAMS,
    )(*shards, *fulls, send_sems, recv_sems, local_sems, after)
    return out[n:]


def _exchange_start(name, fulls, axes):
    n = len(fulls)
    sizes = [f.shape[a] // N_DEV for f, a in zip(fulls, axes)]

    def body(*refs):
        ins = refs[:n]
        send_sems, recv_sems = refs[n:n + 2]
        lands = refs[2 * n + 2:3 * n + 2]
        refs[3 * n + 2][...] = jnp.zeros(TOKEN.shape, TOKEN.dtype)
        for i in range(n):
            for m in range(1, N_DEV):
                peer, index = _peer(m)
                pltpu.make_async_remote_copy(
                    src_ref=_block(ins[i], axes[i], sizes[i], index), dst_ref=lands[i].at[m - 1],
                    send_sem=send_sems.at[N_PEERS * i + m - 1], recv_sem=recv_sems.at[N_PEERS * i + m - 1],
                    device_id=peer, device_id_type=MESH).start()

    lands = []
    for f, a, size in zip(fulls, axes, sizes):
        shape = list(f.shape)
        shape[a] = size
        lands.append(pltpu.HBM((N_PEERS, *shape), f.dtype))
    out = pl.pallas_call(
        body, name=name,
        out_shape=(pltpu.SemaphoreType.DMA((N_PEERS * n,)), pltpu.SemaphoreType.DMA((N_PEERS * n,)),
                   *[pltpu.HBM(f.shape, f.dtype) for f in fulls], *lands, TOKEN),
        in_specs=[HBM_SPEC] * n, out_specs=(SEM_SPEC, SEM_SPEC, *[HBM_SPEC] * (2 * n), VMEM_SPEC),
        input_output_aliases={i: 2 + i for i in range(n)},
        compiler_params=SPLIT_PARAMS,
    )(*_in_hbm(fulls))
    return out[:-1], out[-1]


def _exchange_wait(name, started, axes, after):
    send_sems, recv_sems = started[:2]
    n = (len(started) - 2) // 2
    fulls, lands = list(started[2:2 + n]), list(started[2 + n:])
    sizes = [f.shape[a] // N_DEV for f, a in zip(fulls, axes)]

    def body(*refs):
        ins, land_refs = refs[:n], refs[n:2 * n]
        send_sems, recv_sems = refs[2 * n:2 * n + 2]
        for i in range(n):
            for m in range(1, N_DEV):
                peer, index = _peer(m)
                cp = pltpu.make_async_remote_copy(
                    src_ref=_block(ins[i], axes[i], sizes[i], index), dst_ref=land_refs[i].at[m - 1],
                    send_sem=send_sems.at[N_PEERS * i + m - 1], recv_sem=recv_sems.at[N_PEERS * i + m - 1],
                    device_id=peer, device_id_type=MESH)
                cp.wait_send()
                cp.wait_recv()

    out = pl.pallas_call(
        body, name=name,
        out_shape=tuple(pltpu.HBM(a.shape, a.dtype) for a in fulls + lands),
        in_specs=[HBM_SPEC] * (2 * n) + [SEM_SPEC] * 2 + [ANY_SPEC], out_specs=tuple([HBM_SPEC] * (2 * n)),
        input_output_aliases={i: i for i in range(2 * n)},
        compiler_params=SPLIT_PARAMS,
    )(*fulls, *lands, send_sems, recv_sems, after)
    return out[:n], out[n:]


def _owner_copies(fulls, lands, axes, owners, send_sems, recv_sems):
    me = _my_index()
    pairs = []
    for i, (full, land) in enumerate(zip(fulls, lands)):
        size = full.shape[axes[i]] // len(owners)
        for j, owner in enumerate(owners):
            slot = jnp.maximum(jnp.bitwise_xor(me, owner) - 1, 0)
            pairs.append((pltpu.make_async_remote_copy(
                src_ref=_block(full, axes[i], size, j), dst_ref=land.at[slot],
                send_sem=send_sems.at[len(owners) * i + j], recv_sem=recv_sems.at[N_PEERS * i + slot],
                device_id=(owner // 4, owner // 2 % 2, owner % 2), device_id_type=MESH), me != owner))
    return pairs


def _owners_start(name, fulls, axes, owners):
    n = len(fulls)
    sizes = [f.shape[a] // len(owners) for f, a in zip(fulls, axes)]

    def body(*refs):
        ins = refs[:n]
        send_sems, recv_sems = refs[n:n + 2]
        lands = refs[2 * n + 2:3 * n + 2]
        refs[3 * n + 2][...] = jnp.zeros(TOKEN.shape, TOKEN.dtype)
        for cp, sent in _owner_copies(ins, lands, axes, owners, send_sems, recv_sems):
            pl.when(sent)(cp.start)

    lands = []
    for f, a, size in zip(fulls, axes, sizes):
        shape = list(f.shape)
        shape[a] = size
        lands.append(pltpu.HBM((N_PEERS, *shape), f.dtype))
    out = pl.pallas_call(
        body, name=name,
        out_shape=(pltpu.SemaphoreType.DMA((len(owners) * n,)), pltpu.SemaphoreType.DMA((N_PEERS * n,)),
                   *[pltpu.HBM(f.shape, f.dtype) for f in fulls], *lands, TOKEN),
        in_specs=[HBM_SPEC] * n, out_specs=(SEM_SPEC, SEM_SPEC, *[HBM_SPEC] * (2 * n), VMEM_SPEC),
        input_output_aliases={i: 2 + i for i in range(n)},
        compiler_params=SPLIT_PARAMS,
    )(*_in_hbm(fulls))
    return out[:-1], out[-1]


def _owners_wait(name, started, axes, owners, after):
    send_sems, recv_sems = started[:2]
    n = (len(started) - 2) // 2
    fulls, lands = list(started[2:2 + n]), list(started[2 + n:])

    def body(*refs):
        ins, land_refs = refs[:n], refs[n:2 * n]
        send_sems, recv_sems = refs[2 * n:2 * n + 2]
        me = _my_index()
        for cp, sent in _owner_copies(ins, land_refs, axes, owners, send_sems, recv_sems):
            pl.when(sent)(cp.wait_send)
        is_owner = functools.reduce(jnp.logical_or, [me == owner for owner in owners])

        @pl.when(is_owner)
        def _():
            for i in range(n):
                size = ins[i].shape[axes[i]] // len(owners)
                for m in range(1, N_DEV):
                    pltpu.make_async_remote_copy(
                        src_ref=_block(ins[i], axes[i], size, 0), dst_ref=land_refs[i].at[m - 1],
                        send_sem=send_sems.at[0], recv_sem=recv_sems.at[N_PEERS * i + m - 1],
                        device_id=(0, 0, 0), device_id_type=MESH).wait_recv()

    out = pl.pallas_call(
        body, name=name,
        out_shape=tuple(pltpu.HBM(a.shape, a.dtype) for a in fulls + lands),
        in_specs=[HBM_SPEC] * (2 * n) + [SEM_SPEC] * 2 + [ANY_SPEC], out_specs=tuple([HBM_SPEC] * (2 * n)),
        input_output_aliases={i: i for i in range(2 * n)},
        compiler_params=SPLIT_PARAMS,
    )(*fulls, *lands, send_sems, recv_sems, after)
    return out[:n], out[n:]


def _sum_parts(part_refs, ndim):
    g = None
    for p_ref in part_refs:
        stacked = len(p_ref.shape) > ndim
        terms = [p_ref[s] for s in range(p_ref.shape[0])] if stacked else [p_ref[...]]
        for term in terms:
            term = term.astype(F32)
            g = term if g is None else g + term
    return g


def _adamw_update(w_ref, m_ref, v_ref, g, g_ref, d_ref, nm_ref, nv_ref):
    c1 = 1.0 - ADAM_B1 ** ADAM_STEP
    c2 = 1.0 - ADAM_B2 ** ADAM_STEP
    new_m = ADAM_B1 * m_ref[...] + (1.0 - ADAM_B1) * g
    new_v = ADAM_B2 * v_ref[...] + (1.0 - ADAM_B2) * (g * g)
    g_ref[...] = g
    nm_ref[...] = new_m
    nv_ref[...] = new_v
    d_ref[...] = -ADAM_LR * ((new_m / c1) / (jnp.sqrt(new_v / c2) + ADAM_EPS) + ADAM_WD * w_ref[...])


def _adamw_small(ws, ms, vs, stacks, loss_stack):
    n = len(ws)

    def body(*refs):
        ins, outs = refs[:4 * n + 1], refs[4 * n + 1:]
        for i in range(n):
            _adamw_update(ins[i], ins[n + i], ins[2 * n + i], _sum_parts([ins[3 * n + i]], len(ins[i].shape)),
                          outs[i], outs[n + i], outs[2 * n + i], outs[3 * n + i])
        total = ins[4 * n][0]
        for dev in range(1, N_DEV):
            total = total + ins[4 * n][dev]
        outs[4 * n][...] = total

    res = pl.pallas_call(
        body, name="adamw_small",
        out_shape=[jax.ShapeDtypeStruct(w.shape, F32) for w in ws] * 4 + [jax.ShapeDtypeStruct((1, LANES), F32)],
        compiler_params=_params(None),
    )(*ws, *ms, *vs, *stacks, loss_stack)
    return res[:n], res[n:2 * n], res[2 * n:3 * n], res[3 * n:4 * n], res[4 * n]


def _adamw(name, w, m, v, parts, other_parts=(), use_other=None):
    r, c = w.shape
    tr = _tile(r, 256)
    n_parts, n_other = len(parts), len(other_parts)

    def body(*refs):
        g = _sum_parts(refs[3:3 + n_parts], 2)
        if n_other:
            g = jnp.where(use_other(), _sum_parts(refs[3 + n_parts:3 + n_parts + n_other], 2), g)
        _adamw_update(refs[0], refs[1], refs[2], g, *refs[3 + n_parts + n_other:])

    row = pl.BlockSpec((tr, c), lambda i: (i, 0))
    in_specs = [row, row, row]
    for p in (*parts, *other_parts):
        in_specs.append(row if p.ndim == 2 else pl.BlockSpec((p.shape[0], tr, c), lambda i: (0, i, 0)))
    return pl.pallas_call(
        body, name=name, grid=(r // tr,), in_specs=in_specs, out_specs=[row] * 4,
        out_shape=[jax.ShapeDtypeStruct((r, c), F32)] * 4,
        compiler_params=_params(("arbitrary",)),
    )(w, m, v, *parts, *other_parts)


SMALL = ("norm_gain", "pool_scale", "a_re", "a_im", "log_dt", "b_re", "b_im", "c_re", "c_im", "d_skip", "final_gain")
LARGE = ("w_in", "w_pool", "w_glu", "w_out", "w_ple", "w_ple_gate")
LARGE_AXIS = {"w_in": 1, "w_pool": 1, "w_glu": 1, "w_out": 0, "w_ple": 1, "w_ple_gate": 0}
GATE_OWNERS = (2, 3, 6, 7)
INPUT_OWNERS = (0, 1, 4, 5)
PART_WIDTH = POOL_WIDTH
WEIGHTS = ("norm_gain", "w_in", "w_pool", "pool_scale", "a_re", "a_im", "log_dt", "b_re", "b_im", "c_re", "c_im",
           "d_skip", "w_glu", "w_out", "w_ple", "w_ple_gate", "final_gain")


def kernel(x, p, norm_gain, w_in, w_pool, pool_scale, a_re, a_im, log_dt, b_re, b_im, c_re, c_im, d_skip, w_glu, w_out, w_ple, w_ple_gate, final_gain, loss_target, m_norm_gain, m_w_in, m_w_pool, m_pool_scale, m_a_re, m_a_im, m_log_dt, m_b_re, m_b_im, m_c_re, m_c_im, m_d_skip, m_w_glu, m_w_out, m_w_ple, m_w_ple_gate, m_final_gain, v_norm_gain, v_w_in, v_w_pool, v_pool_scale, v_a_re, v_a_im, v_log_dt, v_b_re, v_b_im, v_c_re, v_c_im, v_d_skip, v_w_glu, v_w_out, v_w_ple, v_w_ple_gate, v_final_gain):
    weights = dict(norm_gain=norm_gain, w_in=w_in, w_pool=w_pool, pool_scale=pool_scale, a_re=a_re, a_im=a_im,
                   log_dt=log_dt, b_re=b_re, b_im=b_im, c_re=c_re, c_im=c_im, d_skip=d_skip, w_glu=w_glu,
                   w_out=w_out, w_ple=w_ple, w_ple_gate=w_ple_gate, final_gain=final_gain)
    mom_m = dict(norm_gain=m_norm_gain, w_in=m_w_in, w_pool=m_w_pool, pool_scale=m_pool_scale, a_re=m_a_re,
                 a_im=m_a_im, log_dt=m_log_dt, b_re=m_b_re, b_im=m_b_im, c_re=m_c_re, c_im=m_c_im,
                 d_skip=m_d_skip, w_glu=m_w_glu, w_out=m_w_out, w_ple=m_w_ple, w_ple_gate=m_w_ple_gate,
                 final_gain=m_final_gain)
    mom_v = dict(norm_gain=v_norm_gain, w_in=v_w_in, w_pool=v_w_pool, pool_scale=v_pool_scale, a_re=v_a_re,
                 a_im=v_a_im, log_dt=v_log_dt, b_re=v_b_re, b_im=v_b_im, c_re=v_c_re, c_im=v_c_im,
                 d_skip=v_d_skip, w_glu=v_w_glu, w_out=v_w_out, w_ple=v_w_ple, w_ple_gate=v_w_ple_gate,
                 final_gain=v_final_gain)

    t = x.shape[1]
    xs = x.reshape(t, D_MODEL)
    ps = p.reshape(t, PLE_DIM)
    target = loss_target.reshape(t, D_MODEL)
    gain1 = norm_gain.reshape(1, D_MODEL)
    gain_f = final_gain.reshape(1, D_MODEL)
    scale_p = pool_scale.reshape(1, POOL_WIDTH)
    skip = d_skip.reshape(1, SSM_WIDTH)

    shard2d = {k: weights[k][0] for k in LARGE}
    shard_bf = {k: shard2d[k].astype(BF16) for k in LARGE}
    full = {"w_in": _all_gather("w_in_all_gather", [shard_bf["w_in"]], [LARGE_AXIS["w_in"]])[0]}
    later = [k for k in LARGE if k != "w_in"]
    later_axes = [LARGE_AXIS[k] for k in later]
    gather, gather_token = _gather_start("weights_gather_start", [shard_bf[k] for k in later], later_axes,
                                         full["w_in"])

    def arrive(k, after):
        i = later.index(k)
        full[k] = _gather_wait("gather_wait_" + k, gather, [i], [later_axes[i]], after)[0]

    ar, ai = a_re[0], a_im[0]
    ldt = log_dt.reshape(N_SSM_GROUPS, 1)
    br_t = jnp.transpose(b_re[0], (0, 2, 1))
    bi_t = jnp.transpose(b_im[0], (0, 2, 1))
    ab_re, ab_im, bb_re, bb_im = _ssm_params(ar, ai, ldt, br_t, bi_t)
    tiles = (SSM_CHUNKS, CHUNK_TILES, LANES)
    abar = jnp.concatenate([ab_re.reshape(tiles), ab_im.reshape(tiles)], axis=-1)
    w_pair = _compact_pair(bb_re, bb_im)
    c_pair = _compact_pair(c_re[0], -c_im[0])

    hn, proj = _norm1_in_proj(xs, gain1, full["w_in"], gather_token)
    pooled = _pool_fwd(proj)
    tm = _tile(t, 1024)
    arrive("w_pool", pooled)
    mixed = _mm("pool_mix", [(pooled, (tm, POOL_GROUP), lambda i, j, s: (i, j),
                              full["w_pool"], (None, POOL_GROUP, POOL_GROUP), lambda i, j, s: (j, 0, 0))],
                DOT_NN, (t // tm, N_POOL_GROUPS, 1),
                [((t, POOL_WIDTH), F32, (tm, POOL_GROUP), lambda i, j, s: (i, j))], 1)[0]
    y, gel, states = _ssm_fwd(proj, w_pair, c_pair, abar, skip)
    arrive("w_glu", gel)
    hg = _mm_nn("glu_proj", gel, full["w_glu"], [F32])[0]
    arrive("w_out", hg)
    cat, h1, h1b = _gate_out_proj(mixed, proj, hg, scale_p, full["w_out"], xs)
    arrive("w_ple", h1b)
    arrive("w_ple_gate", h1b)
    de, dq, dh2, g_final_gain, loss_part = _ple_final(h1, h1b, ps, full["w_ple_gate"], full["w_ple"], target, gain_f)

    grads = {}
    grads["w_ple_gate"] = _mm_tn("ple_gate_wgrad", h1b, dq, BF16)
    grads["w_ple"] = _mm_tn("ple_wgrad", ps, de, BF16)
    sent, tokens = {}, {}

    def send(names):
        sent[names], tokens[names[0]] = _exchange_start(
            "grads_start_" + names[0], [grads[k] for k in names], [LARGE_AXIS[k] for k in names])

    send(("w_ple_gate", "w_ple"))
    dh1, dh1b = _residual_dgrad("ple_gate_dgrad", dq, full["w_ple_gate"], dh2)
    grads["w_out"] = _mm_tn("out_wgrad", cat, dh1b, BF16)
    send(("w_out",))
    dmixed, dproj, dhg, g_pool_scale = _out_dgrad_gate_bwd(
        dh1b, full["w_out"], mixed, proj, hg, scale_p, [tokens["w_ple_gate"], tokens["w_out"]])
    gate_sent, gate_token = _owners_start(
        "grads_start_w_in_gates", [_mm_tn("in_wgrad_gates", hn, dproj, BF16, b_blocks=(PART_WIDTH, 1, 2))],
        [LARGE_AXIS["w_in"]], GATE_OWNERS)

    tk = _tile(t, 1024)
    grads["w_pool"] = _mm("pool_wgrad", [(pooled, (tk, POOL_GROUP), lambda i, j, s: (s, i),
                                          dmixed, (tk, POOL_GROUP), lambda i, j, s: (s, i))],
                          DOT_TN, (N_POOL_GROUPS, 1, t // tk),
                          [((N_POOL_GROUPS, POOL_GROUP, POOL_GROUP), BF16, (None, POOL_GROUP, POOL_GROUP),
                            lambda i, j, s: (i, 0, 0))], t // tk)[0]
    dpooled = _mm("pool_dgrad", [(dmixed, (tm, POOL_GROUP), lambda i, j, s: (i, j),
                                  full["w_pool"], (None, POOL_GROUP, POOL_GROUP), lambda i, j, s: (j, 0, 0))],
                  DOT_NT, (t // tm, N_POOL_GROUPS, 1),
                  [((t, POOL_WIDTH), F32, (tm, POOL_GROUP), lambda i, j, s: (i, j))], 1)[0]
    dproj = _pool_bwd(dpooled, dproj, gate_token)

    grads["w_glu"] = _mm_tn("glu_wgrad", gel, dhg, BF16)
    send(("w_pool", "w_glu"))

    def gelu_bwd_epilogue(acc, ex, out_refs):
        yv = ex[0][...]
        th = jnp.tanh(GELU_C * (yv + GELU_A * yv * yv * yv))
        dgelu = 0.5 * (1.0 + th) + 0.5 * yv * (1.0 - th * th) * GELU_C * (1.0 + 3.0 * GELU_A * yv * yv)
        out_refs[0][...] = acc * dgelu

    dy = _mm_nt("glu_dgrad", dhg, full["w_glu"], [F32], tk=2048, extras=[y], epilogue=gelu_bwd_epilogue,
                after=[tokens["w_pool"]])[0]
    dproj, g_c_pair, g_w_pair, g_abar, g_d_skip = _ssm_bwd(dy, proj, states, w_pair, c_pair, abar, skip, dproj)

    g_ab_re = g_abar[..., :LANES].reshape(N_SSM_GROUPS, SSM_STATE)
    g_ab_im = g_abar[..., LANES:].reshape(N_SSM_GROUPS, SSM_STATE)
    d_ar, d_ai, d_ldt, d_br_t, d_bi_t = _ssm_params_bwd(
        ar, ai, ldt, br_t, bi_t, g_ab_re, g_ab_im,
        _expand_grad(g_w_pair[..., :LANES]), _expand_grad(g_w_pair[..., LANES:]))

    small_grads = dict(
        pool_scale=g_pool_scale, a_re=d_ar, a_im=d_ai, log_dt=d_ldt.reshape(1, N_SSM_GROUPS),
        b_re=d_br_t.astype(BF16), b_im=d_bi_t.astype(BF16), c_re=_expand_grad(g_c_pair[..., :LANES]).astype(BF16),
        c_im=(-_expand_grad(g_c_pair[..., LANES:])).astype(BF16), d_skip=g_d_skip, final_gain=g_final_gain)
    early = [k for k in SMALL if k != "norm_gain"]
    early_sent, early_token = _gather_start(
        "small_grads_start", [small_grads[k][None] for k in early] + [jnp.broadcast_to(loss_part, (1, 1, LANES))],
        [0] * (len(early) + 1), d_ar)

    input_sent, input_token = _owners_start(
        "grads_start_w_in_inputs",
        [_mm_tn("in_wgrad_inputs", hn, dproj, BF16, after=[early_token], b_blocks=(PART_WIDTH, 0, 2))],
        [LARGE_AXIS["w_in"]], INPUT_OWNERS)
    grad_x, g_norm_gain = _in_dgrad_norm1_bwd(dproj, full["w_in"], xs, dh1, gain1, input_token)
    late_sent, late_token = _gather_start("norm_gain_grad_start", [g_norm_gain[None]], [0], g_norm_gain)

    out_g, out_d, out_m, out_v = ({} for _ in range(4))
    me = 4 * lax.axis_index("x") + 2 * lax.axis_index("y") + lax.axis_index("c")
    after = late_token
    for names, started in sent.items():
        axes = [LARGE_AXIS[k] for k in names]
        partials, landed = _exchange_wait("grads_wait_" + names[0], started, axes, after)
        for k, axis, partial, land in zip(names, axes, partials, landed):
            shard_shape = shard2d[k].shape
            size = shard_shape[axis]
            own = lax.dynamic_slice_in_dim(partial, me * size, size, axis=axis)
            view = (-1, shard_shape[-1])
            rows = math.prod(shard_shape[:-1])
            res = _adamw("adamw_" + k, shard2d[k].reshape(view), mom_m[k][0].reshape(view), mom_v[k][0].reshape(view),
                         [own.reshape(view), land.reshape(N_PEERS, rows, shard_shape[-1])])
            out_g[k], out_d[k], out_m[k], out_v[k] = (r.reshape(weights[k].shape) for r in res)
            after = res[0]

    in_axis = LARGE_AXIS["w_in"]
    width = shard2d["w_in"].shape[in_axis]
    place = (me // 4 * 2 + me % 2) * width
    halves = []
    for tag, started, owners in (("inputs", input_sent, INPUT_OWNERS), ("gates", gate_sent, GATE_OWNERS)):
        partials, landed = _owners_wait("grads_wait_w_in_" + tag, started, [in_axis], owners, after)
        halves.append([lax.dynamic_slice_in_dim(partials[0], place, width, axis=in_axis), landed[0]])
    res = _adamw("adamw_w_in", shard2d["w_in"], mom_m["w_in"][0], mom_v["w_in"][0], halves[0], halves[1],
                 lambda: lax.axis_index("y") == 1)
    out_g["w_in"], out_d["w_in"], out_m["w_in"], out_v["w_in"] = (r.reshape(weights["w_in"].shape) for r in res)
    after = res[0]

    def b_view(a):
        return jnp.transpose(a[0], (0, 2, 1))

    views = dict(norm_gain=lambda a: a, pool_scale=lambda a: a, a_re=lambda a: a[0], a_im=lambda a: a[0],
                 log_dt=lambda a: a, b_re=b_view, b_im=b_view, c_re=lambda a: a[0], c_im=lambda a: a[0],
                 d_skip=lambda a: a, final_gain=lambda a: a.reshape(1, D_MODEL))
    landed = _gather_wait("small_grads_wait", early_sent, list(range(len(early) + 1)), [0] * (len(early) + 1), after)
    stack = dict(zip(early, landed))
    stack["norm_gain"] = _gather_wait("norm_gain_grad_wait", late_sent, [0], [0], after)[0]
    *small_out, loss_row = _adamw_small(
        [views[k](weights[k]) for k in SMALL], [views[k](mom_m[k]) for k in SMALL],
        [views[k](mom_v[k]) for k in SMALL], [stack[k] for k in SMALL], landed[-1])
    loss = loss_row[0, 0]
    for out, res in zip((out_g, out_d, out_m, out_v), small_out):
        for k, r in zip(SMALL, res):
            if k in ("b_re", "b_im"):
                r = jnp.transpose(r, (0, 2, 1))
            out[k] = r.reshape(weights[k].shape)

    return (loss, grad_x.reshape(x.shape), *[out_g[k] for k in WEIGHTS], *[out_d[k] for k in WEIGHTS],
            *[out_m[k] for k in WEIGHTS], *[out_v[k] for k in WEIGHTS])
```

```python
import math

import jax
import jax.numpy as jnp
from jax import lax
from jax.experimental import pallas as pl
from jax.experimental.pallas import tpu as pltpu

F32 = jnp.float32
BF16 = jnp.bfloat16
MESH = pl.DeviceIdType.MESH
MESH_AXES = ("x", "y", "c")
N_DEV = 8

D_MODEL = 2048
POOL_WIDTH = 1024
SSM_WIDTH = 1024
N_POOL_GROUPS = 4
POOL_GROUP = 256
SSM_GROUP = 16
N_SSM_GROUPS = 64
SSM_STATE = 64
SSM_FLAT = N_SSM_GROUPS * SSM_STATE
SSM_CHUNKS = 4
CHUNK_IN = SSM_WIDTH // SSM_CHUNKS
CHUNK_STATE = SSM_FLAT // SSM_CHUNKS
PLE_DIM = 256
EPS = 1e-6
A_RE_MAX = -1e-4
ADAM_LR = 0.001
ADAM_B1 = 0.9
ADAM_B2 = 0.999
ADAM_EPS = 1e-08
ADAM_WD = 0.01
ADAM_STEP = 10
GELU_C = math.sqrt(2.0 / math.pi)
GELU_A = 0.044715

SUBLANES = 8
LANES = 128
VMEM_LIMIT_BYTES = 48 * 1024 * 1024

DOT_NN = (((1,), (0,)), ((), ()))
DOT_NT = (((1,), (1,)), ((), ()))
DOT_TN = (((0,), (0,)), ((), ()))


def _tile(n, pref):
    return pref if n % pref == 0 else n


def _params(sem):
    return pltpu.CompilerParams(dimension_semantics=sem, vmem_limit_bytes=VMEM_LIMIT_BYTES)


def _sigmoid(v):
    return 1.0 / (1.0 + jnp.exp(-v))


def _silu_and_grad(v):
    s = _sigmoid(v)
    return v * s, s * (1.0 + v * (1.0 - s))


def _mm(name, pairs, dims, grid, outs, k_steps, extras=(), epilogue=None):
    n_pairs, n_ex, n_out = len(pairs), len(extras), len(outs)
    acc_shape = tuple(d for d in outs[0][2] if d is not None)
    if epilogue is None:
        def epilogue(acc, ex, out_refs):
            out_refs[0][...] = acc.astype(out_refs[0].dtype)

    def body(*refs):
        ab = refs[:2 * n_pairs]
        ex = refs[2 * n_pairs:2 * n_pairs + n_ex]
        out_refs = refs[2 * n_pairs + n_ex:2 * n_pairs + n_ex + n_out]
        acc = refs[-1]
        k = pl.program_id(2)

        @pl.when(k == 0)
        def _():
            acc[...] = jnp.zeros_like(acc)

        part = None
        for q in range(n_pairs):
            d = lax.dot_general(ab[2 * q][...].astype(BF16), ab[2 * q + 1][...].astype(BF16), dims,
                                preferred_element_type=F32)
            part = d if part is None else part + d
        acc[...] += part

        @pl.when(k == k_steps - 1)
        def _():
            epilogue(acc[...], ex, out_refs)

    in_specs, operands = [], []
    for a, a_blk, a_map, b, b_blk, b_map in pairs:
        in_specs += [pl.BlockSpec(a_blk, a_map), pl.BlockSpec(b_blk, b_map)]
        operands += [a, b]
    for e, e_blk, e_map in extras:
        in_specs.append(pl.BlockSpec(e_blk, e_map))
        operands.append(e)
    return pl.pallas_call(
        body, name=name, grid=grid, in_specs=in_specs,
        out_specs=[pl.BlockSpec(o[2], o[3]) for o in outs],
        out_shape=[jax.ShapeDtypeStruct(o[0], o[1]) for o in outs],
        scratch_shapes=[pltpu.VMEM(acc_shape, F32)],
        compiler_params=_params(("arbitrary", "arbitrary", "arbitrary")),
    )(*operands)


def _after(tokens):
    return [(tok, tok.shape, lambda i, j, s: (0, 0)) for tok in tokens]


def _mm_nn(name, a, b, out_dtypes, tm=1024, tn=1024, tk=1024, a_col0=0, extras=(), epilogue=None, after=()):
    m, n = a.shape[0], b.shape[1]
    k = b.shape[0]
    tm, tn, tk = _tile(m, tm), _tile(n, tn), _tile(k, tk)
    outs = [((m, n), dt, (tm, tn), lambda i, j, s: (i, j)) for dt in out_dtypes]
    ex = [(e, (tm, tn), lambda i, j, s: (i, j)) for e in extras] + _after(after)
    return _mm(name, [(a, (tm, tk), lambda i, j, s: (i, a_col0 + s), b, (tk, tn), lambda i, j, s: (s, j))],
               DOT_NN, (m // tm, n // tn, k // tk), outs, k // tk, ex, epilogue)


def _mm_nt(name, a, b, out_dtypes, tm=1024, tn=1024, tk=1024, extras=(), epilogue=None, after=()):
    m, kk = a.shape
    n = b.shape[0]
    tm, tn, tk = _tile(m, tm), _tile(n, tn), _tile(kk, tk)
    outs = [((m, n), dt, (tm, tn), lambda i, j, s: (i, j)) for dt in out_dtypes]
    ex = [(e, (tm, tn), lambda i, j, s: (i, j)) for e in extras] + _after(after)
    return _mm(name, [(a, (tm, tk), lambda i, j, s: (i, s), b, (tn, tk), lambda i, j, s: (j, s))],
               DOT_NT, (m // tm, n // tn, kk // tk), outs, kk // tk, ex, epilogue)


def _mm_tn(name, a, b, out_dtype, tm=512, tn=2048, tk=1024, after=()):
    m, kk = a.shape
    n = b.shape[1]
    tm, tn, tk = _tile(kk, tm), _tile(n, tn), _tile(m, tk)
    outs = [((kk, n), out_dtype, (tm, tn), lambda i, j, s: (i, j))]
    return _mm(name, [(a, (tk, tm), lambda i, j, s: (s, i), b, (tk, tn), lambda i, j, s: (s, j))],
               DOT_TN, (kk // tm, n // tn, m // tk), outs, m // tk, _after(after))[0]


ROW_TILE = 256


def _norm1_in_proj(x, gain, w_in, after):
    t = x.shape[0]
    tm = _tile(t, ROW_TILE)
    n = w_in.shape[1]

    def body(x_ref, g_ref, w_ref, _, hn_ref, proj_ref):
        xv = x_ref[...]
        r = lax.rsqrt(jnp.mean(xv * xv, axis=-1, keepdims=True) + EPS)
        hn = (xv * r * g_ref[...]).astype(BF16)
        hn_ref[...] = hn
        proj_ref[...] = jnp.dot(hn, w_ref[...], preferred_element_type=F32)

    row = pl.BlockSpec((tm, D_MODEL), lambda i: (i, 0))
    return pl.pallas_call(
        body, name="norm1_in_proj", grid=(t // tm,),
        in_specs=[row, pl.BlockSpec((1, D_MODEL), lambda i: (0, 0)), _resident(w_in.shape),
                  pl.BlockSpec(after.shape, lambda i: (0, 0))],
        out_specs=[row, pl.BlockSpec((tm, n), lambda i: (i, 0))],
        out_shape=[jax.ShapeDtypeStruct((t, D_MODEL), BF16), jax.ShapeDtypeStruct((t, n), F32)],
        compiler_params=_params(("arbitrary",)),
    )(x, gain, w_in, after)


def _in_dgrad_norm1_bwd(dproj, w_in, x, dh1, gain, after):
    t = x.shape[0]
    tm = _tile(t, ROW_TILE)

    def body(dp_ref, w_ref, x_ref, dh1_ref, g_ref, _, dx_ref, gg_ref):
        @pl.when(pl.program_id(0) == 0)
        def _():
            gg_ref[...] = jnp.zeros_like(gg_ref)

        dhn = lax.dot_general(dp_ref[...], w_ref[...], DOT_NT, preferred_element_type=F32)
        xv = x_ref[...]
        r = lax.rsqrt(jnp.mean(xv * xv, axis=-1, keepdims=True) + EPS)
        xh = xv * r
        gg_ref[...] += jnp.sum(dhn * xh, axis=0, keepdims=True)
        dxh = dhn * g_ref[...]
        dx_ref[...] = dh1_ref[...] + r * (dxh - xh * jnp.mean(dxh * xh, axis=-1, keepdims=True))

    row = pl.BlockSpec((tm, D_MODEL), lambda i: (i, 0))
    vec = pl.BlockSpec((1, D_MODEL), lambda i: (0, 0))
    return pl.pallas_call(
        body, name="in_dgrad_norm1_bwd", grid=(t // tm,),
        in_specs=[pl.BlockSpec((tm, dproj.shape[1]), lambda i: (i, 0)), _resident(w_in.shape), row, row, vec,
                  pl.BlockSpec(after.shape, lambda i: (0, 0))],
        out_specs=[row, vec],
        out_shape=[jax.ShapeDtypeStruct((t, D_MODEL), F32), jax.ShapeDtypeStruct((1, D_MODEL), F32)],
        compiler_params=_params(("arbitrary",)),
    )(dproj, w_in, x, dh1, gain, after)


def _pool_counts(t, width, group):
    row = lax.broadcasted_iota(jnp.int32, (t, width), 0)
    window = jnp.left_shift(jnp.int32(2), group)
    return row, jnp.minimum(row + 1, window).astype(F32)


def _select_window(group, s2, s4, s8, s16):
    return jnp.where(group == 0, s2, jnp.where(group == 1, s4, jnp.where(group == 2, s8, s16)))


def _pool_mix(proj, w_pool):
    t = proj.shape[0]

    def body(u_ref, w_ref, pooled_ref, mixed_ref):
        group = pl.program_id(0)
        row, count = _pool_counts(t, LANES, group)

        def down(a, j):
            return jnp.where(row >= j, pltpu.roll(a, j, 0), 0.0)

        for h in range(POOL_GROUP // LANES):
            cols = slice(h * LANES, (h + 1) * LANES)
            v = u_ref[:, cols]
            s2 = v + down(v, 1)
            s4 = s2 + down(s2, 2)
            s8 = s4 + down(s4, 4)
            s16 = s8 + down(s8, 8)
            pooled_ref[:, cols] = (_select_window(group, s2, s4, s8, s16) / count - v).astype(BF16)
        mixed_ref[...] = jnp.dot(pooled_ref[...], w_ref[...], preferred_element_type=F32)

    block = pl.BlockSpec((t, POOL_GROUP), lambda g: (0, g))
    return pl.pallas_call(
        body, name="pool_mix", grid=(N_POOL_GROUPS,),
        in_specs=[block, pl.BlockSpec((None, POOL_GROUP, POOL_GROUP), lambda g: (g, 0, 0))],
        out_specs=[block, block],
        out_shape=[jax.ShapeDtypeStruct((t, POOL_WIDTH), BF16), jax.ShapeDtypeStruct((t, POOL_WIDTH), F32)],
        compiler_params=_params(("arbitrary",)),
    )(proj, w_pool)


def _pool_mix_bwd(dmixed, w_pool, dproj):
    t = dmixed.shape[0]

    def body(dm_ref, w_ref, _, o_ref):
        group = pl.program_id(0)
        row, count = _pool_counts(t, LANES, group)

        def up(a, j):
            return jnp.where(row < t - j, pltpu.roll(a, t - j, 0), 0.0)

        dpooled = lax.dot_general(dm_ref[...], w_ref[...], DOT_NT, preferred_element_type=F32)
        for h in range(POOL_GROUP // LANES):
            cols = slice(h * LANES, (h + 1) * LANES)
            dp = dpooled[:, cols]
            r = dp / count
            s2 = r + up(r, 1)
            s4 = s2 + up(s2, 2)
            s8 = s4 + up(s4, 4)
            s16 = s8 + up(s8, 8)
            o_ref[:, cols] = (_select_window(group, s2, s4, s8, s16) - dp).astype(BF16)

    block = pl.BlockSpec((t, POOL_GROUP), lambda g: (0, g))
    return pl.pallas_call(
        body, name="pool_mix_bwd", grid=(N_POOL_GROUPS,),
        in_specs=[block, pl.BlockSpec((None, POOL_GROUP, POOL_GROUP), lambda g: (g, 0, 0)),
                  pl.BlockSpec(memory_space=pl.ANY)],
        out_specs=block,
        out_shape=jax.ShapeDtypeStruct(dproj.shape, dproj.dtype),
        input_output_aliases={2: 0},
        compiler_params=_params(("arbitrary",)),
    )(dmixed, w_pool, dproj)


def _gate_out_proj(mixed, proj, hg, pool_scale, w_out, x):
    t = mixed.shape[0]
    tm = _tile(t, ROW_TILE)

    def body(mx_ref, ga_ref, gb_ref, hg_ref, ps_ref, w_ref, x_ref, cat_ref, h1_ref, h1b_ref):
        silu_a, _ = _silu_and_grad(ga_ref[...])
        cat_ref[:, :POOL_WIDTH] = (mx_ref[...] * ps_ref[...] * silu_a).astype(BF16)
        silu_b, _ = _silu_and_grad(gb_ref[...])
        sb = hg_ref[:, :SSM_WIDTH] * _sigmoid(hg_ref[:, SSM_WIDTH:])
        cat_ref[:, POOL_WIDTH:] = (sb * silu_b).astype(BF16)
        h1 = x_ref[...] + jnp.dot(cat_ref[...], w_ref[...], preferred_element_type=F32)
        h1_ref[...] = h1
        h1b_ref[...] = h1.astype(BF16)

    row = pl.BlockSpec((tm, D_MODEL), lambda i: (i, 0))
    return pl.pallas_call(
        body, name="gate_out_proj", grid=(t // tm,),
        in_specs=[pl.BlockSpec((tm, POOL_WIDTH), lambda i: (i, 0)),
                  pl.BlockSpec((tm, POOL_WIDTH), lambda i: (i, 1)),
                  pl.BlockSpec((tm, SSM_WIDTH), lambda i: (i, 3)),
                  pl.BlockSpec((tm, 2 * SSM_WIDTH), lambda i: (i, 0)),
                  pl.BlockSpec((1, POOL_WIDTH), lambda i: (0, 0)), _resident(w_out.shape), row],
        out_specs=[row, row, row],
        out_shape=[jax.ShapeDtypeStruct((t, D_MODEL), BF16), jax.ShapeDtypeStruct((t, D_MODEL), F32),
                   jax.ShapeDtypeStruct((t, D_MODEL), BF16)],
        compiler_params=_params(("arbitrary",)),
    )(mixed, proj, proj, hg, pool_scale, w_out, x)


def _residual_dgrad(name, dy, w, residual):
    t = dy.shape[0]
    tm = _tile(t, ROW_TILE)
    n = w.shape[0]

    def body(dy_ref, w_ref, r_ref, o_ref, ob_ref):
        o = r_ref[...] + lax.dot_general(dy_ref[...], w_ref[...], DOT_NT, preferred_element_type=F32)
        o_ref[...] = o
        ob_ref[...] = o.astype(BF16)

    out = pl.BlockSpec((tm, n), lambda i: (i, 0))
    return pl.pallas_call(
        body, name=name, grid=(t // tm,),
        in_specs=[pl.BlockSpec((tm, dy.shape[1]), lambda i: (i, 0)), _resident(w.shape), out],
        out_specs=[out, out],
        out_shape=[jax.ShapeDtypeStruct((t, n), F32), jax.ShapeDtypeStruct((t, n), BF16)],
        compiler_params=_params(("arbitrary",)),
    )(dy, w, residual)


def _out_dgrad_gate_bwd(dh1b, w_out, mixed, proj, hg, pool_scale, after):
    t = mixed.shape[0]
    tm = _tile(t, ROW_TILE)
    n_after = len(after)

    def body(dh_ref, w_ref, mx_ref, ga_ref, gb_ref, hg_ref, ps_ref, *rest):
        dmx_ref, dp_ref, dhg_ref, gps_ref = rest[n_after:]

        @pl.when(pl.program_id(0) == 0)
        def _():
            gps_ref[...] = jnp.zeros_like(gps_ref)

        dcat = lax.dot_general(dh_ref[...], w_ref[...], DOT_NT, preferred_element_type=F32)
        ps = ps_ref[...]
        mx = mx_ref[...]
        dya = dcat[:, :POOL_WIDTH]
        silu_a, dsilu_a = _silu_and_grad(ga_ref[...])
        dpa = dya * silu_a
        gps_ref[...] += jnp.sum(dpa * mx, axis=0, keepdims=True)
        dmx_ref[...] = (dpa * ps).astype(BF16)
        dp_ref[:, :POOL_WIDTH] = jnp.zeros((tm, POOL_WIDTH), BF16)
        dp_ref[:, POOL_WIDTH:2 * POOL_WIDTH] = (dya * mx * ps * dsilu_a).astype(BF16)

        dyb = dcat[:, POOL_WIDTH:]
        silu_b, dsilu_b = _silu_and_grad(gb_ref[...])
        h_a = hg_ref[:, :SSM_WIDTH]
        sg = _sigmoid(hg_ref[:, SSM_WIDTH:])
        dsb = dyb * silu_b
        dp_ref[:, 2 * POOL_WIDTH:2 * POOL_WIDTH + SSM_WIDTH] = jnp.zeros((tm, SSM_WIDTH), BF16)
        dp_ref[:, 2 * POOL_WIDTH + SSM_WIDTH:] = (dyb * h_a * sg * dsilu_b).astype(BF16)
        dhg_ref[:, :SSM_WIDTH] = (dsb * sg).astype(BF16)
        dhg_ref[:, SSM_WIDTH:] = (dsb * h_a * sg * (1.0 - sg)).astype(BF16)

    half = pl.BlockSpec((tm, POOL_WIDTH), lambda i: (i, 0))
    full = pl.BlockSpec((tm, D_MODEL), lambda i: (i, 0))
    vec = pl.BlockSpec((1, POOL_WIDTH), lambda i: (0, 0))
    proj_width = 2 * POOL_WIDTH + 2 * SSM_WIDTH
    return pl.pallas_call(
        body, name="out_dgrad_gate_bwd", grid=(t // tm,),
        in_specs=[full, _resident(w_out.shape), half,
                  pl.BlockSpec((tm, POOL_WIDTH), lambda i: (i, 1)),
                  pl.BlockSpec((tm, SSM_WIDTH), lambda i: (i, 3)),
                  full, vec] + [pl.BlockSpec(tok.shape, lambda i: (0, 0)) for tok in after],
        out_specs=[half, pl.BlockSpec((tm, proj_width), lambda i: (i, 0)), full, vec],
        out_shape=[jax.ShapeDtypeStruct((t, POOL_WIDTH), BF16), jax.ShapeDtypeStruct((t, proj_width), BF16),
                   jax.ShapeDtypeStruct((t, 2 * SSM_WIDTH), BF16),
                   jax.ShapeDtypeStruct((1, POOL_WIDTH), F32)],
        compiler_params=_params(("arbitrary",)),
    )(dh1b, w_out, mixed, proj, proj, hg, pool_scale, *after)


def _ple_final(h1, h1b, p, w_gate, w_ple, target, gain):
    t = h1.shape[0]
    tm = _tile(t, 256)

    def body(h1_ref, h1b_ref, p_ref, wg_ref, wp_ref, tg_ref, g_ref, de_ref, dq_ref, dh2_ref, gg_ref, loss_ref):
        @pl.when(pl.program_id(0) == 0)
        def _():
            gg_ref[...] = jnp.zeros_like(gg_ref)
            loss_ref[...] = jnp.zeros_like(loss_ref)

        ev = jnp.dot(p_ref[...].astype(BF16), wp_ref[...], preferred_element_type=F32)
        sg = _sigmoid(jnp.dot(h1b_ref[...], wg_ref[...], preferred_element_type=F32))
        h2 = h1_ref[...] + ev * sg
        r = lax.rsqrt(jnp.mean(h2 * h2, axis=-1, keepdims=True) + EPS)
        n = h2 * r
        gain_v = g_ref[...]
        diff = n * gain_v - tg_ref[...]
        row_loss = jnp.sum(diff * diff, axis=-1, keepdims=True)
        loss_ref[...] += (0.5 / D_MODEL) * jnp.sum(row_loss, axis=0, keepdims=True)
        dout = diff * (1.0 / D_MODEL)
        gg_ref[...] += jnp.sum(dout * n, axis=0, keepdims=True)
        dn = dout * gain_v
        dh2 = r * (dn - n * jnp.mean(dn * n, axis=-1, keepdims=True))
        dh2_ref[...] = dh2
        de_ref[...] = (dh2 * sg).astype(BF16)
        dq_ref[...] = (dh2 * ev * sg * (1.0 - sg)).astype(BF16)

    row = pl.BlockSpec((tm, D_MODEL), lambda i: (i, 0))
    vec = pl.BlockSpec((1, D_MODEL), lambda i: (0, 0))
    return pl.pallas_call(
        body, name="ple_final", grid=(t // tm,),
        in_specs=[row, row, pl.BlockSpec((tm, PLE_DIM), lambda i: (i, 0)), _resident((D_MODEL, D_MODEL)),
                  _resident((PLE_DIM, D_MODEL)), row, vec],
        out_specs=[row, row, row, vec, pl.BlockSpec((1, 1), lambda i: (0, 0))],
        out_shape=[jax.ShapeDtypeStruct((t, D_MODEL), BF16), jax.ShapeDtypeStruct((t, D_MODEL), BF16),
                   jax.ShapeDtypeStruct((t, D_MODEL), F32), jax.ShapeDtypeStruct((1, D_MODEL), F32),
                   jax.ShapeDtypeStruct((1, 1), F32)],
        compiler_params=_params(("arbitrary",)),
    )(h1, h1b, p, w_gate, w_ple, target, gain)


def _zoh(a_re, a_im, log_dt, b_re_t, b_im_t):
    lam_re = jnp.minimum(a_re, A_RE_MAX)
    lam_im = a_im
    dt = jnp.exp(log_dt)
    mag = jnp.exp(lam_re * dt)
    ang = lam_im * dt
    ab_re = mag * jnp.cos(ang)
    ab_im = mag * jnp.sin(ang)
    den = lam_re * lam_re + lam_im * lam_im
    n_re = ab_re - 1.0
    n_im = ab_im
    q_re = (n_re * lam_re + n_im * lam_im) / den
    q_im = (n_im * lam_re - n_re * lam_im) / den
    bb_re = q_re[:, None, :] * b_re_t - q_im[:, None, :] * b_im_t
    bb_im = q_re[:, None, :] * b_im_t + q_im[:, None, :] * b_re_t
    return ab_re, ab_im, bb_re, bb_im


def _ssm_params(a_re, a_im, log_dt, b_re_t, b_im_t):
    def body(are_ref, aim_ref, dt_ref, bre_ref, bim_ref, abre_ref, abim_ref, bbre_ref, bbim_ref):
        ab_re, ab_im, bb_re, bb_im = _zoh(are_ref[...], aim_ref[...], dt_ref[...], bre_ref[...], bim_ref[...])
        abre_ref[...] = ab_re
        abim_ref[...] = ab_im
        bbre_ref[...] = bb_re
        bbim_ref[...] = bb_im

    return pl.pallas_call(
        body, name="ssm_params",
        out_shape=[jax.ShapeDtypeStruct(a_re.shape, F32), jax.ShapeDtypeStruct(a_re.shape, F32),
                   jax.ShapeDtypeStruct(b_re_t.shape, F32), jax.ShapeDtypeStruct(b_re_t.shape, F32)],
        compiler_params=_params(None),
    )(a_re, a_im, log_dt, b_re_t, b_im_t)


def _ssm_params_bwd(a_re, a_im, log_dt, b_re_t, b_im_t, g_ab_re, g_ab_im, g_bb_re, g_bb_im):
    def body(are_ref, aim_ref, dt_ref, bre_ref, bim_ref, gar_ref, gai_ref, gbr_ref, gbi_ref,
             o_are, o_aim, o_dt, o_bre, o_bim):
        _, vjp = jax.vjp(_zoh, are_ref[...], aim_ref[...], dt_ref[...], bre_ref[...], bim_ref[...])
        d_are, d_aim, d_dt, d_bre, d_bim = vjp((gar_ref[...], gai_ref[...], gbr_ref[...], gbi_ref[...]))
        o_are[...] = d_are
        o_aim[...] = d_aim
        o_dt[...] = d_dt
        o_bre[...] = d_bre
        o_bim[...] = d_bim

    ins = (a_re, a_im, log_dt, b_re_t, b_im_t)
    return pl.pallas_call(
        body, name="ssm_params_bwd",
        out_shape=[jax.ShapeDtypeStruct(v.shape, F32) for v in ins],
        compiler_params=_params(None),
    )(*ins, g_ab_re, g_ab_im, g_bb_re, g_bb_im)


CHUNK_TILES = CHUNK_STATE // LANES
CH_PER_TILE = CHUNK_IN // CHUNK_TILES
PAIR = 2 * LANES
SSM_ROWS = 256
SCAN_STEPS = 8
U_COLUMN_BLOCK = 2 * POOL_WIDTH // SSM_WIDTH


def _own_half():
    r = lax.broadcasted_iota(jnp.int32, (CHUNK_IN, LANES), 0) // SSM_GROUP % 2
    c = lax.broadcasted_iota(jnp.int32, (CHUNK_IN, LANES), 1) // SSM_STATE
    return (r == c)[None]


def _compact_weight(w):
    tiled = jnp.tile(w.reshape(SSM_CHUNKS, CHUNK_IN, SSM_STATE), (1, 1, 2))
    return jnp.where(_own_half(), tiled, 0.0)


def _compact_pair(w_a, w_b):
    return jnp.concatenate([_compact_weight(w_a), _compact_weight(w_b)], axis=-1).astype(BF16)


def _expand_grad(g):
    kept = jnp.where(_own_half(), g, 0.0)
    return kept.reshape(SSM_CHUNKS, CHUNK_IN, 2, SSM_STATE).sum(axis=2).reshape(N_SSM_GROUPS, SSM_GROUP, SSM_STATE)


TILES_PER_BLOCK = LANES // CH_PER_TILE
IN_BLOCKS = CHUNK_IN // LANES


def _tile_masks():
    j = lax.broadcasted_iota(jnp.int32, (CHUNK_TILES, LANES), 0) % TILES_PER_BLOCK
    lane = lax.broadcasted_iota(jnp.int32, (CHUNK_TILES, LANES), 1) // CH_PER_TILE
    return (j == lane).astype(F32)


def _tile_rows(ref, j, tt):
    return ref.at[j // TILES_PER_BLOCK, pl.ds(j, tt, stride=CHUNK_TILES), :]


def _spread(ref, v, masks):
    tt = v.shape[0]
    for j in range(CHUNK_TILES):
        block = LANES * (j // TILES_PER_BLOCK)
        _tile_rows(ref, j, tt)[...] = v[:, block:block + LANES] * masks[j:j + 1, :]
    return jnp.concatenate([ref[b] for b in range(IN_BLOCKS)], axis=1).astype(BF16)


def _gather(ref, full, masks):
    tt = full.shape[0] // CHUNK_TILES
    for b in range(IN_BLOCKS):
        ref[b] = full[:, b * LANES:(b + 1) * LANES]
    out = []
    for b in range(IN_BLOCKS):
        acc = None
        for j in range(b * TILES_PER_BLOCK, (b + 1) * TILES_PER_BLOCK):
            part = _tile_rows(ref, j, tt)[...] * masks[j:j + 1, :]
            acc = part if acc is None else acc + part
        out.append(acc)
    return jnp.concatenate(out, axis=1)


def _resident(shape):
    return pl.BlockSpec(shape, lambda i: (0,) * len(shape), pipeline_mode=pl.Buffered(1))


def _halves(ref, k, rows=slice(None)):
    return ref[k, rows, :LANES], ref[k, rows, LANES:]


def _ssm_fwd(proj, w2, c2, a2, d_skip):
    t = proj.shape[0]
    tt = _tile(t, SSM_ROWS)
    rows = tt * CHUNK_TILES

    def body(u_ref, w_ref, c_ref, a_ref, d_ref, y_ref, gel_ref, s_ref, carry, spread_ref, full_ref):
        @pl.when(pl.program_id(0) == 0)
        def _():
            carry[...] = jnp.zeros_like(carry)
            spread_ref[...] = jnp.zeros_like(spread_ref)

        mask = _tile_masks()
        u = u_ref[...]
        for k in range(SSM_CHUNKS):
            uk = _spread(spread_ref, u[:, k * CHUNK_IN:(k + 1) * CHUNK_IN], mask)
            s_ref[k] = jnp.dot(uk, w_ref[k], preferred_element_type=F32)

        abar = [_halves(a_ref, k) for k in range(SSM_CHUNKS)]

        def steps(i, state):
            for v in range(SCAN_STEPS):
                r = pl.ds(pl.multiple_of((i * SCAN_STEPS + v) * CHUNK_TILES, CHUNK_TILES), CHUNK_TILES)
                new = []
                for k, ((a_re, a_im), (s_re, s_im)) in enumerate(zip(abar, state)):
                    b_re, b_im = _halves(s_ref, k, r)
                    s_re, s_im = a_re * s_re - a_im * s_im + b_re, a_re * s_im + a_im * s_re + b_im
                    s_ref[k, r, :LANES] = s_re
                    s_ref[k, r, LANES:] = s_im
                    new.append((s_re, s_im))
                state = tuple(new)
            return state

        state = lax.fori_loop(0, tt // SCAN_STEPS, steps, tuple(_halves(carry, k) for k in range(SSM_CHUNKS)))
        for k, (s_re, s_im) in enumerate(state):
            carry[k, :, :LANES] = s_re
            carry[k, :, LANES:] = s_im

        for k in range(SSM_CHUNKS):
            cols = slice(k * CHUNK_IN, (k + 1) * CHUNK_IN)
            full = lax.dot_general(s_ref[k].astype(BF16), c_ref[k], DOT_NT, preferred_element_type=F32)
            y = _gather(full_ref, full, mask) + d_ref[:, cols] * u[:, cols]
            y_ref[:, cols] = y
            gel_ref[:, cols] = (0.5 * y * (1.0 + jnp.tanh(GELU_C * (y + GELU_A * y * y * y)))).astype(BF16)

    weight = _resident((SSM_CHUNKS, CHUNK_IN, PAIR))
    tokens = pl.BlockSpec((tt, SSM_WIDTH), lambda i: (i, 0))
    return pl.pallas_call(
        body, name="ssm_fwd", grid=(t // tt,),
        in_specs=[pl.BlockSpec((tt, SSM_WIDTH), lambda i: (i, U_COLUMN_BLOCK)), weight, weight,
                  _resident((SSM_CHUNKS, CHUNK_TILES, PAIR)), _resident((1, SSM_WIDTH))],
        out_specs=[tokens, tokens, pl.BlockSpec((SSM_CHUNKS, rows, PAIR), lambda i: (0, i, 0))],
        out_shape=[jax.ShapeDtypeStruct((t, SSM_WIDTH), F32), jax.ShapeDtypeStruct((t, SSM_WIDTH), BF16),
                   jax.ShapeDtypeStruct((SSM_CHUNKS, t * CHUNK_TILES, PAIR), F32)],
        scratch_shapes=[pltpu.VMEM((SSM_CHUNKS, CHUNK_TILES, PAIR), F32), pltpu.VMEM((IN_BLOCKS, rows, LANES), F32),
                        pltpu.VMEM((IN_BLOCKS, rows, LANES), F32)],
        compiler_params=_params(("arbitrary",)),
    )(proj, w2, c2, a2, d_skip)


def _ssm_bwd(dy, proj, s, w2, c2, a2, d_skip, dproj):
    t = dy.shape[0]
    tt = _tile(t, SSM_ROWS)
    rows = tt * CHUNK_TILES
    n_chunks = t // tt

    def body(dy_ref, u_ref, s_ref, w_ref, c_ref, a_ref, d_ref, _, du_ref, gc_ref, gw_ref, ga_ref, gd_ref, z_ref, carry,
             spread_ref, full_ref):
        @pl.when(pl.program_id(0) == 0)
        def _():
            for r in (carry, gc_ref, gw_ref, ga_ref, gd_ref, spread_ref):
                r[...] = jnp.zeros_like(r)

        mask = _tile_masks()
        dy_v = dy_ref[...]
        u = u_ref[...]
        gd_ref[...] += jnp.sum(dy_v * u, axis=0, keepdims=True)
        for k in range(SSM_CHUNKS):
            dk = _spread(spread_ref, dy_v[:, k * CHUNK_IN:(k + 1) * CHUNK_IN], mask)
            z_ref[k] = jnp.dot(dk, c_ref[k], preferred_element_type=F32)
            gc_ref[k] += lax.dot_general(dk, s_ref[k].astype(BF16), DOT_TN, preferred_element_type=F32)

        abar = [_halves(a_ref, k) for k in range(SSM_CHUNKS)]

        def steps(i, state):
            zs, gs = state
            for v in range(SCAN_STEPS):
                tok = tt - 1 - (i * SCAN_STEPS + v)
                r = pl.ds(pl.multiple_of(tok * CHUNK_TILES, CHUNK_TILES), CHUNK_TILES)
                new_z, new_g = [], []
                for k, ((a_re, a_im), (z_re, z_im), (g_re, g_im)) in enumerate(zip(abar, zs, gs)):
                    s_re, s_im = _halves(s_ref, k, r)
                    g_re = g_re + z_re * s_re + z_im * s_im
                    g_im = g_im + z_im * s_re - z_re * s_im
                    d_re, d_im = _halves(z_ref, k, r)
                    z_re, z_im = d_re + a_re * z_re + a_im * z_im, d_im + a_re * z_im - a_im * z_re
                    z_ref[k, r, :LANES] = z_re
                    z_ref[k, r, LANES:] = z_im
                    new_z.append((z_re, z_im))
                    new_g.append((g_re, g_im))
                zs, gs = tuple(new_z), tuple(new_g)
            return zs, gs

        zs, gs = lax.fori_loop(0, tt // SCAN_STEPS, steps,
                               (tuple(_halves(carry, k) for k in range(SSM_CHUNKS)),
                                tuple(_halves(ga_ref, k) for k in range(SSM_CHUNKS))))
        for k in range(SSM_CHUNKS):
            carry[k, :, :LANES], carry[k, :, LANES:] = zs[k]
            ga_ref[k, :, :LANES], ga_ref[k, :, LANES:] = gs[k]

        for k in range(SSM_CHUNKS):
            cols = slice(k * CHUNK_IN, (k + 1) * CHUNK_IN)
            zb = z_ref[k].astype(BF16)
            full = lax.dot_general(zb, w_ref[k], DOT_NT, preferred_element_type=F32)
            du_ref[:, cols] = (_gather(full_ref, full, mask) + d_ref[:, cols] * dy_v[:, cols]).astype(BF16)
            uk = _spread(spread_ref, u[:, cols], mask)
            gw_ref[k] += lax.dot_general(uk, zb, DOT_TN, preferred_element_type=F32)

    weight = _resident((SSM_CHUNKS, CHUNK_IN, PAIR))
    tokens = pl.BlockSpec((tt, SSM_WIDTH), lambda i: (n_chunks - 1 - i, 0))
    grad = pl.BlockSpec((SSM_CHUNKS, CHUNK_IN, PAIR), lambda i: (0, 0, 0))
    return pl.pallas_call(
        body, name="ssm_bwd", grid=(n_chunks,),
        in_specs=[tokens, pl.BlockSpec((tt, SSM_WIDTH), lambda i: (n_chunks - 1 - i, U_COLUMN_BLOCK)),
                  pl.BlockSpec((SSM_CHUNKS, rows, PAIR), lambda i: (0, n_chunks - 1 - i, 0)), weight, weight,
                  _resident((SSM_CHUNKS, CHUNK_TILES, PAIR)), _resident((1, SSM_WIDTH)),
                  pl.BlockSpec(memory_space=pl.ANY)],
        out_specs=[pl.BlockSpec((tt, SSM_WIDTH), lambda i: (n_chunks - 1 - i, U_COLUMN_BLOCK)), grad, grad,
                   pl.BlockSpec((SSM_CHUNKS, CHUNK_TILES, PAIR), lambda i: (0, 0, 0)),
                   pl.BlockSpec((1, SSM_WIDTH), lambda i: (0, 0))],
        out_shape=[jax.ShapeDtypeStruct(dproj.shape, dproj.dtype), jax.ShapeDtypeStruct((SSM_CHUNKS, CHUNK_IN, PAIR), F32),
                   jax.ShapeDtypeStruct((SSM_CHUNKS, CHUNK_IN, PAIR), F32),
                   jax.ShapeDtypeStruct((SSM_CHUNKS, CHUNK_TILES, PAIR), F32), jax.ShapeDtypeStruct((1, SSM_WIDTH), F32)],
        input_output_aliases={7: 0},
        scratch_shapes=[pltpu.VMEM((SSM_CHUNKS, rows, PAIR), F32), pltpu.VMEM((SSM_CHUNKS, CHUNK_TILES, PAIR), F32),
                        pltpu.VMEM((IN_BLOCKS, rows, LANES), F32), pltpu.VMEM((IN_BLOCKS, rows, LANES), F32)],
        compiler_params=_params(("arbitrary",)),
    )(dy, proj, s, w2, c2, a2, d_skip, dproj)


def _block(ref, axis, size, index):
    idx = [slice(None)] * len(ref.shape)
    idx[axis] = pl.ds(pl.multiple_of(index * size, size), size)
    return ref.at[tuple(idx)]


def _all_gather(name, shards, axes):
    n = len(shards)
    sizes = [s.shape[a] for s, a in zip(shards, axes)]

    def body(*refs):
        ins, outs = refs[:n], refs[n:2 * n]
        send_sems, recv_sems, local_sems = refs[2 * n:]
        x, y, c = (lax.axis_index(a) for a in MESH_AXES)
        me, sibling = (x, y, c), (x, y, 1 - c)
        chips = [(1 - x, y), (x, 1 - y), (1 - x, 1 - y)]

        def rows(i, dev):
            return _block(outs[i], axes[i], sizes[i], 4 * dev[0] + 2 * dev[1] + dev[2])

        def copy(i, k, block, to, src=None):
            return pltpu.make_async_remote_copy(
                src_ref=rows(i, block) if src is None else src, dst_ref=rows(i, block),
                send_sem=send_sems.at[7 * i + k], recv_sem=recv_sems.at[7 * i + k],
                device_id=to, device_id_type=MESH)

        mine = [pltpu.make_async_copy(ins[i], rows(i, me), local_sems.at[i]) for i in range(n)]
        for cp in mine:
            cp.start()
        first = []
        for i in range(n):
            first.append(copy(i, 0, me, sibling, src=ins[i]))
            first += [copy(i, 1 + j, me, (*chip, c), src=ins[i]) for j, chip in enumerate(chips)]
        for cp in first:
            cp.start()
        passed = []
        for i in range(n):
            for j, chip in enumerate(chips):
                copy(i, 1 + j, (*chip, c), me).wait_recv()
                fwd = copy(i, 4 + j, (*chip, c), sibling)
                fwd.start()
                passed.append(fwd)
        for i in range(n):
            copy(i, 0, sibling, me).wait_recv()
            for j, chip in enumerate(chips):
                copy(i, 4 + j, (*chip, 1 - c), me).wait_recv()
        for cp in first + passed:
            cp.wait_send()
        for cp in mine:
            cp.wait()

    out_shape = []
    for s, a in zip(shards, axes):
        shape = list(s.shape)
        shape[a] *= N_DEV
        out_shape.append(jax.ShapeDtypeStruct(tuple(shape), s.dtype))
    any_spec = pl.BlockSpec(memory_space=pl.ANY)
    return pl.pallas_call(
        body, name=name, out_shape=out_shape,
        in_specs=[any_spec] * n, out_specs=[any_spec] * n,
        scratch_shapes=[pltpu.SemaphoreType.DMA((7 * n,)), pltpu.SemaphoreType.DMA((7 * n,)),
                        pltpu.SemaphoreType.DMA((n,))],
    )(*shards)


HBM_SPEC = pl.BlockSpec(memory_space=pltpu.HBM)
SEM_SPEC = pl.BlockSpec(memory_space=pltpu.SEMAPHORE)
ANY_SPEC = pl.BlockSpec(memory_space=pl.ANY)
SPLIT_PARAMS = pltpu.CompilerParams(has_side_effects=pltpu.SideEffectType.DATAFLOW_SIDE_EFFECTING)
N_PEERS = N_DEV - 1
TOKEN = jax.ShapeDtypeStruct((SUBLANES, LANES), F32)
VMEM_SPEC = pl.BlockSpec(memory_space=pltpu.VMEM)


def _in_hbm(arrays):
    return [pltpu.with_memory_space_constraint(a, pltpu.HBM) for a in arrays]


def _peer(m):
    x, y, c = (lax.axis_index(a) for a in MESH_AXES)
    px = 1 - x if m & 4 else x
    py = 1 - y if m & 2 else y
    pc = 1 - c if m & 1 else c
    return (px, py, pc), 4 * px + 2 * py + pc


def _my_index():
    x, y, c = (lax.axis_index(a) for a in MESH_AXES)
    return 4 * x + 2 * y + c


def _gather_copies(shard_refs, full_refs, axes, send_sems, recv_sems):
    copies = []
    for i, (shard, full) in enumerate(zip(shard_refs, full_refs)):
        mine = _block(full, axes[i], shard.shape[axes[i]], _my_index())
        for m in range(1, N_DEV):
            peer, _ = _peer(m)
            copies.append(pltpu.make_async_remote_copy(
                src_ref=shard, dst_ref=mine, send_sem=send_sems.at[N_PEERS * i + m - 1],
                recv_sem=recv_sems.at[N_PEERS * i + m - 1], device_id=peer, device_id_type=MESH))
    return copies


def _gather_start(name, shards, axes, after):
    n = len(shards)

    def body(*refs):
        shard_refs = refs[:n]
        send_sems, recv_sems, local_sems = refs[n + 1:n + 4]
        full_refs = refs[2 * n + 4:3 * n + 4]
        refs[3 * n + 4][...] = jnp.zeros(TOKEN.shape, TOKEN.dtype)
        for i in range(n):
            pltpu.make_async_copy(shard_refs[i], _block(full_refs[i], axes[i], shard_refs[i].shape[axes[i]], _my_index()),
                                  local_sems.at[i]).start()
        for cp in _gather_copies(shard_refs, full_refs, axes, send_sems, recv_sems):
            cp.start()

    fulls = []
    for s, a in zip(shards, axes):
        shape = list(s.shape)
        shape[a] *= N_DEV
        fulls.append(pltpu.HBM(tuple(shape), s.dtype))
    out = pl.pallas_call(
        body, name=name,
        out_shape=(pltpu.SemaphoreType.DMA((N_PEERS * n,)), pltpu.SemaphoreType.DMA((N_PEERS * n,)),
                   pltpu.SemaphoreType.DMA((n,)), *[pltpu.HBM(s.shape, s.dtype) for s in shards], *fulls, TOKEN),
        in_specs=[HBM_SPEC] * n + [ANY_SPEC],
        out_specs=(SEM_SPEC, SEM_SPEC, SEM_SPEC, *[HBM_SPEC] * (2 * n), VMEM_SPEC),
        input_output_aliases={i: 3 + i for i in range(n)},
        compiler_params=SPLIT_PARAMS,
    )(*_in_hbm(shards), after)
    return out[:-1], out[-1]


def _gather_wait(name, started, indices, axes, after):
    send_sems, recv_sems, local_sems = started[:3]
    n_all = (len(started) - 3) // 2
    shards = [started[3 + i] for i in indices]
    fulls = [started[3 + n_all + i] for i in indices]
    n = len(indices)

    def body(*refs):
        shard_refs, full_refs = refs[:n], refs[n:2 * n]
        send_sems, recv_sems, local_sems = refs[2 * n:2 * n + 3]
        for j, i in enumerate(indices):
            mine = _block(full_refs[j], axes[j], shard_refs[j].shape[axes[j]], _my_index())
            pltpu.make_async_copy(shard_refs[j], mine, local_sems.at[i]).wait()
            for m in range(1, N_DEV):
                peer, _ = _peer(m)
                cp = pltpu.make_async_remote_copy(
                    src_ref=shard_refs[j], dst_ref=mine, send_sem=send_sems.at[N_PEERS * i + m - 1],
                    recv_sem=recv_sems.at[N_PEERS * i + m - 1], device_id=peer, device_id_type=MESH)
                cp.wait_send()
                cp.wait_recv()

    out = pl.pallas_call(
        body, name=name,
        out_shape=tuple(pltpu.HBM(a.shape, a.dtype) for a in shards + fulls),
        in_specs=[HBM_SPEC] * (2 * n) + [SEM_SPEC] * 3 + [ANY_SPEC], out_specs=tuple([HBM_SPEC] * (2 * n)),
        input_output_aliases={i: i for i in range(2 * n)},
        compiler_params=SPLIT_PARAMS,
    )(*shards, *fulls, send_sems, recv_sems, local_sems, after)
    return out[n:]


def _exchange_start(name, fulls, axes):
    n = len(fulls)
    sizes = [f.shape[a] // N_DEV for f, a in zip(fulls, axes)]

    def body(*refs):
        ins = refs[:n]
        send_sems, recv_sems = refs[n:n + 2]
        lands = refs[2 * n + 2:3 * n + 2]
        refs[3 * n + 2][...] = jnp.zeros(TOKEN.shape, TOKEN.dtype)
        for i in range(n):
            for m in range(1, N_DEV):
                peer, index = _peer(m)
                pltpu.make_async_remote_copy(
                    src_ref=_block(ins[i], axes[i], sizes[i], index), dst_ref=lands[i].at[m - 1],
                    send_sem=send_sems.at[N_PEERS * i + m - 1], recv_sem=recv_sems.at[N_PEERS * i + m - 1],
                    device_id=peer, device_id_type=MESH).start()

    lands = []
    for f, a, size in zip(fulls, axes, sizes):
        shape = list(f.shape)
        shape[a] = size
        lands.append(pltpu.HBM((N_PEERS, *shape), f.dtype))
    out = pl.pallas_call(
        body, name=name,
        out_shape=(pltpu.SemaphoreType.DMA((N_PEERS * n,)), pltpu.SemaphoreType.DMA((N_PEERS * n,)),
                   *[pltpu.HBM(f.shape, f.dtype) for f in fulls], *lands, TOKEN),
        in_specs=[HBM_SPEC] * n, out_specs=(SEM_SPEC, SEM_SPEC, *[HBM_SPEC] * (2 * n), VMEM_SPEC),
        input_output_aliases={i: 2 + i for i in range(n)},
        compiler_params=SPLIT_PARAMS,
    )(*_in_hbm(fulls))
    return out[:-1], out[-1]


def _exchange_wait(name, started, axes, after):
    send_sems, recv_sems = started[:2]
    n = (len(started) - 2) // 2
    fulls, lands = list(started[2:2 + n]), list(started[2 + n:])
    sizes = [f.shape[a] // N_DEV for f, a in zip(fulls, axes)]

    def body(*refs):
        ins, land_refs = refs[:n], refs[n:2 * n]
        send_sems, recv_sems = refs[2 * n:2 * n + 2]
        for i in range(n):
            for m in range(1, N_DEV):
                peer, index = _peer(m)
                cp = pltpu.make_async_remote_copy(
                    src_ref=_block(ins[i], axes[i], sizes[i], index), dst_ref=land_refs[i].at[m - 1],
                    send_sem=send_sems.at[N_PEERS * i + m - 1], recv_sem=recv_sems.at[N_PEERS * i + m - 1],
                    device_id=peer, device_id_type=MESH)
                cp.wait_send()
                cp.wait_recv()

    out = pl.pallas_call(
        body, name=name,
        out_shape=tuple(pltpu.HBM(a.shape, a.dtype) for a in fulls + lands),
        in_specs=[HBM_SPEC] * (2 * n) + [SEM_SPEC] * 2 + [ANY_SPEC], out_specs=tuple([HBM_SPEC] * (2 * n)),
        input_output_aliases={i: i for i in range(2 * n)},
        compiler_params=SPLIT_PARAMS,
    )(*fulls, *lands, send_sems, recv_sems, after)
    return out[:n], out[n:]


def _sum_parts(part_refs, ndim):
    g = None
    for p_ref in part_refs:
        stacked = len(p_ref.shape) > ndim
        terms = [p_ref[s] for s in range(p_ref.shape[0])] if stacked else [p_ref[...]]
        for term in terms:
            term = term.astype(F32)
            g = term if g is None else g + term
    return g


def _adamw_update(w_ref, m_ref, v_ref, g, g_ref, d_ref, nm_ref, nv_ref):
    c1 = 1.0 - ADAM_B1 ** ADAM_STEP
    c2 = 1.0 - ADAM_B2 ** ADAM_STEP
    new_m = ADAM_B1 * m_ref[...] + (1.0 - ADAM_B1) * g
    new_v = ADAM_B2 * v_ref[...] + (1.0 - ADAM_B2) * (g * g)
    g_ref[...] = g
    nm_ref[...] = new_m
    nv_ref[...] = new_v
    d_ref[...] = -ADAM_LR * ((new_m / c1) / (jnp.sqrt(new_v / c2) + ADAM_EPS) + ADAM_WD * w_ref[...])


def _adamw_small(ws, ms, vs, stacks, loss_stack):
    n = len(ws)

    def body(*refs):
        ins, outs = refs[:4 * n + 1], refs[4 * n + 1:]
        for i in range(n):
            _adamw_update(ins[i], ins[n + i], ins[2 * n + i], _sum_parts([ins[3 * n + i]], len(ins[i].shape)),
                          outs[i], outs[n + i], outs[2 * n + i], outs[3 * n + i])
        total = ins[4 * n][0]
        for dev in range(1, N_DEV):
            total = total + ins[4 * n][dev]
        outs[4 * n][...] = total

    res = pl.pallas_call(
        body, name="adamw_small",
        out_shape=[jax.ShapeDtypeStruct(w.shape, F32) for w in ws] * 4 + [jax.ShapeDtypeStruct((1, LANES), F32)],
        compiler_params=_params(None),
    )(*ws, *ms, *vs, *stacks, loss_stack)
    return res[:n], res[n:2 * n], res[2 * n:3 * n], res[3 * n:4 * n], res[4 * n]


def _adamw(name, w, m, v, parts):
    r, c = w.shape
    tr = _tile(r, 256)
    n_parts = len(parts)

    def body(*refs):
        _adamw_update(refs[0], refs[1], refs[2], _sum_parts(refs[3:3 + n_parts], 2), *refs[3 + n_parts:])

    row = pl.BlockSpec((tr, c), lambda i: (i, 0))
    in_specs = [row, row, row]
    for p in parts:
        in_specs.append(row if p.ndim == 2 else pl.BlockSpec((p.shape[0], tr, c), lambda i: (0, i, 0)))
    return pl.pallas_call(
        body, name=name, grid=(r // tr,), in_specs=in_specs, out_specs=[row] * 4,
        out_shape=[jax.ShapeDtypeStruct((r, c), F32)] * 4,
        compiler_params=_params(("arbitrary",)),
    )(w, m, v, *parts)


SMALL = ("norm_gain", "pool_scale", "a_re", "a_im", "log_dt", "b_re", "b_im", "c_re", "c_im", "d_skip", "final_gain")
LARGE = ("w_in", "w_pool", "w_glu", "w_out", "w_ple", "w_ple_gate")
LARGE_AXIS = {"w_in": 1, "w_pool": 1, "w_glu": 1, "w_out": 0, "w_ple": 1, "w_ple_gate": 0}
WEIGHTS = ("norm_gain", "w_in", "w_pool", "pool_scale", "a_re", "a_im", "log_dt", "b_re", "b_im", "c_re", "c_im",
           "d_skip", "w_glu", "w_out", "w_ple", "w_ple_gate", "final_gain")


def kernel(x, p, norm_gain, w_in, w_pool, pool_scale, a_re, a_im, log_dt, b_re, b_im, c_re, c_im, d_skip, w_glu, w_out, w_ple, w_ple_gate, final_gain, loss_target, m_norm_gain, m_w_in, m_w_pool, m_pool_scale, m_a_re, m_a_im, m_log_dt, m_b_re, m_b_im, m_c_re, m_c_im, m_d_skip, m_w_glu, m_w_out, m_w_ple, m_w_ple_gate, m_final_gain, v_norm_gain, v_w_in, v_w_pool, v_pool_scale, v_a_re, v_a_im, v_log_dt, v_b_re, v_b_im, v_c_re, v_c_im, v_d_skip, v_w_glu, v_w_out, v_w_ple, v_w_ple_gate, v_final_gain):
    weights = dict(norm_gain=norm_gain, w_in=w_in, w_pool=w_pool, pool_scale=pool_scale, a_re=a_re, a_im=a_im,
                   log_dt=log_dt, b_re=b_re, b_im=b_im, c_re=c_re, c_im=c_im, d_skip=d_skip, w_glu=w_glu,
                   w_out=w_out, w_ple=w_ple, w_ple_gate=w_ple_gate, final_gain=final_gain)
    mom_m = dict(norm_gain=m_norm_gain, w_in=m_w_in, w_pool=m_w_pool, pool_scale=m_pool_scale, a_re=m_a_re,
                 a_im=m_a_im, log_dt=m_log_dt, b_re=m_b_re, b_im=m_b_im, c_re=m_c_re, c_im=m_c_im,
                 d_skip=m_d_skip, w_glu=m_w_glu, w_out=m_w_out, w_ple=m_w_ple, w_ple_gate=m_w_ple_gate,
                 final_gain=m_final_gain)
    mom_v = dict(norm_gain=v_norm_gain, w_in=v_w_in, w_pool=v_w_pool, pool_scale=v_pool_scale, a_re=v_a_re,
                 a_im=v_a_im, log_dt=v_log_dt, b_re=v_b_re, b_im=v_b_im, c_re=v_c_re, c_im=v_c_im,
                 d_skip=v_d_skip, w_glu=v_w_glu, w_out=v_w_out, w_ple=v_w_ple, w_ple_gate=v_w_ple_gate,
                 final_gain=v_final_gain)

    t = x.shape[1]
    xs = x.reshape(t, D_MODEL)
    ps = p.reshape(t, PLE_DIM)
    target = loss_target.reshape(t, D_MODEL)
    gain1 = norm_gain.reshape(1, D_MODEL)
    gain_f = final_gain.reshape(1, D_MODEL)
    scale_p = pool_scale.reshape(1, POOL_WIDTH)
    skip = d_skip.reshape(1, SSM_WIDTH)

    shard2d = {k: weights[k][0] for k in LARGE}
    shard_bf = {k: shard2d[k].astype(BF16) for k in LARGE}
    full = {"w_in": _all_gather("w_in_all_gather", [shard_bf["w_in"]], [LARGE_AXIS["w_in"]])[0]}
    later = [k for k in LARGE if k != "w_in"]
    later_axes = [LARGE_AXIS[k] for k in later]
    gather, gather_token = _gather_start("weights_gather_start", [shard_bf[k] for k in later], later_axes,
                                         full["w_in"])

    def arrive(k, after):
        i = later.index(k)
        full[k] = _gather_wait("gather_wait_" + k, gather, [i], [later_axes[i]], after)[0]

    ar, ai = a_re[0], a_im[0]
    ldt = log_dt.reshape(N_SSM_GROUPS, 1)
    br_t = jnp.transpose(b_re[0], (0, 2, 1))
    bi_t = jnp.transpose(b_im[0], (0, 2, 1))
    ab_re, ab_im, bb_re, bb_im = _ssm_params(ar, ai, ldt, br_t, bi_t)
    tiles = (SSM_CHUNKS, CHUNK_TILES, LANES)
    abar = jnp.concatenate([ab_re.reshape(tiles), ab_im.reshape(tiles)], axis=-1)
    w_pair = _compact_pair(bb_re, bb_im)
    c_pair = _compact_pair(c_re[0], -c_im[0])

    hn, proj = _norm1_in_proj(xs, gain1, full["w_in"], gather_token)
    arrive("w_pool", proj)
    pooled, mixed = _pool_mix(proj, full["w_pool"])
    y, gel, states = _ssm_fwd(proj, w_pair, c_pair, abar, skip)
    arrive("w_glu", gel)
    hg = _mm_nn("glu_proj", gel, full["w_glu"], [F32])[0]
    arrive("w_out", hg)
    cat, h1, h1b = _gate_out_proj(mixed, proj, hg, scale_p, full["w_out"], xs)
    arrive("w_ple", h1b)
    arrive("w_ple_gate", h1b)
    de, dq, dh2, g_final_gain, loss_part = _ple_final(h1, h1b, ps, full["w_ple_gate"], full["w_ple"], target, gain_f)

    grads = {}
    grads["w_ple_gate"] = _mm_tn("ple_gate_wgrad", h1b, dq, BF16)
    grads["w_ple"] = _mm_tn("ple_wgrad", ps, de, BF16)
    sent, tokens = {}, {}

    def send(names):
        sent[names], tokens[names[0]] = _exchange_start(
            "grads_start_" + names[0], [grads[k] for k in names], [LARGE_AXIS[k] for k in names])

    send(("w_ple_gate", "w_ple"))
    dh1, dh1b = _residual_dgrad("ple_gate_dgrad", dq, full["w_ple_gate"], dh2)
    grads["w_out"] = _mm_tn("out_wgrad", cat, dh1b, BF16)
    send(("w_out",))
    dmixed, dproj, dhg, g_pool_scale = _out_dgrad_gate_bwd(
        dh1b, full["w_out"], mixed, proj, hg, scale_p, [tokens["w_ple_gate"], tokens["w_out"]])

    tk = _tile(t, 1024)
    grads["w_pool"] = _mm("pool_wgrad", [(pooled, (tk, POOL_GROUP), lambda i, j, s: (s, i),
                                          dmixed, (tk, POOL_GROUP), lambda i, j, s: (s, i))],
                          DOT_TN, (N_POOL_GROUPS, 1, t // tk),
                          [((N_POOL_GROUPS, POOL_GROUP, POOL_GROUP), BF16, (None, POOL_GROUP, POOL_GROUP),
                            lambda i, j, s: (i, 0, 0))], t // tk)[0]
    dproj = _pool_mix_bwd(dmixed, full["w_pool"], dproj)

    grads["w_glu"] = _mm_tn("glu_wgrad", gel, dhg, BF16)
    send(("w_pool", "w_glu"))

    def gelu_bwd_epilogue(acc, ex, out_refs):
        yv = ex[0][...]
        th = jnp.tanh(GELU_C * (yv + GELU_A * yv * yv * yv))
        dgelu = 0.5 * (1.0 + th) + 0.5 * yv * (1.0 - th * th) * GELU_C * (1.0 + 3.0 * GELU_A * yv * yv)
        out_refs[0][...] = acc * dgelu

    dy = _mm_nt("glu_dgrad", dhg, full["w_glu"], [F32], tk=2048, extras=[y], epilogue=gelu_bwd_epilogue,
                after=[tokens["w_pool"]])[0]
    dproj, g_c_pair, g_w_pair, g_abar, g_d_skip = _ssm_bwd(dy, proj, states, w_pair, c_pair, abar, skip, dproj)

    g_ab_re = g_abar[..., :LANES].reshape(N_SSM_GROUPS, SSM_STATE)
    g_ab_im = g_abar[..., LANES:].reshape(N_SSM_GROUPS, SSM_STATE)
    d_ar, d_ai, d_ldt, d_br_t, d_bi_t = _ssm_params_bwd(
        ar, ai, ldt, br_t, bi_t, g_ab_re, g_ab_im,
        _expand_grad(g_w_pair[..., :LANES]), _expand_grad(g_w_pair[..., LANES:]))

    small_grads = dict(
        pool_scale=g_pool_scale, a_re=d_ar, a_im=d_ai, log_dt=d_ldt.reshape(1, N_SSM_GROUPS),
        b_re=d_br_t.astype(BF16), b_im=d_bi_t.astype(BF16), c_re=_expand_grad(g_c_pair[..., :LANES]).astype(BF16),
        c_im=(-_expand_grad(g_c_pair[..., LANES:])).astype(BF16), d_skip=g_d_skip, final_gain=g_final_gain)
    early = [k for k in SMALL if k != "norm_gain"]
    early_sent, early_token = _gather_start(
        "small_grads_start", [small_grads[k][None] for k in early] + [jnp.broadcast_to(loss_part, (1, 1, LANES))],
        [0] * (len(early) + 1), d_ar)

    grads["w_in"] = _mm_tn("in_wgrad", hn, dproj, BF16, after=[early_token])
    send(("w_in",))
    grad_x, g_norm_gain = _in_dgrad_norm1_bwd(dproj, full["w_in"], xs, dh1, gain1, tokens["w_in"])
    late_sent, late_token = _gather_start("norm_gain_grad_start", [g_norm_gain[None]], [0], g_norm_gain)

    out_g, out_d, out_m, out_v = ({} for _ in range(4))
    me = 4 * lax.axis_index("x") + 2 * lax.axis_index("y") + lax.axis_index("c")
    after = late_token
    for names, started in sent.items():
        axes = [LARGE_AXIS[k] for k in names]
        partials, landed = _exchange_wait("grads_wait_" + names[0], started, axes, after)
        for k, axis, partial, land in zip(names, axes, partials, landed):
            shard_shape = shard2d[k].shape
            size = shard_shape[axis]
            own = lax.dynamic_slice_in_dim(partial, me * size, size, axis=axis)
            view = (-1, shard_shape[-1])
            rows = math.prod(shard_shape[:-1])
            res = _adamw("adamw_" + k, shard2d[k].reshape(view), mom_m[k][0].reshape(view), mom_v[k][0].reshape(view),
                         [own.reshape(view), land.reshape(N_PEERS, rows, shard_shape[-1])])
            out_g[k], out_d[k], out_m[k], out_v[k] = (r.reshape(weights[k].shape) for r in res)
            after = res[0]

    def b_view(a):
        return jnp.transpose(a[0], (0, 2, 1))

    views = dict(norm_gain=lambda a: a, pool_scale=lambda a: a, a_re=lambda a: a[0], a_im=lambda a: a[0],
                 log_dt=lambda a: a, b_re=b_view, b_im=b_view, c_re=lambda a: a[0], c_im=lambda a: a[0],
                 d_skip=lambda a: a, final_gain=lambda a: a.reshape(1, D_MODEL))
    landed = _gather_wait("small_grads_wait", early_sent, list(range(len(early) + 1)), [0] * (len(early) + 1), after)
    stack = dict(zip(early, landed))
    stack["norm_gain"] = _gather_wait("norm_gain_grad_wait", late_sent, [0], [0], after)[0]
    *small_out, loss_row = _adamw_small(
        [views[k](weights[k]) for k in SMALL], [views[k](mom_m[k]) for k in SMALL],
        [views[k](mom_v[k]) for k in SMALL], [stack[k] for k in SMALL], landed[-1])
    loss = loss_row[0, 0]
    for out, res in zip((out_g, out_d, out_m, out_v), small_out):
        for k, r in zip(SMALL, res):
            if k in ("b_re", "b_im"):
                r = jnp.transpose(r, (0, 2, 1))
            out[k] = r.reshape(weights[k].shape)

    return (loss, grad_x.reshape(x.shape), *[out_g[k] for k in WEIGHTS], *[out_d[k] for k in WEIGHTS],
            *[out_m[k] for k in WEIGHTS], *[out_v[k] for k in WEIGHTS])
```

```python
import math

import jax
import jax.numpy as jnp
from jax import lax
from jax.experimental import pallas as pl
from jax.experimental.pallas import tpu as pltpu

F32 = jnp.float32
BF16 = jnp.bfloat16
MESH = pl.DeviceIdType.MESH
MESH_AXES = ("x", "y", "c")
N_DEV = 8

D_MODEL = 2048
POOL_WIDTH = 1024
SSM_WIDTH = 1024
N_POOL_GROUPS = 4
POOL_GROUP = 256
SSM_GROUP = 16
N_SSM_GROUPS = 64
SSM_STATE = 64
SSM_FLAT = N_SSM_GROUPS * SSM_STATE
SSM_CHUNKS = 4
CHUNK_IN = SSM_WIDTH // SSM_CHUNKS
CHUNK_STATE = SSM_FLAT // SSM_CHUNKS
PLE_DIM = 256
EPS = 1e-6
A_RE_MAX = -1e-4
ADAM_LR = 0.001
ADAM_B1 = 0.9
ADAM_B2 = 0.999
ADAM_EPS = 1e-08
ADAM_WD = 0.01
ADAM_STEP = 10
GELU_C = math.sqrt(2.0 / math.pi)
GELU_A = 0.044715

SUBLANES = 8
LANES = 128
VMEM_LIMIT_BYTES = 48 * 1024 * 1024

DOT_NN = (((1,), (0,)), ((), ()))
DOT_NT = (((1,), (1,)), ((), ()))
DOT_TN = (((0,), (0,)), ((), ()))


def _tile(n, pref):
    return pref if n % pref == 0 else n


def _params(sem):
    return pltpu.CompilerParams(dimension_semantics=sem, vmem_limit_bytes=VMEM_LIMIT_BYTES)


def _sigmoid(v):
    return 1.0 / (1.0 + jnp.exp(-v))


def _silu_and_grad(v):
    s = _sigmoid(v)
    return v * s, s * (1.0 + v * (1.0 - s))


def _mm(name, pairs, dims, grid, outs, k_steps, extras=(), epilogue=None):
    n_pairs, n_ex, n_out = len(pairs), len(extras), len(outs)
    acc_shape = tuple(d for d in outs[0][2] if d is not None)
    if epilogue is None:
        def epilogue(acc, ex, out_refs):
            out_refs[0][...] = acc.astype(out_refs[0].dtype)

    def body(*refs):
        ab = refs[:2 * n_pairs]
        ex = refs[2 * n_pairs:2 * n_pairs + n_ex]
        out_refs = refs[2 * n_pairs + n_ex:2 * n_pairs + n_ex + n_out]
        acc = refs[-1]
        k = pl.program_id(2)

        @pl.when(k == 0)
        def _():
            acc[...] = jnp.zeros_like(acc)

        part = None
        for q in range(n_pairs):
            d = lax.dot_general(ab[2 * q][...].astype(BF16), ab[2 * q + 1][...].astype(BF16), dims,
                                preferred_element_type=F32)
            part = d if part is None else part + d
        acc[...] += part

        @pl.when(k == k_steps - 1)
        def _():
            epilogue(acc[...], ex, out_refs)

    in_specs, operands = [], []
    for a, a_blk, a_map, b, b_blk, b_map in pairs:
        in_specs += [pl.BlockSpec(a_blk, a_map), pl.BlockSpec(b_blk, b_map)]
        operands += [a, b]
    for e, e_blk, e_map in extras:
        in_specs.append(pl.BlockSpec(e_blk, e_map))
        operands.append(e)
    return pl.pallas_call(
        body, name=name, grid=grid, in_specs=in_specs,
        out_specs=[pl.BlockSpec(o[2], o[3]) for o in outs],
        out_shape=[jax.ShapeDtypeStruct(o[0], o[1]) for o in outs],
        scratch_shapes=[pltpu.VMEM(acc_shape, F32)],
        compiler_params=_params(("arbitrary", "arbitrary", "arbitrary")),
    )(*operands)


def _after(tokens):
    return [(tok, tok.shape, lambda i, j, s: (0, 0)) for tok in tokens]


def _mm_nn(name, a, b, out_dtypes, tm=1024, tn=1024, tk=1024, a_col0=0, extras=(), epilogue=None, after=()):
    m, n = a.shape[0], b.shape[1]
    k = b.shape[0]
    tm, tn, tk = _tile(m, tm), _tile(n, tn), _tile(k, tk)
    outs = [((m, n), dt, (tm, tn), lambda i, j, s: (i, j)) for dt in out_dtypes]
    ex = [(e, (tm, tn), lambda i, j, s: (i, j)) for e in extras] + _after(after)
    return _mm(name, [(a, (tm, tk), lambda i, j, s: (i, a_col0 + s), b, (tk, tn), lambda i, j, s: (s, j))],
               DOT_NN, (m // tm, n // tn, k // tk), outs, k // tk, ex, epilogue)


def _mm_nt(name, a, b, out_dtypes, tm=1024, tn=1024, tk=1024, extras=(), epilogue=None, after=()):
    m, kk = a.shape
    n = b.shape[0]
    tm, tn, tk = _tile(m, tm), _tile(n, tn), _tile(kk, tk)
    outs = [((m, n), dt, (tm, tn), lambda i, j, s: (i, j)) for dt in out_dtypes]
    ex = [(e, (tm, tn), lambda i, j, s: (i, j)) for e in extras] + _after(after)
    return _mm(name, [(a, (tm, tk), lambda i, j, s: (i, s), b, (tn, tk), lambda i, j, s: (j, s))],
               DOT_NT, (m // tm, n // tn, kk // tk), outs, kk // tk, ex, epilogue)


def _mm_tn(name, a, b, out_dtype, tm=512, tn=2048, tk=1024, after=()):
    m, kk = a.shape
    n = b.shape[1]
    tm, tn, tk = _tile(kk, tm), _tile(n, tn), _tile(m, tk)
    outs = [((kk, n), out_dtype, (tm, tn), lambda i, j, s: (i, j))]
    return _mm(name, [(a, (tk, tm), lambda i, j, s: (s, i), b, (tk, tn), lambda i, j, s: (s, j))],
               DOT_TN, (kk // tm, n // tn, m // tk), outs, m // tk, _after(after))[0]


ROW_TILE = 256


def _norm1_fwd(x, gain):
    t = x.shape[0]
    tm = _tile(t, 512)

    def body(x_ref, g_ref, hn_ref):
        xv = x_ref[...]
        r = lax.rsqrt(jnp.mean(xv * xv, axis=-1, keepdims=True) + EPS)
        hn_ref[...] = (xv * r * g_ref[...]).astype(BF16)

    return pl.pallas_call(
        body, name="norm1_fwd", grid=(t // tm,),
        in_specs=[pl.BlockSpec((tm, D_MODEL), lambda i: (i, 0)), pl.BlockSpec((1, D_MODEL), lambda i: (0, 0))],
        out_specs=pl.BlockSpec((tm, D_MODEL), lambda i: (i, 0)),
        out_shape=jax.ShapeDtypeStruct((t, D_MODEL), BF16),
        compiler_params=_params(("arbitrary",)),
    )(x, gain)


GATHER_ROWS = 1024
SLOT_MASKS = (0, 1, 2, 4, 6, 3, 5, 7)
DIRECT_SLOTS = (1, 2, 3, 4)
PASSED_ON_SLOTS = (2, 3, 4)


def _gather_in_proj(hn, shard, me):
    t, d = hn.shape
    width = shard.shape[1]
    tm = _tile(t, GATHER_ROWS)
    n_tiles = t // tm
    order = jnp.stack([jnp.bitwise_xor(me, mask) for mask in SLOT_MASKS]).astype(jnp.int32)

    def body(order_ref, hn_ref, shard_ref, proj_ref, full_ref, wbuf, send_sems, recv_sems, local_sems):
        b, i = pl.program_id(0), pl.program_id(1)
        x, y, c = (lax.axis_index(a) for a in MESH_AXES)

        def to_peer(mask, src_slot, dst_slot, k):
            peer = (1 - x if mask & 4 else x, 1 - y if mask & 2 else y, 1 - c if mask & 1 else c)
            return pltpu.make_async_remote_copy(
                src_ref=wbuf.at[src_slot], dst_ref=wbuf.at[dst_slot], send_sem=send_sems.at[k],
                recv_sem=recv_sems.at[dst_slot - 1], device_id=peer, device_id_type=MESH)

        def keep(slot):
            cols = pl.ds(pl.multiple_of(order_ref[slot] * width, width), width)
            return pltpu.make_async_copy(wbuf.at[slot], full_ref.at[:, cols], local_sems.at[slot])

        sends = [to_peer(SLOT_MASKS[slot], 0, slot, k) for k, slot in enumerate(DIRECT_SLOTS)]
        sends += [to_peer(1, slot, slot + len(PASSED_ON_SLOTS), len(DIRECT_SLOTS) + k)
                  for k, slot in enumerate(PASSED_ON_SLOTS)]

        @pl.when((i == 0) & (b == 0))
        def _():
            own = pltpu.make_async_copy(shard_ref, wbuf.at[0], local_sems.at[N_DEV])
            own.start()
            own.wait()
            for cp in sends[:len(DIRECT_SLOTS)]:
                cp.start()
            keep(0).start()

        for slot in range(1, N_DEV):
            @pl.when((i == 0) & (b == slot))
            def _(slot=slot):
                to_peer(1, 0, slot, 0).wait_recv()
                if slot in PASSED_ON_SLOTS:
                    sends[len(DIRECT_SLOTS) + PASSED_ON_SLOTS.index(slot)].start()
                keep(slot).start()

        proj_ref[...] = jnp.dot(hn_ref[...], wbuf[b], preferred_element_type=F32)

        @pl.when((b == N_DEV - 1) & (i == n_tiles - 1))
        def _():
            for cp in sends:
                cp.wait_send()
            for slot in range(N_DEV):
                keep(slot).wait()

    grid_spec = pltpu.PrefetchScalarGridSpec(
        num_scalar_prefetch=1, grid=(N_DEV, n_tiles),
        in_specs=[pl.BlockSpec((tm, d), lambda b, i, order_ref: (i, 0)), pl.BlockSpec(memory_space=pl.ANY)],
        out_specs=[pl.BlockSpec((tm, width), lambda b, i, order_ref: (i, order_ref[b])),
                   pl.BlockSpec(memory_space=pl.ANY)],
        scratch_shapes=[pltpu.VMEM((N_DEV, d, width), BF16), pltpu.SemaphoreType.DMA((N_DEV - 1,)),
                        pltpu.SemaphoreType.DMA((N_DEV - 1,)), pltpu.SemaphoreType.DMA((N_DEV + 1,))])
    return pl.pallas_call(
        body, name="w_in_gather_in_proj", grid_spec=grid_spec,
        out_shape=[jax.ShapeDtypeStruct((t, N_DEV * width), F32), jax.ShapeDtypeStruct((d, N_DEV * width), BF16)],
        compiler_params=_params(("arbitrary", "arbitrary")),
    )(order, hn, shard)


def _in_dgrad_norm1_bwd(dproj, w_in, x, dh1, gain, after):
    t = x.shape[0]
    tm = _tile(t, ROW_TILE)

    def body(dp_ref, w_ref, x_ref, dh1_ref, g_ref, _, dx_ref, gg_ref):
        @pl.when(pl.program_id(0) == 0)
        def _():
            gg_ref[...] = jnp.zeros_like(gg_ref)

        dhn = lax.dot_general(dp_ref[...], w_ref[...], DOT_NT, preferred_element_type=F32)
        xv = x_ref[...]
        r = lax.rsqrt(jnp.mean(xv * xv, axis=-1, keepdims=True) + EPS)
        xh = xv * r
        gg_ref[...] += jnp.sum(dhn * xh, axis=0, keepdims=True)
        dxh = dhn * g_ref[...]
        dx_ref[...] = dh1_ref[...] + r * (dxh - xh * jnp.mean(dxh * xh, axis=-1, keepdims=True))

    row = pl.BlockSpec((tm, D_MODEL), lambda i: (i, 0))
    vec = pl.BlockSpec((1, D_MODEL), lambda i: (0, 0))
    return pl.pallas_call(
        body, name="in_dgrad_norm1_bwd", grid=(t // tm,),
        in_specs=[pl.BlockSpec((tm, dproj.shape[1]), lambda i: (i, 0)), _resident(w_in.shape), row, row, vec,
                  pl.BlockSpec(after.shape, lambda i: (0, 0))],
        out_specs=[row, vec],
        out_shape=[jax.ShapeDtypeStruct((t, D_MODEL), F32), jax.ShapeDtypeStruct((1, D_MODEL), F32)],
        compiler_params=_params(("arbitrary",)),
    )(dproj, w_in, x, dh1, gain, after)


def _pool_counts(t, width, group):
    row = lax.broadcasted_iota(jnp.int32, (t, width), 0)
    window = jnp.left_shift(jnp.int32(2), group)
    return row, jnp.minimum(row + 1, window).astype(F32)


def _select_window(group, s2, s4, s8, s16):
    return jnp.where(group == 0, s2, jnp.where(group == 1, s4, jnp.where(group == 2, s8, s16)))


def _pool_mix(proj, w_pool, after):
    t = proj.shape[0]

    def body(u_ref, w_ref, _, pooled_ref, mixed_ref):
        group = pl.program_id(0)
        row, count = _pool_counts(t, LANES, group)

        def down(a, j):
            return jnp.where(row >= j, pltpu.roll(a, j, 0), 0.0)

        for h in range(POOL_GROUP // LANES):
            cols = slice(h * LANES, (h + 1) * LANES)
            v = u_ref[:, cols]
            s2 = v + down(v, 1)
            s4 = s2 + down(s2, 2)
            s8 = s4 + down(s4, 4)
            s16 = s8 + down(s8, 8)
            pooled_ref[:, cols] = (_select_window(group, s2, s4, s8, s16) / count - v).astype(BF16)
        mixed_ref[...] = jnp.dot(pooled_ref[...], w_ref[...], preferred_element_type=F32)

    block = pl.BlockSpec((t, POOL_GROUP), lambda g: (0, g))
    return pl.pallas_call(
        body, name="pool_mix", grid=(N_POOL_GROUPS,),
        in_specs=[block, pl.BlockSpec((None, POOL_GROUP, POOL_GROUP), lambda g: (g, 0, 0)),
                  pl.BlockSpec(after.shape, lambda g: (0, 0))],
        out_specs=[block, block],
        out_shape=[jax.ShapeDtypeStruct((t, POOL_WIDTH), BF16), jax.ShapeDtypeStruct((t, POOL_WIDTH), F32)],
        compiler_params=_params(("arbitrary",)),
    )(proj, w_pool, after)


def _pool_mix_bwd(dmixed, w_pool, dproj):
    t = dmixed.shape[0]

    def body(dm_ref, w_ref, _, o_ref):
        group = pl.program_id(0)
        row, count = _pool_counts(t, LANES, group)

        def up(a, j):
            return jnp.where(row < t - j, pltpu.roll(a, t - j, 0), 0.0)

        dpooled = lax.dot_general(dm_ref[...], w_ref[...], DOT_NT, preferred_element_type=F32)
        for h in range(POOL_GROUP // LANES):
            cols = slice(h * LANES, (h + 1) * LANES)
            dp = dpooled[:, cols]
            r = dp / count
            s2 = r + up(r, 1)
            s4 = s2 + up(s2, 2)
            s8 = s4 + up(s4, 4)
            s16 = s8 + up(s8, 8)
            o_ref[:, cols] = (_select_window(group, s2, s4, s8, s16) - dp).astype(BF16)

    block = pl.BlockSpec((t, POOL_GROUP), lambda g: (0, g))
    return pl.pallas_call(
        body, name="pool_mix_bwd", grid=(N_POOL_GROUPS,),
        in_specs=[block, pl.BlockSpec((None, POOL_GROUP, POOL_GROUP), lambda g: (g, 0, 0)),
                  pl.BlockSpec(memory_space=pl.ANY)],
        out_specs=block,
        out_shape=jax.ShapeDtypeStruct(dproj.shape, dproj.dtype),
        input_output_aliases={2: 0},
        compiler_params=_params(("arbitrary",)),
    )(dmixed, w_pool, dproj)


def _gate_out_proj(mixed, proj, hg, pool_scale, w_out, x):
    t = mixed.shape[0]
    tm = _tile(t, ROW_TILE)

    def body(mx_ref, ga_ref, gb_ref, hg_ref, ps_ref, w_ref, x_ref, cat_ref, h1_ref, h1b_ref):
        silu_a, _ = _silu_and_grad(ga_ref[...])
        cat_ref[:, :POOL_WIDTH] = (mx_ref[...] * ps_ref[...] * silu_a).astype(BF16)
        silu_b, _ = _silu_and_grad(gb_ref[...])
        sb = hg_ref[:, :SSM_WIDTH] * _sigmoid(hg_ref[:, SSM_WIDTH:])
        cat_ref[:, POOL_WIDTH:] = (sb * silu_b).astype(BF16)
        h1 = x_ref[...] + jnp.dot(cat_ref[...], w_ref[...], preferred_element_type=F32)
        h1_ref[...] = h1
        h1b_ref[...] = h1.astype(BF16)

    row = pl.BlockSpec((tm, D_MODEL), lambda i: (i, 0))
    return pl.pallas_call(
        body, name="gate_out_proj", grid=(t // tm,),
        in_specs=[pl.BlockSpec((tm, POOL_WIDTH), lambda i: (i, 0)),
                  pl.BlockSpec((tm, POOL_WIDTH), lambda i: (i, 1)),
                  pl.BlockSpec((tm, SSM_WIDTH), lambda i: (i, 3)),
                  pl.BlockSpec((tm, 2 * SSM_WIDTH), lambda i: (i, 0)),
                  pl.BlockSpec((1, POOL_WIDTH), lambda i: (0, 0)), _resident(w_out.shape), row],
        out_specs=[row, row, row],
        out_shape=[jax.ShapeDtypeStruct((t, D_MODEL), BF16), jax.ShapeDtypeStruct((t, D_MODEL), F32),
                   jax.ShapeDtypeStruct((t, D_MODEL), BF16)],
        compiler_params=_params(("arbitrary",)),
    )(mixed, proj, proj, hg, pool_scale, w_out, x)


def _residual_dgrad(name, dy, w, residual):
    t = dy.shape[0]
    tm = _tile(t, ROW_TILE)
    n = w.shape[0]

    def body(dy_ref, w_ref, r_ref, o_ref, ob_ref):
        o = r_ref[...] + lax.dot_general(dy_ref[...], w_ref[...], DOT_NT, preferred_element_type=F32)
        o_ref[...] = o
        ob_ref[...] = o.astype(BF16)

    out = pl.BlockSpec((tm, n), lambda i: (i, 0))
    return pl.pallas_call(
        body, name=name, grid=(t // tm,),
        in_specs=[pl.BlockSpec((tm, dy.shape[1]), lambda i: (i, 0)), _resident(w.shape), out],
        out_specs=[out, out],
        out_shape=[jax.ShapeDtypeStruct((t, n), F32), jax.ShapeDtypeStruct((t, n), BF16)],
        compiler_params=_params(("arbitrary",)),
    )(dy, w, residual)


def _out_dgrad_gate_bwd(dh1b, w_out, mixed, proj, hg, pool_scale, after):
    t = mixed.shape[0]
    tm = _tile(t, ROW_TILE)
    n_after = len(after)

    def body(dh_ref, w_ref, mx_ref, ga_ref, gb_ref, hg_ref, ps_ref, *rest):
        dmx_ref, dp_ref, dhg_ref, gps_ref = rest[n_after:]

        @pl.when(pl.program_id(0) == 0)
        def _():
            gps_ref[...] = jnp.zeros_like(gps_ref)

        dcat = lax.dot_general(dh_ref[...], w_ref[...], DOT_NT, preferred_element_type=F32)
        ps = ps_ref[...]
        mx = mx_ref[...]
        dya = dcat[:, :POOL_WIDTH]
        silu_a, dsilu_a = _silu_and_grad(ga_ref[...])
        dpa = dya * silu_a
        gps_ref[...] += jnp.sum(dpa * mx, axis=0, keepdims=True)
        dmx_ref[...] = (dpa * ps).astype(BF16)
        dp_ref[:, :POOL_WIDTH] = jnp.zeros((tm, POOL_WIDTH), BF16)
        dp_ref[:, POOL_WIDTH:2 * POOL_WIDTH] = (dya * mx * ps * dsilu_a).astype(BF16)

        dyb = dcat[:, POOL_WIDTH:]
        silu_b, dsilu_b = _silu_and_grad(gb_ref[...])
        h_a = hg_ref[:, :SSM_WIDTH]
        sg = _sigmoid(hg_ref[:, SSM_WIDTH:])
        dsb = dyb * silu_b
        dp_ref[:, 2 * POOL_WIDTH:2 * POOL_WIDTH + SSM_WIDTH] = jnp.zeros((tm, SSM_WIDTH), BF16)
        dp_ref[:, 2 * POOL_WIDTH + SSM_WIDTH:] = (dyb * h_a * sg * dsilu_b).astype(BF16)
        dhg_ref[:, :SSM_WIDTH] = (dsb * sg).astype(BF16)
        dhg_ref[:, SSM_WIDTH:] = (dsb * h_a * sg * (1.0 - sg)).astype(BF16)

    half = pl.BlockSpec((tm, POOL_WIDTH), lambda i: (i, 0))
    full = pl.BlockSpec((tm, D_MODEL), lambda i: (i, 0))
    vec = pl.BlockSpec((1, POOL_WIDTH), lambda i: (0, 0))
    proj_width = 2 * POOL_WIDTH + 2 * SSM_WIDTH
    return pl.pallas_call(
        body, name="out_dgrad_gate_bwd", grid=(t // tm,),
        in_specs=[full, _resident(w_out.shape), half,
                  pl.BlockSpec((tm, POOL_WIDTH), lambda i: (i, 1)),
                  pl.BlockSpec((tm, SSM_WIDTH), lambda i: (i, 3)),
                  full, vec] + [pl.BlockSpec(tok.shape, lambda i: (0, 0)) for tok in after],
        out_specs=[half, pl.BlockSpec((tm, proj_width), lambda i: (i, 0)), full, vec],
        out_shape=[jax.ShapeDtypeStruct((t, POOL_WIDTH), BF16), jax.ShapeDtypeStruct((t, proj_width), BF16),
                   jax.ShapeDtypeStruct((t, 2 * SSM_WIDTH), BF16),
                   jax.ShapeDtypeStruct((1, POOL_WIDTH), F32)],
        compiler_params=_params(("arbitrary",)),
    )(dh1b, w_out, mixed, proj, proj, hg, pool_scale, *after)


def _ple_final(h1, h1b, p, w_gate, w_ple, target, gain):
    t = h1.shape[0]
    tm = _tile(t, 256)

    def body(h1_ref, h1b_ref, p_ref, wg_ref, wp_ref, tg_ref, g_ref, de_ref, dq_ref, dh2_ref, gg_ref, loss_ref):
        @pl.when(pl.program_id(0) == 0)
        def _():
            gg_ref[...] = jnp.zeros_like(gg_ref)
            loss_ref[...] = jnp.zeros_like(loss_ref)

        ev = jnp.dot(p_ref[...].astype(BF16), wp_ref[...], preferred_element_type=F32)
        sg = _sigmoid(jnp.dot(h1b_ref[...], wg_ref[...], preferred_element_type=F32))
        h2 = h1_ref[...] + ev * sg
        r = lax.rsqrt(jnp.mean(h2 * h2, axis=-1, keepdims=True) + EPS)
        n = h2 * r
        gain_v = g_ref[...]
        diff = n * gain_v - tg_ref[...]
        row_loss = jnp.sum(diff * diff, axis=-1, keepdims=True)
        loss_ref[...] += (0.5 / D_MODEL) * jnp.sum(row_loss, axis=0, keepdims=True)
        dout = diff * (1.0 / D_MODEL)
        gg_ref[...] += jnp.sum(dout * n, axis=0, keepdims=True)
        dn = dout * gain_v
        dh2 = r * (dn - n * jnp.mean(dn * n, axis=-1, keepdims=True))
        dh2_ref[...] = dh2
        de_ref[...] = (dh2 * sg).astype(BF16)
        dq_ref[...] = (dh2 * ev * sg * (1.0 - sg)).astype(BF16)

    row = pl.BlockSpec((tm, D_MODEL), lambda i: (i, 0))
    vec = pl.BlockSpec((1, D_MODEL), lambda i: (0, 0))
    return pl.pallas_call(
        body, name="ple_final", grid=(t // tm,),
        in_specs=[row, row, pl.BlockSpec((tm, PLE_DIM), lambda i: (i, 0)), _resident((D_MODEL, D_MODEL)),
                  _resident((PLE_DIM, D_MODEL)), row, vec],
        out_specs=[row, row, row, vec, pl.BlockSpec((1, 1), lambda i: (0, 0))],
        out_shape=[jax.ShapeDtypeStruct((t, D_MODEL), BF16), jax.ShapeDtypeStruct((t, D_MODEL), BF16),
                   jax.ShapeDtypeStruct((t, D_MODEL), F32), jax.ShapeDtypeStruct((1, D_MODEL), F32),
                   jax.ShapeDtypeStruct((1, 1), F32)],
        compiler_params=_params(("arbitrary",)),
    )(h1, h1b, p, w_gate, w_ple, target, gain)


def _zoh(a_re, a_im, log_dt, b_re_t, b_im_t):
    lam_re = jnp.minimum(a_re, A_RE_MAX)
    lam_im = a_im
    dt = jnp.exp(log_dt)
    mag = jnp.exp(lam_re * dt)
    ang = lam_im * dt
    ab_re = mag * jnp.cos(ang)
    ab_im = mag * jnp.sin(ang)
    den = lam_re * lam_re + lam_im * lam_im
    n_re = ab_re - 1.0
    n_im = ab_im
    q_re = (n_re * lam_re + n_im * lam_im) / den
    q_im = (n_im * lam_re - n_re * lam_im) / den
    bb_re = q_re[:, None, :] * b_re_t - q_im[:, None, :] * b_im_t
    bb_im = q_re[:, None, :] * b_im_t + q_im[:, None, :] * b_re_t
    return ab_re, ab_im, bb_re, bb_im


def _ssm_params(a_re, a_im, log_dt, b_re_t, b_im_t):
    def body(are_ref, aim_ref, dt_ref, bre_ref, bim_ref, abre_ref, abim_ref, bbre_ref, bbim_ref):
        ab_re, ab_im, bb_re, bb_im = _zoh(are_ref[...], aim_ref[...], dt_ref[...], bre_ref[...], bim_ref[...])
        abre_ref[...] = ab_re
        abim_ref[...] = ab_im
        bbre_ref[...] = bb_re
        bbim_ref[...] = bb_im

    return pl.pallas_call(
        body, name="ssm_params",
        out_shape=[jax.ShapeDtypeStruct(a_re.shape, F32), jax.ShapeDtypeStruct(a_re.shape, F32),
                   jax.ShapeDtypeStruct(b_re_t.shape, F32), jax.ShapeDtypeStruct(b_re_t.shape, F32)],
        compiler_params=_params(None),
    )(a_re, a_im, log_dt, b_re_t, b_im_t)


def _ssm_params_bwd(a_re, a_im, log_dt, b_re_t, b_im_t, g_ab_re, g_ab_im, g_bb_re, g_bb_im):
    def body(are_ref, aim_ref, dt_ref, bre_ref, bim_ref, gar_ref, gai_ref, gbr_ref, gbi_ref,
             o_are, o_aim, o_dt, o_bre, o_bim):
        _, vjp = jax.vjp(_zoh, are_ref[...], aim_ref[...], dt_ref[...], bre_ref[...], bim_ref[...])
        d_are, d_aim, d_dt, d_bre, d_bim = vjp((gar_ref[...], gai_ref[...], gbr_ref[...], gbi_ref[...]))
        o_are[...] = d_are
        o_aim[...] = d_aim
        o_dt[...] = d_dt
        o_bre[...] = d_bre
        o_bim[...] = d_bim

    ins = (a_re, a_im, log_dt, b_re_t, b_im_t)
    return pl.pallas_call(
        body, name="ssm_params_bwd",
        out_shape=[jax.ShapeDtypeStruct(v.shape, F32) for v in ins],
        compiler_params=_params(None),
    )(*ins, g_ab_re, g_ab_im, g_bb_re, g_bb_im)


CHUNK_TILES = CHUNK_STATE // LANES
CH_PER_TILE = CHUNK_IN // CHUNK_TILES
PAIR = 2 * LANES
SSM_ROWS = 256
SCAN_STEPS = 8
U_COLUMN_BLOCK = 2 * POOL_WIDTH // SSM_WIDTH


def _own_half():
    r = lax.broadcasted_iota(jnp.int32, (CHUNK_IN, LANES), 0) // SSM_GROUP % 2
    c = lax.broadcasted_iota(jnp.int32, (CHUNK_IN, LANES), 1) // SSM_STATE
    return (r == c)[None]


def _compact_weight(w):
    tiled = jnp.tile(w.reshape(SSM_CHUNKS, CHUNK_IN, SSM_STATE), (1, 1, 2))
    return jnp.where(_own_half(), tiled, 0.0)


def _compact_pair(w_a, w_b):
    return jnp.concatenate([_compact_weight(w_a), _compact_weight(w_b)], axis=-1).astype(BF16)


def _expand_grad(g):
    kept = jnp.where(_own_half(), g, 0.0)
    return kept.reshape(SSM_CHUNKS, CHUNK_IN, 2, SSM_STATE).sum(axis=2).reshape(N_SSM_GROUPS, SSM_GROUP, SSM_STATE)


TILES_PER_BLOCK = LANES // CH_PER_TILE
IN_BLOCKS = CHUNK_IN // LANES


def _tile_masks():
    j = lax.broadcasted_iota(jnp.int32, (CHUNK_TILES, LANES), 0) % TILES_PER_BLOCK
    lane = lax.broadcasted_iota(jnp.int32, (CHUNK_TILES, LANES), 1) // CH_PER_TILE
    return (j == lane).astype(F32)


def _tile_rows(ref, j, tt):
    return ref.at[j // TILES_PER_BLOCK, pl.ds(j, tt, stride=CHUNK_TILES), :]


def _spread(ref, v, masks):
    tt = v.shape[0]
    for j in range(CHUNK_TILES):
        block = LANES * (j // TILES_PER_BLOCK)
        _tile_rows(ref, j, tt)[...] = v[:, block:block + LANES] * masks[j:j + 1, :]
    return jnp.concatenate([ref[b] for b in range(IN_BLOCKS)], axis=1).astype(BF16)


def _gather(ref, full, masks):
    tt = full.shape[0] // CHUNK_TILES
    for b in range(IN_BLOCKS):
        ref[b] = full[:, b * LANES:(b + 1) * LANES]
    out = []
    for b in range(IN_BLOCKS):
        acc = None
        for j in range(b * TILES_PER_BLOCK, (b + 1) * TILES_PER_BLOCK):
            part = _tile_rows(ref, j, tt)[...] * masks[j:j + 1, :]
            acc = part if acc is None else acc + part
        out.append(acc)
    return jnp.concatenate(out, axis=1)


def _resident(shape):
    return pl.BlockSpec(shape, lambda i: (0,) * len(shape), pipeline_mode=pl.Buffered(1))


def _halves(ref, k, rows=slice(None)):
    return ref[k, rows, :LANES], ref[k, rows, LANES:]


def _ssm_fwd(proj, w2, c2, a2, d_skip):
    t = proj.shape[0]
    tt = _tile(t, SSM_ROWS)
    rows = tt * CHUNK_TILES

    def body(u_ref, w_ref, c_ref, a_ref, d_ref, y_ref, gel_ref, s_ref, carry, spread_ref, full_ref):
        @pl.when(pl.program_id(0) == 0)
        def _():
            carry[...] = jnp.zeros_like(carry)
            spread_ref[...] = jnp.zeros_like(spread_ref)

        mask = _tile_masks()
        u = u_ref[...]
        for k in range(SSM_CHUNKS):
            uk = _spread(spread_ref, u[:, k * CHUNK_IN:(k + 1) * CHUNK_IN], mask)
            s_ref[k] = jnp.dot(uk, w_ref[k], preferred_element_type=F32)

        abar = [_halves(a_ref, k) for k in range(SSM_CHUNKS)]

        def steps(i, state):
            for v in range(SCAN_STEPS):
                r = pl.ds(pl.multiple_of((i * SCAN_STEPS + v) * CHUNK_TILES, CHUNK_TILES), CHUNK_TILES)
                new = []
                for k, ((a_re, a_im), (s_re, s_im)) in enumerate(zip(abar, state)):
                    b_re, b_im = _halves(s_ref, k, r)
                    s_re, s_im = a_re * s_re - a_im * s_im + b_re, a_re * s_im + a_im * s_re + b_im
                    s_ref[k, r, :LANES] = s_re
                    s_ref[k, r, LANES:] = s_im
                    new.append((s_re, s_im))
                state = tuple(new)
            return state

        state = lax.fori_loop(0, tt // SCAN_STEPS, steps, tuple(_halves(carry, k) for k in range(SSM_CHUNKS)))
        for k, (s_re, s_im) in enumerate(state):
            carry[k, :, :LANES] = s_re
            carry[k, :, LANES:] = s_im

        for k in range(SSM_CHUNKS):
            cols = slice(k * CHUNK_IN, (k + 1) * CHUNK_IN)
            full = lax.dot_general(s_ref[k].astype(BF16), c_ref[k], DOT_NT, preferred_element_type=F32)
            y = _gather(full_ref, full, mask) + d_ref[:, cols] * u[:, cols]
            y_ref[:, cols] = y
            gel_ref[:, cols] = (0.5 * y * (1.0 + jnp.tanh(GELU_C * (y + GELU_A * y * y * y)))).astype(BF16)

    weight = _resident((SSM_CHUNKS, CHUNK_IN, PAIR))
    tokens = pl.BlockSpec((tt, SSM_WIDTH), lambda i: (i, 0))
    return pl.pallas_call(
        body, name="ssm_fwd", grid=(t // tt,),
        in_specs=[pl.BlockSpec((tt, SSM_WIDTH), lambda i: (i, U_COLUMN_BLOCK)), weight, weight,
                  _resident((SSM_CHUNKS, CHUNK_TILES, PAIR)), _resident((1, SSM_WIDTH))],
        out_specs=[tokens, tokens, pl.BlockSpec((SSM_CHUNKS, rows, PAIR), lambda i: (0, i, 0))],
        out_shape=[jax.ShapeDtypeStruct((t, SSM_WIDTH), F32), jax.ShapeDtypeStruct((t, SSM_WIDTH), BF16),
                   jax.ShapeDtypeStruct((SSM_CHUNKS, t * CHUNK_TILES, PAIR), F32)],
        scratch_shapes=[pltpu.VMEM((SSM_CHUNKS, CHUNK_TILES, PAIR), F32), pltpu.VMEM((IN_BLOCKS, rows, LANES), F32),
                        pltpu.VMEM((IN_BLOCKS, rows, LANES), F32)],
        compiler_params=_params(("arbitrary",)),
    )(proj, w2, c2, a2, d_skip)


def _ssm_bwd(dy, proj, s, w2, c2, a2, d_skip, dproj):
    t = dy.shape[0]
    tt = _tile(t, SSM_ROWS)
    rows = tt * CHUNK_TILES
    n_chunks = t // tt

    def body(dy_ref, u_ref, s_ref, w_ref, c_ref, a_ref, d_ref, _, du_ref, gc_ref, gw_ref, ga_ref, gd_ref, z_ref, carry,
             spread_ref, full_ref):
        @pl.when(pl.program_id(0) == 0)
        def _():
            for r in (carry, gc_ref, gw_ref, ga_ref, gd_ref, spread_ref):
                r[...] = jnp.zeros_like(r)

        mask = _tile_masks()
        dy_v = dy_ref[...]
        u = u_ref[...]
        gd_ref[...] += jnp.sum(dy_v * u, axis=0, keepdims=True)
        for k in range(SSM_CHUNKS):
            dk = _spread(spread_ref, dy_v[:, k * CHUNK_IN:(k + 1) * CHUNK_IN], mask)
            z_ref[k] = jnp.dot(dk, c_ref[k], preferred_element_type=F32)
            gc_ref[k] += lax.dot_general(dk, s_ref[k].astype(BF16), DOT_TN, preferred_element_type=F32)

        abar = [_halves(a_ref, k) for k in range(SSM_CHUNKS)]

        def steps(i, state):
            zs, gs = state
            for v in range(SCAN_STEPS):
                tok = tt - 1 - (i * SCAN_STEPS + v)
                r = pl.ds(pl.multiple_of(tok * CHUNK_TILES, CHUNK_TILES), CHUNK_TILES)
                new_z, new_g = [], []
                for k, ((a_re, a_im), (z_re, z_im), (g_re, g_im)) in enumerate(zip(abar, zs, gs)):
                    s_re, s_im = _halves(s_ref, k, r)
                    g_re = g_re + z_re * s_re + z_im * s_im
                    g_im = g_im + z_im * s_re - z_re * s_im
                    d_re, d_im = _halves(z_ref, k, r)
                    z_re, z_im = d_re + a_re * z_re + a_im * z_im, d_im + a_re * z_im - a_im * z_re
                    z_ref[k, r, :LANES] = z_re
                    z_ref[k, r, LANES:] = z_im
                    new_z.append((z_re, z_im))
                    new_g.append((g_re, g_im))
                zs, gs = tuple(new_z), tuple(new_g)
            return zs, gs

        zs, gs = lax.fori_loop(0, tt // SCAN_STEPS, steps,
                               (tuple(_halves(carry, k) for k in range(SSM_CHUNKS)),
                                tuple(_halves(ga_ref, k) for k in range(SSM_CHUNKS))))
        for k in range(SSM_CHUNKS):
            carry[k, :, :LANES], carry[k, :, LANES:] = zs[k]
            ga_ref[k, :, :LANES], ga_ref[k, :, LANES:] = gs[k]

        for k in range(SSM_CHUNKS):
            cols = slice(k * CHUNK_IN, (k + 1) * CHUNK_IN)
            zb = z_ref[k].astype(BF16)
            full = lax.dot_general(zb, w_ref[k], DOT_NT, preferred_element_type=F32)
            du_ref[:, cols] = (_gather(full_ref, full, mask) + d_ref[:, cols] * dy_v[:, cols]).astype(BF16)
            uk = _spread(spread_ref, u[:, cols], mask)
            gw_ref[k] += lax.dot_general(uk, zb, DOT_TN, preferred_element_type=F32)

    weight = _resident((SSM_CHUNKS, CHUNK_IN, PAIR))
    tokens = pl.BlockSpec((tt, SSM_WIDTH), lambda i: (n_chunks - 1 - i, 0))
    grad = pl.BlockSpec((SSM_CHUNKS, CHUNK_IN, PAIR), lambda i: (0, 0, 0))
    return pl.pallas_call(
        body, name="ssm_bwd", grid=(n_chunks,),
        in_specs=[tokens, pl.BlockSpec((tt, SSM_WIDTH), lambda i: (n_chunks - 1 - i, U_COLUMN_BLOCK)),
                  pl.BlockSpec((SSM_CHUNKS, rows, PAIR), lambda i: (0, n_chunks - 1 - i, 0)), weight, weight,
                  _resident((SSM_CHUNKS, CHUNK_TILES, PAIR)), _resident((1, SSM_WIDTH)),
                  pl.BlockSpec(memory_space=pl.ANY)],
        out_specs=[pl.BlockSpec((tt, SSM_WIDTH), lambda i: (n_chunks - 1 - i, U_COLUMN_BLOCK)), grad, grad,
                   pl.BlockSpec((SSM_CHUNKS, CHUNK_TILES, PAIR), lambda i: (0, 0, 0)),
                   pl.BlockSpec((1, SSM_WIDTH), lambda i: (0, 0))],
        out_shape=[jax.ShapeDtypeStruct(dproj.shape, dproj.dtype), jax.ShapeDtypeStruct((SSM_CHUNKS, CHUNK_IN, PAIR), F32),
                   jax.ShapeDtypeStruct((SSM_CHUNKS, CHUNK_IN, PAIR), F32),
                   jax.ShapeDtypeStruct((SSM_CHUNKS, CHUNK_TILES, PAIR), F32), jax.ShapeDtypeStruct((1, SSM_WIDTH), F32)],
        input_output_aliases={7: 0},
        scratch_shapes=[pltpu.VMEM((SSM_CHUNKS, rows, PAIR), F32), pltpu.VMEM((SSM_CHUNKS, CHUNK_TILES, PAIR), F32),
                        pltpu.VMEM((IN_BLOCKS, rows, LANES), F32), pltpu.VMEM((IN_BLOCKS, rows, LANES), F32)],
        compiler_params=_params(("arbitrary",)),
    )(dy, proj, s, w2, c2, a2, d_skip, dproj)


def _block(ref, axis, size, index):
    idx = [slice(None)] * len(ref.shape)
    idx[axis] = pl.ds(pl.multiple_of(index * size, size), size)
    return ref.at[tuple(idx)]


HBM_SPEC = pl.BlockSpec(memory_space=pltpu.HBM)
SEM_SPEC = pl.BlockSpec(memory_space=pltpu.SEMAPHORE)
ANY_SPEC = pl.BlockSpec(memory_space=pl.ANY)
SPLIT_PARAMS = pltpu.CompilerParams(has_side_effects=pltpu.SideEffectType.DATAFLOW_SIDE_EFFECTING)
N_PEERS = N_DEV - 1
TOKEN = jax.ShapeDtypeStruct((SUBLANES, LANES), F32)
VMEM_SPEC = pl.BlockSpec(memory_space=pltpu.VMEM)


def _in_hbm(arrays):
    return [pltpu.with_memory_space_constraint(a, pltpu.HBM) for a in arrays]


def _peer(m):
    x, y, c = (lax.axis_index(a) for a in MESH_AXES)
    px = 1 - x if m & 4 else x
    py = 1 - y if m & 2 else y
    pc = 1 - c if m & 1 else c
    return (px, py, pc), 4 * px + 2 * py + pc


def _my_index():
    x, y, c = (lax.axis_index(a) for a in MESH_AXES)
    return 4 * x + 2 * y + c


def _gather_copies(shard_refs, full_refs, axes, send_sems, recv_sems):
    copies = []
    for i, (shard, full) in enumerate(zip(shard_refs, full_refs)):
        mine = _block(full, axes[i], shard.shape[axes[i]], _my_index())
        for m in range(1, N_DEV):
            peer, _ = _peer(m)
            copies.append(pltpu.make_async_remote_copy(
                src_ref=shard, dst_ref=mine, send_sem=send_sems.at[N_PEERS * i + m - 1],
                recv_sem=recv_sems.at[N_PEERS * i + m - 1], device_id=peer, device_id_type=MESH))
    return copies


def _gather_start(name, shards, axes, after):
    n = len(shards)

    def body(*refs):
        shard_refs = refs[:n]
        send_sems, recv_sems, local_sems = refs[n + 1:n + 4]
        full_refs = refs[2 * n + 4:3 * n + 4]
        refs[3 * n + 4][...] = jnp.zeros(TOKEN.shape, TOKEN.dtype)
        for i in range(n):
            pltpu.make_async_copy(shard_refs[i], _block(full_refs[i], axes[i], shard_refs[i].shape[axes[i]], _my_index()),
                                  local_sems.at[i]).start()
        for cp in _gather_copies(shard_refs, full_refs, axes, send_sems, recv_sems):
            cp.start()

    fulls = []
    for s, a in zip(shards, axes):
        shape = list(s.shape)
        shape[a] *= N_DEV
        fulls.append(pltpu.HBM(tuple(shape), s.dtype))
    out = pl.pallas_call(
        body, name=name,
        out_shape=(pltpu.SemaphoreType.DMA((N_PEERS * n,)), pltpu.SemaphoreType.DMA((N_PEERS * n,)),
                   pltpu.SemaphoreType.DMA((n,)), *[pltpu.HBM(s.shape, s.dtype) for s in shards], *fulls, TOKEN),
        in_specs=[HBM_SPEC] * n + [ANY_SPEC],
        out_specs=(SEM_SPEC, SEM_SPEC, SEM_SPEC, *[HBM_SPEC] * (2 * n), VMEM_SPEC),
        input_output_aliases={i: 3 + i for i in range(n)},
        compiler_params=SPLIT_PARAMS,
    )(*_in_hbm(shards), after)
    return out[:-1], out[-1]


def _gather_wait(name, started, indices, axes, after):
    send_sems, recv_sems, local_sems = started[:3]
    n_all = (len(started) - 3) // 2
    shards = [started[3 + i] for i in indices]
    fulls = [started[3 + n_all + i] for i in indices]
    n = len(indices)

    def body(*refs):
        shard_refs, full_refs = refs[:n], refs[n:2 * n]
        send_sems, recv_sems, local_sems = refs[2 * n:2 * n + 3]
        for j, i in enumerate(indices):
            mine = _block(full_refs[j], axes[j], shard_refs[j].shape[axes[j]], _my_index())
            pltpu.make_async_copy(shard_refs[j], mine, local_sems.at[i]).wait()
            for m in range(1, N_DEV):
                peer, _ = _peer(m)
                cp = pltpu.make_async_remote_copy(
                    src_ref=shard_refs[j], dst_ref=mine, send_sem=send_sems.at[N_PEERS * i + m - 1],
                    recv_sem=recv_sems.at[N_PEERS * i + m - 1], device_id=peer, device_id_type=MESH)
                cp.wait_send()
                cp.wait_recv()

    out = pl.pallas_call(
        body, name=name,
        out_shape=tuple(pltpu.HBM(a.shape, a.dtype) for a in shards + fulls),
        in_specs=[HBM_SPEC] * (2 * n) + [SEM_SPEC] * 3 + [ANY_SPEC], out_specs=tuple([HBM_SPEC] * (2 * n)),
        input_output_aliases={i: i for i in range(2 * n)},
        compiler_params=SPLIT_PARAMS,
    )(*shards, *fulls, send_sems, recv_sems, local_sems, after)
    return out[n:]


def _exchange_start(name, fulls, axes):
    n = len(fulls)
    sizes = [f.shape[a] // N_DEV for f, a in zip(fulls, axes)]

    def body(*refs):
        ins = refs[:n]
        send_sems, recv_sems = refs[n:n + 2]
        lands = refs[2 * n + 2:3 * n + 2]
        refs[3 * n + 2][...] = jnp.zeros(TOKEN.shape, TOKEN.dtype)
        for i in range(n):
            for m in range(1, N_DEV):
                peer, index = _peer(m)
                pltpu.make_async_remote_copy(
                    src_ref=_block(ins[i], axes[i], sizes[i], index), dst_ref=lands[i].at[m - 1],
                    send_sem=send_sems.at[N_PEERS * i + m - 1], recv_sem=recv_sems.at[N_PEERS * i + m - 1],
                    device_id=peer, device_id_type=MESH).start()

    lands = []
    for f, a, size in zip(fulls, axes, sizes):
        shape = list(f.shape)
        shape[a] = size
        lands.append(pltpu.HBM((N_PEERS, *shape), f.dtype))
    out = pl.pallas_call(
        body, name=name,
        out_shape=(pltpu.SemaphoreType.DMA((N_PEERS * n,)), pltpu.SemaphoreType.DMA((N_PEERS * n,)),
                   *[pltpu.HBM(f.shape, f.dtype) for f in fulls], *lands, TOKEN),
        in_specs=[HBM_SPEC] * n, out_specs=(SEM_SPEC, SEM_SPEC, *[HBM_SPEC] * (2 * n), VMEM_SPEC),
        input_output_aliases={i: 2 + i for i in range(n)},
        compiler_params=SPLIT_PARAMS,
    )(*_in_hbm(fulls))
    return out[:-1], out[-1]


def _exchange_wait(name, started, axes, after):
    send_sems, recv_sems = started[:2]
    n = (len(started) - 2) // 2
    fulls, lands = list(started[2:2 + n]), list(started[2 + n:])
    sizes = [f.shape[a] // N_DEV for f, a in zip(fulls, axes)]

    def body(*refs):
        ins, land_refs = refs[:n], refs[n:2 * n]
        send_sems, recv_sems = refs[2 * n:2 * n + 2]
        for i in range(n):
            for m in range(1, N_DEV):
                peer, index = _peer(m)
                cp = pltpu.make_async_remote_copy(
                    src_ref=_block(ins[i], axes[i], sizes[i], index), dst_ref=land_refs[i].at[m - 1],
                    send_sem=send_sems.at[N_PEERS * i + m - 1], recv_sem=recv_sems.at[N_PEERS * i + m - 1],
                    device_id=peer, device_id_type=MESH)
                cp.wait_send()
                cp.wait_recv()

    out = pl.pallas_call(
        body, name=name,
        out_shape=tuple(pltpu.HBM(a.shape, a.dtype) for a in fulls + lands),
        in_specs=[HBM_SPEC] * (2 * n) + [SEM_SPEC] * 2 + [ANY_SPEC], out_specs=tuple([HBM_SPEC] * (2 * n)),
        input_output_aliases={i: i for i in range(2 * n)},
        compiler_params=SPLIT_PARAMS,
    )(*fulls, *lands, send_sems, recv_sems, after)
    return out[:n], out[n:]


def _sum_parts(part_refs, ndim):
    g = None
    for p_ref in part_refs:
        stacked = len(p_ref.shape) > ndim
        terms = [p_ref[s] for s in range(p_ref.shape[0])] if stacked else [p_ref[...]]
        for term in terms:
            term = term.astype(F32)
            g = term if g is None else g + term
    return g


def _adamw_update(w_ref, m_ref, v_ref, g, g_ref, d_ref, nm_ref, nv_ref):
    c1 = 1.0 - ADAM_B1 ** ADAM_STEP
    c2 = 1.0 - ADAM_B2 ** ADAM_STEP
    new_m = ADAM_B1 * m_ref[...] + (1.0 - ADAM_B1) * g
    new_v = ADAM_B2 * v_ref[...] + (1.0 - ADAM_B2) * (g * g)
    g_ref[...] = g
    nm_ref[...] = new_m
    nv_ref[...] = new_v
    d_ref[...] = -ADAM_LR * ((new_m / c1) / (jnp.sqrt(new_v / c2) + ADAM_EPS) + ADAM_WD * w_ref[...])


def _adamw_small(ws, ms, vs, stacks, loss_stack):
    n = len(ws)

    def body(*refs):
        ins, outs = refs[:4 * n + 1], refs[4 * n + 1:]
        for i in range(n):
            _adamw_update(ins[i], ins[n + i], ins[2 * n + i], _sum_parts([ins[3 * n + i]], len(ins[i].shape)),
                          outs[i], outs[n + i], outs[2 * n + i], outs[3 * n + i])
        total = ins[4 * n][0]
        for dev in range(1, N_DEV):
            total = total + ins[4 * n][dev]
        outs[4 * n][...] = total

    res = pl.pallas_call(
        body, name="adamw_small",
        out_shape=[jax.ShapeDtypeStruct(w.shape, F32) for w in ws] * 4 + [jax.ShapeDtypeStruct((1, LANES), F32)],
        compiler_params=_params(None),
    )(*ws, *ms, *vs, *stacks, loss_stack)
    return res[:n], res[n:2 * n], res[2 * n:3 * n], res[3 * n:4 * n], res[4 * n]


def _adamw(name, w, m, v, parts):
    r, c = w.shape
    tr = _tile(r, 256)
    n_parts = len(parts)

    def body(*refs):
        _adamw_update(refs[0], refs[1], refs[2], _sum_parts(refs[3:3 + n_parts], 2), *refs[3 + n_parts:])

    row = pl.BlockSpec((tr, c), lambda i: (i, 0))
    in_specs = [row, row, row]
    for p in parts:
        in_specs.append(row if p.ndim == 2 else pl.BlockSpec((p.shape[0], tr, c), lambda i: (0, i, 0)))
    return pl.pallas_call(
        body, name=name, grid=(r // tr,), in_specs=in_specs, out_specs=[row] * 4,
        out_shape=[jax.ShapeDtypeStruct((r, c), F32)] * 4,
        compiler_params=_params(("arbitrary",)),
    )(w, m, v, *parts)


SMALL = ("norm_gain", "pool_scale", "a_re", "a_im", "log_dt", "b_re", "b_im", "c_re", "c_im", "d_skip", "final_gain")
LARGE = ("w_in", "w_pool", "w_glu", "w_out", "w_ple", "w_ple_gate")
LARGE_AXIS = {"w_in": 1, "w_pool": 1, "w_glu": 1, "w_out": 0, "w_ple": 1, "w_ple_gate": 0}
WEIGHTS = ("norm_gain", "w_in", "w_pool", "pool_scale", "a_re", "a_im", "log_dt", "b_re", "b_im", "c_re", "c_im",
           "d_skip", "w_glu", "w_out", "w_ple", "w_ple_gate", "final_gain")


def kernel(x, p, norm_gain, w_in, w_pool, pool_scale, a_re, a_im, log_dt, b_re, b_im, c_re, c_im, d_skip, w_glu, w_out, w_ple, w_ple_gate, final_gain, loss_target, m_norm_gain, m_w_in, m_w_pool, m_pool_scale, m_a_re, m_a_im, m_log_dt, m_b_re, m_b_im, m_c_re, m_c_im, m_d_skip, m_w_glu, m_w_out, m_w_ple, m_w_ple_gate, m_final_gain, v_norm_gain, v_w_in, v_w_pool, v_pool_scale, v_a_re, v_a_im, v_log_dt, v_b_re, v_b_im, v_c_re, v_c_im, v_d_skip, v_w_glu, v_w_out, v_w_ple, v_w_ple_gate, v_final_gain):
    weights = dict(norm_gain=norm_gain, w_in=w_in, w_pool=w_pool, pool_scale=pool_scale, a_re=a_re, a_im=a_im,
                   log_dt=log_dt, b_re=b_re, b_im=b_im, c_re=c_re, c_im=c_im, d_skip=d_skip, w_glu=w_glu,
                   w_out=w_out, w_ple=w_ple, w_ple_gate=w_ple_gate, final_gain=final_gain)
    mom_m = dict(norm_gain=m_norm_gain, w_in=m_w_in, w_pool=m_w_pool, pool_scale=m_pool_scale, a_re=m_a_re,
                 a_im=m_a_im, log_dt=m_log_dt, b_re=m_b_re, b_im=m_b_im, c_re=m_c_re, c_im=m_c_im,
                 d_skip=m_d_skip, w_glu=m_w_glu, w_out=m_w_out, w_ple=m_w_ple, w_ple_gate=m_w_ple_gate,
                 final_gain=m_final_gain)
    mom_v = dict(norm_gain=v_norm_gain, w_in=v_w_in, w_pool=v_w_pool, pool_scale=v_pool_scale, a_re=v_a_re,
                 a_im=v_a_im, log_dt=v_log_dt, b_re=v_b_re, b_im=v_b_im, c_re=v_c_re, c_im=v_c_im,
                 d_skip=v_d_skip, w_glu=v_w_glu, w_out=v_w_out, w_ple=v_w_ple, w_ple_gate=v_w_ple_gate,
                 final_gain=v_final_gain)

    t = x.shape[1]
    xs = x.reshape(t, D_MODEL)
    ps = p.reshape(t, PLE_DIM)
    target = loss_target.reshape(t, D_MODEL)
    gain1 = norm_gain.reshape(1, D_MODEL)
    gain_f = final_gain.reshape(1, D_MODEL)
    scale_p = pool_scale.reshape(1, POOL_WIDTH)
    skip = d_skip.reshape(1, SSM_WIDTH)

    shard2d = {k: weights[k][0] for k in LARGE}
    shard_bf = {k: shard2d[k].astype(BF16) for k in LARGE}
    me = 4 * lax.axis_index("x") + 2 * lax.axis_index("y") + lax.axis_index("c")
    hn = _norm1_fwd(xs, gain1)
    proj, w_in_full = _gather_in_proj(hn, shard_bf["w_in"], me)
    full = {"w_in": w_in_full}
    later = [k for k in LARGE if k != "w_in"]
    later_axes = [LARGE_AXIS[k] for k in later]
    gather, gather_token = _gather_start("weights_gather_start", [shard_bf[k] for k in later], later_axes,
                                         full["w_in"])

    def arrive(k, after):
        i = later.index(k)
        full[k] = _gather_wait("gather_wait_" + k, gather, [i], [later_axes[i]], after)[0]

    ar, ai = a_re[0], a_im[0]
    ldt = log_dt.reshape(N_SSM_GROUPS, 1)
    br_t = jnp.transpose(b_re[0], (0, 2, 1))
    bi_t = jnp.transpose(b_im[0], (0, 2, 1))
    ab_re, ab_im, bb_re, bb_im = _ssm_params(ar, ai, ldt, br_t, bi_t)
    tiles = (SSM_CHUNKS, CHUNK_TILES, LANES)
    abar = jnp.concatenate([ab_re.reshape(tiles), ab_im.reshape(tiles)], axis=-1)
    w_pair = _compact_pair(bb_re, bb_im)
    c_pair = _compact_pair(c_re[0], -c_im[0])

    arrive("w_pool", proj)
    pooled, mixed = _pool_mix(proj, full["w_pool"], gather_token)
    y, gel, states = _ssm_fwd(proj, w_pair, c_pair, abar, skip)
    arrive("w_glu", gel)
    hg = _mm_nn("glu_proj", gel, full["w_glu"], [F32])[0]
    arrive("w_out", hg)
    cat, h1, h1b = _gate_out_proj(mixed, proj, hg, scale_p, full["w_out"], xs)
    arrive("w_ple", h1b)
    arrive("w_ple_gate", h1b)
    de, dq, dh2, g_final_gain, loss_part = _ple_final(h1, h1b, ps, full["w_ple_gate"], full["w_ple"], target, gain_f)

    grads = {}
    grads["w_ple_gate"] = _mm_tn("ple_gate_wgrad", h1b, dq, BF16)
    grads["w_ple"] = _mm_tn("ple_wgrad", ps, de, BF16)
    sent, tokens = {}, {}

    def send(names):
        sent[names], tokens[names[0]] = _exchange_start(
            "grads_start_" + names[0], [grads[k] for k in names], [LARGE_AXIS[k] for k in names])

    send(("w_ple_gate", "w_ple"))
    dh1, dh1b = _residual_dgrad("ple_gate_dgrad", dq, full["w_ple_gate"], dh2)
    grads["w_out"] = _mm_tn("out_wgrad", cat, dh1b, BF16)
    send(("w_out",))
    dmixed, dproj, dhg, g_pool_scale = _out_dgrad_gate_bwd(
        dh1b, full["w_out"], mixed, proj, hg, scale_p, [tokens["w_ple_gate"], tokens["w_out"]])

    tk = _tile(t, 1024)
    grads["w_pool"] = _mm("pool_wgrad", [(pooled, (tk, POOL_GROUP), lambda i, j, s: (s, i),
                                          dmixed, (tk, POOL_GROUP), lambda i, j, s: (s, i))],
                          DOT_TN, (N_POOL_GROUPS, 1, t // tk),
                          [((N_POOL_GROUPS, POOL_GROUP, POOL_GROUP), BF16, (None, POOL_GROUP, POOL_GROUP),
                            lambda i, j, s: (i, 0, 0))], t // tk)[0]
    dproj = _pool_mix_bwd(dmixed, full["w_pool"], dproj)

    grads["w_glu"] = _mm_tn("glu_wgrad", gel, dhg, BF16)
    send(("w_pool", "w_glu"))

    def gelu_bwd_epilogue(acc, ex, out_refs):
        yv = ex[0][...]
        th = jnp.tanh(GELU_C * (yv + GELU_A * yv * yv * yv))
        dgelu = 0.5 * (1.0 + th) + 0.5 * yv * (1.0 - th * th) * GELU_C * (1.0 + 3.0 * GELU_A * yv * yv)
        out_refs[0][...] = acc * dgelu

    dy = _mm_nt("glu_dgrad", dhg, full["w_glu"], [F32], tk=2048, extras=[y], epilogue=gelu_bwd_epilogue,
                after=[tokens["w_pool"]])[0]
    dproj, g_c_pair, g_w_pair, g_abar, g_d_skip = _ssm_bwd(dy, proj, states, w_pair, c_pair, abar, skip, dproj)

    g_ab_re = g_abar[..., :LANES].reshape(N_SSM_GROUPS, SSM_STATE)
    g_ab_im = g_abar[..., LANES:].reshape(N_SSM_GROUPS, SSM_STATE)
    d_ar, d_ai, d_ldt, d_br_t, d_bi_t = _ssm_params_bwd(
        ar, ai, ldt, br_t, bi_t, g_ab_re, g_ab_im,
        _expand_grad(g_w_pair[..., :LANES]), _expand_grad(g_w_pair[..., LANES:]))

    small_grads = dict(
        pool_scale=g_pool_scale, a_re=d_ar, a_im=d_ai, log_dt=d_ldt.reshape(1, N_SSM_GROUPS),
        b_re=d_br_t.astype(BF16), b_im=d_bi_t.astype(BF16), c_re=_expand_grad(g_c_pair[..., :LANES]).astype(BF16),
        c_im=(-_expand_grad(g_c_pair[..., LANES:])).astype(BF16), d_skip=g_d_skip, final_gain=g_final_gain)
    early = [k for k in SMALL if k != "norm_gain"]
    early_sent, early_token = _gather_start(
        "small_grads_start", [small_grads[k][None] for k in early] + [jnp.broadcast_to(loss_part, (1, 1, LANES))],
        [0] * (len(early) + 1), d_ar)

    grads["w_in"] = _mm_tn("in_wgrad", hn, dproj, BF16, after=[early_token])
    send(("w_in",))
    grad_x, g_norm_gain = _in_dgrad_norm1_bwd(dproj, full["w_in"], xs, dh1, gain1, tokens["w_in"])
    late_sent, late_token = _gather_start("norm_gain_grad_start", [g_norm_gain[None]], [0], g_norm_gain)

    out_g, out_d, out_m, out_v = ({} for _ in range(4))
    after = late_token
    for names, started in sent.items():
        axes = [LARGE_AXIS[k] for k in names]
        partials, landed = _exchange_wait("grads_wait_" + names[0], started, axes, after)
        for k, axis, partial, land in zip(names, axes, partials, landed):
            shard_shape = shard2d[k].shape
            size = shard_shape[axis]
            own = lax.dynamic_slice_in_dim(partial, me * size, size, axis=axis)
            view = (-1, shard_shape[-1])
            rows = math.prod(shard_shape[:-1])
            res = _adamw("adamw_" + k, shard2d[k].reshape(view), mom_m[k][0].reshape(view), mom_v[k][0].reshape(view),
                         [own.reshape(view), land.reshape(N_PEERS, rows, shard_shape[-1])])
            out_g[k], out_d[k], out_m[k], out_v[k] = (r.reshape(weights[k].shape) for r in res)
            after = res[0]

    def b_view(a):
        return jnp.transpose(a[0], (0, 2, 1))

    views = dict(norm_gain=lambda a: a, pool_scale=lambda a: a, a_re=lambda a: a[0], a_im=lambda a: a[0],
                 log_dt=lambda a: a, b_re=b_view, b_im=b_view, c_re=lambda a: a[0], c_im=lambda a: a[0],
                 d_skip=lambda a: a, final_gain=lambda a: a.reshape(1, D_MODEL))
    landed = _gather_wait("small_grads_wait", early_sent, list(range(len(early) + 1)), [0] * (len(early) + 1), after)
    stack = dict(zip(early, landed))
    stack["norm_gain"] = _gather_wait("norm_gain_grad_wait", late_sent, [0], [0], after)[0]
    *small_out, loss_row = _adamw_small(
        [views[k](weights[k]) for k in SMALL], [views[k](mom_m[k]) for k in SMALL],
        [views[k](mom_v[k]) for k in SMALL], [stack[k] for k in SMALL], landed[-1])
    loss = loss_row[0, 0]
    for out, res in zip((out_g, out_d, out_m, out_v), small_out):
        for k, r in zip(SMALL, res):
            if k in ("b_re", "b_im"):
                r = jnp.transpose(r, (0, 2, 1))
            out[k] = r.reshape(weights[k].shape)

    return (loss, grad_x.reshape(x.shape), *[out_g[k] for k in WEIGHTS], *[out_d[k] for k in WEIGHTS],
            *[out_m[k] for k in WEIGHTS], *[out_v[k] for k in WEIGHTS])
```

```python
import math

import jax
import jax.numpy as jnp
from jax import lax
from jax.experimental import pallas as pl
from jax.experimental.pallas import tpu as pltpu

F32 = jnp.float32
BF16 = jnp.bfloat16
MESH = pl.DeviceIdType.MESH
MESH_AXES = ("x", "y", "c")
N_DEV = 8

D_MODEL = 2048
POOL_WIDTH = 1024
SSM_WIDTH = 1024
N_POOL_GROUPS = 4
POOL_GROUP = 256
SSM_GROUP = 16
N_SSM_GROUPS = 64
SSM_STATE = 64
SSM_FLAT = N_SSM_GROUPS * SSM_STATE
SSM_CHUNKS = 4
CHUNK_IN = SSM_WIDTH // SSM_CHUNKS
CHUNK_STATE = SSM_FLAT // SSM_CHUNKS
PLE_DIM = 256
EPS = 1e-6
A_RE_MAX = -1e-4
ADAM_LR = 0.001
ADAM_B1 = 0.9
ADAM_B2 = 0.999
ADAM_EPS = 1e-08
ADAM_WD = 0.01
ADAM_STEP = 10
GELU_C = math.sqrt(2.0 / math.pi)
GELU_A = 0.044715

SUBLANES = 8
LANES = 128
VMEM_LIMIT_BYTES = 48 * 1024 * 1024

DOT_NN = (((1,), (0,)), ((), ()))
DOT_NT = (((1,), (1,)), ((), ()))
DOT_TN = (((0,), (0,)), ((), ()))


def _tile(n, pref):
    return pref if n % pref == 0 else n


def _params(sem):
    return pltpu.CompilerParams(dimension_semantics=sem, vmem_limit_bytes=VMEM_LIMIT_BYTES)


def _sigmoid(v):
    return 1.0 / (1.0 + jnp.exp(-v))


def _silu_and_grad(v):
    s = _sigmoid(v)
    return v * s, s * (1.0 + v * (1.0 - s))


def _mm(name, pairs, dims, grid, outs, k_steps, extras=(), epilogue=None):
    n_pairs, n_ex, n_out = len(pairs), len(extras), len(outs)
    acc_shape = tuple(d for d in outs[0][2] if d is not None)
    if epilogue is None:
        def epilogue(acc, ex, out_refs):
            out_refs[0][...] = acc.astype(out_refs[0].dtype)

    def body(*refs):
        ab = refs[:2 * n_pairs]
        ex = refs[2 * n_pairs:2 * n_pairs + n_ex]
        out_refs = refs[2 * n_pairs + n_ex:2 * n_pairs + n_ex + n_out]
        acc = refs[-1]
        k = pl.program_id(2)

        @pl.when(k == 0)
        def _():
            acc[...] = jnp.zeros_like(acc)

        part = None
        for q in range(n_pairs):
            d = lax.dot_general(ab[2 * q][...].astype(BF16), ab[2 * q + 1][...].astype(BF16), dims,
                                preferred_element_type=F32)
            part = d if part is None else part + d
        acc[...] += part

        @pl.when(k == k_steps - 1)
        def _():
            epilogue(acc[...], ex, out_refs)

    in_specs, operands = [], []
    for a, a_blk, a_map, b, b_blk, b_map in pairs:
        in_specs += [pl.BlockSpec(a_blk, a_map), pl.BlockSpec(b_blk, b_map)]
        operands += [a, b]
    for e, e_blk, e_map in extras:
        in_specs.append(pl.BlockSpec(e_blk, e_map))
        operands.append(e)
    return pl.pallas_call(
        body, name=name, grid=grid, in_specs=in_specs,
        out_specs=[pl.BlockSpec(o[2], o[3]) for o in outs],
        out_shape=[jax.ShapeDtypeStruct(o[0], o[1]) for o in outs],
        scratch_shapes=[pltpu.VMEM(acc_shape, F32)],
        compiler_params=_params(("arbitrary", "arbitrary", "arbitrary")),
    )(*operands)


def _after(tokens):
    return [(tok, tok.shape, lambda i, j, s: (0, 0)) for tok in tokens]


def _mm_nn(name, a, b, out_dtypes, tm=1024, tn=1024, tk=1024, a_col0=0, extras=(), epilogue=None, after=()):
    m, n = a.shape[0], b.shape[1]
    k = b.shape[0]
    tm, tn, tk = _tile(m, tm), _tile(n, tn), _tile(k, tk)
    outs = [((m, n), dt, (tm, tn), lambda i, j, s: (i, j)) for dt in out_dtypes]
    ex = [(e, (tm, tn), lambda i, j, s: (i, j)) for e in extras] + _after(after)
    return _mm(name, [(a, (tm, tk), lambda i, j, s: (i, a_col0 + s), b, (tk, tn), lambda i, j, s: (s, j))],
               DOT_NN, (m // tm, n // tn, k // tk), outs, k // tk, ex, epilogue)


def _mm_nt(name, a, b, out_dtypes, tm=1024, tn=1024, tk=1024, extras=(), epilogue=None, after=()):
    m, kk = a.shape
    n = b.shape[0]
    tm, tn, tk = _tile(m, tm), _tile(n, tn), _tile(kk, tk)
    outs = [((m, n), dt, (tm, tn), lambda i, j, s: (i, j)) for dt in out_dtypes]
    ex = [(e, (tm, tn), lambda i, j, s: (i, j)) for e in extras] + _after(after)
    return _mm(name, [(a, (tm, tk), lambda i, j, s: (i, s), b, (tn, tk), lambda i, j, s: (j, s))],
               DOT_NT, (m // tm, n // tn, kk // tk), outs, kk // tk, ex, epilogue)


def _mm_tn(name, a, b, out_dtype, tm=512, tn=2048, tk=1024, after=()):
    m, kk = a.shape
    n = b.shape[1]
    tm, tn, tk = _tile(kk, tm), _tile(n, tn), _tile(m, tk)
    outs = [((kk, n), out_dtype, (tm, tn), lambda i, j, s: (i, j))]
    return _mm(name, [(a, (tk, tm), lambda i, j, s: (s, i), b, (tk, tn), lambda i, j, s: (s, j))],
               DOT_TN, (kk // tm, n // tn, m // tk), outs, m // tk, _after(after))[0]


ROW_TILE = 256


def _norm1_fwd(x, gain):
    t = x.shape[0]
    tm = _tile(t, 512)

    def body(x_ref, g_ref, hn_ref):
        xv = x_ref[...]
        r = lax.rsqrt(jnp.mean(xv * xv, axis=-1, keepdims=True) + EPS)
        hn_ref[...] = (xv * r * g_ref[...]).astype(BF16)

    return pl.pallas_call(
        body, name="norm1_fwd", grid=(t // tm,),
        in_specs=[pl.BlockSpec((tm, D_MODEL), lambda i: (i, 0)), pl.BlockSpec((1, D_MODEL), lambda i: (0, 0))],
        out_specs=pl.BlockSpec((tm, D_MODEL), lambda i: (i, 0)),
        out_shape=jax.ShapeDtypeStruct((t, D_MODEL), BF16),
        compiler_params=_params(("arbitrary",)),
    )(x, gain)


GATHER_ROWS = 1024
SLOT_MASKS = (0, 1, 2, 4, 6, 3, 5, 7)
DIRECT_SLOTS = (1, 2, 3, 4)
PASSED_ON_SLOTS = (2, 3, 4)


def _gather_in_proj(hn, shard, me, after):
    t, d = hn.shape
    width = shard.shape[1]
    tm = _tile(t, GATHER_ROWS)
    n_tiles = t // tm
    order = jnp.stack([jnp.bitwise_xor(me, mask) for mask in SLOT_MASKS]).astype(jnp.int32)

    def body(order_ref, hn_ref, shard_ref, _, proj_ref, full_ref, wbuf, send_sems, recv_sems, local_sems):
        b, i = pl.program_id(0), pl.program_id(1)
        x, y, c = (lax.axis_index(a) for a in MESH_AXES)

        def to_peer(mask, src_slot, dst_slot, k):
            peer = (1 - x if mask & 4 else x, 1 - y if mask & 2 else y, 1 - c if mask & 1 else c)
            return pltpu.make_async_remote_copy(
                src_ref=wbuf.at[src_slot], dst_ref=wbuf.at[dst_slot], send_sem=send_sems.at[k],
                recv_sem=recv_sems.at[dst_slot - 1], device_id=peer, device_id_type=MESH)

        def keep(slot):
            cols = pl.ds(pl.multiple_of(order_ref[slot] * width, width), width)
            return pltpu.make_async_copy(wbuf.at[slot], full_ref.at[:, cols], local_sems.at[slot])

        sends = [to_peer(SLOT_MASKS[slot], 0, slot, k) for k, slot in enumerate(DIRECT_SLOTS)]
        sends += [to_peer(1, slot, slot + len(PASSED_ON_SLOTS), len(DIRECT_SLOTS) + k)
                  for k, slot in enumerate(PASSED_ON_SLOTS)]

        @pl.when((i == 0) & (b == 0))
        def _():
            own = pltpu.make_async_copy(shard_ref, wbuf.at[0], local_sems.at[N_DEV])
            own.start()
            own.wait()
            for cp in sends[:len(DIRECT_SLOTS)]:
                cp.start()
            keep(0).start()

        for slot in range(1, N_DEV):
            @pl.when((i == 0) & (b == slot))
            def _(slot=slot):
                to_peer(1, 0, slot, 0).wait_recv()
                if slot in PASSED_ON_SLOTS:
                    sends[len(DIRECT_SLOTS) + PASSED_ON_SLOTS.index(slot)].start()
                keep(slot).start()

        proj_ref[...] = jnp.dot(hn_ref[...], wbuf[b], preferred_element_type=F32)

        @pl.when((b == N_DEV - 1) & (i == n_tiles - 1))
        def _():
            for cp in sends:
                cp.wait_send()
            for slot in range(N_DEV):
                keep(slot).wait()

    grid_spec = pltpu.PrefetchScalarGridSpec(
        num_scalar_prefetch=1, grid=(N_DEV, n_tiles),
        in_specs=[pl.BlockSpec((tm, d), lambda b, i, order_ref: (i, 0)), pl.BlockSpec(memory_space=pl.ANY),
                  pl.BlockSpec(after.shape, lambda b, i, order_ref: (0, 0))],
        out_specs=[pl.BlockSpec((tm, width), lambda b, i, order_ref: (i, order_ref[b])),
                   pl.BlockSpec(memory_space=pl.ANY)],
        scratch_shapes=[pltpu.VMEM((N_DEV, d, width), BF16), pltpu.SemaphoreType.DMA((N_DEV - 1,)),
                        pltpu.SemaphoreType.DMA((N_DEV - 1,)), pltpu.SemaphoreType.DMA((N_DEV + 1,))])
    return pl.pallas_call(
        body, name="w_in_gather_in_proj", grid_spec=grid_spec,
        out_shape=[jax.ShapeDtypeStruct((t, N_DEV * width), F32), jax.ShapeDtypeStruct((d, N_DEV * width), BF16)],
        compiler_params=_params(("arbitrary", "arbitrary")),
    )(order, hn, shard, after)


def _in_dgrad_norm1_bwd(dproj, w_in, x, dh1, gain, after):
    t = x.shape[0]
    tm = _tile(t, ROW_TILE)

    def body(dp_ref, w_ref, x_ref, dh1_ref, g_ref, _, dx_ref, gg_ref):
        @pl.when(pl.program_id(0) == 0)
        def _():
            gg_ref[...] = jnp.zeros_like(gg_ref)

        dhn = lax.dot_general(dp_ref[...], w_ref[...], DOT_NT, preferred_element_type=F32)
        xv = x_ref[...]
        r = lax.rsqrt(jnp.mean(xv * xv, axis=-1, keepdims=True) + EPS)
        xh = xv * r
        gg_ref[...] += jnp.sum(dhn * xh, axis=0, keepdims=True)
        dxh = dhn * g_ref[...]
        dx_ref[...] = dh1_ref[...] + r * (dxh - xh * jnp.mean(dxh * xh, axis=-1, keepdims=True))

    row = pl.BlockSpec((tm, D_MODEL), lambda i: (i, 0))
    vec = pl.BlockSpec((1, D_MODEL), lambda i: (0, 0))
    return pl.pallas_call(
        body, name="in_dgrad_norm1_bwd", grid=(t // tm,),
        in_specs=[pl.BlockSpec((tm, dproj.shape[1]), lambda i: (i, 0)), _resident(w_in.shape), row, row, vec,
                  pl.BlockSpec(after.shape, lambda i: (0, 0))],
        out_specs=[row, vec],
        out_shape=[jax.ShapeDtypeStruct((t, D_MODEL), F32), jax.ShapeDtypeStruct((1, D_MODEL), F32)],
        compiler_params=_params(("arbitrary",)),
    )(dproj, w_in, x, dh1, gain, after)


def _pool_counts(t, width, group):
    row = lax.broadcasted_iota(jnp.int32, (t, width), 0)
    window = jnp.left_shift(jnp.int32(2), group)
    return row, jnp.minimum(row + 1, window).astype(F32)


def _select_window(group, s2, s4, s8, s16):
    return jnp.where(group == 0, s2, jnp.where(group == 1, s4, jnp.where(group == 2, s8, s16)))


def _pool_mix(proj, w_pool, after):
    t = proj.shape[0]

    def body(u_ref, w_ref, _, pooled_ref, mixed_ref):
        group = pl.program_id(0)
        row, count = _pool_counts(t, LANES, group)

        def down(a, j):
            return jnp.where(row >= j, pltpu.roll(a, j, 0), 0.0)

        for h in range(POOL_GROUP // LANES):
            cols = slice(h * LANES, (h + 1) * LANES)
            v = u_ref[:, cols]
            s2 = v + down(v, 1)
            s4 = s2 + down(s2, 2)
            s8 = s4 + down(s4, 4)
            s16 = s8 + down(s8, 8)
            pooled_ref[:, cols] = (_select_window(group, s2, s4, s8, s16) / count - v).astype(BF16)
        mixed_ref[...] = jnp.dot(pooled_ref[...], w_ref[...], preferred_element_type=F32)

    block = pl.BlockSpec((t, POOL_GROUP), lambda g: (0, g))
    return pl.pallas_call(
        body, name="pool_mix", grid=(N_POOL_GROUPS,),
        in_specs=[block, pl.BlockSpec((None, POOL_GROUP, POOL_GROUP), lambda g: (g, 0, 0)),
                  pl.BlockSpec(after.shape, lambda g: (0, 0))],
        out_specs=[block, block],
        out_shape=[jax.ShapeDtypeStruct((t, POOL_WIDTH), BF16), jax.ShapeDtypeStruct((t, POOL_WIDTH), F32)],
        compiler_params=_params(("arbitrary",)),
    )(proj, w_pool, after)


def _pool_mix_bwd(dmixed, w_pool, dproj):
    t = dmixed.shape[0]

    def body(dm_ref, w_ref, _, o_ref):
        group = pl.program_id(0)
        row, count = _pool_counts(t, LANES, group)

        def up(a, j):
            return jnp.where(row < t - j, pltpu.roll(a, t - j, 0), 0.0)

        dpooled = lax.dot_general(dm_ref[...], w_ref[...], DOT_NT, preferred_element_type=F32)
        for h in range(POOL_GROUP // LANES):
            cols = slice(h * LANES, (h + 1) * LANES)
            dp = dpooled[:, cols]
            r = dp / count
            s2 = r + up(r, 1)
            s4 = s2 + up(s2, 2)
            s8 = s4 + up(s4, 4)
            s16 = s8 + up(s8, 8)
            o_ref[:, cols] = (_select_window(group, s2, s4, s8, s16) - dp).astype(BF16)

    block = pl.BlockSpec((t, POOL_GROUP), lambda g: (0, g))
    return pl.pallas_call(
        body, name="pool_mix_bwd", grid=(N_POOL_GROUPS,),
        in_specs=[block, pl.BlockSpec((None, POOL_GROUP, POOL_GROUP), lambda g: (g, 0, 0)),
                  pl.BlockSpec(memory_space=pl.ANY)],
        out_specs=block,
        out_shape=jax.ShapeDtypeStruct(dproj.shape, dproj.dtype),
        input_output_aliases={2: 0},
        compiler_params=_params(("arbitrary",)),
    )(dmixed, w_pool, dproj)


def _gate_out_proj(mixed, proj, hg, pool_scale, w_out, x):
    t = mixed.shape[0]
    tm = _tile(t, ROW_TILE)

    def body(mx_ref, ga_ref, gb_ref, hg_ref, ps_ref, w_ref, x_ref, cat_ref, h1_ref, h1b_ref):
        silu_a, _ = _silu_and_grad(ga_ref[...])
        cat_ref[:, :POOL_WIDTH] = (mx_ref[...] * ps_ref[...] * silu_a).astype(BF16)
        silu_b, _ = _silu_and_grad(gb_ref[...])
        sb = hg_ref[:, :SSM_WIDTH] * _sigmoid(hg_ref[:, SSM_WIDTH:])
        cat_ref[:, POOL_WIDTH:] = (sb * silu_b).astype(BF16)
        h1 = x_ref[...] + jnp.dot(cat_ref[...], w_ref[...], preferred_element_type=F32)
        h1_ref[...] = h1
        h1b_ref[...] = h1.astype(BF16)

    row = pl.BlockSpec((tm, D_MODEL), lambda i: (i, 0))
    return pl.pallas_call(
        body, name="gate_out_proj", grid=(t // tm,),
        in_specs=[pl.BlockSpec((tm, POOL_WIDTH), lambda i: (i, 0)),
                  pl.BlockSpec((tm, POOL_WIDTH), lambda i: (i, 1)),
                  pl.BlockSpec((tm, SSM_WIDTH), lambda i: (i, 3)),
                  pl.BlockSpec((tm, 2 * SSM_WIDTH), lambda i: (i, 0)),
                  pl.BlockSpec((1, POOL_WIDTH), lambda i: (0, 0)), _resident(w_out.shape), row],
        out_specs=[row, row, row],
        out_shape=[jax.ShapeDtypeStruct((t, D_MODEL), BF16), jax.ShapeDtypeStruct((t, D_MODEL), F32),
                   jax.ShapeDtypeStruct((t, D_MODEL), BF16)],
        compiler_params=_params(("arbitrary",)),
    )(mixed, proj, proj, hg, pool_scale, w_out, x)


def _residual_dgrad(name, dy, w, residual):
    t = dy.shape[0]
    tm = _tile(t, ROW_TILE)
    n = w.shape[0]

    def body(dy_ref, w_ref, r_ref, o_ref, ob_ref):
        o = r_ref[...] + lax.dot_general(dy_ref[...], w_ref[...], DOT_NT, preferred_element_type=F32)
        o_ref[...] = o
        ob_ref[...] = o.astype(BF16)

    out = pl.BlockSpec((tm, n), lambda i: (i, 0))
    return pl.pallas_call(
        body, name=name, grid=(t // tm,),
        in_specs=[pl.BlockSpec((tm, dy.shape[1]), lambda i: (i, 0)), _resident(w.shape), out],
        out_specs=[out, out],
        out_shape=[jax.ShapeDtypeStruct((t, n), F32), jax.ShapeDtypeStruct((t, n), BF16)],
        compiler_params=_params(("arbitrary",)),
    )(dy, w, residual)


def _out_dgrad_gate_bwd(dh1b, w_out, mixed, proj, hg, pool_scale, after):
    t = mixed.shape[0]
    tm = _tile(t, ROW_TILE)
    n_after = len(after)

    def body(dh_ref, w_ref, mx_ref, ga_ref, gb_ref, hg_ref, ps_ref, *rest):
        dmx_ref, dp_ref, dhg_ref, gps_ref = rest[n_after:]

        @pl.when(pl.program_id(0) == 0)
        def _():
            gps_ref[...] = jnp.zeros_like(gps_ref)

        dcat = lax.dot_general(dh_ref[...], w_ref[...], DOT_NT, preferred_element_type=F32)
        ps = ps_ref[...]
        mx = mx_ref[...]
        dya = dcat[:, :POOL_WIDTH]
        silu_a, dsilu_a = _silu_and_grad(ga_ref[...])
        dpa = dya * silu_a
        gps_ref[...] += jnp.sum(dpa * mx, axis=0, keepdims=True)
        dmx_ref[...] = (dpa * ps).astype(BF16)
        dp_ref[:, :POOL_WIDTH] = jnp.zeros((tm, POOL_WIDTH), BF16)
        dp_ref[:, POOL_WIDTH:2 * POOL_WIDTH] = (dya * mx * ps * dsilu_a).astype(BF16)

        dyb = dcat[:, POOL_WIDTH:]
        silu_b, dsilu_b = _silu_and_grad(gb_ref[...])
        h_a = hg_ref[:, :SSM_WIDTH]
        sg = _sigmoid(hg_ref[:, SSM_WIDTH:])
        dsb = dyb * silu_b
        dp_ref[:, 2 * POOL_WIDTH:2 * POOL_WIDTH + SSM_WIDTH] = jnp.zeros((tm, SSM_WIDTH), BF16)
        dp_ref[:, 2 * POOL_WIDTH + SSM_WIDTH:] = (dyb * h_a * sg * dsilu_b).astype(BF16)
        dhg_ref[:, :SSM_WIDTH] = (dsb * sg).astype(BF16)
        dhg_ref[:, SSM_WIDTH:] = (dsb * h_a * sg * (1.0 - sg)).astype(BF16)

    half = pl.BlockSpec((tm, POOL_WIDTH), lambda i: (i, 0))
    full = pl.BlockSpec((tm, D_MODEL), lambda i: (i, 0))
    vec = pl.BlockSpec((1, POOL_WIDTH), lambda i: (0, 0))
    proj_width = 2 * POOL_WIDTH + 2 * SSM_WIDTH
    return pl.pallas_call(
        body, name="out_dgrad_gate_bwd", grid=(t // tm,),
        in_specs=[full, _resident(w_out.shape), half,
                  pl.BlockSpec((tm, POOL_WIDTH), lambda i: (i, 1)),
                  pl.BlockSpec((tm, SSM_WIDTH), lambda i: (i, 3)),
                  full, vec] + [pl.BlockSpec(tok.shape, lambda i: (0, 0)) for tok in after],
        out_specs=[half, pl.BlockSpec((tm, proj_width), lambda i: (i, 0)), full, vec],
        out_shape=[jax.ShapeDtypeStruct((t, POOL_WIDTH), BF16), jax.ShapeDtypeStruct((t, proj_width), BF16),
                   jax.ShapeDtypeStruct((t, 2 * SSM_WIDTH), BF16),
                   jax.ShapeDtypeStruct((1, POOL_WIDTH), F32)],
        compiler_params=_params(("arbitrary",)),
    )(dh1b, w_out, mixed, proj, proj, hg, pool_scale, *after)


def _ple_final(h1, h1b, p, w_gate, w_ple, target, gain):
    t = h1.shape[0]
    tm = _tile(t, 256)

    def body(h1_ref, h1b_ref, p_ref, wg_ref, wp_ref, tg_ref, g_ref, de_ref, dq_ref, dh2_ref, gg_ref, loss_ref):
        @pl.when(pl.program_id(0) == 0)
        def _():
            gg_ref[...] = jnp.zeros_like(gg_ref)
            loss_ref[...] = jnp.zeros_like(loss_ref)

        ev = jnp.dot(p_ref[...].astype(BF16), wp_ref[...], preferred_element_type=F32)
        sg = _sigmoid(jnp.dot(h1b_ref[...], wg_ref[...], preferred_element_type=F32))
        h2 = h1_ref[...] + ev * sg
        r = lax.rsqrt(jnp.mean(h2 * h2, axis=-1, keepdims=True) + EPS)
        n = h2 * r
        gain_v = g_ref[...]
        diff = n * gain_v - tg_ref[...]
        row_loss = jnp.sum(diff * diff, axis=-1, keepdims=True)
        loss_ref[...] += (0.5 / D_MODEL) * jnp.sum(row_loss, axis=0, keepdims=True)
        dout = diff * (1.0 / D_MODEL)
        gg_ref[...] += jnp.sum(dout * n, axis=0, keepdims=True)
        dn = dout * gain_v
        dh2 = r * (dn - n * jnp.mean(dn * n, axis=-1, keepdims=True))
        dh2_ref[...] = dh2
        de_ref[...] = (dh2 * sg).astype(BF16)
        dq_ref[...] = (dh2 * ev * sg * (1.0 - sg)).astype(BF16)

    row = pl.BlockSpec((tm, D_MODEL), lambda i: (i, 0))
    vec = pl.BlockSpec((1, D_MODEL), lambda i: (0, 0))
    return pl.pallas_call(
        body, name="ple_final", grid=(t // tm,),
        in_specs=[row, row, pl.BlockSpec((tm, PLE_DIM), lambda i: (i, 0)), _resident((D_MODEL, D_MODEL)),
                  _resident((PLE_DIM, D_MODEL)), row, vec],
        out_specs=[row, row, row, vec, pl.BlockSpec((1, 1), lambda i: (0, 0))],
        out_shape=[jax.ShapeDtypeStruct((t, D_MODEL), BF16), jax.ShapeDtypeStruct((t, D_MODEL), BF16),
                   jax.ShapeDtypeStruct((t, D_MODEL), F32), jax.ShapeDtypeStruct((1, D_MODEL), F32),
                   jax.ShapeDtypeStruct((1, 1), F32)],
        compiler_params=_params(("arbitrary",)),
    )(h1, h1b, p, w_gate, w_ple, target, gain)


def _zoh(a_re, a_im, log_dt, b_re_t, b_im_t):
    lam_re = jnp.minimum(a_re, A_RE_MAX)
    lam_im = a_im
    dt = jnp.exp(log_dt)
    mag = jnp.exp(lam_re * dt)
    ang = lam_im * dt
    ab_re = mag * jnp.cos(ang)
    ab_im = mag * jnp.sin(ang)
    den = lam_re * lam_re + lam_im * lam_im
    n_re = ab_re - 1.0
    n_im = ab_im
    q_re = (n_re * lam_re + n_im * lam_im) / den
    q_im = (n_im * lam_re - n_re * lam_im) / den
    bb_re = q_re[:, None, :] * b_re_t - q_im[:, None, :] * b_im_t
    bb_im = q_re[:, None, :] * b_im_t + q_im[:, None, :] * b_re_t
    return ab_re, ab_im, bb_re, bb_im


def _ssm_params(a_re, a_im, log_dt, b_re_t, b_im_t):
    def body(are_ref, aim_ref, dt_ref, bre_ref, bim_ref, abre_ref, abim_ref, bbre_ref, bbim_ref):
        ab_re, ab_im, bb_re, bb_im = _zoh(are_ref[...], aim_ref[...], dt_ref[...], bre_ref[...], bim_ref[...])
        abre_ref[...] = ab_re
        abim_ref[...] = ab_im
        bbre_ref[...] = bb_re
        bbim_ref[...] = bb_im

    return pl.pallas_call(
        body, name="ssm_params",
        out_shape=[jax.ShapeDtypeStruct(a_re.shape, F32), jax.ShapeDtypeStruct(a_re.shape, F32),
                   jax.ShapeDtypeStruct(b_re_t.shape, F32), jax.ShapeDtypeStruct(b_re_t.shape, F32)],
        compiler_params=_params(None),
    )(a_re, a_im, log_dt, b_re_t, b_im_t)


def _ssm_params_bwd(a_re, a_im, log_dt, b_re_t, b_im_t, g_ab_re, g_ab_im, g_bb_re, g_bb_im):
    def body(are_ref, aim_ref, dt_ref, bre_ref, bim_ref, gar_ref, gai_ref, gbr_ref, gbi_ref,
             o_are, o_aim, o_dt, o_bre, o_bim):
        _, vjp = jax.vjp(_zoh, are_ref[...], aim_ref[...], dt_ref[...], bre_ref[...], bim_ref[...])
        d_are, d_aim, d_dt, d_bre, d_bim = vjp((gar_ref[...], gai_ref[...], gbr_ref[...], gbi_ref[...]))
        o_are[...] = d_are
        o_aim[...] = d_aim
        o_dt[...] = d_dt
        o_bre[...] = d_bre
        o_bim[...] = d_bim

    ins = (a_re, a_im, log_dt, b_re_t, b_im_t)
    return pl.pallas_call(
        body, name="ssm_params_bwd",
        out_shape=[jax.ShapeDtypeStruct(v.shape, F32) for v in ins],
        compiler_params=_params(None),
    )(*ins, g_ab_re, g_ab_im, g_bb_re, g_bb_im)


CHUNK_TILES = CHUNK_STATE // LANES
CH_PER_TILE = CHUNK_IN // CHUNK_TILES
PAIR = 2 * LANES
SSM_ROWS = 256
SCAN_STEPS = 8
U_COLUMN_BLOCK = 2 * POOL_WIDTH // SSM_WIDTH


def _own_half():
    r = lax.broadcasted_iota(jnp.int32, (CHUNK_IN, LANES), 0) // SSM_GROUP % 2
    c = lax.broadcasted_iota(jnp.int32, (CHUNK_IN, LANES), 1) // SSM_STATE
    return (r == c)[None]


def _compact_weight(w):
    tiled = jnp.tile(w.reshape(SSM_CHUNKS, CHUNK_IN, SSM_STATE), (1, 1, 2))
    return jnp.where(_own_half(), tiled, 0.0)


def _compact_pair(w_a, w_b):
    return jnp.concatenate([_compact_weight(w_a), _compact_weight(w_b)], axis=-1).astype(BF16)


def _expand_grad(g):
    kept = jnp.where(_own_half(), g, 0.0)
    return kept.reshape(SSM_CHUNKS, CHUNK_IN, 2, SSM_STATE).sum(axis=2).reshape(N_SSM_GROUPS, SSM_GROUP, SSM_STATE)


TILES_PER_BLOCK = LANES // CH_PER_TILE
IN_BLOCKS = CHUNK_IN // LANES


def _tile_masks():
    j = lax.broadcasted_iota(jnp.int32, (CHUNK_TILES, LANES), 0) % TILES_PER_BLOCK
    lane = lax.broadcasted_iota(jnp.int32, (CHUNK_TILES, LANES), 1) // CH_PER_TILE
    return (j == lane).astype(F32)


def _tile_rows(ref, j, tt):
    return ref.at[j // TILES_PER_BLOCK, pl.ds(j, tt, stride=CHUNK_TILES), :]


def _spread(ref, v, masks):
    tt = v.shape[0]
    for j in range(CHUNK_TILES):
        block = LANES * (j // TILES_PER_BLOCK)
        _tile_rows(ref, j, tt)[...] = v[:, block:block + LANES] * masks[j:j + 1, :]
    return jnp.concatenate([ref[b] for b in range(IN_BLOCKS)], axis=1).astype(BF16)


def _gather(ref, full, masks):
    tt = full.shape[0] // CHUNK_TILES
    for b in range(IN_BLOCKS):
        ref[b] = full[:, b * LANES:(b + 1) * LANES]
    out = []
    for b in range(IN_BLOCKS):
        acc = None
        for j in range(b * TILES_PER_BLOCK, (b + 1) * TILES_PER_BLOCK):
            part = _tile_rows(ref, j, tt)[...] * masks[j:j + 1, :]
            acc = part if acc is None else acc + part
        out.append(acc)
    return jnp.concatenate(out, axis=1)


def _resident(shape):
    return pl.BlockSpec(shape, lambda i: (0,) * len(shape), pipeline_mode=pl.Buffered(1))


def _halves(ref, k, rows=slice(None)):
    return ref[k, rows, :LANES], ref[k, rows, LANES:]


def _ssm_fwd(proj, w2, c2, a2, d_skip):
    t = proj.shape[0]
    tt = _tile(t, SSM_ROWS)
    rows = tt * CHUNK_TILES

    def body(u_ref, w_ref, c_ref, a_ref, d_ref, y_ref, gel_ref, s_ref, carry, spread_ref, full_ref):
        @pl.when(pl.program_id(0) == 0)
        def _():
            carry[...] = jnp.zeros_like(carry)
            spread_ref[...] = jnp.zeros_like(spread_ref)

        mask = _tile_masks()
        u = u_ref[...]
        for k in range(SSM_CHUNKS):
            uk = _spread(spread_ref, u[:, k * CHUNK_IN:(k + 1) * CHUNK_IN], mask)
            s_ref[k] = jnp.dot(uk, w_ref[k], preferred_element_type=F32)

        abar = [_halves(a_ref, k) for k in range(SSM_CHUNKS)]

        def steps(i, state):
            for v in range(SCAN_STEPS):
                r = pl.ds(pl.multiple_of((i * SCAN_STEPS + v) * CHUNK_TILES, CHUNK_TILES), CHUNK_TILES)
                new = []
                for k, ((a_re, a_im), (s_re, s_im)) in enumerate(zip(abar, state)):
                    b_re, b_im = _halves(s_ref, k, r)
                    s_re, s_im = a_re * s_re - a_im * s_im + b_re, a_re * s_im + a_im * s_re + b_im
                    s_ref[k, r, :LANES] = s_re
                    s_ref[k, r, LANES:] = s_im
                    new.append((s_re, s_im))
                state = tuple(new)
            return state

        state = lax.fori_loop(0, tt // SCAN_STEPS, steps, tuple(_halves(carry, k) for k in range(SSM_CHUNKS)))
        for k, (s_re, s_im) in enumerate(state):
            carry[k, :, :LANES] = s_re
            carry[k, :, LANES:] = s_im

        for k in range(SSM_CHUNKS):
            cols = slice(k * CHUNK_IN, (k + 1) * CHUNK_IN)
            full = lax.dot_general(s_ref[k].astype(BF16), c_ref[k], DOT_NT, preferred_element_type=F32)
            y = _gather(full_ref, full, mask) + d_ref[:, cols] * u[:, cols]
            y_ref[:, cols] = y
            gel_ref[:, cols] = (0.5 * y * (1.0 + jnp.tanh(GELU_C * (y + GELU_A * y * y * y)))).astype(BF16)

    weight = _resident((SSM_CHUNKS, CHUNK_IN, PAIR))
    tokens = pl.BlockSpec((tt, SSM_WIDTH), lambda i: (i, 0))
    return pl.pallas_call(
        body, name="ssm_fwd", grid=(t // tt,),
        in_specs=[pl.BlockSpec((tt, SSM_WIDTH), lambda i: (i, U_COLUMN_BLOCK)), weight, weight,
                  _resident((SSM_CHUNKS, CHUNK_TILES, PAIR)), _resident((1, SSM_WIDTH))],
        out_specs=[tokens, tokens, pl.BlockSpec((SSM_CHUNKS, rows, PAIR), lambda i: (0, i, 0))],
        out_shape=[jax.ShapeDtypeStruct((t, SSM_WIDTH), F32), jax.ShapeDtypeStruct((t, SSM_WIDTH), BF16),
                   jax.ShapeDtypeStruct((SSM_CHUNKS, t * CHUNK_TILES, PAIR), F32)],
        scratch_shapes=[pltpu.VMEM((SSM_CHUNKS, CHUNK_TILES, PAIR), F32), pltpu.VMEM((IN_BLOCKS, rows, LANES), F32),
                        pltpu.VMEM((IN_BLOCKS, rows, LANES), F32)],
        compiler_params=_params(("arbitrary",)),
    )(proj, w2, c2, a2, d_skip)


def _ssm_bwd(dy, proj, s, w2, c2, a2, d_skip, dproj):
    t = dy.shape[0]
    tt = _tile(t, SSM_ROWS)
    rows = tt * CHUNK_TILES
    n_chunks = t // tt

    def body(dy_ref, u_ref, s_ref, w_ref, c_ref, a_ref, d_ref, _, du_ref, gc_ref, gw_ref, ga_ref, gd_ref, z_ref, carry,
             spread_ref, full_ref):
        @pl.when(pl.program_id(0) == 0)
        def _():
            for r in (carry, gc_ref, gw_ref, ga_ref, gd_ref, spread_ref):
                r[...] = jnp.zeros_like(r)

        mask = _tile_masks()
        dy_v = dy_ref[...]
        u = u_ref[...]
        gd_ref[...] += jnp.sum(dy_v * u, axis=0, keepdims=True)
        for k in range(SSM_CHUNKS):
            dk = _spread(spread_ref, dy_v[:, k * CHUNK_IN:(k + 1) * CHUNK_IN], mask)
            z_ref[k] = jnp.dot(dk, c_ref[k], preferred_element_type=F32)
            gc_ref[k] += lax.dot_general(dk, s_ref[k].astype(BF16), DOT_TN, preferred_element_type=F32)

        abar = [_halves(a_ref, k) for k in range(SSM_CHUNKS)]

        def steps(i, state):
            zs, gs = state
            for v in range(SCAN_STEPS):
                tok = tt - 1 - (i * SCAN_STEPS + v)
                r = pl.ds(pl.multiple_of(tok * CHUNK_TILES, CHUNK_TILES), CHUNK_TILES)
                new_z, new_g = [], []
                for k, ((a_re, a_im), (z_re, z_im), (g_re, g_im)) in enumerate(zip(abar, zs, gs)):
                    s_re, s_im = _halves(s_ref, k, r)
                    g_re = g_re + z_re * s_re + z_im * s_im
                    g_im = g_im + z_im * s_re - z_re * s_im
                    d_re, d_im = _halves(z_ref, k, r)
                    z_re, z_im = d_re + a_re * z_re + a_im * z_im, d_im + a_re * z_im - a_im * z_re
                    z_ref[k, r, :LANES] = z_re
                    z_ref[k, r, LANES:] = z_im
                    new_z.append((z_re, z_im))
                    new_g.append((g_re, g_im))
                zs, gs = tuple(new_z), tuple(new_g)
            return zs, gs

        zs, gs = lax.fori_loop(0, tt // SCAN_STEPS, steps,
                               (tuple(_halves(carry, k) for k in range(SSM_CHUNKS)),
                                tuple(_halves(ga_ref, k) for k in range(SSM_CHUNKS))))
        for k in range(SSM_CHUNKS):
            carry[k, :, :LANES], carry[k, :, LANES:] = zs[k]
            ga_ref[k, :, :LANES], ga_ref[k, :, LANES:] = gs[k]

        for k in range(SSM_CHUNKS):
            cols = slice(k * CHUNK_IN, (k + 1) * CHUNK_IN)
            zb = z_ref[k].astype(BF16)
            full = lax.dot_general(zb, w_ref[k], DOT_NT, preferred_element_type=F32)
            du_ref[:, cols] = (_gather(full_ref, full, mask) + d_ref[:, cols] * dy_v[:, cols]).astype(BF16)
            uk = _spread(spread_ref, u[:, cols], mask)
            gw_ref[k] += lax.dot_general(uk, zb, DOT_TN, preferred_element_type=F32)

    weight = _resident((SSM_CHUNKS, CHUNK_IN, PAIR))
    tokens = pl.BlockSpec((tt, SSM_WIDTH), lambda i: (n_chunks - 1 - i, 0))
    grad = pl.BlockSpec((SSM_CHUNKS, CHUNK_IN, PAIR), lambda i: (0, 0, 0))
    return pl.pallas_call(
        body, name="ssm_bwd", grid=(n_chunks,),
        in_specs=[tokens, pl.BlockSpec((tt, SSM_WIDTH), lambda i: (n_chunks - 1 - i, U_COLUMN_BLOCK)),
                  pl.BlockSpec((SSM_CHUNKS, rows, PAIR), lambda i: (0, n_chunks - 1 - i, 0)), weight, weight,
                  _resident((SSM_CHUNKS, CHUNK_TILES, PAIR)), _resident((1, SSM_WIDTH)),
                  pl.BlockSpec(memory_space=pl.ANY)],
        out_specs=[pl.BlockSpec((tt, SSM_WIDTH), lambda i: (n_chunks - 1 - i, U_COLUMN_BLOCK)), grad, grad,
                   pl.BlockSpec((SSM_CHUNKS, CHUNK_TILES, PAIR), lambda i: (0, 0, 0)),
                   pl.BlockSpec((1, SSM_WIDTH), lambda i: (0, 0))],
        out_shape=[jax.ShapeDtypeStruct(dproj.shape, dproj.dtype), jax.ShapeDtypeStruct((SSM_CHUNKS, CHUNK_IN, PAIR), F32),
                   jax.ShapeDtypeStruct((SSM_CHUNKS, CHUNK_IN, PAIR), F32),
                   jax.ShapeDtypeStruct((SSM_CHUNKS, CHUNK_TILES, PAIR), F32), jax.ShapeDtypeStruct((1, SSM_WIDTH), F32)],
        input_output_aliases={7: 0},
        scratch_shapes=[pltpu.VMEM((SSM_CHUNKS, rows, PAIR), F32), pltpu.VMEM((SSM_CHUNKS, CHUNK_TILES, PAIR), F32),
                        pltpu.VMEM((IN_BLOCKS, rows, LANES), F32), pltpu.VMEM((IN_BLOCKS, rows, LANES), F32)],
        compiler_params=_params(("arbitrary",)),
    )(dy, proj, s, w2, c2, a2, d_skip, dproj)


def _block(ref, axis, size, index):
    idx = [slice(None)] * len(ref.shape)
    idx[axis] = pl.ds(pl.multiple_of(index * size, size), size)
    return ref.at[tuple(idx)]


HBM_SPEC = pl.BlockSpec(memory_space=pltpu.HBM)
SEM_SPEC = pl.BlockSpec(memory_space=pltpu.SEMAPHORE)
ANY_SPEC = pl.BlockSpec(memory_space=pl.ANY)
SPLIT_PARAMS = pltpu.CompilerParams(has_side_effects=pltpu.SideEffectType.DATAFLOW_SIDE_EFFECTING)
N_PEERS = N_DEV - 1
TOKEN = jax.ShapeDtypeStruct((SUBLANES, LANES), F32)
VMEM_SPEC = pl.BlockSpec(memory_space=pltpu.VMEM)


def _in_hbm(arrays):
    return [pltpu.with_memory_space_constraint(a, pltpu.HBM) for a in arrays]


def _peer(m):
    x, y, c = (lax.axis_index(a) for a in MESH_AXES)
    px = 1 - x if m & 4 else x
    py = 1 - y if m & 2 else y
    pc = 1 - c if m & 1 else c
    return (px, py, pc), 4 * px + 2 * py + pc


def _my_index():
    x, y, c = (lax.axis_index(a) for a in MESH_AXES)
    return 4 * x + 2 * y + c


def _gather_copies(shard_refs, full_refs, axes, send_sems, recv_sems):
    copies = []
    for i, (shard, full) in enumerate(zip(shard_refs, full_refs)):
        mine = _block(full, axes[i], shard.shape[axes[i]], _my_index())
        for m in range(1, N_DEV):
            peer, _ = _peer(m)
            copies.append(pltpu.make_async_remote_copy(
                src_ref=shard, dst_ref=mine, send_sem=send_sems.at[N_PEERS * i + m - 1],
                recv_sem=recv_sems.at[N_PEERS * i + m - 1], device_id=peer, device_id_type=MESH))
    return copies


def _gather_start(name, shards, axes, after):
    n = len(shards)

    def body(*refs):
        shard_refs = refs[:n]
        send_sems, recv_sems, local_sems = refs[n + 1:n + 4]
        full_refs = refs[2 * n + 4:3 * n + 4]
        refs[3 * n + 4][...] = jnp.zeros(TOKEN.shape, TOKEN.dtype)
        for i in range(n):
            pltpu.make_async_copy(shard_refs[i], _block(full_refs[i], axes[i], shard_refs[i].shape[axes[i]], _my_index()),
                                  local_sems.at[i]).start()
        for cp in _gather_copies(shard_refs, full_refs, axes, send_sems, recv_sems):
            cp.start()

    fulls = []
    for s, a in zip(shards, axes):
        shape = list(s.shape)
        shape[a] *= N_DEV
        fulls.append(pltpu.HBM(tuple(shape), s.dtype))
    out = pl.pallas_call(
        body, name=name,
        out_shape=(pltpu.SemaphoreType.DMA((N_PEERS * n,)), pltpu.SemaphoreType.DMA((N_PEERS * n,)),
                   pltpu.SemaphoreType.DMA((n,)), *[pltpu.HBM(s.shape, s.dtype) for s in shards], *fulls, TOKEN),
        in_specs=[HBM_SPEC] * n + [ANY_SPEC],
        out_specs=(SEM_SPEC, SEM_SPEC, SEM_SPEC, *[HBM_SPEC] * (2 * n), VMEM_SPEC),
        input_output_aliases={i: 3 + i for i in range(n)},
        compiler_params=SPLIT_PARAMS,
    )(*_in_hbm(shards), after)
    return out[:-1], out[-1]


def _gather_wait(name, started, indices, axes, after):
    send_sems, recv_sems, local_sems = started[:3]
    n_all = (len(started) - 3) // 2
    shards = [started[3 + i] for i in indices]
    fulls = [started[3 + n_all + i] for i in indices]
    n = len(indices)

    def body(*refs):
        shard_refs, full_refs = refs[:n], refs[n:2 * n]
        send_sems, recv_sems, local_sems = refs[2 * n:2 * n + 3]
        for j, i in enumerate(indices):
            mine = _block(full_refs[j], axes[j], shard_refs[j].shape[axes[j]], _my_index())
            pltpu.make_async_copy(shard_refs[j], mine, local_sems.at[i]).wait()
            for m in range(1, N_DEV):
                peer, _ = _peer(m)
                cp = pltpu.make_async_remote_copy(
                    src_ref=shard_refs[j], dst_ref=mine, send_sem=send_sems.at[N_PEERS * i + m - 1],
                    recv_sem=recv_sems.at[N_PEERS * i + m - 1], device_id=peer, device_id_type=MESH)
                cp.wait_send()
                cp.wait_recv()

    out = pl.pallas_call(
        body, name=name,
        out_shape=tuple(pltpu.HBM(a.shape, a.dtype) for a in shards + fulls),
        in_specs=[HBM_SPEC] * (2 * n) + [SEM_SPEC] * 3 + [ANY_SPEC], out_specs=tuple([HBM_SPEC] * (2 * n)),
        input_output_aliases={i: i for i in range(2 * n)},
        compiler_params=SPLIT_PARAMS,
    )(*shards, *fulls, send_sems, recv_sems, local_sems, after)
    return out[n:]


def _exchange_start(name, fulls, axes):
    n = len(fulls)
    sizes = [f.shape[a] // N_DEV for f, a in zip(fulls, axes)]

    def body(*refs):
        ins = refs[:n]
        send_sems, recv_sems = refs[n:n + 2]
        lands = refs[2 * n + 2:3 * n + 2]
        refs[3 * n + 2][...] = jnp.zeros(TOKEN.shape, TOKEN.dtype)
        for i in range(n):
            for m in range(1, N_DEV):
                peer, index = _peer(m)
                pltpu.make_async_remote_copy(
                    src_ref=_block(ins[i], axes[i], sizes[i], index), dst_ref=lands[i].at[m - 1],
                    send_sem=send_sems.at[N_PEERS * i + m - 1], recv_sem=recv_sems.at[N_PEERS * i + m - 1],
                    device_id=peer, device_id_type=MESH).start()

    lands = []
    for f, a, size in zip(fulls, axes, sizes):
        shape = list(f.shape)
        shape[a] = size
        lands.append(pltpu.HBM((N_PEERS, *shape), f.dtype))
    out = pl.pallas_call(
        body, name=name,
        out_shape=(pltpu.SemaphoreType.DMA((N_PEERS * n,)), pltpu.SemaphoreType.DMA((N_PEERS * n,)),
                   *[pltpu.HBM(f.shape, f.dtype) for f in fulls], *lands, TOKEN),
        in_specs=[HBM_SPEC] * n, out_specs=(SEM_SPEC, SEM_SPEC, *[HBM_SPEC] * (2 * n), VMEM_SPEC),
        input_output_aliases={i: 2 + i for i in range(n)},
        compiler_params=SPLIT_PARAMS,
    )(*_in_hbm(fulls))
    return out[:-1], out[-1]


def _exchange_wait(name, started, axes, after):
    send_sems, recv_sems = started[:2]
    n = (len(started) - 2) // 2
    fulls, lands = list(started[2:2 + n]), list(started[2 + n:])
    sizes = [f.shape[a] // N_DEV for f, a in zip(fulls, axes)]

    def body(*refs):
        ins, land_refs = refs[:n], refs[n:2 * n]
        send_sems, recv_sems = refs[2 * n:2 * n + 2]
        for i in range(n):
            for m in range(1, N_DEV):
                peer, index = _peer(m)
                cp = pltpu.make_async_remote_copy(
                    src_ref=_block(ins[i], axes[i], sizes[i], index), dst_ref=land_refs[i].at[m - 1],
                    send_sem=send_sems.at[N_PEERS * i + m - 1], recv_sem=recv_sems.at[N_PEERS * i + m - 1],
                    device_id=peer, device_id_type=MESH)
                cp.wait_send()
                cp.wait_recv()

    out = pl.pallas_call(
        body, name=name,
        out_shape=tuple(pltpu.HBM(a.shape, a.dtype) for a in fulls + lands),
        in_specs=[HBM_SPEC] * (2 * n) + [SEM_SPEC] * 2 + [ANY_SPEC], out_specs=tuple([HBM_SPEC] * (2 * n)),
        input_output_aliases={i: i for i in range(2 * n)},
        compiler_params=SPLIT_PARAMS,
    )(*fulls, *lands, send_sems, recv_sems, after)
    return out[:n], out[n:]


def _sum_parts(part_refs, ndim):
    g = None
    for p_ref in part_refs:
        stacked = len(p_ref.shape) > ndim
        terms = [p_ref[s] for s in range(p_ref.shape[0])] if stacked else [p_ref[...]]
        for term in terms:
            term = term.astype(F32)
            g = term if g is None else g + term
    return g


def _adamw_update(w_ref, m_ref, v_ref, g, g_ref, d_ref, nm_ref, nv_ref):
    c1 = 1.0 - ADAM_B1 ** ADAM_STEP
    c2 = 1.0 - ADAM_B2 ** ADAM_STEP
    new_m = ADAM_B1 * m_ref[...] + (1.0 - ADAM_B1) * g
    new_v = ADAM_B2 * v_ref[...] + (1.0 - ADAM_B2) * (g * g)
    g_ref[...] = g
    nm_ref[...] = new_m
    nv_ref[...] = new_v
    d_ref[...] = -ADAM_LR * ((new_m / c1) / (jnp.sqrt(new_v / c2) + ADAM_EPS) + ADAM_WD * w_ref[...])


def _adamw_small(ws, ms, vs, stacks, loss_stack):
    n = len(ws)

    def body(*refs):
        ins, outs = refs[:4 * n + 1], refs[4 * n + 1:]
        for i in range(n):
            _adamw_update(ins[i], ins[n + i], ins[2 * n + i], _sum_parts([ins[3 * n + i]], len(ins[i].shape)),
                          outs[i], outs[n + i], outs[2 * n + i], outs[3 * n + i])
        total = ins[4 * n][0]
        for dev in range(1, N_DEV):
            total = total + ins[4 * n][dev]
        outs[4 * n][...] = total

    res = pl.pallas_call(
        body, name="adamw_small",
        out_shape=[jax.ShapeDtypeStruct(w.shape, F32) for w in ws] * 4 + [jax.ShapeDtypeStruct((1, LANES), F32)],
        compiler_params=_params(None),
    )(*ws, *ms, *vs, *stacks, loss_stack)
    return res[:n], res[n:2 * n], res[2 * n:3 * n], res[3 * n:4 * n], res[4 * n]


def _adamw(name, w, m, v, parts):
    r, c = w.shape
    tr = _tile(r, 256)
    n_parts = len(parts)

    def body(*refs):
        _adamw_update(refs[0], refs[1], refs[2], _sum_parts(refs[3:3 + n_parts], 2), *refs[3 + n_parts:])

    row = pl.BlockSpec((tr, c), lambda i: (i, 0))
    in_specs = [row, row, row]
    for p in parts:
        in_specs.append(row if p.ndim == 2 else pl.BlockSpec((p.shape[0], tr, c), lambda i: (0, i, 0)))
    return pl.pallas_call(
        body, name=name, grid=(r // tr,), in_specs=in_specs, out_specs=[row] * 4,
        out_shape=[jax.ShapeDtypeStruct((r, c), F32)] * 4,
        compiler_params=_params(("arbitrary",)),
    )(w, m, v, *parts)


SMALL = ("norm_gain", "pool_scale", "a_re", "a_im", "log_dt", "b_re", "b_im", "c_re", "c_im", "d_skip", "final_gain")
LARGE = ("w_in", "w_pool", "w_glu", "w_out", "w_ple", "w_ple_gate")
LARGE_AXIS = {"w_in": 1, "w_pool": 1, "w_glu": 1, "w_out": 0, "w_ple": 1, "w_ple_gate": 0}
WEIGHTS = ("norm_gain", "w_in", "w_pool", "pool_scale", "a_re", "a_im", "log_dt", "b_re", "b_im", "c_re", "c_im",
           "d_skip", "w_glu", "w_out", "w_ple", "w_ple_gate", "final_gain")


def kernel(x, p, norm_gain, w_in, w_pool, pool_scale, a_re, a_im, log_dt, b_re, b_im, c_re, c_im, d_skip, w_glu, w_out, w_ple, w_ple_gate, final_gain, loss_target, m_norm_gain, m_w_in, m_w_pool, m_pool_scale, m_a_re, m_a_im, m_log_dt, m_b_re, m_b_im, m_c_re, m_c_im, m_d_skip, m_w_glu, m_w_out, m_w_ple, m_w_ple_gate, m_final_gain, v_norm_gain, v_w_in, v_w_pool, v_pool_scale, v_a_re, v_a_im, v_log_dt, v_b_re, v_b_im, v_c_re, v_c_im, v_d_skip, v_w_glu, v_w_out, v_w_ple, v_w_ple_gate, v_final_gain):
    weights = dict(norm_gain=norm_gain, w_in=w_in, w_pool=w_pool, pool_scale=pool_scale, a_re=a_re, a_im=a_im,
                   log_dt=log_dt, b_re=b_re, b_im=b_im, c_re=c_re, c_im=c_im, d_skip=d_skip, w_glu=w_glu,
                   w_out=w_out, w_ple=w_ple, w_ple_gate=w_ple_gate, final_gain=final_gain)
    mom_m = dict(norm_gain=m_norm_gain, w_in=m_w_in, w_pool=m_w_pool, pool_scale=m_pool_scale, a_re=m_a_re,
                 a_im=m_a_im, log_dt=m_log_dt, b_re=m_b_re, b_im=m_b_im, c_re=m_c_re, c_im=m_c_im,
                 d_skip=m_d_skip, w_glu=m_w_glu, w_out=m_w_out, w_ple=m_w_ple, w_ple_gate=m_w_ple_gate,
                 final_gain=m_final_gain)
    mom_v = dict(norm_gain=v_norm_gain, w_in=v_w_in, w_pool=v_w_pool, pool_scale=v_pool_scale, a_re=v_a_re,
                 a_im=v_a_im, log_dt=v_log_dt, b_re=v_b_re, b_im=v_b_im, c_re=v_c_re, c_im=v_c_im,
                 d_skip=v_d_skip, w_glu=v_w_glu, w_out=v_w_out, w_ple=v_w_ple, w_ple_gate=v_w_ple_gate,
                 final_gain=v_final_gain)

    t = x.shape[1]
    xs = x.reshape(t, D_MODEL)
    ps = p.reshape(t, PLE_DIM)
    target = loss_target.reshape(t, D_MODEL)
    gain1 = norm_gain.reshape(1, D_MODEL)
    gain_f = final_gain.reshape(1, D_MODEL)
    scale_p = pool_scale.reshape(1, POOL_WIDTH)
    skip = d_skip.reshape(1, SSM_WIDTH)

    shard2d = {k: weights[k][0] for k in LARGE}
    shard_bf = {k: shard2d[k].astype(BF16) for k in LARGE}
    me = 4 * lax.axis_index("x") + 2 * lax.axis_index("y") + lax.axis_index("c")
    hn = _norm1_fwd(xs, gain1)
    groups = {"small": ("w_pool", "w_glu", "w_ple"), "big": ("w_out", "w_ple_gate")}
    gathers = {}

    def leave(group, after):
        names = groups[group]
        gathers[group], token = _gather_start(
            "weights_gather_start_" + group, [shard_bf[k] for k in names], [LARGE_AXIS[k] for k in names], after)
        return token

    proj, w_in_full = _gather_in_proj(hn, shard_bf["w_in"], me, leave("small", hn))
    full = {"w_in": w_in_full}
    gather_token = leave("big", w_in_full)

    def arrive(k, after):
        group = next(g for g, names in groups.items() if k in names)
        i = groups[group].index(k)
        full[k] = _gather_wait("gather_wait_" + k, gathers[group], [i], [LARGE_AXIS[k]], after)[0]

    ar, ai = a_re[0], a_im[0]
    ldt = log_dt.reshape(N_SSM_GROUPS, 1)
    br_t = jnp.transpose(b_re[0], (0, 2, 1))
    bi_t = jnp.transpose(b_im[0], (0, 2, 1))
    ab_re, ab_im, bb_re, bb_im = _ssm_params(ar, ai, ldt, br_t, bi_t)
    tiles = (SSM_CHUNKS, CHUNK_TILES, LANES)
    abar = jnp.concatenate([ab_re.reshape(tiles), ab_im.reshape(tiles)], axis=-1)
    w_pair = _compact_pair(bb_re, bb_im)
    c_pair = _compact_pair(c_re[0], -c_im[0])

    arrive("w_pool", proj)
    pooled, mixed = _pool_mix(proj, full["w_pool"], gather_token)
    y, gel, states = _ssm_fwd(proj, w_pair, c_pair, abar, skip)
    arrive("w_glu", gel)
    hg = _mm_nn("glu_proj", gel, full["w_glu"], [F32])[0]
    arrive("w_out", hg)
    cat, h1, h1b = _gate_out_proj(mixed, proj, hg, scale_p, full["w_out"], xs)
    arrive("w_ple", h1b)
    arrive("w_ple_gate", h1b)
    de, dq, dh2, g_final_gain, loss_part = _ple_final(h1, h1b, ps, full["w_ple_gate"], full["w_ple"], target, gain_f)

    grads = {}
    grads["w_ple_gate"] = _mm_tn("ple_gate_wgrad", h1b, dq, BF16)
    grads["w_ple"] = _mm_tn("ple_wgrad", ps, de, BF16)
    sent, tokens = {}, {}

    def send(names):
        sent[names], tokens[names[0]] = _exchange_start(
            "grads_start_" + names[0], [grads[k] for k in names], [LARGE_AXIS[k] for k in names])

    send(("w_ple_gate", "w_ple"))
    dh1, dh1b = _residual_dgrad("ple_gate_dgrad", dq, full["w_ple_gate"], dh2)
    grads["w_out"] = _mm_tn("out_wgrad", cat, dh1b, BF16)
    send(("w_out",))
    dmixed, dproj, dhg, g_pool_scale = _out_dgrad_gate_bwd(
        dh1b, full["w_out"], mixed, proj, hg, scale_p, [tokens["w_ple_gate"], tokens["w_out"]])

    tk = _tile(t, 1024)
    grads["w_pool"] = _mm("pool_wgrad", [(pooled, (tk, POOL_GROUP), lambda i, j, s: (s, i),
                                          dmixed, (tk, POOL_GROUP), lambda i, j, s: (s, i))],
                          DOT_TN, (N_POOL_GROUPS, 1, t // tk),
                          [((N_POOL_GROUPS, POOL_GROUP, POOL_GROUP), BF16, (None, POOL_GROUP, POOL_GROUP),
                            lambda i, j, s: (i, 0, 0))], t // tk)[0]
    dproj = _pool_mix_bwd(dmixed, full["w_pool"], dproj)

    grads["w_glu"] = _mm_tn("glu_wgrad", gel, dhg, BF16)
    send(("w_pool", "w_glu"))

    def gelu_bwd_epilogue(acc, ex, out_refs):
        yv = ex[0][...]
        th = jnp.tanh(GELU_C * (yv + GELU_A * yv * yv * yv))
        dgelu = 0.5 * (1.0 + th) + 0.5 * yv * (1.0 - th * th) * GELU_C * (1.0 + 3.0 * GELU_A * yv * yv)
        out_refs[0][...] = acc * dgelu

    dy = _mm_nt("glu_dgrad", dhg, full["w_glu"], [F32], tk=2048, extras=[y], epilogue=gelu_bwd_epilogue,
                after=[tokens["w_pool"]])[0]
    dproj, g_c_pair, g_w_pair, g_abar, g_d_skip = _ssm_bwd(dy, proj, states, w_pair, c_pair, abar, skip, dproj)

    g_ab_re = g_abar[..., :LANES].reshape(N_SSM_GROUPS, SSM_STATE)
    g_ab_im = g_abar[..., LANES:].reshape(N_SSM_GROUPS, SSM_STATE)
    d_ar, d_ai, d_ldt, d_br_t, d_bi_t = _ssm_params_bwd(
        ar, ai, ldt, br_t, bi_t, g_ab_re, g_ab_im,
        _expand_grad(g_w_pair[..., :LANES]), _expand_grad(g_w_pair[..., LANES:]))

    small_grads = dict(
        pool_scale=g_pool_scale, a_re=d_ar, a_im=d_ai, log_dt=d_ldt.reshape(1, N_SSM_GROUPS),
        b_re=d_br_t.astype(BF16), b_im=d_bi_t.astype(BF16), c_re=_expand_grad(g_c_pair[..., :LANES]).astype(BF16),
        c_im=(-_expand_grad(g_c_pair[..., LANES:])).astype(BF16), d_skip=g_d_skip, final_gain=g_final_gain)
    early = [k for k in SMALL if k != "norm_gain"]
    early_sent, early_token = _gather_start(
        "small_grads_start", [small_grads[k][None] for k in early] + [jnp.broadcast_to(loss_part, (1, 1, LANES))],
        [0] * (len(early) + 1), d_ar)

    grads["w_in"] = _mm_tn("in_wgrad", hn, dproj, BF16, after=[early_token])
    send(("w_in",))
    grad_x, g_norm_gain = _in_dgrad_norm1_bwd(dproj, full["w_in"], xs, dh1, gain1, tokens["w_in"])
    late_sent, late_token = _gather_start("norm_gain_grad_start", [g_norm_gain[None]], [0], g_norm_gain)

    out_g, out_d, out_m, out_v = ({} for _ in range(4))
    after = late_token
    for names, started in sent.items():
        axes = [LARGE_AXIS[k] for k in names]
        partials, landed = _exchange_wait("grads_wait_" + names[0], started, axes, after)
        for k, axis, partial, land in zip(names, axes, partials, landed):
            shard_shape = shard2d[k].shape
            size = shard_shape[axis]
            own = lax.dynamic_slice_in_dim(partial, me * size, size, axis=axis)
            view = (-1, shard_shape[-1])
            rows = math.prod(shard_shape[:-1])
            res = _adamw("adamw_" + k, shard2d[k].reshape(view), mom_m[k][0].reshape(view), mom_v[k][0].reshape(view),
                         [own.reshape(view), land.reshape(N_PEERS, rows, shard_shape[-1])])
            out_g[k], out_d[k], out_m[k], out_v[k] = (r.reshape(weights[k].shape) for r in res)
            after = res[0]

    def b_view(a):
        return jnp.transpose(a[0], (0, 2, 1))

    views = dict(norm_gain=lambda a: a, pool_scale=lambda a: a, a_re=lambda a: a[0], a_im=lambda a: a[0],
                 log_dt=lambda a: a, b_re=b_view, b_im=b_view, c_re=lambda a: a[0], c_im=lambda a: a[0],
                 d_skip=lambda a: a, final_gain=lambda a: a.reshape(1, D_MODEL))
    landed = _gather_wait("small_grads_wait", early_sent, list(range(len(early) + 1)), [0] * (len(early) + 1), after)
    stack = dict(zip(early, landed))
    stack["norm_gain"] = _gather_wait("norm_gain_grad_wait", late_sent, [0], [0], after)[0]
    *small_out, loss_row = _adamw_small(
        [views[k](weights[k]) for k in SMALL], [views[k](mom_m[k]) for k in SMALL],
        [views[k](mom_v[k]) for k in SMALL], [stack[k] for k in SMALL], landed[-1])
    loss = loss_row[0, 0]
    for out, res in zip((out_g, out_d, out_m, out_v), small_out):
        for k, r in zip(SMALL, res):
            if k in ("b_re", "b_im"):
                r = jnp.transpose(r, (0, 2, 1))
            out[k] = r.reshape(weights[k].shape)

    return (loss, grad_x.reshape(x.shape), *[out_g[k] for k in WEIGHTS], *[out_d[k] for k in WEIGHTS],
            *[out_m[k] for k in WEIGHTS], *[out_v[k] for k in WEIGHTS])
```

```python
import math

import jax
import jax.numpy as jnp
from jax import lax
from jax.experimental import pallas as pl
from jax.experimental.pallas import tpu as pltpu

F32 = jnp.float32
BF16 = jnp.bfloat16
MESH = pl.DeviceIdType.MESH
MESH_AXES = ("x", "y", "c")
N_DEV = 8

D_MODEL = 2048
POOL_WIDTH = 1024
SSM_WIDTH = 1024
N_POOL_GROUPS = 4
POOL_GROUP = 256
SSM_GROUP = 16
N_SSM_GROUPS = 64
SSM_STATE = 64
SSM_FLAT = N_SSM_GROUPS * SSM_STATE
SSM_CHUNKS = 4
CHUNK_IN = SSM_WIDTH // SSM_CHUNKS
CHUNK_STATE = SSM_FLAT // SSM_CHUNKS
PLE_DIM = 256
EPS = 1e-6
A_RE_MAX = -1e-4
ADAM_LR = 0.001
ADAM_B1 = 0.9
ADAM_B2 = 0.999
ADAM_EPS = 1e-08
ADAM_WD = 0.01
ADAM_STEP = 10
GELU_C = math.sqrt(2.0 / math.pi)
GELU_A = 0.044715

SUBLANES = 8
LANES = 128
VMEM_LIMIT_BYTES = 48 * 1024 * 1024

DOT_NN = (((1,), (0,)), ((), ()))
DOT_NT = (((1,), (1,)), ((), ()))
DOT_TN = (((0,), (0,)), ((), ()))


def _tile(n, pref):
    return pref if n % pref == 0 else n


def _params(sem):
    return pltpu.CompilerParams(dimension_semantics=sem, vmem_limit_bytes=VMEM_LIMIT_BYTES)


def _sigmoid(v):
    return 1.0 / (1.0 + jnp.exp(-v))


def _silu_and_grad(v):
    s = _sigmoid(v)
    return v * s, s * (1.0 + v * (1.0 - s))


def _mm(name, pairs, dims, grid, outs, k_steps, extras=(), epilogue=None):
    n_pairs, n_ex, n_out = len(pairs), len(extras), len(outs)
    acc_shape = tuple(d for d in outs[0][2] if d is not None)
    if epilogue is None:
        def epilogue(acc, ex, out_refs):
            out_refs[0][...] = acc.astype(out_refs[0].dtype)

    def body(*refs):
        ab = refs[:2 * n_pairs]
        ex = refs[2 * n_pairs:2 * n_pairs + n_ex]
        out_refs = refs[2 * n_pairs + n_ex:2 * n_pairs + n_ex + n_out]
        acc = refs[-1]
        k = pl.program_id(2)

        @pl.when(k == 0)
        def _():
            acc[...] = jnp.zeros_like(acc)

        part = None
        for q in range(n_pairs):
            d = lax.dot_general(ab[2 * q][...].astype(BF16), ab[2 * q + 1][...].astype(BF16), dims,
                                preferred_element_type=F32)
            part = d if part is None else part + d
        acc[...] += part

        @pl.when(k == k_steps - 1)
        def _():
            epilogue(acc[...], ex, out_refs)

    in_specs, operands = [], []
    for a, a_blk, a_map, b, b_blk, b_map in pairs:
        in_specs += [pl.BlockSpec(a_blk, a_map), pl.BlockSpec(b_blk, b_map)]
        operands += [a, b]
    for e, e_blk, e_map in extras:
        in_specs.append(pl.BlockSpec(e_blk, e_map))
        operands.append(e)
    return pl.pallas_call(
        body, name=name, grid=grid, in_specs=in_specs,
        out_specs=[pl.BlockSpec(o[2], o[3]) for o in outs],
        out_shape=[jax.ShapeDtypeStruct(o[0], o[1]) for o in outs],
        scratch_shapes=[pltpu.VMEM(acc_shape, F32)],
        compiler_params=_params(("arbitrary", "arbitrary", "arbitrary")),
    )(*operands)


def _after(tokens):
    return [(tok, tok.shape, lambda i, j, s: (0, 0)) for tok in tokens]


def _mm_nn(name, a, b, out_dtypes, tm=1024, tn=1024, tk=1024, a_col0=0, extras=(), epilogue=None, after=()):
    m, n = a.shape[0], b.shape[1]
    k = b.shape[0]
    tm, tn, tk = _tile(m, tm), _tile(n, tn), _tile(k, tk)
    outs = [((m, n), dt, (tm, tn), lambda i, j, s: (i, j)) for dt in out_dtypes]
    ex = [(e, (tm, tn), lambda i, j, s: (i, j)) for e in extras] + _after(after)
    return _mm(name, [(a, (tm, tk), lambda i, j, s: (i, a_col0 + s), b, (tk, tn), lambda i, j, s: (s, j))],
               DOT_NN, (m // tm, n // tn, k // tk), outs, k // tk, ex, epilogue)


def _mm_nt(name, a, b, out_dtypes, tm=1024, tn=1024, tk=1024, extras=(), epilogue=None, after=()):
    m, kk = a.shape
    n = b.shape[0]
    tm, tn, tk = _tile(m, tm), _tile(n, tn), _tile(kk, tk)
    outs = [((m, n), dt, (tm, tn), lambda i, j, s: (i, j)) for dt in out_dtypes]
    ex = [(e, (tm, tn), lambda i, j, s: (i, j)) for e in extras] + _after(after)
    return _mm(name, [(a, (tm, tk), lambda i, j, s: (i, s), b, (tn, tk), lambda i, j, s: (j, s))],
               DOT_NT, (m // tm, n // tn, kk // tk), outs, kk // tk, ex, epilogue)


def _mm_tn(name, a, b, out_dtype, tm=512, tn=2048, tk=1024, after=()):
    m, kk = a.shape
    n = b.shape[1]
    tm, tn, tk = _tile(kk, tm), _tile(n, tn), _tile(m, tk)
    outs = [((kk, n), out_dtype, (tm, tn), lambda i, j, s: (i, j))]
    return _mm(name, [(a, (tk, tm), lambda i, j, s: (s, i), b, (tk, tn), lambda i, j, s: (s, j))],
               DOT_TN, (kk // tm, n // tn, m // tk), outs, m // tk, _after(after))[0]


ROW_TILE = 256


def _norm1_in_proj(x, gain, w_in, after):
    t = x.shape[0]
    tm = _tile(t, ROW_TILE)
    n = w_in.shape[1]

    def body(x_ref, g_ref, w_ref, _, hn_ref, proj_ref):
        xv = x_ref[...]
        r = lax.rsqrt(jnp.mean(xv * xv, axis=-1, keepdims=True) + EPS)
        hn = (xv * r * g_ref[...]).astype(BF16)
        hn_ref[...] = hn
        proj_ref[...] = jnp.dot(hn, w_ref[...], preferred_element_type=F32)

    row = pl.BlockSpec((tm, D_MODEL), lambda i: (i, 0))
    return pl.pallas_call(
        body, name="norm1_in_proj", grid=(t // tm,),
        in_specs=[row, pl.BlockSpec((1, D_MODEL), lambda i: (0, 0)), _resident(w_in.shape),
                  pl.BlockSpec(after.shape, lambda i: (0, 0))],
        out_specs=[row, pl.BlockSpec((tm, n), lambda i: (i, 0))],
        out_shape=[jax.ShapeDtypeStruct((t, D_MODEL), BF16), jax.ShapeDtypeStruct((t, n), F32)],
        compiler_params=_params(("arbitrary",)),
    )(x, gain, w_in, after)


def _in_dgrad_norm1_bwd(dproj, w_in, x, dh1, gain, after):
    t = x.shape[0]
    tm = _tile(t, ROW_TILE)

    def body(dp_ref, w_ref, x_ref, dh1_ref, g_ref, _, dx_ref, gg_ref):
        @pl.when(pl.program_id(0) == 0)
        def _():
            gg_ref[...] = jnp.zeros_like(gg_ref)

        dhn = lax.dot_general(dp_ref[...], w_ref[...], DOT_NT, preferred_element_type=F32)
        xv = x_ref[...]
        r = lax.rsqrt(jnp.mean(xv * xv, axis=-1, keepdims=True) + EPS)
        xh = xv * r
        gg_ref[...] += jnp.sum(dhn * xh, axis=0, keepdims=True)
        dxh = dhn * g_ref[...]
        dx_ref[...] = dh1_ref[...] + r * (dxh - xh * jnp.mean(dxh * xh, axis=-1, keepdims=True))

    row = pl.BlockSpec((tm, D_MODEL), lambda i: (i, 0))
    vec = pl.BlockSpec((1, D_MODEL), lambda i: (0, 0))
    return pl.pallas_call(
        body, name="in_dgrad_norm1_bwd", grid=(t // tm,),
        in_specs=[pl.BlockSpec((tm, dproj.shape[1]), lambda i: (i, 0)), _resident(w_in.shape), row, row, vec,
                  pl.BlockSpec(after.shape, lambda i: (0, 0))],
        out_specs=[row, vec],
        out_shape=[jax.ShapeDtypeStruct((t, D_MODEL), F32), jax.ShapeDtypeStruct((1, D_MODEL), F32)],
        compiler_params=_params(("arbitrary",)),
    )(dproj, w_in, x, dh1, gain, after)


def _pool_counts(t, width, group):
    row = lax.broadcasted_iota(jnp.int32, (t, width), 0)
    window = jnp.left_shift(jnp.int32(2), group)
    return row, jnp.minimum(row + 1, window).astype(F32)


def _select_window(group, s2, s4, s8, s16):
    return jnp.where(group == 0, s2, jnp.where(group == 1, s4, jnp.where(group == 2, s8, s16)))


def _pool_mix(proj, w_pool):
    t = proj.shape[0]

    def body(u_ref, w_ref, pooled_ref, mixed_ref):
        group = pl.program_id(0)
        row, count = _pool_counts(t, LANES, group)

        def down(a, j):
            return jnp.where(row >= j, pltpu.roll(a, j, 0), 0.0)

        for h in range(POOL_GROUP // LANES):
            cols = slice(h * LANES, (h + 1) * LANES)
            v = u_ref[:, cols]
            s2 = v + down(v, 1)
            s4 = s2 + down(s2, 2)
            s8 = s4 + down(s4, 4)
            s16 = s8 + down(s8, 8)
            pooled_ref[:, cols] = (_select_window(group, s2, s4, s8, s16) / count - v).astype(BF16)
        mixed_ref[...] = jnp.dot(pooled_ref[...], w_ref[...], preferred_element_type=F32)

    block = pl.BlockSpec((t, POOL_GROUP), lambda g: (0, g))
    return pl.pallas_call(
        body, name="pool_mix", grid=(N_POOL_GROUPS,),
        in_specs=[block, pl.BlockSpec((None, POOL_GROUP, POOL_GROUP), lambda g: (g, 0, 0))],
        out_specs=[block, block],
        out_shape=[jax.ShapeDtypeStruct((t, POOL_WIDTH), BF16), jax.ShapeDtypeStruct((t, POOL_WIDTH), F32)],
        compiler_params=_params(("arbitrary",)),
    )(proj, w_pool)


def _pool_mix_bwd(dmixed, w_pool, dproj):
    t = dmixed.shape[0]

    def body(dm_ref, w_ref, _, o_ref):
        group = pl.program_id(0)
        row, count = _pool_counts(t, LANES, group)

        def up(a, j):
            return jnp.where(row < t - j, pltpu.roll(a, t - j, 0), 0.0)

        dpooled = lax.dot_general(dm_ref[...], w_ref[...], DOT_NT, preferred_element_type=F32)
        for h in range(POOL_GROUP // LANES):
            cols = slice(h * LANES, (h + 1) * LANES)
            dp = dpooled[:, cols]
            r = dp / count
            s2 = r + up(r, 1)
            s4 = s2 + up(s2, 2)
            s8 = s4 + up(s4, 4)
            s16 = s8 + up(s8, 8)
            o_ref[:, cols] = (_select_window(group, s2, s4, s8, s16) - dp).astype(BF16)

    block = pl.BlockSpec((t, POOL_GROUP), lambda g: (0, g))
    return pl.pallas_call(
        body, name="pool_mix_bwd", grid=(N_POOL_GROUPS,),
        in_specs=[block, pl.BlockSpec((None, POOL_GROUP, POOL_GROUP), lambda g: (g, 0, 0)),
                  pl.BlockSpec(memory_space=pl.ANY)],
        out_specs=block,
        out_shape=jax.ShapeDtypeStruct(dproj.shape, dproj.dtype),
        input_output_aliases={2: 0},
        compiler_params=_params(("arbitrary",)),
    )(dmixed, w_pool, dproj)


def _gate_out_proj(mixed, proj, hg, pool_scale, w_out, x):
    t = mixed.shape[0]
    tm = _tile(t, ROW_TILE)

    def body(mx_ref, ga_ref, gb_ref, hg_ref, ps_ref, w_ref, x_ref, cat_ref, h1_ref, h1b_ref):
        silu_a, _ = _silu_and_grad(ga_ref[...])
        cat_ref[:, :POOL_WIDTH] = (mx_ref[...] * ps_ref[...] * silu_a).astype(BF16)
        silu_b, _ = _silu_and_grad(gb_ref[...])
        sb = hg_ref[:, :SSM_WIDTH] * _sigmoid(hg_ref[:, SSM_WIDTH:])
        cat_ref[:, POOL_WIDTH:] = (sb * silu_b).astype(BF16)
        h1 = x_ref[...] + jnp.dot(cat_ref[...], w_ref[...], preferred_element_type=F32)
        h1_ref[...] = h1
        h1b_ref[...] = h1.astype(BF16)

    row = pl.BlockSpec((tm, D_MODEL), lambda i: (i, 0))
    return pl.pallas_call(
        body, name="gate_out_proj", grid=(t // tm,),
        in_specs=[pl.BlockSpec((tm, POOL_WIDTH), lambda i: (i, 0)),
                  pl.BlockSpec((tm, POOL_WIDTH), lambda i: (i, 1)),
                  pl.BlockSpec((tm, SSM_WIDTH), lambda i: (i, 3)),
                  pl.BlockSpec((tm, 2 * SSM_WIDTH), lambda i: (i, 0)),
                  pl.BlockSpec((1, POOL_WIDTH), lambda i: (0, 0)), _resident(w_out.shape), row],
        out_specs=[row, row, row],
        out_shape=[jax.ShapeDtypeStruct((t, D_MODEL), BF16), jax.ShapeDtypeStruct((t, D_MODEL), F32),
                   jax.ShapeDtypeStruct((t, D_MODEL), BF16)],
        compiler_params=_params(("arbitrary",)),
    )(mixed, proj, proj, hg, pool_scale, w_out, x)


def _residual_dgrad(name, dy, w, residual):
    t = dy.shape[0]
    tm = _tile(t, 2 * ROW_TILE)
    n = w.shape[0]

    def body(dy_ref, w_ref, r_ref, o_ref, ob_ref):
        o = r_ref[...] + lax.dot_general(dy_ref[...], w_ref[...], DOT_NT, preferred_element_type=F32)
        o_ref[...] = o
        ob_ref[...] = o.astype(BF16)

    out = pl.BlockSpec((tm, n), lambda i: (i, 0))
    return pl.pallas_call(
        body, name=name, grid=(t // tm,),
        in_specs=[pl.BlockSpec((tm, dy.shape[1]), lambda i: (i, 0)), _resident(w.shape), out],
        out_specs=[out, out],
        out_shape=[jax.ShapeDtypeStruct((t, n), F32), jax.ShapeDtypeStruct((t, n), BF16)],
        compiler_params=_params(("arbitrary",)),
    )(dy, w, residual)


def _out_dgrad_gate_bwd(dh1b, w_out, mixed, proj, hg, pool_scale, after):
    t = mixed.shape[0]
    tm = _tile(t, ROW_TILE)
    n_after = len(after)

    def body(dh_ref, w_ref, mx_ref, ga_ref, gb_ref, hg_ref, ps_ref, *rest):
        dmx_ref, dp_ref, dhg_ref, gps_ref = rest[n_after:]

        @pl.when(pl.program_id(0) == 0)
        def _():
            gps_ref[...] = jnp.zeros_like(gps_ref)

        dcat = lax.dot_general(dh_ref[...], w_ref[...], DOT_NT, preferred_element_type=F32)
        ps = ps_ref[...]
        mx = mx_ref[...]
        dya = dcat[:, :POOL_WIDTH]
        silu_a, dsilu_a = _silu_and_grad(ga_ref[...])
        dpa = dya * silu_a
        gps_ref[...] += jnp.sum(dpa * mx, axis=0, keepdims=True)
        dmx_ref[...] = (dpa * ps).astype(BF16)
        dp_ref[:, :POOL_WIDTH] = jnp.zeros((tm, POOL_WIDTH), BF16)
        dp_ref[:, POOL_WIDTH:2 * POOL_WIDTH] = (dya * mx * ps * dsilu_a).astype(BF16)

        dyb = dcat[:, POOL_WIDTH:]
        silu_b, dsilu_b = _silu_and_grad(gb_ref[...])
        h_a = hg_ref[:, :SSM_WIDTH]
        sg = _sigmoid(hg_ref[:, SSM_WIDTH:])
        dsb = dyb * silu_b
        dp_ref[:, 2 * POOL_WIDTH:2 * POOL_WIDTH + SSM_WIDTH] = jnp.zeros((tm, SSM_WIDTH), BF16)
        dp_ref[:, 2 * POOL_WIDTH + SSM_WIDTH:] = (dyb * h_a * sg * dsilu_b).astype(BF16)
        dhg_ref[:, :SSM_WIDTH] = (dsb * sg).astype(BF16)
        dhg_ref[:, SSM_WIDTH:] = (dsb * h_a * sg * (1.0 - sg)).astype(BF16)

    half = pl.BlockSpec((tm, POOL_WIDTH), lambda i: (i, 0))
    full = pl.BlockSpec((tm, D_MODEL), lambda i: (i, 0))
    vec = pl.BlockSpec((1, POOL_WIDTH), lambda i: (0, 0))
    proj_width = 2 * POOL_WIDTH + 2 * SSM_WIDTH
    return pl.pallas_call(
        body, name="out_dgrad_gate_bwd", grid=(t // tm,),
        in_specs=[full, _resident(w_out.shape), half,
                  pl.BlockSpec((tm, POOL_WIDTH), lambda i: (i, 1)),
                  pl.BlockSpec((tm, SSM_WIDTH), lambda i: (i, 3)),
                  full, vec] + [pl.BlockSpec(tok.shape, lambda i: (0, 0)) for tok in after],
        out_specs=[half, pl.BlockSpec((tm, proj_width), lambda i: (i, 0)), full, vec],
        out_shape=[jax.ShapeDtypeStruct((t, POOL_WIDTH), BF16), jax.ShapeDtypeStruct((t, proj_width), BF16),
                   jax.ShapeDtypeStruct((t, 2 * SSM_WIDTH), BF16),
                   jax.ShapeDtypeStruct((1, POOL_WIDTH), F32)],
        compiler_params=_params(("arbitrary",)),
    )(dh1b, w_out, mixed, proj, proj, hg, pool_scale, *after)


def _ple_final(h1, h1b, p, w_gate, w_ple, target, gain):
    t = h1.shape[0]
    tm = _tile(t, 256)

    def body(h1_ref, h1b_ref, p_ref, wg_ref, wp_ref, tg_ref, g_ref, de_ref, dq_ref, dh2_ref, gg_ref, loss_ref):
        @pl.when(pl.program_id(0) == 0)
        def _():
            gg_ref[...] = jnp.zeros_like(gg_ref)
            loss_ref[...] = jnp.zeros_like(loss_ref)

        ev = jnp.dot(p_ref[...].astype(BF16), wp_ref[...], preferred_element_type=F32)
        sg = _sigmoid(jnp.dot(h1b_ref[...], wg_ref[...], preferred_element_type=F32))
        h2 = h1_ref[...] + ev * sg
        r = lax.rsqrt(jnp.mean(h2 * h2, axis=-1, keepdims=True) + EPS)
        n = h2 * r
        gain_v = g_ref[...]
        diff = n * gain_v - tg_ref[...]
        row_loss = jnp.sum(diff * diff, axis=-1, keepdims=True)
        loss_ref[...] += (0.5 / D_MODEL) * jnp.sum(row_loss, axis=0, keepdims=True)
        dout = diff * (1.0 / D_MODEL)
        gg_ref[...] += jnp.sum(dout * n, axis=0, keepdims=True)
        dn = dout * gain_v
        dh2 = r * (dn - n * jnp.mean(dn * n, axis=-1, keepdims=True))
        dh2_ref[...] = dh2
        de_ref[...] = (dh2 * sg).astype(BF16)
        dq_ref[...] = (dh2 * ev * sg * (1.0 - sg)).astype(BF16)

    row = pl.BlockSpec((tm, D_MODEL), lambda i: (i, 0))
    vec = pl.BlockSpec((1, D_MODEL), lambda i: (0, 0))
    return pl.pallas_call(
        body, name="ple_final", grid=(t // tm,),
        in_specs=[row, row, pl.BlockSpec((tm, PLE_DIM), lambda i: (i, 0)), _resident((D_MODEL, D_MODEL)),
                  _resident((PLE_DIM, D_MODEL)), row, vec],
        out_specs=[row, row, row, vec, pl.BlockSpec((1, 1), lambda i: (0, 0))],
        out_shape=[jax.ShapeDtypeStruct((t, D_MODEL), BF16), jax.ShapeDtypeStruct((t, D_MODEL), BF16),
                   jax.ShapeDtypeStruct((t, D_MODEL), F32), jax.ShapeDtypeStruct((1, D_MODEL), F32),
                   jax.ShapeDtypeStruct((1, 1), F32)],
        compiler_params=_params(("arbitrary",)),
    )(h1, h1b, p, w_gate, w_ple, target, gain)


def _zoh(a_re, a_im, log_dt, b_re_t, b_im_t):
    lam_re = jnp.minimum(a_re, A_RE_MAX)
    lam_im = a_im
    dt = jnp.exp(log_dt)
    mag = jnp.exp(lam_re * dt)
    ang = lam_im * dt
    ab_re = mag * jnp.cos(ang)
    ab_im = mag * jnp.sin(ang)
    den = lam_re * lam_re + lam_im * lam_im
    n_re = ab_re - 1.0
    n_im = ab_im
    q_re = (n_re * lam_re + n_im * lam_im) / den
    q_im = (n_im * lam_re - n_re * lam_im) / den
    bb_re = q_re[:, None, :] * b_re_t - q_im[:, None, :] * b_im_t
    bb_im = q_re[:, None, :] * b_im_t + q_im[:, None, :] * b_re_t
    return ab_re, ab_im, bb_re, bb_im


def _ssm_params(a_re, a_im, log_dt, b_re_t, b_im_t):
    def body(are_ref, aim_ref, dt_ref, bre_ref, bim_ref, abre_ref, abim_ref, bbre_ref, bbim_ref):
        ab_re, ab_im, bb_re, bb_im = _zoh(are_ref[...], aim_ref[...], dt_ref[...], bre_ref[...], bim_ref[...])
        abre_ref[...] = ab_re
        abim_ref[...] = ab_im
        bbre_ref[...] = bb_re
        bbim_ref[...] = bb_im

    return pl.pallas_call(
        body, name="ssm_params",
        out_shape=[jax.ShapeDtypeStruct(a_re.shape, F32), jax.ShapeDtypeStruct(a_re.shape, F32),
                   jax.ShapeDtypeStruct(b_re_t.shape, F32), jax.ShapeDtypeStruct(b_re_t.shape, F32)],
        compiler_params=_params(None),
    )(a_re, a_im, log_dt, b_re_t, b_im_t)


def _ssm_params_bwd(a_re, a_im, log_dt, b_re_t, b_im_t, g_ab_re, g_ab_im, g_bb_re, g_bb_im):
    def body(are_ref, aim_ref, dt_ref, bre_ref, bim_ref, gar_ref, gai_ref, gbr_ref, gbi_ref,
             o_are, o_aim, o_dt, o_bre, o_bim):
        _, vjp = jax.vjp(_zoh, are_ref[...], aim_ref[...], dt_ref[...], bre_ref[...], bim_ref[...])
        d_are, d_aim, d_dt, d_bre, d_bim = vjp((gar_ref[...], gai_ref[...], gbr_ref[...], gbi_ref[...]))
        o_are[...] = d_are
        o_aim[...] = d_aim
        o_dt[...] = d_dt
        o_bre[...] = d_bre
        o_bim[...] = d_bim

    ins = (a_re, a_im, log_dt, b_re_t, b_im_t)
    return pl.pallas_call(
        body, name="ssm_params_bwd",
        out_shape=[jax.ShapeDtypeStruct(v.shape, F32) for v in ins],
        compiler_params=_params(None),
    )(*ins, g_ab_re, g_ab_im, g_bb_re, g_bb_im)


CHUNK_TILES = CHUNK_STATE // LANES
CH_PER_TILE = CHUNK_IN // CHUNK_TILES
PAIR = 2 * LANES
SSM_ROWS = 256
SCAN_STEPS = 8
U_COLUMN_BLOCK = 2 * POOL_WIDTH // SSM_WIDTH


def _own_half():
    r = lax.broadcasted_iota(jnp.int32, (CHUNK_IN, LANES), 0) // SSM_GROUP % 2
    c = lax.broadcasted_iota(jnp.int32, (CHUNK_IN, LANES), 1) // SSM_STATE
    return (r == c)[None]


def _compact_weight(w):
    tiled = jnp.tile(w.reshape(SSM_CHUNKS, CHUNK_IN, SSM_STATE), (1, 1, 2))
    return jnp.where(_own_half(), tiled, 0.0)


def _compact_pair(w_a, w_b):
    return jnp.concatenate([_compact_weight(w_a), _compact_weight(w_b)], axis=-1).astype(BF16)


def _expand_grad(g):
    kept = jnp.where(_own_half(), g, 0.0)
    return kept.reshape(SSM_CHUNKS, CHUNK_IN, 2, SSM_STATE).sum(axis=2).reshape(N_SSM_GROUPS, SSM_GROUP, SSM_STATE)


TILES_PER_BLOCK = LANES // CH_PER_TILE
IN_BLOCKS = CHUNK_IN // LANES


def _tile_masks():
    j = lax.broadcasted_iota(jnp.int32, (CHUNK_TILES, LANES), 0) % TILES_PER_BLOCK
    lane = lax.broadcasted_iota(jnp.int32, (CHUNK_TILES, LANES), 1) // CH_PER_TILE
    return (j == lane).astype(F32)


def _tile_rows(ref, j, tt):
    return ref.at[j // TILES_PER_BLOCK, pl.ds(j, tt, stride=CHUNK_TILES), :]


def _spread(ref, v, masks):
    tt = v.shape[0]
    for j in range(CHUNK_TILES):
        block = LANES * (j // TILES_PER_BLOCK)
        _tile_rows(ref, j, tt)[...] = v[:, block:block + LANES] * masks[j:j + 1, :]
    return jnp.concatenate([ref[b] for b in range(IN_BLOCKS)], axis=1).astype(BF16)


def _gather(ref, full, masks):
    tt = full.shape[0] // CHUNK_TILES
    for b in range(IN_BLOCKS):
        ref[b] = full[:, b * LANES:(b + 1) * LANES]
    out = []
    for b in range(IN_BLOCKS):
        acc = None
        for j in range(b * TILES_PER_BLOCK, (b + 1) * TILES_PER_BLOCK):
            part = _tile_rows(ref, j, tt)[...] * masks[j:j + 1, :]
            acc = part if acc is None else acc + part
        out.append(acc)
    return jnp.concatenate(out, axis=1)


def _resident(shape):
    return pl.BlockSpec(shape, lambda i: (0,) * len(shape), pipeline_mode=pl.Buffered(1))


def _halves(ref, k, rows=slice(None)):
    return ref[k, rows, :LANES], ref[k, rows, LANES:]


def _ssm_fwd(proj, w2, c2, a2, d_skip):
    t = proj.shape[0]
    tt = _tile(t, SSM_ROWS)
    rows = tt * CHUNK_TILES

    def body(u_ref, w_ref, c_ref, a_ref, d_ref, y_ref, gel_ref, s_ref, carry, spread_ref, full_ref):
        @pl.when(pl.program_id(0) == 0)
        def _():
            carry[...] = jnp.zeros_like(carry)
            spread_ref[...] = jnp.zeros_like(spread_ref)

        mask = _tile_masks()
        u = u_ref[...]
        for k in range(SSM_CHUNKS):
            uk = _spread(spread_ref, u[:, k * CHUNK_IN:(k + 1) * CHUNK_IN], mask)
            s_ref[k] = jnp.dot(uk, w_ref[k], preferred_element_type=F32)

        abar = [_halves(a_ref, k) for k in range(SSM_CHUNKS)]

        def steps(i, state):
            for v in range(SCAN_STEPS):
                r = pl.ds(pl.multiple_of((i * SCAN_STEPS + v) * CHUNK_TILES, CHUNK_TILES), CHUNK_TILES)
                new = []
                for k, ((a_re, a_im), (s_re, s_im)) in enumerate(zip(abar, state)):
                    b_re, b_im = _halves(s_ref, k, r)
                    s_re, s_im = a_re * s_re - a_im * s_im + b_re, a_re * s_im + a_im * s_re + b_im
                    s_ref[k, r, :LANES] = s_re
                    s_ref[k, r, LANES:] = s_im
                    new.append((s_re, s_im))
                state = tuple(new)
            return state

        state = lax.fori_loop(0, tt // SCAN_STEPS, steps, tuple(_halves(carry, k) for k in range(SSM_CHUNKS)))
        for k, (s_re, s_im) in enumerate(state):
            carry[k, :, :LANES] = s_re
            carry[k, :, LANES:] = s_im

        for k in range(SSM_CHUNKS):
            cols = slice(k * CHUNK_IN, (k + 1) * CHUNK_IN)
            full = lax.dot_general(s_ref[k].astype(BF16), c_ref[k], DOT_NT, preferred_element_type=F32)
            y = _gather(full_ref, full, mask) + d_ref[:, cols] * u[:, cols]
            y_ref[:, cols] = y
            gel_ref[:, cols] = (0.5 * y * (1.0 + jnp.tanh(GELU_C * (y + GELU_A * y * y * y)))).astype(BF16)

    weight = _resident((SSM_CHUNKS, CHUNK_IN, PAIR))
    tokens = pl.BlockSpec((tt, SSM_WIDTH), lambda i: (i, 0))
    return pl.pallas_call(
        body, name="ssm_fwd", grid=(t // tt,),
        in_specs=[pl.BlockSpec((tt, SSM_WIDTH), lambda i: (i, U_COLUMN_BLOCK)), weight, weight,
                  _resident((SSM_CHUNKS, CHUNK_TILES, PAIR)), _resident((1, SSM_WIDTH))],
        out_specs=[tokens, tokens, pl.BlockSpec((SSM_CHUNKS, rows, PAIR), lambda i: (0, i, 0))],
        out_shape=[jax.ShapeDtypeStruct((t, SSM_WIDTH), F32), jax.ShapeDtypeStruct((t, SSM_WIDTH), BF16),
                   jax.ShapeDtypeStruct((SSM_CHUNKS, t * CHUNK_TILES, PAIR), F32)],
        scratch_shapes=[pltpu.VMEM((SSM_CHUNKS, CHUNK_TILES, PAIR), F32), pltpu.VMEM((IN_BLOCKS, rows, LANES), F32),
                        pltpu.VMEM((IN_BLOCKS, rows, LANES), F32)],
        compiler_params=_params(("arbitrary",)),
    )(proj, w2, c2, a2, d_skip)


def _ssm_bwd(dy, proj, s, w2, c2, a2, d_skip, dproj):
    t = dy.shape[0]
    tt = _tile(t, SSM_ROWS)
    rows = tt * CHUNK_TILES
    n_chunks = t // tt

    def body(dy_ref, u_ref, s_ref, w_ref, c_ref, a_ref, d_ref, _, du_ref, gc_ref, gw_ref, ga_ref, gd_ref, z_ref, carry,
             spread_ref, full_ref):
        @pl.when(pl.program_id(0) == 0)
        def _():
            for r in (carry, gc_ref, gw_ref, ga_ref, gd_ref, spread_ref):
                r[...] = jnp.zeros_like(r)

        mask = _tile_masks()
        dy_v = dy_ref[...]
        u = u_ref[...]
        gd_ref[...] += jnp.sum(dy_v * u, axis=0, keepdims=True)
        for k in range(SSM_CHUNKS):
            dk = _spread(spread_ref, dy_v[:, k * CHUNK_IN:(k + 1) * CHUNK_IN], mask)
            z_ref[k] = jnp.dot(dk, c_ref[k], preferred_element_type=F32)
            gc_ref[k] += lax.dot_general(dk, s_ref[k].astype(BF16), DOT_TN, preferred_element_type=F32)

        abar = [_halves(a_ref, k) for k in range(SSM_CHUNKS)]

        def steps(i, state):
            zs, gs = state
            for v in range(SCAN_STEPS):
                tok = tt - 1 - (i * SCAN_STEPS + v)
                r = pl.ds(pl.multiple_of(tok * CHUNK_TILES, CHUNK_TILES), CHUNK_TILES)
                new_z, new_g = [], []
                for k, ((a_re, a_im), (z_re, z_im), (g_re, g_im)) in enumerate(zip(abar, zs, gs)):
                    s_re, s_im = _halves(s_ref, k, r)
                    g_re = g_re + z_re * s_re + z_im * s_im
                    g_im = g_im + z_im * s_re - z_re * s_im
                    d_re, d_im = _halves(z_ref, k, r)
                    z_re, z_im = d_re + a_re * z_re + a_im * z_im, d_im + a_re * z_im - a_im * z_re
                    z_ref[k, r, :LANES] = z_re
                    z_ref[k, r, LANES:] = z_im
                    new_z.append((z_re, z_im))
                    new_g.append((g_re, g_im))
                zs, gs = tuple(new_z), tuple(new_g)
            return zs, gs

        zs, gs = lax.fori_loop(0, tt // SCAN_STEPS, steps,
                               (tuple(_halves(carry, k) for k in range(SSM_CHUNKS)),
                                tuple(_halves(ga_ref, k) for k in range(SSM_CHUNKS))))
        for k in range(SSM_CHUNKS):
            carry[k, :, :LANES], carry[k, :, LANES:] = zs[k]
            ga_ref[k, :, :LANES], ga_ref[k, :, LANES:] = gs[k]

        for k in range(SSM_CHUNKS):
            cols = slice(k * CHUNK_IN, (k + 1) * CHUNK_IN)
            zb = z_ref[k].astype(BF16)
            full = lax.dot_general(zb, w_ref[k], DOT_NT, preferred_element_type=F32)
            du_ref[:, cols] = (_gather(full_ref, full, mask) + d_ref[:, cols] * dy_v[:, cols]).astype(BF16)
            uk = _spread(spread_ref, u[:, cols], mask)
            gw_ref[k] += lax.dot_general(uk, zb, DOT_TN, preferred_element_type=F32)

    weight = _resident((SSM_CHUNKS, CHUNK_IN, PAIR))
    tokens = pl.BlockSpec((tt, SSM_WIDTH), lambda i: (n_chunks - 1 - i, 0))
    grad = pl.BlockSpec((SSM_CHUNKS, CHUNK_IN, PAIR), lambda i: (0, 0, 0))
    return pl.pallas_call(
        body, name="ssm_bwd", grid=(n_chunks,),
        in_specs=[tokens, pl.BlockSpec((tt, SSM_WIDTH), lambda i: (n_chunks - 1 - i, U_COLUMN_BLOCK)),
                  pl.BlockSpec((SSM_CHUNKS, rows, PAIR), lambda i: (0, n_chunks - 1 - i, 0)), weight, weight,
                  _resident((SSM_CHUNKS, CHUNK_TILES, PAIR)), _resident((1, SSM_WIDTH)),
                  pl.BlockSpec(memory_space=pl.ANY)],
        out_specs=[pl.BlockSpec((tt, SSM_WIDTH), lambda i: (n_chunks - 1 - i, U_COLUMN_BLOCK)), grad, grad,
                   pl.BlockSpec((SSM_CHUNKS, CHUNK_TILES, PAIR), lambda i: (0, 0, 0)),
                   pl.BlockSpec((1, SSM_WIDTH), lambda i: (0, 0))],
        out_shape=[jax.ShapeDtypeStruct(dproj.shape, dproj.dtype), jax.ShapeDtypeStruct((SSM_CHUNKS, CHUNK_IN, PAIR), F32),
                   jax.ShapeDtypeStruct((SSM_CHUNKS, CHUNK_IN, PAIR), F32),
                   jax.ShapeDtypeStruct((SSM_CHUNKS, CHUNK_TILES, PAIR), F32), jax.ShapeDtypeStruct((1, SSM_WIDTH), F32)],
        input_output_aliases={7: 0},
        scratch_shapes=[pltpu.VMEM((SSM_CHUNKS, rows, PAIR), F32), pltpu.VMEM((SSM_CHUNKS, CHUNK_TILES, PAIR), F32),
                        pltpu.VMEM((IN_BLOCKS, rows, LANES), F32), pltpu.VMEM((IN_BLOCKS, rows, LANES), F32)],
        compiler_params=_params(("arbitrary",)),
    )(dy, proj, s, w2, c2, a2, d_skip, dproj)


def _block(ref, axis, size, index):
    idx = [slice(None)] * len(ref.shape)
    idx[axis] = pl.ds(pl.multiple_of(index * size, size), size)
    return ref.at[tuple(idx)]


def _all_gather(name, shards, axes):
    n = len(shards)
    sizes = [s.shape[a] for s, a in zip(shards, axes)]

    def body(*refs):
        ins, outs = refs[:n], refs[n:2 * n]
        send_sems, recv_sems, local_sems = refs[2 * n:]
        x, y, c = (lax.axis_index(a) for a in MESH_AXES)
        me, sibling = (x, y, c), (x, y, 1 - c)
        chips = [(1 - x, y), (x, 1 - y), (1 - x, 1 - y)]

        def rows(i, dev):
            return _block(outs[i], axes[i], sizes[i], 4 * dev[0] + 2 * dev[1] + dev[2])

        def copy(i, k, block, to, src=None):
            return pltpu.make_async_remote_copy(
                src_ref=rows(i, block) if src is None else src, dst_ref=rows(i, block),
                send_sem=send_sems.at[7 * i + k], recv_sem=recv_sems.at[7 * i + k],
                device_id=to, device_id_type=MESH)

        mine = [pltpu.make_async_copy(ins[i], rows(i, me), local_sems.at[i]) for i in range(n)]
        for cp in mine:
            cp.start()
        first = []
        for i in range(n):
            first.append(copy(i, 0, me, sibling, src=ins[i]))
            first += [copy(i, 1 + j, me, (*chip, c), src=ins[i]) for j, chip in enumerate(chips)]
        for cp in first:
            cp.start()
        passed = []
        for i in range(n):
            for j, chip in enumerate(chips):
                copy(i, 1 + j, (*chip, c), me).wait_recv()
                fwd = copy(i, 4 + j, (*chip, c), sibling)
                fwd.start()
                passed.append(fwd)
        for i in range(n):
            copy(i, 0, sibling, me).wait_recv()
            for j, chip in enumerate(chips):
                copy(i, 4 + j, (*chip, 1 - c), me).wait_recv()
        for cp in first + passed:
            cp.wait_send()
        for cp in mine:
            cp.wait()

    out_shape = []
    for s, a in zip(shards, axes):
        shape = list(s.shape)
        shape[a] *= N_DEV
        out_shape.append(jax.ShapeDtypeStruct(tuple(shape), s.dtype))
    any_spec = pl.BlockSpec(memory_space=pl.ANY)
    return pl.pallas_call(
        body, name=name, out_shape=out_shape,
        in_specs=[any_spec] * n, out_specs=[any_spec] * n,
        scratch_shapes=[pltpu.SemaphoreType.DMA((7 * n,)), pltpu.SemaphoreType.DMA((7 * n,)),
                        pltpu.SemaphoreType.DMA((n,))],
    )(*shards)


HBM_SPEC = pl.BlockSpec(memory_space=pltpu.HBM)
SEM_SPEC = pl.BlockSpec(memory_space=pltpu.SEMAPHORE)
ANY_SPEC = pl.BlockSpec(memory_space=pl.ANY)
SPLIT_PARAMS = pltpu.CompilerParams(has_side_effects=pltpu.SideEffectType.DATAFLOW_SIDE_EFFECTING)
N_PEERS = N_DEV - 1
TOKEN = jax.ShapeDtypeStruct((SUBLANES, LANES), F32)
VMEM_SPEC = pl.BlockSpec(memory_space=pltpu.VMEM)


def _in_hbm(arrays):
    return [pltpu.with_memory_space_constraint(a, pltpu.HBM) for a in arrays]


def _peer(m):
    x, y, c = (lax.axis_index(a) for a in MESH_AXES)
    px = 1 - x if m & 4 else x
    py = 1 - y if m & 2 else y
    pc = 1 - c if m & 1 else c
    return (px, py, pc), 4 * px + 2 * py + pc


def _my_index():
    x, y, c = (lax.axis_index(a) for a in MESH_AXES)
    return 4 * x + 2 * y + c


def _gather_copies(shard_refs, full_refs, axes, send_sems, recv_sems):
    copies = []
    for i, (shard, full) in enumerate(zip(shard_refs, full_refs)):
        mine = _block(full, axes[i], shard.shape[axes[i]], _my_index())
        for m in range(1, N_DEV):
            peer, _ = _peer(m)
            copies.append(pltpu.make_async_remote_copy(
                src_ref=shard, dst_ref=mine, send_sem=send_sems.at[N_PEERS * i + m - 1],
                recv_sem=recv_sems.at[N_PEERS * i + m - 1], device_id=peer, device_id_type=MESH))
    return copies


def _gather_start(name, shards, axes, after):
    n = len(shards)

    def body(*refs):
        shard_refs = refs[:n]
        send_sems, recv_sems, local_sems = refs[n + 1:n + 4]
        full_refs = refs[2 * n + 4:3 * n + 4]
        refs[3 * n + 4][...] = jnp.zeros(TOKEN.shape, TOKEN.dtype)
        for i in range(n):
            pltpu.make_async_copy(shard_refs[i], _block(full_refs[i], axes[i], shard_refs[i].shape[axes[i]], _my_index()),
                                  local_sems.at[i]).start()
        for cp in _gather_copies(shard_refs, full_refs, axes, send_sems, recv_sems):
            cp.start()

    fulls = []
    for s, a in zip(shards, axes):
        shape = list(s.shape)
        shape[a] *= N_DEV
        fulls.append(pltpu.HBM(tuple(shape), s.dtype))
    out = pl.pallas_call(
        body, name=name,
        out_shape=(pltpu.SemaphoreType.DMA((N_PEERS * n,)), pltpu.SemaphoreType.DMA((N_PEERS * n,)),
                   pltpu.SemaphoreType.DMA((n,)), *[pltpu.HBM(s.shape, s.dtype) for s in shards], *fulls, TOKEN),
        in_specs=[HBM_SPEC] * n + [ANY_SPEC],
        out_specs=(SEM_SPEC, SEM_SPEC, SEM_SPEC, *[HBM_SPEC] * (2 * n), VMEM_SPEC),
        input_output_aliases={i: 3 + i for i in range(n)},
        compiler_params=SPLIT_PARAMS,
    )(*_in_hbm(shards), after)
    return out[:-1], out[-1]


def _gather_wait(name, started, indices, axes, after):
    send_sems, recv_sems, local_sems = started[:3]
    n_all = (len(started) - 3) // 2
    shards = [started[3 + i] for i in indices]
    fulls = [started[3 + n_all + i] for i in indices]
    n = len(indices)

    def body(*refs):
        shard_refs, full_refs = refs[:n], refs[n:2 * n]
        send_sems, recv_sems, local_sems = refs[2 * n:2 * n + 3]
        for j, i in enumerate(indices):
            mine = _block(full_refs[j], axes[j], shard_refs[j].shape[axes[j]], _my_index())
            pltpu.make_async_copy(shard_refs[j], mine, local_sems.at[i]).wait()
            for m in range(1, N_DEV):
                peer, _ = _peer(m)
                cp = pltpu.make_async_remote_copy(
                    src_ref=shard_refs[j], dst_ref=mine, send_sem=send_sems.at[N_PEERS * i + m - 1],
                    recv_sem=recv_sems.at[N_PEERS * i + m - 1], device_id=peer, device_id_type=MESH)
                cp.wait_send()
                cp.wait_recv()

    out = pl.pallas_call(
        body, name=name,
        out_shape=tuple(pltpu.HBM(a.shape, a.dtype) for a in shards + fulls),
        in_specs=[HBM_SPEC] * (2 * n) + [SEM_SPEC] * 3 + [ANY_SPEC], out_specs=tuple([HBM_SPEC] * (2 * n)),
        input_output_aliases={i: i for i in range(2 * n)},
        compiler_params=SPLIT_PARAMS,
    )(*shards, *fulls, send_sems, recv_sems, local_sems, after)
    return out[n:]


def _exchange_start(name, fulls, axes):
    n = len(fulls)
    sizes = [f.shape[a] // N_DEV for f, a in zip(fulls, axes)]

    def body(*refs):
        ins = refs[:n]
        send_sems, recv_sems = refs[n:n + 2]
        lands = refs[2 * n + 2:3 * n + 2]
        refs[3 * n + 2][...] = jnp.zeros(TOKEN.shape, TOKEN.dtype)
        for i in range(n):
            for m in range(1, N_DEV):
                peer, index = _peer(m)
                pltpu.make_async_remote_copy(
                    src_ref=_block(ins[i], axes[i], sizes[i], index), dst_ref=lands[i].at[m - 1],
                    send_sem=send_sems.at[N_PEERS * i + m - 1], recv_sem=recv_sems.at[N_PEERS * i + m - 1],
                    device_id=peer, device_id_type=MESH).start()

    lands = []
    for f, a, size in zip(fulls, axes, sizes):
        shape = list(f.shape)
        shape[a] = size
        lands.append(pltpu.HBM((N_PEERS, *shape), f.dtype))
    out = pl.pallas_call(
        body, name=name,
        out_shape=(pltpu.SemaphoreType.DMA((N_PEERS * n,)), pltpu.SemaphoreType.DMA((N_PEERS * n,)),
                   *[pltpu.HBM(f.shape, f.dtype) for f in fulls], *lands, TOKEN),
        in_specs=[HBM_SPEC] * n, out_specs=(SEM_SPEC, SEM_SPEC, *[HBM_SPEC] * (2 * n), VMEM_SPEC),
        input_output_aliases={i: 2 + i for i in range(n)},
        compiler_params=SPLIT_PARAMS,
    )(*_in_hbm(fulls))
    return out[:-1], out[-1]


def _exchange_wait(name, started, axes, after):
    send_sems, recv_sems = started[:2]
    n = (len(started) - 2) // 2
    fulls, lands = list(started[2:2 + n]), list(started[2 + n:])
    sizes = [f.shape[a] // N_DEV for f, a in zip(fulls, axes)]

    def body(*refs):
        ins, land_refs = refs[:n], refs[n:2 * n]
        send_sems, recv_sems = refs[2 * n:2 * n + 2]
        for i in range(n):
            for m in range(1, N_DEV):
                peer, index = _peer(m)
                cp = pltpu.make_async_remote_copy(
                    src_ref=_block(ins[i], axes[i], sizes[i], index), dst_ref=land_refs[i].at[m - 1],
                    send_sem=send_sems.at[N_PEERS * i + m - 1], recv_sem=recv_sems.at[N_PEERS * i + m - 1],
                    device_id=peer, device_id_type=MESH)
                cp.wait_send()
                cp.wait_recv()

    out = pl.pallas_call(
        body, name=name,
        out_shape=tuple(pltpu.HBM(a.shape, a.dtype) for a in fulls + lands),
        in_specs=[HBM_SPEC] * (2 * n) + [SEM_SPEC] * 2 + [ANY_SPEC], out_specs=tuple([HBM_SPEC] * (2 * n)),
        input_output_aliases={i: i for i in range(2 * n)},
        compiler_params=SPLIT_PARAMS,
    )(*fulls, *lands, send_sems, recv_sems, after)
    return out[:n], out[n:]


def _sum_parts(part_refs, ndim):
    g = None
    for p_ref in part_refs:
        stacked = len(p_ref.shape) > ndim
        terms = [p_ref[s] for s in range(p_ref.shape[0])] if stacked else [p_ref[...]]
        for term in terms:
            term = term.astype(F32)
            g = term if g is None else g + term
    return g


def _adamw_update(w_ref, m_ref, v_ref, g, g_ref, d_ref, nm_ref, nv_ref):
    c1 = 1.0 - ADAM_B1 ** ADAM_STEP
    c2 = 1.0 - ADAM_B2 ** ADAM_STEP
    new_m = ADAM_B1 * m_ref[...] + (1.0 - ADAM_B1) * g
    new_v = ADAM_B2 * v_ref[...] + (1.0 - ADAM_B2) * (g * g)
    g_ref[...] = g
    nm_ref[...] = new_m
    nv_ref[...] = new_v
    d_ref[...] = -ADAM_LR * ((new_m / c1) / (jnp.sqrt(new_v / c2) + ADAM_EPS) + ADAM_WD * w_ref[...])


def _adamw_small(ws, ms, vs, stacks, loss_stack):
    n = len(ws)

    def body(*refs):
        ins, outs = refs[:4 * n + 1], refs[4 * n + 1:]
        for i in range(n):
            _adamw_update(ins[i], ins[n + i], ins[2 * n + i], _sum_parts([ins[3 * n + i]], len(ins[i].shape)),
                          outs[i], outs[n + i], outs[2 * n + i], outs[3 * n + i])
        total = ins[4 * n][0]
        for dev in range(1, N_DEV):
            total = total + ins[4 * n][dev]
        outs[4 * n][...] = total

    res = pl.pallas_call(
        body, name="adamw_small",
        out_shape=[jax.ShapeDtypeStruct(w.shape, F32) for w in ws] * 4 + [jax.ShapeDtypeStruct((1, LANES), F32)],
        compiler_params=_params(None),
    )(*ws, *ms, *vs, *stacks, loss_stack)
    return res[:n], res[n:2 * n], res[2 * n:3 * n], res[3 * n:4 * n], res[4 * n]


def _adamw(name, w, m, v, parts):
    r, c = w.shape
    tr = _tile(r, 256)
    n_parts = len(parts)

    def body(*refs):
        _adamw_update(refs[0], refs[1], refs[2], _sum_parts(refs[3:3 + n_parts], 2), *refs[3 + n_parts:])

    row = pl.BlockSpec((tr, c), lambda i: (i, 0))
    in_specs = [row, row, row]
    for p in parts:
        in_specs.append(row if p.ndim == 2 else pl.BlockSpec((p.shape[0], tr, c), lambda i: (0, i, 0)))
    return pl.pallas_call(
        body, name=name, grid=(r // tr,), in_specs=in_specs, out_specs=[row] * 4,
        out_shape=[jax.ShapeDtypeStruct((r, c), F32)] * 4,
        compiler_params=_params(("arbitrary",)),
    )(w, m, v, *parts)


SMALL = ("norm_gain", "pool_scale", "a_re", "a_im", "log_dt", "b_re", "b_im", "c_re", "c_im", "d_skip", "final_gain")
LARGE = ("w_in", "w_pool", "w_glu", "w_out", "w_ple", "w_ple_gate")
LARGE_AXIS = {"w_in": 1, "w_pool": 1, "w_glu": 1, "w_out": 0, "w_ple": 1, "w_ple_gate": 0}
WEIGHTS = ("norm_gain", "w_in", "w_pool", "pool_scale", "a_re", "a_im", "log_dt", "b_re", "b_im", "c_re", "c_im",
           "d_skip", "w_glu", "w_out", "w_ple", "w_ple_gate", "final_gain")


def kernel(x, p, norm_gain, w_in, w_pool, pool_scale, a_re, a_im, log_dt, b_re, b_im, c_re, c_im, d_skip, w_glu, w_out, w_ple, w_ple_gate, final_gain, loss_target, m_norm_gain, m_w_in, m_w_pool, m_pool_scale, m_a_re, m_a_im, m_log_dt, m_b_re, m_b_im, m_c_re, m_c_im, m_d_skip, m_w_glu, m_w_out, m_w_ple, m_w_ple_gate, m_final_gain, v_norm_gain, v_w_in, v_w_pool, v_pool_scale, v_a_re, v_a_im, v_log_dt, v_b_re, v_b_im, v_c_re, v_c_im, v_d_skip, v_w_glu, v_w_out, v_w_ple, v_w_ple_gate, v_final_gain):
    weights = dict(norm_gain=norm_gain, w_in=w_in, w_pool=w_pool, pool_scale=pool_scale, a_re=a_re, a_im=a_im,
                   log_dt=log_dt, b_re=b_re, b_im=b_im, c_re=c_re, c_im=c_im, d_skip=d_skip, w_glu=w_glu,
                   w_out=w_out, w_ple=w_ple, w_ple_gate=w_ple_gate, final_gain=final_gain)
    mom_m = dict(norm_gain=m_norm_gain, w_in=m_w_in, w_pool=m_w_pool, pool_scale=m_pool_scale, a_re=m_a_re,
                 a_im=m_a_im, log_dt=m_log_dt, b_re=m_b_re, b_im=m_b_im, c_re=m_c_re, c_im=m_c_im,
                 d_skip=m_d_skip, w_glu=m_w_glu, w_out=m_w_out, w_ple=m_w_ple, w_ple_gate=m_w_ple_gate,
                 final_gain=m_final_gain)
    mom_v = dict(norm_gain=v_norm_gain, w_in=v_w_in, w_pool=v_w_pool, pool_scale=v_pool_scale, a_re=v_a_re,
                 a_im=v_a_im, log_dt=v_log_dt, b_re=v_b_re, b_im=v_b_im, c_re=v_c_re, c_im=v_c_im,
                 d_skip=v_d_skip, w_glu=v_w_glu, w_out=v_w_out, w_ple=v_w_ple, w_ple_gate=v_w_ple_gate,
                 final_gain=v_final_gain)

    t = x.shape[1]
    xs = x.reshape(t, D_MODEL)
    ps = p.reshape(t, PLE_DIM)
    target = loss_target.reshape(t, D_MODEL)
    gain1 = norm_gain.reshape(1, D_MODEL)
    gain_f = final_gain.reshape(1, D_MODEL)
    scale_p = pool_scale.reshape(1, POOL_WIDTH)
    skip = d_skip.reshape(1, SSM_WIDTH)

    shard2d = {k: weights[k][0] for k in LARGE}
    shard_bf = {k: shard2d[k].astype(BF16) for k in LARGE}
    full = {"w_in": _all_gather("w_in_all_gather", [shard_bf["w_in"]], [LARGE_AXIS["w_in"]])[0]}
    later = [k for k in LARGE if k != "w_in"]
    later_axes = [LARGE_AXIS[k] for k in later]
    gather, gather_token = _gather_start("weights_gather_start", [shard_bf[k] for k in later], later_axes,
                                         full["w_in"])

    def arrive(k, after):
        i = later.index(k)
        full[k] = _gather_wait("gather_wait_" + k, gather, [i], [later_axes[i]], after)[0]

    ar, ai = a_re[0], a_im[0]
    ldt = log_dt.reshape(N_SSM_GROUPS, 1)
    br_t = jnp.transpose(b_re[0], (0, 2, 1))
    bi_t = jnp.transpose(b_im[0], (0, 2, 1))
    ab_re, ab_im, bb_re, bb_im = _ssm_params(ar, ai, ldt, br_t, bi_t)
    tiles = (SSM_CHUNKS, CHUNK_TILES, LANES)
    abar = jnp.concatenate([ab_re.reshape(tiles), ab_im.reshape(tiles)], axis=-1)
    w_pair = _compact_pair(bb_re, bb_im)
    c_pair = _compact_pair(c_re[0], -c_im[0])

    hn, proj = _norm1_in_proj(xs, gain1, full["w_in"], gather_token)
    arrive("w_pool", proj)
    pooled, mixed = _pool_mix(proj, full["w_pool"])
    y, gel, states = _ssm_fwd(proj, w_pair, c_pair, abar, skip)
    arrive("w_glu", gel)
    hg = _mm_nn("glu_proj", gel, full["w_glu"], [F32])[0]
    arrive("w_out", hg)
    cat, h1, h1b = _gate_out_proj(mixed, proj, hg, scale_p, full["w_out"], xs)
    arrive("w_ple", h1b)
    arrive("w_ple_gate", h1b)
    de, dq, dh2, g_final_gain, loss_part = _ple_final(h1, h1b, ps, full["w_ple_gate"], full["w_ple"], target, gain_f)

    grads = {}
    grads["w_ple_gate"] = _mm_tn("ple_gate_wgrad", h1b, dq, BF16)
    grads["w_ple"] = _mm_tn("ple_wgrad", ps, de, BF16)
    sent, tokens = {}, {}

    def send(names):
        sent[names], tokens[names[0]] = _exchange_start(
            "grads_start_" + names[0], [grads[k] for k in names], [LARGE_AXIS[k] for k in names])

    send(("w_ple_gate", "w_ple"))
    dh1, dh1b = _residual_dgrad("ple_gate_dgrad", dq, full["w_ple_gate"], dh2)
    grads["w_out"] = _mm_tn("out_wgrad", cat, dh1b, BF16)
    send(("w_out",))
    dmixed, dproj, dhg, g_pool_scale = _out_dgrad_gate_bwd(
        dh1b, full["w_out"], mixed, proj, hg, scale_p, [tokens["w_ple_gate"], tokens["w_out"]])

    tk = _tile(t, 1024)
    grads["w_pool"] = _mm("pool_wgrad", [(pooled, (tk, POOL_GROUP), lambda i, j, s: (s, i),
                                          dmixed, (tk, POOL_GROUP), lambda i, j, s: (s, i))],
                          DOT_TN, (N_POOL_GROUPS, 1, t // tk),
                          [((N_POOL_GROUPS, POOL_GROUP, POOL_GROUP), BF16, (None, POOL_GROUP, POOL_GROUP),
                            lambda i, j, s: (i, 0, 0))], t // tk)[0]
    dproj = _pool_mix_bwd(dmixed, full["w_pool"], dproj)

    grads["w_glu"] = _mm_tn("glu_wgrad", gel, dhg, BF16)
    send(("w_pool", "w_glu"))

    def gelu_bwd_epilogue(acc, ex, out_refs):
        yv = ex[0][...]
        th = jnp.tanh(GELU_C * (yv + GELU_A * yv * yv * yv))
        dgelu = 0.5 * (1.0 + th) + 0.5 * yv * (1.0 - th * th) * GELU_C * (1.0 + 3.0 * GELU_A * yv * yv)
        out_refs[0][...] = acc * dgelu

    dy = _mm_nt("glu_dgrad", dhg, full["w_glu"], [F32], tk=2048, extras=[y], epilogue=gelu_bwd_epilogue,
                after=[tokens["w_pool"]])[0]
    dproj, g_c_pair, g_w_pair, g_abar, g_d_skip = _ssm_bwd(dy, proj, states, w_pair, c_pair, abar, skip, dproj)

    g_ab_re = g_abar[..., :LANES].reshape(N_SSM_GROUPS, SSM_STATE)
    g_ab_im = g_abar[..., LANES:].reshape(N_SSM_GROUPS, SSM_STATE)
    d_ar, d_ai, d_ldt, d_br_t, d_bi_t = _ssm_params_bwd(
        ar, ai, ldt, br_t, bi_t, g_ab_re, g_ab_im,
        _expand_grad(g_w_pair[..., :LANES]), _expand_grad(g_w_pair[..., LANES:]))

    small_grads = dict(
        pool_scale=g_pool_scale, a_re=d_ar, a_im=d_ai, log_dt=d_ldt.reshape(1, N_SSM_GROUPS),
        b_re=d_br_t.astype(BF16), b_im=d_bi_t.astype(BF16), c_re=_expand_grad(g_c_pair[..., :LANES]).astype(BF16),
        c_im=(-_expand_grad(g_c_pair[..., LANES:])).astype(BF16), d_skip=g_d_skip, final_gain=g_final_gain)
    early = [k for k in SMALL if k != "norm_gain"]
    early_sent, early_token = _gather_start(
        "small_grads_start", [small_grads[k][None] for k in early] + [jnp.broadcast_to(loss_part, (1, 1, LANES))],
        [0] * (len(early) + 1), d_ar)

    grads["w_in"] = _mm_tn("in_wgrad", hn, dproj, BF16, tn=dproj.shape[1], after=[early_token])
    send(("w_in",))
    grad_x, g_norm_gain = _in_dgrad_norm1_bwd(dproj, full["w_in"], xs, dh1, gain1, tokens["w_in"])
    late_sent, late_token = _gather_start("norm_gain_grad_start", [g_norm_gain[None]], [0], g_norm_gain)

    out_g, out_d, out_m, out_v = ({} for _ in range(4))
    me = 4 * lax.axis_index("x") + 2 * lax.axis_index("y") + lax.axis_index("c")
    after = late_token
    for names, started in sent.items():
        axes = [LARGE_AXIS[k] for k in names]
        partials, landed = _exchange_wait("grads_wait_" + names[0], started, axes, after)
        for k, axis, partial, land in zip(names, axes, partials, landed):
            shard_shape = shard2d[k].shape
            size = shard_shape[axis]
            own = lax.dynamic_slice_in_dim(partial, me * size, size, axis=axis)
            view = (-1, shard_shape[-1])
            rows = math.prod(shard_shape[:-1])
            res = _adamw("adamw_" + k, shard2d[k].reshape(view), mom_m[k][0].reshape(view), mom_v[k][0].reshape(view),
                         [own.reshape(view), land.reshape(N_PEERS, rows, shard_shape[-1])])
            out_g[k], out_d[k], out_m[k], out_v[k] = (r.reshape(weights[k].shape) for r in res)
            after = res[0]

    def b_view(a):
        return jnp.transpose(a[0], (0, 2, 1))

    views = dict(norm_gain=lambda a: a, pool_scale=lambda a: a, a_re=lambda a: a[0], a_im=lambda a: a[0],
                 log_dt=lambda a: a, b_re=b_view, b_im=b_view, c_re=lambda a: a[0], c_im=lambda a: a[0],
                 d_skip=lambda a: a, final_gain=lambda a: a.reshape(1, D_MODEL))
    landed = _gather_wait("small_grads_wait", early_sent, list(range(len(early) + 1)), [0] * (len(early) + 1), after)
    stack = dict(zip(early, landed))
    stack["norm_gain"] = _gather_wait("norm_gain_grad_wait", late_sent, [0], [0], after)[0]
    *small_out, loss_row = _adamw_small(
        [views[k](weights[k]) for k in SMALL], [views[k](mom_m[k]) for k in SMALL],
        [views[k](mom_v[k]) for k in SMALL], [stack[k] for k in SMALL], landed[-1])
    loss = loss_row[0, 0]
    for out, res in zip((out_g, out_d, out_m, out_v), small_out):
        for k, r in zip(SMALL, res):
            if k in ("b_re", "b_im"):
                r = jnp.transpose(r, (0, 2, 1))
            out[k] = r.reshape(weights[k].shape)

    return (loss, grad_x.reshape(x.shape), *[out_g[k] for k in WEIGHTS], *[out_d[k] for k in WEIGHTS],
            *[out_m[k] for k in WEIGHTS], *[out_v[k] for k in WEIGHTS])
```

```python
import math

import jax
import jax.numpy as jnp
from jax import lax
from jax.experimental import pallas as pl
from jax.experimental.pallas import tpu as pltpu

F32 = jnp.float32
BF16 = jnp.bfloat16
MESH = pl.DeviceIdType.MESH
MESH_AXES = ("x", "y", "c")
N_DEV = 8

D_MODEL = 2048
POOL_WIDTH = 1024
SSM_WIDTH = 1024
N_POOL_GROUPS = 4
POOL_GROUP = 256
SSM_GROUP = 16
N_SSM_GROUPS = 64
SSM_STATE = 64
SSM_FLAT = N_SSM_GROUPS * SSM_STATE
SSM_CHUNKS = 4
CHUNK_IN = SSM_WIDTH // SSM_CHUNKS
CHUNK_STATE = SSM_FLAT // SSM_CHUNKS
PLE_DIM = 256
EPS = 1e-6
A_RE_MAX = -1e-4
ADAM_LR = 0.001
ADAM_B1 = 0.9
ADAM_B2 = 0.999
ADAM_EPS = 1e-08
ADAM_WD = 0.01
ADAM_STEP = 10
GELU_C = math.sqrt(2.0 / math.pi)
GELU_A = 0.044715

SUBLANES = 8
LANES = 128
VMEM_LIMIT_BYTES = 48 * 1024 * 1024

DOT_NN = (((1,), (0,)), ((), ()))
DOT_NT = (((1,), (1,)), ((), ()))
DOT_TN = (((0,), (0,)), ((), ()))


def _tile(n, pref):
    return pref if n % pref == 0 else n


def _params(sem):
    return pltpu.CompilerParams(dimension_semantics=sem, vmem_limit_bytes=VMEM_LIMIT_BYTES)


def _sigmoid(v):
    return 1.0 / (1.0 + jnp.exp(-v))


def _silu_and_grad(v):
    s = _sigmoid(v)
    return v * s, s * (1.0 + v * (1.0 - s))


def _mm(name, pairs, dims, grid, outs, k_steps, extras=(), epilogue=None):
    n_pairs, n_ex, n_out = len(pairs), len(extras), len(outs)
    acc_shape = tuple(d for d in outs[0][2] if d is not None)
    if epilogue is None:
        def epilogue(acc, ex, out_refs):
            out_refs[0][...] = acc.astype(out_refs[0].dtype)

    def body(*refs):
        ab = refs[:2 * n_pairs]
        ex = refs[2 * n_pairs:2 * n_pairs + n_ex]
        out_refs = refs[2 * n_pairs + n_ex:2 * n_pairs + n_ex + n_out]
        acc = refs[-1]
        k = pl.program_id(2)

        @pl.when(k == 0)
        def _():
            acc[...] = jnp.zeros_like(acc)

        part = None
        for q in range(n_pairs):
            d = lax.dot_general(ab[2 * q][...].astype(BF16), ab[2 * q + 1][...].astype(BF16), dims,
                                preferred_element_type=F32)
            part = d if part is None else part + d
        acc[...] += part

        @pl.when(k == k_steps - 1)
        def _():
            epilogue(acc[...], ex, out_refs)

    in_specs, operands = [], []
    for a, a_blk, a_map, b, b_blk, b_map in pairs:
        in_specs += [pl.BlockSpec(a_blk, a_map), pl.BlockSpec(b_blk, b_map)]
        operands += [a, b]
    for e, e_blk, e_map in extras:
        in_specs.append(pl.BlockSpec(e_blk, e_map))
        operands.append(e)
    return pl.pallas_call(
        body, name=name, grid=grid, in_specs=in_specs,
        out_specs=[pl.BlockSpec(o[2], o[3]) for o in outs],
        out_shape=[jax.ShapeDtypeStruct(o[0], o[1]) for o in outs],
        scratch_shapes=[pltpu.VMEM(acc_shape, F32)],
        compiler_params=_params(("arbitrary", "arbitrary", "arbitrary")),
    )(*operands)


def _after(tokens):
    return [(tok, tok.shape, lambda i, j, s: (0, 0)) for tok in tokens]


def _mm_nn(name, a, b, out_dtypes, tm=1024, tn=1024, tk=1024, a_col0=0, extras=(), epilogue=None, after=()):
    m, n = a.shape[0], b.shape[1]
    k = b.shape[0]
    tm, tn, tk = _tile(m, tm), _tile(n, tn), _tile(k, tk)
    outs = [((m, n), dt, (tm, tn), lambda i, j, s: (i, j)) for dt in out_dtypes]
    ex = [(e, (tm, tn), lambda i, j, s: (i, j)) for e in extras] + _after(after)
    return _mm(name, [(a, (tm, tk), lambda i, j, s: (i, a_col0 + s), b, (tk, tn), lambda i, j, s: (s, j))],
               DOT_NN, (m // tm, n // tn, k // tk), outs, k // tk, ex, epilogue)


def _mm_nt(name, a, b, out_dtypes, tm=1024, tn=1024, tk=1024, extras=(), epilogue=None, after=()):
    m, kk = a.shape
    n = b.shape[0]
    tm, tn, tk = _tile(m, tm), _tile(n, tn), _tile(kk, tk)
    outs = [((m, n), dt, (tm, tn), lambda i, j, s: (i, j)) for dt in out_dtypes]
    ex = [(e, (tm, tn), lambda i, j, s: (i, j)) for e in extras] + _after(after)
    return _mm(name, [(a, (tm, tk), lambda i, j, s: (i, s), b, (tn, tk), lambda i, j, s: (j, s))],
               DOT_NT, (m // tm, n // tn, kk // tk), outs, kk // tk, ex, epilogue)


def _mm_tn(name, a, b, out_dtype, tm=512, tn=2048, tk=1024, after=()):
    m, kk = a.shape
    n = b.shape[1]
    tm, tn, tk = _tile(kk, tm), _tile(n, tn), _tile(m, tk)
    outs = [((kk, n), out_dtype, (tm, tn), lambda i, j, s: (i, j))]
    return _mm(name, [(a, (tk, tm), lambda i, j, s: (s, i), b, (tk, tn), lambda i, j, s: (s, j))],
               DOT_TN, (kk // tm, n // tn, m // tk), outs, m // tk, _after(after))[0]


ROW_TILE = 256


def _norm1_in_proj(x, gain, w_in, after):
    t = x.shape[0]
    tm = _tile(t, ROW_TILE)
    n = w_in.shape[1]

    def body(x_ref, g_ref, w_ref, _, hn_ref, proj_ref):
        xv = x_ref[...]
        r = lax.rsqrt(jnp.mean(xv * xv, axis=-1, keepdims=True) + EPS)
        hn = (xv * r * g_ref[...]).astype(BF16)
        hn_ref[...] = hn
        proj_ref[...] = jnp.dot(hn, w_ref[...], preferred_element_type=F32)

    row = pl.BlockSpec((tm, D_MODEL), lambda i: (i, 0))
    return pl.pallas_call(
        body, name="norm1_in_proj", grid=(t // tm,),
        in_specs=[row, pl.BlockSpec((1, D_MODEL), lambda i: (0, 0)), _resident(w_in.shape),
                  pl.BlockSpec(after.shape, lambda i: (0, 0))],
        out_specs=[row, pl.BlockSpec((tm, n), lambda i: (i, 0))],
        out_shape=[jax.ShapeDtypeStruct((t, D_MODEL), BF16), jax.ShapeDtypeStruct((t, n), F32)],
        compiler_params=_params(("arbitrary",)),
    )(x, gain, w_in, after)


def _in_dgrad_norm1_bwd(dproj, w_in, x, dh1, gain, after):
    t = x.shape[0]
    tm = _tile(t, ROW_TILE)

    def body(dp_ref, w_ref, x_ref, dh1_ref, g_ref, _, dx_ref, gg_ref):
        @pl.when(pl.program_id(0) == 0)
        def _():
            gg_ref[...] = jnp.zeros_like(gg_ref)

        dhn = lax.dot_general(dp_ref[...], w_ref[...], DOT_NT, preferred_element_type=F32)
        xv = x_ref[...]
        r = lax.rsqrt(jnp.mean(xv * xv, axis=-1, keepdims=True) + EPS)
        xh = xv * r
        gg_ref[...] += jnp.sum(dhn * xh, axis=0, keepdims=True)
        dxh = dhn * g_ref[...]
        dx_ref[...] = dh1_ref[...] + r * (dxh - xh * jnp.mean(dxh * xh, axis=-1, keepdims=True))

    row = pl.BlockSpec((tm, D_MODEL), lambda i: (i, 0))
    vec = pl.BlockSpec((1, D_MODEL), lambda i: (0, 0))
    return pl.pallas_call(
        body, name="in_dgrad_norm1_bwd", grid=(t // tm,),
        in_specs=[pl.BlockSpec((tm, dproj.shape[1]), lambda i: (i, 0)), _resident(w_in.shape), row, row, vec,
                  pl.BlockSpec(after.shape, lambda i: (0, 0))],
        out_specs=[row, vec],
        out_shape=[jax.ShapeDtypeStruct((t, D_MODEL), F32), jax.ShapeDtypeStruct((1, D_MODEL), F32)],
        compiler_params=_params(("arbitrary",)),
    )(dproj, w_in, x, dh1, gain, after)


def _pool_counts(t, width, group):
    row = lax.broadcasted_iota(jnp.int32, (t, width), 0)
    window = jnp.left_shift(jnp.int32(2), group)
    return row, jnp.minimum(row + 1, window).astype(F32)


def _select_window(group, s2, s4, s8, s16):
    return jnp.where(group == 0, s2, jnp.where(group == 1, s4, jnp.where(group == 2, s8, s16)))


def _pool_mix(proj, w_pool):
    t = proj.shape[0]

    def body(u_ref, w_ref, pooled_ref, mixed_ref):
        group = pl.program_id(0)
        row, count = _pool_counts(t, LANES, group)

        def down(a, j):
            return jnp.where(row >= j, pltpu.roll(a, j, 0), 0.0)

        for h in range(POOL_GROUP // LANES):
            cols = slice(h * LANES, (h + 1) * LANES)
            v = u_ref[:, cols]
            s2 = v + down(v, 1)
            s4 = s2 + down(s2, 2)
            s8 = s4 + down(s4, 4)
            s16 = s8 + down(s8, 8)
            pooled_ref[:, cols] = (_select_window(group, s2, s4, s8, s16) / count - v).astype(BF16)
        mixed_ref[...] = jnp.dot(pooled_ref[...], w_ref[...], preferred_element_type=F32)

    block = pl.BlockSpec((t, POOL_GROUP), lambda g: (0, g))
    return pl.pallas_call(
        body, name="pool_mix", grid=(N_POOL_GROUPS,),
        in_specs=[block, pl.BlockSpec((None, POOL_GROUP, POOL_GROUP), lambda g: (g, 0, 0))],
        out_specs=[block, block],
        out_shape=[jax.ShapeDtypeStruct((t, POOL_WIDTH), BF16), jax.ShapeDtypeStruct((t, POOL_WIDTH), F32)],
        compiler_params=_params(("arbitrary",)),
    )(proj, w_pool)


def _pool_mix_bwd(dmixed, w_pool, dproj):
    t = dmixed.shape[0]

    def body(dm_ref, w_ref, _, o_ref):
        group = pl.program_id(0)
        row, count = _pool_counts(t, LANES, group)

        def up(a, j):
            return jnp.where(row < t - j, pltpu.roll(a, t - j, 0), 0.0)

        dpooled = lax.dot_general(dm_ref[...], w_ref[...], DOT_NT, preferred_element_type=F32)
        for h in range(POOL_GROUP // LANES):
            cols = slice(h * LANES, (h + 1) * LANES)
            dp = dpooled[:, cols]
            r = dp / count
            s2 = r + up(r, 1)
            s4 = s2 + up(s2, 2)
            s8 = s4 + up(s4, 4)
            s16 = s8 + up(s8, 8)
            o_ref[:, cols] = (_select_window(group, s2, s4, s8, s16) - dp).astype(BF16)

    block = pl.BlockSpec((t, POOL_GROUP), lambda g: (0, g))
    return pl.pallas_call(
        body, name="pool_mix_bwd", grid=(N_POOL_GROUPS,),
        in_specs=[block, pl.BlockSpec((None, POOL_GROUP, POOL_GROUP), lambda g: (g, 0, 0)),
                  pl.BlockSpec(memory_space=pl.ANY)],
        out_specs=block,
        out_shape=jax.ShapeDtypeStruct(dproj.shape, dproj.dtype),
        input_output_aliases={2: 0},
        compiler_params=_params(("arbitrary",)),
    )(dmixed, w_pool, dproj)


def _gate_out_proj(mixed, proj, hg, pool_scale, w_out, x):
    t = mixed.shape[0]
    tm = _tile(t, ROW_TILE)

    def body(mx_ref, ga_ref, gb_ref, hg_ref, ps_ref, w_ref, x_ref, cat_ref, h1_ref, h1b_ref):
        silu_a, _ = _silu_and_grad(ga_ref[...])
        cat_ref[:, :POOL_WIDTH] = (mx_ref[...] * ps_ref[...] * silu_a).astype(BF16)
        silu_b, _ = _silu_and_grad(gb_ref[...])
        sb = hg_ref[:, :SSM_WIDTH] * _sigmoid(hg_ref[:, SSM_WIDTH:])
        cat_ref[:, POOL_WIDTH:] = (sb * silu_b).astype(BF16)
        h1 = x_ref[...] + jnp.dot(cat_ref[...], w_ref[...], preferred_element_type=F32)
        h1_ref[...] = h1
        h1b_ref[...] = h1.astype(BF16)

    row = pl.BlockSpec((tm, D_MODEL), lambda i: (i, 0))
    return pl.pallas_call(
        body, name="gate_out_proj", grid=(t // tm,),
        in_specs=[pl.BlockSpec((tm, POOL_WIDTH), lambda i: (i, 0)),
                  pl.BlockSpec((tm, POOL_WIDTH), lambda i: (i, 1)),
                  pl.BlockSpec((tm, SSM_WIDTH), lambda i: (i, 3)),
                  pl.BlockSpec((tm, 2 * SSM_WIDTH), lambda i: (i, 0)),
                  pl.BlockSpec((1, POOL_WIDTH), lambda i: (0, 0)), _resident(w_out.shape), row],
        out_specs=[row, row, row],
        out_shape=[jax.ShapeDtypeStruct((t, D_MODEL), BF16), jax.ShapeDtypeStruct((t, D_MODEL), F32),
                   jax.ShapeDtypeStruct((t, D_MODEL), BF16)],
        compiler_params=_params(("arbitrary",)),
    )(mixed, proj, proj, hg, pool_scale, w_out, x)


def _residual_dgrad(name, dy, w, residual):
    t = dy.shape[0]
    tm = _tile(t, 2 * ROW_TILE)
    n = w.shape[0]

    def body(dy_ref, w_ref, r_ref, o_ref, ob_ref):
        o = r_ref[...] + lax.dot_general(dy_ref[...], w_ref[...], DOT_NT, preferred_element_type=F32)
        o_ref[...] = o
        ob_ref[...] = o.astype(BF16)

    out = pl.BlockSpec((tm, n), lambda i: (i, 0))
    return pl.pallas_call(
        body, name=name, grid=(t // tm,),
        in_specs=[pl.BlockSpec((tm, dy.shape[1]), lambda i: (i, 0)), _resident(w.shape), out],
        out_specs=[out, out],
        out_shape=[jax.ShapeDtypeStruct((t, n), F32), jax.ShapeDtypeStruct((t, n), BF16)],
        compiler_params=_params(("arbitrary",)),
    )(dy, w, residual)


def _out_dgrad_gate_bwd(dh1b, w_out, mixed, proj, hg, pool_scale, after):
    t = mixed.shape[0]
    tm = _tile(t, ROW_TILE)
    n_after = len(after)

    def body(dh_ref, w_ref, mx_ref, ga_ref, gb_ref, hg_ref, ps_ref, *rest):
        dmx_ref, dp_ref, dhg_ref, gps_ref = rest[n_after:]

        @pl.when(pl.program_id(0) == 0)
        def _():
            gps_ref[...] = jnp.zeros_like(gps_ref)

        dcat = lax.dot_general(dh_ref[...], w_ref[...], DOT_NT, preferred_element_type=F32)
        ps = ps_ref[...]
        mx = mx_ref[...]
        dya = dcat[:, :POOL_WIDTH]
        silu_a, dsilu_a = _silu_and_grad(ga_ref[...])
        dpa = dya * silu_a
        gps_ref[...] += jnp.sum(dpa * mx, axis=0, keepdims=True)
        dmx_ref[...] = (dpa * ps).astype(BF16)
        dp_ref[:, :POOL_WIDTH] = jnp.zeros((tm, POOL_WIDTH), BF16)
        dp_ref[:, POOL_WIDTH:2 * POOL_WIDTH] = (dya * mx * ps * dsilu_a).astype(BF16)

        dyb = dcat[:, POOL_WIDTH:]
        silu_b, dsilu_b = _silu_and_grad(gb_ref[...])
        h_a = hg_ref[:, :SSM_WIDTH]
        sg = _sigmoid(hg_ref[:, SSM_WIDTH:])
        dsb = dyb * silu_b
        dp_ref[:, 2 * POOL_WIDTH:2 * POOL_WIDTH + SSM_WIDTH] = jnp.zeros((tm, SSM_WIDTH), BF16)
        dp_ref[:, 2 * POOL_WIDTH + SSM_WIDTH:] = (dyb * h_a * sg * dsilu_b).astype(BF16)
        dhg_ref[:, :SSM_WIDTH] = (dsb * sg).astype(BF16)
        dhg_ref[:, SSM_WIDTH:] = (dsb * h_a * sg * (1.0 - sg)).astype(BF16)

    half = pl.BlockSpec((tm, POOL_WIDTH), lambda i: (i, 0))
    full = pl.BlockSpec((tm, D_MODEL), lambda i: (i, 0))
    vec = pl.BlockSpec((1, POOL_WIDTH), lambda i: (0, 0))
    proj_width = 2 * POOL_WIDTH + 2 * SSM_WIDTH
    return pl.pallas_call(
        body, name="out_dgrad_gate_bwd", grid=(t // tm,),
        in_specs=[full, _resident(w_out.shape), half,
                  pl.BlockSpec((tm, POOL_WIDTH), lambda i: (i, 1)),
                  pl.BlockSpec((tm, SSM_WIDTH), lambda i: (i, 3)),
                  full, vec] + [pl.BlockSpec(tok.shape, lambda i: (0, 0)) for tok in after],
        out_specs=[half, pl.BlockSpec((tm, proj_width), lambda i: (i, 0)), full, vec],
        out_shape=[jax.ShapeDtypeStruct((t, POOL_WIDTH), BF16), jax.ShapeDtypeStruct((t, proj_width), BF16),
                   jax.ShapeDtypeStruct((t, 2 * SSM_WIDTH), BF16),
                   jax.ShapeDtypeStruct((1, POOL_WIDTH), F32)],
        compiler_params=_params(("arbitrary",)),
    )(dh1b, w_out, mixed, proj, proj, hg, pool_scale, *after)


def _ple_final(h1, h1b, p, w_gate, w_ple, target, gain):
    t = h1.shape[0]
    tm = _tile(t, 256)

    def body(h1_ref, h1b_ref, p_ref, wg_ref, wp_ref, tg_ref, g_ref, de_ref, dq_ref, dh2_ref, gg_ref, loss_ref):
        @pl.when(pl.program_id(0) == 0)
        def _():
            gg_ref[...] = jnp.zeros_like(gg_ref)
            loss_ref[...] = jnp.zeros_like(loss_ref)

        ev = jnp.dot(p_ref[...].astype(BF16), wp_ref[...], preferred_element_type=F32)
        sg = _sigmoid(jnp.dot(h1b_ref[...], wg_ref[...], preferred_element_type=F32))
        h2 = h1_ref[...] + ev * sg
        r = lax.rsqrt(jnp.mean(h2 * h2, axis=-1, keepdims=True) + EPS)
        n = h2 * r
        gain_v = g_ref[...]
        diff = n * gain_v - tg_ref[...]
        row_loss = jnp.sum(diff * diff, axis=-1, keepdims=True)
        loss_ref[...] += (0.5 / D_MODEL) * jnp.sum(row_loss, axis=0, keepdims=True)
        dout = diff * (1.0 / D_MODEL)
        gg_ref[...] += jnp.sum(dout * n, axis=0, keepdims=True)
        dn = dout * gain_v
        dh2 = r * (dn - n * jnp.mean(dn * n, axis=-1, keepdims=True))
        dh2_ref[...] = dh2
        de_ref[...] = (dh2 * sg).astype(BF16)
        dq_ref[...] = (dh2 * ev * sg * (1.0 - sg)).astype(BF16)

    row = pl.BlockSpec((tm, D_MODEL), lambda i: (i, 0))
    vec = pl.BlockSpec((1, D_MODEL), lambda i: (0, 0))
    return pl.pallas_call(
        body, name="ple_final", grid=(t // tm,),
        in_specs=[row, row, pl.BlockSpec((tm, PLE_DIM), lambda i: (i, 0)), _resident((D_MODEL, D_MODEL)),
                  _resident((PLE_DIM, D_MODEL)), row, vec],
        out_specs=[row, row, row, vec, pl.BlockSpec((1, 1), lambda i: (0, 0))],
        out_shape=[jax.ShapeDtypeStruct((t, D_MODEL), BF16), jax.ShapeDtypeStruct((t, D_MODEL), BF16),
                   jax.ShapeDtypeStruct((t, D_MODEL), F32), jax.ShapeDtypeStruct((1, D_MODEL), F32),
                   jax.ShapeDtypeStruct((1, 1), F32)],
        compiler_params=_params(("arbitrary",)),
    )(h1, h1b, p, w_gate, w_ple, target, gain)


def _zoh(a_re, a_im, log_dt, b_re_t, b_im_t):
    lam_re = jnp.minimum(a_re, A_RE_MAX)
    lam_im = a_im
    dt = jnp.exp(log_dt)
    mag = jnp.exp(lam_re * dt)
    ang = lam_im * dt
    ab_re = mag * jnp.cos(ang)
    ab_im = mag * jnp.sin(ang)
    den = lam_re * lam_re + lam_im * lam_im
    n_re = ab_re - 1.0
    n_im = ab_im
    q_re = (n_re * lam_re + n_im * lam_im) / den
    q_im = (n_im * lam_re - n_re * lam_im) / den
    bb_re = q_re[:, None, :] * b_re_t - q_im[:, None, :] * b_im_t
    bb_im = q_re[:, None, :] * b_im_t + q_im[:, None, :] * b_re_t
    return ab_re, ab_im, bb_re, bb_im


def _ssm_params(a_re, a_im, log_dt, b_re_t, b_im_t):
    def body(are_ref, aim_ref, dt_ref, bre_ref, bim_ref, abre_ref, abim_ref, bbre_ref, bbim_ref):
        ab_re, ab_im, bb_re, bb_im = _zoh(are_ref[...], aim_ref[...], dt_ref[...], bre_ref[...], bim_ref[...])
        abre_ref[...] = ab_re
        abim_ref[...] = ab_im
        bbre_ref[...] = bb_re
        bbim_ref[...] = bb_im

    return pl.pallas_call(
        body, name="ssm_params",
        out_shape=[jax.ShapeDtypeStruct(a_re.shape, F32), jax.ShapeDtypeStruct(a_re.shape, F32),
                   jax.ShapeDtypeStruct(b_re_t.shape, F32), jax.ShapeDtypeStruct(b_re_t.shape, F32)],
        compiler_params=_params(None),
    )(a_re, a_im, log_dt, b_re_t, b_im_t)


def _ssm_params_bwd(a_re, a_im, log_dt, b_re_t, b_im_t, g_ab_re, g_ab_im, g_bb_re, g_bb_im):
    def body(are_ref, aim_ref, dt_ref, bre_ref, bim_ref, gar_ref, gai_ref, gbr_ref, gbi_ref,
             o_are, o_aim, o_dt, o_bre, o_bim):
        _, vjp = jax.vjp(_zoh, are_ref[...], aim_ref[...], dt_ref[...], bre_ref[...], bim_ref[...])
        d_are, d_aim, d_dt, d_bre, d_bim = vjp((gar_ref[...], gai_ref[...], gbr_ref[...], gbi_ref[...]))
        o_are[...] = d_are
        o_aim[...] = d_aim
        o_dt[...] = d_dt
        o_bre[...] = d_bre
        o_bim[...] = d_bim

    ins = (a_re, a_im, log_dt, b_re_t, b_im_t)
    return pl.pallas_call(
        body, name="ssm_params_bwd",
        out_shape=[jax.ShapeDtypeStruct(v.shape, F32) for v in ins],
        compiler_params=_params(None),
    )(*ins, g_ab_re, g_ab_im, g_bb_re, g_bb_im)


CHUNK_TILES = CHUNK_STATE // LANES
CH_PER_TILE = CHUNK_IN // CHUNK_TILES
PAIR = 2 * LANES
SSM_ROWS = 256
SCAN_STEPS = 8
U_COLUMN_BLOCK = 2 * POOL_WIDTH // SSM_WIDTH


def _own_half():
    r = lax.broadcasted_iota(jnp.int32, (CHUNK_IN, LANES), 0) // SSM_GROUP % 2
    c = lax.broadcasted_iota(jnp.int32, (CHUNK_IN, LANES), 1) // SSM_STATE
    return (r == c)[None]


def _compact_weight(w):
    tiled = jnp.tile(w.reshape(SSM_CHUNKS, CHUNK_IN, SSM_STATE), (1, 1, 2))
    return jnp.where(_own_half(), tiled, 0.0)


def _compact_pair(w_a, w_b):
    return jnp.concatenate([_compact_weight(w_a), _compact_weight(w_b)], axis=-1).astype(BF16)


def _expand_grad(g):
    kept = jnp.where(_own_half(), g, 0.0)
    return kept.reshape(SSM_CHUNKS, CHUNK_IN, 2, SSM_STATE).sum(axis=2).reshape(N_SSM_GROUPS, SSM_GROUP, SSM_STATE)


TILES_PER_BLOCK = LANES // CH_PER_TILE
IN_BLOCKS = CHUNK_IN // LANES


def _tile_masks():
    j = lax.broadcasted_iota(jnp.int32, (CHUNK_TILES, LANES), 0) % TILES_PER_BLOCK
    lane = lax.broadcasted_iota(jnp.int32, (CHUNK_TILES, LANES), 1) // CH_PER_TILE
    return (j == lane).astype(F32)


def _tile_rows(ref, j, tt):
    return ref.at[j // TILES_PER_BLOCK, pl.ds(j, tt, stride=CHUNK_TILES), :]


def _spread(ref, v, masks):
    tt = v.shape[0]
    for j in range(CHUNK_TILES):
        block = LANES * (j // TILES_PER_BLOCK)
        _tile_rows(ref, j, tt)[...] = v[:, block:block + LANES] * masks[j:j + 1, :]
    return jnp.concatenate([ref[b] for b in range(IN_BLOCKS)], axis=1).astype(BF16)


def _gather(ref, full, masks):
    tt = full.shape[0] // CHUNK_TILES
    for b in range(IN_BLOCKS):
        ref[b] = full[:, b * LANES:(b + 1) * LANES]
    out = []
    for b in range(IN_BLOCKS):
        acc = None
        for j in range(b * TILES_PER_BLOCK, (b + 1) * TILES_PER_BLOCK):
            part = _tile_rows(ref, j, tt)[...] * masks[j:j + 1, :]
            acc = part if acc is None else acc + part
        out.append(acc)
    return jnp.concatenate(out, axis=1)


def _resident(shape):
    return pl.BlockSpec(shape, lambda i: (0,) * len(shape), pipeline_mode=pl.Buffered(1))


def _halves(ref, k, rows=slice(None)):
    return ref[k, rows, :LANES], ref[k, rows, LANES:]


def _ssm_fwd(proj, w2, c2, a2, d_skip):
    t = proj.shape[0]
    tt = _tile(t, SSM_ROWS)
    rows = tt * CHUNK_TILES

    def body(u_ref, w_ref, c_ref, a_ref, d_ref, y_ref, gel_ref, s_ref, carry, spread_ref, full_ref):
        @pl.when(pl.program_id(0) == 0)
        def _():
            carry[...] = jnp.zeros_like(carry)
            spread_ref[...] = jnp.zeros_like(spread_ref)

        mask = _tile_masks()
        u = u_ref[...]
        for k in range(SSM_CHUNKS):
            uk = _spread(spread_ref, u[:, k * CHUNK_IN:(k + 1) * CHUNK_IN], mask)
            s_ref[k] = jnp.dot(uk, w_ref[k], preferred_element_type=F32)

        abar = [_halves(a_ref, k) for k in range(SSM_CHUNKS)]

        def steps(i, state):
            for v in range(SCAN_STEPS):
                r = pl.ds(pl.multiple_of((i * SCAN_STEPS + v) * CHUNK_TILES, CHUNK_TILES), CHUNK_TILES)
                new = []
                for k, ((a_re, a_im), (s_re, s_im)) in enumerate(zip(abar, state)):
                    b_re, b_im = _halves(s_ref, k, r)
                    s_re, s_im = a_re * s_re - a_im * s_im + b_re, a_re * s_im + a_im * s_re + b_im
                    s_ref[k, r, :LANES] = s_re
                    s_ref[k, r, LANES:] = s_im
                    new.append((s_re, s_im))
                state = tuple(new)
            return state

        state = lax.fori_loop(0, tt // SCAN_STEPS, steps, tuple(_halves(carry, k) for k in range(SSM_CHUNKS)))
        for k, (s_re, s_im) in enumerate(state):
            carry[k, :, :LANES] = s_re
            carry[k, :, LANES:] = s_im

        for k in range(SSM_CHUNKS):
            cols = slice(k * CHUNK_IN, (k + 1) * CHUNK_IN)
            full = lax.dot_general(s_ref[k].astype(BF16), c_ref[k], DOT_NT, preferred_element_type=F32)
            y = _gather(full_ref, full, mask) + d_ref[:, cols] * u[:, cols]
            y_ref[:, cols] = y
            gel_ref[:, cols] = (0.5 * y * (1.0 + jnp.tanh(GELU_C * (y + GELU_A * y * y * y)))).astype(BF16)

    weight = _resident((SSM_CHUNKS, CHUNK_IN, PAIR))
    tokens = pl.BlockSpec((tt, SSM_WIDTH), lambda i: (i, 0))
    return pl.pallas_call(
        body, name="ssm_fwd", grid=(t // tt,),
        in_specs=[pl.BlockSpec((tt, SSM_WIDTH), lambda i: (i, U_COLUMN_BLOCK)), weight, weight,
                  _resident((SSM_CHUNKS, CHUNK_TILES, PAIR)), _resident((1, SSM_WIDTH))],
        out_specs=[tokens, tokens, pl.BlockSpec((SSM_CHUNKS, rows, PAIR), lambda i: (0, i, 0))],
        out_shape=[jax.ShapeDtypeStruct((t, SSM_WIDTH), F32), jax.ShapeDtypeStruct((t, SSM_WIDTH), BF16),
                   jax.ShapeDtypeStruct((SSM_CHUNKS, t * CHUNK_TILES, PAIR), F32)],
        scratch_shapes=[pltpu.VMEM((SSM_CHUNKS, CHUNK_TILES, PAIR), F32), pltpu.VMEM((IN_BLOCKS, rows, LANES), F32),
                        pltpu.VMEM((IN_BLOCKS, rows, LANES), F32)],
        compiler_params=_params(("arbitrary",)),
    )(proj, w2, c2, a2, d_skip)


def _ssm_bwd(dy, proj, s, w2, c2, a2, d_skip, dproj):
    t = dy.shape[0]
    tt = _tile(t, SSM_ROWS)
    rows = tt * CHUNK_TILES
    n_chunks = t // tt

    def body(dy_ref, u_ref, s_ref, w_ref, c_ref, a_ref, d_ref, _, du_ref, gc_ref, gw_ref, ga_ref, gd_ref, z_ref, carry,
             spread_ref, full_ref):
        @pl.when(pl.program_id(0) == 0)
        def _():
            for r in (carry, gc_ref, gw_ref, ga_ref, gd_ref, spread_ref):
                r[...] = jnp.zeros_like(r)

        mask = _tile_masks()
        dy_v = dy_ref[...]
        u = u_ref[...]
        gd_ref[...] += jnp.sum(dy_v * u, axis=0, keepdims=True)
        for k in range(SSM_CHUNKS):
            dk = _spread(spread_ref, dy_v[:, k * CHUNK_IN:(k + 1) * CHUNK_IN], mask)
            z_ref[k] = jnp.dot(dk, c_ref[k], preferred_element_type=F32)
            gc_ref[k] += lax.dot_general(dk, s_ref[k].astype(BF16), DOT_TN, preferred_element_type=F32)

        abar = [_halves(a_ref, k) for k in range(SSM_CHUNKS)]

        def steps(i, state):
            zs, gs = state
            for v in range(SCAN_STEPS):
                tok = tt - 1 - (i * SCAN_STEPS + v)
                r = pl.ds(pl.multiple_of(tok * CHUNK_TILES, CHUNK_TILES), CHUNK_TILES)
                new_z, new_g = [], []
                for k, ((a_re, a_im), (z_re, z_im), (g_re, g_im)) in enumerate(zip(abar, zs, gs)):
                    s_re, s_im = _halves(s_ref, k, r)
                    g_re = g_re + z_re * s_re + z_im * s_im
                    g_im = g_im + z_im * s_re - z_re * s_im
                    d_re, d_im = _halves(z_ref, k, r)
                    z_re, z_im = d_re + a_re * z_re + a_im * z_im, d_im + a_re * z_im - a_im * z_re
                    z_ref[k, r, :LANES] = z_re
                    z_ref[k, r, LANES:] = z_im
                    new_z.append((z_re, z_im))
                    new_g.append((g_re, g_im))
                zs, gs = tuple(new_z), tuple(new_g)
            return zs, gs

        zs, gs = lax.fori_loop(0, tt // SCAN_STEPS, steps,
                               (tuple(_halves(carry, k) for k in range(SSM_CHUNKS)),
                                tuple(_halves(ga_ref, k) for k in range(SSM_CHUNKS))))
        for k in range(SSM_CHUNKS):
            carry[k, :, :LANES], carry[k, :, LANES:] = zs[k]
            ga_ref[k, :, :LANES], ga_ref[k, :, LANES:] = gs[k]

        for k in range(SSM_CHUNKS):
            cols = slice(k * CHUNK_IN, (k + 1) * CHUNK_IN)
            zb = z_ref[k].astype(BF16)
            full = lax.dot_general(zb, w_ref[k], DOT_NT, preferred_element_type=F32)
            du_ref[:, cols] = (_gather(full_ref, full, mask) + d_ref[:, cols] * dy_v[:, cols]).astype(BF16)
            uk = _spread(spread_ref, u[:, cols], mask)
            gw_ref[k] += lax.dot_general(uk, zb, DOT_TN, preferred_element_type=F32)

    weight = _resident((SSM_CHUNKS, CHUNK_IN, PAIR))
    tokens = pl.BlockSpec((tt, SSM_WIDTH), lambda i: (n_chunks - 1 - i, 0))
    grad = pl.BlockSpec((SSM_CHUNKS, CHUNK_IN, PAIR), lambda i: (0, 0, 0))
    return pl.pallas_call(
        body, name="ssm_bwd", grid=(n_chunks,),
        in_specs=[tokens, pl.BlockSpec((tt, SSM_WIDTH), lambda i: (n_chunks - 1 - i, U_COLUMN_BLOCK)),
                  pl.BlockSpec((SSM_CHUNKS, rows, PAIR), lambda i: (0, n_chunks - 1 - i, 0)), weight, weight,
                  _resident((SSM_CHUNKS, CHUNK_TILES, PAIR)), _resident((1, SSM_WIDTH)),
                  pl.BlockSpec(memory_space=pl.ANY)],
        out_specs=[pl.BlockSpec((tt, SSM_WIDTH), lambda i: (n_chunks - 1 - i, U_COLUMN_BLOCK)), grad, grad,
                   pl.BlockSpec((SSM_CHUNKS, CHUNK_TILES, PAIR), lambda i: (0, 0, 0)),
                   pl.BlockSpec((1, SSM_WIDTH), lambda i: (0, 0))],
        out_shape=[jax.ShapeDtypeStruct(dproj.shape, dproj.dtype), jax.ShapeDtypeStruct((SSM_CHUNKS, CHUNK_IN, PAIR), F32),
                   jax.ShapeDtypeStruct((SSM_CHUNKS, CHUNK_IN, PAIR), F32),
                   jax.ShapeDtypeStruct((SSM_CHUNKS, CHUNK_TILES, PAIR), F32), jax.ShapeDtypeStruct((1, SSM_WIDTH), F32)],
        input_output_aliases={7: 0},
        scratch_shapes=[pltpu.VMEM((SSM_CHUNKS, rows, PAIR), F32), pltpu.VMEM((SSM_CHUNKS, CHUNK_TILES, PAIR), F32),
                        pltpu.VMEM((IN_BLOCKS, rows, LANES), F32), pltpu.VMEM((IN_BLOCKS, rows, LANES), F32)],
        compiler_params=_params(("arbitrary",)),
    )(dy, proj, s, w2, c2, a2, d_skip, dproj)


def _block(ref, axis, size, index):
    idx = [slice(None)] * len(ref.shape)
    idx[axis] = pl.ds(pl.multiple_of(index * size, size), size)
    return ref.at[tuple(idx)]


def _all_gather(name, shards, axes):
    n = len(shards)
    sizes = [s.shape[a] for s, a in zip(shards, axes)]

    def body(*refs):
        ins, outs = refs[:n], refs[n:2 * n]
        send_sems, recv_sems, local_sems = refs[2 * n:]
        x, y, c = (lax.axis_index(a) for a in MESH_AXES)
        me, sibling = (x, y, c), (x, y, 1 - c)
        chips = [(1 - x, y), (x, 1 - y), (1 - x, 1 - y)]

        def rows(i, dev):
            return _block(outs[i], axes[i], sizes[i], 4 * dev[0] + 2 * dev[1] + dev[2])

        def copy(i, k, block, to, src=None):
            return pltpu.make_async_remote_copy(
                src_ref=rows(i, block) if src is None else src, dst_ref=rows(i, block),
                send_sem=send_sems.at[7 * i + k], recv_sem=recv_sems.at[7 * i + k],
                device_id=to, device_id_type=MESH)

        mine = [pltpu.make_async_copy(ins[i], rows(i, me), local_sems.at[i]) for i in range(n)]
        for cp in mine:
            cp.start()
        first = []
        for i in range(n):
            first.append(copy(i, 0, me, sibling, src=ins[i]))
            first += [copy(i, 1 + j, me, (*chip, c), src=ins[i]) for j, chip in enumerate(chips)]
        for cp in first:
            cp.start()
        passed = []
        for i in range(n):
            for j, chip in enumerate(chips):
                copy(i, 1 + j, (*chip, c), me).wait_recv()
                fwd = copy(i, 4 + j, (*chip, c), sibling)
                fwd.start()
                passed.append(fwd)
        for i in range(n):
            copy(i, 0, sibling, me).wait_recv()
            for j, chip in enumerate(chips):
                copy(i, 4 + j, (*chip, 1 - c), me).wait_recv()
        for cp in first + passed:
            cp.wait_send()
        for cp in mine:
            cp.wait()

    out_shape = []
    for s, a in zip(shards, axes):
        shape = list(s.shape)
        shape[a] *= N_DEV
        out_shape.append(jax.ShapeDtypeStruct(tuple(shape), s.dtype))
    any_spec = pl.BlockSpec(memory_space=pl.ANY)
    return pl.pallas_call(
        body, name=name, out_shape=out_shape,
        in_specs=[any_spec] * n, out_specs=[any_spec] * n,
        scratch_shapes=[pltpu.SemaphoreType.DMA((7 * n,)), pltpu.SemaphoreType.DMA((7 * n,)),
                        pltpu.SemaphoreType.DMA((n,))],
    )(*shards)


HBM_SPEC = pl.BlockSpec(memory_space=pltpu.HBM)
SEM_SPEC = pl.BlockSpec(memory_space=pltpu.SEMAPHORE)
ANY_SPEC = pl.BlockSpec(memory_space=pl.ANY)
SPLIT_PARAMS = pltpu.CompilerParams(has_side_effects=pltpu.SideEffectType.DATAFLOW_SIDE_EFFECTING)
N_PEERS = N_DEV - 1
TOKEN = jax.ShapeDtypeStruct((SUBLANES, LANES), F32)
VMEM_SPEC = pl.BlockSpec(memory_space=pltpu.VMEM)


def _in_hbm(arrays):
    return [pltpu.with_memory_space_constraint(a, pltpu.HBM) for a in arrays]


def _peer(m):
    x, y, c = (lax.axis_index(a) for a in MESH_AXES)
    px = 1 - x if m & 4 else x
    py = 1 - y if m & 2 else y
    pc = 1 - c if m & 1 else c
    return (px, py, pc), 4 * px + 2 * py + pc


def _my_index():
    x, y, c = (lax.axis_index(a) for a in MESH_AXES)
    return 4 * x + 2 * y + c


def _gather_copies(shard_refs, full_refs, axes, send_sems, recv_sems):
    copies = []
    for i, (shard, full) in enumerate(zip(shard_refs, full_refs)):
        mine = _block(full, axes[i], shard.shape[axes[i]], _my_index())
        for m in range(1, N_DEV):
            peer, _ = _peer(m)
            copies.append(pltpu.make_async_remote_copy(
                src_ref=shard, dst_ref=mine, send_sem=send_sems.at[N_PEERS * i + m - 1],
                recv_sem=recv_sems.at[N_PEERS * i + m - 1], device_id=peer, device_id_type=MESH))
    return copies


def _gather_start(name, shards, axes, after):
    n = len(shards)

    def body(*refs):
        shard_refs = refs[:n]
        send_sems, recv_sems, local_sems = refs[n + 1:n + 4]
        full_refs = refs[2 * n + 4:3 * n + 4]
        refs[3 * n + 4][...] = jnp.zeros(TOKEN.shape, TOKEN.dtype)
        for i in range(n):
            pltpu.make_async_copy(shard_refs[i], _block(full_refs[i], axes[i], shard_refs[i].shape[axes[i]], _my_index()),
                                  local_sems.at[i]).start()
        for cp in _gather_copies(shard_refs, full_refs, axes, send_sems, recv_sems):
            cp.start()

    fulls = []
    for s, a in zip(shards, axes):
        shape = list(s.shape)
        shape[a] *= N_DEV
        fulls.append(pltpu.HBM(tuple(shape), s.dtype))
    out = pl.pallas_call(
        body, name=name,
        out_shape=(pltpu.SemaphoreType.DMA((N_PEERS * n,)), pltpu.SemaphoreType.DMA((N_PEERS * n,)),
                   pltpu.SemaphoreType.DMA((n,)), *[pltpu.HBM(s.shape, s.dtype) for s in shards], *fulls, TOKEN),
        in_specs=[HBM_SPEC] * n + [ANY_SPEC],
        out_specs=(SEM_SPEC, SEM_SPEC, SEM_SPEC, *[HBM_SPEC] * (2 * n), VMEM_SPEC),
        input_output_aliases={i: 3 + i for i in range(n)},
        compiler_params=SPLIT_PARAMS,
    )(*_in_hbm(shards), after)
    return out[:-1], out[-1]


def _gather_wait(name, started, indices, axes, after):
    send_sems, recv_sems, local_sems = started[:3]
    n_all = (len(started) - 3) // 2
    shards = [started[3 + i] for i in indices]
    fulls = [started[3 + n_all + i] for i in indices]
    n = len(indices)

    def body(*refs):
        shard_refs, full_refs = refs[:n], refs[n:2 * n]
        send_sems, recv_sems, local_sems = refs[2 * n:2 * n + 3]
        for j, i in enumerate(indices):
            mine = _block(full_refs[j], axes[j], shard_refs[j].shape[axes[j]], _my_index())
            pltpu.make_async_copy(shard_refs[j], mine, local_sems.at[i]).wait()
            for m in range(1, N_DEV):
                peer, _ = _peer(m)
                cp = pltpu.make_async_remote_copy(
                    src_ref=shard_refs[j], dst_ref=mine, send_sem=send_sems.at[N_PEERS * i + m - 1],
                    recv_sem=recv_sems.at[N_PEERS * i + m - 1], device_id=peer, device_id_type=MESH)
                cp.wait_send()
                cp.wait_recv()

    out = pl.pallas_call(
        body, name=name,
        out_shape=tuple(pltpu.HBM(a.shape, a.dtype) for a in shards + fulls),
        in_specs=[HBM_SPEC] * (2 * n) + [SEM_SPEC] * 3 + [ANY_SPEC], out_specs=tuple([HBM_SPEC] * (2 * n)),
        input_output_aliases={i: i for i in range(2 * n)},
        compiler_params=SPLIT_PARAMS,
    )(*shards, *fulls, send_sems, recv_sems, local_sems, after)
    return out[n:]


def _exchange_start(name, fulls, axes):
    n = len(fulls)
    sizes = [f.shape[a] // N_DEV for f, a in zip(fulls, axes)]

    def body(*refs):
        ins = refs[:n]
        send_sems, recv_sems = refs[n:n + 2]
        lands = refs[2 * n + 2:3 * n + 2]
        refs[3 * n + 2][...] = jnp.zeros(TOKEN.shape, TOKEN.dtype)
        for i in range(n):
            for m in range(1, N_DEV):
                peer, index = _peer(m)
                pltpu.make_async_remote_copy(
                    src_ref=_block(ins[i], axes[i], sizes[i], index), dst_ref=lands[i].at[m - 1],
                    send_sem=send_sems.at[N_PEERS * i + m - 1], recv_sem=recv_sems.at[N_PEERS * i + m - 1],
                    device_id=peer, device_id_type=MESH).start()

    lands = []
    for f, a, size in zip(fulls, axes, sizes):
        shape = list(f.shape)
        shape[a] = size
        lands.append(pltpu.HBM((N_PEERS, *shape), f.dtype))
    out = pl.pallas_call(
        body, name=name,
        out_shape=(pltpu.SemaphoreType.DMA((N_PEERS * n,)), pltpu.SemaphoreType.DMA((N_PEERS * n,)),
                   *[pltpu.HBM(f.shape, f.dtype) for f in fulls], *lands, TOKEN),
        in_specs=[HBM_SPEC] * n, out_specs=(SEM_SPEC, SEM_SPEC, *[HBM_SPEC] * (2 * n), VMEM_SPEC),
        input_output_aliases={i: 2 + i for i in range(n)},
        compiler_params=SPLIT_PARAMS,
    )(*_in_hbm(fulls))
    return out[:-1], out[-1]


def _exchange_wait(name, started, axes, after):
    send_sems, recv_sems = started[:2]
    n = (len(started) - 2) // 2
    fulls, lands = list(started[2:2 + n]), list(started[2 + n:])
    sizes = [f.shape[a] // N_DEV for f, a in zip(fulls, axes)]

    def body(*refs):
        ins, land_refs = refs[:n], refs[n:2 * n]
        send_sems, recv_sems = refs[2 * n:2 * n + 2]
        for i in range(n):
            for m in range(1, N_DEV):
                peer, index = _peer(m)
                cp = pltpu.make_async_remote_copy(
                    src_ref=_block(ins[i], axes[i], sizes[i], index), dst_ref=land_refs[i].at[m - 1],
                    send_sem=send_sems.at[N_PEERS * i + m - 1], recv_sem=recv_sems.at[N_PEERS * i + m - 1],
                    device_id=peer, device_id_type=MESH)
                cp.wait_send()
                cp.wait_recv()

    out = pl.pallas_call(
        body, name=name,
        out_shape=tuple(pltpu.HBM(a.shape, a.dtype) for a in fulls + lands),
        in_specs=[HBM_SPEC] * (2 * n) + [SEM_SPEC] * 2 + [ANY_SPEC], out_specs=tuple([HBM_SPEC] * (2 * n)),
        input_output_aliases={i: i for i in range(2 * n)},
        compiler_params=SPLIT_PARAMS,
    )(*fulls, *lands, send_sems, recv_sems, after)
    return out[:n], out[n:]


def _sum_parts(part_refs, ndim):
    g = None
    for p_ref in part_refs:
        stacked = len(p_ref.shape) > ndim
        terms = [p_ref[s] for s in range(p_ref.shape[0])] if stacked else [p_ref[...]]
        for term in terms:
            term = term.astype(F32)
            g = term if g is None else g + term
    return g


def _adamw_update(w_ref, m_ref, v_ref, g, g_ref, d_ref, nm_ref, nv_ref):
    c1 = 1.0 - ADAM_B1 ** ADAM_STEP
    c2 = 1.0 - ADAM_B2 ** ADAM_STEP
    new_m = ADAM_B1 * m_ref[...] + (1.0 - ADAM_B1) * g
    new_v = ADAM_B2 * v_ref[...] + (1.0 - ADAM_B2) * (g * g)
    g_ref[...] = g
    nm_ref[...] = new_m
    nv_ref[...] = new_v
    d_ref[...] = -ADAM_LR * ((new_m / c1) / (jnp.sqrt(new_v / c2) + ADAM_EPS) + ADAM_WD * w_ref[...])


def _adamw_small(ws, ms, vs, stacks, loss_stack):
    n = len(ws)

    def body(*refs):
        ins, outs = refs[:4 * n + 1], refs[4 * n + 1:]
        for i in range(n):
            _adamw_update(ins[i], ins[n + i], ins[2 * n + i], _sum_parts([ins[3 * n + i]], len(ins[i].shape)),
                          outs[i], outs[n + i], outs[2 * n + i], outs[3 * n + i])
        total = ins[4 * n][0]
        for dev in range(1, N_DEV):
            total = total + ins[4 * n][dev]
        outs[4 * n][...] = total

    res = pl.pallas_call(
        body, name="adamw_small",
        out_shape=[jax.ShapeDtypeStruct(w.shape, F32) for w in ws] * 4 + [jax.ShapeDtypeStruct((1, LANES), F32)],
        compiler_params=_params(None),
    )(*ws, *ms, *vs, *stacks, loss_stack)
    return res[:n], res[n:2 * n], res[2 * n:3 * n], res[3 * n:4 * n], res[4 * n]


def _adamw(name, w, m, v, parts):
    r, c = w.shape
    tr = _tile(r, 256)
    n_parts = len(parts)

    def body(*refs):
        _adamw_update(refs[0], refs[1], refs[2], _sum_parts(refs[3:3 + n_parts], 2), *refs[3 + n_parts:])

    row = pl.BlockSpec((tr, c), lambda i: (i, 0))
    in_specs = [row, row, row]
    for p in parts:
        in_specs.append(row if p.ndim == 2 else pl.BlockSpec((p.shape[0], tr, c), lambda i: (0, i, 0)))
    return pl.pallas_call(
        body, name=name, grid=(r // tr,), in_specs=in_specs, out_specs=[row] * 4,
        out_shape=[jax.ShapeDtypeStruct((r, c), F32)] * 4,
        compiler_params=_params(("arbitrary",)),
    )(w, m, v, *parts)


SMALL = ("norm_gain", "pool_scale", "a_re", "a_im", "log_dt", "b_re", "b_im", "c_re", "c_im", "d_skip", "final_gain")
LARGE = ("w_in", "w_pool", "w_glu", "w_out", "w_ple", "w_ple_gate")
LARGE_AXIS = {"w_in": 1, "w_pool": 1, "w_glu": 1, "w_out": 0, "w_ple": 1, "w_ple_gate": 0}
WEIGHTS = ("norm_gain", "w_in", "w_pool", "pool_scale", "a_re", "a_im", "log_dt", "b_re", "b_im", "c_re", "c_im",
           "d_skip", "w_glu", "w_out", "w_ple", "w_ple_gate", "final_gain")


def kernel(x, p, norm_gain, w_in, w_pool, pool_scale, a_re, a_im, log_dt, b_re, b_im, c_re, c_im, d_skip, w_glu, w_out, w_ple, w_ple_gate, final_gain, loss_target, m_norm_gain, m_w_in, m_w_pool, m_pool_scale, m_a_re, m_a_im, m_log_dt, m_b_re, m_b_im, m_c_re, m_c_im, m_d_skip, m_w_glu, m_w_out, m_w_ple, m_w_ple_gate, m_final_gain, v_norm_gain, v_w_in, v_w_pool, v_pool_scale, v_a_re, v_a_im, v_log_dt, v_b_re, v_b_im, v_c_re, v_c_im, v_d_skip, v_w_glu, v_w_out, v_w_ple, v_w_ple_gate, v_final_gain):
    weights = dict(norm_gain=norm_gain, w_in=w_in, w_pool=w_pool, pool_scale=pool_scale, a_re=a_re, a_im=a_im,
                   log_dt=log_dt, b_re=b_re, b_im=b_im, c_re=c_re, c_im=c_im, d_skip=d_skip, w_glu=w_glu,
                   w_out=w_out, w_ple=w_ple, w_ple_gate=w_ple_gate, final_gain=final_gain)
    mom_m = dict(norm_gain=m_norm_gain, w_in=m_w_in, w_pool=m_w_pool, pool_scale=m_pool_scale, a_re=m_a_re,
                 a_im=m_a_im, log_dt=m_log_dt, b_re=m_b_re, b_im=m_b_im, c_re=m_c_re, c_im=m_c_im,
                 d_skip=m_d_skip, w_glu=m_w_glu, w_out=m_w_out, w_ple=m_w_ple, w_ple_gate=m_w_ple_gate,
                 final_gain=m_final_gain)
    mom_v = dict(norm_gain=v_norm_gain, w_in=v_w_in, w_pool=v_w_pool, pool_scale=v_pool_scale, a_re=v_a_re,
                 a_im=v_a_im, log_dt=v_log_dt, b_re=v_b_re, b_im=v_b_im, c_re=v_c_re, c_im=v_c_im,
                 d_skip=v_d_skip, w_glu=v_w_glu, w_out=v_w_out, w_ple=v_w_ple, w_ple_gate=v_w_ple_gate,
                 final_gain=v_final_gain)

    t = x.shape[1]
    xs = x.reshape(t, D_MODEL)
    ps = p.reshape(t, PLE_DIM)
    target = loss_target.reshape(t, D_MODEL)
    gain1 = norm_gain.reshape(1, D_MODEL)
    gain_f = final_gain.reshape(1, D_MODEL)
    scale_p = pool_scale.reshape(1, POOL_WIDTH)
    skip = d_skip.reshape(1, SSM_WIDTH)

    shard2d = {k: weights[k][0] for k in LARGE}
    shard_bf = {k: shard2d[k].astype(BF16) for k in LARGE}
    full = {"w_in": _all_gather("w_in_all_gather", [shard_bf["w_in"]], [LARGE_AXIS["w_in"]])[0]}
    later = [k for k in LARGE if k != "w_in"]
    later_axes = [LARGE_AXIS[k] for k in later]
    gather, gather_token = _gather_start("weights_gather_start", [shard_bf[k] for k in later], later_axes,
                                         full["w_in"])

    def arrive(k, after):
        i = later.index(k)
        full[k] = _gather_wait("gather_wait_" + k, gather, [i], [later_axes[i]], after)[0]

    ar, ai = a_re[0], a_im[0]
    ldt = log_dt.reshape(N_SSM_GROUPS, 1)
    br_t = jnp.transpose(b_re[0], (0, 2, 1))
    bi_t = jnp.transpose(b_im[0], (0, 2, 1))
    ab_re, ab_im, bb_re, bb_im = _ssm_params(ar, ai, ldt, br_t, bi_t)
    tiles = (SSM_CHUNKS, CHUNK_TILES, LANES)
    abar = jnp.concatenate([ab_re.reshape(tiles), ab_im.reshape(tiles)], axis=-1)
    w_pair = _compact_pair(bb_re, bb_im)
    c_pair = _compact_pair(c_re[0], -c_im[0])

    hn, proj = _norm1_in_proj(xs, gain1, full["w_in"], gather_token)
    arrive("w_pool", proj)
    pooled, mixed = _pool_mix(proj, full["w_pool"])
    y, gel, states = _ssm_fwd(proj, w_pair, c_pair, abar, skip)
    arrive("w_glu", gel)
    hg = _mm_nn("glu_proj", gel, full["w_glu"], [F32])[0]
    arrive("w_out", hg)
    cat, h1, h1b = _gate_out_proj(mixed, proj, hg, scale_p, full["w_out"], xs)
    arrive("w_ple", h1b)
    arrive("w_ple_gate", h1b)
    de, dq, dh2, g_final_gain, loss_part = _ple_final(h1, h1b, ps, full["w_ple_gate"], full["w_ple"], target, gain_f)

    grads = {}
    grads["w_ple_gate"] = _mm_tn("ple_gate_wgrad", h1b, dq, BF16)
    grads["w_ple"] = _mm_tn("ple_wgrad", ps, de, BF16)
    sent, tokens = {}, {}

    def send(names):
        sent[names], tokens[names[0]] = _exchange_start(
            "grads_start_" + names[0], [grads[k] for k in names], [LARGE_AXIS[k] for k in names])

    send(("w_ple_gate", "w_ple"))
    dh1, dh1b = _residual_dgrad("ple_gate_dgrad", dq, full["w_ple_gate"], dh2)
    grads["w_out"] = _mm_tn("out_wgrad", cat, dh1b, BF16)
    send(("w_out",))
    dmixed, dproj, dhg, g_pool_scale = _out_dgrad_gate_bwd(
        dh1b, full["w_out"], mixed, proj, hg, scale_p, [tokens["w_ple_gate"], tokens["w_out"]])

    tk = _tile(t, 1024)
    grads["w_pool"] = _mm("pool_wgrad", [(pooled, (tk, POOL_GROUP), lambda i, j, s: (s, i),
                                          dmixed, (tk, POOL_GROUP), lambda i, j, s: (s, i))],
                          DOT_TN, (N_POOL_GROUPS, 1, t // tk),
                          [((N_POOL_GROUPS, POOL_GROUP, POOL_GROUP), BF16, (None, POOL_GROUP, POOL_GROUP),
                            lambda i, j, s: (i, 0, 0))], t // tk)[0]
    dproj = _pool_mix_bwd(dmixed, full["w_pool"], dproj)

    grads["w_glu"] = _mm_tn("glu_wgrad", gel, dhg, BF16)
    send(("w_pool", "w_glu"))

    def gelu_bwd_epilogue(acc, ex, out_refs):
        yv = ex[0][...]
        th = jnp.tanh(GELU_C * (yv + GELU_A * yv * yv * yv))
        dgelu = 0.5 * (1.0 + th) + 0.5 * yv * (1.0 - th * th) * GELU_C * (1.0 + 3.0 * GELU_A * yv * yv)
        out_refs[0][...] = acc * dgelu

    dy = _mm_nt("glu_dgrad", dhg, full["w_glu"], [F32], tk=2048, extras=[y], epilogue=gelu_bwd_epilogue,
                after=[tokens["w_pool"]])[0]
    dproj, g_c_pair, g_w_pair, g_abar, g_d_skip = _ssm_bwd(dy, proj, states, w_pair, c_pair, abar, skip, dproj)

    g_ab_re = g_abar[..., :LANES].reshape(N_SSM_GROUPS, SSM_STATE)
    g_ab_im = g_abar[..., LANES:].reshape(N_SSM_GROUPS, SSM_STATE)
    d_ar, d_ai, d_ldt, d_br_t, d_bi_t = _ssm_params_bwd(
        ar, ai, ldt, br_t, bi_t, g_ab_re, g_ab_im,
        _expand_grad(g_w_pair[..., :LANES]), _expand_grad(g_w_pair[..., LANES:]))

    small_grads = dict(
        pool_scale=g_pool_scale, a_re=d_ar, a_im=d_ai, log_dt=d_ldt.reshape(1, N_SSM_GROUPS),
        b_re=d_br_t.astype(BF16), b_im=d_bi_t.astype(BF16), c_re=_expand_grad(g_c_pair[..., :LANES]).astype(BF16),
        c_im=(-_expand_grad(g_c_pair[..., LANES:])).astype(BF16), d_skip=g_d_skip, final_gain=g_final_gain)
    early = [k for k in SMALL if k != "norm_gain"]
    early_sent, early_token = _gather_start(
        "small_grads_start", [small_grads[k][None] for k in early] + [jnp.broadcast_to(loss_part, (1, 1, LANES))],
        [0] * (len(early) + 1), d_ar)

    grads["w_in"] = _mm_tn("in_wgrad", hn, dproj, BF16, tn=dproj.shape[1], after=[early_token])
    send(("w_in",))
    grad_x, g_norm_gain = _in_dgrad_norm1_bwd(dproj, full["w_in"], xs, dh1, gain1, tokens["w_in"])
    late_sent, late_token = _gather_start("norm_gain_grad_start", [g_norm_gain[None]], [0], g_norm_gain)

    out_g, out_d, out_m, out_v = ({} for _ in range(4))
    me = 4 * lax.axis_index("x") + 2 * lax.axis_index("y") + lax.axis_index("c")
    def update_owned(names, started, after):
        axes = [LARGE_AXIS[k] for k in names]
        partials, landed = _exchange_wait("grads_wait_" + names[0], started, axes, after)
        for k, axis, partial, land in zip(names, axes, partials, landed):
            shard_shape = shard2d[k].shape
            size = shard_shape[axis]
            own = lax.dynamic_slice_in_dim(partial, me * size, size, axis=axis)
            view = (-1, shard_shape[-1])
            rows = math.prod(shard_shape[:-1])
            res = _adamw("adamw_" + k, shard2d[k].reshape(view), mom_m[k][0].reshape(view), mom_v[k][0].reshape(view),
                         [own.reshape(view), land.reshape(N_PEERS, rows, shard_shape[-1])])
            out_g[k], out_d[k], out_m[k], out_v[k] = (r.reshape(weights[k].shape) for r in res)
            after = res[0]
        return after

    *earlier, last = sent.items()
    after = late_token
    for names, started in earlier:
        after = update_owned(names, started, after)

    def b_view(a):
        return jnp.transpose(a[0], (0, 2, 1))

    views = dict(norm_gain=lambda a: a, pool_scale=lambda a: a, a_re=lambda a: a[0], a_im=lambda a: a[0],
                 log_dt=lambda a: a, b_re=b_view, b_im=b_view, c_re=lambda a: a[0], c_im=lambda a: a[0],
                 d_skip=lambda a: a, final_gain=lambda a: a.reshape(1, D_MODEL))
    landed = _gather_wait("small_grads_wait", early_sent, list(range(len(early) + 1)), [0] * (len(early) + 1), after)
    stack = dict(zip(early, landed))
    stack["norm_gain"] = _gather_wait("norm_gain_grad_wait", late_sent, [0], [0], after)[0]
    *small_out, loss_row = _adamw_small(
        [views[k](weights[k]) for k in SMALL], [views[k](mom_m[k]) for k in SMALL],
        [views[k](mom_v[k]) for k in SMALL], [stack[k] for k in SMALL], landed[-1])
    loss = loss_row[0, 0]
    for out, res in zip((out_g, out_d, out_m, out_v), small_out):
        for k, r in zip(SMALL, res):
            if k in ("b_re", "b_im"):
                r = jnp.transpose(r, (0, 2, 1))
            out[k] = r.reshape(weights[k].shape)

    update_owned(*last, loss_row)

    return (loss, grad_x.reshape(x.shape), *[out_g[k] for k in WEIGHTS], *[out_d[k] for k in WEIGHTS],
            *[out_m[k] for k in WEIGHTS], *[out_v[k] for k in WEIGHTS])
```

```python
import math

import jax
import jax.numpy as jnp
from jax import lax
from jax.experimental import pallas as pl
from jax.experimental.pallas import tpu as pltpu

F32 = jnp.float32
BF16 = jnp.bfloat16
MESH = pl.DeviceIdType.MESH
MESH_AXES = ("x", "y", "c")
N_DEV = 8

D_MODEL = 2048
POOL_WIDTH = 1024
SSM_WIDTH = 1024
N_POOL_GROUPS = 4
POOL_GROUP = 256
SSM_GROUP = 16
N_SSM_GROUPS = 64
SSM_STATE = 64
SSM_FLAT = N_SSM_GROUPS * SSM_STATE
SSM_CHUNKS = 4
CHUNK_IN = SSM_WIDTH // SSM_CHUNKS
CHUNK_STATE = SSM_FLAT // SSM_CHUNKS
PLE_DIM = 256
EPS = 1e-6
A_RE_MAX = -1e-4
ADAM_LR = 0.001
ADAM_B1 = 0.9
ADAM_B2 = 0.999
ADAM_EPS = 1e-08
ADAM_WD = 0.01
ADAM_STEP = 10
GELU_C = math.sqrt(2.0 / math.pi)
GELU_A = 0.044715

SUBLANES = 8
LANES = 128
VMEM_LIMIT_BYTES = 48 * 1024 * 1024

DOT_NN = (((1,), (0,)), ((), ()))
DOT_NT = (((1,), (1,)), ((), ()))
DOT_TN = (((0,), (0,)), ((), ()))


def _tile(n, pref):
    return pref if n % pref == 0 else n


def _params(sem):
    return pltpu.CompilerParams(dimension_semantics=sem, vmem_limit_bytes=VMEM_LIMIT_BYTES)


def _sigmoid(v):
    return 1.0 / (1.0 + jnp.exp(-v))


def _silu_and_grad(v):
    s = _sigmoid(v)
    return v * s, s * (1.0 + v * (1.0 - s))


def _mm(name, pairs, dims, grid, outs, k_steps, extras=(), epilogue=None):
    n_pairs, n_ex, n_out = len(pairs), len(extras), len(outs)
    acc_shape = tuple(d for d in outs[0][2] if d is not None)
    if epilogue is None:
        def epilogue(acc, ex, out_refs):
            out_refs[0][...] = acc.astype(out_refs[0].dtype)

    def body(*refs):
        ab = refs[:2 * n_pairs]
        ex = refs[2 * n_pairs:2 * n_pairs + n_ex]
        out_refs = refs[2 * n_pairs + n_ex:2 * n_pairs + n_ex + n_out]
        acc = refs[-1]
        k = pl.program_id(2)

        @pl.when(k == 0)
        def _():
            acc[...] = jnp.zeros_like(acc)

        part = None
        for q in range(n_pairs):
            d = lax.dot_general(ab[2 * q][...].astype(BF16), ab[2 * q + 1][...].astype(BF16), dims,
                                preferred_element_type=F32)
            part = d if part is None else part + d
        acc[...] += part

        @pl.when(k == k_steps - 1)
        def _():
            epilogue(acc[...], ex, out_refs)

    in_specs, operands = [], []
    for a, a_blk, a_map, b, b_blk, b_map in pairs:
        in_specs += [pl.BlockSpec(a_blk, a_map), pl.BlockSpec(b_blk, b_map)]
        operands += [a, b]
    for e, e_blk, e_map in extras:
        in_specs.append(pl.BlockSpec(e_blk, e_map))
        operands.append(e)
    return pl.pallas_call(
        body, name=name, grid=grid, in_specs=in_specs,
        out_specs=[pl.BlockSpec(o[2], o[3]) for o in outs],
        out_shape=[jax.ShapeDtypeStruct(o[0], o[1]) for o in outs],
        scratch_shapes=[pltpu.VMEM(acc_shape, F32)],
        compiler_params=_params(("arbitrary", "arbitrary", "arbitrary")),
    )(*operands)


def _after(tokens):
    return [(tok, tok.shape, lambda i, j, s: (0, 0)) for tok in tokens]


def _mm_nn(name, a, b, out_dtypes, tm=1024, tn=1024, tk=1024, a_col0=0, extras=(), epilogue=None, after=()):
    m, n = a.shape[0], b.shape[1]
    k = b.shape[0]
    tm, tn, tk = _tile(m, tm), _tile(n, tn), _tile(k, tk)
    outs = [((m, n), dt, (tm, tn), lambda i, j, s: (i, j)) for dt in out_dtypes]
    ex = [(e, (tm, tn), lambda i, j, s: (i, j)) for e in extras] + _after(after)
    return _mm(name, [(a, (tm, tk), lambda i, j, s: (i, a_col0 + s), b, (tk, tn), lambda i, j, s: (s, j))],
               DOT_NN, (m // tm, n // tn, k // tk), outs, k // tk, ex, epilogue)


def _mm_nt(name, a, b, out_dtypes, tm=1024, tn=1024, tk=1024, extras=(), epilogue=None, after=()):
    m, kk = a.shape
    n = b.shape[0]
    tm, tn, tk = _tile(m, tm), _tile(n, tn), _tile(kk, tk)
    outs = [((m, n), dt, (tm, tn), lambda i, j, s: (i, j)) for dt in out_dtypes]
    ex = [(e, (tm, tn), lambda i, j, s: (i, j)) for e in extras] + _after(after)
    return _mm(name, [(a, (tm, tk), lambda i, j, s: (i, s), b, (tn, tk), lambda i, j, s: (j, s))],
               DOT_NT, (m // tm, n // tn, kk // tk), outs, kk // tk, ex, epilogue)


def _mm_tn(name, a, b, out_dtype, tm=512, tn=2048, tk=1024, after=()):
    m, kk = a.shape
    n = b.shape[1]
    tm, tn, tk = _tile(kk, tm), _tile(n, tn), _tile(m, tk)
    outs = [((kk, n), out_dtype, (tm, tn), lambda i, j, s: (i, j))]
    return _mm(name, [(a, (tk, tm), lambda i, j, s: (s, i), b, (tk, tn), lambda i, j, s: (s, j))],
               DOT_TN, (kk // tm, n // tn, m // tk), outs, m // tk, _after(after))[0]


ROW_TILE = 256


def _norm1_in_proj(x, gain, w_in, after):
    t = x.shape[0]
    tm = _tile(t, ROW_TILE)
    n = w_in.shape[1]

    def body(x_ref, g_ref, w_ref, _, hn_ref, proj_ref):
        xv = x_ref[...]
        r = lax.rsqrt(jnp.mean(xv * xv, axis=-1, keepdims=True) + EPS)
        hn = (xv * r * g_ref[...]).astype(BF16)
        hn_ref[...] = hn
        proj_ref[...] = jnp.dot(hn, w_ref[...], preferred_element_type=F32)

    row = pl.BlockSpec((tm, D_MODEL), lambda i: (i, 0))
    return pl.pallas_call(
        body, name="norm1_in_proj", grid=(t // tm,),
        in_specs=[row, pl.BlockSpec((1, D_MODEL), lambda i: (0, 0)), _resident(w_in.shape),
                  pl.BlockSpec(after.shape, lambda i: (0, 0))],
        out_specs=[row, pl.BlockSpec((tm, n), lambda i: (i, 0))],
        out_shape=[jax.ShapeDtypeStruct((t, D_MODEL), BF16), jax.ShapeDtypeStruct((t, n), F32)],
        compiler_params=_params(("arbitrary",)),
    )(x, gain, w_in, after)


def _in_dgrad_norm1_bwd(dproj, w_in, x, dh1, gain, after):
    t = x.shape[0]
    tm = _tile(t, ROW_TILE)

    def body(dp_ref, w_ref, x_ref, dh1_ref, g_ref, _, dx_ref, gg_ref):
        @pl.when(pl.program_id(0) == 0)
        def _():
            gg_ref[...] = jnp.zeros_like(gg_ref)

        dhn = lax.dot_general(dp_ref[...], w_ref[...], DOT_NT, preferred_element_type=F32)
        xv = x_ref[...]
        r = lax.rsqrt(jnp.mean(xv * xv, axis=-1, keepdims=True) + EPS)
        xh = xv * r
        gg_ref[...] += jnp.sum(dhn * xh, axis=0, keepdims=True)
        dxh = dhn * g_ref[...]
        dx_ref[...] = dh1_ref[...] + r * (dxh - xh * jnp.mean(dxh * xh, axis=-1, keepdims=True))

    row = pl.BlockSpec((tm, D_MODEL), lambda i: (i, 0))
    vec = pl.BlockSpec((1, D_MODEL), lambda i: (0, 0))
    return pl.pallas_call(
        body, name="in_dgrad_norm1_bwd", grid=(t // tm,),
        in_specs=[pl.BlockSpec((tm, dproj.shape[1]), lambda i: (i, 0)), _resident(w_in.shape), row, row, vec,
                  pl.BlockSpec(after.shape, lambda i: (0, 0))],
        out_specs=[row, vec],
        out_shape=[jax.ShapeDtypeStruct((t, D_MODEL), F32), jax.ShapeDtypeStruct((1, D_MODEL), F32)],
        compiler_params=_params(("arbitrary",)),
    )(dproj, w_in, x, dh1, gain, after)


def _pool_counts(t, width, group):
    row = lax.broadcasted_iota(jnp.int32, (t, width), 0)
    window = jnp.left_shift(jnp.int32(2), group)
    return row, jnp.minimum(row + 1, window).astype(F32)


def _select_window(group, s2, s4, s8, s16):
    return jnp.where(group == 0, s2, jnp.where(group == 1, s4, jnp.where(group == 2, s8, s16)))


def _pool_mix(proj, w_pool):
    t = proj.shape[0]

    def body(u_ref, w_ref, pooled_ref, mixed_ref):
        group = pl.program_id(0)
        row, count = _pool_counts(t, LANES, group)

        def down(a, j):
            return jnp.where(row >= j, pltpu.roll(a, j, 0), 0.0)

        for h in range(POOL_GROUP // LANES):
            cols = slice(h * LANES, (h + 1) * LANES)
            v = u_ref[:, cols]
            s2 = v + down(v, 1)
            s4 = s2 + down(s2, 2)
            s8 = s4 + down(s4, 4)
            s16 = s8 + down(s8, 8)
            pooled_ref[:, cols] = (_select_window(group, s2, s4, s8, s16) / count - v).astype(BF16)
        mixed_ref[...] = jnp.dot(pooled_ref[...], w_ref[...], preferred_element_type=F32)

    block = pl.BlockSpec((t, POOL_GROUP), lambda g: (0, g))
    return pl.pallas_call(
        body, name="pool_mix", grid=(N_POOL_GROUPS,),
        in_specs=[block, pl.BlockSpec((None, POOL_GROUP, POOL_GROUP), lambda g: (g, 0, 0))],
        out_specs=[block, block],
        out_shape=[jax.ShapeDtypeStruct((t, POOL_WIDTH), BF16), jax.ShapeDtypeStruct((t, POOL_WIDTH), F32)],
        compiler_params=_params(("arbitrary",)),
    )(proj, w_pool)


def _pool_mix_bwd(dmixed, w_pool, dproj):
    t = dmixed.shape[0]

    def body(dm_ref, w_ref, _, o_ref):
        group = pl.program_id(0)
        row, count = _pool_counts(t, LANES, group)

        def up(a, j):
            return jnp.where(row < t - j, pltpu.roll(a, t - j, 0), 0.0)

        dpooled = lax.dot_general(dm_ref[...], w_ref[...], DOT_NT, preferred_element_type=F32)
        for h in range(POOL_GROUP // LANES):
            cols = slice(h * LANES, (h + 1) * LANES)
            dp = dpooled[:, cols]
            r = dp / count
            s2 = r + up(r, 1)
            s4 = s2 + up(s2, 2)
            s8 = s4 + up(s4, 4)
            s16 = s8 + up(s8, 8)
            o_ref[:, cols] = (_select_window(group, s2, s4, s8, s16) - dp).astype(BF16)

    block = pl.BlockSpec((t, POOL_GROUP), lambda g: (0, g))
    return pl.pallas_call(
        body, name="pool_mix_bwd", grid=(N_POOL_GROUPS,),
        in_specs=[block, pl.BlockSpec((None, POOL_GROUP, POOL_GROUP), lambda g: (g, 0, 0)),
                  pl.BlockSpec(memory_space=pl.ANY)],
        out_specs=block,
        out_shape=jax.ShapeDtypeStruct(dproj.shape, dproj.dtype),
        input_output_aliases={2: 0},
        compiler_params=_params(("arbitrary",)),
    )(dmixed, w_pool, dproj)


def _gate_out_proj(mixed, proj, hg, pool_scale, w_out, x):
    t = mixed.shape[0]
    tm = _tile(t, ROW_TILE)

    def body(mx_ref, ga_ref, gb_ref, hg_ref, ps_ref, w_ref, x_ref, cat_ref, h1_ref, h1b_ref):
        silu_a, _ = _silu_and_grad(ga_ref[...])
        cat_ref[:, :POOL_WIDTH] = (mx_ref[...] * ps_ref[...] * silu_a).astype(BF16)
        silu_b, _ = _silu_and_grad(gb_ref[...])
        sb = hg_ref[:, :SSM_WIDTH] * _sigmoid(hg_ref[:, SSM_WIDTH:])
        cat_ref[:, POOL_WIDTH:] = (sb * silu_b).astype(BF16)
        h1 = x_ref[...] + jnp.dot(cat_ref[...], w_ref[...], preferred_element_type=F32)
        h1_ref[...] = h1
        h1b_ref[...] = h1.astype(BF16)

    row = pl.BlockSpec((tm, D_MODEL), lambda i: (i, 0))
    return pl.pallas_call(
        body, name="gate_out_proj", grid=(t // tm,),
        in_specs=[pl.BlockSpec((tm, POOL_WIDTH), lambda i: (i, 0)),
                  pl.BlockSpec((tm, POOL_WIDTH), lambda i: (i, 1)),
                  pl.BlockSpec((tm, SSM_WIDTH), lambda i: (i, 3)),
                  pl.BlockSpec((tm, 2 * SSM_WIDTH), lambda i: (i, 0)),
                  pl.BlockSpec((1, POOL_WIDTH), lambda i: (0, 0)), _resident(w_out.shape), row],
        out_specs=[row, row, row],
        out_shape=[jax.ShapeDtypeStruct((t, D_MODEL), BF16), jax.ShapeDtypeStruct((t, D_MODEL), F32),
                   jax.ShapeDtypeStruct((t, D_MODEL), BF16)],
        compiler_params=_params(("arbitrary",)),
    )(mixed, proj, proj, hg, pool_scale, w_out, x)


def _residual_dgrad(name, dy, w, residual):
    t = dy.shape[0]
    tm = _tile(t, 2 * ROW_TILE)
    n = w.shape[0]

    def body(dy_ref, w_ref, r_ref, o_ref, ob_ref):
        o = r_ref[...] + lax.dot_general(dy_ref[...], w_ref[...], DOT_NT, preferred_element_type=F32)
        o_ref[...] = o
        ob_ref[...] = o.astype(BF16)

    out = pl.BlockSpec((tm, n), lambda i: (i, 0))
    return pl.pallas_call(
        body, name=name, grid=(t // tm,),
        in_specs=[pl.BlockSpec((tm, dy.shape[1]), lambda i: (i, 0)), _resident(w.shape), out],
        out_specs=[out, out],
        out_shape=[jax.ShapeDtypeStruct((t, n), F32), jax.ShapeDtypeStruct((t, n), BF16)],
        compiler_params=_params(("arbitrary",)),
    )(dy, w, residual)


def _out_dgrad_gate_bwd(dh1b, w_out, mixed, proj, hg, pool_scale, after):
    t = mixed.shape[0]
    tm = _tile(t, ROW_TILE)
    n_after = len(after)

    def body(dh_ref, w_ref, mx_ref, ga_ref, gb_ref, hg_ref, ps_ref, *rest):
        dmx_ref, dp_ref, dhg_ref, gps_ref = rest[n_after:]

        @pl.when(pl.program_id(0) == 0)
        def _():
            gps_ref[...] = jnp.zeros_like(gps_ref)

        dcat = lax.dot_general(dh_ref[...], w_ref[...], DOT_NT, preferred_element_type=F32)
        ps = ps_ref[...]
        mx = mx_ref[...]
        dya = dcat[:, :POOL_WIDTH]
        silu_a, dsilu_a = _silu_and_grad(ga_ref[...])
        dpa = dya * silu_a
        gps_ref[...] += jnp.sum(dpa * mx, axis=0, keepdims=True)
        dmx_ref[...] = (dpa * ps).astype(BF16)
        dp_ref[:, :POOL_WIDTH] = jnp.zeros((tm, POOL_WIDTH), BF16)
        dp_ref[:, POOL_WIDTH:2 * POOL_WIDTH] = (dya * mx * ps * dsilu_a).astype(BF16)

        dyb = dcat[:, POOL_WIDTH:]
        silu_b, dsilu_b = _silu_and_grad(gb_ref[...])
        h_a = hg_ref[:, :SSM_WIDTH]
        sg = _sigmoid(hg_ref[:, SSM_WIDTH:])
        dsb = dyb * silu_b
        dp_ref[:, 2 * POOL_WIDTH:2 * POOL_WIDTH + SSM_WIDTH] = jnp.zeros((tm, SSM_WIDTH), BF16)
        dp_ref[:, 2 * POOL_WIDTH + SSM_WIDTH:] = (dyb * h_a * sg * dsilu_b).astype(BF16)
        dhg_ref[:, :SSM_WIDTH] = (dsb * sg).astype(BF16)
        dhg_ref[:, SSM_WIDTH:] = (dsb * h_a * sg * (1.0 - sg)).astype(BF16)

    half = pl.BlockSpec((tm, POOL_WIDTH), lambda i: (i, 0))
    full = pl.BlockSpec((tm, D_MODEL), lambda i: (i, 0))
    vec = pl.BlockSpec((1, POOL_WIDTH), lambda i: (0, 0))
    proj_width = 2 * POOL_WIDTH + 2 * SSM_WIDTH
    return pl.pallas_call(
        body, name="out_dgrad_gate_bwd", grid=(t // tm,),
        in_specs=[full, _resident(w_out.shape), half,
                  pl.BlockSpec((tm, POOL_WIDTH), lambda i: (i, 1)),
                  pl.BlockSpec((tm, SSM_WIDTH), lambda i: (i, 3)),
                  full, vec] + [pl.BlockSpec(tok.shape, lambda i: (0, 0)) for tok in after],
        out_specs=[half, pl.BlockSpec((tm, proj_width), lambda i: (i, 0)), full, vec],
        out_shape=[jax.ShapeDtypeStruct((t, POOL_WIDTH), BF16), jax.ShapeDtypeStruct((t, proj_width), BF16),
                   jax.ShapeDtypeStruct((t, 2 * SSM_WIDTH), BF16),
                   jax.ShapeDtypeStruct((1, POOL_WIDTH), F32)],
        compiler_params=_params(("arbitrary",)),
    )(dh1b, w_out, mixed, proj, proj, hg, pool_scale, *after)


def _ple_final(h1, h1b, p, w_gate, w_ple, target, gain):
    t = h1.shape[0]
    tm = _tile(t, 256)

    def body(h1_ref, h1b_ref, p_ref, wg_ref, wp_ref, tg_ref, g_ref, de_ref, dq_ref, dh2_ref, gg_ref, loss_ref):
        @pl.when(pl.program_id(0) == 0)
        def _():
            gg_ref[...] = jnp.zeros_like(gg_ref)
            loss_ref[...] = jnp.zeros_like(loss_ref)

        ev = jnp.dot(p_ref[...].astype(BF16), wp_ref[...], preferred_element_type=F32)
        sg = _sigmoid(jnp.dot(h1b_ref[...], wg_ref[...], preferred_element_type=F32))
        h2 = h1_ref[...] + ev * sg
        r = lax.rsqrt(jnp.mean(h2 * h2, axis=-1, keepdims=True) + EPS)
        n = h2 * r
        gain_v = g_ref[...]
        diff = n * gain_v - tg_ref[...]
        row_loss = jnp.sum(diff * diff, axis=-1, keepdims=True)
        loss_ref[...] += (0.5 / D_MODEL) * jnp.sum(row_loss, axis=0, keepdims=True)
        dout = diff * (1.0 / D_MODEL)
        gg_ref[...] += jnp.sum(dout * n, axis=0, keepdims=True)
        dn = dout * gain_v
        dh2 = r * (dn - n * jnp.mean(dn * n, axis=-1, keepdims=True))
        dh2_ref[...] = dh2
        de_ref[...] = (dh2 * sg).astype(BF16)
        dq_ref[...] = (dh2 * ev * sg * (1.0 - sg)).astype(BF16)

    row = pl.BlockSpec((tm, D_MODEL), lambda i: (i, 0))
    vec = pl.BlockSpec((1, D_MODEL), lambda i: (0, 0))
    return pl.pallas_call(
        body, name="ple_final", grid=(t // tm,),
        in_specs=[row, row, pl.BlockSpec((tm, PLE_DIM), lambda i: (i, 0)), _resident((D_MODEL, D_MODEL)),
                  _resident((PLE_DIM, D_MODEL)), row, vec],
        out_specs=[row, row, row, vec, pl.BlockSpec((1, 1), lambda i: (0, 0))],
        out_shape=[jax.ShapeDtypeStruct((t, D_MODEL), BF16), jax.ShapeDtypeStruct((t, D_MODEL), BF16),
                   jax.ShapeDtypeStruct((t, D_MODEL), F32), jax.ShapeDtypeStruct((1, D_MODEL), F32),
                   jax.ShapeDtypeStruct((1, 1), F32)],
        compiler_params=_params(("arbitrary",)),
    )(h1, h1b, p, w_gate, w_ple, target, gain)


def _zoh(a_re, a_im, log_dt, b_re_t, b_im_t):
    lam_re = jnp.minimum(a_re, A_RE_MAX)
    lam_im = a_im
    dt = jnp.exp(log_dt)
    mag = jnp.exp(lam_re * dt)
    ang = lam_im * dt
    ab_re = mag * jnp.cos(ang)
    ab_im = mag * jnp.sin(ang)
    den = lam_re * lam_re + lam_im * lam_im
    n_re = ab_re - 1.0
    n_im = ab_im
    q_re = (n_re * lam_re + n_im * lam_im) / den
    q_im = (n_im * lam_re - n_re * lam_im) / den
    bb_re = q_re[:, None, :] * b_re_t - q_im[:, None, :] * b_im_t
    bb_im = q_re[:, None, :] * b_im_t + q_im[:, None, :] * b_re_t
    return ab_re, ab_im, bb_re, bb_im


def _ssm_params(a_re, a_im, log_dt, b_re_t, b_im_t):
    def body(are_ref, aim_ref, dt_ref, bre_ref, bim_ref, abre_ref, abim_ref, bbre_ref, bbim_ref):
        ab_re, ab_im, bb_re, bb_im = _zoh(are_ref[...], aim_ref[...], dt_ref[...], bre_ref[...], bim_ref[...])
        abre_ref[...] = ab_re
        abim_ref[...] = ab_im
        bbre_ref[...] = bb_re
        bbim_ref[...] = bb_im

    return pl.pallas_call(
        body, name="ssm_params",
        out_shape=[jax.ShapeDtypeStruct(a_re.shape, F32), jax.ShapeDtypeStruct(a_re.shape, F32),
                   jax.ShapeDtypeStruct(b_re_t.shape, F32), jax.ShapeDtypeStruct(b_re_t.shape, F32)],
        compiler_params=_params(None),
    )(a_re, a_im, log_dt, b_re_t, b_im_t)


def _ssm_params_bwd(a_re, a_im, log_dt, b_re_t, b_im_t, g_ab_re, g_ab_im, g_bb_re, g_bb_im):
    def body(are_ref, aim_ref, dt_ref, bre_ref, bim_ref, gar_ref, gai_ref, gbr_ref, gbi_ref,
             o_are, o_aim, o_dt, o_bre, o_bim):
        _, vjp = jax.vjp(_zoh, are_ref[...], aim_ref[...], dt_ref[...], bre_ref[...], bim_ref[...])
        d_are, d_aim, d_dt, d_bre, d_bim = vjp((gar_ref[...], gai_ref[...], gbr_ref[...], gbi_ref[...]))
        o_are[...] = d_are
        o_aim[...] = d_aim
        o_dt[...] = d_dt
        o_bre[...] = d_bre
        o_bim[...] = d_bim

    ins = (a_re, a_im, log_dt, b_re_t, b_im_t)
    return pl.pallas_call(
        body, name="ssm_params_bwd",
        out_shape=[jax.ShapeDtypeStruct(v.shape, F32) for v in ins],
        compiler_params=_params(None),
    )(*ins, g_ab_re, g_ab_im, g_bb_re, g_bb_im)


CHUNK_TILES = CHUNK_STATE // LANES
CH_PER_TILE = CHUNK_IN // CHUNK_TILES
PAIR = 2 * LANES
SSM_ROWS = 256
SCAN_STEPS = 8
U_COLUMN_BLOCK = 2 * POOL_WIDTH // SSM_WIDTH


def _own_half():
    r = lax.broadcasted_iota(jnp.int32, (CHUNK_IN, LANES), 0) // SSM_GROUP % 2
    c = lax.broadcasted_iota(jnp.int32, (CHUNK_IN, LANES), 1) // SSM_STATE
    return (r == c)[None]


def _compact_weight(w):
    tiled = jnp.tile(w.reshape(SSM_CHUNKS, CHUNK_IN, SSM_STATE), (1, 1, 2))
    return jnp.where(_own_half(), tiled, 0.0)


def _compact_pair(w_a, w_b):
    return jnp.concatenate([_compact_weight(w_a), _compact_weight(w_b)], axis=-1).astype(BF16)


def _expand_grad(g):
    kept = jnp.where(_own_half(), g, 0.0)
    return kept.reshape(SSM_CHUNKS, CHUNK_IN, 2, SSM_STATE).sum(axis=2).reshape(N_SSM_GROUPS, SSM_GROUP, SSM_STATE)


TILES_PER_BLOCK = LANES // CH_PER_TILE
IN_BLOCKS = CHUNK_IN // LANES


def _tile_masks():
    j = lax.broadcasted_iota(jnp.int32, (CHUNK_TILES, LANES), 0) % TILES_PER_BLOCK
    lane = lax.broadcasted_iota(jnp.int32, (CHUNK_TILES, LANES), 1) // CH_PER_TILE
    return (j == lane).astype(F32)


def _tile_rows(ref, j, tt):
    return ref.at[j // TILES_PER_BLOCK, pl.ds(j, tt, stride=CHUNK_TILES), :]


def _spread(ref, v, masks):
    tt = v.shape[0]
    for j in range(CHUNK_TILES):
        block = LANES * (j // TILES_PER_BLOCK)
        _tile_rows(ref, j, tt)[...] = v[:, block:block + LANES] * masks[j:j + 1, :]
    return jnp.concatenate([ref[b] for b in range(IN_BLOCKS)], axis=1).astype(BF16)


def _gather(ref, full, masks):
    tt = full.shape[0] // CHUNK_TILES
    for b in range(IN_BLOCKS):
        ref[b] = full[:, b * LANES:(b + 1) * LANES]
    out = []
    for b in range(IN_BLOCKS):
        acc = None
        for j in range(b * TILES_PER_BLOCK, (b + 1) * TILES_PER_BLOCK):
            part = _tile_rows(ref, j, tt)[...] * masks[j:j + 1, :]
            acc = part if acc is None else acc + part
        out.append(acc)
    return jnp.concatenate(out, axis=1)


def _resident(shape):
    return pl.BlockSpec(shape, lambda i: (0,) * len(shape), pipeline_mode=pl.Buffered(1))


def _halves(ref, k, rows=slice(None)):
    return ref[k, rows, :LANES], ref[k, rows, LANES:]


def _ssm_fwd(proj, w2, c2, a2, d_skip):
    t = proj.shape[0]
    tt = _tile(t, SSM_ROWS)
    rows = tt * CHUNK_TILES

    def body(u_ref, w_ref, c_ref, a_ref, d_ref, y_ref, gel_ref, s_ref, carry, spread_ref, full_ref):
        @pl.when(pl.program_id(0) == 0)
        def _():
            carry[...] = jnp.zeros_like(carry)
            spread_ref[...] = jnp.zeros_like(spread_ref)

        mask = _tile_masks()
        u = u_ref[...]
        for k in range(SSM_CHUNKS):
            uk = _spread(spread_ref, u[:, k * CHUNK_IN:(k + 1) * CHUNK_IN], mask)
            s_ref[k] = jnp.dot(uk, w_ref[k], preferred_element_type=F32)

        abar = [_halves(a_ref, k) for k in range(SSM_CHUNKS)]

        def steps(i, state):
            for v in range(SCAN_STEPS):
                r = pl.ds(pl.multiple_of((i * SCAN_STEPS + v) * CHUNK_TILES, CHUNK_TILES), CHUNK_TILES)
                new = []
                for k, ((a_re, a_im), (s_re, s_im)) in enumerate(zip(abar, state)):
                    b_re, b_im = _halves(s_ref, k, r)
                    s_re, s_im = a_re * s_re - a_im * s_im + b_re, a_re * s_im + a_im * s_re + b_im
                    s_ref[k, r, :LANES] = s_re
                    s_ref[k, r, LANES:] = s_im
                    new.append((s_re, s_im))
                state = tuple(new)
            return state

        state = lax.fori_loop(0, tt // SCAN_STEPS, steps, tuple(_halves(carry, k) for k in range(SSM_CHUNKS)))
        for k, (s_re, s_im) in enumerate(state):
            carry[k, :, :LANES] = s_re
            carry[k, :, LANES:] = s_im

        for k in range(SSM_CHUNKS):
            cols = slice(k * CHUNK_IN, (k + 1) * CHUNK_IN)
            full = lax.dot_general(s_ref[k].astype(BF16), c_ref[k], DOT_NT, preferred_element_type=F32)
            y = _gather(full_ref, full, mask) + d_ref[:, cols] * u[:, cols]
            y_ref[:, cols] = y
            gel_ref[:, cols] = (0.5 * y * (1.0 + jnp.tanh(GELU_C * (y + GELU_A * y * y * y)))).astype(BF16)

    weight = _resident((SSM_CHUNKS, CHUNK_IN, PAIR))
    tokens = pl.BlockSpec((tt, SSM_WIDTH), lambda i: (i, 0))
    return pl.pallas_call(
        body, name="ssm_fwd", grid=(t // tt,),
        in_specs=[pl.BlockSpec((tt, SSM_WIDTH), lambda i: (i, U_COLUMN_BLOCK)), weight, weight,
                  _resident((SSM_CHUNKS, CHUNK_TILES, PAIR)), _resident((1, SSM_WIDTH))],
        out_specs=[tokens, tokens, pl.BlockSpec((SSM_CHUNKS, rows, PAIR), lambda i: (0, i, 0))],
        out_shape=[jax.ShapeDtypeStruct((t, SSM_WIDTH), F32), jax.ShapeDtypeStruct((t, SSM_WIDTH), BF16),
                   jax.ShapeDtypeStruct((SSM_CHUNKS, t * CHUNK_TILES, PAIR), F32)],
        scratch_shapes=[pltpu.VMEM((SSM_CHUNKS, CHUNK_TILES, PAIR), F32), pltpu.VMEM((IN_BLOCKS, rows, LANES), F32),
                        pltpu.VMEM((IN_BLOCKS, rows, LANES), F32)],
        compiler_params=_params(("arbitrary",)),
    )(proj, w2, c2, a2, d_skip)


def _ssm_bwd(dy, proj, s, w2, c2, a2, d_skip, dproj):
    t = dy.shape[0]
    tt = _tile(t, SSM_ROWS)
    rows = tt * CHUNK_TILES
    n_chunks = t // tt

    def body(dy_ref, u_ref, s_ref, w_ref, c_ref, a_ref, d_ref, _, du_ref, gc_ref, gw_ref, ga_ref, gd_ref, z_ref, carry,
             spread_ref, full_ref):
        @pl.when(pl.program_id(0) == 0)
        def _():
            for r in (carry, gc_ref, gw_ref, ga_ref, gd_ref, spread_ref):
                r[...] = jnp.zeros_like(r)

        mask = _tile_masks()
        dy_v = dy_ref[...]
        u = u_ref[...]
        gd_ref[...] += jnp.sum(dy_v * u, axis=0, keepdims=True)
        for k in range(SSM_CHUNKS):
            dk = _spread(spread_ref, dy_v[:, k * CHUNK_IN:(k + 1) * CHUNK_IN], mask)
            z_ref[k] = jnp.dot(dk, c_ref[k], preferred_element_type=F32)
            gc_ref[k] += lax.dot_general(dk, s_ref[k].astype(BF16), DOT_TN, preferred_element_type=F32)

        abar = [_halves(a_ref, k) for k in range(SSM_CHUNKS)]

        def steps(i, state):
            zs, gs = state
            for v in range(SCAN_STEPS):
                tok = tt - 1 - (i * SCAN_STEPS + v)
                r = pl.ds(pl.multiple_of(tok * CHUNK_TILES, CHUNK_TILES), CHUNK_TILES)
                new_z, new_g = [], []
                for k, ((a_re, a_im), (z_re, z_im), (g_re, g_im)) in enumerate(zip(abar, zs, gs)):
                    s_re, s_im = _halves(s_ref, k, r)
                    g_re = g_re + z_re * s_re + z_im * s_im
                    g_im = g_im + z_im * s_re - z_re * s_im
                    d_re, d_im = _halves(z_ref, k, r)
                    z_re, z_im = d_re + a_re * z_re + a_im * z_im, d_im + a_re * z_im - a_im * z_re
                    z_ref[k, r, :LANES] = z_re
                    z_ref[k, r, LANES:] = z_im
                    new_z.append((z_re, z_im))
                    new_g.append((g_re, g_im))
                zs, gs = tuple(new_z), tuple(new_g)
            return zs, gs

        zs, gs = lax.fori_loop(0, tt // SCAN_STEPS, steps,
                               (tuple(_halves(carry, k) for k in range(SSM_CHUNKS)),
                                tuple(_halves(ga_ref, k) for k in range(SSM_CHUNKS))))
        for k in range(SSM_CHUNKS):
            carry[k, :, :LANES], carry[k, :, LANES:] = zs[k]
            ga_ref[k, :, :LANES], ga_ref[k, :, LANES:] = gs[k]

        for k in range(SSM_CHUNKS):
            cols = slice(k * CHUNK_IN, (k + 1) * CHUNK_IN)
            zb = z_ref[k].astype(BF16)
            full = lax.dot_general(zb, w_ref[k], DOT_NT, preferred_element_type=F32)
            du_ref[:, cols] = (_gather(full_ref, full, mask) + d_ref[:, cols] * dy_v[:, cols]).astype(BF16)
            uk = _spread(spread_ref, u[:, cols], mask)
            gw_ref[k] += lax.dot_general(uk, zb, DOT_TN, preferred_element_type=F32)

    weight = _resident((SSM_CHUNKS, CHUNK_IN, PAIR))
    tokens = pl.BlockSpec((tt, SSM_WIDTH), lambda i: (n_chunks - 1 - i, 0))
    grad = pl.BlockSpec((SSM_CHUNKS, CHUNK_IN, PAIR), lambda i: (0, 0, 0))
    return pl.pallas_call(
        body, name="ssm_bwd", grid=(n_chunks,),
        in_specs=[tokens, pl.BlockSpec((tt, SSM_WIDTH), lambda i: (n_chunks - 1 - i, U_COLUMN_BLOCK)),
                  pl.BlockSpec((SSM_CHUNKS, rows, PAIR), lambda i: (0, n_chunks - 1 - i, 0)), weight, weight,
                  _resident((SSM_CHUNKS, CHUNK_TILES, PAIR)), _resident((1, SSM_WIDTH)),
                  pl.BlockSpec(memory_space=pl.ANY)],
        out_specs=[pl.BlockSpec((tt, SSM_WIDTH), lambda i: (n_chunks - 1 - i, U_COLUMN_BLOCK)), grad, grad,
                   pl.BlockSpec((SSM_CHUNKS, CHUNK_TILES, PAIR), lambda i: (0, 0, 0)),
                   pl.BlockSpec((1, SSM_WIDTH), lambda i: (0, 0))],
        out_shape=[jax.ShapeDtypeStruct(dproj.shape, dproj.dtype), jax.ShapeDtypeStruct((SSM_CHUNKS, CHUNK_IN, PAIR), F32),
                   jax.ShapeDtypeStruct((SSM_CHUNKS, CHUNK_IN, PAIR), F32),
                   jax.ShapeDtypeStruct((SSM_CHUNKS, CHUNK_TILES, PAIR), F32), jax.ShapeDtypeStruct((1, SSM_WIDTH), F32)],
        input_output_aliases={7: 0},
        scratch_shapes=[pltpu.VMEM((SSM_CHUNKS, rows, PAIR), F32), pltpu.VMEM((SSM_CHUNKS, CHUNK_TILES, PAIR), F32),
                        pltpu.VMEM((IN_BLOCKS, rows, LANES), F32), pltpu.VMEM((IN_BLOCKS, rows, LANES), F32)],
        compiler_params=_params(("arbitrary",)),
    )(dy, proj, s, w2, c2, a2, d_skip, dproj)


def _block(ref, axis, size, index):
    idx = [slice(None)] * len(ref.shape)
    idx[axis] = pl.ds(pl.multiple_of(index * size, size), size)
    return ref.at[tuple(idx)]


def _all_gather(name, shards, axes):
    n = len(shards)
    sizes = [s.shape[a] for s, a in zip(shards, axes)]

    def body(*refs):
        ins, outs = refs[:n], refs[n:2 * n]
        send_sems, recv_sems, local_sems = refs[2 * n:]
        x, y, c = (lax.axis_index(a) for a in MESH_AXES)
        me, sibling = (x, y, c), (x, y, 1 - c)
        chips = [(1 - x, y), (x, 1 - y), (1 - x, 1 - y)]

        def rows(i, dev):
            return _block(outs[i], axes[i], sizes[i], 4 * dev[0] + 2 * dev[1] + dev[2])

        def copy(i, k, block, to, src=None):
            return pltpu.make_async_remote_copy(
                src_ref=rows(i, block) if src is None else src, dst_ref=rows(i, block),
                send_sem=send_sems.at[7 * i + k], recv_sem=recv_sems.at[7 * i + k],
                device_id=to, device_id_type=MESH)

        mine = [pltpu.make_async_copy(ins[i], rows(i, me), local_sems.at[i]) for i in range(n)]
        for cp in mine:
            cp.start()
        first = []
        for i in range(n):
            first.append(copy(i, 0, me, sibling, src=ins[i]))
            first += [copy(i, 1 + j, me, (*chip, c), src=ins[i]) for j, chip in enumerate(chips)]
        for cp in first:
            cp.start()
        passed = []
        for i in range(n):
            for j, chip in enumerate(chips):
                copy(i, 1 + j, (*chip, c), me).wait_recv()
                fwd = copy(i, 4 + j, (*chip, c), sibling)
                fwd.start()
                passed.append(fwd)
        for i in range(n):
            copy(i, 0, sibling, me).wait_recv()
            for j, chip in enumerate(chips):
                copy(i, 4 + j, (*chip, 1 - c), me).wait_recv()
        for cp in first + passed:
            cp.wait_send()
        for cp in mine:
            cp.wait()

    out_shape = []
    for s, a in zip(shards, axes):
        shape = list(s.shape)
        shape[a] *= N_DEV
        out_shape.append(jax.ShapeDtypeStruct(tuple(shape), s.dtype))
    any_spec = pl.BlockSpec(memory_space=pl.ANY)
    return pl.pallas_call(
        body, name=name, out_shape=out_shape,
        in_specs=[any_spec] * n, out_specs=[any_spec] * n,
        scratch_shapes=[pltpu.SemaphoreType.DMA((7 * n,)), pltpu.SemaphoreType.DMA((7 * n,)),
                        pltpu.SemaphoreType.DMA((n,))],
    )(*shards)


HBM_SPEC = pl.BlockSpec(memory_space=pltpu.HBM)
SEM_SPEC = pl.BlockSpec(memory_space=pltpu.SEMAPHORE)
ANY_SPEC = pl.BlockSpec(memory_space=pl.ANY)
SPLIT_PARAMS = pltpu.CompilerParams(has_side_effects=pltpu.SideEffectType.DATAFLOW_SIDE_EFFECTING)
N_PEERS = N_DEV - 1
TOKEN = jax.ShapeDtypeStruct((SUBLANES, LANES), F32)
VMEM_SPEC = pl.BlockSpec(memory_space=pltpu.VMEM)


def _in_hbm(arrays):
    return [pltpu.with_memory_space_constraint(a, pltpu.HBM) for a in arrays]


def _peer(m):
    x, y, c = (lax.axis_index(a) for a in MESH_AXES)
    px = 1 - x if m & 4 else x
    py = 1 - y if m & 2 else y
    pc = 1 - c if m & 1 else c
    return (px, py, pc), 4 * px + 2 * py + pc


def _my_index():
    x, y, c = (lax.axis_index(a) for a in MESH_AXES)
    return 4 * x + 2 * y + c


def _gather_copies(shard_refs, full_refs, axes, send_sems, recv_sems):
    copies = []
    for i, (shard, full) in enumerate(zip(shard_refs, full_refs)):
        mine = _block(full, axes[i], shard.shape[axes[i]], _my_index())
        for m in range(1, N_DEV):
            peer, _ = _peer(m)
            copies.append(pltpu.make_async_remote_copy(
                src_ref=shard, dst_ref=mine, send_sem=send_sems.at[N_PEERS * i + m - 1],
                recv_sem=recv_sems.at[N_PEERS * i + m - 1], device_id=peer, device_id_type=MESH))
    return copies


def _gather_start(name, shards, axes, after):
    n = len(shards)

    def body(*refs):
        shard_refs = refs[:n]
        send_sems, recv_sems, local_sems = refs[n + 1:n + 4]
        full_refs = refs[2 * n + 4:3 * n + 4]
        refs[3 * n + 4][...] = jnp.zeros(TOKEN.shape, TOKEN.dtype)
        for i in range(n):
            pltpu.make_async_copy(shard_refs[i], _block(full_refs[i], axes[i], shard_refs[i].shape[axes[i]], _my_index()),
                                  local_sems.at[i]).start()
        for cp in _gather_copies(shard_refs, full_refs, axes, send_sems, recv_sems):
            cp.start()

    fulls = []
    for s, a in zip(shards, axes):
        shape = list(s.shape)
        shape[a] *= N_DEV
        fulls.append(pltpu.HBM(tuple(shape), s.dtype))
    out = pl.pallas_call(
        body, name=name,
        out_shape=(pltpu.SemaphoreType.DMA((N_PEERS * n,)), pltpu.SemaphoreType.DMA((N_PEERS * n,)),
                   pltpu.SemaphoreType.DMA((n,)), *[pltpu.HBM(s.shape, s.dtype) for s in shards], *fulls, TOKEN),
        in_specs=[HBM_SPEC] * n + [ANY_SPEC],
        out_specs=(SEM_SPEC, SEM_SPEC, SEM_SPEC, *[HBM_SPEC] * (2 * n), VMEM_SPEC),
        input_output_aliases={i: 3 + i for i in range(n)},
        compiler_params=SPLIT_PARAMS,
    )(*_in_hbm(shards), after)
    return out[:-1], out[-1]


def _gather_wait(name, started, indices, axes, after):
    send_sems, recv_sems, local_sems = started[:3]
    n_all = (len(started) - 3) // 2
    shards = [started[3 + i] for i in indices]
    fulls = [started[3 + n_all + i] for i in indices]
    n = len(indices)

    def body(*refs):
        shard_refs, full_refs = refs[:n], refs[n:2 * n]
        send_sems, recv_sems, local_sems = refs[2 * n:2 * n + 3]
        for j, i in enumerate(indices):
            mine = _block(full_refs[j], axes[j], shard_refs[j].shape[axes[j]], _my_index())
            pltpu.make_async_copy(shard_refs[j], mine, local_sems.at[i]).wait()
            for m in range(1, N_DEV):
                peer, _ = _peer(m)
                cp = pltpu.make_async_remote_copy(
                    src_ref=shard_refs[j], dst_ref=mine, send_sem=send_sems.at[N_PEERS * i + m - 1],
                    recv_sem=recv_sems.at[N_PEERS * i + m - 1], device_id=peer, device_id_type=MESH)
                cp.wait_send()
                cp.wait_recv()

    out = pl.pallas_call(
        body, name=name,
        out_shape=tuple(pltpu.HBM(a.shape, a.dtype) for a in shards + fulls),
        in_specs=[HBM_SPEC] * (2 * n) + [SEM_SPEC] * 3 + [ANY_SPEC], out_specs=tuple([HBM_SPEC] * (2 * n)),
        input_output_aliases={i: i for i in range(2 * n)},
        compiler_params=SPLIT_PARAMS,
    )(*shards, *fulls, send_sems, recv_sems, local_sems, after)
    return out[n:]


def _exchange_start(name, fulls, axes):
    n = len(fulls)
    sizes = [f.shape[a] // N_DEV for f, a in zip(fulls, axes)]

    def body(*refs):
        ins = refs[:n]
        send_sems, recv_sems = refs[n:n + 2]
        lands = refs[2 * n + 2:3 * n + 2]
        refs[3 * n + 2][...] = jnp.zeros(TOKEN.shape, TOKEN.dtype)
        for i in range(n):
            for m in range(1, N_DEV):
                peer, index = _peer(m)
                pltpu.make_async_remote_copy(
                    src_ref=_block(ins[i], axes[i], sizes[i], index), dst_ref=lands[i].at[m - 1],
                    send_sem=send_sems.at[N_PEERS * i + m - 1], recv_sem=recv_sems.at[N_PEERS * i + m - 1],
                    device_id=peer, device_id_type=MESH).start()

    lands = []
    for f, a, size in zip(fulls, axes, sizes):
        shape = list(f.shape)
        shape[a] = size
        lands.append(pltpu.HBM((N_PEERS, *shape), f.dtype))
    out = pl.pallas_call(
        body, name=name,
        out_shape=(pltpu.SemaphoreType.DMA((N_PEERS * n,)), pltpu.SemaphoreType.DMA((N_PEERS * n,)),
                   *[pltpu.HBM(f.shape, f.dtype) for f in fulls], *lands, TOKEN),
        in_specs=[HBM_SPEC] * n, out_specs=(SEM_SPEC, SEM_SPEC, *[HBM_SPEC] * (2 * n), VMEM_SPEC),
        input_output_aliases={i: 2 + i for i in range(n)},
        compiler_params=SPLIT_PARAMS,
    )(*_in_hbm(fulls))
    return out[:-1], out[-1]


def _exchange_wait(name, started, axes, after):
    send_sems, recv_sems = started[:2]
    n = (len(started) - 2) // 2
    fulls, lands = list(started[2:2 + n]), list(started[2 + n:])
    sizes = [f.shape[a] // N_DEV for f, a in zip(fulls, axes)]

    def body(*refs):
        ins, land_refs = refs[:n], refs[n:2 * n]
        send_sems, recv_sems = refs[2 * n:2 * n + 2]
        for i in range(n):
            for m in range(1, N_DEV):
                peer, index = _peer(m)
                cp = pltpu.make_async_remote_copy(
                    src_ref=_block(ins[i], axes[i], sizes[i], index), dst_ref=land_refs[i].at[m - 1],
                    send_sem=send_sems.at[N_PEERS * i + m - 1], recv_sem=recv_sems.at[N_PEERS * i + m - 1],
                    device_id=peer, device_id_type=MESH)
                cp.wait_send()
                cp.wait_recv()

    out = pl.pallas_call(
        body, name=name,
        out_shape=tuple(pltpu.HBM(a.shape, a.dtype) for a in fulls + lands),
        in_specs=[HBM_SPEC] * (2 * n) + [SEM_SPEC] * 2 + [ANY_SPEC], out_specs=tuple([HBM_SPEC] * (2 * n)),
        input_output_aliases={i: i for i in range(2 * n)},
        compiler_params=SPLIT_PARAMS,
    )(*fulls, *lands, send_sems, recv_sems, after)
    return out[:n], out[n:]


def _sum_parts(part_refs, ndim):
    g = None
    for p_ref in part_refs:
        stacked = len(p_ref.shape) > ndim
        terms = [p_ref[s] for s in range(p_ref.shape[0])] if stacked else [p_ref[...]]
        for term in terms:
            term = term.astype(F32)
            g = term if g is None else g + term
    return g


def _adamw_update(w_ref, m_ref, v_ref, g, g_ref, d_ref, nm_ref, nv_ref):
    c1 = 1.0 - ADAM_B1 ** ADAM_STEP
    c2 = 1.0 - ADAM_B2 ** ADAM_STEP
    new_m = ADAM_B1 * m_ref[...] + (1.0 - ADAM_B1) * g
    new_v = ADAM_B2 * v_ref[...] + (1.0 - ADAM_B2) * (g * g)
    g_ref[...] = g
    nm_ref[...] = new_m
    nv_ref[...] = new_v
    d_ref[...] = -ADAM_LR * ((new_m / c1) / (jnp.sqrt(new_v / c2) + ADAM_EPS) + ADAM_WD * w_ref[...])


def _adamw_small(ws, ms, vs, stacks, loss_stack):
    n = len(ws)

    def body(*refs):
        ins, outs = refs[:4 * n + 1], refs[4 * n + 1:]
        for i in range(n):
            _adamw_update(ins[i], ins[n + i], ins[2 * n + i], _sum_parts([ins[3 * n + i]], len(ins[i].shape)),
                          outs[i], outs[n + i], outs[2 * n + i], outs[3 * n + i])
        total = ins[4 * n][0]
        for dev in range(1, N_DEV):
            total = total + ins[4 * n][dev]
        outs[4 * n][...] = total

    res = pl.pallas_call(
        body, name="adamw_small",
        out_shape=[jax.ShapeDtypeStruct(w.shape, F32) for w in ws] * 4 + [jax.ShapeDtypeStruct((1, LANES), F32)],
        compiler_params=_params(None),
    )(*ws, *ms, *vs, *stacks, loss_stack)
    return res[:n], res[n:2 * n], res[2 * n:3 * n], res[3 * n:4 * n], res[4 * n]


def _adamw(name, w, m, v, parts, own=None):
    r, c = w.shape
    tr = _tile(r, 256)
    steps = r // tr
    n_index = int(own is not None)
    n_parts = len(parts) + n_index

    def body(*refs):
        refs = refs[n_index:]
        _adamw_update(refs[0], refs[1], refs[2], _sum_parts(refs[3:3 + n_parts], 2), *refs[3 + n_parts:])

    row = pl.BlockSpec((tr, c), lambda i, *_: (i, 0))
    in_specs, operands, index = [row, row, row], [w, m, v], []
    if own is not None:
        full, axis, which = own
        own_map = (lambda i, me: (me[0] * steps + i, 0)) if axis == 0 else (lambda i, me: (i, me[0]))
        in_specs.append(pl.BlockSpec((tr, c), own_map))
        operands.append(full)
        index = [which.astype(jnp.int32).reshape(1)]
    for p in parts:
        in_specs.append(row if p.ndim == 2 else pl.BlockSpec((p.shape[0], tr, c), lambda i, *_: (0, i, 0)))
    return pl.pallas_call(
        body, name=name,
        grid_spec=pltpu.PrefetchScalarGridSpec(
            num_scalar_prefetch=n_index, grid=(steps,), in_specs=in_specs, out_specs=[row] * 4),
        out_shape=[jax.ShapeDtypeStruct((r, c), F32)] * 4,
        compiler_params=_params(("arbitrary",)),
    )(*index, *operands, *parts)


SMALL = ("norm_gain", "pool_scale", "a_re", "a_im", "log_dt", "b_re", "b_im", "c_re", "c_im", "d_skip", "final_gain")
LARGE = ("w_in", "w_pool", "w_glu", "w_out", "w_ple", "w_ple_gate")
LARGE_AXIS = {"w_in": 1, "w_pool": 1, "w_glu": 1, "w_out": 0, "w_ple": 1, "w_ple_gate": 0}
WEIGHTS = ("norm_gain", "w_in", "w_pool", "pool_scale", "a_re", "a_im", "log_dt", "b_re", "b_im", "c_re", "c_im",
           "d_skip", "w_glu", "w_out", "w_ple", "w_ple_gate", "final_gain")


def kernel(x, p, norm_gain, w_in, w_pool, pool_scale, a_re, a_im, log_dt, b_re, b_im, c_re, c_im, d_skip, w_glu, w_out, w_ple, w_ple_gate, final_gain, loss_target, m_norm_gain, m_w_in, m_w_pool, m_pool_scale, m_a_re, m_a_im, m_log_dt, m_b_re, m_b_im, m_c_re, m_c_im, m_d_skip, m_w_glu, m_w_out, m_w_ple, m_w_ple_gate, m_final_gain, v_norm_gain, v_w_in, v_w_pool, v_pool_scale, v_a_re, v_a_im, v_log_dt, v_b_re, v_b_im, v_c_re, v_c_im, v_d_skip, v_w_glu, v_w_out, v_w_ple, v_w_ple_gate, v_final_gain):
    weights = dict(norm_gain=norm_gain, w_in=w_in, w_pool=w_pool, pool_scale=pool_scale, a_re=a_re, a_im=a_im,
                   log_dt=log_dt, b_re=b_re, b_im=b_im, c_re=c_re, c_im=c_im, d_skip=d_skip, w_glu=w_glu,
                   w_out=w_out, w_ple=w_ple, w_ple_gate=w_ple_gate, final_gain=final_gain)
    mom_m = dict(norm_gain=m_norm_gain, w_in=m_w_in, w_pool=m_w_pool, pool_scale=m_pool_scale, a_re=m_a_re,
                 a_im=m_a_im, log_dt=m_log_dt, b_re=m_b_re, b_im=m_b_im, c_re=m_c_re, c_im=m_c_im,
                 d_skip=m_d_skip, w_glu=m_w_glu, w_out=m_w_out, w_ple=m_w_ple, w_ple_gate=m_w_ple_gate,
                 final_gain=m_final_gain)
    mom_v = dict(norm_gain=v_norm_gain, w_in=v_w_in, w_pool=v_w_pool, pool_scale=v_pool_scale, a_re=v_a_re,
                 a_im=v_a_im, log_dt=v_log_dt, b_re=v_b_re, b_im=v_b_im, c_re=v_c_re, c_im=v_c_im,
                 d_skip=v_d_skip, w_glu=v_w_glu, w_out=v_w_out, w_ple=v_w_ple, w_ple_gate=v_w_ple_gate,
                 final_gain=v_final_gain)

    t = x.shape[1]
    xs = x.reshape(t, D_MODEL)
    ps = p.reshape(t, PLE_DIM)
    target = loss_target.reshape(t, D_MODEL)
    gain1 = norm_gain.reshape(1, D_MODEL)
    gain_f = final_gain.reshape(1, D_MODEL)
    scale_p = pool_scale.reshape(1, POOL_WIDTH)
    skip = d_skip.reshape(1, SSM_WIDTH)

    shard2d = {k: weights[k][0] for k in LARGE}
    shard_bf = {k: shard2d[k].astype(BF16) for k in LARGE}
    full = {"w_in": _all_gather("w_in_all_gather", [shard_bf["w_in"]], [LARGE_AXIS["w_in"]])[0]}
    later = [k for k in LARGE if k != "w_in"]
    later_axes = [LARGE_AXIS[k] for k in later]
    gather, gather_token = _gather_start("weights_gather_start", [shard_bf[k] for k in later], later_axes,
                                         full["w_in"])

    def arrive(k, after):
        i = later.index(k)
        full[k] = _gather_wait("gather_wait_" + k, gather, [i], [later_axes[i]], after)[0]

    ar, ai = a_re[0], a_im[0]
    ldt = log_dt.reshape(N_SSM_GROUPS, 1)
    br_t = jnp.transpose(b_re[0], (0, 2, 1))
    bi_t = jnp.transpose(b_im[0], (0, 2, 1))
    ab_re, ab_im, bb_re, bb_im = _ssm_params(ar, ai, ldt, br_t, bi_t)
    tiles = (SSM_CHUNKS, CHUNK_TILES, LANES)
    abar = jnp.concatenate([ab_re.reshape(tiles), ab_im.reshape(tiles)], axis=-1)
    w_pair = _compact_pair(bb_re, bb_im)
    c_pair = _compact_pair(c_re[0], -c_im[0])

    hn, proj = _norm1_in_proj(xs, gain1, full["w_in"], gather_token)
    arrive("w_pool", proj)
    pooled, mixed = _pool_mix(proj, full["w_pool"])
    y, gel, states = _ssm_fwd(proj, w_pair, c_pair, abar, skip)
    arrive("w_glu", gel)
    hg = _mm_nn("glu_proj", gel, full["w_glu"], [F32])[0]
    arrive("w_out", hg)
    cat, h1, h1b = _gate_out_proj(mixed, proj, hg, scale_p, full["w_out"], xs)
    arrive("w_ple", h1b)
    arrive("w_ple_gate", h1b)
    de, dq, dh2, g_final_gain, loss_part = _ple_final(h1, h1b, ps, full["w_ple_gate"], full["w_ple"], target, gain_f)

    grads = {}
    grads["w_ple_gate"] = _mm_tn("ple_gate_wgrad", h1b, dq, BF16)
    grads["w_ple"] = _mm_tn("ple_wgrad", ps, de, BF16)
    sent, tokens = {}, {}

    def send(names):
        sent[names], tokens[names[0]] = _exchange_start(
            "grads_start_" + names[0], [grads[k] for k in names], [LARGE_AXIS[k] for k in names])

    send(("w_ple_gate", "w_ple"))
    dh1, dh1b = _residual_dgrad("ple_gate_dgrad", dq, full["w_ple_gate"], dh2)
    grads["w_out"] = _mm_tn("out_wgrad", cat, dh1b, BF16)
    send(("w_out",))
    dmixed, dproj, dhg, g_pool_scale = _out_dgrad_gate_bwd(
        dh1b, full["w_out"], mixed, proj, hg, scale_p, [tokens["w_ple_gate"], tokens["w_out"]])

    tk = _tile(t, 1024)
    grads["w_pool"] = _mm("pool_wgrad", [(pooled, (tk, POOL_GROUP), lambda i, j, s: (s, i),
                                          dmixed, (tk, POOL_GROUP), lambda i, j, s: (s, i))],
                          DOT_TN, (N_POOL_GROUPS, 1, t // tk),
                          [((N_POOL_GROUPS, POOL_GROUP, POOL_GROUP), BF16, (None, POOL_GROUP, POOL_GROUP),
                            lambda i, j, s: (i, 0, 0))], t // tk)[0]
    dproj = _pool_mix_bwd(dmixed, full["w_pool"], dproj)

    grads["w_glu"] = _mm_tn("glu_wgrad", gel, dhg, BF16)
    send(("w_pool", "w_glu"))

    def gelu_bwd_epilogue(acc, ex, out_refs):
        yv = ex[0][...]
        th = jnp.tanh(GELU_C * (yv + GELU_A * yv * yv * yv))
        dgelu = 0.5 * (1.0 + th) + 0.5 * yv * (1.0 - th * th) * GELU_C * (1.0 + 3.0 * GELU_A * yv * yv)
        out_refs[0][...] = acc * dgelu

    dy = _mm_nt("glu_dgrad", dhg, full["w_glu"], [F32], tk=2048, extras=[y], epilogue=gelu_bwd_epilogue,
                after=[tokens["w_pool"]])[0]
    dproj, g_c_pair, g_w_pair, g_abar, g_d_skip = _ssm_bwd(dy, proj, states, w_pair, c_pair, abar, skip, dproj)

    g_ab_re = g_abar[..., :LANES].reshape(N_SSM_GROUPS, SSM_STATE)
    g_ab_im = g_abar[..., LANES:].reshape(N_SSM_GROUPS, SSM_STATE)
    d_ar, d_ai, d_ldt, d_br_t, d_bi_t = _ssm_params_bwd(
        ar, ai, ldt, br_t, bi_t, g_ab_re, g_ab_im,
        _expand_grad(g_w_pair[..., :LANES]), _expand_grad(g_w_pair[..., LANES:]))

    small_grads = dict(
        pool_scale=g_pool_scale, a_re=d_ar, a_im=d_ai, log_dt=d_ldt.reshape(1, N_SSM_GROUPS),
        b_re=d_br_t.astype(BF16), b_im=d_bi_t.astype(BF16), c_re=_expand_grad(g_c_pair[..., :LANES]).astype(BF16),
        c_im=(-_expand_grad(g_c_pair[..., LANES:])).astype(BF16), d_skip=g_d_skip, final_gain=g_final_gain)
    early = [k for k in SMALL if k != "norm_gain"]
    early_sent, early_token = _gather_start(
        "small_grads_start", [small_grads[k][None] for k in early] + [jnp.broadcast_to(loss_part, (1, 1, LANES))],
        [0] * (len(early) + 1), d_ar)

    grads["w_in"] = _mm_tn("in_wgrad", hn, dproj, BF16, tn=dproj.shape[1], after=[early_token])
    send(("w_in",))
    grad_x, g_norm_gain = _in_dgrad_norm1_bwd(dproj, full["w_in"], xs, dh1, gain1, tokens["w_in"])
    late_sent, late_token = _gather_start("norm_gain_grad_start", [g_norm_gain[None]], [0], g_norm_gain)

    out_g, out_d, out_m, out_v = ({} for _ in range(4))
    me = 4 * lax.axis_index("x") + 2 * lax.axis_index("y") + lax.axis_index("c")
    after = late_token
    for names, started in sent.items():
        axes = [LARGE_AXIS[k] for k in names]
        partials, landed = _exchange_wait("grads_wait_" + names[0], started, axes, after)
        for k, axis, partial, land in zip(names, axes, partials, landed):
            shard_shape = shard2d[k].shape
            size = shard_shape[axis]
            view = (-1, shard_shape[-1])
            rows = math.prod(shard_shape[:-1])
            parts = [land.reshape(N_PEERS, rows, shard_shape[-1])]
            if len(shard_shape) == 2:
                own = (partial, axis, me)
            else:
                own = None
                parts.insert(0, lax.dynamic_slice_in_dim(partial, me * size, size, axis=axis).reshape(view))
            res = _adamw("adamw_" + k, shard2d[k].reshape(view), mom_m[k][0].reshape(view), mom_v[k][0].reshape(view),
                         parts, own)
            out_g[k], out_d[k], out_m[k], out_v[k] = (r.reshape(weights[k].shape) for r in res)
            after = res[0]

    def b_view(a):
        return jnp.transpose(a[0], (0, 2, 1))

    views = dict(norm_gain=lambda a: a, pool_scale=lambda a: a, a_re=lambda a: a[0], a_im=lambda a: a[0],
                 log_dt=lambda a: a, b_re=b_view, b_im=b_view, c_re=lambda a: a[0], c_im=lambda a: a[0],
                 d_skip=lambda a: a, final_gain=lambda a: a.reshape(1, D_MODEL))
    landed = _gather_wait("small_grads_wait", early_sent, list(range(len(early) + 1)), [0] * (len(early) + 1), after)
    stack = dict(zip(early, landed))
    stack["norm_gain"] = _gather_wait("norm_gain_grad_wait", late_sent, [0], [0], after)[0]
    *small_out, loss_row = _adamw_small(
        [views[k](weights[k]) for k in SMALL], [views[k](mom_m[k]) for k in SMALL],
        [views[k](mom_v[k]) for k in SMALL], [stack[k] for k in SMALL], landed[-1])
    loss = loss_row[0, 0]
    for out, res in zip((out_g, out_d, out_m, out_v), small_out):
        for k, r in zip(SMALL, res):
            if k in ("b_re", "b_im"):
                r = jnp.transpose(r, (0, 2, 1))
            out[k] = r.reshape(weights[k].shape)

    return (loss, grad_x.reshape(x.shape), *[out_g[k] for k in WEIGHTS], *[out_d[k] for k in WEIGHTS],
            *[out_m[k] for k in WEIGHTS], *[out_v[k] for k in WEIGHTS])
```

```python
import math

import jax
import jax.numpy as jnp
from jax import lax
from jax.experimental import pallas as pl
from jax.experimental.pallas import tpu as pltpu

F32 = jnp.float32
BF16 = jnp.bfloat16
MESH = pl.DeviceIdType.MESH
MESH_AXES = ("x", "y", "c")
N_DEV = 8

D_MODEL = 2048
POOL_WIDTH = 1024
SSM_WIDTH = 1024
N_POOL_GROUPS = 4
POOL_GROUP = 256
SSM_GROUP = 16
N_SSM_GROUPS = 64
SSM_STATE = 64
SSM_FLAT = N_SSM_GROUPS * SSM_STATE
SSM_CHUNKS = 4
CHUNK_IN = SSM_WIDTH // SSM_CHUNKS
CHUNK_STATE = SSM_FLAT // SSM_CHUNKS
PLE_DIM = 256
EPS = 1e-6
A_RE_MAX = -1e-4
ADAM_LR = 0.001
ADAM_B1 = 0.9
ADAM_B2 = 0.999
ADAM_EPS = 1e-08
ADAM_WD = 0.01
ADAM_STEP = 10
GELU_C = math.sqrt(2.0 / math.pi)
GELU_A = 0.044715

SUBLANES = 8
LANES = 128
VMEM_LIMIT_BYTES = 48 * 1024 * 1024

DOT_NN = (((1,), (0,)), ((), ()))
DOT_NT = (((1,), (1,)), ((), ()))
DOT_TN = (((0,), (0,)), ((), ()))


def _tile(n, pref):
    return pref if n % pref == 0 else n


def _params(sem):
    return pltpu.CompilerParams(dimension_semantics=sem, vmem_limit_bytes=VMEM_LIMIT_BYTES)


def _sigmoid(v):
    return 1.0 / (1.0 + jnp.exp(-v))


def _silu_and_grad(v):
    s = _sigmoid(v)
    return v * s, s * (1.0 + v * (1.0 - s))


def _mm(name, pairs, dims, grid, outs, k_steps, extras=(), epilogue=None):
    n_pairs, n_ex, n_out = len(pairs), len(extras), len(outs)
    acc_shape = tuple(d for d in outs[0][2] if d is not None)
    if epilogue is None:
        def epilogue(acc, ex, out_refs):
            out_refs[0][...] = acc.astype(out_refs[0].dtype)

    def body(*refs):
        ab = refs[:2 * n_pairs]
        ex = refs[2 * n_pairs:2 * n_pairs + n_ex]
        out_refs = refs[2 * n_pairs + n_ex:2 * n_pairs + n_ex + n_out]
        acc = refs[-1]
        k = pl.program_id(2)

        @pl.when(k == 0)
        def _():
            acc[...] = jnp.zeros_like(acc)

        part = None
        for q in range(n_pairs):
            d = lax.dot_general(ab[2 * q][...].astype(BF16), ab[2 * q + 1][...].astype(BF16), dims,
                                preferred_element_type=F32)
            part = d if part is None else part + d
        acc[...] += part

        @pl.when(k == k_steps - 1)
        def _():
            epilogue(acc[...], ex, out_refs)

    in_specs, operands = [], []
    for a, a_blk, a_map, b, b_blk, b_map in pairs:
        in_specs += [pl.BlockSpec(a_blk, a_map), pl.BlockSpec(b_blk, b_map)]
        operands += [a, b]
    for e, e_blk, e_map in extras:
        in_specs.append(pl.BlockSpec(e_blk, e_map))
        operands.append(e)
    return pl.pallas_call(
        body, name=name, grid=grid, in_specs=in_specs,
        out_specs=[pl.BlockSpec(o[2], o[3]) for o in outs],
        out_shape=[jax.ShapeDtypeStruct(o[0], o[1]) for o in outs],
        scratch_shapes=[pltpu.VMEM(acc_shape, F32)],
        compiler_params=_params(("arbitrary", "arbitrary", "arbitrary")),
    )(*operands)


def _after(tokens):
    return [(tok, tok.shape, lambda i, j, s: (0, 0)) for tok in tokens]


def _mm_nn(name, a, b, out_dtypes, tm=1024, tn=1024, tk=1024, a_col0=0, extras=(), epilogue=None, after=()):
    m, n = a.shape[0], b.shape[1]
    k = b.shape[0]
    tm, tn, tk = _tile(m, tm), _tile(n, tn), _tile(k, tk)
    outs = [((m, n), dt, (tm, tn), lambda i, j, s: (i, j)) for dt in out_dtypes]
    ex = [(e, (tm, tn), lambda i, j, s: (i, j)) for e in extras] + _after(after)
    return _mm(name, [(a, (tm, tk), lambda i, j, s: (i, a_col0 + s), b, (tk, tn), lambda i, j, s: (s, j))],
               DOT_NN, (m // tm, n // tn, k // tk), outs, k // tk, ex, epilogue)


def _mm_nt(name, a, b, out_dtypes, tm=1024, tn=1024, tk=1024, extras=(), epilogue=None, after=()):
    m, kk = a.shape
    n = b.shape[0]
    tm, tn, tk = _tile(m, tm), _tile(n, tn), _tile(kk, tk)
    outs = [((m, n), dt, (tm, tn), lambda i, j, s: (i, j)) for dt in out_dtypes]
    ex = [(e, (tm, tn), lambda i, j, s: (i, j)) for e in extras] + _after(after)
    return _mm(name, [(a, (tm, tk), lambda i, j, s: (i, s), b, (tn, tk), lambda i, j, s: (j, s))],
               DOT_NT, (m // tm, n // tn, kk // tk), outs, kk // tk, ex, epilogue)


def _mm_tn(name, a, b, out_dtype, tm=512, tn=2048, tk=1024, after=()):
    m, kk = a.shape
    n = b.shape[1]
    tm, tn, tk = _tile(kk, tm), _tile(n, tn), _tile(m, tk)
    outs = [((kk, n), out_dtype, (tm, tn), lambda i, j, s: (i, j))]
    return _mm(name, [(a, (tk, tm), lambda i, j, s: (s, i), b, (tk, tn), lambda i, j, s: (s, j))],
               DOT_TN, (kk // tm, n // tn, m // tk), outs, m // tk, _after(after))[0]


ROW_TILE = 256


def _norm1_in_proj(x, gain, w_in, after):
    t = x.shape[0]
    tm = _tile(t, ROW_TILE)
    n = w_in.shape[1]

    def body(x_ref, g_ref, w_ref, _, hn_ref, proj_ref):
        xv = x_ref[...]
        r = lax.rsqrt(jnp.mean(xv * xv, axis=-1, keepdims=True) + EPS)
        hn = (xv * r * g_ref[...]).astype(BF16)
        hn_ref[...] = hn
        proj_ref[...] = jnp.dot(hn, w_ref[...], preferred_element_type=F32)

    row = pl.BlockSpec((tm, D_MODEL), lambda i: (i, 0))
    return pl.pallas_call(
        body, name="norm1_in_proj", grid=(t // tm,),
        in_specs=[row, pl.BlockSpec((1, D_MODEL), lambda i: (0, 0)), _resident(w_in.shape),
                  pl.BlockSpec(after.shape, lambda i: (0, 0))],
        out_specs=[row, pl.BlockSpec((tm, n), lambda i: (i, 0))],
        out_shape=[jax.ShapeDtypeStruct((t, D_MODEL), BF16), jax.ShapeDtypeStruct((t, n), F32)],
        compiler_params=_params(("arbitrary",)),
    )(x, gain, w_in, after)


def _in_dgrad_norm1_bwd(dproj, w_in, x, dh1, gain, after):
    t = x.shape[0]
    tm = _tile(t, ROW_TILE)

    def body(dp_ref, w_ref, x_ref, dh1_ref, g_ref, _, dx_ref, gg_ref):
        @pl.when(pl.program_id(0) == 0)
        def _():
            gg_ref[...] = jnp.zeros_like(gg_ref)

        dhn = lax.dot_general(dp_ref[...], w_ref[...], DOT_NT, preferred_element_type=F32)
        xv = x_ref[...]
        r = lax.rsqrt(jnp.mean(xv * xv, axis=-1, keepdims=True) + EPS)
        xh = xv * r
        gg_ref[...] += jnp.sum(dhn * xh, axis=0, keepdims=True)
        dxh = dhn * g_ref[...]
        dx_ref[...] = dh1_ref[...] + r * (dxh - xh * jnp.mean(dxh * xh, axis=-1, keepdims=True))

    row = pl.BlockSpec((tm, D_MODEL), lambda i: (i, 0))
    vec = pl.BlockSpec((1, D_MODEL), lambda i: (0, 0))
    return pl.pallas_call(
        body, name="in_dgrad_norm1_bwd", grid=(t // tm,),
        in_specs=[pl.BlockSpec((tm, dproj.shape[1]), lambda i: (i, 0)), _resident(w_in.shape), row, row, vec,
                  pl.BlockSpec(after.shape, lambda i: (0, 0))],
        out_specs=[row, vec],
        out_shape=[jax.ShapeDtypeStruct((t, D_MODEL), F32), jax.ShapeDtypeStruct((1, D_MODEL), F32)],
        compiler_params=_params(("arbitrary",)),
    )(dproj, w_in, x, dh1, gain, after)


def _pool_counts(t, width, group):
    row = lax.broadcasted_iota(jnp.int32, (t, width), 0)
    window = jnp.left_shift(jnp.int32(2), group)
    return row, jnp.minimum(row + 1, window).astype(F32)


def _select_window(group, s2, s4, s8, s16):
    return jnp.where(group == 0, s2, jnp.where(group == 1, s4, jnp.where(group == 2, s8, s16)))


def _pool_mix(proj, w_pool):
    t = proj.shape[0]

    def body(u_ref, w_ref, pooled_ref, mixed_ref):
        group = pl.program_id(0)
        row, count = _pool_counts(t, LANES, group)

        def down(a, j):
            return jnp.where(row >= j, pltpu.roll(a, j, 0), 0.0)

        for h in range(POOL_GROUP // LANES):
            cols = slice(h * LANES, (h + 1) * LANES)
            v = u_ref[:, cols]
            s2 = v + down(v, 1)
            s4 = s2 + down(s2, 2)
            s8 = s4 + down(s4, 4)
            s16 = s8 + down(s8, 8)
            pooled_ref[:, cols] = (_select_window(group, s2, s4, s8, s16) / count - v).astype(BF16)
        mixed_ref[...] = jnp.dot(pooled_ref[...], w_ref[...], preferred_element_type=F32)

    block = pl.BlockSpec((t, POOL_GROUP), lambda g: (0, g))
    return pl.pallas_call(
        body, name="pool_mix", grid=(N_POOL_GROUPS,),
        in_specs=[block, pl.BlockSpec((None, POOL_GROUP, POOL_GROUP), lambda g: (g, 0, 0))],
        out_specs=[block, block],
        out_shape=[jax.ShapeDtypeStruct((t, POOL_WIDTH), BF16), jax.ShapeDtypeStruct((t, POOL_WIDTH), F32)],
        compiler_params=_params(("arbitrary",)),
    )(proj, w_pool)


def _pool_mix_bwd(dmixed, w_pool, dproj):
    t = dmixed.shape[0]

    def body(dm_ref, w_ref, _, o_ref):
        group = pl.program_id(0)
        row, count = _pool_counts(t, LANES, group)

        def up(a, j):
            return jnp.where(row < t - j, pltpu.roll(a, t - j, 0), 0.0)

        dpooled = lax.dot_general(dm_ref[...], w_ref[...], DOT_NT, preferred_element_type=F32)
        for h in range(POOL_GROUP // LANES):
            cols = slice(h * LANES, (h + 1) * LANES)
            dp = dpooled[:, cols]
            r = dp / count
            s2 = r + up(r, 1)
            s4 = s2 + up(s2, 2)
            s8 = s4 + up(s4, 4)
            s16 = s8 + up(s8, 8)
            o_ref[:, cols] = (_select_window(group, s2, s4, s8, s16) - dp).astype(BF16)

    block = pl.BlockSpec((t, POOL_GROUP), lambda g: (0, g))
    return pl.pallas_call(
        body, name="pool_mix_bwd", grid=(N_POOL_GROUPS,),
        in_specs=[block, pl.BlockSpec((None, POOL_GROUP, POOL_GROUP), lambda g: (g, 0, 0)),
                  pl.BlockSpec(memory_space=pl.ANY)],
        out_specs=block,
        out_shape=jax.ShapeDtypeStruct(dproj.shape, dproj.dtype),
        input_output_aliases={2: 0},
        compiler_params=_params(("arbitrary",)),
    )(dmixed, w_pool, dproj)


def _gate_out_proj(mixed, proj, hg, pool_scale, w_out, x):
    t = mixed.shape[0]
    tm = _tile(t, ROW_TILE)

    def body(mx_ref, ga_ref, gb_ref, hg_ref, ps_ref, w_ref, x_ref, cat_ref, h1_ref, h1b_ref):
        silu_a, _ = _silu_and_grad(ga_ref[...])
        cat_ref[:, :POOL_WIDTH] = (mx_ref[...] * ps_ref[...] * silu_a).astype(BF16)
        silu_b, _ = _silu_and_grad(gb_ref[...])
        sb = hg_ref[:, :SSM_WIDTH] * _sigmoid(hg_ref[:, SSM_WIDTH:])
        cat_ref[:, POOL_WIDTH:] = (sb * silu_b).astype(BF16)
        h1 = x_ref[...] + jnp.dot(cat_ref[...], w_ref[...], preferred_element_type=F32)
        h1_ref[...] = h1
        h1b_ref[...] = h1.astype(BF16)

    row = pl.BlockSpec((tm, D_MODEL), lambda i: (i, 0))
    return pl.pallas_call(
        body, name="gate_out_proj", grid=(t // tm,),
        in_specs=[pl.BlockSpec((tm, POOL_WIDTH), lambda i: (i, 0)),
                  pl.BlockSpec((tm, POOL_WIDTH), lambda i: (i, 1)),
                  pl.BlockSpec((tm, SSM_WIDTH), lambda i: (i, 3)),
                  pl.BlockSpec((tm, 2 * SSM_WIDTH), lambda i: (i, 0)),
                  pl.BlockSpec((1, POOL_WIDTH), lambda i: (0, 0)), _resident(w_out.shape), row],
        out_specs=[row, row, row],
        out_shape=[jax.ShapeDtypeStruct((t, D_MODEL), BF16), jax.ShapeDtypeStruct((t, D_MODEL), F32),
                   jax.ShapeDtypeStruct((t, D_MODEL), BF16)],
        compiler_params=_params(("arbitrary",)),
    )(mixed, proj, proj, hg, pool_scale, w_out, x)


def _residual_dgrad(name, dy, w, residual):
    t = dy.shape[0]
    tm = _tile(t, 2 * ROW_TILE)
    n = w.shape[0]

    def body(dy_ref, w_ref, r_ref, o_ref, ob_ref):
        o = r_ref[...] + lax.dot_general(dy_ref[...], w_ref[...], DOT_NT, preferred_element_type=F32)
        o_ref[...] = o
        ob_ref[...] = o.astype(BF16)

    out = pl.BlockSpec((tm, n), lambda i: (i, 0))
    return pl.pallas_call(
        body, name=name, grid=(t // tm,),
        in_specs=[pl.BlockSpec((tm, dy.shape[1]), lambda i: (i, 0)), _resident(w.shape), out],
        out_specs=[out, out],
        out_shape=[jax.ShapeDtypeStruct((t, n), F32), jax.ShapeDtypeStruct((t, n), BF16)],
        compiler_params=_params(("arbitrary",)),
    )(dy, w, residual)


def _out_dgrad_gate_bwd(dh1b, w_out, mixed, proj, hg, pool_scale, after):
    t = mixed.shape[0]
    tm = _tile(t, ROW_TILE)
    n_after = len(after)

    def body(dh_ref, w_ref, mx_ref, ga_ref, gb_ref, hg_ref, ps_ref, *rest):
        dmx_ref, dp_ref, dhg_ref, gps_ref = rest[n_after:]

        @pl.when(pl.program_id(0) == 0)
        def _():
            gps_ref[...] = jnp.zeros_like(gps_ref)

        dcat = lax.dot_general(dh_ref[...], w_ref[...], DOT_NT, preferred_element_type=F32)
        ps = ps_ref[...]
        mx = mx_ref[...]
        dya = dcat[:, :POOL_WIDTH]
        silu_a, dsilu_a = _silu_and_grad(ga_ref[...])
        dpa = dya * silu_a
        gps_ref[...] += jnp.sum(dpa * mx, axis=0, keepdims=True)
        dmx_ref[...] = (dpa * ps).astype(BF16)
        dp_ref[:, :POOL_WIDTH] = jnp.zeros((tm, POOL_WIDTH), BF16)
        dp_ref[:, POOL_WIDTH:2 * POOL_WIDTH] = (dya * mx * ps * dsilu_a).astype(BF16)

        dyb = dcat[:, POOL_WIDTH:]
        silu_b, dsilu_b = _silu_and_grad(gb_ref[...])
        h_a = hg_ref[:, :SSM_WIDTH]
        sg = _sigmoid(hg_ref[:, SSM_WIDTH:])
        dsb = dyb * silu_b
        dp_ref[:, 2 * POOL_WIDTH:2 * POOL_WIDTH + SSM_WIDTH] = jnp.zeros((tm, SSM_WIDTH), BF16)
        dp_ref[:, 2 * POOL_WIDTH + SSM_WIDTH:] = (dyb * h_a * sg * dsilu_b).astype(BF16)
        dhg_ref[:, :SSM_WIDTH] = (dsb * sg).astype(BF16)
        dhg_ref[:, SSM_WIDTH:] = (dsb * h_a * sg * (1.0 - sg)).astype(BF16)

    half = pl.BlockSpec((tm, POOL_WIDTH), lambda i: (i, 0))
    full = pl.BlockSpec((tm, D_MODEL), lambda i: (i, 0))
    vec = pl.BlockSpec((1, POOL_WIDTH), lambda i: (0, 0))
    proj_width = 2 * POOL_WIDTH + 2 * SSM_WIDTH
    return pl.pallas_call(
        body, name="out_dgrad_gate_bwd", grid=(t // tm,),
        in_specs=[full, _resident(w_out.shape), half,
                  pl.BlockSpec((tm, POOL_WIDTH), lambda i: (i, 1)),
                  pl.BlockSpec((tm, SSM_WIDTH), lambda i: (i, 3)),
                  full, vec] + [pl.BlockSpec(tok.shape, lambda i: (0, 0)) for tok in after],
        out_specs=[half, pl.BlockSpec((tm, proj_width), lambda i: (i, 0)), full, vec],
        out_shape=[jax.ShapeDtypeStruct((t, POOL_WIDTH), BF16), jax.ShapeDtypeStruct((t, proj_width), BF16),
                   jax.ShapeDtypeStruct((t, 2 * SSM_WIDTH), BF16),
                   jax.ShapeDtypeStruct((1, POOL_WIDTH), F32)],
        compiler_params=_params(("arbitrary",)),
    )(dh1b, w_out, mixed, proj, proj, hg, pool_scale, *after)


def _ple_final(h1, h1b, p, w_gate, w_ple, target, gain):
    t = h1.shape[0]
    tm = _tile(t, 256)

    def body(h1_ref, h1b_ref, p_ref, wg_ref, wp_ref, tg_ref, g_ref, de_ref, dq_ref, dh2_ref, gg_ref, loss_ref):
        @pl.when(pl.program_id(0) == 0)
        def _():
            gg_ref[...] = jnp.zeros_like(gg_ref)
            loss_ref[...] = jnp.zeros_like(loss_ref)

        ev = jnp.dot(p_ref[...].astype(BF16), wp_ref[...], preferred_element_type=F32)
        sg = _sigmoid(jnp.dot(h1b_ref[...], wg_ref[...], preferred_element_type=F32))
        h2 = h1_ref[...] + ev * sg
        r = lax.rsqrt(jnp.mean(h2 * h2, axis=-1, keepdims=True) + EPS)
        n = h2 * r
        gain_v = g_ref[...]
        diff = n * gain_v - tg_ref[...]
        row_loss = jnp.sum(diff * diff, axis=-1, keepdims=True)
        loss_ref[...] += (0.5 / D_MODEL) * jnp.sum(row_loss, axis=0, keepdims=True)
        dout = diff * (1.0 / D_MODEL)
        gg_ref[...] += jnp.sum(dout * n, axis=0, keepdims=True)
        dn = dout * gain_v
        dh2 = r * (dn - n * jnp.mean(dn * n, axis=-1, keepdims=True))
        dh2_ref[...] = dh2
        de_ref[...] = (dh2 * sg).astype(BF16)
        dq_ref[...] = (dh2 * ev * sg * (1.0 - sg)).astype(BF16)

    row = pl.BlockSpec((tm, D_MODEL), lambda i: (i, 0))
    vec = pl.BlockSpec((1, D_MODEL), lambda i: (0, 0))
    return pl.pallas_call(
        body, name="ple_final", grid=(t // tm,),
        in_specs=[row, row, pl.BlockSpec((tm, PLE_DIM), lambda i: (i, 0)), _resident((D_MODEL, D_MODEL)),
                  _resident((PLE_DIM, D_MODEL)), row, vec],
        out_specs=[row, row, row, vec, pl.BlockSpec((1, 1), lambda i: (0, 0))],
        out_shape=[jax.ShapeDtypeStruct((t, D_MODEL), BF16), jax.ShapeDtypeStruct((t, D_MODEL), BF16),
                   jax.ShapeDtypeStruct((t, D_MODEL), F32), jax.ShapeDtypeStruct((1, D_MODEL), F32),
                   jax.ShapeDtypeStruct((1, 1), F32)],
        compiler_params=_params(("arbitrary",)),
    )(h1, h1b, p, w_gate, w_ple, target, gain)


def _zoh(a_re, a_im, log_dt, b_re_t, b_im_t):
    lam_re = jnp.minimum(a_re, A_RE_MAX)
    lam_im = a_im
    dt = jnp.exp(log_dt)
    mag = jnp.exp(lam_re * dt)
    ang = lam_im * dt
    ab_re = mag * jnp.cos(ang)
    ab_im = mag * jnp.sin(ang)
    den = lam_re * lam_re + lam_im * lam_im
    n_re = ab_re - 1.0
    n_im = ab_im
    q_re = (n_re * lam_re + n_im * lam_im) / den
    q_im = (n_im * lam_re - n_re * lam_im) / den
    bb_re = q_re[:, None, :] * b_re_t - q_im[:, None, :] * b_im_t
    bb_im = q_re[:, None, :] * b_im_t + q_im[:, None, :] * b_re_t
    return ab_re, ab_im, bb_re, bb_im


def _ssm_params(a_re, a_im, log_dt, b_re_t, b_im_t):
    def body(are_ref, aim_ref, dt_ref, bre_ref, bim_ref, abre_ref, abim_ref, bbre_ref, bbim_ref):
        ab_re, ab_im, bb_re, bb_im = _zoh(are_ref[...], aim_ref[...], dt_ref[...], bre_ref[...], bim_ref[...])
        abre_ref[...] = ab_re
        abim_ref[...] = ab_im
        bbre_ref[...] = bb_re
        bbim_ref[...] = bb_im

    return pl.pallas_call(
        body, name="ssm_params",
        out_shape=[jax.ShapeDtypeStruct(a_re.shape, F32), jax.ShapeDtypeStruct(a_re.shape, F32),
                   jax.ShapeDtypeStruct(b_re_t.shape, F32), jax.ShapeDtypeStruct(b_re_t.shape, F32)],
        compiler_params=_params(None),
    )(a_re, a_im, log_dt, b_re_t, b_im_t)


def _ssm_params_bwd(a_re, a_im, log_dt, b_re_t, b_im_t, g_ab_re, g_ab_im, g_bb_re, g_bb_im):
    def body(are_ref, aim_ref, dt_ref, bre_ref, bim_ref, gar_ref, gai_ref, gbr_ref, gbi_ref,
             o_are, o_aim, o_dt, o_bre, o_bim):
        _, vjp = jax.vjp(_zoh, are_ref[...], aim_ref[...], dt_ref[...], bre_ref[...], bim_ref[...])
        d_are, d_aim, d_dt, d_bre, d_bim = vjp((gar_ref[...], gai_ref[...], gbr_ref[...], gbi_ref[...]))
        o_are[...] = d_are
        o_aim[...] = d_aim
        o_dt[...] = d_dt
        o_bre[...] = d_bre
        o_bim[...] = d_bim

    ins = (a_re, a_im, log_dt, b_re_t, b_im_t)
    return pl.pallas_call(
        body, name="ssm_params_bwd",
        out_shape=[jax.ShapeDtypeStruct(v.shape, F32) for v in ins],
        compiler_params=_params(None),
    )(*ins, g_ab_re, g_ab_im, g_bb_re, g_bb_im)


CHUNK_TILES = CHUNK_STATE // LANES
CH_PER_TILE = CHUNK_IN // CHUNK_TILES
PAIR = 2 * LANES
SSM_ROWS = 256
SCAN_STEPS = 8
U_COLUMN_BLOCK = 2 * POOL_WIDTH // SSM_WIDTH


def _own_half():
    r = lax.broadcasted_iota(jnp.int32, (CHUNK_IN, LANES), 0) // SSM_GROUP % 2
    c = lax.broadcasted_iota(jnp.int32, (CHUNK_IN, LANES), 1) // SSM_STATE
    return (r == c)[None]


def _compact_weight(w):
    tiled = jnp.tile(w.reshape(SSM_CHUNKS, CHUNK_IN, SSM_STATE), (1, 1, 2))
    return jnp.where(_own_half(), tiled, 0.0)


def _compact_pair(w_a, w_b):
    return jnp.concatenate([_compact_weight(w_a), _compact_weight(w_b)], axis=-1).astype(BF16)


def _expand_grad(g):
    kept = jnp.where(_own_half(), g, 0.0)
    return kept.reshape(SSM_CHUNKS, CHUNK_IN, 2, SSM_STATE).sum(axis=2).reshape(N_SSM_GROUPS, SSM_GROUP, SSM_STATE)


TILES_PER_BLOCK = LANES // CH_PER_TILE
IN_BLOCKS = CHUNK_IN // LANES


def _tile_masks():
    j = lax.broadcasted_iota(jnp.int32, (CHUNK_TILES, LANES), 0) % TILES_PER_BLOCK
    lane = lax.broadcasted_iota(jnp.int32, (CHUNK_TILES, LANES), 1) // CH_PER_TILE
    return (j == lane).astype(F32)


def _tile_rows(ref, j, tt):
    return ref.at[j // TILES_PER_BLOCK, pl.ds(j, tt, stride=CHUNK_TILES), :]


def _spread(ref, v, masks):
    tt = v.shape[0]
    for j in range(CHUNK_TILES):
        block = LANES * (j // TILES_PER_BLOCK)
        _tile_rows(ref, j, tt)[...] = v[:, block:block + LANES] * masks[j:j + 1, :]
    return jnp.concatenate([ref[b] for b in range(IN_BLOCKS)], axis=1).astype(BF16)


def _gather(ref, full, masks):
    tt = full.shape[0] // CHUNK_TILES
    for b in range(IN_BLOCKS):
        ref[b] = full[:, b * LANES:(b + 1) * LANES]
    out = []
    for b in range(IN_BLOCKS):
        acc = None
        for j in range(b * TILES_PER_BLOCK, (b + 1) * TILES_PER_BLOCK):
            part = _tile_rows(ref, j, tt)[...] * masks[j:j + 1, :]
            acc = part if acc is None else acc + part
        out.append(acc)
    return jnp.concatenate(out, axis=1)


def _resident(shape):
    return pl.BlockSpec(shape, lambda i: (0,) * len(shape), pipeline_mode=pl.Buffered(1))


def _halves(ref, k, rows=slice(None)):
    return ref[k, rows, :LANES], ref[k, rows, LANES:]


def _ssm_fwd(proj, w2, c2, a2, d_skip):
    t = proj.shape[0]
    tt = _tile(t, SSM_ROWS)
    rows = tt * CHUNK_TILES

    def body(u_ref, w_ref, c_ref, a_ref, d_ref, y_ref, gel_ref, s_ref, carry, spread_ref, full_ref):
        @pl.when(pl.program_id(0) == 0)
        def _():
            carry[...] = jnp.zeros_like(carry)
            spread_ref[...] = jnp.zeros_like(spread_ref)

        mask = _tile_masks()
        u = u_ref[...]
        for k in range(SSM_CHUNKS):
            uk = _spread(spread_ref, u[:, k * CHUNK_IN:(k + 1) * CHUNK_IN], mask)
            s_ref[k] = jnp.dot(uk, w_ref[k], preferred_element_type=F32)

        abar = [_halves(a_ref, k) for k in range(SSM_CHUNKS)]

        def steps(i, state):
            for v in range(SCAN_STEPS):
                r = pl.ds(pl.multiple_of((i * SCAN_STEPS + v) * CHUNK_TILES, CHUNK_TILES), CHUNK_TILES)
                new = []
                for k, ((a_re, a_im), (s_re, s_im)) in enumerate(zip(abar, state)):
                    b_re, b_im = _halves(s_ref, k, r)
                    s_re, s_im = a_re * s_re - a_im * s_im + b_re, a_re * s_im + a_im * s_re + b_im
                    s_ref[k, r, :LANES] = s_re
                    s_ref[k, r, LANES:] = s_im
                    new.append((s_re, s_im))
                state = tuple(new)
            return state

        state = lax.fori_loop(0, tt // SCAN_STEPS, steps, tuple(_halves(carry, k) for k in range(SSM_CHUNKS)))
        for k, (s_re, s_im) in enumerate(state):
            carry[k, :, :LANES] = s_re
            carry[k, :, LANES:] = s_im

        for k in range(SSM_CHUNKS):
            cols = slice(k * CHUNK_IN, (k + 1) * CHUNK_IN)
            full = lax.dot_general(s_ref[k].astype(BF16), c_ref[k], DOT_NT, preferred_element_type=F32)
            y = _gather(full_ref, full, mask) + d_ref[:, cols] * u[:, cols]
            y_ref[:, cols] = y
            gel_ref[:, cols] = (0.5 * y * (1.0 + jnp.tanh(GELU_C * (y + GELU_A * y * y * y)))).astype(BF16)

    weight = _resident((SSM_CHUNKS, CHUNK_IN, PAIR))
    tokens = pl.BlockSpec((tt, SSM_WIDTH), lambda i: (i, 0))
    return pl.pallas_call(
        body, name="ssm_fwd", grid=(t // tt,),
        in_specs=[pl.BlockSpec((tt, SSM_WIDTH), lambda i: (i, U_COLUMN_BLOCK)), weight, weight,
                  _resident((SSM_CHUNKS, CHUNK_TILES, PAIR)), _resident((1, SSM_WIDTH))],
        out_specs=[tokens, tokens, pl.BlockSpec((SSM_CHUNKS, rows, PAIR), lambda i: (0, i, 0))],
        out_shape=[jax.ShapeDtypeStruct((t, SSM_WIDTH), F32), jax.ShapeDtypeStruct((t, SSM_WIDTH), BF16),
                   jax.ShapeDtypeStruct((SSM_CHUNKS, t * CHUNK_TILES, PAIR), F32)],
        scratch_shapes=[pltpu.VMEM((SSM_CHUNKS, CHUNK_TILES, PAIR), F32), pltpu.VMEM((IN_BLOCKS, rows, LANES), F32),
                        pltpu.VMEM((IN_BLOCKS, rows, LANES), F32)],
        compiler_params=_params(("arbitrary",)),
    )(proj, w2, c2, a2, d_skip)


def _ssm_bwd(dy, proj, s, w2, c2, a2, d_skip, dproj):
    t = dy.shape[0]
    tt = _tile(t, SSM_ROWS)
    rows = tt * CHUNK_TILES
    n_chunks = t // tt

    def body(dy_ref, u_ref, s_ref, w_ref, c_ref, a_ref, d_ref, _, du_ref, gc_ref, gw_ref, ga_ref, gd_ref, z_ref, carry,
             spread_ref, full_ref):
        @pl.when(pl.program_id(0) == 0)
        def _():
            for r in (carry, gc_ref, gw_ref, ga_ref, gd_ref, spread_ref):
                r[...] = jnp.zeros_like(r)

        mask = _tile_masks()
        dy_v = dy_ref[...]
        u = u_ref[...]
        gd_ref[...] += jnp.sum(dy_v * u, axis=0, keepdims=True)
        for k in range(SSM_CHUNKS):
            dk = _spread(spread_ref, dy_v[:, k * CHUNK_IN:(k + 1) * CHUNK_IN], mask)
            z_ref[k] = jnp.dot(dk, c_ref[k], preferred_element_type=F32)
            gc_ref[k] += lax.dot_general(dk, s_ref[k].astype(BF16), DOT_TN, preferred_element_type=F32)

        abar = [_halves(a_ref, k) for k in range(SSM_CHUNKS)]

        def steps(i, state):
            zs, gs = state
            for v in range(SCAN_STEPS):
                tok = tt - 1 - (i * SCAN_STEPS + v)
                r = pl.ds(pl.multiple_of(tok * CHUNK_TILES, CHUNK_TILES), CHUNK_TILES)
                new_z, new_g = [], []
                for k, ((a_re, a_im), (z_re, z_im), (g_re, g_im)) in enumerate(zip(abar, zs, gs)):
                    s_re, s_im = _halves(s_ref, k, r)
                    g_re = g_re + z_re * s_re + z_im * s_im
                    g_im = g_im + z_im * s_re - z_re * s_im
                    d_re, d_im = _halves(z_ref, k, r)
                    z_re, z_im = d_re + a_re * z_re + a_im * z_im, d_im + a_re * z_im - a_im * z_re
                    z_ref[k, r, :LANES] = z_re
                    z_ref[k, r, LANES:] = z_im
                    new_z.append((z_re, z_im))
                    new_g.append((g_re, g_im))
                zs, gs = tuple(new_z), tuple(new_g)
            return zs, gs

        zs, gs = lax.fori_loop(0, tt // SCAN_STEPS, steps,
                               (tuple(_halves(carry, k) for k in range(SSM_CHUNKS)),
                                tuple(_halves(ga_ref, k) for k in range(SSM_CHUNKS))))
        for k in range(SSM_CHUNKS):
            carry[k, :, :LANES], carry[k, :, LANES:] = zs[k]
            ga_ref[k, :, :LANES], ga_ref[k, :, LANES:] = gs[k]

        for k in range(SSM_CHUNKS):
            cols = slice(k * CHUNK_IN, (k + 1) * CHUNK_IN)
            zb = z_ref[k].astype(BF16)
            full = lax.dot_general(zb, w_ref[k], DOT_NT, preferred_element_type=F32)
            du_ref[:, cols] = (_gather(full_ref, full, mask) + d_ref[:, cols] * dy_v[:, cols]).astype(BF16)
            uk = _spread(spread_ref, u[:, cols], mask)
            gw_ref[k] += lax.dot_general(uk, zb, DOT_TN, preferred_element_type=F32)

    weight = _resident((SSM_CHUNKS, CHUNK_IN, PAIR))
    tokens = pl.BlockSpec((tt, SSM_WIDTH), lambda i: (n_chunks - 1 - i, 0))
    grad = pl.BlockSpec((SSM_CHUNKS, CHUNK_IN, PAIR), lambda i: (0, 0, 0))
    return pl.pallas_call(
        body, name="ssm_bwd", grid=(n_chunks,),
        in_specs=[tokens, pl.BlockSpec((tt, SSM_WIDTH), lambda i: (n_chunks - 1 - i, U_COLUMN_BLOCK)),
                  pl.BlockSpec((SSM_CHUNKS, rows, PAIR), lambda i: (0, n_chunks - 1 - i, 0)), weight, weight,
                  _resident((SSM_CHUNKS, CHUNK_TILES, PAIR)), _resident((1, SSM_WIDTH)),
                  pl.BlockSpec(memory_space=pl.ANY)],
        out_specs=[pl.BlockSpec((tt, SSM_WIDTH), lambda i: (n_chunks - 1 - i, U_COLUMN_BLOCK)), grad, grad,
                   pl.BlockSpec((SSM_CHUNKS, CHUNK_TILES, PAIR), lambda i: (0, 0, 0)),
                   pl.BlockSpec((1, SSM_WIDTH), lambda i: (0, 0))],
        out_shape=[jax.ShapeDtypeStruct(dproj.shape, dproj.dtype), jax.ShapeDtypeStruct((SSM_CHUNKS, CHUNK_IN, PAIR), F32),
                   jax.ShapeDtypeStruct((SSM_CHUNKS, CHUNK_IN, PAIR), F32),
                   jax.ShapeDtypeStruct((SSM_CHUNKS, CHUNK_TILES, PAIR), F32), jax.ShapeDtypeStruct((1, SSM_WIDTH), F32)],
        input_output_aliases={7: 0},
        scratch_shapes=[pltpu.VMEM((SSM_CHUNKS, rows, PAIR), F32), pltpu.VMEM((SSM_CHUNKS, CHUNK_TILES, PAIR), F32),
                        pltpu.VMEM((IN_BLOCKS, rows, LANES), F32), pltpu.VMEM((IN_BLOCKS, rows, LANES), F32)],
        compiler_params=_params(("arbitrary",)),
    )(dy, proj, s, w2, c2, a2, d_skip, dproj)


def _block(ref, axis, size, index):
    idx = [slice(None)] * len(ref.shape)
    idx[axis] = pl.ds(pl.multiple_of(index * size, size), size)
    return ref.at[tuple(idx)]


def _all_gather(name, shards, axes):
    n = len(shards)
    sizes = [s.shape[a] for s, a in zip(shards, axes)]

    def body(*refs):
        ins, outs = refs[:n], refs[n:2 * n]
        send_sems, recv_sems, local_sems = refs[2 * n:]
        x, y, c = (lax.axis_index(a) for a in MESH_AXES)
        me, sibling = (x, y, c), (x, y, 1 - c)
        chips = [(1 - x, y), (x, 1 - y), (1 - x, 1 - y)]

        def rows(i, dev):
            return _block(outs[i], axes[i], sizes[i], 4 * dev[0] + 2 * dev[1] + dev[2])

        def copy(i, k, block, to, src=None):
            return pltpu.make_async_remote_copy(
                src_ref=rows(i, block) if src is None else src, dst_ref=rows(i, block),
                send_sem=send_sems.at[7 * i + k], recv_sem=recv_sems.at[7 * i + k],
                device_id=to, device_id_type=MESH)

        mine = [pltpu.make_async_copy(ins[i], rows(i, me), local_sems.at[i]) for i in range(n)]
        for cp in mine:
            cp.start()
        first = []
        for i in range(n):
            first.append(copy(i, 0, me, sibling, src=ins[i]))
            first += [copy(i, 1 + j, me, (*chip, c), src=ins[i]) for j, chip in enumerate(chips)]
        for cp in first:
            cp.start()
        passed = []
        for i in range(n):
            for j, chip in enumerate(chips):
                copy(i, 1 + j, (*chip, c), me).wait_recv()
                fwd = copy(i, 4 + j, (*chip, c), sibling)
                fwd.start()
                passed.append(fwd)
        for i in range(n):
            copy(i, 0, sibling, me).wait_recv()
            for j, chip in enumerate(chips):
                copy(i, 4 + j, (*chip, 1 - c), me).wait_recv()
        for cp in first + passed:
            cp.wait_send()
        for cp in mine:
            cp.wait()

    out_shape = []
    for s, a in zip(shards, axes):
        shape = list(s.shape)
        shape[a] *= N_DEV
        out_shape.append(jax.ShapeDtypeStruct(tuple(shape), s.dtype))
    any_spec = pl.BlockSpec(memory_space=pl.ANY)
    return pl.pallas_call(
        body, name=name, out_shape=out_shape,
        in_specs=[any_spec] * n, out_specs=[any_spec] * n,
        scratch_shapes=[pltpu.SemaphoreType.DMA((7 * n,)), pltpu.SemaphoreType.DMA((7 * n,)),
                        pltpu.SemaphoreType.DMA((n,))],
    )(*shards)


HBM_SPEC = pl.BlockSpec(memory_space=pltpu.HBM)
SEM_SPEC = pl.BlockSpec(memory_space=pltpu.SEMAPHORE)
ANY_SPEC = pl.BlockSpec(memory_space=pl.ANY)
SPLIT_PARAMS = pltpu.CompilerParams(has_side_effects=pltpu.SideEffectType.DATAFLOW_SIDE_EFFECTING)
N_PEERS = N_DEV - 1
TOKEN = jax.ShapeDtypeStruct((SUBLANES, LANES), F32)
VMEM_SPEC = pl.BlockSpec(memory_space=pltpu.VMEM)


def _in_hbm(arrays):
    return [pltpu.with_memory_space_constraint(a, pltpu.HBM) for a in arrays]


def _peer(m):
    x, y, c = (lax.axis_index(a) for a in MESH_AXES)
    px = 1 - x if m & 4 else x
    py = 1 - y if m & 2 else y
    pc = 1 - c if m & 1 else c
    return (px, py, pc), 4 * px + 2 * py + pc


def _my_index():
    x, y, c = (lax.axis_index(a) for a in MESH_AXES)
    return 4 * x + 2 * y + c


def _gather_copies(shard_refs, full_refs, axes, send_sems, recv_sems):
    copies = []
    for i, (shard, full) in enumerate(zip(shard_refs, full_refs)):
        mine = _block(full, axes[i], shard.shape[axes[i]], _my_index())
        for m in range(1, N_DEV):
            peer, _ = _peer(m)
            copies.append(pltpu.make_async_remote_copy(
                src_ref=shard, dst_ref=mine, send_sem=send_sems.at[N_PEERS * i + m - 1],
                recv_sem=recv_sems.at[N_PEERS * i + m - 1], device_id=peer, device_id_type=MESH))
    return copies


def _gather_start(name, shards, axes, after):
    n = len(shards)

    def body(*refs):
        shard_refs = refs[:n]
        send_sems, recv_sems, local_sems = refs[n + 1:n + 4]
        full_refs = refs[2 * n + 4:3 * n + 4]
        refs[3 * n + 4][...] = jnp.zeros(TOKEN.shape, TOKEN.dtype)
        for i in range(n):
            pltpu.make_async_copy(shard_refs[i], _block(full_refs[i], axes[i], shard_refs[i].shape[axes[i]], _my_index()),
                                  local_sems.at[i]).start()
        for cp in _gather_copies(shard_refs, full_refs, axes, send_sems, recv_sems):
            cp.start()

    fulls = []
    for s, a in zip(shards, axes):
        shape = list(s.shape)
        shape[a] *= N_DEV
        fulls.append(pltpu.HBM(tuple(shape), s.dtype))
    out = pl.pallas_call(
        body, name=name,
        out_shape=(pltpu.SemaphoreType.DMA((N_PEERS * n,)), pltpu.SemaphoreType.DMA((N_PEERS * n,)),
                   pltpu.SemaphoreType.DMA((n,)), *[pltpu.HBM(s.shape, s.dtype) for s in shards], *fulls, TOKEN),
        in_specs=[HBM_SPEC] * n + [ANY_SPEC],
        out_specs=(SEM_SPEC, SEM_SPEC, SEM_SPEC, *[HBM_SPEC] * (2 * n), VMEM_SPEC),
        input_output_aliases={i: 3 + i for i in range(n)},
        compiler_params=SPLIT_PARAMS,
    )(*_in_hbm(shards), after)
    return out[:-1], out[-1]


def _gather_wait(name, started, indices, axes, after):
    send_sems, recv_sems, local_sems = started[:3]
    n_all = (len(started) - 3) // 2
    shards = [started[3 + i] for i in indices]
    fulls = [started[3 + n_all + i] for i in indices]
    n = len(indices)

    def body(*refs):
        shard_refs, full_refs = refs[:n], refs[n:2 * n]
        send_sems, recv_sems, local_sems = refs[2 * n:2 * n + 3]
        for j, i in enumerate(indices):
            mine = _block(full_refs[j], axes[j], shard_refs[j].shape[axes[j]], _my_index())
            pltpu.make_async_copy(shard_refs[j], mine, local_sems.at[i]).wait()
            for m in range(1, N_DEV):
                peer, _ = _peer(m)
                cp = pltpu.make_async_remote_copy(
                    src_ref=shard_refs[j], dst_ref=mine, send_sem=send_sems.at[N_PEERS * i + m - 1],
                    recv_sem=recv_sems.at[N_PEERS * i + m - 1], device_id=peer, device_id_type=MESH)
                cp.wait_send()
                cp.wait_recv()

    out = pl.pallas_call(
        body, name=name,
        out_shape=tuple(pltpu.HBM(a.shape, a.dtype) for a in shards + fulls),
        in_specs=[HBM_SPEC] * (2 * n) + [SEM_SPEC] * 3 + [ANY_SPEC], out_specs=tuple([HBM_SPEC] * (2 * n)),
        input_output_aliases={i: i for i in range(2 * n)},
        compiler_params=SPLIT_PARAMS,
    )(*shards, *fulls, send_sems, recv_sems, local_sems, after)
    return out[n:]


def _exchange_start(name, fulls, axes):
    n = len(fulls)
    sizes = [f.shape[a] // N_DEV for f, a in zip(fulls, axes)]

    def body(*refs):
        ins = refs[:n]
        send_sems, recv_sems = refs[n:n + 2]
        lands = refs[2 * n + 2:3 * n + 2]
        refs[3 * n + 2][...] = jnp.zeros(TOKEN.shape, TOKEN.dtype)
        for i in range(n):
            for m in reversed(range(1, N_DEV)):
                peer, index = _peer(m)
                pltpu.make_async_remote_copy(
                    src_ref=_block(ins[i], axes[i], sizes[i], index), dst_ref=lands[i].at[m - 1],
                    send_sem=send_sems.at[N_PEERS * i + m - 1], recv_sem=recv_sems.at[N_PEERS * i + m - 1],
                    device_id=peer, device_id_type=MESH).start()

    lands = []
    for f, a, size in zip(fulls, axes, sizes):
        shape = list(f.shape)
        shape[a] = size
        lands.append(pltpu.HBM((N_PEERS, *shape), f.dtype))
    out = pl.pallas_call(
        body, name=name,
        out_shape=(pltpu.SemaphoreType.DMA((N_PEERS * n,)), pltpu.SemaphoreType.DMA((N_PEERS * n,)),
                   *[pltpu.HBM(f.shape, f.dtype) for f in fulls], *lands, TOKEN),
        in_specs=[HBM_SPEC] * n, out_specs=(SEM_SPEC, SEM_SPEC, *[HBM_SPEC] * (2 * n), VMEM_SPEC),
        input_output_aliases={i: 2 + i for i in range(n)},
        compiler_params=SPLIT_PARAMS,
    )(*_in_hbm(fulls))
    return out[:-1], out[-1]


def _exchange_wait(name, started, axes, after):
    send_sems, recv_sems = started[:2]
    n = (len(started) - 2) // 2
    fulls, lands = list(started[2:2 + n]), list(started[2 + n:])
    sizes = [f.shape[a] // N_DEV for f, a in zip(fulls, axes)]

    def body(*refs):
        ins, land_refs = refs[:n], refs[n:2 * n]
        send_sems, recv_sems = refs[2 * n:2 * n + 2]
        for i in range(n):
            for m in range(1, N_DEV):
                peer, index = _peer(m)
                cp = pltpu.make_async_remote_copy(
                    src_ref=_block(ins[i], axes[i], sizes[i], index), dst_ref=land_refs[i].at[m - 1],
                    send_sem=send_sems.at[N_PEERS * i + m - 1], recv_sem=recv_sems.at[N_PEERS * i + m - 1],
                    device_id=peer, device_id_type=MESH)
                cp.wait_send()
                cp.wait_recv()

    out = pl.pallas_call(
        body, name=name,
        out_shape=tuple(pltpu.HBM(a.shape, a.dtype) for a in fulls + lands),
        in_specs=[HBM_SPEC] * (2 * n) + [SEM_SPEC] * 2 + [ANY_SPEC], out_specs=tuple([HBM_SPEC] * (2 * n)),
        input_output_aliases={i: i for i in range(2 * n)},
        compiler_params=SPLIT_PARAMS,
    )(*fulls, *lands, send_sems, recv_sems, after)
    return out[:n], out[n:]


def _sum_parts(part_refs, ndim):
    g = None
    for p_ref in part_refs:
        stacked = len(p_ref.shape) > ndim
        terms = [p_ref[s] for s in range(p_ref.shape[0])] if stacked else [p_ref[...]]
        for term in terms:
            term = term.astype(F32)
            g = term if g is None else g + term
    return g


def _adamw_update(w_ref, m_ref, v_ref, g, g_ref, d_ref, nm_ref, nv_ref):
    c1 = 1.0 - ADAM_B1 ** ADAM_STEP
    c2 = 1.0 - ADAM_B2 ** ADAM_STEP
    new_m = ADAM_B1 * m_ref[...] + (1.0 - ADAM_B1) * g
    new_v = ADAM_B2 * v_ref[...] + (1.0 - ADAM_B2) * (g * g)
    g_ref[...] = g
    nm_ref[...] = new_m
    nv_ref[...] = new_v
    d_ref[...] = -ADAM_LR * ((new_m / c1) / (jnp.sqrt(new_v / c2) + ADAM_EPS) + ADAM_WD * w_ref[...])


def _adamw_small(ws, ms, vs, stacks, loss_stack):
    n = len(ws)

    def body(*refs):
        ins, outs = refs[:4 * n + 1], refs[4 * n + 1:]
        for i in range(n):
            _adamw_update(ins[i], ins[n + i], ins[2 * n + i], _sum_parts([ins[3 * n + i]], len(ins[i].shape)),
                          outs[i], outs[n + i], outs[2 * n + i], outs[3 * n + i])
        total = ins[4 * n][0]
        for dev in range(1, N_DEV):
            total = total + ins[4 * n][dev]
        outs[4 * n][...] = total

    res = pl.pallas_call(
        body, name="adamw_small",
        out_shape=[jax.ShapeDtypeStruct(w.shape, F32) for w in ws] * 4 + [jax.ShapeDtypeStruct((1, LANES), F32)],
        compiler_params=_params(None),
    )(*ws, *ms, *vs, *stacks, loss_stack)
    return res[:n], res[n:2 * n], res[2 * n:3 * n], res[3 * n:4 * n], res[4 * n]


def _adamw(name, w, m, v, parts):
    r, c = w.shape
    tr = _tile(r, 256)
    n_parts = len(parts)

    def body(*refs):
        _adamw_update(refs[0], refs[1], refs[2], _sum_parts(refs[3:3 + n_parts], 2), *refs[3 + n_parts:])

    row = pl.BlockSpec((tr, c), lambda i: (i, 0))
    in_specs = [row, row, row]
    for p in parts:
        in_specs.append(row if p.ndim == 2 else pl.BlockSpec((p.shape[0], tr, c), lambda i: (0, i, 0)))
    return pl.pallas_call(
        body, name=name, grid=(r // tr,), in_specs=in_specs, out_specs=[row] * 4,
        out_shape=[jax.ShapeDtypeStruct((r, c), F32)] * 4,
        compiler_params=_params(("arbitrary",)),
    )(w, m, v, *parts)


SMALL = ("norm_gain", "pool_scale", "a_re", "a_im", "log_dt", "b_re", "b_im", "c_re", "c_im", "d_skip", "final_gain")
LARGE = ("w_in", "w_pool", "w_glu", "w_out", "w_ple", "w_ple_gate")
LARGE_AXIS = {"w_in": 1, "w_pool": 1, "w_glu": 1, "w_out": 0, "w_ple": 1, "w_ple_gate": 0}
WEIGHTS = ("norm_gain", "w_in", "w_pool", "pool_scale", "a_re", "a_im", "log_dt", "b_re", "b_im", "c_re", "c_im",
           "d_skip", "w_glu", "w_out", "w_ple", "w_ple_gate", "final_gain")


def kernel(x, p, norm_gain, w_in, w_pool, pool_scale, a_re, a_im, log_dt, b_re, b_im, c_re, c_im, d_skip, w_glu, w_out, w_ple, w_ple_gate, final_gain, loss_target, m_norm_gain, m_w_in, m_w_pool, m_pool_scale, m_a_re, m_a_im, m_log_dt, m_b_re, m_b_im, m_c_re, m_c_im, m_d_skip, m_w_glu, m_w_out, m_w_ple, m_w_ple_gate, m_final_gain, v_norm_gain, v_w_in, v_w_pool, v_pool_scale, v_a_re, v_a_im, v_log_dt, v_b_re, v_b_im, v_c_re, v_c_im, v_d_skip, v_w_glu, v_w_out, v_w_ple, v_w_ple_gate, v_final_gain):
    weights = dict(norm_gain=norm_gain, w_in=w_in, w_pool=w_pool, pool_scale=pool_scale, a_re=a_re, a_im=a_im,
                   log_dt=log_dt, b_re=b_re, b_im=b_im, c_re=c_re, c_im=c_im, d_skip=d_skip, w_glu=w_glu,
                   w_out=w_out, w_ple=w_ple, w_ple_gate=w_ple_gate, final_gain=final_gain)
    mom_m = dict(norm_gain=m_norm_gain, w_in=m_w_in, w_pool=m_w_pool, pool_scale=m_pool_scale, a_re=m_a_re,
                 a_im=m_a_im, log_dt=m_log_dt, b_re=m_b_re, b_im=m_b_im, c_re=m_c_re, c_im=m_c_im,
                 d_skip=m_d_skip, w_glu=m_w_glu, w_out=m_w_out, w_ple=m_w_ple, w_ple_gate=m_w_ple_gate,
                 final_gain=m_final_gain)
    mom_v = dict(norm_gain=v_norm_gain, w_in=v_w_in, w_pool=v_w_pool, pool_scale=v_pool_scale, a_re=v_a_re,
                 a_im=v_a_im, log_dt=v_log_dt, b_re=v_b_re, b_im=v_b_im, c_re=v_c_re, c_im=v_c_im,
                 d_skip=v_d_skip, w_glu=v_w_glu, w_out=v_w_out, w_ple=v_w_ple, w_ple_gate=v_w_ple_gate,
                 final_gain=v_final_gain)

    t = x.shape[1]
    xs = x.reshape(t, D_MODEL)
    ps = p.reshape(t, PLE_DIM)
    target = loss_target.reshape(t, D_MODEL)
    gain1 = norm_gain.reshape(1, D_MODEL)
    gain_f = final_gain.reshape(1, D_MODEL)
    scale_p = pool_scale.reshape(1, POOL_WIDTH)
    skip = d_skip.reshape(1, SSM_WIDTH)

    shard2d = {k: weights[k][0] for k in LARGE}
    shard_bf = {k: shard2d[k].astype(BF16) for k in LARGE}
    full = {"w_in": _all_gather("w_in_all_gather", [shard_bf["w_in"]], [LARGE_AXIS["w_in"]])[0]}
    later = [k for k in LARGE if k != "w_in"]
    later_axes = [LARGE_AXIS[k] for k in later]
    gather, gather_token = _gather_start("weights_gather_start", [shard_bf[k] for k in later], later_axes,
                                         full["w_in"])

    def arrive(k, after):
        i = later.index(k)
        full[k] = _gather_wait("gather_wait_" + k, gather, [i], [later_axes[i]], after)[0]

    ar, ai = a_re[0], a_im[0]
    ldt = log_dt.reshape(N_SSM_GROUPS, 1)
    br_t = jnp.transpose(b_re[0], (0, 2, 1))
    bi_t = jnp.transpose(b_im[0], (0, 2, 1))
    ab_re, ab_im, bb_re, bb_im = _ssm_params(ar, ai, ldt, br_t, bi_t)
    tiles = (SSM_CHUNKS, CHUNK_TILES, LANES)
    abar = jnp.concatenate([ab_re.reshape(tiles), ab_im.reshape(tiles)], axis=-1)
    w_pair = _compact_pair(bb_re, bb_im)
    c_pair = _compact_pair(c_re[0], -c_im[0])

    hn, proj = _norm1_in_proj(xs, gain1, full["w_in"], gather_token)
    arrive("w_pool", proj)
    pooled, mixed = _pool_mix(proj, full["w_pool"])
    y, gel, states = _ssm_fwd(proj, w_pair, c_pair, abar, skip)
    arrive("w_glu", gel)
    hg = _mm_nn("glu_proj", gel, full["w_glu"], [F32])[0]
    arrive("w_out", hg)
    cat, h1, h1b = _gate_out_proj(mixed, proj, hg, scale_p, full["w_out"], xs)
    arrive("w_ple", h1b)
    arrive("w_ple_gate", h1b)
    de, dq, dh2, g_final_gain, loss_part = _ple_final(h1, h1b, ps, full["w_ple_gate"], full["w_ple"], target, gain_f)

    grads = {}
    grads["w_ple_gate"] = _mm_tn("ple_gate_wgrad", h1b, dq, BF16)
    grads["w_ple"] = _mm_tn("ple_wgrad", ps, de, BF16)
    sent, tokens = {}, {}

    def send(names):
        sent[names], tokens[names[0]] = _exchange_start(
            "grads_start_" + names[0], [grads[k] for k in names], [LARGE_AXIS[k] for k in names])

    send(("w_ple_gate", "w_ple"))
    dh1, dh1b = _residual_dgrad("ple_gate_dgrad", dq, full["w_ple_gate"], dh2)
    grads["w_out"] = _mm_tn("out_wgrad", cat, dh1b, BF16)
    send(("w_out",))
    dmixed, dproj, dhg, g_pool_scale = _out_dgrad_gate_bwd(
        dh1b, full["w_out"], mixed, proj, hg, scale_p, [tokens["w_ple_gate"], tokens["w_out"]])

    tk = _tile(t, 1024)
    grads["w_pool"] = _mm("pool_wgrad", [(pooled, (tk, POOL_GROUP), lambda i, j, s: (s, i),
                                          dmixed, (tk, POOL_GROUP), lambda i, j, s: (s, i))],
                          DOT_TN, (N_POOL_GROUPS, 1, t // tk),
                          [((N_POOL_GROUPS, POOL_GROUP, POOL_GROUP), BF16, (None, POOL_GROUP, POOL_GROUP),
                            lambda i, j, s: (i, 0, 0))], t // tk)[0]
    dproj = _pool_mix_bwd(dmixed, full["w_pool"], dproj)

    grads["w_glu"] = _mm_tn("glu_wgrad", gel, dhg, BF16)
    send(("w_pool", "w_glu"))

    def gelu_bwd_epilogue(acc, ex, out_refs):
        yv = ex[0][...]
        th = jnp.tanh(GELU_C * (yv + GELU_A * yv * yv * yv))
        dgelu = 0.5 * (1.0 + th) + 0.5 * yv * (1.0 - th * th) * GELU_C * (1.0 + 3.0 * GELU_A * yv * yv)
        out_refs[0][...] = acc * dgelu

    dy = _mm_nt("glu_dgrad", dhg, full["w_glu"], [F32], tk=2048, extras=[y], epilogue=gelu_bwd_epilogue,
                after=[tokens["w_pool"]])[0]
    dproj, g_c_pair, g_w_pair, g_abar, g_d_skip = _ssm_bwd(dy, proj, states, w_pair, c_pair, abar, skip, dproj)

    g_ab_re = g_abar[..., :LANES].reshape(N_SSM_GROUPS, SSM_STATE)
    g_ab_im = g_abar[..., LANES:].reshape(N_SSM_GROUPS, SSM_STATE)
    d_ar, d_ai, d_ldt, d_br_t, d_bi_t = _ssm_params_bwd(
        ar, ai, ldt, br_t, bi_t, g_ab_re, g_ab_im,
        _expand_grad(g_w_pair[..., :LANES]), _expand_grad(g_w_pair[..., LANES:]))

    small_grads = dict(
        pool_scale=g_pool_scale, a_re=d_ar, a_im=d_ai, log_dt=d_ldt.reshape(1, N_SSM_GROUPS),
        b_re=d_br_t.astype(BF16), b_im=d_bi_t.astype(BF16), c_re=_expand_grad(g_c_pair[..., :LANES]).astype(BF16),
        c_im=(-_expand_grad(g_c_pair[..., LANES:])).astype(BF16), d_skip=g_d_skip, final_gain=g_final_gain)
    early = [k for k in SMALL if k != "norm_gain"]
    early_sent, early_token = _gather_start(
        "small_grads_start", [small_grads[k][None] for k in early] + [jnp.broadcast_to(loss_part, (1, 1, LANES))],
        [0] * (len(early) + 1), d_ar)

    grads["w_in"] = _mm_tn("in_wgrad", hn, dproj, BF16, tn=dproj.shape[1], after=[early_token])
    send(("w_in",))
    grad_x, g_norm_gain = _in_dgrad_norm1_bwd(dproj, full["w_in"], xs, dh1, gain1, tokens["w_in"])
    late_sent, late_token = _gather_start("norm_gain_grad_start", [g_norm_gain[None]], [0], g_norm_gain)

    out_g, out_d, out_m, out_v = ({} for _ in range(4))
    me = 4 * lax.axis_index("x") + 2 * lax.axis_index("y") + lax.axis_index("c")
    after = late_token
    for names, started in sent.items():
        axes = [LARGE_AXIS[k] for k in names]
        partials, landed = _exchange_wait("grads_wait_" + names[0], started, axes, after)
        for k, axis, partial, land in zip(names, axes, partials, landed):
            shard_shape = shard2d[k].shape
            size = shard_shape[axis]
            own = lax.dynamic_slice_in_dim(partial, me * size, size, axis=axis)
            view = (-1, shard_shape[-1])
            rows = math.prod(shard_shape[:-1])
            res = _adamw("adamw_" + k, shard2d[k].reshape(view), mom_m[k][0].reshape(view), mom_v[k][0].reshape(view),
                         [own.reshape(view), land.reshape(N_PEERS, rows, shard_shape[-1])])
            out_g[k], out_d[k], out_m[k], out_v[k] = (r.reshape(weights[k].shape) for r in res)
            after = res[0]

    def b_view(a):
        return jnp.transpose(a[0], (0, 2, 1))

    views = dict(norm_gain=lambda a: a, pool_scale=lambda a: a, a_re=lambda a: a[0], a_im=lambda a: a[0],
                 log_dt=lambda a: a, b_re=b_view, b_im=b_view, c_re=lambda a: a[0], c_im=lambda a: a[0],
                 d_skip=lambda a: a, final_gain=lambda a: a.reshape(1, D_MODEL))
    landed = _gather_wait("small_grads_wait", early_sent, list(range(len(early) + 1)), [0] * (len(early) + 1), after)
    stack = dict(zip(early, landed))
    stack["norm_gain"] = _gather_wait("norm_gain_grad_wait", late_sent, [0], [0], after)[0]
    *small_out, loss_row = _adamw_small(
        [views[k](weights[k]) for k in SMALL], [views[k](mom_m[k]) for k in SMALL],
        [views[k](mom_v[k]) for k in SMALL], [stack[k] for k in SMALL], landed[-1])
    loss = loss_row[0, 0]
    for out, res in zip((out_g, out_d, out_m, out_v), small_out):
        for k, r in zip(SMALL, res):
            if k in ("b_re", "b_im"):
                r = jnp.transpose(r, (0, 2, 1))
            out[k] = r.reshape(weights[k].shape)

    return (loss, grad_x.reshape(x.shape), *[out_g[k] for k in WEIGHTS], *[out_d[k] for k in WEIGHTS],
            *[out_m[k] for k in WEIGHTS], *[out_v[k] for k in WEIGHTS])
```

```python
import math

import jax
import jax.numpy as jnp
from jax import lax
from jax.experimental import pallas as pl
from jax.experimental.pallas import tpu as pltpu

F32 = jnp.float32
BF16 = jnp.bfloat16
MESH = pl.DeviceIdType.MESH
MESH_AXES = ("x", "y", "c")
N_DEV = 8

D_MODEL = 2048
POOL_WIDTH = 1024
SSM_WIDTH = 1024
N_POOL_GROUPS = 4
POOL_GROUP = 256
SSM_GROUP = 16
N_SSM_GROUPS = 64
SSM_STATE = 64
SSM_FLAT = N_SSM_GROUPS * SSM_STATE
SSM_CHUNKS = 4
CHUNK_IN = SSM_WIDTH // SSM_CHUNKS
CHUNK_STATE = SSM_FLAT // SSM_CHUNKS
PLE_DIM = 256
EPS = 1e-6
A_RE_MAX = -1e-4
ADAM_LR = 0.001
ADAM_B1 = 0.9
ADAM_B2 = 0.999
ADAM_EPS = 1e-08
ADAM_WD = 0.01
ADAM_STEP = 10
GELU_C = math.sqrt(2.0 / math.pi)
GELU_A = 0.044715

SUBLANES = 8
LANES = 128
VMEM_LIMIT_BYTES = 48 * 1024 * 1024

DOT_NN = (((1,), (0,)), ((), ()))
DOT_NT = (((1,), (1,)), ((), ()))
DOT_TN = (((0,), (0,)), ((), ()))


def _tile(n, pref):
    return pref if n % pref == 0 else n


def _params(sem):
    return pltpu.CompilerParams(dimension_semantics=sem, vmem_limit_bytes=VMEM_LIMIT_BYTES)


def _sigmoid(v):
    return 1.0 / (1.0 + jnp.exp(-v))


def _silu_and_grad(v):
    s = _sigmoid(v)
    return v * s, s * (1.0 + v * (1.0 - s))


def _mm(name, pairs, dims, grid, outs, k_steps, extras=(), epilogue=None):
    n_pairs, n_ex, n_out = len(pairs), len(extras), len(outs)
    acc_shape = tuple(d for d in outs[0][2] if d is not None)
    if epilogue is None:
        def epilogue(acc, ex, out_refs):
            out_refs[0][...] = acc.astype(out_refs[0].dtype)

    def body(*refs):
        ab = refs[:2 * n_pairs]
        ex = refs[2 * n_pairs:2 * n_pairs + n_ex]
        out_refs = refs[2 * n_pairs + n_ex:2 * n_pairs + n_ex + n_out]
        acc = refs[-1]
        k = pl.program_id(2)

        @pl.when(k == 0)
        def _():
            acc[...] = jnp.zeros_like(acc)

        part = None
        for q in range(n_pairs):
            d = lax.dot_general(ab[2 * q][...].astype(BF16), ab[2 * q + 1][...].astype(BF16), dims,
                                preferred_element_type=F32)
            part = d if part is None else part + d
        acc[...] += part

        @pl.when(k == k_steps - 1)
        def _():
            epilogue(acc[...], ex, out_refs)

    in_specs, operands = [], []
    for a, a_blk, a_map, b, b_blk, b_map in pairs:
        in_specs += [pl.BlockSpec(a_blk, a_map), pl.BlockSpec(b_blk, b_map)]
        operands += [a, b]
    for e, e_blk, e_map in extras:
        in_specs.append(pl.BlockSpec(e_blk, e_map))
        operands.append(e)
    return pl.pallas_call(
        body, name=name, grid=grid, in_specs=in_specs,
        out_specs=[pl.BlockSpec(o[2], o[3]) for o in outs],
        out_shape=[jax.ShapeDtypeStruct(o[0], o[1]) for o in outs],
        scratch_shapes=[pltpu.VMEM(acc_shape, F32)],
        compiler_params=_params(("arbitrary", "arbitrary", "arbitrary")),
    )(*operands)


def _after(tokens):
    return [(tok, tok.shape, lambda i, j, s: (0, 0)) for tok in tokens]


def _mm_nn(name, a, b, out_dtypes, tm=1024, tn=1024, tk=1024, a_col0=0, extras=(), epilogue=None, after=()):
    m, n = a.shape[0], b.shape[1]
    k = b.shape[0]
    tm, tn, tk = _tile(m, tm), _tile(n, tn), _tile(k, tk)
    outs = [((m, n), dt, (tm, tn), lambda i, j, s: (i, j)) for dt in out_dtypes]
    ex = [(e, (tm, tn), lambda i, j, s: (i, j)) for e in extras] + _after(after)
    return _mm(name, [(a, (tm, tk), lambda i, j, s: (i, a_col0 + s), b, (tk, tn), lambda i, j, s: (s, j))],
               DOT_NN, (m // tm, n // tn, k // tk), outs, k // tk, ex, epilogue)


def _mm_nt(name, a, b, out_dtypes, tm=1024, tn=1024, tk=1024, extras=(), epilogue=None, after=()):
    m, kk = a.shape
    n = b.shape[0]
    tm, tn, tk = _tile(m, tm), _tile(n, tn), _tile(kk, tk)
    outs = [((m, n), dt, (tm, tn), lambda i, j, s: (i, j)) for dt in out_dtypes]
    ex = [(e, (tm, tn), lambda i, j, s: (i, j)) for e in extras] + _after(after)
    return _mm(name, [(a, (tm, tk), lambda i, j, s: (i, s), b, (tn, tk), lambda i, j, s: (j, s))],
               DOT_NT, (m // tm, n // tn, kk // tk), outs, kk // tk, ex, epilogue)


def _mm_tn(name, a, b, out_dtype, tm=512, tn=2048, tk=1024, after=()):
    m, kk = a.shape
    n = b.shape[1]
    tm, tn, tk = _tile(kk, tm), _tile(n, tn), _tile(m, tk)
    outs = [((kk, n), out_dtype, (tm, tn), lambda i, j, s: (i, j))]
    return _mm(name, [(a, (tk, tm), lambda i, j, s: (s, i), b, (tk, tn), lambda i, j, s: (s, j))],
               DOT_TN, (kk // tm, n // tn, m // tk), outs, m // tk, _after(after))[0]


ROW_TILE = 256


def _norm1_in_proj(x, gain, w_in, after):
    t = x.shape[0]
    tm = _tile(t, ROW_TILE)
    n = w_in.shape[1]

    def body(x_ref, g_ref, w_ref, _, hn_ref, proj_ref):
        xv = x_ref[...]
        r = lax.rsqrt(jnp.mean(xv * xv, axis=-1, keepdims=True) + EPS)
        hn = (xv * r * g_ref[...]).astype(BF16)
        hn_ref[...] = hn
        proj_ref[...] = jnp.dot(hn, w_ref[...], preferred_element_type=F32)

    row = pl.BlockSpec((tm, D_MODEL), lambda i: (i, 0))
    return pl.pallas_call(
        body, name="norm1_in_proj", grid=(t // tm,),
        in_specs=[row, pl.BlockSpec((1, D_MODEL), lambda i: (0, 0)), _resident(w_in.shape),
                  pl.BlockSpec(after.shape, lambda i: (0, 0))],
        out_specs=[row, pl.BlockSpec((tm, n), lambda i: (i, 0))],
        out_shape=[jax.ShapeDtypeStruct((t, D_MODEL), BF16), jax.ShapeDtypeStruct((t, n), F32)],
        compiler_params=_params(("arbitrary",)),
    )(x, gain, w_in, after)


def _in_dgrad_norm1_bwd(dproj, w_in, x, dh1, gain, after):
    t = x.shape[0]
    tm = _tile(t, ROW_TILE)

    def body(dp_ref, w_ref, x_ref, dh1_ref, g_ref, _, dx_ref, gg_ref):
        @pl.when(pl.program_id(0) == 0)
        def _():
            gg_ref[...] = jnp.zeros_like(gg_ref)

        dhn = lax.dot_general(dp_ref[...], w_ref[...], DOT_NT, preferred_element_type=F32)
        xv = x_ref[...]
        r = lax.rsqrt(jnp.mean(xv * xv, axis=-1, keepdims=True) + EPS)
        xh = xv * r
        gg_ref[...] += jnp.sum(dhn * xh, axis=0, keepdims=True)
        dxh = dhn * g_ref[...]
        dx_ref[...] = dh1_ref[...] + r * (dxh - xh * jnp.mean(dxh * xh, axis=-1, keepdims=True))

    row = pl.BlockSpec((tm, D_MODEL), lambda i: (i, 0))
    vec = pl.BlockSpec((1, D_MODEL), lambda i: (0, 0))
    return pl.pallas_call(
        body, name="in_dgrad_norm1_bwd", grid=(t // tm,),
        in_specs=[pl.BlockSpec((tm, dproj.shape[1]), lambda i: (i, 0)), _resident(w_in.shape), row, row, vec,
                  pl.BlockSpec(after.shape, lambda i: (0, 0))],
        out_specs=[row, vec],
        out_shape=[jax.ShapeDtypeStruct((t, D_MODEL), F32), jax.ShapeDtypeStruct((1, D_MODEL), F32)],
        compiler_params=_params(("arbitrary",)),
    )(dproj, w_in, x, dh1, gain, after)


def _pool_counts(t, width, group):
    row = lax.broadcasted_iota(jnp.int32, (t, width), 0)
    window = jnp.left_shift(jnp.int32(2), group)
    return row, jnp.minimum(row + 1, window).astype(F32)


def _select_window(group, s2, s4, s8, s16):
    return jnp.where(group == 0, s2, jnp.where(group == 1, s4, jnp.where(group == 2, s8, s16)))


def _pool_mix(proj, w_pool):
    t = proj.shape[0]

    def body(u_ref, w_ref, pooled_ref, mixed_ref):
        group = pl.program_id(0)
        row, count = _pool_counts(t, LANES, group)

        def down(a, j):
            return jnp.where(row >= j, pltpu.roll(a, j, 0), 0.0)

        for h in range(POOL_GROUP // LANES):
            cols = slice(h * LANES, (h + 1) * LANES)
            v = u_ref[:, cols]
            s2 = v + down(v, 1)
            s4 = s2 + down(s2, 2)
            s8 = s4 + down(s4, 4)
            s16 = s8 + down(s8, 8)
            pooled_ref[:, cols] = (_select_window(group, s2, s4, s8, s16) / count - v).astype(BF16)
        mixed_ref[...] = jnp.dot(pooled_ref[...], w_ref[...], preferred_element_type=F32)

    block = pl.BlockSpec((t, POOL_GROUP), lambda g: (0, g))
    return pl.pallas_call(
        body, name="pool_mix", grid=(N_POOL_GROUPS,),
        in_specs=[block, pl.BlockSpec((None, POOL_GROUP, POOL_GROUP), lambda g: (g, 0, 0))],
        out_specs=[block, block],
        out_shape=[jax.ShapeDtypeStruct((t, POOL_WIDTH), BF16), jax.ShapeDtypeStruct((t, POOL_WIDTH), F32)],
        compiler_params=_params(("arbitrary",)),
    )(proj, w_pool)


def _pool_mix_bwd(dmixed, w_pool, dproj):
    t = dmixed.shape[0]

    def body(dm_ref, w_ref, _, o_ref):
        group = pl.program_id(0)
        row, count = _pool_counts(t, LANES, group)

        def up(a, j):
            return jnp.where(row < t - j, pltpu.roll(a, t - j, 0), 0.0)

        dpooled = lax.dot_general(dm_ref[...], w_ref[...], DOT_NT, preferred_element_type=F32)
        for h in range(POOL_GROUP // LANES):
            cols = slice(h * LANES, (h + 1) * LANES)
            dp = dpooled[:, cols]
            r = dp / count
            s2 = r + up(r, 1)
            s4 = s2 + up(s2, 2)
            s8 = s4 + up(s4, 4)
            s16 = s8 + up(s8, 8)
            o_ref[:, cols] = (_select_window(group, s2, s4, s8, s16) - dp).astype(BF16)

    block = pl.BlockSpec((t, POOL_GROUP), lambda g: (0, g))
    return pl.pallas_call(
        body, name="pool_mix_bwd", grid=(N_POOL_GROUPS,),
        in_specs=[block, pl.BlockSpec((None, POOL_GROUP, POOL_GROUP), lambda g: (g, 0, 0)),
                  pl.BlockSpec(memory_space=pl.ANY)],
        out_specs=block,
        out_shape=jax.ShapeDtypeStruct(dproj.shape, dproj.dtype),
        input_output_aliases={2: 0},
        compiler_params=_params(("arbitrary",)),
    )(dmixed, w_pool, dproj)


def _gate_out_proj(mixed, proj, hg, pool_scale, w_out, x):
    t = mixed.shape[0]
    tm = _tile(t, ROW_TILE)

    def body(mx_ref, ga_ref, gb_ref, hg_ref, ps_ref, w_ref, x_ref, cat_ref, h1_ref, h1b_ref):
        silu_a, _ = _silu_and_grad(ga_ref[...])
        cat_ref[:, :POOL_WIDTH] = (mx_ref[...] * ps_ref[...] * silu_a).astype(BF16)
        silu_b, _ = _silu_and_grad(gb_ref[...])
        sb = hg_ref[:, :SSM_WIDTH] * _sigmoid(hg_ref[:, SSM_WIDTH:])
        cat_ref[:, POOL_WIDTH:] = (sb * silu_b).astype(BF16)
        h1 = x_ref[...] + jnp.dot(cat_ref[...], w_ref[...], preferred_element_type=F32)
        h1_ref[...] = h1
        h1b_ref[...] = h1.astype(BF16)

    row = pl.BlockSpec((tm, D_MODEL), lambda i: (i, 0))
    return pl.pallas_call(
        body, name="gate_out_proj", grid=(t // tm,),
        in_specs=[pl.BlockSpec((tm, POOL_WIDTH), lambda i: (i, 0)),
                  pl.BlockSpec((tm, POOL_WIDTH), lambda i: (i, 1)),
                  pl.BlockSpec((tm, SSM_WIDTH), lambda i: (i, 3)),
                  pl.BlockSpec((tm, 2 * SSM_WIDTH), lambda i: (i, 0)),
                  pl.BlockSpec((1, POOL_WIDTH), lambda i: (0, 0)), _resident(w_out.shape), row],
        out_specs=[row, row, row],
        out_shape=[jax.ShapeDtypeStruct((t, D_MODEL), BF16), jax.ShapeDtypeStruct((t, D_MODEL), F32),
                   jax.ShapeDtypeStruct((t, D_MODEL), BF16)],
        compiler_params=_params(("arbitrary",)),
    )(mixed, proj, proj, hg, pool_scale, w_out, x)


def _residual_dgrad(name, dy, w, residual):
    t = dy.shape[0]
    tm = _tile(t, 2 * ROW_TILE)
    n = w.shape[0]

    def body(dy_ref, w_ref, r_ref, o_ref, ob_ref):
        o = r_ref[...] + lax.dot_general(dy_ref[...], w_ref[...], DOT_NT, preferred_element_type=F32)
        o_ref[...] = o
        ob_ref[...] = o.astype(BF16)

    out = pl.BlockSpec((tm, n), lambda i: (i, 0))
    return pl.pallas_call(
        body, name=name, grid=(t // tm,),
        in_specs=[pl.BlockSpec((tm, dy.shape[1]), lambda i: (i, 0)), _resident(w.shape), out],
        out_specs=[out, out],
        out_shape=[jax.ShapeDtypeStruct((t, n), F32), jax.ShapeDtypeStruct((t, n), BF16)],
        compiler_params=_params(("arbitrary",)),
    )(dy, w, residual)


def _out_dgrad_gate_bwd(dh1b, w_out, mixed, proj, hg, pool_scale, after):
    t = mixed.shape[0]
    tm = _tile(t, ROW_TILE)
    n_after = len(after)

    def body(dh_ref, w_ref, mx_ref, ga_ref, gb_ref, hg_ref, ps_ref, *rest):
        dmx_ref, dp_ref, dhg_ref, gps_ref = rest[n_after:]

        @pl.when(pl.program_id(0) == 0)
        def _():
            gps_ref[...] = jnp.zeros_like(gps_ref)

        dcat = lax.dot_general(dh_ref[...], w_ref[...], DOT_NT, preferred_element_type=F32)
        ps = ps_ref[...]
        mx = mx_ref[...]
        dya = dcat[:, :POOL_WIDTH]
        silu_a, dsilu_a = _silu_and_grad(ga_ref[...])
        dpa = dya * silu_a
        gps_ref[...] += jnp.sum(dpa * mx, axis=0, keepdims=True)
        dmx_ref[...] = (dpa * ps).astype(BF16)
        dp_ref[:, :POOL_WIDTH] = jnp.zeros((tm, POOL_WIDTH), BF16)
        dp_ref[:, POOL_WIDTH:2 * POOL_WIDTH] = (dya * mx * ps * dsilu_a).astype(BF16)

        dyb = dcat[:, POOL_WIDTH:]
        silu_b, dsilu_b = _silu_and_grad(gb_ref[...])
        h_a = hg_ref[:, :SSM_WIDTH]
        sg = _sigmoid(hg_ref[:, SSM_WIDTH:])
        dsb = dyb * silu_b
        dp_ref[:, 2 * POOL_WIDTH:2 * POOL_WIDTH + SSM_WIDTH] = jnp.zeros((tm, SSM_WIDTH), BF16)
        dp_ref[:, 2 * POOL_WIDTH + SSM_WIDTH:] = (dyb * h_a * sg * dsilu_b).astype(BF16)
        dhg_ref[:, :SSM_WIDTH] = (dsb * sg).astype(BF16)
        dhg_ref[:, SSM_WIDTH:] = (dsb * h_a * sg * (1.0 - sg)).astype(BF16)

    half = pl.BlockSpec((tm, POOL_WIDTH), lambda i: (i, 0))
    full = pl.BlockSpec((tm, D_MODEL), lambda i: (i, 0))
    vec = pl.BlockSpec((1, POOL_WIDTH), lambda i: (0, 0))
    proj_width = 2 * POOL_WIDTH + 2 * SSM_WIDTH
    return pl.pallas_call(
        body, name="out_dgrad_gate_bwd", grid=(t // tm,),
        in_specs=[full, _resident(w_out.shape), half,
                  pl.BlockSpec((tm, POOL_WIDTH), lambda i: (i, 1)),
                  pl.BlockSpec((tm, SSM_WIDTH), lambda i: (i, 3)),
                  full, vec] + [pl.BlockSpec(tok.shape, lambda i: (0, 0)) for tok in after],
        out_specs=[half, pl.BlockSpec((tm, proj_width), lambda i: (i, 0)), full, vec],
        out_shape=[jax.ShapeDtypeStruct((t, POOL_WIDTH), BF16), jax.ShapeDtypeStruct((t, proj_width), BF16),
                   jax.ShapeDtypeStruct((t, 2 * SSM_WIDTH), BF16),
                   jax.ShapeDtypeStruct((1, POOL_WIDTH), F32)],
        compiler_params=_params(("arbitrary",)),
    )(dh1b, w_out, mixed, proj, proj, hg, pool_scale, *after)


def _ple_final(h1, h1b, p, w_gate, w_ple, target, gain):
    t = h1.shape[0]
    tm = _tile(t, 256)

    def body(h1_ref, h1b_ref, p_ref, wg_ref, wp_ref, tg_ref, g_ref, de_ref, dq_ref, dh2_ref, gg_ref, loss_ref):
        @pl.when(pl.program_id(0) == 0)
        def _():
            gg_ref[...] = jnp.zeros_like(gg_ref)
            loss_ref[...] = jnp.zeros_like(loss_ref)

        ev = jnp.dot(p_ref[...].astype(BF16), wp_ref[...], preferred_element_type=F32)
        sg = _sigmoid(jnp.dot(h1b_ref[...], wg_ref[...], preferred_element_type=F32))
        h2 = h1_ref[...] + ev * sg
        r = lax.rsqrt(jnp.mean(h2 * h2, axis=-1, keepdims=True) + EPS)
        n = h2 * r
        gain_v = g_ref[...]
        diff = n * gain_v - tg_ref[...]
        row_loss = jnp.sum(diff * diff, axis=-1, keepdims=True)
        loss_ref[...] += (0.5 / D_MODEL) * jnp.sum(row_loss, axis=0, keepdims=True)
        dout = diff * (1.0 / D_MODEL)
        gg_ref[...] += jnp.sum(dout * n, axis=0, keepdims=True)
        dn = dout * gain_v
        dh2 = r * (dn - n * jnp.mean(dn * n, axis=-1, keepdims=True))
        dh2_ref[...] = dh2
        de_ref[...] = (dh2 * sg).astype(BF16)
        dq_ref[...] = (dh2 * ev * sg * (1.0 - sg)).astype(BF16)

    row = pl.BlockSpec((tm, D_MODEL), lambda i: (i, 0))
    vec = pl.BlockSpec((1, D_MODEL), lambda i: (0, 0))
    return pl.pallas_call(
        body, name="ple_final", grid=(t // tm,),
        in_specs=[row, row, pl.BlockSpec((tm, PLE_DIM), lambda i: (i, 0)), _resident((D_MODEL, D_MODEL)),
                  _resident((PLE_DIM, D_MODEL)), row, vec],
        out_specs=[row, row, row, vec, pl.BlockSpec((1, 1), lambda i: (0, 0))],
        out_shape=[jax.ShapeDtypeStruct((t, D_MODEL), BF16), jax.ShapeDtypeStruct((t, D_MODEL), BF16),
                   jax.ShapeDtypeStruct((t, D_MODEL), F32), jax.ShapeDtypeStruct((1, D_MODEL), F32),
                   jax.ShapeDtypeStruct((1, 1), F32)],
        compiler_params=_params(("arbitrary",)),
    )(h1, h1b, p, w_gate, w_ple, target, gain)


def _zoh(a_re, a_im, log_dt, b_re_t, b_im_t):
    lam_re = jnp.minimum(a_re, A_RE_MAX)
    lam_im = a_im
    dt = jnp.exp(log_dt)
    mag = jnp.exp(lam_re * dt)
    ang = lam_im * dt
    ab_re = mag * jnp.cos(ang)
    ab_im = mag * jnp.sin(ang)
    den = lam_re * lam_re + lam_im * lam_im
    n_re = ab_re - 1.0
    n_im = ab_im
    q_re = (n_re * lam_re + n_im * lam_im) / den
    q_im = (n_im * lam_re - n_re * lam_im) / den
    bb_re = q_re[:, None, :] * b_re_t - q_im[:, None, :] * b_im_t
    bb_im = q_re[:, None, :] * b_im_t + q_im[:, None, :] * b_re_t
    return ab_re, ab_im, bb_re, bb_im


def _ssm_params(a_re, a_im, log_dt, b_re_t, b_im_t):
    def body(are_ref, aim_ref, dt_ref, bre_ref, bim_ref, abre_ref, abim_ref, bbre_ref, bbim_ref):
        ab_re, ab_im, bb_re, bb_im = _zoh(are_ref[...], aim_ref[...], dt_ref[...], bre_ref[...], bim_ref[...])
        abre_ref[...] = ab_re
        abim_ref[...] = ab_im
        bbre_ref[...] = bb_re
        bbim_ref[...] = bb_im

    return pl.pallas_call(
        body, name="ssm_params",
        out_shape=[jax.ShapeDtypeStruct(a_re.shape, F32), jax.ShapeDtypeStruct(a_re.shape, F32),
                   jax.ShapeDtypeStruct(b_re_t.shape, F32), jax.ShapeDtypeStruct(b_re_t.shape, F32)],
        compiler_params=_params(None),
    )(a_re, a_im, log_dt, b_re_t, b_im_t)


def _ssm_params_bwd(a_re, a_im, log_dt, b_re_t, b_im_t, g_ab_re, g_ab_im, g_bb_re, g_bb_im):
    def body(are_ref, aim_ref, dt_ref, bre_ref, bim_ref, gar_ref, gai_ref, gbr_ref, gbi_ref,
             o_are, o_aim, o_dt, o_bre, o_bim):
        _, vjp = jax.vjp(_zoh, are_ref[...], aim_ref[...], dt_ref[...], bre_ref[...], bim_ref[...])
        d_are, d_aim, d_dt, d_bre, d_bim = vjp((gar_ref[...], gai_ref[...], gbr_ref[...], gbi_ref[...]))
        o_are[...] = d_are
        o_aim[...] = d_aim
        o_dt[...] = d_dt
        o_bre[...] = d_bre
        o_bim[...] = d_bim

    ins = (a_re, a_im, log_dt, b_re_t, b_im_t)
    return pl.pallas_call(
        body, name="ssm_params_bwd",
        out_shape=[jax.ShapeDtypeStruct(v.shape, F32) for v in ins],
        compiler_params=_params(None),
    )(*ins, g_ab_re, g_ab_im, g_bb_re, g_bb_im)


CHUNK_TILES = CHUNK_STATE // LANES
CH_PER_TILE = CHUNK_IN // CHUNK_TILES
PAIR = 2 * LANES
SSM_ROWS = 256
SCAN_STEPS = 8
U_COLUMN_BLOCK = 2 * POOL_WIDTH // SSM_WIDTH


def _own_half():
    r = lax.broadcasted_iota(jnp.int32, (CHUNK_IN, LANES), 0) // SSM_GROUP % 2
    c = lax.broadcasted_iota(jnp.int32, (CHUNK_IN, LANES), 1) // SSM_STATE
    return (r == c)[None]


def _compact_weight(w):
    tiled = jnp.tile(w.reshape(SSM_CHUNKS, CHUNK_IN, SSM_STATE), (1, 1, 2))
    return jnp.where(_own_half(), tiled, 0.0)


def _compact_pair(w_a, w_b):
    return jnp.concatenate([_compact_weight(w_a), _compact_weight(w_b)], axis=-1).astype(BF16)


def _expand_grad(g):
    kept = jnp.where(_own_half(), g, 0.0)
    return kept.reshape(SSM_CHUNKS, CHUNK_IN, 2, SSM_STATE).sum(axis=2).reshape(N_SSM_GROUPS, SSM_GROUP, SSM_STATE)


TILES_PER_BLOCK = LANES // CH_PER_TILE
IN_BLOCKS = CHUNK_IN // LANES


def _tile_masks():
    j = lax.broadcasted_iota(jnp.int32, (CHUNK_TILES, LANES), 0) % TILES_PER_BLOCK
    lane = lax.broadcasted_iota(jnp.int32, (CHUNK_TILES, LANES), 1) // CH_PER_TILE
    return (j == lane).astype(F32)


def _tile_rows(ref, j, tt):
    return ref.at[j // TILES_PER_BLOCK, pl.ds(j, tt, stride=CHUNK_TILES), :]


def _spread(ref, v, masks):
    tt = v.shape[0]
    for j in range(CHUNK_TILES):
        block = LANES * (j // TILES_PER_BLOCK)
        _tile_rows(ref, j, tt)[...] = v[:, block:block + LANES] * masks[j:j + 1, :]
    return jnp.concatenate([ref[b] for b in range(IN_BLOCKS)], axis=1).astype(BF16)


def _gather(ref, full, masks):
    tt = full.shape[0] // CHUNK_TILES
    for b in range(IN_BLOCKS):
        ref[b] = full[:, b * LANES:(b + 1) * LANES]
    out = []
    for b in range(IN_BLOCKS):
        acc = None
        for j in range(b * TILES_PER_BLOCK, (b + 1) * TILES_PER_BLOCK):
            part = _tile_rows(ref, j, tt)[...] * masks[j:j + 1, :]
            acc = part if acc is None else acc + part
        out.append(acc)
    return jnp.concatenate(out, axis=1)


def _resident(shape):
    return pl.BlockSpec(shape, lambda i: (0,) * len(shape), pipeline_mode=pl.Buffered(1))


def _halves(ref, k, rows=slice(None)):
    return ref[k, rows, :LANES], ref[k, rows, LANES:]


def _ssm_fwd(proj, w2, c2, a2, d_skip):
    t = proj.shape[0]
    tt = _tile(t, SSM_ROWS)
    rows = tt * CHUNK_TILES

    def body(u_ref, w_ref, c_ref, a_ref, d_ref, y_ref, gel_ref, s_ref, carry, spread_ref, full_ref):
        @pl.when(pl.program_id(0) == 0)
        def _():
            carry[...] = jnp.zeros_like(carry)
            spread_ref[...] = jnp.zeros_like(spread_ref)

        mask = _tile_masks()
        u = u_ref[...]
        for k in range(SSM_CHUNKS):
            uk = _spread(spread_ref, u[:, k * CHUNK_IN:(k + 1) * CHUNK_IN], mask)
            s_ref[k] = jnp.dot(uk, w_ref[k], preferred_element_type=F32)

        abar = [_halves(a_ref, k) for k in range(SSM_CHUNKS)]

        def steps(i, state):
            for v in range(SCAN_STEPS):
                r = pl.ds(pl.multiple_of((i * SCAN_STEPS + v) * CHUNK_TILES, CHUNK_TILES), CHUNK_TILES)
                new = []
                for k, ((a_re, a_im), (s_re, s_im)) in enumerate(zip(abar, state)):
                    b_re, b_im = _halves(s_ref, k, r)
                    s_re, s_im = a_re * s_re - a_im * s_im + b_re, a_re * s_im + a_im * s_re + b_im
                    s_ref[k, r, :LANES] = s_re
                    s_ref[k, r, LANES:] = s_im
                    new.append((s_re, s_im))
                state = tuple(new)
            return state

        state = lax.fori_loop(0, tt // SCAN_STEPS, steps, tuple(_halves(carry, k) for k in range(SSM_CHUNKS)))
        for k, (s_re, s_im) in enumerate(state):
            carry[k, :, :LANES] = s_re
            carry[k, :, LANES:] = s_im

        for k in range(SSM_CHUNKS):
            cols = slice(k * CHUNK_IN, (k + 1) * CHUNK_IN)
            full = lax.dot_general(s_ref[k].astype(BF16), c_ref[k], DOT_NT, preferred_element_type=F32)
            y = _gather(full_ref, full, mask) + d_ref[:, cols] * u[:, cols]
            y_ref[:, cols] = y
            gel_ref[:, cols] = (0.5 * y * (1.0 + jnp.tanh(GELU_C * (y + GELU_A * y * y * y)))).astype(BF16)

    weight = _resident((SSM_CHUNKS, CHUNK_IN, PAIR))
    tokens = pl.BlockSpec((tt, SSM_WIDTH), lambda i: (i, 0))
    return pl.pallas_call(
        body, name="ssm_fwd", grid=(t // tt,),
        in_specs=[pl.BlockSpec((tt, SSM_WIDTH), lambda i: (i, U_COLUMN_BLOCK)), weight, weight,
                  _resident((SSM_CHUNKS, CHUNK_TILES, PAIR)), _resident((1, SSM_WIDTH))],
        out_specs=[tokens, tokens, pl.BlockSpec((SSM_CHUNKS, rows, PAIR), lambda i: (0, i, 0))],
        out_shape=[jax.ShapeDtypeStruct((t, SSM_WIDTH), F32), jax.ShapeDtypeStruct((t, SSM_WIDTH), BF16),
                   jax.ShapeDtypeStruct((SSM_CHUNKS, t * CHUNK_TILES, PAIR), F32)],
        scratch_shapes=[pltpu.VMEM((SSM_CHUNKS, CHUNK_TILES, PAIR), F32), pltpu.VMEM((IN_BLOCKS, rows, LANES), F32),
                        pltpu.VMEM((IN_BLOCKS, rows, LANES), F32)],
        compiler_params=_params(("arbitrary",)),
    )(proj, w2, c2, a2, d_skip)


def _ssm_bwd(dy, proj, s, w2, c2, a2, d_skip, dproj):
    t = dy.shape[0]
    tt = _tile(t, SSM_ROWS)
    rows = tt * CHUNK_TILES
    n_chunks = t // tt

    def body(dy_ref, u_ref, s_ref, w_ref, c_ref, a_ref, d_ref, _, du_ref, gc_ref, gw_ref, ga_ref, gd_ref, z_ref, carry,
             spread_ref, full_ref):
        @pl.when(pl.program_id(0) == 0)
        def _():
            for r in (carry, gc_ref, gw_ref, ga_ref, gd_ref, spread_ref):
                r[...] = jnp.zeros_like(r)

        mask = _tile_masks()
        dy_v = dy_ref[...]
        u = u_ref[...]
        gd_ref[...] += jnp.sum(dy_v * u, axis=0, keepdims=True)
        for k in range(SSM_CHUNKS):
            dk = _spread(spread_ref, dy_v[:, k * CHUNK_IN:(k + 1) * CHUNK_IN], mask)
            z_ref[k] = jnp.dot(dk, c_ref[k], preferred_element_type=F32)
            gc_ref[k] += lax.dot_general(dk, s_ref[k].astype(BF16), DOT_TN, preferred_element_type=F32)

        abar = [_halves(a_ref, k) for k in range(SSM_CHUNKS)]

        def steps(i, state):
            zs, gs = state
            for v in range(SCAN_STEPS):
                tok = tt - 1 - (i * SCAN_STEPS + v)
                r = pl.ds(pl.multiple_of(tok * CHUNK_TILES, CHUNK_TILES), CHUNK_TILES)
                new_z, new_g = [], []
                for k, ((a_re, a_im), (z_re, z_im), (g_re, g_im)) in enumerate(zip(abar, zs, gs)):
                    s_re, s_im = _halves(s_ref, k, r)
                    g_re = g_re + z_re * s_re + z_im * s_im
                    g_im = g_im + z_im * s_re - z_re * s_im
                    d_re, d_im = _halves(z_ref, k, r)
                    z_re, z_im = d_re + a_re * z_re + a_im * z_im, d_im + a_re * z_im - a_im * z_re
                    z_ref[k, r, :LANES] = z_re
                    z_ref[k, r, LANES:] = z_im
                    new_z.append((z_re, z_im))
                    new_g.append((g_re, g_im))
                zs, gs = tuple(new_z), tuple(new_g)
            return zs, gs

        zs, gs = lax.fori_loop(0, tt // SCAN_STEPS, steps,
                               (tuple(_halves(carry, k) for k in range(SSM_CHUNKS)),
                                tuple(_halves(ga_ref, k) for k in range(SSM_CHUNKS))))
        for k in range(SSM_CHUNKS):
            carry[k, :, :LANES], carry[k, :, LANES:] = zs[k]
            ga_ref[k, :, :LANES], ga_ref[k, :, LANES:] = gs[k]

        for k in range(SSM_CHUNKS):
            cols = slice(k * CHUNK_IN, (k + 1) * CHUNK_IN)
            zb = z_ref[k].astype(BF16)
            full = lax.dot_general(zb, w_ref[k], DOT_NT, preferred_element_type=F32)
            du_ref[:, cols] = (_gather(full_ref, full, mask) + d_ref[:, cols] * dy_v[:, cols]).astype(BF16)
            uk = _spread(spread_ref, u[:, cols], mask)
            gw_ref[k] += lax.dot_general(uk, zb, DOT_TN, preferred_element_type=F32)

    weight = _resident((SSM_CHUNKS, CHUNK_IN, PAIR))
    tokens = pl.BlockSpec((tt, SSM_WIDTH), lambda i: (n_chunks - 1 - i, 0))
    grad = pl.BlockSpec((SSM_CHUNKS, CHUNK_IN, PAIR), lambda i: (0, 0, 0))
    return pl.pallas_call(
        body, name="ssm_bwd", grid=(n_chunks,),
        in_specs=[tokens, pl.BlockSpec((tt, SSM_WIDTH), lambda i: (n_chunks - 1 - i, U_COLUMN_BLOCK)),
                  pl.BlockSpec((SSM_CHUNKS, rows, PAIR), lambda i: (0, n_chunks - 1 - i, 0)), weight, weight,
                  _resident((SSM_CHUNKS, CHUNK_TILES, PAIR)), _resident((1, SSM_WIDTH)),
                  pl.BlockSpec(memory_space=pl.ANY)],
        out_specs=[pl.BlockSpec((tt, SSM_WIDTH), lambda i: (n_chunks - 1 - i, U_COLUMN_BLOCK)), grad, grad,
                   pl.BlockSpec((SSM_CHUNKS, CHUNK_TILES, PAIR), lambda i: (0, 0, 0)),
                   pl.BlockSpec((1, SSM_WIDTH), lambda i: (0, 0))],
        out_shape=[jax.ShapeDtypeStruct(dproj.shape, dproj.dtype), jax.ShapeDtypeStruct((SSM_CHUNKS, CHUNK_IN, PAIR), F32),
                   jax.ShapeDtypeStruct((SSM_CHUNKS, CHUNK_IN, PAIR), F32),
                   jax.ShapeDtypeStruct((SSM_CHUNKS, CHUNK_TILES, PAIR), F32), jax.ShapeDtypeStruct((1, SSM_WIDTH), F32)],
        input_output_aliases={7: 0},
        scratch_shapes=[pltpu.VMEM((SSM_CHUNKS, rows, PAIR), F32), pltpu.VMEM((SSM_CHUNKS, CHUNK_TILES, PAIR), F32),
                        pltpu.VMEM((IN_BLOCKS, rows, LANES), F32), pltpu.VMEM((IN_BLOCKS, rows, LANES), F32)],
        compiler_params=_params(("arbitrary",)),
    )(dy, proj, s, w2, c2, a2, d_skip, dproj)


def _block(ref, axis, size, index):
    idx = [slice(None)] * len(ref.shape)
    idx[axis] = pl.ds(pl.multiple_of(index * size, size), size)
    return ref.at[tuple(idx)]


def _all_gather(name, shards, axes):
    n = len(shards)
    sizes = [s.shape[a] for s, a in zip(shards, axes)]

    def body(*refs):
        ins, outs = refs[:n], refs[n:2 * n]
        send_sems, recv_sems, local_sems = refs[2 * n:]
        x, y, c = (lax.axis_index(a) for a in MESH_AXES)
        me, sibling = (x, y, c), (x, y, 1 - c)
        chips = [(1 - x, y), (x, 1 - y), (1 - x, 1 - y)]

        def rows(i, dev):
            return _block(outs[i], axes[i], sizes[i], 4 * dev[0] + 2 * dev[1] + dev[2])

        def copy(i, k, block, to, src=None):
            return pltpu.make_async_remote_copy(
                src_ref=rows(i, block) if src is None else src, dst_ref=rows(i, block),
                send_sem=send_sems.at[7 * i + k], recv_sem=recv_sems.at[7 * i + k],
                device_id=to, device_id_type=MESH)

        mine = [pltpu.make_async_copy(ins[i], rows(i, me), local_sems.at[i]) for i in range(n)]
        for cp in mine:
            cp.start()
        first = []
        for i in range(n):
            first.append(copy(i, 0, me, sibling, src=ins[i]))
            first += [copy(i, 1 + j, me, (*chip, c), src=ins[i]) for j, chip in enumerate(chips)]
        for cp in first:
            cp.start()
        passed = []
        for i in range(n):
            for j, chip in enumerate(chips):
                copy(i, 1 + j, (*chip, c), me).wait_recv()
                fwd = copy(i, 4 + j, (*chip, c), sibling)
                fwd.start()
                passed.append(fwd)
        for i in range(n):
            copy(i, 0, sibling, me).wait_recv()
            for j, chip in enumerate(chips):
                copy(i, 4 + j, (*chip, 1 - c), me).wait_recv()
        for cp in first + passed:
            cp.wait_send()
        for cp in mine:
            cp.wait()

    out_shape = []
    for s, a in zip(shards, axes):
        shape = list(s.shape)
        shape[a] *= N_DEV
        out_shape.append(jax.ShapeDtypeStruct(tuple(shape), s.dtype))
    any_spec = pl.BlockSpec(memory_space=pl.ANY)
    return pl.pallas_call(
        body, name=name, out_shape=out_shape,
        in_specs=[any_spec] * n, out_specs=[any_spec] * n,
        scratch_shapes=[pltpu.SemaphoreType.DMA((7 * n,)), pltpu.SemaphoreType.DMA((7 * n,)),
                        pltpu.SemaphoreType.DMA((n,))],
    )(*shards)


HBM_SPEC = pl.BlockSpec(memory_space=pltpu.HBM)
SEM_SPEC = pl.BlockSpec(memory_space=pltpu.SEMAPHORE)
ANY_SPEC = pl.BlockSpec(memory_space=pl.ANY)
SPLIT_PARAMS = pltpu.CompilerParams(has_side_effects=pltpu.SideEffectType.DATAFLOW_SIDE_EFFECTING)
N_PEERS = N_DEV - 1
TOKEN = jax.ShapeDtypeStruct((SUBLANES, LANES), F32)
VMEM_SPEC = pl.BlockSpec(memory_space=pltpu.VMEM)


def _in_hbm(arrays):
    return [pltpu.with_memory_space_constraint(a, pltpu.HBM) for a in arrays]


def _peer(m):
    x, y, c = (lax.axis_index(a) for a in MESH_AXES)
    px = 1 - x if m & 4 else x
    py = 1 - y if m & 2 else y
    pc = 1 - c if m & 1 else c
    return (px, py, pc), 4 * px + 2 * py + pc


def _my_index():
    x, y, c = (lax.axis_index(a) for a in MESH_AXES)
    return 4 * x + 2 * y + c


def _gather_copies(shard_refs, full_refs, axes, send_sems, recv_sems):
    copies = []
    for i, (shard, full) in enumerate(zip(shard_refs, full_refs)):
        mine = _block(full, axes[i], shard.shape[axes[i]], _my_index())
        for m in range(1, N_DEV):
            peer, _ = _peer(m)
            copies.append(pltpu.make_async_remote_copy(
                src_ref=shard, dst_ref=mine, send_sem=send_sems.at[N_PEERS * i + m - 1],
                recv_sem=recv_sems.at[N_PEERS * i + m - 1], device_id=peer, device_id_type=MESH))
    return copies


def _gather_start(name, shards, axes, after):
    n = len(shards)

    def body(*refs):
        shard_refs = refs[:n]
        send_sems, recv_sems, local_sems = refs[n + 1:n + 4]
        full_refs = refs[2 * n + 4:3 * n + 4]
        refs[3 * n + 4][...] = jnp.zeros(TOKEN.shape, TOKEN.dtype)
        for i in range(n):
            pltpu.make_async_copy(shard_refs[i], _block(full_refs[i], axes[i], shard_refs[i].shape[axes[i]], _my_index()),
                                  local_sems.at[i]).start()
        for cp in _gather_copies(shard_refs, full_refs, axes, send_sems, recv_sems):
            cp.start()

    fulls = []
    for s, a in zip(shards, axes):
        shape = list(s.shape)
        shape[a] *= N_DEV
        fulls.append(pltpu.HBM(tuple(shape), s.dtype))
    out = pl.pallas_call(
        body, name=name,
        out_shape=(pltpu.SemaphoreType.DMA((N_PEERS * n,)), pltpu.SemaphoreType.DMA((N_PEERS * n,)),
                   pltpu.SemaphoreType.DMA((n,)), *[pltpu.HBM(s.shape, s.dtype) for s in shards], *fulls, TOKEN),
        in_specs=[HBM_SPEC] * n + [ANY_SPEC],
        out_specs=(SEM_SPEC, SEM_SPEC, SEM_SPEC, *[HBM_SPEC] * (2 * n), VMEM_SPEC),
        input_output_aliases={i: 3 + i for i in range(n)},
        compiler_params=SPLIT_PARAMS,
    )(*_in_hbm(shards), after)
    return out[:-1], out[-1]


def _gather_wait(name, started, indices, axes, after):
    send_sems, recv_sems, local_sems = started[:3]
    n_all = (len(started) - 3) // 2
    shards = [started[3 + i] for i in indices]
    fulls = [started[3 + n_all + i] for i in indices]
    n = len(indices)

    def body(*refs):
        shard_refs, full_refs = refs[:n], refs[n:2 * n]
        send_sems, recv_sems, local_sems = refs[2 * n:2 * n + 3]
        for j, i in enumerate(indices):
            mine = _block(full_refs[j], axes[j], shard_refs[j].shape[axes[j]], _my_index())
            pltpu.make_async_copy(shard_refs[j], mine, local_sems.at[i]).wait()
            for m in range(1, N_DEV):
                peer, _ = _peer(m)
                cp = pltpu.make_async_remote_copy(
                    src_ref=shard_refs[j], dst_ref=mine, send_sem=send_sems.at[N_PEERS * i + m - 1],
                    recv_sem=recv_sems.at[N_PEERS * i + m - 1], device_id=peer, device_id_type=MESH)
                cp.wait_send()
                cp.wait_recv()

    out = pl.pallas_call(
        body, name=name,
        out_shape=tuple(pltpu.HBM(a.shape, a.dtype) for a in shards + fulls),
        in_specs=[HBM_SPEC] * (2 * n) + [SEM_SPEC] * 3 + [ANY_SPEC], out_specs=tuple([HBM_SPEC] * (2 * n)),
        input_output_aliases={i: i for i in range(2 * n)},
        compiler_params=SPLIT_PARAMS,
    )(*shards, *fulls, send_sems, recv_sems, local_sems, after)
    return out[n:]


def _exchange_start(name, fulls, axes):
    n = len(fulls)
    sizes = [f.shape[a] // N_DEV for f, a in zip(fulls, axes)]

    def body(*refs):
        ins = refs[:n]
        send_sems, recv_sems = refs[n:n + 2]
        lands = refs[2 * n + 2:3 * n + 2]
        refs[3 * n + 2][...] = jnp.zeros(TOKEN.shape, TOKEN.dtype)
        for i in range(n):
            for m in range(1, N_DEV):
                peer, index = _peer(m)
                pltpu.make_async_remote_copy(
                    src_ref=_block(ins[i], axes[i], sizes[i], index), dst_ref=lands[i].at[m - 1],
                    send_sem=send_sems.at[N_PEERS * i + m - 1], recv_sem=recv_sems.at[N_PEERS * i + m - 1],
                    device_id=peer, device_id_type=MESH).start()

    lands = []
    for f, a, size in zip(fulls, axes, sizes):
        shape = list(f.shape)
        shape[a] = size
        lands.append(pltpu.HBM((N_PEERS, *shape), f.dtype))
    out = pl.pallas_call(
        body, name=name,
        out_shape=(pltpu.SemaphoreType.DMA((N_PEERS * n,)), pltpu.SemaphoreType.DMA((N_PEERS * n,)),
                   *[pltpu.HBM(f.shape, f.dtype) for f in fulls], *lands, TOKEN),
        in_specs=[HBM_SPEC] * n, out_specs=(SEM_SPEC, SEM_SPEC, *[HBM_SPEC] * (2 * n), VMEM_SPEC),
        input_output_aliases={i: 2 + i for i in range(n)},
        compiler_params=SPLIT_PARAMS,
    )(*_in_hbm(fulls))
    return out[:-1], out[-1]


def _exchange_wait(name, started, axes, after):
    send_sems, recv_sems = started[:2]
    n = (len(started) - 2) // 2
    fulls, lands = list(started[2:2 + n]), list(started[2 + n:])
    sizes = [f.shape[a] // N_DEV for f, a in zip(fulls, axes)]

    def body(*refs):
        ins, land_refs = refs[:n], refs[n:2 * n]
        send_sems, recv_sems = refs[2 * n:2 * n + 2]
        for i in range(n):
            for m in range(1, N_DEV):
                peer, index = _peer(m)
                cp = pltpu.make_async_remote_copy(
                    src_ref=_block(ins[i], axes[i], sizes[i], index), dst_ref=land_refs[i].at[m - 1],
                    send_sem=send_sems.at[N_PEERS * i + m - 1], recv_sem=recv_sems.at[N_PEERS * i + m - 1],
                    device_id=peer, device_id_type=MESH)
                cp.wait_send()
                cp.wait_recv()

    out = pl.pallas_call(
        body, name=name,
        out_shape=tuple(pltpu.HBM(a.shape, a.dtype) for a in fulls + lands),
        in_specs=[HBM_SPEC] * (2 * n) + [SEM_SPEC] * 2 + [ANY_SPEC], out_specs=tuple([HBM_SPEC] * (2 * n)),
        input_output_aliases={i: i for i in range(2 * n)},
        compiler_params=SPLIT_PARAMS,
    )(*fulls, *lands, send_sems, recv_sems, after)
    return out[:n], out[n:]


def _sum_parts(part_refs, ndim):
    g = None
    for p_ref in part_refs:
        stacked = len(p_ref.shape) > ndim
        terms = [p_ref[s] for s in range(p_ref.shape[0])] if stacked else [p_ref[...]]
        for term in terms:
            term = term.astype(F32)
            g = term if g is None else g + term
    return g


def _adamw_update(w_ref, m_ref, v_ref, g, g_ref, d_ref, nm_ref, nv_ref):
    c1 = 1.0 - ADAM_B1 ** ADAM_STEP
    c2 = 1.0 - ADAM_B2 ** ADAM_STEP
    new_m = ADAM_B1 * m_ref[...] + (1.0 - ADAM_B1) * g
    new_v = ADAM_B2 * v_ref[...] + (1.0 - ADAM_B2) * (g * g)
    g_ref[...] = g
    nm_ref[...] = new_m
    nv_ref[...] = new_v
    d_ref[...] = -ADAM_LR * ((new_m / c1) / (jnp.sqrt(new_v / c2) + ADAM_EPS) + ADAM_WD * w_ref[...])


def _adamw_small(ws, ms, vs, stacks, loss_stack):
    n = len(ws)

    def body(*refs):
        ins, outs = refs[:4 * n + 1], refs[4 * n + 1:]
        for i in range(n):
            _adamw_update(ins[i], ins[n + i], ins[2 * n + i], _sum_parts([ins[3 * n + i]], len(ins[i].shape)),
                          outs[i], outs[n + i], outs[2 * n + i], outs[3 * n + i])
        total = ins[4 * n][0]
        for dev in range(1, N_DEV):
            total = total + ins[4 * n][dev]
        outs[4 * n][...] = total

    res = pl.pallas_call(
        body, name="adamw_small",
        out_shape=[jax.ShapeDtypeStruct(w.shape, F32) for w in ws] * 4 + [jax.ShapeDtypeStruct((1, LANES), F32)],
        compiler_params=_params(None),
    )(*ws, *ms, *vs, *stacks, loss_stack)
    return res[:n], res[n:2 * n], res[2 * n:3 * n], res[3 * n:4 * n], res[4 * n]


def _adamw(name, w, m, v, parts):
    r, c = w.shape
    tr = _tile(r, 256)
    n_parts = len(parts)

    def body(*refs):
        _adamw_update(refs[0], refs[1], refs[2], _sum_parts(refs[3:3 + n_parts], 2), *refs[3 + n_parts:])

    row = pl.BlockSpec((tr, c), lambda i: (i, 0))
    in_specs = [row, row, row]
    for p in parts:
        in_specs.append(row if p.ndim == 2 else pl.BlockSpec((p.shape[0], tr, c), lambda i: (0, i, 0)))
    return pl.pallas_call(
        body, name=name, grid=(r // tr,), in_specs=in_specs, out_specs=[row] * 4,
        out_shape=[jax.ShapeDtypeStruct((r, c), F32)] * 4,
        compiler_params=_params(("arbitrary",)),
    )(w, m, v, *parts)


SMALL = ("norm_gain", "pool_scale", "a_re", "a_im", "log_dt", "b_re", "b_im", "c_re", "c_im", "d_skip", "final_gain")
LARGE = ("w_in", "w_pool", "w_glu", "w_out", "w_ple", "w_ple_gate")
LARGE_AXIS = {"w_in": 1, "w_pool": 1, "w_glu": 1, "w_out": 0, "w_ple": 1, "w_ple_gate": 0}
WEIGHTS = ("norm_gain", "w_in", "w_pool", "pool_scale", "a_re", "a_im", "log_dt", "b_re", "b_im", "c_re", "c_im",
           "d_skip", "w_glu", "w_out", "w_ple", "w_ple_gate", "final_gain")


def kernel(x, p, norm_gain, w_in, w_pool, pool_scale, a_re, a_im, log_dt, b_re, b_im, c_re, c_im, d_skip, w_glu, w_out, w_ple, w_ple_gate, final_gain, loss_target, m_norm_gain, m_w_in, m_w_pool, m_pool_scale, m_a_re, m_a_im, m_log_dt, m_b_re, m_b_im, m_c_re, m_c_im, m_d_skip, m_w_glu, m_w_out, m_w_ple, m_w_ple_gate, m_final_gain, v_norm_gain, v_w_in, v_w_pool, v_pool_scale, v_a_re, v_a_im, v_log_dt, v_b_re, v_b_im, v_c_re, v_c_im, v_d_skip, v_w_glu, v_w_out, v_w_ple, v_w_ple_gate, v_final_gain):
    weights = dict(norm_gain=norm_gain, w_in=w_in, w_pool=w_pool, pool_scale=pool_scale, a_re=a_re, a_im=a_im,
                   log_dt=log_dt, b_re=b_re, b_im=b_im, c_re=c_re, c_im=c_im, d_skip=d_skip, w_glu=w_glu,
                   w_out=w_out, w_ple=w_ple, w_ple_gate=w_ple_gate, final_gain=final_gain)
    mom_m = dict(norm_gain=m_norm_gain, w_in=m_w_in, w_pool=m_w_pool, pool_scale=m_pool_scale, a_re=m_a_re,
                 a_im=m_a_im, log_dt=m_log_dt, b_re=m_b_re, b_im=m_b_im, c_re=m_c_re, c_im=m_c_im,
                 d_skip=m_d_skip, w_glu=m_w_glu, w_out=m_w_out, w_ple=m_w_ple, w_ple_gate=m_w_ple_gate,
                 final_gain=m_final_gain)
    mom_v = dict(norm_gain=v_norm_gain, w_in=v_w_in, w_pool=v_w_pool, pool_scale=v_pool_scale, a_re=v_a_re,
                 a_im=v_a_im, log_dt=v_log_dt, b_re=v_b_re, b_im=v_b_im, c_re=v_c_re, c_im=v_c_im,
                 d_skip=v_d_skip, w_glu=v_w_glu, w_out=v_w_out, w_ple=v_w_ple, w_ple_gate=v_w_ple_gate,
                 final_gain=v_final_gain)

    t = x.shape[1]
    xs = x.reshape(t, D_MODEL)
    ps = p.reshape(t, PLE_DIM)
    target = loss_target.reshape(t, D_MODEL)
    gain1 = norm_gain.reshape(1, D_MODEL)
    gain_f = final_gain.reshape(1, D_MODEL)
    scale_p = pool_scale.reshape(1, POOL_WIDTH)
    skip = d_skip.reshape(1, SSM_WIDTH)

    shard2d = {k: weights[k][0] for k in LARGE}
    shard_bf = {k: shard2d[k].astype(BF16) for k in LARGE}
    full = {"w_in": _all_gather("w_in_all_gather", [shard_bf["w_in"]], [LARGE_AXIS["w_in"]])[0]}
    later = [k for k in LARGE if k != "w_in"]
    later_axes = [LARGE_AXIS[k] for k in later]
    gather, gather_token = _gather_start("weights_gather_start", [shard_bf[k] for k in later], later_axes,
                                         full["w_in"])

    def arrive(k, after):
        i = later.index(k)
        full[k] = _gather_wait("gather_wait_" + k, gather, [i], [later_axes[i]], after)[0]

    ar, ai = a_re[0], a_im[0]
    ldt = log_dt.reshape(N_SSM_GROUPS, 1)
    br_t = jnp.transpose(b_re[0], (0, 2, 1))
    bi_t = jnp.transpose(b_im[0], (0, 2, 1))
    ab_re, ab_im, bb_re, bb_im = _ssm_params(ar, ai, ldt, br_t, bi_t)
    tiles = (SSM_CHUNKS, CHUNK_TILES, LANES)
    abar = jnp.concatenate([ab_re.reshape(tiles), ab_im.reshape(tiles)], axis=-1)
    w_pair = _compact_pair(bb_re, bb_im)
    c_pair = _compact_pair(c_re[0], -c_im[0])

    hn, proj = _norm1_in_proj(xs, gain1, full["w_in"], gather_token)
    arrive("w_pool", proj)
    pooled, mixed = _pool_mix(proj, full["w_pool"])
    y, gel, states = _ssm_fwd(proj, w_pair, c_pair, abar, skip)
    arrive("w_glu", gel)
    hg = _mm_nn("glu_proj", gel, full["w_glu"], [F32])[0]
    arrive("w_out", hg)
    cat, h1, h1b = _gate_out_proj(mixed, proj, hg, scale_p, full["w_out"], xs)
    arrive("w_ple", h1b)
    arrive("w_ple_gate", h1b)
    de, dq, dh2, g_final_gain, loss_part = _ple_final(h1, h1b, ps, full["w_ple_gate"], full["w_ple"], target, gain_f)

    grads = {}
    grads["w_ple_gate"] = _mm_tn("ple_gate_wgrad", h1b, dq, BF16)
    grads["w_ple"] = _mm_tn("ple_wgrad", ps, de, BF16)
    sent, tokens = {}, {}

    def send(names):
        sent[names], tokens[names[0]] = _exchange_start(
            "grads_start_" + names[0], [grads[k] for k in names], [LARGE_AXIS[k] for k in names])

    send(("w_ple_gate", "w_ple"))
    dh1, dh1b = _residual_dgrad("ple_gate_dgrad", dq, full["w_ple_gate"], dh2)
    grads["w_out"] = _mm_tn("out_wgrad", cat, dh1b, BF16)
    send(("w_out",))
    dmixed, dproj, dhg, g_pool_scale = _out_dgrad_gate_bwd(
        dh1b, full["w_out"], mixed, proj, hg, scale_p, [tokens["w_ple_gate"], tokens["w_out"]])

    def gelu_bwd_epilogue(acc, ex, out_refs):
        yv = ex[0][...]
        th = jnp.tanh(GELU_C * (yv + GELU_A * yv * yv * yv))
        dgelu = 0.5 * (1.0 + th) + 0.5 * yv * (1.0 - th * th) * GELU_C * (1.0 + 3.0 * GELU_A * yv * yv)
        out_refs[0][...] = acc * dgelu

    dy = _mm_nt("glu_dgrad", dhg, full["w_glu"], [F32], tk=2048, extras=[y], epilogue=gelu_bwd_epilogue)[0]
    dproj, g_c_pair, g_w_pair, g_abar, g_d_skip = _ssm_bwd(dy, proj, states, w_pair, c_pair, abar, skip, dproj)

    g_ab_re = g_abar[..., :LANES].reshape(N_SSM_GROUPS, SSM_STATE)
    g_ab_im = g_abar[..., LANES:].reshape(N_SSM_GROUPS, SSM_STATE)
    d_ar, d_ai, d_ldt, d_br_t, d_bi_t = _ssm_params_bwd(
        ar, ai, ldt, br_t, bi_t, g_ab_re, g_ab_im,
        _expand_grad(g_w_pair[..., :LANES]), _expand_grad(g_w_pair[..., LANES:]))

    small_grads = dict(
        pool_scale=g_pool_scale, a_re=d_ar, a_im=d_ai, log_dt=d_ldt.reshape(1, N_SSM_GROUPS),
        b_re=d_br_t.astype(BF16), b_im=d_bi_t.astype(BF16), c_re=_expand_grad(g_c_pair[..., :LANES]).astype(BF16),
        c_im=(-_expand_grad(g_c_pair[..., LANES:])).astype(BF16), d_skip=g_d_skip, final_gain=g_final_gain)
    early = [k for k in SMALL if k != "norm_gain"]
    early_sent, early_token = _gather_start(
        "small_grads_start", [small_grads[k][None] for k in early] + [jnp.broadcast_to(loss_part, (1, 1, LANES))],
        [0] * (len(early) + 1), d_ar)

    grads["w_glu"] = _mm_tn("glu_wgrad", gel, dhg, BF16, after=[early_token])
    tk = _tile(t, 1024)
    grads["w_pool"] = _mm("pool_wgrad", [(pooled, (tk, POOL_GROUP), lambda i, j, s: (s, i),
                                          dmixed, (tk, POOL_GROUP), lambda i, j, s: (s, i))],
                          DOT_TN, (N_POOL_GROUPS, 1, t // tk),
                          [((N_POOL_GROUPS, POOL_GROUP, POOL_GROUP), BF16, (None, POOL_GROUP, POOL_GROUP),
                            lambda i, j, s: (i, 0, 0))], t // tk)[0]
    dproj = _pool_mix_bwd(dmixed, full["w_pool"], dproj)
    send(("w_pool", "w_glu"))

    grads["w_in"] = _mm_tn("in_wgrad", hn, dproj, BF16, tn=dproj.shape[1], after=[tokens["w_pool"]])
    send(("w_in",))
    grad_x, g_norm_gain = _in_dgrad_norm1_bwd(dproj, full["w_in"], xs, dh1, gain1, tokens["w_in"])
    late_sent, late_token = _gather_start("norm_gain_grad_start", [g_norm_gain[None]], [0], g_norm_gain)

    out_g, out_d, out_m, out_v = ({} for _ in range(4))
    me = 4 * lax.axis_index("x") + 2 * lax.axis_index("y") + lax.axis_index("c")
    after = late_token
    for names, started in sent.items():
        axes = [LARGE_AXIS[k] for k in names]
        partials, landed = _exchange_wait("grads_wait_" + names[0], started, axes, after)
        for k, axis, partial, land in zip(names, axes, partials, landed):
            shard_shape = shard2d[k].shape
            size = shard_shape[axis]
            own = lax.dynamic_slice_in_dim(partial, me * size, size, axis=axis)
            view = (-1, shard_shape[-1])
            rows = math.prod(shard_shape[:-1])
            res = _adamw("adamw_" + k, shard2d[k].reshape(view), mom_m[k][0].reshape(view), mom_v[k][0].reshape(view),
                         [own.reshape(view), land.reshape(N_PEERS, rows, shard_shape[-1])])
            out_g[k], out_d[k], out_m[k], out_v[k] = (r.reshape(weights[k].shape) for r in res)
            after = res[0]

    def b_view(a):
        return jnp.transpose(a[0], (0, 2, 1))

    views = dict(norm_gain=lambda a: a, pool_scale=lambda a: a, a_re=lambda a: a[0], a_im=lambda a: a[0],
                 log_dt=lambda a: a, b_re=b_view, b_im=b_view, c_re=lambda a: a[0], c_im=lambda a: a[0],
                 d_skip=lambda a: a, final_gain=lambda a: a.reshape(1, D_MODEL))
    landed = _gather_wait("small_grads_wait", early_sent, list(range(len(early) + 1)), [0] * (len(early) + 1), after)
    stack = dict(zip(early, landed))
    stack["norm_gain"] = _gather_wait("norm_gain_grad_wait", late_sent, [0], [0], after)[0]
    *small_out, loss_row = _adamw_small(
        [views[k](weights[k]) for k in SMALL], [views[k](mom_m[k]) for k in SMALL],
        [views[k](mom_v[k]) for k in SMALL], [stack[k] for k in SMALL], landed[-1])
    loss = loss_row[0, 0]
    for out, res in zip((out_g, out_d, out_m, out_v), small_out):
        for k, r in zip(SMALL, res):
            if k in ("b_re", "b_im"):
                r = jnp.transpose(r, (0, 2, 1))
            out[k] = r.reshape(weights[k].shape)

    return (loss, grad_x.reshape(x.shape), *[out_g[k] for k in WEIGHTS], *[out_d[k] for k in WEIGHTS],
            *[out_m[k] for k in WEIGHTS], *[out_v[k] for k in WEIGHTS])
```
